```python
import math
import jax, jax.numpy as jnp
from jax import lax
import numpy as np

D_MODEL = 2048
BATCH = 8
SEQ = 2048
DEPTH = 2

GRID_W = 64
CTX_LEN = 256

NA_HEADS = 8
NA_HEAD_DIM = 128
NA_WIN_ROWS = 8
NA_WIN_COLS = 16

RET_HEADS = 8
RET_KEY_DIM = 128
RET_VAL_DIM = 256
RET_CHUNK = 128

ROPE_BASE = 10000.0
ROPE_FREQS_PER_AXIS = RET_KEY_DIM // 4
NORM_EPS = 1e-6
MASK_VALUE = -1e30

W_NA = NA_HEADS * NA_HEAD_DIM
W_RET_QK = RET_HEADS * RET_KEY_DIM
W_RET_V = RET_HEADS * RET_VAL_DIM

NA_Q, NA_K, NA_V, NA_Z, RET_Q, RET_K, RET_V, RET_Z, G_NA, G_RET = range(10)
SPLIT_SIZES = (W_NA, W_NA, W_NA, W_NA, W_RET_QK, W_RET_QK, W_RET_V, W_RET_V, D_MODEL, D_MODEL)
SPLIT_OFFSETS = tuple(int(o) for o in np.cumsum((0,) + SPLIT_SIZES))
N_SPLITS = len(SPLIT_SIZES)
IN_COLS = SPLIT_OFFSETS[-1]

kernel_name = 'hybrid_na_retention_dit'


def rmsnorm(x, g):
    xf = x.astype(jnp.float32)
    y = xf * lax.rsqrt(jnp.mean(xf * xf, axis=-1, keepdims=True) + NORM_EPS)
    return (y * g.astype(jnp.float32)).astype(x.dtype)


def to_heads(t, n_heads):
    b, n, w = t.shape
    return t.reshape(b, n, n_heads, w // n_heads).transpose(0, 2, 1, 3)


def from_heads(t):
    b, h, n, d = t.shape
    return t.transpose(0, 2, 1, 3).reshape(b, n, h * d)


def in_block(t, i):
    return t[..., SPLIT_OFFSETS[i]:SPLIT_OFFSETS[i + 1]]


def axial_rope(n_tokens, dtype):
    t = jnp.arange(n_tokens)
    row = (t // GRID_W).astype(jnp.float32)
    col = (t % GRID_W).astype(jnp.float32)
    inv_freq = ROPE_BASE ** (-jnp.arange(ROPE_FREQS_PER_AXIS, dtype=jnp.float32) / ROPE_FREQS_PER_AXIS)
    ang = jnp.concatenate([row[:, None] * inv_freq, col[:, None] * inv_freq], axis=-1)
    return jnp.cos(ang).astype(dtype), jnp.sin(ang).astype(dtype)


def apply_rope(x, cos, sin):
    half = x.shape[-1] // 2
    x1, x2 = x[..., :half], x[..., half:]
    return jnp.concatenate([x1 * cos - x2 * sin, x2 * cos + x1 * sin], axis=-1)


def neighbourhood_attention(q, k, v, k_ctx, v_ctx, rpb):
    b, h, s, d = q.shape
    rows = s // GRID_W
    kh = min(NA_WIN_ROWS, rows)
    kw = NA_WIN_COLS
    r = jnp.arange(rows)
    cidx = jnp.arange(GRID_W)
    r0 = jnp.clip(r - kh // 2, 0, rows - kh)
    row_idx = r0[:, None] + jnp.arange(kh)[None, :]
    c0 = jnp.clip(cidx - kw // 2, 0, GRID_W - kw)
    col_in = (cidx[None, :] >= c0[:, None]) & (cidx[None, :] < c0[:, None] + kw)
    qg = q.reshape(b, h, rows, GRID_W, d)
    kg = jnp.take(k.reshape(b, h, rows, GRID_W, d), row_idx, axis=2)
    vg = jnp.take(v.reshape(b, h, rows, GRID_W, d), row_idx, axis=2)
    scale = d ** -0.5
    s_loc = jnp.einsum('bhrcd,bhrkwd->bhrckw', qg, kg).astype(jnp.float32) * scale
    dr = row_idx - r[:, None] + (NA_WIN_ROWS - 1)
    dc = jnp.clip(cidx[None, :] - cidx[:, None] + (NA_WIN_COLS - 1), 0, 2 * NA_WIN_COLS - 2)
    bias = rpb[:, dr[:, None, :, None], dc[None, :, None, :]].astype(jnp.float32)
    s_loc = jnp.where(col_in[None, None, None, :, None, :], s_loc + bias[None], MASK_VALUE)
    s_ctx = jnp.einsum('bhrcd,bhld->bhrcl', qg, k_ctx).astype(jnp.float32) * scale
    n_loc = kh * GRID_W
    scores = jnp.concatenate([s_loc.reshape(b, h, rows, GRID_W, n_loc), s_ctx], axis=-1)
    p = jax.nn.softmax(scores, axis=-1).astype(v.dtype)
    p_loc = p[..., :n_loc].reshape(b, h, rows, GRID_W, kh, GRID_W)
    p_ctx = p[..., n_loc:]
    out = (jnp.einsum('bhrckw,bhrkwd->bhrcd', p_loc, vg)
           + jnp.einsum('bhrcl,bhld->bhrcd', p_ctx, v_ctx))
    return out.reshape(b, h, s, d)


def context_attention(q, k, v):
    s = jnp.einsum('bhqd,bhkd->bhqk', q, k).astype(jnp.float32) * (q.shape[-1] ** -0.5)
    p = jax.nn.softmax(s, axis=-1).astype(v.dtype)
    return jnp.einsum('bhqk,bhkd->bhqd', p, v)


def retention_chunkwise(q, k, v, log_gamma, state0):
    b, h, n, dk = q.shape
    dv = v.shape[-1]
    cs = RET_CHUNK
    nc = n // cs
    pos = jnp.arange(cs, dtype=jnp.float32)
    diff = pos[:, None] - pos[None, :]
    intra = jnp.where(diff >= 0, jnp.exp(jnp.maximum(diff, 0.0) * log_gamma[:, None, None]), 0.0)
    q_dec = jnp.exp((pos + 1.0)[None, :] * log_gamma[:, None])
    k_dec = jnp.exp((cs - 1.0 - pos)[None, :] * log_gamma[:, None])
    chunk_dec = jnp.exp(cs * log_gamma)
    qc = q.reshape(b, h, nc, cs, dk)
    kc = k.reshape(b, h, nc, cs, dk)
    vc = v.reshape(b, h, nc, cs, dv)
    scores = jnp.einsum('bhnid,bhnjd->bhnij', qc, kc) * intra[None, :, None]
    inner = jnp.einsum('bhnij,bhnje->bhnie', scores, vc)

    def step(state, xs):
        q_i, k_i, v_i = xs
        cross = jnp.einsum('bhid,bhde->bhie', q_i * q_dec[None, :, :, None], state)
        state = (state * chunk_dec[None, :, None, None]
                 + jnp.einsum('bhjd,bhje->bhde', k_i * k_dec[None, :, :, None], v_i))
        return state, cross

    xs = (jnp.moveaxis(qc, 2, 0), jnp.moveaxis(kc, 2, 0), jnp.moveaxis(vc, 2, 0))
    state_final, cross = lax.scan(step, state0, xs)
    out = inner + jnp.moveaxis(cross, 0, 2)
    return out.reshape(b, h, n, dv), state_final


def bidir_retention(q, k, v, log_gamma, state_fwd, state_bwd):
    q, k, v = q.astype(jnp.float32), k.astype(jnp.float32), v.astype(jnp.float32)
    o_f, s_f = retention_chunkwise(q, k, v, log_gamma[0], state_fwd)
    o_b, s_b = retention_chunkwise(jnp.flip(q, 2), jnp.flip(k, 2), jnp.flip(v, 2), log_gamma[1], state_bwd)
    return o_f + jnp.flip(o_b, 2), s_f, s_b


def context_final_states(k, v, log_gamma):
    k, v = k.astype(jnp.float32), v.astype(jnp.float32)
    n = k.shape[2]
    pos = jnp.arange(n, dtype=jnp.float32)
    w_f = jnp.exp((n - 1.0 - pos)[None, :] * log_gamma[0][:, None])
    w_b = jnp.exp(pos[None, :] * log_gamma[1][:, None])
    s_f = jnp.einsum('bhld,bhle->bhde', k * w_f[None, :, :, None], v)
    s_b = jnp.einsum('bhld,bhle->bhde', k * w_b[None, :, :, None], v)
    return s_f, s_b


def ret_head_norm(o, dtype):
    o = o * lax.rsqrt(jnp.mean(o * o, axis=-1, keepdims=True) + NORM_EPS)
    return o.astype(dtype)


def merge_branches(o_na, o_ret, blocks, w_proj_na, w_proj_ret, w_out):
    dtype = blocks[NA_Z].dtype
    y_na = (from_heads(o_na) * jax.nn.silu(blocks[NA_Z])) @ w_proj_na
    y_ret = (from_heads(ret_head_norm(o_ret, dtype)) * jax.nn.silu(blocks[RET_Z])) @ w_proj_ret
    merged = jax.nn.sigmoid(blocks[G_NA]) * y_na + jax.nn.sigmoid(blocks[G_RET]) * y_ret
    return merged @ w_out


def hybrid_layer(x_lat, x_ctx, mod_lat, mod_ctx, norm_g, w_in, rpb, decay_logit,
                 w_proj_na, w_proj_ret, w_out, update_ctx):
    shift, scale, gate = jnp.split(mod_lat, 3, axis=-1)
    c_shift, c_scale, c_gate = jnp.split(mod_ctx, 3, axis=-1)
    h_lat = rmsnorm(x_lat, norm_g) * (1.0 + scale[:, None]) + shift[:, None]
    h_ctx = rmsnorm(x_ctx, norm_g) * (1.0 + c_scale) + c_shift
    log_gamma = jax.nn.log_sigmoid(decay_logit.astype(jnp.float32))

    u = h_lat @ w_in
    lat = {i: in_block(u, i) for i in range(N_SPLITS)}
    if update_ctx:
        uc = h_ctx @ w_in
        cb = {i: in_block(uc, i) for i in range(N_SPLITS)}
    else:
        cb = {i: h_ctx @ in_block(w_in, i) for i in (NA_K, NA_V, RET_K, RET_V)}

    k_na_ctx = to_heads(cb[NA_K], NA_HEADS)
    v_na_ctx = to_heads(cb[NA_V], NA_HEADS)
    o_na = neighbourhood_attention(to_heads(lat[NA_Q], NA_HEADS), to_heads(lat[NA_K], NA_HEADS),
                                   to_heads(lat[NA_V], NA_HEADS), k_na_ctx, v_na_ctx, rpb)

    k_scale = RET_KEY_DIM ** -0.5
    cos, sin = axial_rope(x_lat.shape[1], x_lat.dtype)
    q_ret = apply_rope(to_heads(lat[RET_Q], RET_HEADS), cos, sin)
    k_ret = apply_rope(to_heads(lat[RET_K], RET_HEADS), cos, sin) * k_scale
    v_ret = to_heads(lat[RET_V], RET_HEADS)
    k_ret_ctx = to_heads(cb[RET_K], RET_HEADS) * k_scale
    v_ret_ctx = to_heads(cb[RET_V], RET_HEADS)
    if update_ctx:
        b = x_ctx.shape[0]
        zeros = jnp.zeros((b, RET_HEADS, RET_KEY_DIM, RET_VAL_DIM), jnp.float32)
        o_ret_ctx, s_f, s_b = bidir_retention(to_heads(cb[RET_Q], RET_HEADS), k_ret_ctx, v_ret_ctx,
                                              log_gamma, zeros, zeros)
    else:
        s_f, s_b = context_final_states(k_ret_ctx, v_ret_ctx, log_gamma)
    o_ret, _, _ = bidir_retention(q_ret, k_ret, v_ret, log_gamma, s_f, s_b)

    out_lat = merge_branches(o_na, o_ret, lat, w_proj_na, w_proj_ret, w_out)
    x_lat = x_lat + gate[:, None] * out_lat
    if update_ctx:
        o_na_ctx = context_attention(to_heads(cb[NA_Q], NA_HEADS), k_na_ctx, v_na_ctx)
        out_ctx = merge_branches(o_na_ctx, o_ret_ctx, cb, w_proj_na, w_proj_ret, w_out)
        x_ctx = x_ctx + c_gate * out_ctx
    return x_lat, x_ctx


def _fwd_setup_inputs(seed: int = 0) -> dict:
    key = jax.random.key(seed)
    ks = jax.random.split(key, 14)
    f32 = jnp.float32
    base_logit = jnp.log(2.0 ** (5.0 + jnp.arange(RET_HEADS, dtype=f32)) - 1.0)
    return {
        'x': jax.random.normal(ks[0], (BATCH, SEQ, D_MODEL), f32),
        'c': jax.random.normal(ks[1], (BATCH, D_MODEL), f32),
        'ctx': jax.random.normal(ks[2], (BATCH, CTX_LEN, D_MODEL), f32),
        'c_ctx': jax.random.normal(ks[3], (D_MODEL,), f32),
        'ada_w': jax.random.normal(ks[4], (DEPTH, D_MODEL, 3 * D_MODEL), f32) * D_MODEL ** -0.5,
        'ada_b': jax.random.normal(ks[5], (DEPTH, 3 * D_MODEL), f32) * 0.01,
        'norm_g': 1.0 + 0.1 * jax.random.normal(ks[6], (DEPTH, D_MODEL), f32),
        'w_in': jax.random.normal(ks[7], (DEPTH, D_MODEL, IN_COLS), f32) * D_MODEL ** -0.5,
        'na_rpb': 0.1 * jax.random.normal(ks[8], (DEPTH, NA_HEADS, 2 * NA_WIN_ROWS - 1, 2 * NA_WIN_COLS - 1), f32),
        'ret_decay_logit': base_logit[None, None, :] + 0.1 * jax.random.normal(ks[9], (DEPTH, 2, RET_HEADS), f32),
        'w_proj_na': jax.random.normal(ks[10], (DEPTH, W_NA, D_MODEL), f32) * W_NA ** -0.5,
        'w_proj_ret': jax.random.normal(ks[11], (DEPTH, W_RET_V, D_MODEL), f32) * W_RET_V ** -0.5,
        'w_out': jax.random.normal(ks[12], (DEPTH, D_MODEL, D_MODEL), f32) * D_MODEL ** -0.5,
        'final_g': 1.0 + 0.1 * jax.random.normal(ks[13], (D_MODEL,), f32),
    }


def _fwd_reference(x, c, ctx, c_ctx, ada_w, ada_b, norm_g, w_in, na_rpb, ret_decay_logit,
              w_proj_na, w_proj_ret, w_out, final_g):
    c_silu = jax.nn.silu(c)
    cc_silu = jax.nn.silu(c_ctx)
    x_lat, x_ctx = x, ctx
    for l in range(DEPTH):
        mod_lat = c_silu @ ada_w[l] + ada_b[l]
        mod_ctx = cc_silu @ ada_w[l] + ada_b[l]
        x_lat, x_ctx = hybrid_layer(x_lat, x_ctx, mod_lat, mod_ctx, norm_g[l], w_in[l], na_rpb[l],
                                    ret_decay_logit[l], w_proj_na[l], w_proj_ret[l], w_out[l],
                                    update_ctx=(l < DEPTH - 1))
    return rmsnorm(x_lat, final_g)


import jax as _jax
import jax.numpy as _jnp

TWIN_FORMAT = 'train_step'
FWD_PARAMS = ['x', 'c', 'ctx', 'c_ctx', 'ada_w', 'ada_b', 'norm_g', 'w_in', 'na_rpb', 'ret_decay_logit', 'w_proj_na', 'w_proj_ret', 'w_out', 'final_g']
TWIN_WEIGHTS = ['c_ctx', 'ada_w', 'ada_b', 'norm_g', 'w_in', 'na_rpb', 'ret_decay_logit', 'w_proj_na', 'w_proj_ret', 'w_out', 'final_g']
TWIN_DIFF_INPUT = 'x'
TWIN_INPUTS = ['x', 'c', 'ctx', 'c_ctx', 'ada_w', 'ada_b', 'norm_g', 'w_in', 'na_rpb', 'ret_decay_logit', 'w_proj_na', 'w_proj_ret', 'w_out', 'final_g', 'loss_target', 'm_c_ctx', 'm_ada_w', 'm_ada_b', 'm_norm_g', 'm_w_in', 'm_na_rpb', 'm_ret_decay_logit', 'm_w_proj_na', 'm_w_proj_ret', 'm_w_out', 'm_final_g', 'v_c_ctx', 'v_ada_w', 'v_ada_b', 'v_norm_g', 'v_w_in', 'v_na_rpb', 'v_ret_decay_logit', 'v_w_proj_na', 'v_w_proj_ret', 'v_w_out', 'v_final_g']
TWIN_OUTPUTS = ['loss', 'grad_x', 'grad_c_ctx', 'grad_ada_w', 'grad_ada_b', 'grad_norm_g', 'grad_w_in', 'grad_na_rpb', 'grad_ret_decay_logit', 'grad_w_proj_na', 'grad_w_proj_ret', 'grad_w_out', 'grad_final_g', 'delta_c_ctx', 'delta_ada_w', 'delta_ada_b', 'delta_norm_g', 'delta_w_in', 'delta_na_rpb', 'delta_ret_decay_logit', 'delta_w_proj_na', 'delta_w_proj_ret', 'delta_w_out', 'delta_final_g', 'new_m_c_ctx', 'new_m_ada_w', 'new_m_ada_b', 'new_m_norm_g', 'new_m_w_in', 'new_m_na_rpb', 'new_m_ret_decay_logit', 'new_m_w_proj_na', 'new_m_w_proj_ret', 'new_m_w_out', 'new_m_final_g', 'new_v_c_ctx', 'new_v_ada_w', 'new_v_ada_b', 'new_v_norm_g', 'new_v_w_in', 'new_v_na_rpb', 'new_v_ret_decay_logit', 'new_v_w_proj_na', 'new_v_w_proj_ret', 'new_v_w_out', 'new_v_final_g']
TWIN_LEAF_KINDS = {'loss': 'loss', 'grad_x': 'grad_x', 'grad_c_ctx': 'grad_w', 'grad_ada_w': 'grad_w', 'grad_ada_b': 'grad_w', 'grad_norm_g': 'grad_w', 'grad_w_in': 'grad_w', 'grad_na_rpb': 'grad_w', 'grad_ret_decay_logit': 'grad_w', 'grad_w_proj_na': 'grad_w', 'grad_w_proj_ret': 'grad_w', 'grad_w_out': 'grad_w', 'grad_final_g': 'grad_w', 'delta_c_ctx': 'delta_w', 'delta_ada_w': 'delta_w', 'delta_ada_b': 'delta_w', 'delta_norm_g': 'delta_w', 'delta_w_in': 'delta_w', 'delta_na_rpb': 'delta_w', 'delta_ret_decay_logit': 'delta_w', 'delta_w_proj_na': 'delta_w', 'delta_w_proj_ret': 'delta_w', 'delta_w_out': 'delta_w', 'delta_final_g': 'delta_w', 'new_m_c_ctx': 'new_m', 'new_m_ada_w': 'new_m', 'new_m_ada_b': 'new_m', 'new_m_norm_g': 'new_m', 'new_m_w_in': 'new_m', 'new_m_na_rpb': 'new_m', 'new_m_ret_decay_logit': 'new_m', 'new_m_w_proj_na': 'new_m', 'new_m_w_proj_ret': 'new_m', 'new_m_w_out': 'new_m', 'new_m_final_g': 'new_m', 'new_v_c_ctx': 'new_v', 'new_v_ada_w': 'new_v', 'new_v_ada_b': 'new_v', 'new_v_norm_g': 'new_v', 'new_v_w_in': 'new_v', 'new_v_na_rpb': 'new_v', 'new_v_ret_decay_logit': 'new_v', 'new_v_w_proj_na': 'new_v', 'new_v_w_proj_ret': 'new_v', 'new_v_w_out': 'new_v', 'new_v_final_g': 'new_v'}


def _forward(args):
    return _fwd_reference(*[args[k] for k in FWD_PARAMS])


def _output_shape():
    out = _jax.eval_shape(lambda: _forward(_fwd_setup_inputs(0)))
    return out.shape, out.dtype

N_MICROBATCH = 1
ADAM_LR = 0.001
ADAM_B1 = 0.9
ADAM_B2 = 0.999
ADAM_EPS = 1e-08
ADAM_WD = 0.01
ADAM_STEP = 10
PER_EXAMPLE_BATCH_AXIS = {'x': 0, 'c': 0, 'ctx': 0, 'loss_target': 0}
SHARED_INPUTS = []
_WEIGHT_DTYPES = {'c_ctx': _jnp.float32, 'ada_w': _jnp.float32, 'ada_b': _jnp.float32, 'norm_g': _jnp.float32, 'w_in': _jnp.float32, 'na_rpb': _jnp.float32, 'ret_decay_logit': _jnp.float32, 'w_proj_na': _jnp.float32, 'w_proj_ret': _jnp.float32, 'w_out': _jnp.float32, 'final_g': _jnp.float32}
MOMENT_SCALE = {'c_ctx': 3.205995e-02, 'ada_w': 4.174787e-02, 'ada_b': 7.590042e-02, 'norm_g': 3.636762e-02, 'w_in': 2.082820e-02, 'na_rpb': 1.529985e-03, 'ret_decay_logit': 6.606826e-02, 'w_proj_na': 1.059721e-02, 'w_proj_ret': 1.871533e-02, 'w_out': 2.169768e-02, 'final_g': 8.088962e+00}


def _to_microbatches(a, axis):
    t = _jnp.moveaxis(a, axis, 0)
    t = t.reshape((N_MICROBATCH, t.shape[0] // N_MICROBATCH) + t.shape[1:])
    return _jnp.moveaxis(t, 1, axis + 1)


def setup_inputs(seed: int = 0) -> dict:
    inp = _fwd_setup_inputs(seed)
    key = _jax.random.fold_in(_jax.random.key(seed), 7919)
    shape, _ = _output_shape()
    out = dict(inp)
    out["loss_target"] = _jax.random.normal(_jax.random.fold_in(key, 0), shape, _jnp.float32)
    for i, name in enumerate(TWIN_WEIGHTS):
        w = inp[name].astype(_jnp.float32)
        if MOMENT_SCALE is None:
            s = _jnp.sqrt(_jnp.mean(_jnp.square(w)) + 1e-30)
        else:
            s = MOMENT_SCALE[name]
        km, kv = _jax.random.split(_jax.random.fold_in(key, i + 1))
        out[name] = w
        out["m_" + name] = s * _jax.random.normal(km, w.shape, _jnp.float32)
        out["v_" + name] = (s * s) * _jax.random.uniform(kv, w.shape, _jnp.float32, 0.5, 1.5)
    if N_MICROBATCH > 1:
        for name, axis in PER_EXAMPLE_BATCH_AXIS.items():
            out[name] = _to_microbatches(out[name], axis)
    return {'x': out['x'], 'c': out['c'], 'ctx': out['ctx'], 'c_ctx': out['c_ctx'], 'ada_w': out['ada_w'], 'ada_b': out['ada_b'], 'norm_g': out['norm_g'], 'w_in': out['w_in'], 'na_rpb': out['na_rpb'], 'ret_decay_logit': out['ret_decay_logit'], 'w_proj_na': out['w_proj_na'], 'w_proj_ret': out['w_proj_ret'], 'w_out': out['w_out'], 'final_g': out['final_g'], 'loss_target': out['loss_target'], 'm_c_ctx': out['m_c_ctx'], 'm_ada_w': out['m_ada_w'], 'm_ada_b': out['m_ada_b'], 'm_norm_g': out['m_norm_g'], 'm_w_in': out['m_w_in'], 'm_na_rpb': out['m_na_rpb'], 'm_ret_decay_logit': out['m_ret_decay_logit'], 'm_w_proj_na': out['m_w_proj_na'], 'm_w_proj_ret': out['m_w_proj_ret'], 'm_w_out': out['m_w_out'], 'm_final_g': out['m_final_g'], 'v_c_ctx': out['v_c_ctx'], 'v_ada_w': out['v_ada_w'], 'v_ada_b': out['v_ada_b'], 'v_norm_g': out['v_norm_g'], 'v_w_in': out['v_w_in'], 'v_na_rpb': out['v_na_rpb'], 'v_ret_decay_logit': out['v_ret_decay_logit'], 'v_w_proj_na': out['v_w_proj_na'], 'v_w_proj_ret': out['v_w_proj_ret'], 'v_w_out': out['v_w_out'], 'v_final_g': out['v_final_g']}


def _loss(weights, diff, rest, loss_target):
    with _jax.named_scope("forward"):
        args = {**rest, TWIN_DIFF_INPUT: diff, **{k: w.astype(_WEIGHT_DTYPES[k]) for k, w in weights.items()}}
        y = _forward(args)
    with _jax.named_scope("loss_head"):
        err = _jnp.square(y.astype(_jnp.float32) - loss_target)
        return 0.5 * _jnp.sum(_jnp.mean(err, axis=-1)) if err.ndim else 0.5 * err


def _adamw(w, g, m, v):
    m = ADAM_B1 * m + (1.0 - ADAM_B1) * g
    v = ADAM_B2 * v + (1.0 - ADAM_B2) * _jnp.square(g)
    m_hat = m / (1.0 - ADAM_B1 ** ADAM_STEP)
    v_hat = v / (1.0 - ADAM_B2 ** ADAM_STEP)
    delta = -ADAM_LR * (m_hat / (_jnp.sqrt(v_hat) + ADAM_EPS) + ADAM_WD * w)
    return delta, m, v


def reference(x, c, ctx, c_ctx, ada_w, ada_b, norm_g, w_in, na_rpb, ret_decay_logit, w_proj_na, w_proj_ret, w_out, final_g, loss_target, m_c_ctx, m_ada_w, m_ada_b, m_norm_g, m_w_in, m_na_rpb, m_ret_decay_logit, m_w_proj_na, m_w_proj_ret, m_w_out, m_final_g, v_c_ctx, v_ada_w, v_ada_b, v_norm_g, v_w_in, v_na_rpb, v_ret_decay_logit, v_w_proj_na, v_w_proj_ret, v_w_out, v_final_g):
    given = dict(x=x, c=c, ctx=ctx, c_ctx=c_ctx, ada_w=ada_w, ada_b=ada_b, norm_g=norm_g, w_in=w_in, na_rpb=na_rpb, ret_decay_logit=ret_decay_logit, w_proj_na=w_proj_na, w_proj_ret=w_proj_ret, w_out=w_out, final_g=final_g, loss_target=loss_target, m_c_ctx=m_c_ctx, m_ada_w=m_ada_w, m_ada_b=m_ada_b, m_norm_g=m_norm_g, m_w_in=m_w_in, m_na_rpb=m_na_rpb, m_ret_decay_logit=m_ret_decay_logit, m_w_proj_na=m_w_proj_na, m_w_proj_ret=m_w_proj_ret, m_w_out=m_w_out, m_final_g=m_final_g, v_c_ctx=v_c_ctx, v_ada_w=v_ada_w, v_ada_b=v_ada_b, v_norm_g=v_norm_g, v_w_in=v_w_in, v_na_rpb=v_na_rpb, v_ret_decay_logit=v_ret_decay_logit, v_w_proj_na=v_w_proj_na, v_w_proj_ret=v_w_proj_ret, v_w_out=v_w_out, v_final_g=v_final_g)
    weights = {n: given[n] for n in TWIN_WEIGHTS}
    shared = {n: given[n] for n in SHARED_INPUTS}
    per_example = {n: given[n] for n in ['x', 'c', 'ctx']}
    grad_fn = _jax.value_and_grad(_loss, argnums=(0, 1))

    def one_microbatch(ex, loss_target):
        ex = dict(ex)
        diff = ex.pop(TWIN_DIFF_INPUT)
        return grad_fn(weights, diff, {**shared, **ex}, loss_target)

    if N_MICROBATCH == 1:
        loss, (grad_w, grad_x) = one_microbatch(per_example, given["loss_target"])
    else:
        def body(carry, xs):
            loss_sum, grad_sum = carry
            l_k, (gw_k, gx_k) = one_microbatch(xs[0], xs[1])
            with _jax.named_scope("update"):
                return (loss_sum + l_k, _jax.tree.map(_jnp.add, grad_sum, gw_k)), gx_k

        init = (_jnp.zeros((), _jnp.float32), _jax.tree.map(_jnp.zeros_like, weights))
        (loss, grad_w), grad_x = _jax.lax.scan(body, init, (per_example, given["loss_target"]))
    with _jax.named_scope("update"):
        delta_w, new_m, new_v = {}, {}, {}
        for n in TWIN_WEIGHTS:
            delta_w[n], new_m[n], new_v[n] = _adamw(weights[n], grad_w[n], given["m_" + n], given["v_" + n])
    return (loss, grad_x, *[grad_w[n] for n in TWIN_WEIGHTS], *[delta_w[n] for n in TWIN_WEIGHTS],
            *[new_m[n] for n in TWIN_WEIGHTS], *[new_v[n] for n in TWIN_WEIGHTS])
```

```python
import functools
import math

import numpy as np
import jax
import jax.numpy as jnp
from jax import lax
from jax.experimental import pallas as pl
from jax.experimental.pallas import tpu as pltpu

GRID_W = 64
NA_HEAD_DIM = 128
NA_WIN_ROWS = 8
NA_WIN_COLS = 16
RET_KEY_DIM = 128
RET_VAL_DIM = 256
RET_CHUNK = 128
ROPE_BASE = 10000.0
NORM_EPS = 1e-6
MASK_VALUE = -1e30
ADAM_LR = 0.001
ADAM_B1 = 0.9
ADAM_B2 = 0.999
ADAM_EPS = 1e-08
ADAM_WD = 0.01
ADAM_STEP = 10

N_CHIPS = 4
N_DEV = 8
LANES = 128
VMEM_LIMIT = 56 * 1024 * 1024
BF16 = jnp.bfloat16
F32 = jnp.float32
MESH = pl.DeviceIdType.MESH
ANY = pl.BlockSpec(memory_space=pl.ANY)


def _tile(dim, pref, align=LANES):
    if dim <= pref:
        return dim
    t = (pref // align) * align
    while t >= align:
        if dim % t == 0:
            return t
        t -= align
    return dim


def _rows_per_tile(rows, width, tile_bytes=1 << 20):
    return _tile(rows, max(8, tile_bytes // (4 * width)), 8)


def _params(sem):
    return pltpu.CompilerParams(dimension_semantics=sem, vmem_limit_bytes=VMEM_LIMIT)


def _sigmoid(x):
    return 1.0 / (1.0 + jnp.exp(-x))


def _dot(a, b, ca, cb):
    return lax.dot_general(a, b, (((ca,), (cb,)), ((), ())), preferred_element_type=F32)


def _mm(a, b, *, ta=False, tb=False, a_lead=None, b_lead=None, out_dtype=BF16, tm=768, tn=512, tk=2048, name):
    ash = a.shape[1:] if a_lead is not None else a.shape
    bsh = b.shape[1:] if b_lead is not None else b.shape
    m, k = (ash[1], ash[0]) if ta else ash
    n, k2 = bsh if tb else (bsh[1], bsh[0])
    assert k == k2, (name, ash, bsh)
    tm, tn, tk = _tile(m, tm), _tile(n, tn), _tile(k, tk)
    nk = k // tk

    def lead(spec_shape, imap, l):
        if l is None:
            return pl.BlockSpec(spec_shape, imap)
        return pl.BlockSpec((None,) + spec_shape, lambda i, j, kk: (l,) + imap(i, j, kk))

    a_spec = lead((tk, tm), lambda i, j, kk: (kk, i), a_lead) if ta else lead((tm, tk), lambda i, j, kk: (i, kk), a_lead)
    b_spec = lead((tn, tk), lambda i, j, kk: (j, kk), b_lead) if tb else lead((tk, tn), lambda i, j, kk: (kk, j), b_lead)
    ca, cb = (0 if ta else 1), (1 if tb else 0)

    def body(a_ref, b_ref, o_ref, *scratch):
        part = _dot(a_ref[...].astype(BF16), b_ref[...].astype(BF16), ca, cb)
        if nk == 1:
            o_ref[...] = part.astype(o_ref.dtype)
            return
        acc_ref, = scratch
        kk = pl.program_id(2)

        @pl.when(kk == 0)
        def _():
            acc_ref[...] = part

        @pl.when(kk > 0)
        def _():
            acc_ref[...] += part

        @pl.when(kk == nk - 1)
        def _():
            o_ref[...] = acc_ref[...].astype(o_ref.dtype)

    return pl.pallas_call(
        body, name=name, grid=(m // tm, n // tn, nk),
        in_specs=[a_spec, b_spec],
        out_specs=pl.BlockSpec((tm, tn), lambda i, j, kk: (i, j)),
        out_shape=jax.ShapeDtypeStruct((m, n), out_dtype),
        scratch_shapes=[] if nk == 1 else [pltpu.VMEM((tm, tn), F32)],
        compiler_params=_params(("parallel", "parallel", "arbitrary")),
    )(a, b)


def _ew(fn, ins, outs, *, rows, tr, name, n0=None, aliases=None):
    assert rows % tr == 0, (name, rows, tr)
    nt = rows // tr

    def grp(i):
        return 0 if n0 is None else jnp.where(i < n0, 0, 1)

    in_specs, args = [], []
    for spec in ins:
        if spec[0] == 't':
            arr, cb, w = spec[1], spec[2], spec[3]
            l = spec[4] if len(spec) > 4 else None
            if l is None:
                in_specs.append(pl.BlockSpec((tr, w), functools.partial(lambda i, cb: (i, cb), cb=cb)))
            else:
                in_specs.append(pl.BlockSpec((None, tr, w), functools.partial(lambda i, cb, l: (l, i, cb), cb=cb, l=l)))
            args.append(arr)
        else:
            arr = spec[1]
            g = arr.shape[0]
            if g == 1:
                in_specs.append(pl.BlockSpec((None, 1, arr.shape[2]), lambda i: (0, 0, 0)))
            else:
                in_specs.append(pl.BlockSpec((None, 1, arr.shape[2]), lambda i: (grp(i), 0, 0)))
            args.append(arr)
    out_specs, out_shapes, is_red = [], [], []
    for spec in outs:
        if spec[0] == 't':
            w, dt = spec[1], spec[2]
            if len(spec) > 3:
                l, nl = spec[3], spec[4]
                out_specs.append(pl.BlockSpec((None, tr, w), functools.partial(lambda i, l: (l, i, 0), l=l)))
                out_shapes.append(jax.ShapeDtypeStruct((nl, rows, w), dt))
            else:
                out_specs.append(pl.BlockSpec((tr, w), lambda i: (i, 0)))
                out_shapes.append(jax.ShapeDtypeStruct((rows, w), dt))
            is_red.append(False)
        else:
            w, g = spec[1], spec[2]
            if g == 1:
                out_specs.append(pl.BlockSpec((None, 1, w), lambda i: (0, 0, 0)))
            else:
                out_specs.append(pl.BlockSpec((None, 1, w), lambda i: (grp(i), 0, 0)))
            out_shapes.append(jax.ShapeDtypeStruct((g, 1, w), F32))
            is_red.append(True)
    n_in = len(ins)
    n_alias = 0 if aliases is None else len(aliases)

    def body(*refs):
        in_refs = refs[:n_in]
        out_refs = refs[n_in + n_alias:]
        res = fn(*[r[...] for r in in_refs])
        if not isinstance(res, (tuple, list)):
            res = (res,)
        i = pl.program_id(0)
        first = (i == 0) if n0 is None else ((i == 0) | (i == n0))
        for o_ref, val, red in zip(out_refs, res, is_red):
            if not red:
                o_ref[...] = val.astype(o_ref.dtype)
            else:
                @pl.when(first)
                def _(o_ref=o_ref, val=val):
                    o_ref[...] = val

                @pl.when(jnp.logical_not(first))
                def _(o_ref=o_ref, val=val):
                    o_ref[...] += val

    io_alias = {}
    if aliases is not None:
        for a_idx, (arr, o_idx) in enumerate(aliases):
            in_specs.append(ANY)
            args.append(arr)
            io_alias[n_in + a_idx] = o_idx
    has_red = any(is_red)
    return pl.pallas_call(
        body, name=name, grid=(nt,), in_specs=in_specs, out_specs=out_specs, out_shape=out_shapes,
        input_output_aliases=io_alias,
        compiler_params=_params(("arbitrary",) if has_red else ("parallel",)),
    )(*args)


def _rsum(v):
    return jnp.sum(v, axis=0, keepdims=True)


def _silu_parts(z):
    sg = _sigmoid(z)
    return z * sg, sg * (1.0 + z * (1.0 - sg))


def _na_bias_table(rpb, rows):
    kh, kw = NA_WIN_ROWS, NA_WIN_COLS
    assert rows >= kh
    cidx = np.arange(GRID_W)
    c0 = np.clip(cidx - kw // 2, 0, GRID_W - kw)
    col_in = (cidx[None, :] >= c0[:, None]) & (cidx[None, :] < c0[:, None] + kw)
    dc = np.clip(cidx[None, :] - cidx[:, None] + (kw - 1), 0, 2 * kw - 2)
    dr = np.arange(kh)[None, :] - np.arange(kh)[:, None] + (kh - 1)
    bias = rpb[:, dr[:, None, :, None], dc[None, :, None, :]]
    bias = jnp.where(col_in[None, None, :, None, :], bias, MASK_VALUE)
    return bias.reshape(rpb.shape[0], kh, GRID_W, kh * GRID_W)


def _na_onehots():
    kh, kw = NA_WIN_ROWS, NA_WIN_COLS
    cidx = np.arange(GRID_W)
    dc = cidx[None, :] - cidx[:, None] + (kw - 1)
    e2 = np.zeros((GRID_W * GRID_W, LANES), np.float32)
    ok = (dc >= 0) & (dc <= 2 * kw - 2)
    cq, ck = np.nonzero(ok)
    e2[cq * GRID_W + ck, dc[cq, ck]] = 1.0
    dr = np.arange(kh)[None, :] - np.arange(kh)[:, None] + (kh - 1)
    e1 = np.zeros((16, kh * kh), np.float32)
    dl, kr = np.nonzero(np.ones_like(dr))
    e1[dr[dl, kr], dl * kh + kr] = 1.0
    return jnp.asarray(e1), jnp.asarray(e2)


def _na_row_scores(q, kl, kc, bias, scale):
    s_loc = _dot(q, kl, 1, 1) * scale + bias
    s_ctx = _dot(q, kc, 1, 1) * scale
    m = jnp.maximum(jnp.max(s_loc, axis=-1, keepdims=True), jnp.max(s_ctx, axis=-1, keepdims=True))
    p_loc = jnp.exp(s_loc - m)
    p_ctx = jnp.exp(s_ctx - m)
    den = jnp.sum(p_loc, axis=-1, keepdims=True) + jnp.sum(p_ctx, axis=-1, keepdims=True)
    return p_loc, p_ctx, den


def _na_fwd(u, bias, *, s_len, heads, name):
    t_len = u.shape[0]
    rows = s_len // GRID_W
    nloc = NA_WIN_ROWS * GRID_W
    scale = NA_HEAD_DIM ** -0.5
    hd = NA_HEAD_DIM

    def body(q_ref, k_ref, v_ref, b_ref, o_ref):
        kc = k_ref[s_len:t_len, :]
        vc = v_ref[s_len:t_len, :]

        def row(r, carry):
            r0 = jnp.clip(r - NA_WIN_ROWS // 2, 0, rows - NA_WIN_ROWS)
            qs = pl.multiple_of(r * GRID_W, GRID_W)
            ks = pl.multiple_of(r0 * GRID_W, GRID_W)
            q = q_ref[pl.ds(qs, GRID_W), :]
            kl = k_ref[pl.ds(ks, nloc), :]
            vl = v_ref[pl.ds(ks, nloc), :]
            p_loc, p_ctx, den = _na_row_scores(q, kl, kc, b_ref[r - r0], scale)
            o = _dot(p_loc.astype(BF16), vl, 1, 0) + _dot(p_ctx.astype(BF16), vc, 1, 0)
            o_ref[pl.ds(qs, GRID_W), :] = (o / den).astype(o_ref.dtype)
            return carry

        lax.fori_loop(0, rows, row, 0)
        qc = q_ref[s_len:t_len, :]
        s = _dot(qc, kc, 1, 1) * scale
        p = jnp.exp(s - jnp.max(s, axis=-1, keepdims=True))
        o = _dot(p.astype(BF16), vc, 1, 0) / jnp.sum(p, axis=-1, keepdims=True)
        o_ref[s_len:t_len, :] = o.astype(o_ref.dtype)

    col = lambda off: pl.BlockSpec((t_len, hd), functools.partial(lambda h, off: (0, off + h), off=off))
    return pl.pallas_call(
        body, name=name, grid=(heads,),
        in_specs=[col(0), col(heads), col(2 * heads),
                  pl.BlockSpec((None, NA_WIN_ROWS, GRID_W, nloc), lambda h: (h, 0, 0, 0))],
        out_specs=pl.BlockSpec((t_len, hd), lambda h: (0, h)),
        out_shape=jax.ShapeDtypeStruct((t_len, heads * hd), BF16),
        compiler_params=_params(("parallel",)),
    )(u, u, u, bias)


def _na_bwd(u, bias, o, do, *, s_len, heads, name):
    t_len = u.shape[0]
    rows = s_len // GRID_W
    nloc = NA_WIN_ROWS * GRID_W
    scale = NA_HEAD_DIM ** -0.5
    hd = NA_HEAD_DIM

    def body(q_ref, k_ref, v_ref, b_ref, o_ref, do_ref, dq_ref, dk_ref, dv_ref, db_ref, dk_acc, dv_acc):
        kc = k_ref[s_len:t_len, :]
        vc = v_ref[s_len:t_len, :]
        dk_acc[...] = jnp.zeros_like(dk_acc)
        dv_acc[...] = jnp.zeros_like(dv_acc)
        db_ref[...] = jnp.zeros_like(db_ref)

        def row(r, carry):
            r0 = jnp.clip(r - NA_WIN_ROWS // 2, 0, rows - NA_WIN_ROWS)
            dl = r - r0
            qs = pl.multiple_of(r * GRID_W, GRID_W)
            ks = pl.multiple_of(r0 * GRID_W, GRID_W)
            q = q_ref[pl.ds(qs, GRID_W), :]
            kl = k_ref[pl.ds(ks, nloc), :]
            vl = v_ref[pl.ds(ks, nloc), :]
            dout = do_ref[pl.ds(qs, GRID_W), :]
            out = o_ref[pl.ds(qs, GRID_W), :]
            p_loc, p_ctx, den = _na_row_scores(q, kl, kc, b_ref[dl], scale)
            inv = 1.0 / den
            p_loc = p_loc * inv
            p_ctx = p_ctx * inv
            dlt = jnp.sum(dout.astype(F32) * out.astype(F32), axis=-1, keepdims=True)
            ds_loc = p_loc * (_dot(dout, vl, 1, 1) - dlt)
            ds_ctx = p_ctx * (_dot(dout, vc, 1, 1) - dlt)
            db_ref[dl] += ds_loc
            ds_loc_b = ds_loc.astype(BF16)
            ds_ctx_b = ds_ctx.astype(BF16)
            dq = (_dot(ds_loc_b, kl, 1, 0) + _dot(ds_ctx_b, kc, 1, 0)) * scale
            dq_ref[pl.ds(qs, GRID_W), :] = dq.astype(dq_ref.dtype)
            dk_acc[pl.ds(ks, nloc), :] += _dot(ds_loc_b, q, 0, 0) * scale
            dv_acc[pl.ds(ks, nloc), :] += _dot(p_loc.astype(BF16), dout, 0, 0)
            dk_acc[s_len:t_len, :] += _dot(ds_ctx_b, q, 0, 0) * scale
            dv_acc[s_len:t_len, :] += _dot(p_ctx.astype(BF16), dout, 0, 0)
            return carry

        lax.fori_loop(0, rows, row, 0)
        qc = q_ref[s_len:t_len, :]
        dout = do_ref[s_len:t_len, :]
        out = o_ref[s_len:t_len, :]
        s = _dot(qc, kc, 1, 1) * scale
        p = jnp.exp(s - jnp.max(s, axis=-1, keepdims=True))
        p = p / jnp.sum(p, axis=-1, keepdims=True)
        dlt = jnp.sum(dout.astype(F32) * out.astype(F32), axis=-1, keepdims=True)
        ds = (p * (_dot(dout, vc, 1, 1) - dlt)).astype(BF16)
        dq_ref[s_len:t_len, :] = (_dot(ds, kc, 1, 0) * scale).astype(dq_ref.dtype)
        dk_acc[s_len:t_len, :] += _dot(ds, qc, 0, 0) * scale
        dv_acc[s_len:t_len, :] += _dot(p.astype(BF16), dout, 0, 0)
        dk_ref[...] = dk_acc[...].astype(dk_ref.dtype)
        dv_ref[...] = dv_acc[...].astype(dv_ref.dtype)

    col = lambda off: pl.BlockSpec((t_len, hd), functools.partial(lambda h, off: (0, off + h), off=off))
    tbl = pl.BlockSpec((None, NA_WIN_ROWS, GRID_W, nloc), lambda h: (h, 0, 0, 0))
    tok = jax.ShapeDtypeStruct((t_len, heads * hd), BF16)
    return pl.pallas_call(
        body, name=name, grid=(heads,),
        in_specs=[col(0), col(heads), col(2 * heads), tbl, col(0), col(0)],
        out_specs=[col(0), col(0), col(0), tbl],
        out_shape=[tok, tok, tok, jax.ShapeDtypeStruct(bias.shape, F32)],
        scratch_shapes=[pltpu.VMEM((t_len, hd), F32), pltpu.VMEM((t_len, hd), F32)],
        compiler_params=_params(("parallel",)),
    )(u, u, u, bias, o, do)


def _split3(x):
    hi = x.astype(BF16)
    r1 = x - hi.astype(F32)
    mid = r1.astype(BF16)
    lo = (r1 - mid.astype(F32)).astype(BF16)
    return hi, mid, lo


def _rpb_grad(dbias, *, name):
    heads = dbias.shape[0]
    kh = NA_WIN_ROWS
    e1, e2 = _na_onehots()
    x = dbias.reshape(heads, kh, GRID_W, kh, GRID_W).transpose(0, 1, 3, 2, 4).reshape(heads, kh * kh, GRID_W * GRID_W)

    def body(x_ref, e1_ref, e2_ref, o_ref):
        e2b = e2_ref[...].astype(BF16)
        y = sum(_dot(part, e2b, 1, 0) for part in _split3(x_ref[...]))
        e1b = e1_ref[...].astype(BF16)
        o_ref[...] = sum(_dot(e1b, part, 1, 0) for part in _split3(y))

    out = pl.pallas_call(
        body, name=name, grid=(heads,),
        in_specs=[pl.BlockSpec((None, kh * kh, GRID_W * GRID_W), lambda h: (h, 0, 0)),
                  pl.BlockSpec(e1.shape, lambda h: (0, 0)), pl.BlockSpec(e2.shape, lambda h: (0, 0))],
        out_specs=pl.BlockSpec((None, 16, LANES), lambda h: (h, 0, 0)),
        out_shape=jax.ShapeDtypeStruct((heads, 16, LANES), F32),
        compiler_params=_params(("parallel",)),
    )(x, e1, e2)
    return out[:, :2 * kh - 1, :2 * NA_WIN_COLS - 1]


def _rope_tables(s_len, l_len):
    nf = RET_KEY_DIM // 4
    t = np.arange(s_len)
    row = (t // GRID_W).astype(np.float32)
    colp = (t % GRID_W).astype(np.float32)
    inv_freq = jnp.asarray(ROPE_BASE, F32) ** (-jnp.arange(nf, dtype=F32) / nf)
    ang = jnp.concatenate([jnp.asarray(row)[:, None] * inv_freq, jnp.asarray(colp)[:, None] * inv_freq], axis=-1)
    cos, sin = jnp.cos(ang), jnp.sin(ang)
    c2 = jnp.concatenate([cos, cos], axis=-1)
    s2 = jnp.concatenate([-sin, sin], axis=-1)
    c2 = jnp.concatenate([c2, jnp.ones((l_len, RET_KEY_DIM), F32)], axis=0)
    s2 = jnp.concatenate([s2, jnp.zeros((l_len, RET_KEY_DIM), F32)], axis=0)
    return c2, s2


def _rope(x, c2, s2):
    return x * c2 + pltpu.roll(x, RET_KEY_DIM // 2, 1) * s2


def _rope_t(d, c2, s2):
    return d * c2 + pltpu.roll(d * s2, RET_KEY_DIM // 2, 1)


def _ret_decays(lg, direction):
    cs = RET_CHUNK
    i_col = lax.broadcasted_iota(jnp.int32, (cs, 1), 0)
    p_col = jnp.where(direction == 0, i_col, cs - 1 - i_col).astype(F32)
    pi = lax.broadcasted_iota(jnp.int32, (cs, cs), 0)
    pj = lax.broadcasted_iota(jnp.int32, (cs, cs), 1)
    diff = jnp.where(direction == 0, pi - pj, pj - pi).astype(F32)
    dm = jnp.where(diff >= 0, jnp.exp(jnp.maximum(diff, 0.0) * lg), 0.0)
    qdec = jnp.exp((p_col + 1.0) * lg)
    kdec = jnp.exp((cs - 1.0 - p_col) * lg)
    cd = jnp.exp(jnp.full((1, 1), cs, F32) * lg)
    return p_col, dm, qdec, kdec, cd


def _ret_chunk_index(t, direction, n_chunks, lat_chunks):
    return jnp.where(direction == 0, lax.rem(t + lat_chunks, n_chunks), n_chunks - 1 - t)


def _ret_fwd(u, c2, s2, lg, *, s_len, heads, q_off, name):
    t_len = u.shape[0]
    cs, dk, dv = RET_CHUNK, RET_KEY_DIM, RET_VAL_DIM
    n_chunks, lat_chunks = t_len // cs, s_len // cs
    k_scale = dk ** -0.5
    qb, kb, vb = q_off // dk, q_off // dk + heads, (q_off + 2 * heads * dk) // dv

    def body(lg_ref, q_ref, k_ref, v_ref, c_ref, s_ref, o_ref, st_ref, state):
        h, d = pl.program_id(0), pl.program_id(1)
        _, dm, qdec, kdec, cd = _ret_decays(lg_ref[d, h], d)
        state[...] = jnp.zeros_like(state)

        def step(t, carry):
            c = _ret_chunk_index(t, d, n_chunks, lat_chunks)
            r = pl.ds(pl.multiple_of(c * cs, cs), cs)
            cc, ss = c_ref[r, :], s_ref[r, :]
            qc = _rope(q_ref[r, :].astype(F32), cc, ss)
            kc = _rope(k_ref[r, :].astype(F32), cc, ss) * k_scale
            vc = v_ref[r, :]
            st = state[...]
            st_ref[t] = st
            a = _dot(qc.astype(BF16), kc.astype(BF16), 1, 1) * dm
            oc = _dot(a.astype(BF16), vc, 1, 0) + _dot((qc * qdec).astype(BF16), st.astype(BF16), 1, 0)
            state[...] = st * cd + _dot((kc * kdec).astype(BF16), vc, 0, 0)

            @pl.when(d == 0)
            def _():
                o_ref[r, :] = oc

            @pl.when(d == 1)
            def _():
                o_ref[r, :] += oc

            return carry

        lax.fori_loop(0, n_chunks, step, 0)

    return pl.pallas_call(
        body, name=name, grid=(heads, 2),
        in_specs=[pl.BlockSpec(memory_space=pltpu.SMEM),
                  pl.BlockSpec((t_len, dk), lambda h, d: (0, qb + h)),
                  pl.BlockSpec((t_len, dk), lambda h, d: (0, kb + h)),
                  pl.BlockSpec((t_len, dv), lambda h, d: (0, vb + h)),
                  pl.BlockSpec((t_len, dk), lambda h, d: (0, 0)),
                  pl.BlockSpec((t_len, dk), lambda h, d: (0, 0))],
        out_specs=[pl.BlockSpec((t_len, dv), lambda h, d: (0, h)),
                   pl.BlockSpec((None, None, n_chunks, dk, dv), lambda h, d: (h, d, 0, 0, 0))],
        out_shape=[jax.ShapeDtypeStruct((t_len, heads * dv), F32),
                   jax.ShapeDtypeStruct((heads, 2, n_chunks, dk, dv), F32)],
        scratch_shapes=[pltpu.VMEM((dk, dv), F32)],
        compiler_params=_params(("parallel", "arbitrary")),
    )(lg, u, u, u, c2, s2)


def _ret_bwd(u, c2, s2, lg, states, do, *, s_len, heads, q_off, name):
    t_len = u.shape[0]
    cs, dk, dv = RET_CHUNK, RET_KEY_DIM, RET_VAL_DIM
    n_chunks, lat_chunks = t_len // cs, s_len // cs
    k_scale = dk ** -0.5
    qb, kb, vb = q_off // dk, q_off // dk + heads, (q_off + 2 * heads * dk) // dv

    def body(lg_ref, q_ref, k_ref, v_ref, c_ref, s_ref, st_ref, do_ref, dq_ref, dk_ref, dv_ref, dlg_ref, dstate, acc):
        h, d = pl.program_id(0), pl.program_id(1)
        p_col, dm, qdec, kdec, cd = _ret_decays(lg_ref[d, h], d)
        dstate[...] = jnp.zeros_like(dstate)
        acc[...] = jnp.zeros_like(acc)

        def step(i, carry):
            t = n_chunks - 1 - i
            c = _ret_chunk_index(t, d, n_chunks, lat_chunks)
            r = pl.ds(pl.multiple_of(c * cs, cs), cs)
            cc, ss = c_ref[r, :], s_ref[r, :]
            qc = _rope(q_ref[r, :].astype(F32), cc, ss)
            kc = _rope(k_ref[r, :].astype(F32), cc, ss) * k_scale
            vc = v_ref[r, :]
            doc = do_ref[r, :].astype(BF16)
            st = st_ref[t]
            dst = dstate[...]
            qb16, kb16 = qc.astype(BF16), kc.astype(BF16)
            a = _dot(qb16, kb16, 1, 1) * dm
            dam = (_dot(doc, vc, 1, 1) * dm).astype(BF16)
            dq_i = _dot(dam, kb16, 1, 0)
            dk_i = _dot(dam, qb16, 0, 0)
            dq_c = _dot(doc, st.astype(BF16), 1, 1) * qdec
            dst16 = dst.astype(BF16)
            dvc = _dot(a.astype(BF16), doc, 0, 0) + _dot((kc * kdec).astype(BF16), dst16, 1, 0)
            dk_s = _dot(vc, dst16, 1, 1) * kdec
            g = (jnp.sum(qc * (p_col * dq_i + (p_col + 1.0) * dq_c), axis=-1, keepdims=True)
                 + jnp.sum(kc * ((cs - 1.0 - p_col) * dk_s - p_col * dk_i), axis=-1, keepdims=True))
            g = jnp.sum(g, axis=0, keepdims=True) + cs * cd * jnp.sum(jnp.sum(dst * st, axis=-1, keepdims=True), axis=0, keepdims=True)
            acc[...] += jnp.broadcast_to(g, acc.shape)
            dstate[...] = dst * cd + _dot((qc * qdec).astype(BF16), doc, 0, 0)
            dq = _rope_t(dq_i + dq_c, cc, ss)
            dkk = _rope_t((dk_i + dk_s) * k_scale, cc, ss)

            @pl.when(d == 0)
            def _():
                dq_ref[r, :] = dq.astype(dq_ref.dtype)
                dk_ref[r, :] = dkk.astype(dk_ref.dtype)
                dv_ref[r, :] = dvc.astype(dv_ref.dtype)

            @pl.when(d == 1)
            def _():
                dq_ref[r, :] = (dq_ref[r, :].astype(F32) + dq).astype(dq_ref.dtype)
                dk_ref[r, :] = (dk_ref[r, :].astype(F32) + dkk).astype(dk_ref.dtype)
                dv_ref[r, :] = (dv_ref[r, :].astype(F32) + dvc).astype(dv_ref.dtype)

            return carry

        lax.fori_loop(0, n_chunks, step, 0)
        dlg_ref[...] = acc[...]

    return pl.pallas_call(
        body, name=name, grid=(heads, 2),
        in_specs=[pl.BlockSpec(memory_space=pltpu.SMEM),
                  pl.BlockSpec((t_len, dk), lambda h, d: (0, qb + h)),
                  pl.BlockSpec((t_len, dk), lambda h, d: (0, kb + h)),
                  pl.BlockSpec((t_len, dv), lambda h, d: (0, vb + h)),
                  pl.BlockSpec((t_len, dk), lambda h, d: (0, 0)),
                  pl.BlockSpec((t_len, dk), lambda h, d: (0, 0)),
                  pl.BlockSpec((None, None, n_chunks, dk, dv), lambda h, d: (h, d, 0, 0, 0)),
                  pl.BlockSpec((t_len, dv), lambda h, d: (0, h))],
        out_specs=[pl.BlockSpec((t_len, dk), lambda h, d: (0, h)),
                   pl.BlockSpec((t_len, dk), lambda h, d: (0, h)),
                   pl.BlockSpec((t_len, dv), lambda h, d: (0, h)),
                   pl.BlockSpec((None, None, 8, LANES), lambda h, d: (h, d, 0, 0))],
        out_shape=[jax.ShapeDtypeStruct((t_len, heads * dk), BF16),
                   jax.ShapeDtypeStruct((t_len, heads * dk), BF16),
                   jax.ShapeDtypeStruct((t_len, heads * dv), BF16),
                   jax.ShapeDtypeStruct((heads, 2, 8, LANES), F32)],
        scratch_shapes=[pltpu.VMEM((dk, dv), F32), pltpu.VMEM((8, LANES), F32)],
        compiler_params=_params(("parallel", "arbitrary")),
    )(lg, u, u, u, c2, s2, states, do)


def _mesh_pos():
    return lax.axis_index("x"), lax.axis_index("y"), lax.axis_index("c")


def _all_gather_small(buf, *, name):
    r = buf.shape[0]

    def body(x_ref, o_ref, send_sems, recv_sems, local_sem):
        x, y, c = _mesh_pos()
        me = 4 * x + 2 * y + c
        mine = pltpu.make_async_copy(x_ref, o_ref.at[me], local_sem)
        mine.start()
        copies = []
        for k in range(1, N_DEV):
            px, py, pc = x ^ ((k >> 2) & 1), y ^ ((k >> 1) & 1), c ^ (k & 1)
            cp = pltpu.make_async_remote_copy(
                src_ref=x_ref, dst_ref=o_ref.at[me], send_sem=send_sems.at[k - 1], recv_sem=recv_sems.at[k - 1],
                device_id=(px, py, pc), device_id_type=MESH)
            cp.start()
            copies.append((cp, 4 * px + 2 * py + pc))
        for k, (cp, peer) in enumerate(copies):
            pltpu.make_async_remote_copy(
                src_ref=x_ref, dst_ref=o_ref.at[peer], send_sem=send_sems.at[k], recv_sem=recv_sems.at[k],
                device_id=(x, y, c), device_id_type=MESH).wait_recv()
        for cp, _ in copies:
            cp.wait_send()
        mine.wait()

    return pl.pallas_call(
        body, name=name,
        in_specs=[pl.BlockSpec(memory_space=pltpu.VMEM)],
        out_specs=pl.BlockSpec(memory_space=pltpu.VMEM),
        out_shape=jax.ShapeDtypeStruct((N_DEV, r, LANES), F32),
        scratch_shapes=[pltpu.SemaphoreType.DMA((N_DEV - 1,)), pltpu.SemaphoreType.DMA((N_DEV - 1,)),
                        pltpu.SemaphoreType.DMA],
        compiler_params=pltpu.CompilerParams(vmem_limit_bytes=VMEM_LIMIT),
    )(buf)


def _shard_slice(ref, axis, j, width):
    idx = [slice(None)] * len(ref.shape)
    idx[axis] = pl.ds(pl.multiple_of(j * width, width), width)
    return ref.at[tuple(idx)]


def _gather_weights(shards, axes, *, name):
    n = len(shards)
    full_shapes = []
    for s, ax in zip(shards, axes):
        shp = list(s.shape)
        shp[ax] *= N_CHIPS
        full_shapes.append(jax.ShapeDtypeStruct(tuple(shp), s.dtype))

    def body(*refs):
        ins, outs = refs[:n], refs[n:2 * n]
        send_sems, recv_sems, local_sems = refs[2 * n:]
        x, y, c = _mesh_pos()
        chip = 2 * x + y
        started = []
        for i in range(n):
            w = ins[i].shape[axes[i]]
            mine = pltpu.make_async_copy(ins[i], _shard_slice(outs[i], axes[i], chip, w), local_sems.at[i])
            mine.start()
            started.append(mine)
        sends = []
        for i in range(n):
            w = ins[i].shape[axes[i]]
            for k in range(1, N_CHIPS):
                px, py = x ^ (k >> 1), y ^ (k & 1)
                cp = pltpu.make_async_remote_copy(
                    src_ref=ins[i], dst_ref=_shard_slice(outs[i], axes[i], chip, w),
                    send_sem=send_sems.at[i, k - 1], recv_sem=recv_sems.at[i, k - 1],
                    device_id=(px, py, c), device_id_type=MESH)
                cp.start()
                sends.append(cp)
        for i in range(n):
            w = ins[i].shape[axes[i]]
            for k in range(1, N_CHIPS):
                peer_chip = 2 * (x ^ (k >> 1)) + (y ^ (k & 1))
                pltpu.make_async_remote_copy(
                    src_ref=ins[i], dst_ref=_shard_slice(outs[i], axes[i], peer_chip, w),
                    send_sem=send_sems.at[i, k - 1], recv_sem=recv_sems.at[i, k - 1],
                    device_id=(x, y, c), device_id_type=MESH).wait_recv()
        for cp in sends:
            cp.wait_send()
        for cp in started:
            cp.wait()

    return pl.pallas_call(
        body, name=name, in_specs=[ANY] * n, out_specs=[ANY] * n, out_shape=full_shapes,
        scratch_shapes=[pltpu.SemaphoreType.DMA((n, N_CHIPS - 1)), pltpu.SemaphoreType.DMA((n, N_CHIPS - 1)),
                        pltpu.SemaphoreType.DMA((n,))],
    )(*shards)


def _scatter_grads(grads, axes, *, name):
    n = len(grads)
    out_shapes, widths = [], []
    for g, ax in zip(grads, axes):
        shp = list(g.shape)
        shp[ax] //= N_CHIPS
        widths.append(shp[ax])
        out_shapes.append(jax.ShapeDtypeStruct((N_CHIPS,) + tuple(shp), g.dtype))

    def body(*refs):
        ins, outs = refs[:n], refs[n:2 * n]
        send_sems, recv_sems, local_sems = refs[2 * n:]
        x, y, c = _mesh_pos()
        chip = 2 * x + y
        started = []
        for i in range(n):
            mine = pltpu.make_async_copy(_shard_slice(ins[i], axes[i], chip, widths[i]), outs[i].at[0], local_sems.at[i])
            mine.start()
            started.append(mine)
        sends = []
        for i in range(n):
            for k in range(1, N_CHIPS):
                px, py = x ^ (k >> 1), y ^ (k & 1)
                cp = pltpu.make_async_remote_copy(
                    src_ref=_shard_slice(ins[i], axes[i], 2 * px + py, widths[i]), dst_ref=outs[i].at[k],
                    send_sem=send_sems.at[i, k - 1], recv_sem=recv_sems.at[i, k - 1],
                    device_id=(px, py, c), device_id_type=MESH)
                cp.start()
                sends.append(cp)
        for i in range(n):
            for k in range(1, N_CHIPS):
                pltpu.make_async_remote_copy(
                    src_ref=outs[i].at[k], dst_ref=outs[i].at[k],
                    send_sem=send_sems.at[i, k - 1], recv_sem=recv_sems.at[i, k - 1],
                    device_id=(x, y, c), device_id_type=MESH).wait_recv()
        for cp in sends:
            cp.wait_send()
        for cp in started:
            cp.wait()

    return pl.pallas_call(
        body, name=name, in_specs=[ANY] * n, out_specs=[ANY] * n, out_shape=out_shapes,
        scratch_shapes=[pltpu.SemaphoreType.DMA((n, N_CHIPS - 1)), pltpu.SemaphoreType.DMA((n, N_CHIPS - 1)),
                        pltpu.SemaphoreType.DMA((n,))],
    )(*grads)


def _swap_with_sibling(parts, *, name):
    n = len(parts)

    def body(*refs):
        ins, outs = refs[:n], refs[n:2 * n]
        send_sems, recv_sems = refs[2 * n:]
        x, y, c = _mesh_pos()
        sends = []
        for i in range(n):
            cp = pltpu.make_async_remote_copy(
                src_ref=ins[i], dst_ref=outs[i], send_sem=send_sems.at[i], recv_sem=recv_sems.at[i],
                device_id=(x, y, 1 - c), device_id_type=MESH)
            cp.start()
            sends.append(cp)
        for cp in sends:
            cp.wait()

    return pl.pallas_call(
        body, name=name, in_specs=[ANY] * n, out_specs=[ANY] * n,
        out_shape=[jax.ShapeDtypeStruct(p.shape, p.dtype) for p in parts],
        scratch_shapes=[pltpu.SemaphoreType.DMA((n,)), pltpu.SemaphoreType.DMA((n,))],
    )(*parts)


def _adamw_math(w, g, m, v):
    m = ADAM_B1 * m + (1.0 - ADAM_B1) * g
    v = ADAM_B2 * v + (1.0 - ADAM_B2) * (g * g)
    m_hat = m / (1.0 - ADAM_B1 ** ADAM_STEP)
    v_hat = v / (1.0 - ADAM_B2 ** ADAM_STEP)
    delta = -ADAM_LR * (m_hat / (jnp.sqrt(v_hat) + ADAM_EPS) + ADAM_WD * w)
    return delta, m, v


def _adamw_layer(w3, m3, v3, p, q, layer, prev, *, name):
    nl, rows, width = w3.shape
    tr = _rows_per_tile(rows, width)

    def fn(*t):
        if q is None:
            w, m, v, g = t
        else:
            w, m, v, g, g2 = t
            g = g + g2
        delta, m, v = _adamw_math(w, g, m, v)
        return g, delta, m, v

    ins = [('t', w3, 0, width, layer), ('t', m3, 0, width, layer), ('t', v3, 0, width, layer), ('t', p, 0, width)]
    if q is not None:
        ins.append(('t', q, 0, width))
    outs = [('t', width, F32, layer, nl)] * 4
    aliases = None if prev is None else [(prev[i], i) for i in range(4)]
    return _ew(fn, ins, outs, rows=rows, tr=tr, name=name, aliases=aliases)


def _pack_rows(vec):
    n = vec.shape[0]
    r = -(-n // (8 * LANES)) * 8
    return jnp.pad(vec, (0, r * LANES - n)).reshape(r, LANES)


def kernel(x, c, ctx, c_ctx, ada_w, ada_b, norm_g, w_in, na_rpb, ret_decay_logit, w_proj_na, w_proj_ret, w_out, final_g, loss_target, m_c_ctx, m_ada_w, m_ada_b, m_norm_g, m_w_in, m_na_rpb, m_ret_decay_logit, m_w_proj_na, m_w_proj_ret, m_w_out, m_final_g, v_c_ctx, v_ada_w, v_ada_b, v_norm_g, v_w_in, v_na_rpb, v_ret_decay_logit, v_w_proj_na, v_w_proj_ret, v_w_out, v_final_g):
    depth = w_in.shape[0]
    s_len, d_model = x.shape[1], x.shape[2]
    l_len = ctx.shape[1]
    t_len = s_len + l_len
    na_heads = na_rpb.shape[1]
    ret_heads = ret_decay_logit.shape[2]
    w_na = na_heads * NA_HEAD_DIM
    w_qk = ret_heads * RET_KEY_DIM
    w_v = ret_heads * RET_VAL_DIM
    in_cols = w_in.shape[2] * N_CHIPS
    assert in_cols == 4 * w_na + 2 * w_qk + 2 * w_v + 2 * d_model
    assert x.shape[0] == 1 and s_len % (NA_WIN_ROWS * GRID_W) == 0 and l_len % RET_CHUNK == 0
    off = np.cumsum([0, w_na, w_na, w_na, w_na, w_qk, w_qk, w_v, w_v, d_model, d_model])
    o_naz, o_retq, o_retz, o_gna, o_gret = int(off[3]), int(off[4]), int(off[7]), int(off[8]), int(off[9])
    rows = s_len // GRID_W
    tr = _tile(l_len, 256, 8)
    n0 = s_len // tr
    mod_cols = 3 * d_model
    mod_shard = ada_w.shape[2]

    xi, yi, ci = _mesh_pos()
    me = 4 * xi + 2 * yi + ci
    chip = 2 * xi + yi

    def cast_bf16(w3):
        nl, r, wd = w3.shape
        out = _ew(lambda t: t, [('t', w3.reshape(nl * r, wd), 0, wd)], [('t', wd, BF16)],
                  rows=nl * r, tr=_tile(nl * r, 512, 8), name="cast_w_%dx%d" % (r, wd))[0]
        return out.reshape(nl, r, wd)

    shards = [cast_bf16(w_in), cast_bf16(w_proj_na), cast_bf16(w_proj_ret), cast_bf16(w_out)]
    shard_axes = [2, 2, 1, 1]
    win_f, wpn_f, wpr_f, wout_f = _gather_weights(shards, shard_axes, name="gather_weights")

    c_silu = c[0] * _sigmoid(c[0])
    cc_silu = c_ctx * _sigmoid(c_ctx)
    c_all = _all_gather_small(_pack_rows(c_silu), name="gather_c")[:, :d_model // LANES].reshape(N_DEV, d_model)
    a_rows = jnp.concatenate([c_all, cc_silu[None], jnp.zeros((16 - N_DEV - 1, d_model), F32)], axis=0)
    mod_part = jnp.stack([_mm(a_rows, ada_w, b_lead=l, out_dtype=F32, name="ada_fwd_%d" % l) for l in range(depth)])
    mod_all = _all_gather_small(_pack_rows(mod_part.reshape(-1)), name="gather_mod")
    n_mod = depth * 16 * mod_shard
    mod_all = mod_all.reshape(N_DEV, -1)[:, :n_mod].reshape(N_CHIPS, 2, depth, 16, mod_shard)[:, 0]
    mod_all = jnp.transpose(mod_all, (1, 2, 0, 3)).reshape(depth, 16, mod_cols) + ada_b[:, None, :]
    mod_lat = lax.dynamic_index_in_dim(mod_all, me, axis=1, keepdims=False)
    mod_ctx = mod_all[:, N_DEV]

    c2, s2 = _rope_tables(s_len, l_len)
    log_gamma = jax.nn.log_sigmoid(ret_decay_logit)
    x_all = jnp.concatenate([x[0], ctx[0]], axis=0)

    def grp(lat_vec, ctx_vec):
        return jnp.stack([lat_vec, ctx_vec])[:, None, :]

    saved = []
    for l in range(depth):
        shift, scale, gate = [grp(mod_lat[l, i * d_model:(i + 1) * d_model], mod_ctx[l, i * d_model:(i + 1) * d_model])
                              for i in range(3)]
        gs = norm_g[l][None, None, :] * (1.0 + scale)

        def modnorm(xt, gs_t, sh_t):
            r = lax.rsqrt(jnp.mean(xt * xt, axis=-1, keepdims=True) + NORM_EPS)
            return xt * r * gs_t + sh_t

        h, = _ew(modnorm, [('t', x_all, 0, d_model), ('g', gs), ('g', shift)], [('t', d_model, BF16)],
                 rows=t_len, tr=tr, n0=n0, name="modnorm_%d" % l)
        u = _mm(h, win_f, b_lead=l, name="in_proj_%d" % l)
        bias = _na_bias_table(na_rpb[l], rows)
        o_na = _na_fwd(u, bias, s_len=s_len, heads=na_heads, name="na_fwd_%d" % l)
        o_ret, states = _ret_fwd(u, c2, s2, log_gamma[l], s_len=s_len, heads=ret_heads, q_off=o_retq, name="ret_fwd_%d" % l)

        def act(o1, z1, o2, z2):
            a1 = o1.astype(F32) * _silu_parts(z1.astype(F32))[0]
            sz = _silu_parts(z2.astype(F32))[0]
            outs = []
            for hh in range(ret_heads):
                sl = slice(hh * RET_VAL_DIM, (hh + 1) * RET_VAL_DIM)
                oh = o2[:, sl]
                r = lax.rsqrt(jnp.mean(oh * oh, axis=-1, keepdims=True) + NORM_EPS)
                outs.append(oh * r * sz[:, sl])
            return a1, jnp.concatenate(outs, axis=-1)

        a_na, a_ret = _ew(act, [('t', o_na, 0, w_na), ('t', u, o_naz // w_na, w_na), ('t', o_ret, 0, w_v), ('t', u, o_retz // w_v, w_v)],
                          [('t', w_na, BF16), ('t', w_v, BF16)], rows=t_len, tr=tr, name="act_%d" % l)
        y_na = _mm(a_na, wpn_f, b_lead=l, name="proj_na_%d" % l)
        y_ret = _mm(a_ret, wpr_f, b_lead=l, name="proj_ret_%d" % l)

        def merge(y1, y2, g1, g2):
            return _sigmoid(g1.astype(F32)) * y1.astype(F32) + _sigmoid(g2.astype(F32)) * y2.astype(F32)

        merged, = _ew(merge, [('t', y_na, 0, d_model), ('t', y_ret, 0, d_model), ('t', u, o_gna // d_model, d_model), ('t', u, o_gret // d_model, d_model)],
                      [('t', d_model, BF16)], rows=t_len, tr=tr, name="merge_%d" % l)
        out = _mm(merged, wout_f, b_lead=l, out_dtype=F32, name="out_proj_%d" % l)
        x_new, = _ew(lambda xt, ot, gt: xt + gt * ot, [('t', x_all, 0, d_model), ('t', out, 0, d_model), ('g', gate)],
                     [('t', d_model, F32)], rows=t_len, tr=tr, n0=n0, name="resid_%d" % l)
        saved.append(dict(x=x_all, h=h, u=u, bias=bias, o_na=o_na, o_ret=o_ret, states=states, a_na=a_na, a_ret=a_ret,
                          y_na=y_na, y_ret=y_ret, merged=merged, out=out, gate=gate, gs=gs, scale=scale))
        x_all = x_new

    def final(xt, tt, gt):
        r = lax.rsqrt(jnp.mean(xt * xt, axis=-1, keepdims=True) + NORM_EPS)
        xh = xt * r
        e = xh * gt - tt
        dy = e * (1.0 / d_model)
        dyg = dy * gt
        dx = r * (dyg - xh * jnp.mean(dyg * xh, axis=-1, keepdims=True))
        return dx, _rsum(dy * xh), _rsum(e * e)

    dx_lat, d_final_g, loss_cols = _ew(final, [('t', x_all, 0, d_model), ('t', loss_target[0], 0, d_model), ('g', final_g[None, None, :])],
                                       [('t', d_model, F32), ('r', d_model, 1), ('r', d_model, 1)], rows=s_len, tr=tr, name="final")
    loss_part = (0.5 / d_model) * jnp.sum(loss_cols)
    dx_all = jnp.concatenate([dx_lat, jnp.zeros((l_len, d_model), F32)], axis=0)

    big_axes = [1, 1, 0, 0]
    big_w = [(w_in, m_w_in, v_w_in), (w_proj_na, m_w_proj_na, v_w_proj_na), (w_proj_ret, m_w_proj_ret, v_w_proj_ret), (w_out, m_w_out, v_w_out)]
    big_res = [None] * 4
    small = dict(dmod_lat=[None] * depth, dmod_ctx=[None] * depth, dnorm_g=[None] * depth, drpb=[None] * depth, ddecay=[None] * depth)
    for l in reversed(range(depth)):
        sv = saved[l]

        def resid_bwd(dxt, ot, gt):
            return gt * dxt, _rsum(dxt * ot)

        dout, dgate = _ew(resid_bwd, [('t', dx_all, 0, d_model), ('t', sv['out'], 0, d_model), ('g', sv['gate'])],
                          [('t', d_model, BF16), ('r', d_model, 2)], rows=t_len, tr=tr, n0=n0, name="resid_bwd_%d" % l)
        dmerged = _mm(dout, wout_f, tb=True, b_lead=l, name="out_proj_dx_%d" % l)
        g_wout = _mm(sv['merged'], dout, ta=True, tm=512, tk=t_len, name="out_proj_dw_%d" % l)

        def merge_bwd(dm, y1, y2, g1, g2):
            dm = dm.astype(F32)
            s1, s2_ = _sigmoid(g1.astype(F32)), _sigmoid(g2.astype(F32))
            return dm * s1, dm * s2_, dm * y1.astype(F32) * s1 * (1.0 - s1), dm * y2.astype(F32) * s2_ * (1.0 - s2_)

        u = sv['u']
        dy_na, dy_ret, dg_na, dg_ret = _ew(
            merge_bwd, [('t', dmerged, 0, d_model), ('t', sv['y_na'], 0, d_model), ('t', sv['y_ret'], 0, d_model),
                        ('t', u, o_gna // d_model, d_model), ('t', u, o_gret // d_model, d_model)],
            [('t', d_model, BF16)] * 4, rows=t_len, tr=tr, name="merge_bwd_%d" % l)
        da_na = _mm(dy_na, wpn_f, tb=True, b_lead=l, name="proj_na_dx_%d" % l)
        g_wpn = _mm(sv['a_na'], dy_na, ta=True, tm=512, tk=t_len, name="proj_na_dw_%d" % l)
        da_ret = _mm(dy_ret, wpr_f, tb=True, b_lead=l, name="proj_ret_dx_%d" % l)
        g_wpr = _mm(sv['a_ret'], dy_ret, ta=True, tm=512, tk=t_len, name="proj_ret_dw_%d" % l)

        def act_bwd(da1, o1, z1, da2, o2, z2):
            da1, da2 = da1.astype(F32), da2.astype(F32)
            si1, ds1 = _silu_parts(z1.astype(F32))
            si2, ds2 = _silu_parts(z2.astype(F32))
            do1 = da1 * si1
            dz1 = da1 * o1.astype(F32) * ds1
            dn = da2 * si2
            do2, dz2 = [], []
            for hh in range(ret_heads):
                sl = slice(hh * RET_VAL_DIM, (hh + 1) * RET_VAL_DIM)
                oh = o2[:, sl]
                r = lax.rsqrt(jnp.mean(oh * oh, axis=-1, keepdims=True) + NORM_EPS)
                nh = oh * r
                dz2.append(da2[:, sl] * nh * ds2[:, sl])
                do2.append(r * (dn[:, sl] - nh * jnp.mean(dn[:, sl] * nh, axis=-1, keepdims=True)))
            return do1, dz1, jnp.concatenate(do2, axis=-1), jnp.concatenate(dz2, axis=-1)

        do_na, dz_na, do_ret, dz_ret = _ew(
            act_bwd, [('t', da_na, 0, w_na), ('t', sv['o_na'], 0, w_na), ('t', u, o_naz // w_na, w_na),
                      ('t', da_ret, 0, w_v), ('t', sv['o_ret'], 0, w_v), ('t', u, o_retz // w_v, w_v)],
            [('t', w_na, BF16), ('t', w_na, BF16), ('t', w_v, BF16), ('t', w_v, BF16)], rows=t_len, tr=tr, name="act_bwd_%d" % l)
        dq_na, dk_na, dv_na, dbias = _na_bwd(u, sv['bias'], sv['o_na'], do_na, s_len=s_len, heads=na_heads, name="na_bwd_%d" % l)
        small['drpb'][l] = _rpb_grad(dbias, name="rpb_grad_%d" % l)
        dq_r, dk_r, dv_r, dlg = _ret_bwd(u, c2, s2, log_gamma[l], sv['states'], do_ret, s_len=s_len, heads=ret_heads,
                                         q_off=o_retq, name="ret_bwd_%d" % l)
        small['ddecay'][l] = jnp.transpose(dlg[:, :, 0, 0]) * _sigmoid(-ret_decay_logit[l])
        du = jnp.concatenate([dq_na, dk_na, dv_na, dz_na, dq_r, dk_r, dv_r, dz_ret, dg_na, dg_ret], axis=1)
        dh = _mm(du, win_f, tb=True, b_lead=l, out_dtype=F32, tn=1024, name="in_proj_dx_%d" % l)
        g_win = _mm(sv['h'], du, ta=True, tm=512, tk=t_len, name="in_proj_dw_%d" % l)

        def modnorm_bwd(xt, dht, dxt, gs_t):
            r = lax.rsqrt(jnp.mean(xt * xt, axis=-1, keepdims=True) + NORM_EPS)
            xh = xt * r
            dhg = dht * gs_t
            dx = r * (dhg - xh * jnp.mean(dhg * xh, axis=-1, keepdims=True)) + dxt
            return dx, _rsum(dht), _rsum(dht * xh)

        dx_all, dshift, dgs = _ew(modnorm_bwd, [('t', sv['x'], 0, d_model), ('t', dh, 0, d_model), ('t', dx_all, 0, d_model), ('g', sv['gs'])],
                                  [('t', d_model, F32), ('r', d_model, 2), ('r', d_model, 2)], rows=t_len, tr=tr, n0=n0, name="modnorm_bwd_%d" % l)
        dscale = dgs * norm_g[l][None, None, :]
        small['dnorm_g'][l] = jnp.sum(dgs * (1.0 + sv['scale']), axis=(0, 1))
        dmod = jnp.concatenate([dshift, dscale, dgate], axis=-1)[:, 0]
        small['dmod_lat'][l], small['dmod_ctx'][l] = dmod[0], dmod[1]

        recv = _scatter_grads([g_win, g_wpn, g_wpr, g_wout], big_axes, name="scatter_grads_%d" % l)
        parts = []
        for i, rbuf in enumerate(recv):
            _, pr, pw = rbuf.shape
            p, = _ew(lambda a, b, c_, d: ((a.astype(F32) + b.astype(F32)) + c_.astype(F32)) + d.astype(F32),
                     [('t', rbuf, 0, pw, k) for k in range(N_CHIPS)], [('t', pw, F32)],
                     rows=pr, tr=_rows_per_tile(pr, pw), name="sum_chips_%d_%d" % (i, l))
            parts.append(p)
        others = _swap_with_sibling(parts, name="swap_sibling_%d" % l)
        for i in range(4):
            w3, m3, v3 = big_w[i]
            big_res[i] = _adamw_layer(w3, m3, v3, parts[i], others[i], l, big_res[i], name="adamw_big_%d_%d" % (i, l))

    grad_x = dx_all[:s_len][None]

    drpb = jnp.stack(small['drpb']).reshape(-1)
    ddecay = jnp.stack(small['ddecay']).reshape(-1)
    pieces = [jnp.stack(small['dmod_lat']).reshape(-1), jnp.stack(small['dmod_ctx']).reshape(-1),
              jnp.stack(small['dnorm_g']).reshape(-1), d_final_g.reshape(-1), drpb, ddecay, loss_part[None]]
    sizes = [int(p.shape[0]) for p in pieces]
    pads = [-(-s // LANES) * LANES for s in sizes]
    packed = jnp.concatenate([jnp.pad(p, (0, pd - s)) for p, s, pd in zip(pieces, sizes, pads)])
    gathered = _all_gather_small(_pack_rows(packed), name="gather_small_grads")
    r_small = gathered.shape[1]

    def sum8(*t):
        acc = t[0]
        for other in t[1:]:
            acc = acc + other
        return acc

    total, = _ew(sum8, [('t', gathered, 0, LANES, k) for k in range(N_DEV)], [('t', LANES, F32)], rows=r_small, tr=r_small, name="sum_devices")
    total = total.reshape(-1)
    starts = np.cumsum([0] + pads)
    g_mod_lat_sum, g_mod_ctx, g_norm_g, g_final_g, g_rpb, g_decay, loss = [total[starts[i]:starts[i] + sizes[i]] for i in range(len(pieces))]
    loss = loss[0]
    g_ada_b = (g_mod_lat_sum + g_mod_ctx).reshape(depth, mod_cols)
    g_mod_ctx = g_mod_ctx.reshape(depth, mod_cols)
    dmod_lat_all = gathered.reshape(N_DEV, -1)[:, :depth * mod_cols].reshape(N_DEV, depth, mod_cols)

    ada_res = None
    dcc_part = jnp.zeros((16, d_model), F32)
    for l in reversed(range(depth)):
        lat_cols = lax.dynamic_slice_in_dim(dmod_lat_all[:, l], chip * mod_shard, mod_shard, axis=1)
        ctx_cols = lax.dynamic_slice_in_dim(g_mod_ctx[l], chip * mod_shard, mod_shard, axis=0)
        d_rows = jnp.concatenate([lat_cols, ctx_cols[None], jnp.zeros((16 - N_DEV - 1, mod_shard), F32)], axis=0)
        g_ada = _mm(a_rows, d_rows, ta=True, out_dtype=F32, tm=512, name="ada_dw_%d" % l)
        ada_res = _adamw_layer(ada_w, m_ada_w, v_ada_w, g_ada, None, l, ada_res, name="adamw_ada_%d" % l)
        c_rows = jnp.concatenate([ctx_cols[None], jnp.zeros((15, mod_shard), F32)], axis=0)
        dcc_part = dcc_part + _mm(c_rows, ada_w, tb=True, b_lead=l, out_dtype=F32, name="ada_dc_%d" % l)
    dcc_all = _all_gather_small(_pack_rows(dcc_part[0]), name="gather_dcc")[:, :d_model // LANES].reshape(N_CHIPS, 2, d_model)[:, 0]
    dcc = ((dcc_all[0] + dcc_all[1]) + dcc_all[2]) + dcc_all[3]
    sg = _sigmoid(c_ctx)
    g_c_ctx = dcc * (sg * (1.0 + c_ctx * (1.0 - sg)))

    small_w = [(c_ctx, m_c_ctx, v_c_ctx, g_c_ctx), (ada_b, m_ada_b, v_ada_b, g_ada_b),
               (norm_g, m_norm_g, v_norm_g, g_norm_g), (na_rpb, m_na_rpb, v_na_rpb, g_rpb),
               (ret_decay_logit, m_ret_decay_logit, v_ret_decay_logit, g_decay), (final_g, m_final_g, v_final_g, g_final_g)]
    sw_sizes = [int(np.prod(t[0].shape)) for t in small_w]
    sw_pads = [-(-s // LANES) * LANES for s in sw_sizes]

    def pack(j):
        return _pack_rows(jnp.concatenate([jnp.pad(t[j].reshape(-1), (0, pd - s)) for t, s, pd in zip(small_w, sw_sizes, sw_pads)]))

    pw_, pm_, pv_, pg_ = pack(0), pack(1), pack(2), pack(3)
    sw_out = _ew(lambda w, m, v, g: (g,) + _adamw_math(w, g, m, v),
                 [('t', pw_, 0, LANES), ('t', pm_, 0, LANES), ('t', pv_, 0, LANES), ('t', pg_, 0, LANES)],
                 [('t', LANES, F32)] * 4, rows=pw_.shape[0], tr=pw_.shape[0], name="adamw_small")
    sw_starts = np.cumsum([0] + sw_pads)

    def unpack(arr, i):
        return arr.reshape(-1)[sw_starts[i]:sw_starts[i] + sw_sizes[i]].reshape(small_w[i][0].shape)

    sm = [[unpack(sw_out[j], i) for i in range(len(small_w))] for j in range(4)]
    def ordered(j):
        return [sm[j][0], ada_res[j], sm[j][1], sm[j][2], big_res[0][j], sm[j][3], sm[j][4],
                big_res[1][j], big_res[2][j], big_res[3][j], sm[j][5]]

    return (loss, grad_x, *ordered(0), *ordered(1), *ordered(2), *ordered(3))
```

```python
import functools
import math

import numpy as np
import jax
import jax.numpy as jnp
from jax import lax
from jax.experimental import pallas as pl
from jax.experimental.pallas import tpu as pltpu

GRID_W = 64
NA_HEAD_DIM = 128
NA_WIN_ROWS = 8
NA_WIN_COLS = 16
RET_KEY_DIM = 128
RET_VAL_DIM = 256
RET_CHUNK = 128
ROPE_BASE = 10000.0
NORM_EPS = 1e-6
MASK_VALUE = -1e30
ADAM_LR = 0.001
ADAM_B1 = 0.9
ADAM_B2 = 0.999
ADAM_EPS = 1e-08
ADAM_WD = 0.01
ADAM_STEP = 10

N_CHIPS = 4
N_DEV = 8
LANES = 128
VMEM_LIMIT = 56 * 1024 * 1024
BF16 = jnp.bfloat16
F32 = jnp.float32
MESH = pl.DeviceIdType.MESH
ANY = pl.BlockSpec(memory_space=pl.ANY)


def _tile(dim, pref, align=LANES):
    if dim <= pref:
        return dim
    t = (pref // align) * align
    while t >= align:
        if dim % t == 0:
            return t
        t -= align
    return dim


def _rows_per_tile(rows, width, tile_bytes=1 << 20):
    return _tile(rows, max(8, tile_bytes // (4 * width)), 8)


def _params(sem):
    return pltpu.CompilerParams(dimension_semantics=sem, vmem_limit_bytes=VMEM_LIMIT)


def _sigmoid(x):
    return 1.0 / (1.0 + jnp.exp(-x))


def _dot(a, b, ca, cb):
    return lax.dot_general(a, b, (((ca,), (cb,)), ((), ())), preferred_element_type=F32)


def _mm(a, b, *, ta=False, tb=False, a_lead=None, b_lead=None, out_dtype=BF16, tm=768, tn=512, tk=2048, name):
    ash = a.shape[1:] if a_lead is not None else a.shape
    bsh = b.shape[1:] if b_lead is not None else b.shape
    m, k = (ash[1], ash[0]) if ta else ash
    n, k2 = bsh if tb else (bsh[1], bsh[0])
    assert k == k2, (name, ash, bsh)
    tm, tn, tk = _tile(m, tm), _tile(n, tn), _tile(k, tk)
    nk = k // tk

    def lead(spec_shape, imap, l):
        if l is None:
            return pl.BlockSpec(spec_shape, imap)
        return pl.BlockSpec((None,) + spec_shape, lambda i, j, kk: (l,) + imap(i, j, kk))

    a_spec = lead((tk, tm), lambda i, j, kk: (kk, i), a_lead) if ta else lead((tm, tk), lambda i, j, kk: (i, kk), a_lead)
    b_spec = lead((tn, tk), lambda i, j, kk: (j, kk), b_lead) if tb else lead((tk, tn), lambda i, j, kk: (kk, j), b_lead)
    ca, cb = (0 if ta else 1), (1 if tb else 0)

    def body(a_ref, b_ref, o_ref, *scratch):
        part = _dot(a_ref[...].astype(BF16), b_ref[...].astype(BF16), ca, cb)
        if nk == 1:
            o_ref[...] = part.astype(o_ref.dtype)
            return
        acc_ref, = scratch
        kk = pl.program_id(2)

        @pl.when(kk == 0)
        def _():
            acc_ref[...] = part

        @pl.when(kk > 0)
        def _():
            acc_ref[...] += part

        @pl.when(kk == nk - 1)
        def _():
            o_ref[...] = acc_ref[...].astype(o_ref.dtype)

    return pl.pallas_call(
        body, name=name, grid=(m // tm, n // tn, nk),
        in_specs=[a_spec, b_spec],
        out_specs=pl.BlockSpec((tm, tn), lambda i, j, kk: (i, j)),
        out_shape=jax.ShapeDtypeStruct((m, n), out_dtype),
        scratch_shapes=[] if nk == 1 else [pltpu.VMEM((tm, tn), F32)],
        compiler_params=_params(("parallel", "parallel", "arbitrary")),
    )(a, b)


def _ew(fn, ins, outs, *, rows, tr, name, n0=None, aliases=None):
    assert rows % tr == 0, (name, rows, tr)
    nt = rows // tr

    def grp(i):
        return 0 if n0 is None else jnp.where(i < n0, 0, 1)

    in_specs, args = [], []
    for spec in ins:
        if spec[0] == 't':
            arr, cb, w = spec[1], spec[2], spec[3]
            l = spec[4] if len(spec) > 4 else None
            if l is None:
                in_specs.append(pl.BlockSpec((tr, w), functools.partial(lambda i, cb: (i, cb), cb=cb)))
            else:
                in_specs.append(pl.BlockSpec((None, tr, w), functools.partial(lambda i, cb, l: (l, i, cb), cb=cb, l=l)))
            args.append(arr)
        else:
            arr = spec[1]
            g = arr.shape[0]
            if g == 1:
                in_specs.append(pl.BlockSpec((None, 1, arr.shape[2]), lambda i: (0, 0, 0)))
            else:
                in_specs.append(pl.BlockSpec((None, 1, arr.shape[2]), lambda i: (grp(i), 0, 0)))
            args.append(arr)
    out_specs, out_shapes, is_red = [], [], []
    for spec in outs:
        if spec[0] == 't':
            w, dt = spec[1], spec[2]
            if len(spec) > 3:
                l, nl = spec[3], spec[4]
                out_specs.append(pl.BlockSpec((None, tr, w), functools.partial(lambda i, l: (l, i, 0), l=l)))
                out_shapes.append(jax.ShapeDtypeStruct((nl, rows, w), dt))
            else:
                out_specs.append(pl.BlockSpec((tr, w), lambda i: (i, 0)))
                out_shapes.append(jax.ShapeDtypeStruct((rows, w), dt))
            is_red.append(False)
        else:
            w, g = spec[1], spec[2]
            if g == 1:
                out_specs.append(pl.BlockSpec((None, 1, w), lambda i: (0, 0, 0)))
            else:
                out_specs.append(pl.BlockSpec((None, 1, w), lambda i: (grp(i), 0, 0)))
            out_shapes.append(jax.ShapeDtypeStruct((g, 1, w), F32))
            is_red.append(True)
    n_in = len(ins)
    n_alias = 0 if aliases is None else len(aliases)

    def body(*refs):
        in_refs = refs[:n_in]
        out_refs = refs[n_in + n_alias:]
        res = fn(*[r[...] for r in in_refs])
        if not isinstance(res, (tuple, list)):
            res = (res,)
        i = pl.program_id(0)
        first = (i == 0) if n0 is None else ((i == 0) | (i == n0))
        for o_ref, val, red in zip(out_refs, res, is_red):
            if not red:
                o_ref[...] = val.astype(o_ref.dtype)
            else:
                @pl.when(first)
                def _(o_ref=o_ref, val=val):
                    o_ref[...] = val

                @pl.when(jnp.logical_not(first))
                def _(o_ref=o_ref, val=val):
                    o_ref[...] += val

    io_alias = {}
    if aliases is not None:
        for a_idx, (arr, o_idx) in enumerate(aliases):
            in_specs.append(ANY)
            args.append(arr)
            io_alias[n_in + a_idx] = o_idx
    has_red = any(is_red)
    return pl.pallas_call(
        body, name=name, grid=(nt,), in_specs=in_specs, out_specs=out_specs, out_shape=out_shapes,
        input_output_aliases=io_alias,
        compiler_params=_params(("arbitrary",) if has_red else ("parallel",)),
    )(*args)


def _rsum(v):
    return jnp.sum(v, axis=0, keepdims=True)


def _silu_parts(z):
    sg = _sigmoid(z)
    return z * sg, sg * (1.0 + z * (1.0 - sg))


def _na_bias_table(rpb, rows, *, name):
    kh, kw = NA_WIN_ROWS, NA_WIN_COLS
    assert rows >= kh
    heads = rpb.shape[0]
    e1, e2 = _na_onehots()
    rpb16 = jnp.pad(rpb, ((0, 0), (0, 16 - rpb.shape[1]), (0, LANES - rpb.shape[2])))

    def body(r_ref, e1_ref, e2_ref, o_ref):
        e1b = e1_ref[...].astype(BF16)
        y = sum(_dot(e1b, part, 0, 0) for part in _split3(r_ref[...]))
        e2b = e2_ref[...].astype(BF16)
        o_ref[...] = sum(_dot(part, e2b, 1, 1) for part in _split3(y))

    z = pl.pallas_call(
        body, name=name, grid=(heads,),
        in_specs=[pl.BlockSpec((None, 16, LANES), lambda h: (h, 0, 0)),
                  pl.BlockSpec(e1.shape, lambda h: (0, 0)), pl.BlockSpec(e2.shape, lambda h: (0, 0))],
        out_specs=pl.BlockSpec((None, kh * kh, GRID_W * GRID_W), lambda h: (h, 0, 0)),
        out_shape=jax.ShapeDtypeStruct((heads, kh * kh, GRID_W * GRID_W), F32),
        compiler_params=_params(("parallel",)),
    )(rpb16, e1, e2)
    cidx = np.arange(GRID_W)
    c0 = np.clip(cidx - kw // 2, 0, GRID_W - kw)
    col_in = (cidx[None, :] >= c0[:, None]) & (cidx[None, :] < c0[:, None] + kw)
    bias = z.reshape(heads, kh, kh, GRID_W, GRID_W).transpose(0, 1, 3, 2, 4)
    bias = jnp.where(col_in[None, None, :, None, :], bias, MASK_VALUE)
    return bias.reshape(heads, kh, GRID_W, kh * GRID_W)


def _na_onehots():
    kh, kw = NA_WIN_ROWS, NA_WIN_COLS
    cidx = np.arange(GRID_W)
    dc = cidx[None, :] - cidx[:, None] + (kw - 1)
    e2 = np.zeros((GRID_W * GRID_W, LANES), np.float32)
    ok = (dc >= 0) & (dc <= 2 * kw - 2)
    cq, ck = np.nonzero(ok)
    e2[cq * GRID_W + ck, dc[cq, ck]] = 1.0
    dr = np.arange(kh)[None, :] - np.arange(kh)[:, None] + (kh - 1)
    e1 = np.zeros((16, kh * kh), np.float32)
    dl, kr = np.nonzero(np.ones_like(dr))
    e1[dr[dl, kr], dl * kh + kr] = 1.0
    return jnp.asarray(e1), jnp.asarray(e2)


def _na_row_scores(q, kl, kc, bias, scale):
    s_loc = _dot(q, kl, 1, 1) * scale + bias
    s_ctx = _dot(q, kc, 1, 1) * scale
    m = jnp.maximum(jnp.max(s_loc, axis=-1, keepdims=True), jnp.max(s_ctx, axis=-1, keepdims=True))
    p_loc = jnp.exp(s_loc - m)
    p_ctx = jnp.exp(s_ctx - m)
    den = jnp.sum(p_loc, axis=-1, keepdims=True) + jnp.sum(p_ctx, axis=-1, keepdims=True)
    return p_loc, p_ctx, den


def _na_fwd(u, bias, *, s_len, heads, name):
    t_len = u.shape[0]
    rows = s_len // GRID_W
    nloc = NA_WIN_ROWS * GRID_W
    scale = NA_HEAD_DIM ** -0.5
    hd = NA_HEAD_DIM

    def body(q_ref, k_ref, v_ref, b_ref, o_ref):
        kc = k_ref[s_len:t_len, :]
        vc = v_ref[s_len:t_len, :]

        def row(r, carry):
            r0 = jnp.clip(r - NA_WIN_ROWS // 2, 0, rows - NA_WIN_ROWS)
            qs = pl.multiple_of(r * GRID_W, GRID_W)
            ks = pl.multiple_of(r0 * GRID_W, GRID_W)
            q = q_ref[pl.ds(qs, GRID_W), :]
            kl = k_ref[pl.ds(ks, nloc), :]
            vl = v_ref[pl.ds(ks, nloc), :]
            p_loc, p_ctx, den = _na_row_scores(q, kl, kc, b_ref[r - r0], scale)
            o = _dot(p_loc.astype(BF16), vl, 1, 0) + _dot(p_ctx.astype(BF16), vc, 1, 0)
            o_ref[pl.ds(qs, GRID_W), :] = (o / den).astype(o_ref.dtype)
            return carry

        lax.fori_loop(0, rows, row, 0)
        qc = q_ref[s_len:t_len, :]
        s = _dot(qc, kc, 1, 1) * scale
        p = jnp.exp(s - jnp.max(s, axis=-1, keepdims=True))
        o = _dot(p.astype(BF16), vc, 1, 0) / jnp.sum(p, axis=-1, keepdims=True)
        o_ref[s_len:t_len, :] = o.astype(o_ref.dtype)

    col = lambda off: pl.BlockSpec((t_len, hd), functools.partial(lambda h, off: (0, off + h), off=off))
    return pl.pallas_call(
        body, name=name, grid=(heads,),
        in_specs=[col(0), col(heads), col(2 * heads),
                  pl.BlockSpec((None, NA_WIN_ROWS, GRID_W, nloc), lambda h: (h, 0, 0, 0))],
        out_specs=pl.BlockSpec((t_len, hd), lambda h: (0, h)),
        out_shape=jax.ShapeDtypeStruct((t_len, heads * hd), BF16),
        compiler_params=_params(("parallel",)),
    )(u, u, u, bias)


def _na_bwd(u, bias, o, do, *, s_len, heads, name):
    t_len = u.shape[0]
    rows = s_len // GRID_W
    nloc = NA_WIN_ROWS * GRID_W
    scale = NA_HEAD_DIM ** -0.5
    hd = NA_HEAD_DIM

    def body(q_ref, k_ref, v_ref, b_ref, o_ref, do_ref, dq_ref, dk_ref, dv_ref, db_ref, dk_acc, dv_acc):
        kc = k_ref[s_len:t_len, :]
        vc = v_ref[s_len:t_len, :]
        dk_acc[...] = jnp.zeros_like(dk_acc)
        dv_acc[...] = jnp.zeros_like(dv_acc)
        db_ref[...] = jnp.zeros_like(db_ref)

        def row(r, carry):
            r0 = jnp.clip(r - NA_WIN_ROWS // 2, 0, rows - NA_WIN_ROWS)
            dl = r - r0
            qs = pl.multiple_of(r * GRID_W, GRID_W)
            ks = pl.multiple_of(r0 * GRID_W, GRID_W)
            q = q_ref[pl.ds(qs, GRID_W), :]
            kl = k_ref[pl.ds(ks, nloc), :]
            vl = v_ref[pl.ds(ks, nloc), :]
            dout = do_ref[pl.ds(qs, GRID_W), :]
            out = o_ref[pl.ds(qs, GRID_W), :]
            p_loc, p_ctx, den = _na_row_scores(q, kl, kc, b_ref[dl], scale)
            inv = 1.0 / den
            p_loc = p_loc * inv
            p_ctx = p_ctx * inv
            dlt = jnp.sum(dout.astype(F32) * out.astype(F32), axis=-1, keepdims=True)
            ds_loc = p_loc * (_dot(dout, vl, 1, 1) - dlt)
            ds_ctx = p_ctx * (_dot(dout, vc, 1, 1) - dlt)
            db_ref[dl] += ds_loc
            ds_loc_b = ds_loc.astype(BF16)
            ds_ctx_b = ds_ctx.astype(BF16)
            dq = (_dot(ds_loc_b, kl, 1, 0) + _dot(ds_ctx_b, kc, 1, 0)) * scale
            dq_ref[pl.ds(qs, GRID_W), :] = dq.astype(dq_ref.dtype)
            dk_acc[pl.ds(ks, nloc), :] += _dot(ds_loc_b, q, 0, 0) * scale
            dv_acc[pl.ds(ks, nloc), :] += _dot(p_loc.astype(BF16), dout, 0, 0)
            dk_acc[s_len:t_len, :] += _dot(ds_ctx_b, q, 0, 0) * scale
            dv_acc[s_len:t_len, :] += _dot(p_ctx.astype(BF16), dout, 0, 0)
            return carry

        lax.fori_loop(0, rows, row, 0)
        qc = q_ref[s_len:t_len, :]
        dout = do_ref[s_len:t_len, :]
        out = o_ref[s_len:t_len, :]
        s = _dot(qc, kc, 1, 1) * scale
        p = jnp.exp(s - jnp.max(s, axis=-1, keepdims=True))
        p = p / jnp.sum(p, axis=-1, keepdims=True)
        dlt = jnp.sum(dout.astype(F32) * out.astype(F32), axis=-1, keepdims=True)
        ds = (p * (_dot(dout, vc, 1, 1) - dlt)).astype(BF16)
        dq_ref[s_len:t_len, :] = (_dot(ds, kc, 1, 0) * scale).astype(dq_ref.dtype)
        dk_acc[s_len:t_len, :] += _dot(ds, qc, 0, 0) * scale
        dv_acc[s_len:t_len, :] += _dot(p.astype(BF16), dout, 0, 0)
        dk_ref[...] = dk_acc[...].astype(dk_ref.dtype)
        dv_ref[...] = dv_acc[...].astype(dv_ref.dtype)

    col = lambda off: pl.BlockSpec((t_len, hd), functools.partial(lambda h, off: (0, off + h), off=off))
    tbl = pl.BlockSpec((None, NA_WIN_ROWS, GRID_W, nloc), lambda h: (h, 0, 0, 0))
    tok = jax.ShapeDtypeStruct((t_len, heads * hd), BF16)
    return pl.pallas_call(
        body, name=name, grid=(heads,),
        in_specs=[col(0), col(heads), col(2 * heads), tbl, col(0), col(0)],
        out_specs=[col(0), col(0), col(0), tbl],
        out_shape=[tok, tok, tok, jax.ShapeDtypeStruct(bias.shape, F32)],
        scratch_shapes=[pltpu.VMEM((t_len, hd), F32), pltpu.VMEM((t_len, hd), F32)],
        compiler_params=_params(("parallel",)),
    )(u, u, u, bias, o, do)


def _split3(x):
    hi = x.astype(BF16)
    r1 = x - hi.astype(F32)
    mid = r1.astype(BF16)
    lo = (r1 - mid.astype(F32)).astype(BF16)
    return hi, mid, lo


def _rpb_grad(dbias, *, name):
    heads = dbias.shape[0]
    kh = NA_WIN_ROWS
    e1, e2 = _na_onehots()
    x = dbias.reshape(heads, kh, GRID_W, kh, GRID_W).transpose(0, 1, 3, 2, 4).reshape(heads, kh * kh, GRID_W * GRID_W)

    def body(x_ref, e1_ref, e2_ref, o_ref):
        e2b = e2_ref[...].astype(BF16)
        y = sum(_dot(part, e2b, 1, 0) for part in _split3(x_ref[...]))
        e1b = e1_ref[...].astype(BF16)
        o_ref[...] = sum(_dot(e1b, part, 1, 0) for part in _split3(y))

    out = pl.pallas_call(
        body, name=name, grid=(heads,),
        in_specs=[pl.BlockSpec((None, kh * kh, GRID_W * GRID_W), lambda h: (h, 0, 0)),
                  pl.BlockSpec(e1.shape, lambda h: (0, 0)), pl.BlockSpec(e2.shape, lambda h: (0, 0))],
        out_specs=pl.BlockSpec((None, 16, LANES), lambda h: (h, 0, 0)),
        out_shape=jax.ShapeDtypeStruct((heads, 16, LANES), F32),
        compiler_params=_params(("parallel",)),
    )(x, e1, e2)
    return out[:, :2 * kh - 1, :2 * NA_WIN_COLS - 1]


def _rope_tables(s_len, l_len):
    nf = RET_KEY_DIM // 4
    t = np.arange(s_len)
    row = (t // GRID_W).astype(np.float32)
    colp = (t % GRID_W).astype(np.float32)
    inv_freq = jnp.asarray(ROPE_BASE, F32) ** (-jnp.arange(nf, dtype=F32) / nf)
    ang = jnp.concatenate([jnp.asarray(row)[:, None] * inv_freq, jnp.asarray(colp)[:, None] * inv_freq], axis=-1)
    cos, sin = jnp.cos(ang), jnp.sin(ang)
    c2 = jnp.concatenate([cos, cos], axis=-1)
    s2 = jnp.concatenate([-sin, sin], axis=-1)
    c2 = jnp.concatenate([c2, jnp.ones((l_len, RET_KEY_DIM), F32)], axis=0)
    s2 = jnp.concatenate([s2, jnp.zeros((l_len, RET_KEY_DIM), F32)], axis=0)
    return c2, s2


def _rope(x, c2, s2):
    return x * c2 + pltpu.roll(x, RET_KEY_DIM // 2, 1) * s2


def _rope_t(d, c2, s2):
    return d * c2 + pltpu.roll(d * s2, RET_KEY_DIM // 2, 1)


def _ret_decays(lg, direction):
    cs = RET_CHUNK
    i_col = lax.broadcasted_iota(jnp.int32, (cs, 1), 0)
    p_col = jnp.where(direction == 0, i_col, cs - 1 - i_col).astype(F32)
    pi = lax.broadcasted_iota(jnp.int32, (cs, cs), 0)
    pj = lax.broadcasted_iota(jnp.int32, (cs, cs), 1)
    diff = jnp.where(direction == 0, pi - pj, pj - pi).astype(F32)
    dm = jnp.where(diff >= 0, jnp.exp(jnp.maximum(diff, 0.0) * lg), 0.0)
    qdec = jnp.exp((p_col + 1.0) * lg)
    kdec = jnp.exp((cs - 1.0 - p_col) * lg)
    cd = jnp.exp(jnp.full((1, 1), cs, F32) * lg)
    return p_col, dm, qdec, kdec, cd


def _ret_chunk_index(t, direction, n_chunks, lat_chunks):
    return jnp.where(direction == 0, lax.rem(t + lat_chunks, n_chunks), n_chunks - 1 - t)


def _ret_fwd(u, c2, s2, lg, *, s_len, heads, q_off, name):
    t_len = u.shape[0]
    cs, dk, dv = RET_CHUNK, RET_KEY_DIM, RET_VAL_DIM
    n_chunks, lat_chunks = t_len // cs, s_len // cs
    k_scale = dk ** -0.5
    qb, kb, vb = q_off // dk, q_off // dk + heads, (q_off + 2 * heads * dk) // dv

    def body(lg_ref, q_ref, k_ref, v_ref, c_ref, s_ref, o_ref, st_ref, state):
        h, d = pl.program_id(0), pl.program_id(1)
        _, dm, qdec, kdec, cd = _ret_decays(lg_ref[d, h], d)
        state[...] = jnp.zeros_like(state)

        def step(t, carry):
            c = _ret_chunk_index(t, d, n_chunks, lat_chunks)
            r = pl.ds(pl.multiple_of(c * cs, cs), cs)
            cc, ss = c_ref[r, :], s_ref[r, :]
            qc = _rope(q_ref[r, :].astype(F32), cc, ss)
            kc = _rope(k_ref[r, :].astype(F32), cc, ss) * k_scale
            vc = v_ref[r, :]
            st = state[...]
            st_ref[t] = st
            a = _dot(qc.astype(BF16), kc.astype(BF16), 1, 1) * dm
            oc = _dot(a.astype(BF16), vc, 1, 0) + _dot((qc * qdec).astype(BF16), st.astype(BF16), 1, 0)
            state[...] = st * cd + _dot((kc * kdec).astype(BF16), vc, 0, 0)

            @pl.when(d == 0)
            def _():
                o_ref[r, :] = oc

            @pl.when(d == 1)
            def _():
                o_ref[r, :] += oc

            return carry

        lax.fori_loop(0, n_chunks, step, 0)

    return pl.pallas_call(
        body, name=name, grid=(heads, 2),
        in_specs=[pl.BlockSpec(memory_space=pltpu.SMEM),
                  pl.BlockSpec((t_len, dk), lambda h, d: (0, qb + h)),
                  pl.BlockSpec((t_len, dk), lambda h, d: (0, kb + h)),
                  pl.BlockSpec((t_len, dv), lambda h, d: (0, vb + h)),
                  pl.BlockSpec((t_len, dk), lambda h, d: (0, 0)),
                  pl.BlockSpec((t_len, dk), lambda h, d: (0, 0))],
        out_specs=[pl.BlockSpec((t_len, dv), lambda h, d: (0, h)),
                   pl.BlockSpec((None, None, n_chunks, dk, dv), lambda h, d: (h, d, 0, 0, 0))],
        out_shape=[jax.ShapeDtypeStruct((t_len, heads * dv), F32),
                   jax.ShapeDtypeStruct((heads, 2, n_chunks, dk, dv), F32)],
        scratch_shapes=[pltpu.VMEM((dk, dv), F32)],
        compiler_params=_params(("parallel", "arbitrary")),
    )(lg, u, u, u, c2, s2)


def _ret_bwd(u, c2, s2, lg, states, do, *, s_len, heads, q_off, name):
    t_len = u.shape[0]
    cs, dk, dv = RET_CHUNK, RET_KEY_DIM, RET_VAL_DIM
    n_chunks, lat_chunks = t_len // cs, s_len // cs
    k_scale = dk ** -0.5
    qb, kb, vb = q_off // dk, q_off // dk + heads, (q_off + 2 * heads * dk) // dv

    def body(lg_ref, q_ref, k_ref, v_ref, c_ref, s_ref, st_ref, do_ref, dq_ref, dk_ref, dv_ref, dlg_ref, dstate, acc):
        h, d = pl.program_id(0), pl.program_id(1)
        p_col, dm, qdec, kdec, cd = _ret_decays(lg_ref[d, h], d)
        dstate[...] = jnp.zeros_like(dstate)
        acc[...] = jnp.zeros_like(acc)

        def step(i, carry):
            t = n_chunks - 1 - i
            c = _ret_chunk_index(t, d, n_chunks, lat_chunks)
            r = pl.ds(pl.multiple_of(c * cs, cs), cs)
            cc, ss = c_ref[r, :], s_ref[r, :]
            qc = _rope(q_ref[r, :].astype(F32), cc, ss)
            kc = _rope(k_ref[r, :].astype(F32), cc, ss) * k_scale
            vc = v_ref[r, :]
            doc = do_ref[r, :].astype(BF16)
            st = st_ref[t]
            dst = dstate[...]
            qb16, kb16 = qc.astype(BF16), kc.astype(BF16)
            a = _dot(qb16, kb16, 1, 1) * dm
            dam = (_dot(doc, vc, 1, 1) * dm).astype(BF16)
            dq_i = _dot(dam, kb16, 1, 0)
            dk_i = _dot(dam, qb16, 0, 0)
            dq_c = _dot(doc, st.astype(BF16), 1, 1) * qdec
            dst16 = dst.astype(BF16)
            dvc = _dot(a.astype(BF16), doc, 0, 0) + _dot((kc * kdec).astype(BF16), dst16, 1, 0)
            dk_s = _dot(vc, dst16, 1, 1) * kdec
            g = (jnp.sum(qc * (p_col * dq_i + (p_col + 1.0) * dq_c), axis=-1, keepdims=True)
                 + jnp.sum(kc * ((cs - 1.0 - p_col) * dk_s - p_col * dk_i), axis=-1, keepdims=True))
            g = jnp.sum(g, axis=0, keepdims=True) + cs * cd * jnp.sum(jnp.sum(dst * st, axis=-1, keepdims=True), axis=0, keepdims=True)
            acc[...] += jnp.broadcast_to(g, acc.shape)
            dstate[...] = dst * cd + _dot((qc * qdec).astype(BF16), doc, 0, 0)
            dq = _rope_t(dq_i + dq_c, cc, ss)
            dkk = _rope_t((dk_i + dk_s) * k_scale, cc, ss)

            @pl.when(d == 0)
            def _():
                dq_ref[r, :] = dq.astype(dq_ref.dtype)
                dk_ref[r, :] = dkk.astype(dk_ref.dtype)
                dv_ref[r, :] = dvc.astype(dv_ref.dtype)

            @pl.when(d == 1)
            def _():
                dq_ref[r, :] = (dq_ref[r, :].astype(F32) + dq).astype(dq_ref.dtype)
                dk_ref[r, :] = (dk_ref[r, :].astype(F32) + dkk).astype(dk_ref.dtype)
                dv_ref[r, :] = (dv_ref[r, :].astype(F32) + dvc).astype(dv_ref.dtype)

            return carry

        lax.fori_loop(0, n_chunks, step, 0)
        dlg_ref[...] = acc[...]

    return pl.pallas_call(
        body, name=name, grid=(heads, 2),
        in_specs=[pl.BlockSpec(memory_space=pltpu.SMEM),
                  pl.BlockSpec((t_len, dk), lambda h, d: (0, qb + h)),
                  pl.BlockSpec((t_len, dk), lambda h, d: (0, kb + h)),
                  pl.BlockSpec((t_len, dv), lambda h, d: (0, vb + h)),
                  pl.BlockSpec((t_len, dk), lambda h, d: (0, 0)),
                  pl.BlockSpec((t_len, dk), lambda h, d: (0, 0)),
                  pl.BlockSpec((None, None, n_chunks, dk, dv), lambda h, d: (h, d, 0, 0, 0)),
                  pl.BlockSpec((t_len, dv), lambda h, d: (0, h))],
        out_specs=[pl.BlockSpec((t_len, dk), lambda h, d: (0, h)),
                   pl.BlockSpec((t_len, dk), lambda h, d: (0, h)),
                   pl.BlockSpec((t_len, dv), lambda h, d: (0, h)),
                   pl.BlockSpec((None, None, 8, LANES), lambda h, d: (h, d, 0, 0))],
        out_shape=[jax.ShapeDtypeStruct((t_len, heads * dk), BF16),
                   jax.ShapeDtypeStruct((t_len, heads * dk), BF16),
                   jax.ShapeDtypeStruct((t_len, heads * dv), BF16),
                   jax.ShapeDtypeStruct((heads, 2, 8, LANES), F32)],
        scratch_shapes=[pltpu.VMEM((dk, dv), F32), pltpu.VMEM((8, LANES), F32)],
        compiler_params=_params(("parallel", "arbitrary")),
    )(lg, u, u, u, c2, s2, states, do)


def _mesh_pos():
    return lax.axis_index("x"), lax.axis_index("y"), lax.axis_index("c")


def _all_gather_small(buf, *, name):
    r = buf.shape[0]

    def body(x_ref, o_ref, send_sems, recv_sems, local_sem):
        x, y, c = _mesh_pos()
        me = 4 * x + 2 * y + c
        mine = pltpu.make_async_copy(x_ref, o_ref.at[me], local_sem)
        mine.start()
        copies = []
        for k in range(1, N_DEV):
            px, py, pc = x ^ ((k >> 2) & 1), y ^ ((k >> 1) & 1), c ^ (k & 1)
            cp = pltpu.make_async_remote_copy(
                src_ref=x_ref, dst_ref=o_ref.at[me], send_sem=send_sems.at[k - 1], recv_sem=recv_sems.at[k - 1],
                device_id=(px, py, pc), device_id_type=MESH)
            cp.start()
            copies.append((cp, 4 * px + 2 * py + pc))
        for k, (cp, peer) in enumerate(copies):
            pltpu.make_async_remote_copy(
                src_ref=x_ref, dst_ref=o_ref.at[peer], send_sem=send_sems.at[k], recv_sem=recv_sems.at[k],
                device_id=(x, y, c), device_id_type=MESH).wait_recv()
        for cp, _ in copies:
            cp.wait_send()
        mine.wait()

    return pl.pallas_call(
        body, name=name,
        in_specs=[pl.BlockSpec(memory_space=pltpu.VMEM)],
        out_specs=pl.BlockSpec(memory_space=pltpu.VMEM),
        out_shape=jax.ShapeDtypeStruct((N_DEV, r, LANES), F32),
        scratch_shapes=[pltpu.SemaphoreType.DMA((N_DEV - 1,)), pltpu.SemaphoreType.DMA((N_DEV - 1,)),
                        pltpu.SemaphoreType.DMA],
        compiler_params=pltpu.CompilerParams(vmem_limit_bytes=VMEM_LIMIT),
    )(buf)


def _shard_slice(ref, axis, j, width):
    idx = [slice(None)] * len(ref.shape)
    idx[axis] = pl.ds(pl.multiple_of(j * width, width), width)
    return ref.at[tuple(idx)]


def _cut(ref, shard_axis, *, chip=None, half=None, lead=None):
    shape = ref.shape[1:] if lead is not None else ref.shape
    idx = [slice(None), slice(None)]
    if chip is not None:
        w = shape[shard_axis] // N_CHIPS
        idx[shard_axis] = pl.ds(pl.multiple_of(chip * w, w), w)
    if half is not None:
        hw = shape[1 - shard_axis] // 2
        idx[1 - shard_axis] = pl.ds(pl.multiple_of(half * hw, hw), hw)
    if lead is not None:
        idx = [lead] + idx
    return ref.at[tuple(idx)]


def _wait_recv(ref, send_sem, recv_sem):
    pltpu.make_async_remote_copy(src_ref=ref, dst_ref=ref, send_sem=send_sem, recv_sem=recv_sem,
                                 device_id=_mesh_pos(), device_id_type=MESH).wait_recv()


def _gather_weights(shards, axes, layer, *, name):
    n = len(shards)
    full_shapes = []
    for s, ax in zip(shards, axes):
        shp = list(s.shape[1:])
        shp[ax] *= N_CHIPS
        full_shapes.append(jax.ShapeDtypeStruct(tuple(shp), s.dtype))

    def body(*refs):
        ins, outs = refs[:n], refs[n:2 * n]
        ici_send, ici_recv, d2d_send, d2d_recv, local_sems = refs[2 * n:]
        x, y, c = _mesh_pos()
        chip = 2 * x + y
        pending = []
        for i in range(n):
            mine = pltpu.make_async_copy(ins[i].at[layer], _cut(outs[i], axes[i], chip=chip), local_sems.at[i])
            mine.start()
            pending.append(mine)
        for i in range(n):
            for k in range(1, N_CHIPS):
                cp = pltpu.make_async_remote_copy(
                    src_ref=_cut(ins[i], axes[i], half=c, lead=layer), dst_ref=_cut(outs[i], axes[i], chip=chip, half=c),
                    send_sem=ici_send.at[i, k - 1], recv_sem=ici_recv.at[i, k - 1],
                    device_id=(x ^ (k >> 1), y ^ (k & 1), c), device_id_type=MESH)
                cp.start()
                pending.append(cp)
        for i in range(n):
            for k in range(1, N_CHIPS):
                peer_chip = 2 * (x ^ (k >> 1)) + (y ^ (k & 1))
                landed = _cut(outs[i], axes[i], chip=peer_chip, half=c)
                _wait_recv(landed, ici_send.at[i, k - 1], ici_recv.at[i, k - 1])
                cp = pltpu.make_async_remote_copy(
                    src_ref=landed, dst_ref=landed, send_sem=d2d_send.at[i, k - 1], recv_sem=d2d_recv.at[i, k - 1],
                    device_id=(x, y, 1 - c), device_id_type=MESH)
                cp.start()
                pending.append(cp)
        for i in range(n):
            for k in range(1, N_CHIPS):
                peer_chip = 2 * (x ^ (k >> 1)) + (y ^ (k & 1))
                _wait_recv(_cut(outs[i], axes[i], chip=peer_chip, half=1 - c), d2d_send.at[i, k - 1], d2d_recv.at[i, k - 1])
        for cp in pending[n:]:
            cp.wait_send()
        for cp in pending[:n]:
            cp.wait()

    pairs = pltpu.SemaphoreType.DMA((n, N_CHIPS - 1))
    return pl.pallas_call(
        body, name=name, in_specs=[ANY] * n, out_specs=[ANY] * n, out_shape=full_shapes,
        scratch_shapes=[pairs, pairs, pairs, pairs, pltpu.SemaphoreType.DMA((n,))],
    )(*shards)


def _pair_exchange(grads, axes, *, name):
    n = len(grads)
    half_shapes = []
    for g, ax in zip(grads, axes):
        shp = list(g.shape)
        shp[1 - ax] //= 2
        half_shapes.append(jax.ShapeDtypeStruct(tuple(shp), g.dtype))

    def body(*refs):
        ins, mine, theirs = refs[:n], refs[n:2 * n], refs[2 * n:3 * n]
        send_sems, recv_sems, local_sems = refs[3 * n:]
        x, y, c = _mesh_pos()
        local, sends = [], []
        for i in range(n):
            cp = pltpu.make_async_copy(_cut(ins[i], axes[i], half=c), mine[i], local_sems.at[i])
            cp.start()
            local.append(cp)
            cp = pltpu.make_async_remote_copy(
                src_ref=_cut(ins[i], axes[i], half=1 - c), dst_ref=theirs[i], send_sem=send_sems.at[i], recv_sem=recv_sems.at[i],
                device_id=(x, y, 1 - c), device_id_type=MESH)
            cp.start()
            sends.append(cp)
        for cp in sends:
            cp.wait()
        for cp in local:
            cp.wait()

    sems = pltpu.SemaphoreType.DMA((n,))
    res = pl.pallas_call(
        body, name=name, in_specs=[ANY] * n, out_specs=[ANY] * (2 * n), out_shape=half_shapes + half_shapes,
        scratch_shapes=[sems, sems, sems],
    )(*grads)
    return res[:n], res[n:]


def _share_halves(parts, axes, *, name):
    n = len(parts)
    full_shapes = []
    for p, ax in zip(parts, axes):
        shp = list(p.shape)
        shp[1 - ax] *= 2
        full_shapes.append(jax.ShapeDtypeStruct(tuple(shp), p.dtype))

    def body(*refs):
        ins, outs = refs[:n], refs[n:2 * n]
        send_sems, recv_sems, local_sems = refs[2 * n:]
        x, y, c = _mesh_pos()
        local, sends = [], []
        for i in range(n):
            cp = pltpu.make_async_copy(ins[i], _cut(outs[i], axes[i], half=c), local_sems.at[i])
            cp.start()
            local.append(cp)
            cp = pltpu.make_async_remote_copy(
                src_ref=ins[i], dst_ref=_cut(outs[i], axes[i], half=c), send_sem=send_sems.at[i], recv_sem=recv_sems.at[i],
                device_id=(x, y, 1 - c), device_id_type=MESH)
            cp.start()
            sends.append(cp)
        for i in range(n):
            _wait_recv(_cut(outs[i], axes[i], half=1 - c), send_sems.at[i], recv_sems.at[i])
        for cp in sends:
            cp.wait_send()
        for cp in local:
            cp.wait()

    sems = pltpu.SemaphoreType.DMA((n,))
    return pl.pallas_call(
        body, name=name, in_specs=[ANY] * n, out_specs=[ANY] * n, out_shape=full_shapes,
        scratch_shapes=[sems, sems, sems],
    )(*parts)


def _scatter_grads(grads, axes, *, name):
    n = len(grads)
    out_shapes, widths = [], []
    for g, ax in zip(grads, axes):
        shp = list(g.shape)
        shp[ax] //= N_CHIPS
        widths.append(shp[ax])
        out_shapes.append(jax.ShapeDtypeStruct((N_CHIPS,) + tuple(shp), g.dtype))

    def body(*refs):
        ins, outs = refs[:n], refs[n:2 * n]
        send_sems, recv_sems, local_sems = refs[2 * n:]
        x, y, c = _mesh_pos()
        chip = 2 * x + y
        started = []
        for i in range(n):
            mine = pltpu.make_async_copy(_shard_slice(ins[i], axes[i], chip, widths[i]), outs[i].at[0], local_sems.at[i])
            mine.start()
            started.append(mine)
        sends = []
        for i in range(n):
            for k in range(1, N_CHIPS):
                px, py = x ^ (k >> 1), y ^ (k & 1)
                cp = pltpu.make_async_remote_copy(
                    src_ref=_shard_slice(ins[i], axes[i], 2 * px + py, widths[i]), dst_ref=outs[i].at[k],
                    send_sem=send_sems.at[i, k - 1], recv_sem=recv_sems.at[i, k - 1],
                    device_id=(px, py, c), device_id_type=MESH)
                cp.start()
                sends.append(cp)
        for i in range(n):
            for k in range(1, N_CHIPS):
                pltpu.make_async_remote_copy(
                    src_ref=outs[i].at[k], dst_ref=outs[i].at[k],
                    send_sem=send_sems.at[i, k - 1], recv_sem=recv_sems.at[i, k - 1],
                    device_id=(x, y, c), device_id_type=MESH).wait_recv()
        for cp in sends:
            cp.wait_send()
        for cp in started:
            cp.wait()

    return pl.pallas_call(
        body, name=name, in_specs=[ANY] * n, out_specs=[ANY] * n, out_shape=out_shapes,
        scratch_shapes=[pltpu.SemaphoreType.DMA((n, N_CHIPS - 1)), pltpu.SemaphoreType.DMA((n, N_CHIPS - 1)),
                        pltpu.SemaphoreType.DMA((n,))],
    )(*grads)


def _adamw_math(w, g, m, v):
    m = ADAM_B1 * m + (1.0 - ADAM_B1) * g
    v = ADAM_B2 * v + (1.0 - ADAM_B2) * (g * g)
    m_hat = m / (1.0 - ADAM_B1 ** ADAM_STEP)
    v_hat = v / (1.0 - ADAM_B2 ** ADAM_STEP)
    delta = -ADAM_LR * (m_hat / (jnp.sqrt(v_hat) + ADAM_EPS) + ADAM_WD * w)
    return delta, m, v


def _adamw_layer(w3, m3, v3, p, q, layer, prev, *, name):
    nl, rows, width = w3.shape
    tr = _rows_per_tile(rows, width)

    def fn(*t):
        if q is None:
            w, m, v, g = t
        else:
            w, m, v, g, g2 = t
            g = g + g2
        delta, m, v = _adamw_math(w, g, m, v)
        return g, delta, m, v

    ins = [('t', w3, 0, width, layer), ('t', m3, 0, width, layer), ('t', v3, 0, width, layer), ('t', p, 0, width)]
    if q is not None:
        ins.append(('t', q, 0, width))
    outs = [('t', width, F32, layer, nl)] * 4
    aliases = None if prev is None else [(prev[i], i) for i in range(4)]
    return _ew(fn, ins, outs, rows=rows, tr=tr, name=name, aliases=aliases)


def _pack_rows(vec):
    n = vec.shape[0]
    r = -(-n // (8 * LANES)) * 8
    return jnp.pad(vec, (0, r * LANES - n)).reshape(r, LANES)


def kernel(x, c, ctx, c_ctx, ada_w, ada_b, norm_g, w_in, na_rpb, ret_decay_logit, w_proj_na, w_proj_ret, w_out, final_g, loss_target, m_c_ctx, m_ada_w, m_ada_b, m_norm_g, m_w_in, m_na_rpb, m_ret_decay_logit, m_w_proj_na, m_w_proj_ret, m_w_out, m_final_g, v_c_ctx, v_ada_w, v_ada_b, v_norm_g, v_w_in, v_na_rpb, v_ret_decay_logit, v_w_proj_na, v_w_proj_ret, v_w_out, v_final_g):
    depth = w_in.shape[0]
    s_len, d_model = x.shape[1], x.shape[2]
    l_len = ctx.shape[1]
    t_len = s_len + l_len
    na_heads = na_rpb.shape[1]
    ret_heads = ret_decay_logit.shape[2]
    w_na = na_heads * NA_HEAD_DIM
    w_qk = ret_heads * RET_KEY_DIM
    w_v = ret_heads * RET_VAL_DIM
    in_cols = w_in.shape[2] * N_CHIPS
    assert in_cols == 4 * w_na + 2 * w_qk + 2 * w_v + 2 * d_model
    assert x.shape[0] == 1 and s_len % (NA_WIN_ROWS * GRID_W) == 0 and l_len % RET_CHUNK == 0
    off = np.cumsum([0, w_na, w_na, w_na, w_na, w_qk, w_qk, w_v, w_v, d_model, d_model])
    o_naz, o_retq, o_retz, o_gna, o_gret = int(off[3]), int(off[4]), int(off[7]), int(off[8]), int(off[9])
    rows = s_len // GRID_W
    tr = _tile(l_len, 256, 8)
    n0 = s_len // tr
    mod_cols = 3 * d_model
    mod_shard = ada_w.shape[2]

    xi, yi, ci = _mesh_pos()
    me = 4 * xi + 2 * yi + ci
    chip = 2 * xi + yi

    def cast_bf16(w3):
        nl, r, wd = w3.shape
        out = _ew(lambda t: t, [('t', w3.reshape(nl * r, wd), 0, wd)], [('t', wd, BF16)],
                  rows=nl * r, tr=_tile(nl * r, 512, 8), name="cast_w_%dx%d" % (r, wd))[0]
        return out.reshape(nl, r, wd)

    shards = [cast_bf16(w_in), cast_bf16(w_proj_na), cast_bf16(w_proj_ret), cast_bf16(w_out)]
    big_axes = [1, 1, 0, 0]
    full_w = [_gather_weights(shards, big_axes, l, name="gather_weights_%d" % l) for l in range(depth)]

    c_silu = c[0] * _sigmoid(c[0])
    cc_silu = c_ctx * _sigmoid(c_ctx)
    c_all = _all_gather_small(_pack_rows(c_silu), name="gather_c")[:, :d_model // LANES].reshape(N_DEV, d_model)
    a_rows = jnp.concatenate([c_all, cc_silu[None], jnp.zeros((16 - N_DEV - 1, d_model), F32)], axis=0)
    mod_part = jnp.stack([_mm(a_rows, ada_w, b_lead=l, out_dtype=F32, name="ada_fwd_%d" % l) for l in range(depth)])
    mod_all = _all_gather_small(_pack_rows(mod_part.reshape(-1)), name="gather_mod")
    n_mod = depth * 16 * mod_shard
    mod_all = mod_all.reshape(N_DEV, -1)[:, :n_mod].reshape(N_CHIPS, 2, depth, 16, mod_shard)[:, 0]
    mod_all = jnp.transpose(mod_all, (1, 2, 0, 3)).reshape(depth, 16, mod_cols) + ada_b[:, None, :]
    mod_lat = lax.dynamic_index_in_dim(mod_all, me, axis=1, keepdims=False)
    mod_ctx = mod_all[:, N_DEV]

    c2, s2 = _rope_tables(s_len, l_len)
    log_gamma = jax.nn.log_sigmoid(ret_decay_logit)
    x_all = jnp.concatenate([x[0], ctx[0]], axis=0)

    def grp(lat_vec, ctx_vec):
        return jnp.stack([lat_vec, ctx_vec])[:, None, :]

    saved = []
    for l in range(depth):
        shift, scale, gate = [grp(mod_lat[l, i * d_model:(i + 1) * d_model], mod_ctx[l, i * d_model:(i + 1) * d_model])
                              for i in range(3)]
        gs = norm_g[l][None, None, :] * (1.0 + scale)

        def modnorm(xt, gs_t, sh_t):
            r = lax.rsqrt(jnp.mean(xt * xt, axis=-1, keepdims=True) + NORM_EPS)
            return xt * r * gs_t + sh_t

        h, = _ew(modnorm, [('t', x_all, 0, d_model), ('g', gs), ('g', shift)], [('t', d_model, BF16)],
                 rows=t_len, tr=tr, n0=n0, name="modnorm_%d" % l)
        win_f, wpn_f, wpr_f, wout_f = full_w[l]
        u = _mm(h, win_f, name="in_proj_%d" % l)
        bias = _na_bias_table(na_rpb[l], rows, name="na_bias_%d" % l)
        o_na = _na_fwd(u, bias, s_len=s_len, heads=na_heads, name="na_fwd_%d" % l)
        o_ret, states = _ret_fwd(u, c2, s2, log_gamma[l], s_len=s_len, heads=ret_heads, q_off=o_retq, name="ret_fwd_%d" % l)

        def act(o1, z1, o2, z2):
            a1 = o1.astype(F32) * _silu_parts(z1.astype(F32))[0]
            sz = _silu_parts(z2.astype(F32))[0]
            outs = []
            for hh in range(ret_heads):
                sl = slice(hh * RET_VAL_DIM, (hh + 1) * RET_VAL_DIM)
                oh = o2[:, sl]
                r = lax.rsqrt(jnp.mean(oh * oh, axis=-1, keepdims=True) + NORM_EPS)
                outs.append(oh * r * sz[:, sl])
            return a1, jnp.concatenate(outs, axis=-1)

        a_na, a_ret = _ew(act, [('t', o_na, 0, w_na), ('t', u, o_naz // w_na, w_na), ('t', o_ret, 0, w_v), ('t', u, o_retz // w_v, w_v)],
                          [('t', w_na, BF16), ('t', w_v, BF16)], rows=t_len, tr=tr, name="act_%d" % l)
        y_na = _mm(a_na, wpn_f, name="proj_na_%d" % l)
        y_ret = _mm(a_ret, wpr_f, name="proj_ret_%d" % l)

        def merge(y1, y2, g1, g2):
            return _sigmoid(g1.astype(F32)) * y1.astype(F32) + _sigmoid(g2.astype(F32)) * y2.astype(F32)

        merged, = _ew(merge, [('t', y_na, 0, d_model), ('t', y_ret, 0, d_model), ('t', u, o_gna // d_model, d_model), ('t', u, o_gret // d_model, d_model)],
                      [('t', d_model, BF16)], rows=t_len, tr=tr, name="merge_%d" % l)
        out = _mm(merged, wout_f, out_dtype=F32, name="out_proj_%d" % l)
        x_new, = _ew(lambda xt, ot, gt: xt + gt * ot, [('t', x_all, 0, d_model), ('t', out, 0, d_model), ('g', gate)],
                     [('t', d_model, F32)], rows=t_len, tr=tr, n0=n0, name="resid_%d" % l)
        saved.append(dict(x=x_all, h=h, u=u, bias=bias, o_na=o_na, o_ret=o_ret, states=states, a_na=a_na, a_ret=a_ret,
                          y_na=y_na, y_ret=y_ret, merged=merged, out=out, gate=gate, gs=gs, scale=scale))
        x_all = x_new

    def final(xt, tt, gt):
        r = lax.rsqrt(jnp.mean(xt * xt, axis=-1, keepdims=True) + NORM_EPS)
        xh = xt * r
        e = xh * gt - tt
        dy = e * (1.0 / d_model)
        dyg = dy * gt
        dx = r * (dyg - xh * jnp.mean(dyg * xh, axis=-1, keepdims=True))
        return dx, _rsum(dy * xh), _rsum(e * e)

    dx_lat, d_final_g, loss_cols = _ew(final, [('t', x_all, 0, d_model), ('t', loss_target[0], 0, d_model), ('g', final_g[None, None, :])],
                                       [('t', d_model, F32), ('r', d_model, 1), ('r', d_model, 1)], rows=s_len, tr=tr, name="final")
    loss_part = (0.5 / d_model) * jnp.sum(loss_cols)
    dx_all = jnp.concatenate([dx_lat, jnp.zeros((l_len, d_model), F32)], axis=0)

    big_w = [(w_in, m_w_in, v_w_in), (w_proj_na, m_w_proj_na, v_w_proj_na), (w_proj_ret, m_w_proj_ret, v_w_proj_ret), (w_out, m_w_out, v_w_out)]
    big_res = [None] * 4
    small = dict(dmod_lat=[None] * depth, dmod_ctx=[None] * depth, dnorm_g=[None] * depth, drpb=[None] * depth, ddecay=[None] * depth)
    for l in reversed(range(depth)):
        sv = saved[l]
        win_f, wpn_f, wpr_f, wout_f = full_w[l]

        def resid_bwd(dxt, ot, gt):
            return gt * dxt, _rsum(dxt * ot)

        dout, dgate = _ew(resid_bwd, [('t', dx_all, 0, d_model), ('t', sv['out'], 0, d_model), ('g', sv['gate'])],
                          [('t', d_model, BF16), ('r', d_model, 2)], rows=t_len, tr=tr, n0=n0, name="resid_bwd_%d" % l)
        dmerged = _mm(dout, wout_f, tb=True, name="out_proj_dx_%d" % l)
        g_wout = _mm(sv['merged'], dout, ta=True, tm=512, tk=t_len, name="out_proj_dw_%d" % l)

        def merge_bwd(dm, y1, y2, g1, g2):
            dm = dm.astype(F32)
            s1, s2_ = _sigmoid(g1.astype(F32)), _sigmoid(g2.astype(F32))
            return dm * s1, dm * s2_, dm * y1.astype(F32) * s1 * (1.0 - s1), dm * y2.astype(F32) * s2_ * (1.0 - s2_)

        u = sv['u']
        dy_na, dy_ret, dg_na, dg_ret = _ew(
            merge_bwd, [('t', dmerged, 0, d_model), ('t', sv['y_na'], 0, d_model), ('t', sv['y_ret'], 0, d_model),
                        ('t', u, o_gna // d_model, d_model), ('t', u, o_gret // d_model, d_model)],
            [('t', d_model, BF16)] * 4, rows=t_len, tr=tr, name="merge_bwd_%d" % l)
        da_na = _mm(dy_na, wpn_f, tb=True, name="proj_na_dx_%d" % l)
        g_wpn = _mm(sv['a_na'], dy_na, ta=True, tm=512, tk=t_len, name="proj_na_dw_%d" % l)
        da_ret = _mm(dy_ret, wpr_f, tb=True, name="proj_ret_dx_%d" % l)
        g_wpr = _mm(sv['a_ret'], dy_ret, ta=True, tm=512, tk=t_len, name="proj_ret_dw_%d" % l)

        def act_bwd(da1, o1, z1, da2, o2, z2):
            da1, da2 = da1.astype(F32), da2.astype(F32)
            si1, ds1 = _silu_parts(z1.astype(F32))
            si2, ds2 = _silu_parts(z2.astype(F32))
            do1 = da1 * si1
            dz1 = da1 * o1.astype(F32) * ds1
            dn = da2 * si2
            do2, dz2 = [], []
            for hh in range(ret_heads):
                sl = slice(hh * RET_VAL_DIM, (hh + 1) * RET_VAL_DIM)
                oh = o2[:, sl]
                r = lax.rsqrt(jnp.mean(oh * oh, axis=-1, keepdims=True) + NORM_EPS)
                nh = oh * r
                dz2.append(da2[:, sl] * nh * ds2[:, sl])
                do2.append(r * (dn[:, sl] - nh * jnp.mean(dn[:, sl] * nh, axis=-1, keepdims=True)))
            return do1, dz1, jnp.concatenate(do2, axis=-1), jnp.concatenate(dz2, axis=-1)

        do_na, dz_na, do_ret, dz_ret = _ew(
            act_bwd, [('t', da_na, 0, w_na), ('t', sv['o_na'], 0, w_na), ('t', u, o_naz // w_na, w_na),
                      ('t', da_ret, 0, w_v), ('t', sv['o_ret'], 0, w_v), ('t', u, o_retz // w_v, w_v)],
            [('t', w_na, BF16), ('t', w_na, BF16), ('t', w_v, BF16), ('t', w_v, BF16)], rows=t_len, tr=tr, name="act_bwd_%d" % l)
        dq_na, dk_na, dv_na, dbias = _na_bwd(u, sv['bias'], sv['o_na'], do_na, s_len=s_len, heads=na_heads, name="na_bwd_%d" % l)
        small['drpb'][l] = _rpb_grad(dbias, name="rpb_grad_%d" % l)
        dq_r, dk_r, dv_r, dlg = _ret_bwd(u, c2, s2, log_gamma[l], sv['states'], do_ret, s_len=s_len, heads=ret_heads,
                                         q_off=o_retq, name="ret_bwd_%d" % l)
        small['ddecay'][l] = jnp.transpose(dlg[:, :, 0, 0]) * _sigmoid(-ret_decay_logit[l])
        du = jnp.concatenate([dq_na, dk_na, dv_na, dz_na, dq_r, dk_r, dv_r, dz_ret, dg_na, dg_ret], axis=1)
        dh = _mm(du, win_f, tb=True, out_dtype=F32, tn=1024, name="in_proj_dx_%d" % l)
        g_win = _mm(sv['h'], du, ta=True, tm=512, tk=t_len, name="in_proj_dw_%d" % l)

        def modnorm_bwd(xt, dht, dxt, gs_t):
            r = lax.rsqrt(jnp.mean(xt * xt, axis=-1, keepdims=True) + NORM_EPS)
            xh = xt * r
            dhg = dht * gs_t
            dx = r * (dhg - xh * jnp.mean(dhg * xh, axis=-1, keepdims=True)) + dxt
            return dx, _rsum(dht), _rsum(dht * xh)

        dx_all, dshift, dgs = _ew(modnorm_bwd, [('t', sv['x'], 0, d_model), ('t', dh, 0, d_model), ('t', dx_all, 0, d_model), ('g', sv['gs'])],
                                  [('t', d_model, F32), ('r', d_model, 2), ('r', d_model, 2)], rows=t_len, tr=tr, n0=n0, name="modnorm_bwd_%d" % l)
        dscale = dgs * norm_g[l][None, None, :]
        small['dnorm_g'][l] = jnp.sum(dgs * (1.0 + sv['scale']), axis=(0, 1))
        dmod = jnp.concatenate([dshift, dscale, dgate], axis=-1)[:, 0]
        small['dmod_lat'][l], small['dmod_ctx'][l] = dmod[0], dmod[1]

        mine, theirs = _pair_exchange([g_win, g_wpn, g_wpr, g_wout], big_axes, name="pair_exchange_%d" % l)
        pair = []
        for i in range(4):
            pr, pw = mine[i].shape
            s, = _ew(lambda a, b: a.astype(F32) + b.astype(F32), [('t', mine[i], 0, pw), ('t', theirs[i], 0, pw)], [('t', pw, BF16)],
                     rows=pr, tr=_rows_per_tile(pr, pw), name="sum_pair_%d_%d" % (i, l))
            pair.append(s)
        recv = _scatter_grads(pair, big_axes, name="scatter_grads_%d" % l)
        parts = []
        for i, rbuf in enumerate(recv):
            _, pr, pw = rbuf.shape
            p, = _ew(lambda a, b, c_, d: ((a.astype(F32) + b.astype(F32)) + c_.astype(F32)) + d.astype(F32),
                     [('t', rbuf, 0, pw, k) for k in range(N_CHIPS)], [('t', pw, F32)],
                     rows=pr, tr=_rows_per_tile(pr, pw), name="sum_chips_%d_%d" % (i, l))
            parts.append(p)
        shard_g = _share_halves(parts, big_axes, name="share_halves_%d" % l)
        for i in range(4):
            w3, m3, v3 = big_w[i]
            big_res[i] = _adamw_layer(w3, m3, v3, shard_g[i], None, l, big_res[i], name="adamw_big_%d_%d" % (i, l))

    grad_x = dx_all[:s_len][None]

    drpb = jnp.stack(small['drpb']).reshape(-1)
    ddecay = jnp.stack(small['ddecay']).reshape(-1)
    pieces = [jnp.stack(small['dmod_lat']).reshape(-1), jnp.stack(small['dmod_ctx']).reshape(-1),
              jnp.stack(small['dnorm_g']).reshape(-1), d_final_g.reshape(-1), drpb, ddecay, loss_part[None]]
    sizes = [int(p.shape[0]) for p in pieces]
    pads = [-(-s // LANES) * LANES for s in sizes]
    packed = jnp.concatenate([jnp.pad(p, (0, pd - s)) for p, s, pd in zip(pieces, sizes, pads)])
    gathered = _all_gather_small(_pack_rows(packed), name="gather_small_grads")
    r_small = gathered.shape[1]

    def sum8(*t):
        acc = t[0]
        for other in t[1:]:
            acc = acc + other
        return acc

    total, = _ew(sum8, [('t', gathered, 0, LANES, k) for k in range(N_DEV)], [('t', LANES, F32)], rows=r_small, tr=r_small, name="sum_devices")
    total = total.reshape(-1)
    starts = np.cumsum([0] + pads)
    g_mod_lat_sum, g_mod_ctx, g_norm_g, g_final_g, g_rpb, g_decay, loss = [total[starts[i]:starts[i] + sizes[i]] for i in range(len(pieces))]
    loss = loss[0]
    g_ada_b = (g_mod_lat_sum + g_mod_ctx).reshape(depth, mod_cols)
    g_mod_ctx = g_mod_ctx.reshape(depth, mod_cols)
    dmod_lat_all = gathered.reshape(N_DEV, -1)[:, :depth * mod_cols].reshape(N_DEV, depth, mod_cols)

    ada_res = None
    dcc_part = jnp.zeros((16, d_model), F32)
    for l in reversed(range(depth)):
        lat_cols = lax.dynamic_slice_in_dim(dmod_lat_all[:, l], chip * mod_shard, mod_shard, axis=1)
        ctx_cols = lax.dynamic_slice_in_dim(g_mod_ctx[l], chip * mod_shard, mod_shard, axis=0)
        d_rows = jnp.concatenate([lat_cols, ctx_cols[None], jnp.zeros((16 - N_DEV - 1, mod_shard), F32)], axis=0)
        g_ada = _mm(a_rows, d_rows, ta=True, out_dtype=F32, tm=512, name="ada_dw_%d" % l)
        ada_res = _adamw_layer(ada_w, m_ada_w, v_ada_w, g_ada, None, l, ada_res, name="adamw_ada_%d" % l)
        c_rows = jnp.concatenate([ctx_cols[None], jnp.zeros((15, mod_shard), F32)], axis=0)
        dcc_part = dcc_part + _mm(c_rows, ada_w, tb=True, b_lead=l, out_dtype=F32, name="ada_dc_%d" % l)
    dcc_all = _all_gather_small(_pack_rows(dcc_part[0]), name="gather_dcc")[:, :d_model // LANES].reshape(N_CHIPS, 2, d_model)[:, 0]
    dcc = ((dcc_all[0] + dcc_all[1]) + dcc_all[2]) + dcc_all[3]
    sg = _sigmoid(c_ctx)
    g_c_ctx = dcc * (sg * (1.0 + c_ctx * (1.0 - sg)))

    small_w = [(c_ctx, m_c_ctx, v_c_ctx, g_c_ctx), (ada_b, m_ada_b, v_ada_b, g_ada_b),
               (norm_g, m_norm_g, v_norm_g, g_norm_g), (na_rpb, m_na_rpb, v_na_rpb, g_rpb),
               (ret_decay_logit, m_ret_decay_logit, v_ret_decay_logit, g_decay), (final_g, m_final_g, v_final_g, g_final_g)]
    sw_sizes = [int(np.prod(t[0].shape)) for t in small_w]
    sw_pads = [-(-s // LANES) * LANES for s in sw_sizes]

    def pack(j):
        return _pack_rows(jnp.concatenate([jnp.pad(t[j].reshape(-1), (0, pd - s)) for t, s, pd in zip(small_w, sw_sizes, sw_pads)]))

    pw_, pm_, pv_, pg_ = pack(0), pack(1), pack(2), pack(3)
    sw_out = _ew(lambda w, m, v, g: (g,) + _adamw_math(w, g, m, v),
                 [('t', pw_, 0, LANES), ('t', pm_, 0, LANES), ('t', pv_, 0, LANES), ('t', pg_, 0, LANES)],
                 [('t', LANES, F32)] * 4, rows=pw_.shape[0], tr=pw_.shape[0], name="adamw_small")
    sw_starts = np.cumsum([0] + sw_pads)

    def unpack(arr, i):
        return arr.reshape(-1)[sw_starts[i]:sw_starts[i] + sw_sizes[i]].reshape(small_w[i][0].shape)

    sm = [[unpack(sw_out[j], i) for i in range(len(small_w))] for j in range(4)]
    def ordered(j):
        return [sm[j][0], ada_res[j], sm[j][1], sm[j][2], big_res[0][j], sm[j][3], sm[j][4],
                big_res[1][j], big_res[2][j], big_res[3][j], sm[j][5]]

    return (loss, grad_x, *ordered(0), *ordered(1), *ordered(2), *ordered(3))
```

```python
import functools
import math

import numpy as np
import jax
import jax.numpy as jnp
from jax import lax
from jax.experimental import pallas as pl
from jax.experimental.pallas import tpu as pltpu

GRID_W = 64
NA_HEAD_DIM = 128
NA_WIN_ROWS = 8
NA_WIN_COLS = 16
RET_KEY_DIM = 128
RET_VAL_DIM = 256
RET_CHUNK = 128
ROPE_BASE = 10000.0
NORM_EPS = 1e-6
MASK_VALUE = -1e30
ADAM_LR = 0.001
ADAM_B1 = 0.9
ADAM_B2 = 0.999
ADAM_EPS = 1e-08
ADAM_WD = 0.01
ADAM_STEP = 10

N_CHIPS = 4
N_DEV = 8
LANES = 128
VMEM_LIMIT = 56 * 1024 * 1024
BF16 = jnp.bfloat16
F32 = jnp.float32
MESH = pl.DeviceIdType.MESH
ANY = pl.BlockSpec(memory_space=pl.ANY)


def _tile(dim, pref, align=LANES):
    if dim <= pref:
        return dim
    t = (pref // align) * align
    while t >= align:
        if dim % t == 0:
            return t
        t -= align
    return dim


def _rows_per_tile(rows, width, tile_bytes=1 << 20):
    return _tile(rows, max(8, tile_bytes // (4 * width)), 8)


def _params(sem):
    return pltpu.CompilerParams(dimension_semantics=sem, vmem_limit_bytes=VMEM_LIMIT)


def _sigmoid(x):
    return 1.0 / (1.0 + jnp.exp(-x))


def _dot(a, b, ca, cb):
    return lax.dot_general(a, b, (((ca,), (cb,)), ((), ())), preferred_element_type=F32)


def _mm(a, b, *, ta=False, tb=False, a_lead=None, b_lead=None, out_dtype=BF16, tm=768, tn=512, tk=2048, name):
    ash = a.shape[1:] if a_lead is not None else a.shape
    bsh = b.shape[1:] if b_lead is not None else b.shape
    m, k = (ash[1], ash[0]) if ta else ash
    n, k2 = bsh if tb else (bsh[1], bsh[0])
    assert k == k2, (name, ash, bsh)
    tm, tn, tk = _tile(m, tm), _tile(n, tn), _tile(k, tk)
    nk = k // tk

    def lead(spec_shape, imap, l):
        if l is None:
            return pl.BlockSpec(spec_shape, imap)
        return pl.BlockSpec((None,) + spec_shape, lambda i, j, kk: (l,) + imap(i, j, kk))

    a_spec = lead((tk, tm), lambda i, j, kk: (kk, i), a_lead) if ta else lead((tm, tk), lambda i, j, kk: (i, kk), a_lead)
    b_spec = lead((tn, tk), lambda i, j, kk: (j, kk), b_lead) if tb else lead((tk, tn), lambda i, j, kk: (kk, j), b_lead)
    ca, cb = (0 if ta else 1), (1 if tb else 0)

    def body(a_ref, b_ref, o_ref, *scratch):
        part = _dot(a_ref[...].astype(BF16), b_ref[...].astype(BF16), ca, cb)
        if nk == 1:
            o_ref[...] = part.astype(o_ref.dtype)
            return
        acc_ref, = scratch
        kk = pl.program_id(2)

        @pl.when(kk == 0)
        def _():
            acc_ref[...] = part

        @pl.when(kk > 0)
        def _():
            acc_ref[...] += part

        @pl.when(kk == nk - 1)
        def _():
            o_ref[...] = acc_ref[...].astype(o_ref.dtype)

    return pl.pallas_call(
        body, name=name, grid=(m // tm, n // tn, nk),
        in_specs=[a_spec, b_spec],
        out_specs=pl.BlockSpec((tm, tn), lambda i, j, kk: (i, j)),
        out_shape=jax.ShapeDtypeStruct((m, n), out_dtype),
        scratch_shapes=[] if nk == 1 else [pltpu.VMEM((tm, tn), F32)],
        compiler_params=_params(("parallel", "parallel", "arbitrary")),
    )(a, b)


def _ew(fn, ins, outs, *, rows, tr, name, n0=None, aliases=None):
    assert rows % tr == 0, (name, rows, tr)
    nt = rows // tr

    def grp(i):
        return 0 if n0 is None else jnp.where(i < n0, 0, 1)

    in_specs, args = [], []
    for spec in ins:
        if spec[0] == 't':
            arr, cb, w = spec[1], spec[2], spec[3]
            l = spec[4] if len(spec) > 4 else None
            if l is None:
                in_specs.append(pl.BlockSpec((tr, w), functools.partial(lambda i, cb: (i, cb), cb=cb)))
            else:
                in_specs.append(pl.BlockSpec((None, tr, w), functools.partial(lambda i, cb, l: (l, i, cb), cb=cb, l=l)))
            args.append(arr)
        else:
            arr = spec[1]
            g = arr.shape[0]
            if g == 1:
                in_specs.append(pl.BlockSpec((None, 1, arr.shape[2]), lambda i: (0, 0, 0)))
            else:
                in_specs.append(pl.BlockSpec((None, 1, arr.shape[2]), lambda i: (grp(i), 0, 0)))
            args.append(arr)
    out_specs, out_shapes, is_red = [], [], []
    for spec in outs:
        if spec[0] == 't':
            w, dt = spec[1], spec[2]
            if len(spec) > 3:
                l, nl = spec[3], spec[4]
                out_specs.append(pl.BlockSpec((None, tr, w), functools.partial(lambda i, l: (l, i, 0), l=l)))
                out_shapes.append(jax.ShapeDtypeStruct((nl, rows, w), dt))
            else:
                out_specs.append(pl.BlockSpec((tr, w), lambda i: (i, 0)))
                out_shapes.append(jax.ShapeDtypeStruct((rows, w), dt))
            is_red.append(False)
        else:
            w, g = spec[1], spec[2]
            if g == 1:
                out_specs.append(pl.BlockSpec((None, 1, w), lambda i: (0, 0, 0)))
            else:
                out_specs.append(pl.BlockSpec((None, 1, w), lambda i: (grp(i), 0, 0)))
            out_shapes.append(jax.ShapeDtypeStruct((g, 1, w), F32))
            is_red.append(True)
    n_in = len(ins)
    n_alias = 0 if aliases is None else len(aliases)

    def body(*refs):
        in_refs = refs[:n_in]
        out_refs = refs[n_in + n_alias:]
        res = fn(*[r[...] for r in in_refs])
        if not isinstance(res, (tuple, list)):
            res = (res,)
        i = pl.program_id(0)
        first = (i == 0) if n0 is None else ((i == 0) | (i == n0))
        for o_ref, val, red in zip(out_refs, res, is_red):
            if not red:
                o_ref[...] = val.astype(o_ref.dtype)
            else:
                @pl.when(first)
                def _(o_ref=o_ref, val=val):
                    o_ref[...] = val

                @pl.when(jnp.logical_not(first))
                def _(o_ref=o_ref, val=val):
                    o_ref[...] += val

    io_alias = {}
    if aliases is not None:
        for a_idx, (arr, o_idx) in enumerate(aliases):
            in_specs.append(ANY)
            args.append(arr)
            io_alias[n_in + a_idx] = o_idx
    has_red = any(is_red)
    return pl.pallas_call(
        body, name=name, grid=(nt,), in_specs=in_specs, out_specs=out_specs, out_shape=out_shapes,
        input_output_aliases=io_alias,
        compiler_params=_params(("arbitrary",) if has_red else ("parallel",)),
    )(*args)


def _rsum(v):
    return jnp.sum(v, axis=0, keepdims=True)


def _silu_parts(z):
    sg = _sigmoid(z)
    return z * sg, sg * (1.0 + z * (1.0 - sg))


def _na_bias_table(rpb, rows, *, name):
    kh, kw = NA_WIN_ROWS, NA_WIN_COLS
    assert rows >= kh
    heads = rpb.shape[0]
    e1, e2 = _na_onehots()
    rpb16 = jnp.pad(rpb, ((0, 0), (0, 16 - rpb.shape[1]), (0, LANES - rpb.shape[2])))

    def body(r_ref, e1_ref, e2_ref, o_ref):
        e1b = e1_ref[...].astype(BF16)
        y = sum(_dot(e1b, part, 0, 0) for part in _split3(r_ref[...]))
        e2b = e2_ref[...].astype(BF16)
        o_ref[...] = sum(_dot(part, e2b, 1, 1) for part in _split3(y))

    z = pl.pallas_call(
        body, name=name, grid=(heads,),
        in_specs=[pl.BlockSpec((None, 16, LANES), lambda h: (h, 0, 0)),
                  pl.BlockSpec(e1.shape, lambda h: (0, 0)), pl.BlockSpec(e2.shape, lambda h: (0, 0))],
        out_specs=pl.BlockSpec((None, kh * kh, GRID_W * GRID_W), lambda h: (h, 0, 0)),
        out_shape=jax.ShapeDtypeStruct((heads, kh * kh, GRID_W * GRID_W), F32),
        compiler_params=_params(("parallel",)),
    )(rpb16, e1, e2)
    cidx = np.arange(GRID_W)
    c0 = np.clip(cidx - kw // 2, 0, GRID_W - kw)
    col_in = (cidx[None, :] >= c0[:, None]) & (cidx[None, :] < c0[:, None] + kw)
    bias = z.reshape(heads, kh, kh, GRID_W, GRID_W).transpose(0, 1, 3, 2, 4)
    bias = jnp.where(col_in[None, None, :, None, :], bias, MASK_VALUE)
    return bias.reshape(heads, kh, GRID_W, kh * GRID_W)


def _na_onehots():
    kh, kw = NA_WIN_ROWS, NA_WIN_COLS
    cidx = np.arange(GRID_W)
    dc = cidx[None, :] - cidx[:, None] + (kw - 1)
    e2 = np.zeros((GRID_W * GRID_W, LANES), np.float32)
    ok = (dc >= 0) & (dc <= 2 * kw - 2)
    cq, ck = np.nonzero(ok)
    e2[cq * GRID_W + ck, dc[cq, ck]] = 1.0
    dr = np.arange(kh)[None, :] - np.arange(kh)[:, None] + (kh - 1)
    e1 = np.zeros((16, kh * kh), np.float32)
    dl, kr = np.nonzero(np.ones_like(dr))
    e1[dr[dl, kr], dl * kh + kr] = 1.0
    return jnp.asarray(e1), jnp.asarray(e2)


def _na_row_scores(q, kl, kc, bias, scale):
    s_loc = _dot(q, kl, 1, 1) * scale + bias
    s_ctx = _dot(q, kc, 1, 1) * scale
    m = jnp.maximum(jnp.max(s_loc, axis=-1, keepdims=True), jnp.max(s_ctx, axis=-1, keepdims=True))
    p_loc = jnp.exp(s_loc - m)
    p_ctx = jnp.exp(s_ctx - m)
    den = jnp.sum(p_loc, axis=-1, keepdims=True) + jnp.sum(p_ctx, axis=-1, keepdims=True)
    return p_loc, p_ctx, den


def _na_fwd(u, bias, *, s_len, heads, name):
    t_len = u.shape[0]
    rows = s_len // GRID_W
    nloc = NA_WIN_ROWS * GRID_W
    scale = NA_HEAD_DIM ** -0.5
    hd = NA_HEAD_DIM

    def body(q_ref, k_ref, v_ref, b_ref, o_ref):
        kc = k_ref[s_len:t_len, :]
        vc = v_ref[s_len:t_len, :]

        def row(r, carry):
            r0 = jnp.clip(r - NA_WIN_ROWS // 2, 0, rows - NA_WIN_ROWS)
            qs = pl.multiple_of(r * GRID_W, GRID_W)
            ks = pl.multiple_of(r0 * GRID_W, GRID_W)
            q = q_ref[pl.ds(qs, GRID_W), :]
            kl = k_ref[pl.ds(ks, nloc), :]
            vl = v_ref[pl.ds(ks, nloc), :]
            p_loc, p_ctx, den = _na_row_scores(q, kl, kc, b_ref[r - r0], scale)
            o = _dot(p_loc.astype(BF16), vl, 1, 0) + _dot(p_ctx.astype(BF16), vc, 1, 0)
            o_ref[pl.ds(qs, GRID_W), :] = (o / den).astype(o_ref.dtype)
            return carry

        lax.fori_loop(0, rows, row, 0)
        qc = q_ref[s_len:t_len, :]
        s = _dot(qc, kc, 1, 1) * scale
        p = jnp.exp(s - jnp.max(s, axis=-1, keepdims=True))
        o = _dot(p.astype(BF16), vc, 1, 0) / jnp.sum(p, axis=-1, keepdims=True)
        o_ref[s_len:t_len, :] = o.astype(o_ref.dtype)

    col = lambda off: pl.BlockSpec((t_len, hd), functools.partial(lambda h, off: (0, off + h), off=off))
    return pl.pallas_call(
        body, name=name, grid=(heads,),
        in_specs=[col(0), col(heads), col(2 * heads),
                  pl.BlockSpec((None, NA_WIN_ROWS, GRID_W, nloc), lambda h: (h, 0, 0, 0))],
        out_specs=pl.BlockSpec((t_len, hd), lambda h: (0, h)),
        out_shape=jax.ShapeDtypeStruct((t_len, heads * hd), BF16),
        compiler_params=_params(("parallel",)),
    )(u, u, u, bias)


def _na_bwd(u, bias, o, do, *, s_len, heads, name):
    t_len = u.shape[0]
    rows = s_len // GRID_W
    nloc = NA_WIN_ROWS * GRID_W
    scale = NA_HEAD_DIM ** -0.5
    hd = NA_HEAD_DIM

    def body(q_ref, k_ref, v_ref, b_ref, o_ref, do_ref, dq_ref, dk_ref, dv_ref, db_ref, dk_acc, dv_acc):
        kc = k_ref[s_len:t_len, :]
        vc = v_ref[s_len:t_len, :]
        dk_acc[...] = jnp.zeros_like(dk_acc)
        dv_acc[...] = jnp.zeros_like(dv_acc)
        db_ref[...] = jnp.zeros_like(db_ref)

        def row(r, carry):
            r0 = jnp.clip(r - NA_WIN_ROWS // 2, 0, rows - NA_WIN_ROWS)
            dl = r - r0
            qs = pl.multiple_of(r * GRID_W, GRID_W)
            ks = pl.multiple_of(r0 * GRID_W, GRID_W)
            q = q_ref[pl.ds(qs, GRID_W), :]
            kl = k_ref[pl.ds(ks, nloc), :]
            vl = v_ref[pl.ds(ks, nloc), :]
            dout = do_ref[pl.ds(qs, GRID_W), :]
            out = o_ref[pl.ds(qs, GRID_W), :]
            p_loc, p_ctx, den = _na_row_scores(q, kl, kc, b_ref[dl], scale)
            inv = 1.0 / den
            p_loc = p_loc * inv
            p_ctx = p_ctx * inv
            dlt = jnp.sum(dout.astype(F32) * out.astype(F32), axis=-1, keepdims=True)
            ds_loc = p_loc * (_dot(dout, vl, 1, 1) - dlt)
            ds_ctx = p_ctx * (_dot(dout, vc, 1, 1) - dlt)
            db_ref[dl] += ds_loc
            ds_loc_b = ds_loc.astype(BF16)
            ds_ctx_b = ds_ctx.astype(BF16)
            dq = (_dot(ds_loc_b, kl, 1, 0) + _dot(ds_ctx_b, kc, 1, 0)) * scale
            dq_ref[pl.ds(qs, GRID_W), :] = dq.astype(dq_ref.dtype)
            dk_acc[pl.ds(ks, nloc), :] += _dot(ds_loc_b, q, 0, 0) * scale
            dv_acc[pl.ds(ks, nloc), :] += _dot(p_loc.astype(BF16), dout, 0, 0)
            dk_acc[s_len:t_len, :] += _dot(ds_ctx_b, q, 0, 0) * scale
            dv_acc[s_len:t_len, :] += _dot(p_ctx.astype(BF16), dout, 0, 0)
            return carry

        lax.fori_loop(0, rows, row, 0)
        qc = q_ref[s_len:t_len, :]
        dout = do_ref[s_len:t_len, :]
        out = o_ref[s_len:t_len, :]
        s = _dot(qc, kc, 1, 1) * scale
        p = jnp.exp(s - jnp.max(s, axis=-1, keepdims=True))
        p = p / jnp.sum(p, axis=-1, keepdims=True)
        dlt = jnp.sum(dout.astype(F32) * out.astype(F32), axis=-1, keepdims=True)
        ds = (p * (_dot(dout, vc, 1, 1) - dlt)).astype(BF16)
        dq_ref[s_len:t_len, :] = (_dot(ds, kc, 1, 0) * scale).astype(dq_ref.dtype)
        dk_acc[s_len:t_len, :] += _dot(ds, qc, 0, 0) * scale
        dv_acc[s_len:t_len, :] += _dot(p.astype(BF16), dout, 0, 0)
        dk_ref[...] = dk_acc[...].astype(dk_ref.dtype)
        dv_ref[...] = dv_acc[...].astype(dv_ref.dtype)

    col = lambda off: pl.BlockSpec((t_len, hd), functools.partial(lambda h, off: (0, off + h), off=off))
    tbl = pl.BlockSpec((None, NA_WIN_ROWS, GRID_W, nloc), lambda h: (h, 0, 0, 0))
    tok = jax.ShapeDtypeStruct((t_len, heads * hd), BF16)
    return pl.pallas_call(
        body, name=name, grid=(heads,),
        in_specs=[col(0), col(heads), col(2 * heads), tbl, col(0), col(0)],
        out_specs=[col(0), col(0), col(0), tbl],
        out_shape=[tok, tok, tok, jax.ShapeDtypeStruct(bias.shape, F32)],
        scratch_shapes=[pltpu.VMEM((t_len, hd), F32), pltpu.VMEM((t_len, hd), F32)],
        compiler_params=_params(("parallel",)),
    )(u, u, u, bias, o, do)


def _split3(x):
    hi = x.astype(BF16)
    r1 = x - hi.astype(F32)
    mid = r1.astype(BF16)
    lo = (r1 - mid.astype(F32)).astype(BF16)
    return hi, mid, lo


def _rpb_grad(dbias, *, name):
    heads = dbias.shape[0]
    kh = NA_WIN_ROWS
    e1, e2 = _na_onehots()
    x = dbias.reshape(heads, kh, GRID_W, kh, GRID_W).transpose(0, 1, 3, 2, 4).reshape(heads, kh * kh, GRID_W * GRID_W)

    def body(x_ref, e1_ref, e2_ref, o_ref):
        e2b = e2_ref[...].astype(BF16)
        y = sum(_dot(part, e2b, 1, 0) for part in _split3(x_ref[...]))
        e1b = e1_ref[...].astype(BF16)
        o_ref[...] = sum(_dot(e1b, part, 1, 0) for part in _split3(y))

    out = pl.pallas_call(
        body, name=name, grid=(heads,),
        in_specs=[pl.BlockSpec((None, kh * kh, GRID_W * GRID_W), lambda h: (h, 0, 0)),
                  pl.BlockSpec(e1.shape, lambda h: (0, 0)), pl.BlockSpec(e2.shape, lambda h: (0, 0))],
        out_specs=pl.BlockSpec((None, 16, LANES), lambda h: (h, 0, 0)),
        out_shape=jax.ShapeDtypeStruct((heads, 16, LANES), F32),
        compiler_params=_params(("parallel",)),
    )(x, e1, e2)
    return out[:, :2 * kh - 1, :2 * NA_WIN_COLS - 1]


def _rope_tables(s_len, l_len):
    nf = RET_KEY_DIM // 4
    t = np.arange(s_len)
    row = (t // GRID_W).astype(np.float32)
    colp = (t % GRID_W).astype(np.float32)
    inv_freq = jnp.asarray(ROPE_BASE, F32) ** (-jnp.arange(nf, dtype=F32) / nf)
    ang = jnp.concatenate([jnp.asarray(row)[:, None] * inv_freq, jnp.asarray(colp)[:, None] * inv_freq], axis=-1)
    cos, sin = jnp.cos(ang), jnp.sin(ang)
    c2 = jnp.concatenate([cos, cos], axis=-1)
    s2 = jnp.concatenate([-sin, sin], axis=-1)
    c2 = jnp.concatenate([c2, jnp.ones((l_len, RET_KEY_DIM), F32)], axis=0)
    s2 = jnp.concatenate([s2, jnp.zeros((l_len, RET_KEY_DIM), F32)], axis=0)
    return c2, s2


def _rope(x, c2, s2):
    return x * c2 + pltpu.roll(x, RET_KEY_DIM // 2, 1) * s2


def _rope_t(d, c2, s2):
    return d * c2 + pltpu.roll(d * s2, RET_KEY_DIM // 2, 1)


def _ret_decays(lg, direction):
    cs = RET_CHUNK
    i_col = lax.broadcasted_iota(jnp.int32, (cs, 1), 0)
    p_col = jnp.where(direction == 0, i_col, cs - 1 - i_col).astype(F32)
    pi = lax.broadcasted_iota(jnp.int32, (cs, cs), 0)
    pj = lax.broadcasted_iota(jnp.int32, (cs, cs), 1)
    diff = jnp.where(direction == 0, pi - pj, pj - pi).astype(F32)
    dm = jnp.where(diff >= 0, jnp.exp(jnp.maximum(diff, 0.0) * lg), 0.0)
    qdec = jnp.exp((p_col + 1.0) * lg)
    kdec = jnp.exp((cs - 1.0 - p_col) * lg)
    cd = jnp.exp(jnp.full((1, 1), cs, F32) * lg)
    return p_col, dm, qdec, kdec, cd


def _ret_chunk_index(t, direction, n_chunks, lat_chunks):
    return jnp.where(direction == 0, lax.rem(t + lat_chunks, n_chunks), n_chunks - 1 - t)


def _ret_fwd(u, c2, s2, lg, *, s_len, heads, q_off, name):
    t_len = u.shape[0]
    cs, dk, dv = RET_CHUNK, RET_KEY_DIM, RET_VAL_DIM
    n_chunks, lat_chunks = t_len // cs, s_len // cs
    k_scale = dk ** -0.5
    qb, kb, vb = q_off // dk, q_off // dk + heads, (q_off + 2 * heads * dk) // dv

    def body(lg_ref, q_ref, k_ref, v_ref, c_ref, s_ref, o_ref, st_ref, state):
        h, d = pl.program_id(0), pl.program_id(1)
        _, dm, qdec, kdec, cd = _ret_decays(lg_ref[d, h], d)
        state[...] = jnp.zeros_like(state)

        def step(t, carry):
            c = _ret_chunk_index(t, d, n_chunks, lat_chunks)
            r = pl.ds(pl.multiple_of(c * cs, cs), cs)
            cc, ss = c_ref[r, :], s_ref[r, :]
            qc = _rope(q_ref[r, :].astype(F32), cc, ss)
            kc = _rope(k_ref[r, :].astype(F32), cc, ss) * k_scale
            vc = v_ref[r, :]
            st = state[...]
            st_ref[t] = st
            a = _dot(qc.astype(BF16), kc.astype(BF16), 1, 1) * dm
            oc = _dot(a.astype(BF16), vc, 1, 0) + _dot((qc * qdec).astype(BF16), st.astype(BF16), 1, 0)
            state[...] = st * cd + _dot((kc * kdec).astype(BF16), vc, 0, 0)

            @pl.when(d == 0)
            def _():
                o_ref[r, :] = oc

            @pl.when(d == 1)
            def _():
                o_ref[r, :] += oc

            return carry

        lax.fori_loop(0, n_chunks, step, 0)

    return pl.pallas_call(
        body, name=name, grid=(heads, 2),
        in_specs=[pl.BlockSpec(memory_space=pltpu.SMEM),
                  pl.BlockSpec((t_len, dk), lambda h, d: (0, qb + h)),
                  pl.BlockSpec((t_len, dk), lambda h, d: (0, kb + h)),
                  pl.BlockSpec((t_len, dv), lambda h, d: (0, vb + h)),
                  pl.BlockSpec((t_len, dk), lambda h, d: (0, 0)),
                  pl.BlockSpec((t_len, dk), lambda h, d: (0, 0))],
        out_specs=[pl.BlockSpec((t_len, dv), lambda h, d: (0, h)),
                   pl.BlockSpec((None, None, n_chunks, dk, dv), lambda h, d: (h, d, 0, 0, 0))],
        out_shape=[jax.ShapeDtypeStruct((t_len, heads * dv), F32),
                   jax.ShapeDtypeStruct((heads, 2, n_chunks, dk, dv), F32)],
        scratch_shapes=[pltpu.VMEM((dk, dv), F32)],
        compiler_params=_params(("parallel", "arbitrary")),
    )(lg, u, u, u, c2, s2)


def _ret_bwd(u, c2, s2, lg, states, do, *, s_len, heads, q_off, name):
    t_len = u.shape[0]
    cs, dk, dv = RET_CHUNK, RET_KEY_DIM, RET_VAL_DIM
    n_chunks, lat_chunks = t_len // cs, s_len // cs
    k_scale = dk ** -0.5
    qb, kb, vb = q_off // dk, q_off // dk + heads, (q_off + 2 * heads * dk) // dv

    def body(lg_ref, q_ref, k_ref, v_ref, c_ref, s_ref, st_ref, do_ref, dq_ref, dk_ref, dv_ref, dlg_ref, dstate, acc):
        h, d = pl.program_id(0), pl.program_id(1)
        p_col, dm, qdec, kdec, cd = _ret_decays(lg_ref[d, h], d)
        dstate[...] = jnp.zeros_like(dstate)
        acc[...] = jnp.zeros_like(acc)

        def step(i, carry):
            t = n_chunks - 1 - i
            c = _ret_chunk_index(t, d, n_chunks, lat_chunks)
            r = pl.ds(pl.multiple_of(c * cs, cs), cs)
            cc, ss = c_ref[r, :], s_ref[r, :]
            qc = _rope(q_ref[r, :].astype(F32), cc, ss)
            kc = _rope(k_ref[r, :].astype(F32), cc, ss) * k_scale
            vc = v_ref[r, :]
            doc = do_ref[r, :].astype(BF16)
            st = st_ref[t]
            dst = dstate[...]
            qb16, kb16 = qc.astype(BF16), kc.astype(BF16)
            a = _dot(qb16, kb16, 1, 1) * dm
            dam = (_dot(doc, vc, 1, 1) * dm).astype(BF16)
            dq_i = _dot(dam, kb16, 1, 0)
            dk_i = _dot(dam, qb16, 0, 0)
            dq_c = _dot(doc, st.astype(BF16), 1, 1) * qdec
            dst16 = dst.astype(BF16)
            dvc = _dot(a.astype(BF16), doc, 0, 0) + _dot((kc * kdec).astype(BF16), dst16, 1, 0)
            dk_s = _dot(vc, dst16, 1, 1) * kdec
            g = (jnp.sum(qc * (p_col * dq_i + (p_col + 1.0) * dq_c), axis=-1, keepdims=True)
                 + jnp.sum(kc * ((cs - 1.0 - p_col) * dk_s - p_col * dk_i), axis=-1, keepdims=True))
            g = jnp.sum(g, axis=0, keepdims=True) + cs * cd * jnp.sum(jnp.sum(dst * st, axis=-1, keepdims=True), axis=0, keepdims=True)
            acc[...] += jnp.broadcast_to(g, acc.shape)
            dstate[...] = dst * cd + _dot((qc * qdec).astype(BF16), doc, 0, 0)
            dq = _rope_t(dq_i + dq_c, cc, ss)
            dkk = _rope_t((dk_i + dk_s) * k_scale, cc, ss)

            @pl.when(d == 0)
            def _():
                dq_ref[r, :] = dq.astype(dq_ref.dtype)
                dk_ref[r, :] = dkk.astype(dk_ref.dtype)
                dv_ref[r, :] = dvc.astype(dv_ref.dtype)

            @pl.when(d == 1)
            def _():
                dq_ref[r, :] = (dq_ref[r, :].astype(F32) + dq).astype(dq_ref.dtype)
                dk_ref[r, :] = (dk_ref[r, :].astype(F32) + dkk).astype(dk_ref.dtype)
                dv_ref[r, :] = (dv_ref[r, :].astype(F32) + dvc).astype(dv_ref.dtype)

            return carry

        lax.fori_loop(0, n_chunks, step, 0)
        dlg_ref[...] = acc[...]

    return pl.pallas_call(
        body, name=name, grid=(heads, 2),
        in_specs=[pl.BlockSpec(memory_space=pltpu.SMEM),
                  pl.BlockSpec((t_len, dk), lambda h, d: (0, qb + h)),
                  pl.BlockSpec((t_len, dk), lambda h, d: (0, kb + h)),
                  pl.BlockSpec((t_len, dv), lambda h, d: (0, vb + h)),
                  pl.BlockSpec((t_len, dk), lambda h, d: (0, 0)),
                  pl.BlockSpec((t_len, dk), lambda h, d: (0, 0)),
                  pl.BlockSpec((None, None, n_chunks, dk, dv), lambda h, d: (h, d, 0, 0, 0)),
                  pl.BlockSpec((t_len, dv), lambda h, d: (0, h))],
        out_specs=[pl.BlockSpec((t_len, dk), lambda h, d: (0, h)),
                   pl.BlockSpec((t_len, dk), lambda h, d: (0, h)),
                   pl.BlockSpec((t_len, dv), lambda h, d: (0, h)),
                   pl.BlockSpec((None, None, 8, LANES), lambda h, d: (h, d, 0, 0))],
        out_shape=[jax.ShapeDtypeStruct((t_len, heads * dk), BF16),
                   jax.ShapeDtypeStruct((t_len, heads * dk), BF16),
                   jax.ShapeDtypeStruct((t_len, heads * dv), BF16),
                   jax.ShapeDtypeStruct((heads, 2, 8, LANES), F32)],
        scratch_shapes=[pltpu.VMEM((dk, dv), F32), pltpu.VMEM((8, LANES), F32)],
        compiler_params=_params(("parallel", "arbitrary")),
    )(lg, u, u, u, c2, s2, states, do)


def _mesh_pos():
    return lax.axis_index("x"), lax.axis_index("y"), lax.axis_index("c")


def _all_gather_small(buf, *, name):
    r = buf.shape[0]

    def body(x_ref, o_ref, send_sems, recv_sems, local_sem):
        x, y, c = _mesh_pos()
        me = 4 * x + 2 * y + c
        mine = pltpu.make_async_copy(x_ref, o_ref.at[me], local_sem)
        mine.start()
        copies = []
        for k in range(1, N_DEV):
            px, py, pc = x ^ ((k >> 2) & 1), y ^ ((k >> 1) & 1), c ^ (k & 1)
            cp = pltpu.make_async_remote_copy(
                src_ref=x_ref, dst_ref=o_ref.at[me], send_sem=send_sems.at[k - 1], recv_sem=recv_sems.at[k - 1],
                device_id=(px, py, pc), device_id_type=MESH)
            cp.start()
            copies.append((cp, 4 * px + 2 * py + pc))
        for k, (cp, peer) in enumerate(copies):
            pltpu.make_async_remote_copy(
                src_ref=x_ref, dst_ref=o_ref.at[peer], send_sem=send_sems.at[k], recv_sem=recv_sems.at[k],
                device_id=(x, y, c), device_id_type=MESH).wait_recv()
        for cp, _ in copies:
            cp.wait_send()
        mine.wait()

    return pl.pallas_call(
        body, name=name,
        in_specs=[pl.BlockSpec(memory_space=pltpu.VMEM)],
        out_specs=pl.BlockSpec(memory_space=pltpu.VMEM),
        out_shape=jax.ShapeDtypeStruct((N_DEV, r, LANES), F32),
        scratch_shapes=[pltpu.SemaphoreType.DMA((N_DEV - 1,)), pltpu.SemaphoreType.DMA((N_DEV - 1,)),
                        pltpu.SemaphoreType.DMA],
        compiler_params=pltpu.CompilerParams(vmem_limit_bytes=VMEM_LIMIT),
    )(buf)


def _shard_slice(ref, axis, j, width):
    idx = [slice(None)] * len(ref.shape)
    idx[axis] = pl.ds(pl.multiple_of(j * width, width), width)
    return ref.at[tuple(idx)]


def _cut(ref, shard_axis, *, chip=None, half=None, lead=None):
    shape = ref.shape[1:] if lead is not None else ref.shape
    idx = [slice(None), slice(None)]
    if chip is not None:
        w = shape[shard_axis] // N_CHIPS
        idx[shard_axis] = pl.ds(pl.multiple_of(chip * w, w), w)
    if half is not None:
        hw = shape[1 - shard_axis] // 2
        idx[1 - shard_axis] = pl.ds(pl.multiple_of(half * hw, hw), hw)
    if lead is not None:
        idx = [lead] + idx
    return ref.at[tuple(idx)]


def _wait_recv(ref, send_sem, recv_sem):
    pltpu.make_async_remote_copy(src_ref=ref, dst_ref=ref, send_sem=send_sem, recv_sem=recv_sem,
                                 device_id=_mesh_pos(), device_id_type=MESH).wait_recv()


def _gather_weights(shards, axes, layer, *, name):
    n = len(shards)
    full_shapes = []
    for s, ax in zip(shards, axes):
        shp = list(s.shape[1:])
        shp[ax] *= N_CHIPS
        full_shapes.append(jax.ShapeDtypeStruct(tuple(shp), s.dtype))

    def body(*refs):
        ins, outs = refs[:n], refs[n:2 * n]
        ici_send, ici_recv, d2d_send, d2d_recv, local_sems = refs[2 * n:]
        x, y, c = _mesh_pos()
        chip = 2 * x + y
        pending = []
        for i in range(n):
            mine = pltpu.make_async_copy(ins[i].at[layer], _cut(outs[i], axes[i], chip=chip), local_sems.at[i])
            mine.start()
            pending.append(mine)
        for i in range(n):
            for k in range(1, N_CHIPS):
                cp = pltpu.make_async_remote_copy(
                    src_ref=_cut(ins[i], axes[i], half=c, lead=layer), dst_ref=_cut(outs[i], axes[i], chip=chip, half=c),
                    send_sem=ici_send.at[i, k - 1], recv_sem=ici_recv.at[i, k - 1],
                    device_id=(x ^ (k >> 1), y ^ (k & 1), c), device_id_type=MESH)
                cp.start()
                pending.append(cp)
        for i in range(n):
            for k in range(1, N_CHIPS):
                peer_chip = 2 * (x ^ (k >> 1)) + (y ^ (k & 1))
                landed = _cut(outs[i], axes[i], chip=peer_chip, half=c)
                _wait_recv(landed, ici_send.at[i, k - 1], ici_recv.at[i, k - 1])
                cp = pltpu.make_async_remote_copy(
                    src_ref=landed, dst_ref=landed, send_sem=d2d_send.at[i, k - 1], recv_sem=d2d_recv.at[i, k - 1],
                    device_id=(x, y, 1 - c), device_id_type=MESH)
                cp.start()
                pending.append(cp)
        for i in range(n):
            for k in range(1, N_CHIPS):
                peer_chip = 2 * (x ^ (k >> 1)) + (y ^ (k & 1))
                _wait_recv(_cut(outs[i], axes[i], chip=peer_chip, half=1 - c), d2d_send.at[i, k - 1], d2d_recv.at[i, k - 1])
        for cp in pending[n:]:
            cp.wait_send()
        for cp in pending[:n]:
            cp.wait()

    pairs = pltpu.SemaphoreType.DMA((n, N_CHIPS - 1))
    return pl.pallas_call(
        body, name=name, in_specs=[ANY] * n, out_specs=[ANY] * n, out_shape=full_shapes,
        scratch_shapes=[pairs, pairs, pairs, pairs, pltpu.SemaphoreType.DMA((n,))],
    )(*shards)


def _send_to_sibling(parts, *, name):
    n = len(parts)

    def body(*refs):
        ins, outs = refs[:n], refs[n:2 * n]
        send_sems, recv_sems = refs[2 * n:]
        x, y, c = _mesh_pos()
        sends = []
        for i in range(n):
            cp = pltpu.make_async_remote_copy(
                src_ref=ins[i], dst_ref=outs[i], send_sem=send_sems.at[i], recv_sem=recv_sems.at[i],
                device_id=(x, y, 1 - c), device_id_type=MESH)
            cp.start()
            sends.append(cp)
        for cp in sends:
            cp.wait()

    sems = pltpu.SemaphoreType.DMA((n,))
    return pl.pallas_call(
        body, name=name, in_specs=[ANY] * n, out_specs=[ANY] * n,
        out_shape=[jax.ShapeDtypeStruct(p.shape, p.dtype) for p in parts],
        scratch_shapes=[sems, sems],
    )(*parts)


def _scatter_grads(grads, axes, *, name):
    n = len(grads)
    out_shapes, widths = [], []
    for g, ax in zip(grads, axes):
        shp = list(g.shape)
        shp[ax] //= N_CHIPS
        widths.append(shp[ax])
        out_shapes.append(jax.ShapeDtypeStruct((N_CHIPS,) + tuple(shp), g.dtype))

    def body(*refs):
        ins, outs = refs[:n], refs[n:2 * n]
        send_sems, recv_sems, local_sems = refs[2 * n:]
        x, y, c = _mesh_pos()
        chip = 2 * x + y
        started = []
        for i in range(n):
            mine = pltpu.make_async_copy(_shard_slice(ins[i], axes[i], chip, widths[i]), outs[i].at[0], local_sems.at[i])
            mine.start()
            started.append(mine)
        sends = []
        for i in range(n):
            for k in range(1, N_CHIPS):
                px, py = x ^ (k >> 1), y ^ (k & 1)
                cp = pltpu.make_async_remote_copy(
                    src_ref=_shard_slice(ins[i], axes[i], 2 * px + py, widths[i]), dst_ref=outs[i].at[k],
                    send_sem=send_sems.at[i, k - 1], recv_sem=recv_sems.at[i, k - 1],
                    device_id=(px, py, c), device_id_type=MESH)
                cp.start()
                sends.append(cp)
        for i in range(n):
            for k in range(1, N_CHIPS):
                pltpu.make_async_remote_copy(
                    src_ref=outs[i].at[k], dst_ref=outs[i].at[k],
                    send_sem=send_sems.at[i, k - 1], recv_sem=recv_sems.at[i, k - 1],
                    device_id=(x, y, c), device_id_type=MESH).wait_recv()
        for cp in sends:
            cp.wait_send()
        for cp in started:
            cp.wait()

    return pl.pallas_call(
        body, name=name, in_specs=[ANY] * n, out_specs=[ANY] * n, out_shape=out_shapes,
        scratch_shapes=[pltpu.SemaphoreType.DMA((n, N_CHIPS - 1)), pltpu.SemaphoreType.DMA((n, N_CHIPS - 1)),
                        pltpu.SemaphoreType.DMA((n,))],
    )(*grads)


def _adamw_math(w, g, m, v):
    m = ADAM_B1 * m + (1.0 - ADAM_B1) * g
    v = ADAM_B2 * v + (1.0 - ADAM_B2) * (g * g)
    m_hat = m / (1.0 - ADAM_B1 ** ADAM_STEP)
    v_hat = v / (1.0 - ADAM_B2 ** ADAM_STEP)
    delta = -ADAM_LR * (m_hat / (jnp.sqrt(v_hat) + ADAM_EPS) + ADAM_WD * w)
    return delta, m, v


def _adamw_layer(w3, m3, v3, p, q, layer, prev, *, name):
    nl, rows, width = w3.shape
    tr = _rows_per_tile(rows, width)

    def fn(*t):
        if q is None:
            w, m, v, g = t
        else:
            w, m, v, g, g2 = t
            g = g + g2
        delta, m, v = _adamw_math(w, g, m, v)
        return g, delta, m, v

    ins = [('t', w3, 0, width, layer), ('t', m3, 0, width, layer), ('t', v3, 0, width, layer), ('t', p, 0, width)]
    if q is not None:
        ins.append(('t', q, 0, width))
    outs = [('t', width, F32, layer, nl)] * 4
    aliases = None if prev is None else [(prev[i], i) for i in range(4)]
    return _ew(fn, ins, outs, rows=rows, tr=tr, name=name, aliases=aliases)


def _pack_rows(vec):
    n = vec.shape[0]
    r = -(-n // (8 * LANES)) * 8
    return jnp.pad(vec, (0, r * LANES - n)).reshape(r, LANES)


def kernel(x, c, ctx, c_ctx, ada_w, ada_b, norm_g, w_in, na_rpb, ret_decay_logit, w_proj_na, w_proj_ret, w_out, final_g, loss_target, m_c_ctx, m_ada_w, m_ada_b, m_norm_g, m_w_in, m_na_rpb, m_ret_decay_logit, m_w_proj_na, m_w_proj_ret, m_w_out, m_final_g, v_c_ctx, v_ada_w, v_ada_b, v_norm_g, v_w_in, v_na_rpb, v_ret_decay_logit, v_w_proj_na, v_w_proj_ret, v_w_out, v_final_g):
    depth = w_in.shape[0]
    s_len, d_model = x.shape[1], x.shape[2]
    l_len = ctx.shape[1]
    t_len = s_len + l_len
    na_heads = na_rpb.shape[1]
    ret_heads = ret_decay_logit.shape[2]
    w_na = na_heads * NA_HEAD_DIM
    w_qk = ret_heads * RET_KEY_DIM
    w_v = ret_heads * RET_VAL_DIM
    in_cols = w_in.shape[2] * N_CHIPS
    assert in_cols == 4 * w_na + 2 * w_qk + 2 * w_v + 2 * d_model
    assert x.shape[0] == 1 and s_len % (NA_WIN_ROWS * GRID_W) == 0 and l_len % RET_CHUNK == 0
    off = np.cumsum([0, w_na, w_na, w_na, w_na, w_qk, w_qk, w_v, w_v, d_model, d_model])
    o_naz, o_retq, o_retz, o_gna, o_gret = int(off[3]), int(off[4]), int(off[7]), int(off[8]), int(off[9])
    rows = s_len // GRID_W
    tr = _tile(l_len, 256, 8)
    n0 = s_len // tr
    mod_cols = 3 * d_model
    mod_shard = ada_w.shape[2]

    xi, yi, ci = _mesh_pos()
    me = 4 * xi + 2 * yi + ci
    chip = 2 * xi + yi

    def cast_bf16(w3):
        nl, r, wd = w3.shape
        out = _ew(lambda t: t, [('t', w3.reshape(nl * r, wd), 0, wd)], [('t', wd, BF16)],
                  rows=nl * r, tr=_tile(nl * r, 512, 8), name="cast_w_%dx%d" % (r, wd))[0]
        return out.reshape(nl, r, wd)

    shards = [cast_bf16(w_in), cast_bf16(w_proj_na), cast_bf16(w_proj_ret), cast_bf16(w_out)]
    big_axes = [1, 1, 0, 0]
    full_w = [_gather_weights(shards, big_axes, l, name="gather_weights_%d" % l) for l in range(depth)]

    c_silu = c[0] * _sigmoid(c[0])
    cc_silu = c_ctx * _sigmoid(c_ctx)
    c_all = _all_gather_small(_pack_rows(c_silu), name="gather_c")[:, :d_model // LANES].reshape(N_DEV, d_model)
    a_rows = jnp.concatenate([c_all, cc_silu[None], jnp.zeros((16 - N_DEV - 1, d_model), F32)], axis=0)
    mod_part = jnp.stack([_mm(a_rows, ada_w, b_lead=l, out_dtype=F32, name="ada_fwd_%d" % l) for l in range(depth)])
    mod_all = _all_gather_small(_pack_rows(mod_part.reshape(-1)), name="gather_mod")
    n_mod = depth * 16 * mod_shard
    mod_all = mod_all.reshape(N_DEV, -1)[:, :n_mod].reshape(N_CHIPS, 2, depth, 16, mod_shard)[:, 0]
    mod_all = jnp.transpose(mod_all, (1, 2, 0, 3)).reshape(depth, 16, mod_cols) + ada_b[:, None, :]
    mod_lat = lax.dynamic_index_in_dim(mod_all, me, axis=1, keepdims=False)
    mod_ctx = mod_all[:, N_DEV]

    c2, s2 = _rope_tables(s_len, l_len)
    log_gamma = jax.nn.log_sigmoid(ret_decay_logit)
    x_all = jnp.concatenate([x[0], ctx[0]], axis=0)

    def grp(lat_vec, ctx_vec):
        return jnp.stack([lat_vec, ctx_vec])[:, None, :]

    saved = []
    for l in range(depth):
        shift, scale, gate = [grp(mod_lat[l, i * d_model:(i + 1) * d_model], mod_ctx[l, i * d_model:(i + 1) * d_model])
                              for i in range(3)]
        gs = norm_g[l][None, None, :] * (1.0 + scale)

        def modnorm(xt, gs_t, sh_t):
            r = lax.rsqrt(jnp.mean(xt * xt, axis=-1, keepdims=True) + NORM_EPS)
            return xt * r * gs_t + sh_t

        h, = _ew(modnorm, [('t', x_all, 0, d_model), ('g', gs), ('g', shift)], [('t', d_model, BF16)],
                 rows=t_len, tr=tr, n0=n0, name="modnorm_%d" % l)
        win_f, wpn_f, wpr_f, wout_f = full_w[l]
        u = _mm(h, win_f, name="in_proj_%d" % l)
        bias = _na_bias_table(na_rpb[l], rows, name="na_bias_%d" % l)
        o_na = _na_fwd(u, bias, s_len=s_len, heads=na_heads, name="na_fwd_%d" % l)
        o_ret, states = _ret_fwd(u, c2, s2, log_gamma[l], s_len=s_len, heads=ret_heads, q_off=o_retq, name="ret_fwd_%d" % l)

        def act(o1, z1, o2, z2):
            a1 = o1.astype(F32) * _silu_parts(z1.astype(F32))[0]
            sz = _silu_parts(z2.astype(F32))[0]
            outs = []
            for hh in range(ret_heads):
                sl = slice(hh * RET_VAL_DIM, (hh + 1) * RET_VAL_DIM)
                oh = o2[:, sl]
                r = lax.rsqrt(jnp.mean(oh * oh, axis=-1, keepdims=True) + NORM_EPS)
                outs.append(oh * r * sz[:, sl])
            return a1, jnp.concatenate(outs, axis=-1)

        a_na, a_ret = _ew(act, [('t', o_na, 0, w_na), ('t', u, o_naz // w_na, w_na), ('t', o_ret, 0, w_v), ('t', u, o_retz // w_v, w_v)],
                          [('t', w_na, BF16), ('t', w_v, BF16)], rows=t_len, tr=tr, name="act_%d" % l)
        y_na = _mm(a_na, wpn_f, name="proj_na_%d" % l)
        y_ret = _mm(a_ret, wpr_f, name="proj_ret_%d" % l)

        def merge(y1, y2, g1, g2):
            return _sigmoid(g1.astype(F32)) * y1.astype(F32) + _sigmoid(g2.astype(F32)) * y2.astype(F32)

        merged, = _ew(merge, [('t', y_na, 0, d_model), ('t', y_ret, 0, d_model), ('t', u, o_gna // d_model, d_model), ('t', u, o_gret // d_model, d_model)],
                      [('t', d_model, BF16)], rows=t_len, tr=tr, name="merge_%d" % l)
        out = _mm(merged, wout_f, out_dtype=F32, name="out_proj_%d" % l)
        x_new, = _ew(lambda xt, ot, gt: xt + gt * ot, [('t', x_all, 0, d_model), ('t', out, 0, d_model), ('g', gate)],
                     [('t', d_model, F32)], rows=t_len, tr=tr, n0=n0, name="resid_%d" % l)
        saved.append(dict(x=x_all, h=h, u=u, bias=bias, o_na=o_na, o_ret=o_ret, states=states, a_na=a_na, a_ret=a_ret,
                          y_na=y_na, y_ret=y_ret, merged=merged, out=out, gate=gate, gs=gs, scale=scale))
        x_all = x_new

    def final(xt, tt, gt):
        r = lax.rsqrt(jnp.mean(xt * xt, axis=-1, keepdims=True) + NORM_EPS)
        xh = xt * r
        e = xh * gt - tt
        dy = e * (1.0 / d_model)
        dyg = dy * gt
        dx = r * (dyg - xh * jnp.mean(dyg * xh, axis=-1, keepdims=True))
        return dx, _rsum(dy * xh), _rsum(e * e)

    dx_lat, d_final_g, loss_cols = _ew(final, [('t', x_all, 0, d_model), ('t', loss_target[0], 0, d_model), ('g', final_g[None, None, :])],
                                       [('t', d_model, F32), ('r', d_model, 1), ('r', d_model, 1)], rows=s_len, tr=tr, name="final")
    loss_part = (0.5 / d_model) * jnp.sum(loss_cols)
    dx_all = jnp.concatenate([dx_lat, jnp.zeros((l_len, d_model), F32)], axis=0)

    big_w = [(w_in, m_w_in, v_w_in), (w_proj_na, m_w_proj_na, v_w_proj_na), (w_proj_ret, m_w_proj_ret, v_w_proj_ret), (w_out, m_w_out, v_w_out)]
    big_res = [None] * 4
    small = dict(dmod_lat=[None] * depth, dmod_ctx=[None] * depth, dnorm_g=[None] * depth, drpb=[None] * depth, ddecay=[None] * depth)
    for l in reversed(range(depth)):
        sv = saved[l]
        win_f, wpn_f, wpr_f, wout_f = full_w[l]

        def resid_bwd(dxt, ot, gt):
            return gt * dxt, _rsum(dxt * ot)

        dout, dgate = _ew(resid_bwd, [('t', dx_all, 0, d_model), ('t', sv['out'], 0, d_model), ('g', sv['gate'])],
                          [('t', d_model, BF16), ('r', d_model, 2)], rows=t_len, tr=tr, n0=n0, name="resid_bwd_%d" % l)
        dmerged = _mm(dout, wout_f, tb=True, name="out_proj_dx_%d" % l)
        g_wout = _mm(sv['merged'], dout, ta=True, tm=512, tk=t_len, name="out_proj_dw_%d" % l)

        def merge_bwd(dm, y1, y2, g1, g2):
            dm = dm.astype(F32)
            s1, s2_ = _sigmoid(g1.astype(F32)), _sigmoid(g2.astype(F32))
            return dm * s1, dm * s2_, dm * y1.astype(F32) * s1 * (1.0 - s1), dm * y2.astype(F32) * s2_ * (1.0 - s2_)

        u = sv['u']
        dy_na, dy_ret, dg_na, dg_ret = _ew(
            merge_bwd, [('t', dmerged, 0, d_model), ('t', sv['y_na'], 0, d_model), ('t', sv['y_ret'], 0, d_model),
                        ('t', u, o_gna // d_model, d_model), ('t', u, o_gret // d_model, d_model)],
            [('t', d_model, BF16)] * 4, rows=t_len, tr=tr, name="merge_bwd_%d" % l)
        da_na = _mm(dy_na, wpn_f, tb=True, name="proj_na_dx_%d" % l)
        g_wpn = _mm(sv['a_na'], dy_na, ta=True, tm=512, tk=t_len, name="proj_na_dw_%d" % l)
        da_ret = _mm(dy_ret, wpr_f, tb=True, name="proj_ret_dx_%d" % l)
        g_wpr = _mm(sv['a_ret'], dy_ret, ta=True, tm=512, tk=t_len, name="proj_ret_dw_%d" % l)

        def act_bwd(da1, o1, z1, da2, o2, z2):
            da1, da2 = da1.astype(F32), da2.astype(F32)
            si1, ds1 = _silu_parts(z1.astype(F32))
            si2, ds2 = _silu_parts(z2.astype(F32))
            do1 = da1 * si1
            dz1 = da1 * o1.astype(F32) * ds1
            dn = da2 * si2
            do2, dz2 = [], []
            for hh in range(ret_heads):
                sl = slice(hh * RET_VAL_DIM, (hh + 1) * RET_VAL_DIM)
                oh = o2[:, sl]
                r = lax.rsqrt(jnp.mean(oh * oh, axis=-1, keepdims=True) + NORM_EPS)
                nh = oh * r
                dz2.append(da2[:, sl] * nh * ds2[:, sl])
                do2.append(r * (dn[:, sl] - nh * jnp.mean(dn[:, sl] * nh, axis=-1, keepdims=True)))
            return do1, dz1, jnp.concatenate(do2, axis=-1), jnp.concatenate(dz2, axis=-1)

        do_na, dz_na, do_ret, dz_ret = _ew(
            act_bwd, [('t', da_na, 0, w_na), ('t', sv['o_na'], 0, w_na), ('t', u, o_naz // w_na, w_na),
                      ('t', da_ret, 0, w_v), ('t', sv['o_ret'], 0, w_v), ('t', u, o_retz // w_v, w_v)],
            [('t', w_na, BF16), ('t', w_na, BF16), ('t', w_v, BF16), ('t', w_v, BF16)], rows=t_len, tr=tr, name="act_bwd_%d" % l)
        dq_na, dk_na, dv_na, dbias = _na_bwd(u, sv['bias'], sv['o_na'], do_na, s_len=s_len, heads=na_heads, name="na_bwd_%d" % l)
        small['drpb'][l] = _rpb_grad(dbias, name="rpb_grad_%d" % l)
        dq_r, dk_r, dv_r, dlg = _ret_bwd(u, c2, s2, log_gamma[l], sv['states'], do_ret, s_len=s_len, heads=ret_heads,
                                         q_off=o_retq, name="ret_bwd_%d" % l)
        small['ddecay'][l] = jnp.transpose(dlg[:, :, 0, 0]) * _sigmoid(-ret_decay_logit[l])
        du = jnp.concatenate([dq_na, dk_na, dv_na, dz_na, dq_r, dk_r, dv_r, dz_ret, dg_na, dg_ret], axis=1)
        dh = _mm(du, win_f, tb=True, out_dtype=F32, tn=1024, name="in_proj_dx_%d" % l)
        g_win = _mm(sv['h'], du, ta=True, tm=512, tk=t_len, name="in_proj_dw_%d" % l)

        def modnorm_bwd(xt, dht, dxt, gs_t):
            r = lax.rsqrt(jnp.mean(xt * xt, axis=-1, keepdims=True) + NORM_EPS)
            xh = xt * r
            dhg = dht * gs_t
            dx = r * (dhg - xh * jnp.mean(dhg * xh, axis=-1, keepdims=True)) + dxt
            return dx, _rsum(dht), _rsum(dht * xh)

        dx_all, dshift, dgs = _ew(modnorm_bwd, [('t', sv['x'], 0, d_model), ('t', dh, 0, d_model), ('t', dx_all, 0, d_model), ('g', sv['gs'])],
                                  [('t', d_model, F32), ('r', d_model, 2), ('r', d_model, 2)], rows=t_len, tr=tr, n0=n0, name="modnorm_bwd_%d" % l)
        dscale = dgs * norm_g[l][None, None, :]
        small['dnorm_g'][l] = jnp.sum(dgs * (1.0 + sv['scale']), axis=(0, 1))
        dmod = jnp.concatenate([dshift, dscale, dgate], axis=-1)[:, 0]
        small['dmod_lat'][l], small['dmod_ctx'][l] = dmod[0], dmod[1]

        grads_l = [g_win, g_wpn, g_wpr, g_wout]
        half_sz = [g.shape[1 - ax] // 2 for g, ax in zip(grads_l, big_axes)]
        mine = [lax.dynamic_slice_in_dim(g, ci * hs, hs, axis=1 - ax) for g, hs, ax in zip(grads_l, half_sz, big_axes)]
        to_send = [lax.dynamic_slice_in_dim(g, (1 - ci) * hs, hs, axis=1 - ax) for g, hs, ax in zip(grads_l, half_sz, big_axes)]
        theirs = _send_to_sibling(to_send, name="pair_exchange_%d" % l)
        pair = []
        for i in range(4):
            pr, pw = mine[i].shape
            s, = _ew(lambda a, b: a.astype(F32) + b.astype(F32), [('t', mine[i], 0, pw), ('t', theirs[i], 0, pw)], [('t', pw, BF16)],
                     rows=pr, tr=_rows_per_tile(pr, pw), name="sum_pair_%d_%d" % (i, l))
            pair.append(s)
        recv = _scatter_grads(pair, big_axes, name="scatter_grads_%d" % l)
        parts = []
        for i, rbuf in enumerate(recv):
            _, pr, pw = rbuf.shape
            p, = _ew(lambda a, b, c_, d: ((a.astype(F32) + b.astype(F32)) + c_.astype(F32)) + d.astype(F32),
                     [('t', rbuf, 0, pw, k) for k in range(N_CHIPS)], [('t', pw, F32)],
                     rows=pr, tr=_rows_per_tile(pr, pw), name="sum_chips_%d_%d" % (i, l))
            parts.append(p)
        others = _send_to_sibling(parts, name="share_halves_%d" % l)
        shard_g = [jnp.where(ci == 0, jnp.concatenate([p, o], axis=1 - ax), jnp.concatenate([o, p], axis=1 - ax))
                   for p, o, ax in zip(parts, others, big_axes)]
        for i in range(4):
            w3, m3, v3 = big_w[i]
            big_res[i] = _adamw_layer(w3, m3, v3, shard_g[i], None, l, big_res[i], name="adamw_big_%d_%d" % (i, l))

    grad_x = dx_all[:s_len][None]

    drpb = jnp.stack(small['drpb']).reshape(-1)
    ddecay = jnp.stack(small['ddecay']).reshape(-1)
    pieces = [jnp.stack(small['dmod_lat']).reshape(-1), jnp.stack(small['dmod_ctx']).reshape(-1),
              jnp.stack(small['dnorm_g']).reshape(-1), d_final_g.reshape(-1), drpb, ddecay, loss_part[None]]
    sizes = [int(p.shape[0]) for p in pieces]
    pads = [-(-s // LANES) * LANES for s in sizes]
    packed = jnp.concatenate([jnp.pad(p, (0, pd - s)) for p, s, pd in zip(pieces, sizes, pads)])
    gathered = _all_gather_small(_pack_rows(packed), name="gather_small_grads")
    r_small = gathered.shape[1]

    def sum8(*t):
        acc = t[0]
        for other in t[1:]:
            acc = acc + other
        return acc

    total, = _ew(sum8, [('t', gathered, 0, LANES, k) for k in range(N_DEV)], [('t', LANES, F32)], rows=r_small, tr=r_small, name="sum_devices")
    total = total.reshape(-1)
    starts = np.cumsum([0] + pads)
    g_mod_lat_sum, g_mod_ctx, g_norm_g, g_final_g, g_rpb, g_decay, loss = [total[starts[i]:starts[i] + sizes[i]] for i in range(len(pieces))]
    loss = loss[0]
    g_ada_b = (g_mod_lat_sum + g_mod_ctx).reshape(depth, mod_cols)
    g_mod_ctx = g_mod_ctx.reshape(depth, mod_cols)
    dmod_lat_all = gathered.reshape(N_DEV, -1)[:, :depth * mod_cols].reshape(N_DEV, depth, mod_cols)

    ada_res = None
    dcc_part = jnp.zeros((16, d_model), F32)
    for l in reversed(range(depth)):
        lat_cols = lax.dynamic_slice_in_dim(dmod_lat_all[:, l], chip * mod_shard, mod_shard, axis=1)
        ctx_cols = lax.dynamic_slice_in_dim(g_mod_ctx[l], chip * mod_shard, mod_shard, axis=0)
        d_rows = jnp.concatenate([lat_cols, ctx_cols[None], jnp.zeros((16 - N_DEV - 1, mod_shard), F32)], axis=0)
        g_ada = _mm(a_rows, d_rows, ta=True, out_dtype=F32, tm=512, name="ada_dw_%d" % l)
        ada_res = _adamw_layer(ada_w, m_ada_w, v_ada_w, g_ada, None, l, ada_res, name="adamw_ada_%d" % l)
        c_rows = jnp.concatenate([ctx_cols[None], jnp.zeros((15, mod_shard), F32)], axis=0)
        dcc_part = dcc_part + _mm(c_rows, ada_w, tb=True, b_lead=l, out_dtype=F32, name="ada_dc_%d" % l)
    dcc_all = _all_gather_small(_pack_rows(dcc_part[0]), name="gather_dcc")[:, :d_model // LANES].reshape(N_CHIPS, 2, d_model)[:, 0]
    dcc = ((dcc_all[0] + dcc_all[1]) + dcc_all[2]) + dcc_all[3]
    sg = _sigmoid(c_ctx)
    g_c_ctx = dcc * (sg * (1.0 + c_ctx * (1.0 - sg)))

    small_w = [(c_ctx, m_c_ctx, v_c_ctx, g_c_ctx), (ada_b, m_ada_b, v_ada_b, g_ada_b),
               (norm_g, m_norm_g, v_norm_g, g_norm_g), (na_rpb, m_na_rpb, v_na_rpb, g_rpb),
               (ret_decay_logit, m_ret_decay_logit, v_ret_decay_logit, g_decay), (final_g, m_final_g, v_final_g, g_final_g)]
    sw_sizes = [int(np.prod(t[0].shape)) for t in small_w]
    sw_pads = [-(-s // LANES) * LANES for s in sw_sizes]

    def pack(j):
        return _pack_rows(jnp.concatenate([jnp.pad(t[j].reshape(-1), (0, pd - s)) for t, s, pd in zip(small_w, sw_sizes, sw_pads)]))

    pw_, pm_, pv_, pg_ = pack(0), pack(1), pack(2), pack(3)
    sw_out = _ew(lambda w, m, v, g: (g,) + _adamw_math(w, g, m, v),
                 [('t', pw_, 0, LANES), ('t', pm_, 0, LANES), ('t', pv_, 0, LANES), ('t', pg_, 0, LANES)],
                 [('t', LANES, F32)] * 4, rows=pw_.shape[0], tr=pw_.shape[0], name="adamw_small")
    sw_starts = np.cumsum([0] + sw_pads)

    def unpack(arr, i):
        return arr.reshape(-1)[sw_starts[i]:sw_starts[i] + sw_sizes[i]].reshape(small_w[i][0].shape)

    sm = [[unpack(sw_out[j], i) for i in range(len(small_w))] for j in range(4)]
    def ordered(j):
        return [sm[j][0], ada_res[j], sm[j][1], sm[j][2], big_res[0][j], sm[j][3], sm[j][4],
                big_res[1][j], big_res[2][j], big_res[3][j], sm[j][5]]

    return (loss, grad_x, *ordered(0), *ordered(1), *ordered(2), *ordered(3))
```

```python
import functools
import math

import numpy as np
import jax
import jax.numpy as jnp
from jax import lax
from jax.experimental import pallas as pl
from jax.experimental.pallas import tpu as pltpu

GRID_W = 64
NA_HEAD_DIM = 128
NA_WIN_ROWS = 8
NA_WIN_COLS = 16
RET_KEY_DIM = 128
RET_VAL_DIM = 256
RET_CHUNK = 128
ROPE_BASE = 10000.0
NORM_EPS = 1e-6
MASK_VALUE = -1e30
ADAM_LR = 0.001
ADAM_B1 = 0.9
ADAM_B2 = 0.999
ADAM_EPS = 1e-08
ADAM_WD = 0.01
ADAM_STEP = 10

N_CHIPS = 4
N_DEV = 8
LANES = 128
VMEM_LIMIT = 56 * 1024 * 1024
BF16 = jnp.bfloat16
F32 = jnp.float32
MESH = pl.DeviceIdType.MESH
ANY = pl.BlockSpec(memory_space=pl.ANY)


def _tile(dim, pref, align=LANES):
    if dim <= pref:
        return dim
    t = (pref // align) * align
    while t >= align:
        if dim % t == 0:
            return t
        t -= align
    return dim


def _rows_per_tile(rows, width, tile_bytes=1 << 20):
    return _tile(rows, max(8, tile_bytes // (4 * width)), 8)


def _params(sem):
    return pltpu.CompilerParams(dimension_semantics=sem, vmem_limit_bytes=VMEM_LIMIT)


def _sigmoid(x):
    return 1.0 / (1.0 + jnp.exp(-x))


def _dot(a, b, ca, cb):
    return lax.dot_general(a, b, (((ca,), (cb,)), ((), ())), preferred_element_type=F32)


def _mm(a, b, *, ta=False, tb=False, a_lead=None, b_lead=None, out_dtype=BF16, tm=768, tn=512, tk=2048, name):
    ash = a.shape[1:] if a_lead is not None else a.shape
    bsh = b.shape[1:] if b_lead is not None else b.shape
    m, k = (ash[1], ash[0]) if ta else ash
    n, k2 = bsh if tb else (bsh[1], bsh[0])
    assert k == k2, (name, ash, bsh)
    tm, tn, tk = _tile(m, tm), _tile(n, tn), _tile(k, tk)
    nk = k // tk

    def lead(spec_shape, imap, l):
        if l is None:
            return pl.BlockSpec(spec_shape, imap)
        return pl.BlockSpec((None,) + spec_shape, lambda i, j, kk: (l,) + imap(i, j, kk))

    a_spec = lead((tk, tm), lambda i, j, kk: (kk, i), a_lead) if ta else lead((tm, tk), lambda i, j, kk: (i, kk), a_lead)
    b_spec = lead((tn, tk), lambda i, j, kk: (j, kk), b_lead) if tb else lead((tk, tn), lambda i, j, kk: (kk, j), b_lead)
    ca, cb = (0 if ta else 1), (1 if tb else 0)

    def body(a_ref, b_ref, o_ref, *scratch):
        part = _dot(a_ref[...].astype(BF16), b_ref[...].astype(BF16), ca, cb)
        if nk == 1:
            o_ref[...] = part.astype(o_ref.dtype)
            return
        acc_ref, = scratch
        kk = pl.program_id(2)

        @pl.when(kk == 0)
        def _():
            acc_ref[...] = part

        @pl.when(kk > 0)
        def _():
            acc_ref[...] += part

        @pl.when(kk == nk - 1)
        def _():
            o_ref[...] = acc_ref[...].astype(o_ref.dtype)

    return pl.pallas_call(
        body, name=name, grid=(m // tm, n // tn, nk),
        in_specs=[a_spec, b_spec],
        out_specs=pl.BlockSpec((tm, tn), lambda i, j, kk: (i, j)),
        out_shape=jax.ShapeDtypeStruct((m, n), out_dtype),
        scratch_shapes=[] if nk == 1 else [pltpu.VMEM((tm, tn), F32)],
        compiler_params=_params(("parallel", "parallel", "arbitrary")),
    )(a, b)


def _ew(fn, ins, outs, *, rows, tr, name, n0=None, aliases=None):
    assert rows % tr == 0, (name, rows, tr)
    nt = rows // tr

    def grp(i):
        return 0 if n0 is None else jnp.where(i < n0, 0, 1)

    in_specs, args = [], []
    for spec in ins:
        if spec[0] == 't':
            arr, cb, w = spec[1], spec[2], spec[3]
            l = spec[4] if len(spec) > 4 else None
            if l is None:
                in_specs.append(pl.BlockSpec((tr, w), functools.partial(lambda i, cb: (i, cb), cb=cb)))
            else:
                in_specs.append(pl.BlockSpec((None, tr, w), functools.partial(lambda i, cb, l: (l, i, cb), cb=cb, l=l)))
            args.append(arr)
        else:
            arr = spec[1]
            g = arr.shape[0]
            if g == 1:
                in_specs.append(pl.BlockSpec((None, 1, arr.shape[2]), lambda i: (0, 0, 0)))
            else:
                in_specs.append(pl.BlockSpec((None, 1, arr.shape[2]), lambda i: (grp(i), 0, 0)))
            args.append(arr)
    out_specs, out_shapes, is_red = [], [], []
    for spec in outs:
        if spec[0] == 't':
            w, dt = spec[1], spec[2]
            if len(spec) > 3:
                l, nl = spec[3], spec[4]
                out_specs.append(pl.BlockSpec((None, tr, w), functools.partial(lambda i, l: (l, i, 0), l=l)))
                out_shapes.append(jax.ShapeDtypeStruct((nl, rows, w), dt))
            else:
                out_specs.append(pl.BlockSpec((tr, w), lambda i: (i, 0)))
                out_shapes.append(jax.ShapeDtypeStruct((rows, w), dt))
            is_red.append(False)
        else:
            w, g = spec[1], spec[2]
            if g == 1:
                out_specs.append(pl.BlockSpec((None, 1, w), lambda i: (0, 0, 0)))
            else:
                out_specs.append(pl.BlockSpec((None, 1, w), lambda i: (grp(i), 0, 0)))
            out_shapes.append(jax.ShapeDtypeStruct((g, 1, w), F32))
            is_red.append(True)
    n_in = len(ins)
    n_alias = 0 if aliases is None else len(aliases)

    def body(*refs):
        in_refs = refs[:n_in]
        out_refs = refs[n_in + n_alias:]
        res = fn(*[r[...] for r in in_refs])
        if not isinstance(res, (tuple, list)):
            res = (res,)
        i = pl.program_id(0)
        first = (i == 0) if n0 is None else ((i == 0) | (i == n0))
        for o_ref, val, red in zip(out_refs, res, is_red):
            if not red:
                o_ref[...] = val.astype(o_ref.dtype)
            else:
                @pl.when(first)
                def _(o_ref=o_ref, val=val):
                    o_ref[...] = val

                @pl.when(jnp.logical_not(first))
                def _(o_ref=o_ref, val=val):
                    o_ref[...] += val

    io_alias = {}
    if aliases is not None:
        for a_idx, (arr, o_idx) in enumerate(aliases):
            in_specs.append(ANY)
            args.append(arr)
            io_alias[n_in + a_idx] = o_idx
    has_red = any(is_red)
    return pl.pallas_call(
        body, name=name, grid=(nt,), in_specs=in_specs, out_specs=out_specs, out_shape=out_shapes,
        input_output_aliases=io_alias,
        compiler_params=_params(("arbitrary",) if has_red else ("parallel",)),
    )(*args)


def _rsum(v):
    return jnp.sum(v, axis=0, keepdims=True)


def _silu_parts(z):
    sg = _sigmoid(z)
    return z * sg, sg * (1.0 + z * (1.0 - sg))


def _na_bias_table(rpb, rows, *, name):
    kh, kw = NA_WIN_ROWS, NA_WIN_COLS
    assert rows >= kh
    heads = rpb.shape[0]
    e1, e2 = _na_onehots()
    rpb16 = jnp.pad(rpb, ((0, 0), (0, 16 - rpb.shape[1]), (0, LANES - rpb.shape[2])))

    def body(r_ref, e1_ref, e2_ref, o_ref):
        e1b = e1_ref[...].astype(BF16)
        y = sum(_dot(e1b, part, 0, 0) for part in _split3(r_ref[...]))
        e2b = e2_ref[...].astype(BF16)
        o_ref[...] = sum(_dot(part, e2b, 1, 1) for part in _split3(y))

    z = pl.pallas_call(
        body, name=name, grid=(heads,),
        in_specs=[pl.BlockSpec((None, 16, LANES), lambda h: (h, 0, 0)),
                  pl.BlockSpec(e1.shape, lambda h: (0, 0)), pl.BlockSpec(e2.shape, lambda h: (0, 0))],
        out_specs=pl.BlockSpec((None, kh * kh, GRID_W * GRID_W), lambda h: (h, 0, 0)),
        out_shape=jax.ShapeDtypeStruct((heads, kh * kh, GRID_W * GRID_W), F32),
        compiler_params=_params(("parallel",)),
    )(rpb16, e1, e2)
    cidx = np.arange(GRID_W)
    c0 = np.clip(cidx - kw // 2, 0, GRID_W - kw)
    col_in = (cidx[None, :] >= c0[:, None]) & (cidx[None, :] < c0[:, None] + kw)
    bias = z.reshape(heads, kh, kh, GRID_W, GRID_W).transpose(0, 1, 3, 2, 4)
    bias = jnp.where(col_in[None, None, :, None, :], bias, MASK_VALUE)
    return bias.reshape(heads, kh, GRID_W, kh * GRID_W)


def _na_onehots():
    kh, kw = NA_WIN_ROWS, NA_WIN_COLS
    cidx = np.arange(GRID_W)
    dc = cidx[None, :] - cidx[:, None] + (kw - 1)
    e2 = np.zeros((GRID_W * GRID_W, LANES), np.float32)
    ok = (dc >= 0) & (dc <= 2 * kw - 2)
    cq, ck = np.nonzero(ok)
    e2[cq * GRID_W + ck, dc[cq, ck]] = 1.0
    dr = np.arange(kh)[None, :] - np.arange(kh)[:, None] + (kh - 1)
    e1 = np.zeros((16, kh * kh), np.float32)
    dl, kr = np.nonzero(np.ones_like(dr))
    e1[dr[dl, kr], dl * kh + kr] = 1.0
    return jnp.asarray(e1), jnp.asarray(e2)


def _na_row_scores(q, kl, kc, bias, scale):
    s_loc = _dot(q, kl, 1, 1) * scale + bias
    s_ctx = _dot(q, kc, 1, 1) * scale
    m = jnp.maximum(jnp.max(s_loc, axis=-1, keepdims=True), jnp.max(s_ctx, axis=-1, keepdims=True))
    p_loc = jnp.exp(s_loc - m)
    p_ctx = jnp.exp(s_ctx - m)
    den = jnp.sum(p_loc, axis=-1, keepdims=True) + jnp.sum(p_ctx, axis=-1, keepdims=True)
    return p_loc, p_ctx, den


def _na_fwd(u, bias, *, s_len, heads, name):
    t_len = u.shape[0]
    rows = s_len // GRID_W
    nloc = NA_WIN_ROWS * GRID_W
    scale = NA_HEAD_DIM ** -0.5
    hd = NA_HEAD_DIM

    def body(q_ref, k_ref, v_ref, b_ref, o_ref):
        kc = k_ref[s_len:t_len, :]
        vc = v_ref[s_len:t_len, :]

        def row(r, carry):
            r0 = jnp.clip(r - NA_WIN_ROWS // 2, 0, rows - NA_WIN_ROWS)
            qs = pl.multiple_of(r * GRID_W, GRID_W)
            ks = pl.multiple_of(r0 * GRID_W, GRID_W)
            q = q_ref[pl.ds(qs, GRID_W), :]
            kl = k_ref[pl.ds(ks, nloc), :]
            vl = v_ref[pl.ds(ks, nloc), :]
            p_loc, p_ctx, den = _na_row_scores(q, kl, kc, b_ref[r - r0], scale)
            o = _dot(p_loc.astype(BF16), vl, 1, 0) + _dot(p_ctx.astype(BF16), vc, 1, 0)
            o_ref[pl.ds(qs, GRID_W), :] = (o / den).astype(o_ref.dtype)
            return carry

        lax.fori_loop(0, rows, row, 0)
        qc = q_ref[s_len:t_len, :]
        s = _dot(qc, kc, 1, 1) * scale
        p = jnp.exp(s - jnp.max(s, axis=-1, keepdims=True))
        o = _dot(p.astype(BF16), vc, 1, 0) / jnp.sum(p, axis=-1, keepdims=True)
        o_ref[s_len:t_len, :] = o.astype(o_ref.dtype)

    col = lambda off: pl.BlockSpec((t_len, hd), functools.partial(lambda h, off: (0, off + h), off=off))
    return pl.pallas_call(
        body, name=name, grid=(heads,),
        in_specs=[col(0), col(heads), col(2 * heads),
                  pl.BlockSpec((None, NA_WIN_ROWS, GRID_W, nloc), lambda h: (h, 0, 0, 0))],
        out_specs=pl.BlockSpec((t_len, hd), lambda h: (0, h)),
        out_shape=jax.ShapeDtypeStruct((t_len, heads * hd), BF16),
        compiler_params=_params(("parallel",)),
    )(u, u, u, bias)


def _na_bwd(u, bias, o, do, *, s_len, heads, name):
    t_len = u.shape[0]
    rows = s_len // GRID_W
    nloc = NA_WIN_ROWS * GRID_W
    scale = NA_HEAD_DIM ** -0.5
    hd = NA_HEAD_DIM

    def body(q_ref, k_ref, v_ref, b_ref, o_ref, do_ref, dq_ref, dk_ref, dv_ref, db_ref, dk_acc, dv_acc):
        kc = k_ref[s_len:t_len, :]
        vc = v_ref[s_len:t_len, :]
        dk_acc[...] = jnp.zeros_like(dk_acc)
        dv_acc[...] = jnp.zeros_like(dv_acc)
        db_ref[...] = jnp.zeros_like(db_ref)

        def row(r, carry):
            r0 = jnp.clip(r - NA_WIN_ROWS // 2, 0, rows - NA_WIN_ROWS)
            dl = r - r0
            qs = pl.multiple_of(r * GRID_W, GRID_W)
            ks = pl.multiple_of(r0 * GRID_W, GRID_W)
            q = q_ref[pl.ds(qs, GRID_W), :]
            kl = k_ref[pl.ds(ks, nloc), :]
            vl = v_ref[pl.ds(ks, nloc), :]
            dout = do_ref[pl.ds(qs, GRID_W), :]
            out = o_ref[pl.ds(qs, GRID_W), :]
            p_loc, p_ctx, den = _na_row_scores(q, kl, kc, b_ref[dl], scale)
            inv = 1.0 / den
            p_loc = p_loc * inv
            p_ctx = p_ctx * inv
            dlt = jnp.sum(dout.astype(F32) * out.astype(F32), axis=-1, keepdims=True)
            ds_loc = p_loc * (_dot(dout, vl, 1, 1) - dlt)
            ds_ctx = p_ctx * (_dot(dout, vc, 1, 1) - dlt)
            db_ref[dl] += ds_loc
            ds_loc_b = ds_loc.astype(BF16)
            ds_ctx_b = ds_ctx.astype(BF16)
            dq = (_dot(ds_loc_b, kl, 1, 0) + _dot(ds_ctx_b, kc, 1, 0)) * scale
            dq_ref[pl.ds(qs, GRID_W), :] = dq.astype(dq_ref.dtype)
            dk_acc[pl.ds(ks, nloc), :] += _dot(ds_loc_b, q, 0, 0) * scale
            dv_acc[pl.ds(ks, nloc), :] += _dot(p_loc.astype(BF16), dout, 0, 0)
            dk_acc[s_len:t_len, :] += _dot(ds_ctx_b, q, 0, 0) * scale
            dv_acc[s_len:t_len, :] += _dot(p_ctx.astype(BF16), dout, 0, 0)
            return carry

        lax.fori_loop(0, rows, row, 0)
        qc = q_ref[s_len:t_len, :]
        dout = do_ref[s_len:t_len, :]
        out = o_ref[s_len:t_len, :]
        s = _dot(qc, kc, 1, 1) * scale
        p = jnp.exp(s - jnp.max(s, axis=-1, keepdims=True))
        p = p / jnp.sum(p, axis=-1, keepdims=True)
        dlt = jnp.sum(dout.astype(F32) * out.astype(F32), axis=-1, keepdims=True)
        ds = (p * (_dot(dout, vc, 1, 1) - dlt)).astype(BF16)
        dq_ref[s_len:t_len, :] = (_dot(ds, kc, 1, 0) * scale).astype(dq_ref.dtype)
        dk_acc[s_len:t_len, :] += _dot(ds, qc, 0, 0) * scale
        dv_acc[s_len:t_len, :] += _dot(p.astype(BF16), dout, 0, 0)
        dk_ref[...] = dk_acc[...].astype(dk_ref.dtype)
        dv_ref[...] = dv_acc[...].astype(dv_ref.dtype)

    col = lambda off: pl.BlockSpec((t_len, hd), functools.partial(lambda h, off: (0, off + h), off=off))
    tbl = pl.BlockSpec((None, NA_WIN_ROWS, GRID_W, nloc), lambda h: (h, 0, 0, 0))
    tok = jax.ShapeDtypeStruct((t_len, heads * hd), BF16)
    return pl.pallas_call(
        body, name=name, grid=(heads,),
        in_specs=[col(0), col(heads), col(2 * heads), tbl, col(0), col(0)],
        out_specs=[col(0), col(0), col(0), tbl],
        out_shape=[tok, tok, tok, jax.ShapeDtypeStruct(bias.shape, F32)],
        scratch_shapes=[pltpu.VMEM((t_len, hd), F32), pltpu.VMEM((t_len, hd), F32)],
        compiler_params=_params(("parallel",)),
    )(u, u, u, bias, o, do)


def _split3(x):
    hi = x.astype(BF16)
    r1 = x - hi.astype(F32)
    mid = r1.astype(BF16)
    lo = (r1 - mid.astype(F32)).astype(BF16)
    return hi, mid, lo


def _rpb_grad(dbias, *, name):
    heads = dbias.shape[0]
    kh = NA_WIN_ROWS
    e1, e2 = _na_onehots()
    x = dbias.reshape(heads, kh, GRID_W, kh, GRID_W).transpose(0, 1, 3, 2, 4).reshape(heads, kh * kh, GRID_W * GRID_W)

    def body(x_ref, e1_ref, e2_ref, o_ref):
        e2b = e2_ref[...].astype(BF16)
        y = sum(_dot(part, e2b, 1, 0) for part in _split3(x_ref[...]))
        e1b = e1_ref[...].astype(BF16)
        o_ref[...] = sum(_dot(e1b, part, 1, 0) for part in _split3(y))

    out = pl.pallas_call(
        body, name=name, grid=(heads,),
        in_specs=[pl.BlockSpec((None, kh * kh, GRID_W * GRID_W), lambda h: (h, 0, 0)),
                  pl.BlockSpec(e1.shape, lambda h: (0, 0)), pl.BlockSpec(e2.shape, lambda h: (0, 0))],
        out_specs=pl.BlockSpec((None, 16, LANES), lambda h: (h, 0, 0)),
        out_shape=jax.ShapeDtypeStruct((heads, 16, LANES), F32),
        compiler_params=_params(("parallel",)),
    )(x, e1, e2)
    return out[:, :2 * kh - 1, :2 * NA_WIN_COLS - 1]


def _rope_tables(s_len, l_len):
    nf = RET_KEY_DIM // 4
    t = np.arange(s_len)
    row = (t // GRID_W).astype(np.float32)
    colp = (t % GRID_W).astype(np.float32)
    inv_freq = jnp.asarray(ROPE_BASE, F32) ** (-jnp.arange(nf, dtype=F32) / nf)
    ang = jnp.concatenate([jnp.asarray(row)[:, None] * inv_freq, jnp.asarray(colp)[:, None] * inv_freq], axis=-1)
    cos, sin = jnp.cos(ang), jnp.sin(ang)
    c2 = jnp.concatenate([cos, cos], axis=-1)
    s2 = jnp.concatenate([-sin, sin], axis=-1)
    c2 = jnp.concatenate([c2, jnp.ones((l_len, RET_KEY_DIM), F32)], axis=0)
    s2 = jnp.concatenate([s2, jnp.zeros((l_len, RET_KEY_DIM), F32)], axis=0)
    return c2, s2


def _rope(x, c2, s2):
    return x * c2 + pltpu.roll(x, RET_KEY_DIM // 2, 1) * s2


def _rope_t(d, c2, s2):
    return d * c2 + pltpu.roll(d * s2, RET_KEY_DIM // 2, 1)


def _ret_decays(lg, direction):
    cs = RET_CHUNK
    i_col = lax.broadcasted_iota(jnp.int32, (cs, 1), 0)
    p_col = jnp.where(direction == 0, i_col, cs - 1 - i_col).astype(F32)
    pi = lax.broadcasted_iota(jnp.int32, (cs, cs), 0)
    pj = lax.broadcasted_iota(jnp.int32, (cs, cs), 1)
    diff = jnp.where(direction == 0, pi - pj, pj - pi).astype(F32)
    dm = jnp.where(diff >= 0, jnp.exp(jnp.maximum(diff, 0.0) * lg), 0.0)
    qdec = jnp.exp((p_col + 1.0) * lg)
    kdec = jnp.exp((cs - 1.0 - p_col) * lg)
    cd = jnp.exp(jnp.full((1, 1), cs, F32) * lg)
    return p_col, dm, qdec, kdec, cd


def _ret_chunk_index(t, direction, n_chunks, lat_chunks):
    return jnp.where(direction == 0, lax.rem(t + lat_chunks, n_chunks), n_chunks - 1 - t)


def _ret_fwd(u, c2, s2, lg, *, s_len, heads, q_off, name):
    t_len = u.shape[0]
    cs, dk, dv = RET_CHUNK, RET_KEY_DIM, RET_VAL_DIM
    n_chunks, lat_chunks = t_len // cs, s_len // cs
    k_scale = dk ** -0.5
    qb, kb, vb = q_off // dk, q_off // dk + heads, (q_off + 2 * heads * dk) // dv

    def body(lg_ref, q_ref, k_ref, v_ref, c_ref, s_ref, o_ref, st_ref, state):
        h, d = pl.program_id(0), pl.program_id(1)
        _, dm, qdec, kdec, cd = _ret_decays(lg_ref[d, h], d)
        state[...] = jnp.zeros_like(state)

        def step(t, carry):
            c = _ret_chunk_index(t, d, n_chunks, lat_chunks)
            r = pl.ds(pl.multiple_of(c * cs, cs), cs)
            cc, ss = c_ref[r, :], s_ref[r, :]
            qc = _rope(q_ref[r, :].astype(F32), cc, ss)
            kc = _rope(k_ref[r, :].astype(F32), cc, ss) * k_scale
            vc = v_ref[r, :]
            st = state[...]
            st_ref[t] = st
            a = _dot(qc.astype(BF16), kc.astype(BF16), 1, 1) * dm
            oc = _dot(a.astype(BF16), vc, 1, 0) + _dot((qc * qdec).astype(BF16), st.astype(BF16), 1, 0)
            state[...] = st * cd + _dot((kc * kdec).astype(BF16), vc, 0, 0)

            @pl.when(d == 0)
            def _():
                o_ref[r, :] = oc

            @pl.when(d == 1)
            def _():
                o_ref[r, :] += oc

            return carry

        lax.fori_loop(0, n_chunks, step, 0)

    return pl.pallas_call(
        body, name=name, grid=(heads, 2),
        in_specs=[pl.BlockSpec(memory_space=pltpu.SMEM),
                  pl.BlockSpec((t_len, dk), lambda h, d: (0, qb + h)),
                  pl.BlockSpec((t_len, dk), lambda h, d: (0, kb + h)),
                  pl.BlockSpec((t_len, dv), lambda h, d: (0, vb + h)),
                  pl.BlockSpec((t_len, dk), lambda h, d: (0, 0)),
                  pl.BlockSpec((t_len, dk), lambda h, d: (0, 0))],
        out_specs=[pl.BlockSpec((t_len, dv), lambda h, d: (0, h)),
                   pl.BlockSpec((None, None, n_chunks, dk, dv), lambda h, d: (h, d, 0, 0, 0))],
        out_shape=[jax.ShapeDtypeStruct((t_len, heads * dv), F32),
                   jax.ShapeDtypeStruct((heads, 2, n_chunks, dk, dv), F32)],
        scratch_shapes=[pltpu.VMEM((dk, dv), F32)],
        compiler_params=_params(("parallel", "arbitrary")),
    )(lg, u, u, u, c2, s2)


def _ret_bwd(u, c2, s2, lg, states, do, *, s_len, heads, q_off, name):
    t_len = u.shape[0]
    cs, dk, dv = RET_CHUNK, RET_KEY_DIM, RET_VAL_DIM
    n_chunks, lat_chunks = t_len // cs, s_len // cs
    k_scale = dk ** -0.5
    qb, kb, vb = q_off // dk, q_off // dk + heads, (q_off + 2 * heads * dk) // dv

    def body(lg_ref, q_ref, k_ref, v_ref, c_ref, s_ref, st_ref, do_ref, dq_ref, dk_ref, dv_ref, dlg_ref, dstate, acc):
        h, d = pl.program_id(0), pl.program_id(1)
        p_col, dm, qdec, kdec, cd = _ret_decays(lg_ref[d, h], d)
        dstate[...] = jnp.zeros_like(dstate)
        acc[...] = jnp.zeros_like(acc)

        def step(i, carry):
            t = n_chunks - 1 - i
            c = _ret_chunk_index(t, d, n_chunks, lat_chunks)
            r = pl.ds(pl.multiple_of(c * cs, cs), cs)
            cc, ss = c_ref[r, :], s_ref[r, :]
            qc = _rope(q_ref[r, :].astype(F32), cc, ss)
            kc = _rope(k_ref[r, :].astype(F32), cc, ss) * k_scale
            vc = v_ref[r, :]
            doc = do_ref[r, :].astype(BF16)
            st = st_ref[t]
            dst = dstate[...]
            qb16, kb16 = qc.astype(BF16), kc.astype(BF16)
            a = _dot(qb16, kb16, 1, 1) * dm
            dam = (_dot(doc, vc, 1, 1) * dm).astype(BF16)
            dq_i = _dot(dam, kb16, 1, 0)
            dk_i = _dot(dam, qb16, 0, 0)
            dq_c = _dot(doc, st.astype(BF16), 1, 1) * qdec
            dst16 = dst.astype(BF16)
            dvc = _dot(a.astype(BF16), doc, 0, 0) + _dot((kc * kdec).astype(BF16), dst16, 1, 0)
            dk_s = _dot(vc, dst16, 1, 1) * kdec
            g = (jnp.sum(qc * (p_col * dq_i + (p_col + 1.0) * dq_c), axis=-1, keepdims=True)
                 + jnp.sum(kc * ((cs - 1.0 - p_col) * dk_s - p_col * dk_i), axis=-1, keepdims=True))
            g = jnp.sum(g, axis=0, keepdims=True) + cs * cd * jnp.sum(jnp.sum(dst * st, axis=-1, keepdims=True), axis=0, keepdims=True)
            acc[...] += jnp.broadcast_to(g, acc.shape)
            dstate[...] = dst * cd + _dot((qc * qdec).astype(BF16), doc, 0, 0)
            dq = _rope_t(dq_i + dq_c, cc, ss)
            dkk = _rope_t((dk_i + dk_s) * k_scale, cc, ss)

            @pl.when(d == 0)
            def _():
                dq_ref[r, :] = dq.astype(dq_ref.dtype)
                dk_ref[r, :] = dkk.astype(dk_ref.dtype)
                dv_ref[r, :] = dvc.astype(dv_ref.dtype)

            @pl.when(d == 1)
            def _():
                dq_ref[r, :] = (dq_ref[r, :].astype(F32) + dq).astype(dq_ref.dtype)
                dk_ref[r, :] = (dk_ref[r, :].astype(F32) + dkk).astype(dk_ref.dtype)
                dv_ref[r, :] = (dv_ref[r, :].astype(F32) + dvc).astype(dv_ref.dtype)

            return carry

        lax.fori_loop(0, n_chunks, step, 0)
        dlg_ref[...] = acc[...]

    return pl.pallas_call(
        body, name=name, grid=(heads, 2),
        in_specs=[pl.BlockSpec(memory_space=pltpu.SMEM),
                  pl.BlockSpec((t_len, dk), lambda h, d: (0, qb + h)),
                  pl.BlockSpec((t_len, dk), lambda h, d: (0, kb + h)),
                  pl.BlockSpec((t_len, dv), lambda h, d: (0, vb + h)),
                  pl.BlockSpec((t_len, dk), lambda h, d: (0, 0)),
                  pl.BlockSpec((t_len, dk), lambda h, d: (0, 0)),
                  pl.BlockSpec((None, None, n_chunks, dk, dv), lambda h, d: (h, d, 0, 0, 0)),
                  pl.BlockSpec((t_len, dv), lambda h, d: (0, h))],
        out_specs=[pl.BlockSpec((t_len, dk), lambda h, d: (0, h)),
                   pl.BlockSpec((t_len, dk), lambda h, d: (0, h)),
                   pl.BlockSpec((t_len, dv), lambda h, d: (0, h)),
                   pl.BlockSpec((None, None, 8, LANES), lambda h, d: (h, d, 0, 0))],
        out_shape=[jax.ShapeDtypeStruct((t_len, heads * dk), BF16),
                   jax.ShapeDtypeStruct((t_len, heads * dk), BF16),
                   jax.ShapeDtypeStruct((t_len, heads * dv), BF16),
                   jax.ShapeDtypeStruct((heads, 2, 8, LANES), F32)],
        scratch_shapes=[pltpu.VMEM((dk, dv), F32), pltpu.VMEM((8, LANES), F32)],
        compiler_params=_params(("parallel", "arbitrary")),
    )(lg, u, u, u, c2, s2, states, do)


def _mesh_pos():
    return lax.axis_index("x"), lax.axis_index("y"), lax.axis_index("c")


def _all_gather_small(buf, *, name):
    r = buf.shape[0]

    def body(x_ref, o_ref, send_sems, recv_sems, local_sem):
        x, y, c = _mesh_pos()
        me = 4 * x + 2 * y + c
        mine = pltpu.make_async_copy(x_ref, o_ref.at[me], local_sem)
        mine.start()
        copies = []
        for k in range(1, N_DEV):
            px, py, pc = x ^ ((k >> 2) & 1), y ^ ((k >> 1) & 1), c ^ (k & 1)
            cp = pltpu.make_async_remote_copy(
                src_ref=x_ref, dst_ref=o_ref.at[me], send_sem=send_sems.at[k - 1], recv_sem=recv_sems.at[k - 1],
                device_id=(px, py, pc), device_id_type=MESH)
            cp.start()
            copies.append((cp, 4 * px + 2 * py + pc))
        for k, (cp, peer) in enumerate(copies):
            pltpu.make_async_remote_copy(
                src_ref=x_ref, dst_ref=o_ref.at[peer], send_sem=send_sems.at[k], recv_sem=recv_sems.at[k],
                device_id=(x, y, c), device_id_type=MESH).wait_recv()
        for cp, _ in copies:
            cp.wait_send()
        mine.wait()

    return pl.pallas_call(
        body, name=name,
        in_specs=[pl.BlockSpec(memory_space=pltpu.VMEM)],
        out_specs=pl.BlockSpec(memory_space=pltpu.VMEM),
        out_shape=jax.ShapeDtypeStruct((N_DEV, r, LANES), F32),
        scratch_shapes=[pltpu.SemaphoreType.DMA((N_DEV - 1,)), pltpu.SemaphoreType.DMA((N_DEV - 1,)),
                        pltpu.SemaphoreType.DMA],
        compiler_params=pltpu.CompilerParams(vmem_limit_bytes=VMEM_LIMIT),
    )(buf)


def _cut(ref, shard_axis, *, chip=None, half=None, lead=None):
    shape = ref.shape[1:] if lead is not None else ref.shape
    idx = [slice(None), slice(None)]
    if chip is not None:
        w = shape[shard_axis] // N_CHIPS
        idx[shard_axis] = pl.ds(pl.multiple_of(chip * w, w), w)
    if half is not None:
        hw = shape[1 - shard_axis] // 2
        idx[1 - shard_axis] = pl.ds(pl.multiple_of(half * hw, hw), hw)
    if lead is not None:
        idx = [lead] + idx
    return ref.at[tuple(idx)]


def _wait_recv(ref, send_sem, recv_sem):
    pltpu.make_async_remote_copy(src_ref=ref, dst_ref=ref, send_sem=send_sem, recv_sem=recv_sem,
                                 device_id=_mesh_pos(), device_id_type=MESH).wait_recv()


def _gather_plan(axes):
    def plan(srcs, lands, send_sems, recv_sems):
        x, y, c = _mesh_pos()
        chip = 2 * x + y
        copies = []
        for i, ax in enumerate(axes):
            for k in range(1, N_CHIPS):
                px, py = x ^ (k >> 1), y ^ (k & 1)
                src = _cut(srcs[i], ax, half=c)
                j = i * (N_CHIPS - 1) + k - 1
                sems = dict(send_sem=send_sems.at[j], recv_sem=recv_sems.at[j], device_id=(px, py, c), device_id_type=MESH)
                send = pltpu.make_async_remote_copy(src_ref=src, dst_ref=_cut(lands[i], ax, chip=chip, half=c), **sems)
                recv = pltpu.make_async_remote_copy(src_ref=src, dst_ref=_cut(lands[i], ax, chip=2 * px + py, half=c), **sems)
                copies.append((send, recv))
        return copies
    return plan


def _scatter_plan(axes):
    def plan(srcs, lands, send_sems, recv_sems):
        x, y, c = _mesh_pos()
        copies = []
        for i, ax in enumerate(axes):
            for k in range(1, N_CHIPS):
                px, py = x ^ (k >> 1), y ^ (k & 1)
                j = i * (N_CHIPS - 1) + k - 1
                cp = pltpu.make_async_remote_copy(
                    src_ref=_cut(srcs[i], ax, chip=2 * px + py), dst_ref=lands[i].at[k - 1],
                    send_sem=send_sems.at[j], recv_sem=recv_sems.at[j], device_id=(px, py, c), device_id_type=MESH)
                copies.append((cp, cp))
        return copies
    return plan


HBM = pl.BlockSpec(memory_space=pltpu.HBM)
SEM = pl.BlockSpec(memory_space=pltpu.SEMAPHORE)
EFFECT = pltpu.SideEffectType.DATAFLOW_SIDE_EFFECTING


def _in_hbm(arrays):
    return [pltpu.with_memory_space_constraint(a, pltpu.HBM) for a in arrays]


def _split_start(srcs, lands, plan, *, name):
    n = len(srcs)

    def body(*refs):
        send_sems, recv_sems, token = refs[2 * n], refs[2 * n + 1], refs[-1]
        for send, _ in plan(refs[:n], refs[n:2 * n], send_sems, recv_sems):
            send.start()
        token[...] = jnp.zeros_like(token)

    sems = pltpu.SemaphoreType.DMA((n * (N_CHIPS - 1),))
    res = pl.pallas_call(
        body, name=name, in_specs=[HBM] * (2 * n),
        out_specs=[SEM, SEM] + [HBM] * (2 * n) + [pl.BlockSpec(memory_space=pltpu.VMEM)],
        out_shape=[sems, sems] + [pltpu.HBM(a.shape, a.dtype) for a in list(srcs) + list(lands)] + [jax.ShapeDtypeStruct((8, LANES), F32)],
        input_output_aliases={j: 2 + j for j in range(2 * n)},
        compiler_params=pltpu.CompilerParams(has_side_effects=EFFECT),
    )(*_in_hbm(list(srcs) + list(lands)))
    return res[0], res[1], res[2:2 + n], res[2 + n:2 + 2 * n], res[-1]


def _split_wait(started, after, plan, *, name):
    send_sems, recv_sems, srcs, lands, _ = started
    n = len(srcs)

    def body(*refs):
        for send, recv in plan(refs[:n], refs[n:2 * n], refs[2 * n], refs[2 * n + 1]):
            send.wait_send()
            recv.wait_recv()

    res = pl.pallas_call(
        body, name=name, in_specs=[HBM] * (2 * n) + [SEM, SEM, ANY], out_specs=[HBM] * (2 * n),
        out_shape=[pltpu.HBM(a.shape, a.dtype) for a in list(srcs) + list(lands)],
        input_output_aliases={j: j for j in range(2 * n)},
        compiler_params=pltpu.CompilerParams(has_side_effects=EFFECT),
    )(*srcs, *lands, send_sems, recv_sems, after)
    return res[n:]


def _forward_halves(fulls, axes, *, name):
    n = len(fulls)

    def body(*refs):
        bufs = refs[:n]
        send_sems, recv_sems = refs[2 * n:]
        x, y, c = _mesh_pos()
        sends = []
        for i in range(n):
            for k in range(1, N_CHIPS):
                landed = _cut(bufs[i], axes[i], chip=2 * (x ^ (k >> 1)) + (y ^ (k & 1)), half=c)
                cp = pltpu.make_async_remote_copy(
                    src_ref=landed, dst_ref=landed, send_sem=send_sems.at[i, k - 1], recv_sem=recv_sems.at[i, k - 1],
                    device_id=(x, y, 1 - c), device_id_type=MESH)
                cp.start()
                sends.append(cp)
        for i in range(n):
            for k in range(1, N_CHIPS):
                other = _cut(bufs[i], axes[i], chip=2 * (x ^ (k >> 1)) + (y ^ (k & 1)), half=1 - c)
                _wait_recv(other, send_sems.at[i, k - 1], recv_sems.at[i, k - 1])
        for cp in sends:
            cp.wait_send()

    pairs = pltpu.SemaphoreType.DMA((n, N_CHIPS - 1))
    return pl.pallas_call(
        body, name=name, in_specs=[ANY] * n, out_specs=[ANY] * n,
        out_shape=[jax.ShapeDtypeStruct(a.shape, a.dtype) for a in fulls],
        input_output_aliases={j: j for j in range(n)},
        scratch_shapes=[pairs, pairs],
    )(*fulls)


def _send_to_sibling(parts, *, name):
    n = len(parts)

    def body(*refs):
        ins, outs = refs[:n], refs[n:2 * n]
        send_sems, recv_sems = refs[2 * n:]
        x, y, c = _mesh_pos()
        sends = []
        for i in range(n):
            cp = pltpu.make_async_remote_copy(
                src_ref=ins[i], dst_ref=outs[i], send_sem=send_sems.at[i], recv_sem=recv_sems.at[i],
                device_id=(x, y, 1 - c), device_id_type=MESH)
            cp.start()
            sends.append(cp)
        for cp in sends:
            cp.wait()

    sems = pltpu.SemaphoreType.DMA((n,))
    return pl.pallas_call(
        body, name=name, in_specs=[ANY] * n, out_specs=[ANY] * n,
        out_shape=[jax.ShapeDtypeStruct(p.shape, p.dtype) for p in parts],
        scratch_shapes=[sems, sems],
    )(*parts)


def _adamw_math(w, g, m, v):
    m = ADAM_B1 * m + (1.0 - ADAM_B1) * g
    v = ADAM_B2 * v + (1.0 - ADAM_B2) * (g * g)
    m_hat = m / (1.0 - ADAM_B1 ** ADAM_STEP)
    v_hat = v / (1.0 - ADAM_B2 ** ADAM_STEP)
    delta = -ADAM_LR * (m_hat / (jnp.sqrt(v_hat) + ADAM_EPS) + ADAM_WD * w)
    return delta, m, v


def _adamw_layer(w3, m3, v3, p, q, layer, prev, *, name):
    nl, rows, width = w3.shape
    tr = _rows_per_tile(rows, width)

    def fn(*t):
        if q is None:
            w, m, v, g = t
        else:
            w, m, v, g, g2 = t
            g = g + g2
        delta, m, v = _adamw_math(w, g, m, v)
        return g, delta, m, v

    ins = [('t', w3, 0, width, layer), ('t', m3, 0, width, layer), ('t', v3, 0, width, layer), ('t', p, 0, width)]
    if q is not None:
        ins.append(('t', q, 0, width))
    outs = [('t', width, F32, layer, nl)] * 4
    aliases = None if prev is None else [(prev[i], i) for i in range(4)]
    return _ew(fn, ins, outs, rows=rows, tr=tr, name=name, aliases=aliases)


def _pack_rows(vec):
    n = vec.shape[0]
    r = -(-n // (8 * LANES)) * 8
    return jnp.pad(vec, (0, r * LANES - n)).reshape(r, LANES)


def kernel(x, c, ctx, c_ctx, ada_w, ada_b, norm_g, w_in, na_rpb, ret_decay_logit, w_proj_na, w_proj_ret, w_out, final_g, loss_target, m_c_ctx, m_ada_w, m_ada_b, m_norm_g, m_w_in, m_na_rpb, m_ret_decay_logit, m_w_proj_na, m_w_proj_ret, m_w_out, m_final_g, v_c_ctx, v_ada_w, v_ada_b, v_norm_g, v_w_in, v_na_rpb, v_ret_decay_logit, v_w_proj_na, v_w_proj_ret, v_w_out, v_final_g):
    depth = w_in.shape[0]
    s_len, d_model = x.shape[1], x.shape[2]
    l_len = ctx.shape[1]
    t_len = s_len + l_len
    na_heads = na_rpb.shape[1]
    ret_heads = ret_decay_logit.shape[2]
    w_na = na_heads * NA_HEAD_DIM
    w_qk = ret_heads * RET_KEY_DIM
    w_v = ret_heads * RET_VAL_DIM
    in_cols = w_in.shape[2] * N_CHIPS
    assert in_cols == 4 * w_na + 2 * w_qk + 2 * w_v + 2 * d_model
    assert x.shape[0] == 1 and s_len % (NA_WIN_ROWS * GRID_W) == 0 and l_len % RET_CHUNK == 0
    off = np.cumsum([0, w_na, w_na, w_na, w_na, w_qk, w_qk, w_v, w_v, d_model, d_model])
    o_naz, o_retq, o_retz, o_gna, o_gret = int(off[3]), int(off[4]), int(off[7]), int(off[8]), int(off[9])
    rows = s_len // GRID_W
    tr = _tile(l_len, 256, 8)
    n0 = s_len // tr
    mod_cols = 3 * d_model
    mod_shard = ada_w.shape[2]

    xi, yi, ci = _mesh_pos()
    me = 4 * xi + 2 * yi + ci
    chip = 2 * xi + yi

    def cast_bf16(w3, l, tag):
        _, r, wd = w3.shape
        return _ew(lambda t: t, [('t', w3, 0, wd, l)], [('t', wd, BF16)], rows=r, tr=_rows_per_tile(r, wd, 4 << 20),
                   name="cast_%s_%d" % (tag, l))[0]

    big_axes = [1, 1, 0, 0]
    gather_plan, scatter_plan = _gather_plan(big_axes), _scatter_plan(big_axes)
    gathers = []
    for l in range(depth):
        shards = [cast_bf16(w, l, tag) for w, tag in ((w_in, "w_in"), (w_proj_na, "w_proj_na"), (w_proj_ret, "w_proj_ret"), (w_out, "w_out"))]
        lands = []
        for s, ax in zip(shards, big_axes):
            shp = list(s.shape)
            shp[ax] *= N_CHIPS
            lands.append(lax.dynamic_update_slice_in_dim(lax.empty(tuple(shp), BF16), s, chip * s.shape[ax], axis=ax))
        gathers.append(_split_start(shards, lands, gather_plan, name="gather_start_%d" % l))
    start_token = sum(g[4][0, 0] for g in gathers)

    c_silu = c[0] * _sigmoid(c[0])
    cc_silu = c_ctx * _sigmoid(c_ctx)
    c_all = _all_gather_small(_pack_rows(c_silu), name="gather_c")[:, :d_model // LANES].reshape(N_DEV, d_model)
    a_rows = jnp.concatenate([c_all, cc_silu[None], jnp.zeros((16 - N_DEV - 1, d_model), F32)], axis=0)
    mod_part = jnp.stack([_mm(a_rows, ada_w, b_lead=l, out_dtype=F32, name="ada_fwd_%d" % l) for l in range(depth)])
    mod_all = _all_gather_small(_pack_rows(mod_part.reshape(-1)), name="gather_mod")
    n_mod = depth * 16 * mod_shard
    mod_all = mod_all.reshape(N_DEV, -1)[:, :n_mod].reshape(N_CHIPS, 2, depth, 16, mod_shard)[:, 0]
    mod_all = jnp.transpose(mod_all, (1, 2, 0, 3)).reshape(depth, 16, mod_cols) + ada_b[:, None, :]
    mod_lat = lax.dynamic_index_in_dim(mod_all, me, axis=1, keepdims=False)
    mod_ctx = mod_all[:, N_DEV]

    c2, s2 = _rope_tables(s_len, l_len)
    log_gamma = jax.nn.log_sigmoid(ret_decay_logit)
    x_all = jnp.concatenate([x[0], ctx[0]], axis=0)

    def grp(lat_vec, ctx_vec):
        return jnp.stack([lat_vec, ctx_vec])[:, None, :]

    saved, full_w = [], []
    for l in range(depth):
        shift, scale, gate = [grp(mod_lat[l, i * d_model:(i + 1) * d_model], mod_ctx[l, i * d_model:(i + 1) * d_model])
                              for i in range(3)]
        gs = norm_g[l][None, None, :] * (1.0 + scale) + start_token
        landed = _split_wait(gathers[l], x_all, gather_plan, name="gather_wait_%d" % l)
        full_w.append(_forward_halves(landed, big_axes, name="gather_forward_%d" % l))

        def modnorm(xt, gs_t, sh_t):
            r = lax.rsqrt(jnp.mean(xt * xt, axis=-1, keepdims=True) + NORM_EPS)
            return xt * r * gs_t + sh_t

        h, = _ew(modnorm, [('t', x_all, 0, d_model), ('g', gs), ('g', shift)], [('t', d_model, BF16)],
                 rows=t_len, tr=tr, n0=n0, name="modnorm_%d" % l)
        win_f, wpn_f, wpr_f, wout_f = full_w[l]
        u = _mm(h, win_f, name="in_proj_%d" % l)
        bias = _na_bias_table(na_rpb[l], rows, name="na_bias_%d" % l)
        o_na = _na_fwd(u, bias, s_len=s_len, heads=na_heads, name="na_fwd_%d" % l)
        o_ret, states = _ret_fwd(u, c2, s2, log_gamma[l], s_len=s_len, heads=ret_heads, q_off=o_retq, name="ret_fwd_%d" % l)

        def act(o1, z1, o2, z2):
            a1 = o1.astype(F32) * _silu_parts(z1.astype(F32))[0]
            sz = _silu_parts(z2.astype(F32))[0]
            outs = []
            for hh in range(ret_heads):
                sl = slice(hh * RET_VAL_DIM, (hh + 1) * RET_VAL_DIM)
                oh = o2[:, sl]
                r = lax.rsqrt(jnp.mean(oh * oh, axis=-1, keepdims=True) + NORM_EPS)
                outs.append(oh * r * sz[:, sl])
            return a1, jnp.concatenate(outs, axis=-1)

        a_na, a_ret = _ew(act, [('t', o_na, 0, w_na), ('t', u, o_naz // w_na, w_na), ('t', o_ret, 0, w_v), ('t', u, o_retz // w_v, w_v)],
                          [('t', w_na, BF16), ('t', w_v, BF16)], rows=t_len, tr=tr, name="act_%d" % l)
        y_na = _mm(a_na, wpn_f, name="proj_na_%d" % l)
        y_ret = _mm(a_ret, wpr_f, name="proj_ret_%d" % l)

        def merge(y1, y2, g1, g2):
            return _sigmoid(g1.astype(F32)) * y1.astype(F32) + _sigmoid(g2.astype(F32)) * y2.astype(F32)

        merged, = _ew(merge, [('t', y_na, 0, d_model), ('t', y_ret, 0, d_model), ('t', u, o_gna // d_model, d_model), ('t', u, o_gret // d_model, d_model)],
                      [('t', d_model, BF16)], rows=t_len, tr=tr, name="merge_%d" % l)
        out = _mm(merged, wout_f, out_dtype=F32, name="out_proj_%d" % l)
        x_new, = _ew(lambda xt, ot, gt: xt + gt * ot, [('t', x_all, 0, d_model), ('t', out, 0, d_model), ('g', gate)],
                     [('t', d_model, F32)], rows=t_len, tr=tr, n0=n0, name="resid_%d" % l)
        saved.append(dict(x=x_all, h=h, u=u, bias=bias, o_na=o_na, o_ret=o_ret, states=states, a_na=a_na, a_ret=a_ret,
                          y_na=y_na, y_ret=y_ret, merged=merged, out=out, gate=gate, gs=gs, scale=scale))
        x_all = x_new

    def final(xt, tt, gt):
        r = lax.rsqrt(jnp.mean(xt * xt, axis=-1, keepdims=True) + NORM_EPS)
        xh = xt * r
        e = xh * gt - tt
        dy = e * (1.0 / d_model)
        dyg = dy * gt
        dx = r * (dyg - xh * jnp.mean(dyg * xh, axis=-1, keepdims=True))
        return dx, _rsum(dy * xh), _rsum(e * e)

    dx_lat, d_final_g, loss_cols = _ew(final, [('t', x_all, 0, d_model), ('t', loss_target[0], 0, d_model), ('g', final_g[None, None, :])],
                                       [('t', d_model, F32), ('r', d_model, 1), ('r', d_model, 1)], rows=s_len, tr=tr, name="final")
    loss_part = (0.5 / d_model) * jnp.sum(loss_cols)
    dx_all = jnp.concatenate([dx_lat, jnp.zeros((l_len, d_model), F32)], axis=0)

    big_w = [(w_in, m_w_in, v_w_in), (w_proj_na, m_w_proj_na, v_w_proj_na), (w_proj_ret, m_w_proj_ret, v_w_proj_ret), (w_out, m_w_out, v_w_out)]
    big_res = [None] * 4
    scatters = [None] * depth
    back_token = jnp.zeros((), F32)
    small = dict(dmod_lat=[None] * depth, dmod_ctx=[None] * depth, dnorm_g=[None] * depth, drpb=[None] * depth, ddecay=[None] * depth)
    for l in reversed(range(depth)):
        sv = saved[l]
        win_f, wpn_f, wpr_f, wout_f = full_w[l]

        def resid_bwd(dxt, ot, gt):
            return gt * dxt, _rsum(dxt * ot)

        dout, dgate = _ew(resid_bwd, [('t', dx_all, 0, d_model), ('t', sv['out'], 0, d_model), ('g', sv['gate'] + back_token)],
                          [('t', d_model, BF16), ('r', d_model, 2)], rows=t_len, tr=tr, n0=n0, name="resid_bwd_%d" % l)
        dmerged = _mm(dout, wout_f, tb=True, name="out_proj_dx_%d" % l)
        g_wout = _mm(sv['merged'], dout, ta=True, tm=512, tk=t_len, name="out_proj_dw_%d" % l)

        def merge_bwd(dm, y1, y2, g1, g2):
            dm = dm.astype(F32)
            s1, s2_ = _sigmoid(g1.astype(F32)), _sigmoid(g2.astype(F32))
            return dm * s1, dm * s2_, dm * y1.astype(F32) * s1 * (1.0 - s1), dm * y2.astype(F32) * s2_ * (1.0 - s2_)

        u = sv['u']
        dy_na, dy_ret, dg_na, dg_ret = _ew(
            merge_bwd, [('t', dmerged, 0, d_model), ('t', sv['y_na'], 0, d_model), ('t', sv['y_ret'], 0, d_model),
                        ('t', u, o_gna // d_model, d_model), ('t', u, o_gret // d_model, d_model)],
            [('t', d_model, BF16)] * 4, rows=t_len, tr=tr, name="merge_bwd_%d" % l)
        da_na = _mm(dy_na, wpn_f, tb=True, name="proj_na_dx_%d" % l)
        g_wpn = _mm(sv['a_na'], dy_na, ta=True, tm=512, tk=t_len, name="proj_na_dw_%d" % l)
        da_ret = _mm(dy_ret, wpr_f, tb=True, name="proj_ret_dx_%d" % l)
        g_wpr = _mm(sv['a_ret'], dy_ret, ta=True, tm=512, tk=t_len, name="proj_ret_dw_%d" % l)

        def act_bwd(da1, o1, z1, da2, o2, z2):
            da1, da2 = da1.astype(F32), da2.astype(F32)
            si1, ds1 = _silu_parts(z1.astype(F32))
            si2, ds2 = _silu_parts(z2.astype(F32))
            do1 = da1 * si1
            dz1 = da1 * o1.astype(F32) * ds1
            dn = da2 * si2
            do2, dz2 = [], []
            for hh in range(ret_heads):
                sl = slice(hh * RET_VAL_DIM, (hh + 1) * RET_VAL_DIM)
                oh = o2[:, sl]
                r = lax.rsqrt(jnp.mean(oh * oh, axis=-1, keepdims=True) + NORM_EPS)
                nh = oh * r
                dz2.append(da2[:, sl] * nh * ds2[:, sl])
                do2.append(r * (dn[:, sl] - nh * jnp.mean(dn[:, sl] * nh, axis=-1, keepdims=True)))
            return do1, dz1, jnp.concatenate(do2, axis=-1), jnp.concatenate(dz2, axis=-1)

        do_na, dz_na, do_ret, dz_ret = _ew(
            act_bwd, [('t', da_na, 0, w_na), ('t', sv['o_na'], 0, w_na), ('t', u, o_naz // w_na, w_na),
                      ('t', da_ret, 0, w_v), ('t', sv['o_ret'], 0, w_v), ('t', u, o_retz // w_v, w_v)],
            [('t', w_na, BF16), ('t', w_na, BF16), ('t', w_v, BF16), ('t', w_v, BF16)], rows=t_len, tr=tr, name="act_bwd_%d" % l)
        dq_na, dk_na, dv_na, dbias = _na_bwd(u, sv['bias'], sv['o_na'], do_na, s_len=s_len, heads=na_heads, name="na_bwd_%d" % l)
        small['drpb'][l] = _rpb_grad(dbias, name="rpb_grad_%d" % l)
        dq_r, dk_r, dv_r, dlg = _ret_bwd(u, c2, s2, log_gamma[l], sv['states'], do_ret, s_len=s_len, heads=ret_heads,
                                         q_off=o_retq, name="ret_bwd_%d" % l)
        small['ddecay'][l] = jnp.transpose(dlg[:, :, 0, 0]) * _sigmoid(-ret_decay_logit[l])
        du = jnp.concatenate([dq_na, dk_na, dv_na, dz_na, dq_r, dk_r, dv_r, dz_ret, dg_na, dg_ret], axis=1)
        dh = _mm(du, win_f, tb=True, out_dtype=F32, tn=1024, name="in_proj_dx_%d" % l)
        g_win = _mm(sv['h'], du, ta=True, tm=512, tk=t_len, name="in_proj_dw_%d" % l)

        def modnorm_bwd(xt, dht, dxt, gs_t):
            r = lax.rsqrt(jnp.mean(xt * xt, axis=-1, keepdims=True) + NORM_EPS)
            xh = xt * r
            dhg = dht * gs_t
            dx = r * (dhg - xh * jnp.mean(dhg * xh, axis=-1, keepdims=True)) + dxt
            return dx, _rsum(dht), _rsum(dht * xh)

        dx_all, dshift, dgs = _ew(modnorm_bwd, [('t', sv['x'], 0, d_model), ('t', dh, 0, d_model), ('t', dx_all, 0, d_model), ('g', sv['gs'])],
                                  [('t', d_model, F32), ('r', d_model, 2), ('r', d_model, 2)], rows=t_len, tr=tr, n0=n0, name="modnorm_bwd_%d" % l)
        dscale = dgs * norm_g[l][None, None, :]
        small['dnorm_g'][l] = jnp.sum(dgs * (1.0 + sv['scale']), axis=(0, 1))
        dmod = jnp.concatenate([dshift, dscale, dgate], axis=-1)[:, 0]
        small['dmod_lat'][l], small['dmod_ctx'][l] = dmod[0], dmod[1]

        grads_l = [g_win, g_wpn, g_wpr, g_wout]
        half_sz = [g.shape[1 - ax] // 2 for g, ax in zip(grads_l, big_axes)]
        mine = [lax.dynamic_slice_in_dim(g, ci * hs, hs, axis=1 - ax) for g, hs, ax in zip(grads_l, half_sz, big_axes)]
        to_send = [lax.dynamic_slice_in_dim(g, (1 - ci) * hs, hs, axis=1 - ax) for g, hs, ax in zip(grads_l, half_sz, big_axes)]
        theirs = _send_to_sibling(to_send, name="pair_exchange_%d" % l)
        pair = []
        for i in range(4):
            pr, pw = mine[i].shape
            s, = _ew(lambda a, b: a.astype(F32) + b.astype(F32), [('t', mine[i], 0, pw), ('t', theirs[i], 0, pw)], [('t', pw, BF16)],
                     rows=pr, tr=_rows_per_tile(pr, pw), name="sum_pair_%d_%d" % (i, l))
            pair.append(s)
        own = [lax.dynamic_slice_in_dim(s, chip * (s.shape[ax] // N_CHIPS), s.shape[ax] // N_CHIPS, axis=ax) for s, ax in zip(pair, big_axes)]
        lands = [lax.empty((N_CHIPS - 1,) + o.shape, BF16) for o in own]
        scatters[l] = (_split_start(pair, lands, scatter_plan, name="scatter_start_%d" % l), own)
        back_token = scatters[l][0][4][0, 0]

    def finish_layer(l, after, big_res):
        started, own = scatters[l]
        recv = _split_wait(started, after, scatter_plan, name="scatter_wait_%d" % l)
        parts = []
        for i, rbuf in enumerate(recv):
            pr, pw = own[i].shape
            p, = _ew(lambda a, b, c_, d: ((a.astype(F32) + b.astype(F32)) + c_.astype(F32)) + d.astype(F32),
                     [('t', own[i], 0, pw)] + [('t', rbuf, 0, pw, k) for k in range(N_CHIPS - 1)], [('t', pw, F32)],
                     rows=pr, tr=_rows_per_tile(pr, pw), name="sum_chips_%d_%d" % (i, l))
            parts.append(p)
        others = _send_to_sibling(parts, name="share_halves_%d" % l)
        shard_g = [jnp.where(ci == 0, jnp.concatenate([p, o], axis=1 - ax), jnp.concatenate([o, p], axis=1 - ax))
                   for p, o, ax in zip(parts, others, big_axes)]
        for i in range(4):
            w3, m3, v3 = big_w[i]
            big_res[i] = _adamw_layer(w3, m3, v3, shard_g[i], None, l, big_res[i], name="adamw_big_%d_%d" % (i, l))
        return big_res

    grad_x = dx_all[:s_len][None]
    for l in reversed(range(1, depth)):
        big_res = finish_layer(l, dx_all, big_res)

    drpb = jnp.stack(small['drpb']).reshape(-1)
    ddecay = jnp.stack(small['ddecay']).reshape(-1)
    pieces = [jnp.stack(small['dmod_lat']).reshape(-1), jnp.stack(small['dmod_ctx']).reshape(-1),
              jnp.stack(small['dnorm_g']).reshape(-1), d_final_g.reshape(-1), drpb, ddecay, loss_part[None]]
    sizes = [int(p.shape[0]) for p in pieces]
    pads = [-(-s // LANES) * LANES for s in sizes]
    packed = jnp.concatenate([jnp.pad(p, (0, pd - s)) for p, s, pd in zip(pieces, sizes, pads)])
    gathered = _all_gather_small(_pack_rows(packed), name="gather_small_grads")
    r_small = gathered.shape[1]

    def sum8(*t):
        acc = t[0]
        for other in t[1:]:
            acc = acc + other
        return acc

    total, = _ew(sum8, [('t', gathered, 0, LANES, k) for k in range(N_DEV)], [('t', LANES, F32)], rows=r_small, tr=r_small, name="sum_devices")
    total = total.reshape(-1)
    starts = np.cumsum([0] + pads)
    g_mod_lat_sum, g_mod_ctx, g_norm_g, g_final_g, g_rpb, g_decay, loss = [total[starts[i]:starts[i] + sizes[i]] for i in range(len(pieces))]
    loss = loss[0]
    g_ada_b = (g_mod_lat_sum + g_mod_ctx).reshape(depth, mod_cols)
    g_mod_ctx = g_mod_ctx.reshape(depth, mod_cols)
    dmod_lat_all = gathered.reshape(N_DEV, -1)[:, :depth * mod_cols].reshape(N_DEV, depth, mod_cols)

    ada_res = None
    dcc_part = jnp.zeros((16, d_model), F32)
    for l in reversed(range(depth)):
        lat_cols = lax.dynamic_slice_in_dim(dmod_lat_all[:, l], chip * mod_shard, mod_shard, axis=1)
        ctx_cols = lax.dynamic_slice_in_dim(g_mod_ctx[l], chip * mod_shard, mod_shard, axis=0)
        d_rows = jnp.concatenate([lat_cols, ctx_cols[None], jnp.zeros((16 - N_DEV - 1, mod_shard), F32)], axis=0)
        g_ada = _mm(a_rows, d_rows, ta=True, out_dtype=F32, tm=512, name="ada_dw_%d" % l)
        ada_res = _adamw_layer(ada_w, m_ada_w, v_ada_w, g_ada, None, l, ada_res, name="adamw_ada_%d" % l)
        c_rows = jnp.concatenate([ctx_cols[None], jnp.zeros((15, mod_shard), F32)], axis=0)
        dcc_part = dcc_part + _mm(c_rows, ada_w, tb=True, b_lead=l, out_dtype=F32, name="ada_dc_%d" % l)
    dcc_all = _all_gather_small(_pack_rows(dcc_part[0]), name="gather_dcc")[:, :d_model // LANES].reshape(N_CHIPS, 2, d_model)[:, 0]
    dcc = ((dcc_all[0] + dcc_all[1]) + dcc_all[2]) + dcc_all[3]
    sg = _sigmoid(c_ctx)
    g_c_ctx = dcc * (sg * (1.0 + c_ctx * (1.0 - sg)))

    small_w = [(c_ctx, m_c_ctx, v_c_ctx, g_c_ctx), (ada_b, m_ada_b, v_ada_b, g_ada_b),
               (norm_g, m_norm_g, v_norm_g, g_norm_g), (na_rpb, m_na_rpb, v_na_rpb, g_rpb),
               (ret_decay_logit, m_ret_decay_logit, v_ret_decay_logit, g_decay), (final_g, m_final_g, v_final_g, g_final_g)]
    sw_sizes = [int(np.prod(t[0].shape)) for t in small_w]
    sw_pads = [-(-s // LANES) * LANES for s in sw_sizes]

    def pack(j):
        return _pack_rows(jnp.concatenate([jnp.pad(t[j].reshape(-1), (0, pd - s)) for t, s, pd in zip(small_w, sw_sizes, sw_pads)]))

    pw_, pm_, pv_, pg_ = pack(0), pack(1), pack(2), pack(3)
    sw_out = _ew(lambda w, m, v, g: (g,) + _adamw_math(w, g, m, v),
                 [('t', pw_, 0, LANES), ('t', pm_, 0, LANES), ('t', pv_, 0, LANES), ('t', pg_, 0, LANES)],
                 [('t', LANES, F32)] * 4, rows=pw_.shape[0], tr=pw_.shape[0], name="adamw_small")
    sw_starts = np.cumsum([0] + sw_pads)
    big_res = finish_layer(0, sw_out[0], big_res)

    def unpack(arr, i):
        return arr.reshape(-1)[sw_starts[i]:sw_starts[i] + sw_sizes[i]].reshape(small_w[i][0].shape)

    sm = [[unpack(sw_out[j], i) for i in range(len(small_w))] for j in range(4)]
    def ordered(j):
        return [sm[j][0], ada_res[j], sm[j][1], sm[j][2], big_res[0][j], sm[j][3], sm[j][4],
                big_res[1][j], big_res[2][j], big_res[3][j], sm[j][5]]

    return (loss, grad_x, *ordered(0), *ordered(1), *ordered(2), *ordered(3))
```

```python
import functools
import math

import numpy as np
import jax
import jax.numpy as jnp
from jax import lax
from jax.experimental import pallas as pl
from jax.experimental.pallas import tpu as pltpu

GRID_W = 64
NA_HEAD_DIM = 128
NA_WIN_ROWS = 8
NA_WIN_COLS = 16
RET_KEY_DIM = 128
RET_VAL_DIM = 256
RET_CHUNK = 128
ROPE_BASE = 10000.0
NORM_EPS = 1e-6
MASK_VALUE = -1e30
ADAM_LR = 0.001
ADAM_B1 = 0.9
ADAM_B2 = 0.999
ADAM_EPS = 1e-08
ADAM_WD = 0.01
ADAM_STEP = 10

N_CHIPS = 4
N_DEV = 8
LANES = 128
VMEM_LIMIT = 56 * 1024 * 1024
BF16 = jnp.bfloat16
F32 = jnp.float32
MESH = pl.DeviceIdType.MESH
ANY = pl.BlockSpec(memory_space=pl.ANY)


def _tile(dim, pref, align=LANES):
    if dim <= pref:
        return dim
    t = (pref // align) * align
    while t >= align:
        if dim % t == 0:
            return t
        t -= align
    return dim


def _rows_per_tile(rows, width, tile_bytes=1 << 20):
    return _tile(rows, max(8, tile_bytes // (4 * width)), 8)


def _params(sem):
    return pltpu.CompilerParams(dimension_semantics=sem, vmem_limit_bytes=VMEM_LIMIT)


def _sigmoid(x):
    return 1.0 / (1.0 + jnp.exp(-x))


def _dot(a, b, ca, cb):
    return lax.dot_general(a, b, (((ca,), (cb,)), ((), ())), preferred_element_type=F32)


def _mm(a, b, *, ta=False, tb=False, a_lead=None, b_lead=None, out_dtype=BF16, tm=768, tn=512, tk=2048, name):
    ash = a.shape[1:] if a_lead is not None else a.shape
    bsh = b.shape[1:] if b_lead is not None else b.shape
    m, k = (ash[1], ash[0]) if ta else ash
    n, k2 = bsh if tb else (bsh[1], bsh[0])
    assert k == k2, (name, ash, bsh)
    tm, tn, tk = _tile(m, tm), _tile(n, tn), _tile(k, tk)
    nk = k // tk

    def lead(spec_shape, imap, l):
        if l is None:
            return pl.BlockSpec(spec_shape, imap)
        return pl.BlockSpec((None,) + spec_shape, lambda i, j, kk: (l,) + imap(i, j, kk))

    a_spec = lead((tk, tm), lambda i, j, kk: (kk, i), a_lead) if ta else lead((tm, tk), lambda i, j, kk: (i, kk), a_lead)
    b_spec = lead((tn, tk), lambda i, j, kk: (j, kk), b_lead) if tb else lead((tk, tn), lambda i, j, kk: (kk, j), b_lead)
    ca, cb = (0 if ta else 1), (1 if tb else 0)

    def body(a_ref, b_ref, o_ref, *scratch):
        part = _dot(a_ref[...].astype(BF16), b_ref[...].astype(BF16), ca, cb)
        if nk == 1:
            o_ref[...] = part.astype(o_ref.dtype)
            return
        acc_ref, = scratch
        kk = pl.program_id(2)

        @pl.when(kk == 0)
        def _():
            acc_ref[...] = part

        @pl.when(kk > 0)
        def _():
            acc_ref[...] += part

        @pl.when(kk == nk - 1)
        def _():
            o_ref[...] = acc_ref[...].astype(o_ref.dtype)

    return pl.pallas_call(
        body, name=name, grid=(m // tm, n // tn, nk),
        in_specs=[a_spec, b_spec],
        out_specs=pl.BlockSpec((tm, tn), lambda i, j, kk: (i, j)),
        out_shape=jax.ShapeDtypeStruct((m, n), out_dtype),
        scratch_shapes=[] if nk == 1 else [pltpu.VMEM((tm, tn), F32)],
        compiler_params=_params(("parallel", "parallel", "arbitrary")),
    )(a, b)


def _ew(fn, ins, outs, *, rows, tr, name, n0=None, aliases=None):
    assert rows % tr == 0, (name, rows, tr)
    nt = rows // tr

    def grp(i):
        return 0 if n0 is None else jnp.where(i < n0, 0, 1)

    in_specs, args = [], []
    for spec in ins:
        if spec[0] == 't':
            arr, cb, w = spec[1], spec[2], spec[3]
            l = spec[4] if len(spec) > 4 else None
            if l is None:
                in_specs.append(pl.BlockSpec((tr, w), functools.partial(lambda i, cb: (i, cb), cb=cb)))
            else:
                in_specs.append(pl.BlockSpec((None, tr, w), functools.partial(lambda i, cb, l: (l, i, cb), cb=cb, l=l)))
            args.append(arr)
        else:
            arr = spec[1]
            g = arr.shape[0]
            if g == 1:
                in_specs.append(pl.BlockSpec((None, 1, arr.shape[2]), lambda i: (0, 0, 0)))
            else:
                in_specs.append(pl.BlockSpec((None, 1, arr.shape[2]), lambda i: (grp(i), 0, 0)))
            args.append(arr)
    out_specs, out_shapes, is_red = [], [], []
    for spec in outs:
        if spec[0] == 't':
            w, dt = spec[1], spec[2]
            if len(spec) > 3:
                l, nl = spec[3], spec[4]
                out_specs.append(pl.BlockSpec((None, tr, w), functools.partial(lambda i, l: (l, i, 0), l=l)))
                out_shapes.append(jax.ShapeDtypeStruct((nl, rows, w), dt))
            else:
                out_specs.append(pl.BlockSpec((tr, w), lambda i: (i, 0)))
                out_shapes.append(jax.ShapeDtypeStruct((rows, w), dt))
            is_red.append(False)
        else:
            w, g = spec[1], spec[2]
            if g == 1:
                out_specs.append(pl.BlockSpec((None, 1, w), lambda i: (0, 0, 0)))
            else:
                out_specs.append(pl.BlockSpec((None, 1, w), lambda i: (grp(i), 0, 0)))
            out_shapes.append(jax.ShapeDtypeStruct((g, 1, w), F32))
            is_red.append(True)
    n_in = len(ins)
    n_alias = 0 if aliases is None else len(aliases)

    def body(*refs):
        in_refs = refs[:n_in]
        out_refs = refs[n_in + n_alias:]
        res = fn(*[r[...] for r in in_refs])
        if not isinstance(res, (tuple, list)):
            res = (res,)
        i = pl.program_id(0)
        first = (i == 0) if n0 is None else ((i == 0) | (i == n0))
        for o_ref, val, red in zip(out_refs, res, is_red):
            if not red:
                o_ref[...] = val.astype(o_ref.dtype)
            else:
                @pl.when(first)
                def _(o_ref=o_ref, val=val):
                    o_ref[...] = val

                @pl.when(jnp.logical_not(first))
                def _(o_ref=o_ref, val=val):
                    o_ref[...] += val

    io_alias = {}
    if aliases is not None:
        for a_idx, (arr, o_idx) in enumerate(aliases):
            in_specs.append(ANY)
            args.append(arr)
            io_alias[n_in + a_idx] = o_idx
    has_red = any(is_red)
    return pl.pallas_call(
        body, name=name, grid=(nt,), in_specs=in_specs, out_specs=out_specs, out_shape=out_shapes,
        input_output_aliases=io_alias,
        compiler_params=_params(("arbitrary",) if has_red else ("parallel",)),
    )(*args)


def _rsum(v):
    return jnp.sum(v, axis=0, keepdims=True)


def _silu_parts(z):
    sg = _sigmoid(z)
    return z * sg, sg * (1.0 + z * (1.0 - sg))


def _na_bias_table(rpb, rows, *, name):
    kh, kw = NA_WIN_ROWS, NA_WIN_COLS
    assert rows >= kh
    heads = rpb.shape[0]
    e1, e2 = _na_onehots()
    rpb16 = jnp.pad(rpb, ((0, 0), (0, 16 - rpb.shape[1]), (0, LANES - rpb.shape[2])))

    def body(r_ref, e1_ref, e2_ref, o_ref):
        e1b = e1_ref[...].astype(BF16)
        y = sum(_dot(e1b, part, 0, 0) for part in _split3(r_ref[...]))
        e2b = e2_ref[...].astype(BF16)
        o_ref[...] = sum(_dot(part, e2b, 1, 1) for part in _split3(y))

    z = pl.pallas_call(
        body, name=name, grid=(heads,),
        in_specs=[pl.BlockSpec((None, 16, LANES), lambda h: (h, 0, 0)),
                  pl.BlockSpec(e1.shape, lambda h: (0, 0)), pl.BlockSpec(e2.shape, lambda h: (0, 0))],
        out_specs=pl.BlockSpec((None, kh * kh, GRID_W * GRID_W), lambda h: (h, 0, 0)),
        out_shape=jax.ShapeDtypeStruct((heads, kh * kh, GRID_W * GRID_W), F32),
        compiler_params=_params(("parallel",)),
    )(rpb16, e1, e2)
    cidx = np.arange(GRID_W)
    c0 = np.clip(cidx - kw // 2, 0, GRID_W - kw)
    col_in = (cidx[None, :] >= c0[:, None]) & (cidx[None, :] < c0[:, None] + kw)
    bias = z.reshape(heads, kh, kh, GRID_W, GRID_W).transpose(0, 1, 3, 2, 4)
    bias = jnp.where(col_in[None, None, :, None, :], bias, MASK_VALUE)
    return bias.reshape(heads, kh, GRID_W, kh * GRID_W)


def _na_onehots():
    kh, kw = NA_WIN_ROWS, NA_WIN_COLS
    cidx = np.arange(GRID_W)
    dc = cidx[None, :] - cidx[:, None] + (kw - 1)
    e2 = np.zeros((GRID_W * GRID_W, LANES), np.float32)
    ok = (dc >= 0) & (dc <= 2 * kw - 2)
    cq, ck = np.nonzero(ok)
    e2[cq * GRID_W + ck, dc[cq, ck]] = 1.0
    dr = np.arange(kh)[None, :] - np.arange(kh)[:, None] + (kh - 1)
    e1 = np.zeros((16, kh * kh), np.float32)
    dl, kr = np.nonzero(np.ones_like(dr))
    e1[dr[dl, kr], dl * kh + kr] = 1.0
    return jnp.asarray(e1), jnp.asarray(e2)


def _na_row_scores(q, kl, kc, bias, scale):
    s_loc = _dot(q, kl, 1, 1) * scale + bias
    s_ctx = _dot(q, kc, 1, 1) * scale
    m = jnp.maximum(jnp.max(s_loc, axis=-1, keepdims=True), jnp.max(s_ctx, axis=-1, keepdims=True))
    p_loc = jnp.exp(s_loc - m)
    p_ctx = jnp.exp(s_ctx - m)
    den = jnp.sum(p_loc, axis=-1, keepdims=True) + jnp.sum(p_ctx, axis=-1, keepdims=True)
    return p_loc, p_ctx, den


def _na_fwd(u, bias, *, s_len, heads, name):
    t_len = u.shape[0]
    rows = s_len // GRID_W
    nloc = NA_WIN_ROWS * GRID_W
    scale = NA_HEAD_DIM ** -0.5
    hd = NA_HEAD_DIM

    def body(q_ref, k_ref, v_ref, b_ref, o_ref):
        kc = k_ref[s_len:t_len, :]
        vc = v_ref[s_len:t_len, :]

        def row(r, carry):
            r0 = jnp.clip(r - NA_WIN_ROWS // 2, 0, rows - NA_WIN_ROWS)
            qs = pl.multiple_of(r * GRID_W, GRID_W)
            ks = pl.multiple_of(r0 * GRID_W, GRID_W)
            q = q_ref[pl.ds(qs, GRID_W), :]
            kl = k_ref[pl.ds(ks, nloc), :]
            vl = v_ref[pl.ds(ks, nloc), :]
            p_loc, p_ctx, den = _na_row_scores(q, kl, kc, b_ref[r - r0], scale)
            o = _dot(p_loc.astype(BF16), vl, 1, 0) + _dot(p_ctx.astype(BF16), vc, 1, 0)
            o_ref[pl.ds(qs, GRID_W), :] = (o / den).astype(o_ref.dtype)
            return carry

        lax.fori_loop(0, rows, row, 0)
        qc = q_ref[s_len:t_len, :]
        s = _dot(qc, kc, 1, 1) * scale
        p = jnp.exp(s - jnp.max(s, axis=-1, keepdims=True))
        o = _dot(p.astype(BF16), vc, 1, 0) / jnp.sum(p, axis=-1, keepdims=True)
        o_ref[s_len:t_len, :] = o.astype(o_ref.dtype)

    col = lambda off: pl.BlockSpec((t_len, hd), functools.partial(lambda h, off: (0, off + h), off=off))
    return pl.pallas_call(
        body, name=name, grid=(heads,),
        in_specs=[col(0), col(heads), col(2 * heads),
                  pl.BlockSpec((None, NA_WIN_ROWS, GRID_W, nloc), lambda h: (h, 0, 0, 0))],
        out_specs=pl.BlockSpec((t_len, hd), lambda h: (0, h)),
        out_shape=jax.ShapeDtypeStruct((t_len, heads * hd), BF16),
        compiler_params=_params(("parallel",)),
    )(u, u, u, bias)


def _na_bwd(u, bias, o, do, *, s_len, heads, name):
    t_len = u.shape[0]
    rows = s_len // GRID_W
    nloc = NA_WIN_ROWS * GRID_W
    scale = NA_HEAD_DIM ** -0.5
    hd = NA_HEAD_DIM

    def body(q_ref, k_ref, v_ref, b_ref, o_ref, do_ref, dq_ref, dk_ref, dv_ref, db_ref, dk_acc, dv_acc):
        kc = k_ref[s_len:t_len, :]
        vc = v_ref[s_len:t_len, :]
        dk_acc[...] = jnp.zeros_like(dk_acc)
        dv_acc[...] = jnp.zeros_like(dv_acc)
        db_ref[...] = jnp.zeros_like(db_ref)

        def row(r, carry):
            r0 = jnp.clip(r - NA_WIN_ROWS // 2, 0, rows - NA_WIN_ROWS)
            dl = r - r0
            qs = pl.multiple_of(r * GRID_W, GRID_W)
            ks = pl.multiple_of(r0 * GRID_W, GRID_W)
            q = q_ref[pl.ds(qs, GRID_W), :]
            kl = k_ref[pl.ds(ks, nloc), :]
            vl = v_ref[pl.ds(ks, nloc), :]
            dout = do_ref[pl.ds(qs, GRID_W), :]
            out = o_ref[pl.ds(qs, GRID_W), :]
            p_loc, p_ctx, den = _na_row_scores(q, kl, kc, b_ref[dl], scale)
            inv = 1.0 / den
            p_loc = p_loc * inv
            p_ctx = p_ctx * inv
            dlt = jnp.sum(dout.astype(F32) * out.astype(F32), axis=-1, keepdims=True)
            ds_loc = p_loc * (_dot(dout, vl, 1, 1) - dlt)
            ds_ctx = p_ctx * (_dot(dout, vc, 1, 1) - dlt)
            db_ref[dl] += ds_loc
            ds_loc_b = ds_loc.astype(BF16)
            ds_ctx_b = ds_ctx.astype(BF16)
            dq = (_dot(ds_loc_b, kl, 1, 0) + _dot(ds_ctx_b, kc, 1, 0)) * scale
            dq_ref[pl.ds(qs, GRID_W), :] = dq.astype(dq_ref.dtype)
            dk_acc[pl.ds(ks, nloc), :] += _dot(ds_loc_b, q, 0, 0) * scale
            dv_acc[pl.ds(ks, nloc), :] += _dot(p_loc.astype(BF16), dout, 0, 0)
            dk_acc[s_len:t_len, :] += _dot(ds_ctx_b, q, 0, 0) * scale
            dv_acc[s_len:t_len, :] += _dot(p_ctx.astype(BF16), dout, 0, 0)
            return carry

        lax.fori_loop(0, rows, row, 0)
        qc = q_ref[s_len:t_len, :]
        dout = do_ref[s_len:t_len, :]
        out = o_ref[s_len:t_len, :]
        s = _dot(qc, kc, 1, 1) * scale
        p = jnp.exp(s - jnp.max(s, axis=-1, keepdims=True))
        p = p / jnp.sum(p, axis=-1, keepdims=True)
        dlt = jnp.sum(dout.astype(F32) * out.astype(F32), axis=-1, keepdims=True)
        ds = (p * (_dot(dout, vc, 1, 1) - dlt)).astype(BF16)
        dq_ref[s_len:t_len, :] = (_dot(ds, kc, 1, 0) * scale).astype(dq_ref.dtype)
        dk_acc[s_len:t_len, :] += _dot(ds, qc, 0, 0) * scale
        dv_acc[s_len:t_len, :] += _dot(p.astype(BF16), dout, 0, 0)
        dk_ref[...] = dk_acc[...].astype(dk_ref.dtype)
        dv_ref[...] = dv_acc[...].astype(dv_ref.dtype)

    col = lambda off: pl.BlockSpec((t_len, hd), functools.partial(lambda h, off: (0, off + h), off=off))
    tbl = pl.BlockSpec((None, NA_WIN_ROWS, GRID_W, nloc), lambda h: (h, 0, 0, 0))
    tok = jax.ShapeDtypeStruct((t_len, heads * hd), BF16)
    return pl.pallas_call(
        body, name=name, grid=(heads,),
        in_specs=[col(0), col(heads), col(2 * heads), tbl, col(0), col(0)],
        out_specs=[col(0), col(0), col(0), tbl],
        out_shape=[tok, tok, tok, jax.ShapeDtypeStruct(bias.shape, F32)],
        scratch_shapes=[pltpu.VMEM((t_len, hd), F32), pltpu.VMEM((t_len, hd), F32)],
        compiler_params=_params(("parallel",)),
    )(u, u, u, bias, o, do)


def _split3(x):
    hi = x.astype(BF16)
    r1 = x - hi.astype(F32)
    mid = r1.astype(BF16)
    lo = (r1 - mid.astype(F32)).astype(BF16)
    return hi, mid, lo


def _rpb_grad(dbias, *, name):
    heads = dbias.shape[0]
    kh = NA_WIN_ROWS
    e1, e2 = _na_onehots()
    x = dbias.reshape(heads, kh, GRID_W, kh, GRID_W).transpose(0, 1, 3, 2, 4).reshape(heads, kh * kh, GRID_W * GRID_W)

    def body(x_ref, e1_ref, e2_ref, o_ref):
        e2b = e2_ref[...].astype(BF16)
        y = sum(_dot(part, e2b, 1, 0) for part in _split3(x_ref[...]))
        e1b = e1_ref[...].astype(BF16)
        o_ref[...] = sum(_dot(e1b, part, 1, 0) for part in _split3(y))

    out = pl.pallas_call(
        body, name=name, grid=(heads,),
        in_specs=[pl.BlockSpec((None, kh * kh, GRID_W * GRID_W), lambda h: (h, 0, 0)),
                  pl.BlockSpec(e1.shape, lambda h: (0, 0)), pl.BlockSpec(e2.shape, lambda h: (0, 0))],
        out_specs=pl.BlockSpec((None, 16, LANES), lambda h: (h, 0, 0)),
        out_shape=jax.ShapeDtypeStruct((heads, 16, LANES), F32),
        compiler_params=_params(("parallel",)),
    )(x, e1, e2)
    return out[:, :2 * kh - 1, :2 * NA_WIN_COLS - 1]


def _rope_tables(s_len, l_len):
    nf = RET_KEY_DIM // 4
    t = np.arange(s_len)
    row = (t // GRID_W).astype(np.float32)
    colp = (t % GRID_W).astype(np.float32)
    inv_freq = jnp.asarray(ROPE_BASE, F32) ** (-jnp.arange(nf, dtype=F32) / nf)
    ang = jnp.concatenate([jnp.asarray(row)[:, None] * inv_freq, jnp.asarray(colp)[:, None] * inv_freq], axis=-1)
    cos, sin = jnp.cos(ang), jnp.sin(ang)
    c2 = jnp.concatenate([cos, cos], axis=-1)
    s2 = jnp.concatenate([-sin, sin], axis=-1)
    c2 = jnp.concatenate([c2, jnp.ones((l_len, RET_KEY_DIM), F32)], axis=0)
    s2 = jnp.concatenate([s2, jnp.zeros((l_len, RET_KEY_DIM), F32)], axis=0)
    return c2, s2


def _rope(x, c2, s2):
    return x * c2 + pltpu.roll(x, RET_KEY_DIM // 2, 1) * s2


def _rope_t(d, c2, s2):
    return d * c2 + pltpu.roll(d * s2, RET_KEY_DIM // 2, 1)


def _ret_decays(lg, direction):
    cs = RET_CHUNK
    i_col = lax.broadcasted_iota(jnp.int32, (cs, 1), 0)
    p_col = jnp.where(direction == 0, i_col, cs - 1 - i_col).astype(F32)
    pi = lax.broadcasted_iota(jnp.int32, (cs, cs), 0)
    pj = lax.broadcasted_iota(jnp.int32, (cs, cs), 1)
    diff = jnp.where(direction == 0, pi - pj, pj - pi).astype(F32)
    dm = jnp.where(diff >= 0, jnp.exp(jnp.maximum(diff, 0.0) * lg), 0.0)
    qdec = jnp.exp((p_col + 1.0) * lg)
    kdec = jnp.exp((cs - 1.0 - p_col) * lg)
    cd = jnp.exp(jnp.full((1, 1), cs, F32) * lg)
    return p_col, dm, qdec, kdec, cd


def _ret_chunk_index(t, direction, n_chunks, lat_chunks):
    return jnp.where(direction == 0, lax.rem(t + lat_chunks, n_chunks), n_chunks - 1 - t)


def _ret_fwd(u, c2, s2, lg, *, s_len, heads, q_off, name):
    t_len = u.shape[0]
    cs, dk, dv = RET_CHUNK, RET_KEY_DIM, RET_VAL_DIM
    n_chunks, lat_chunks = t_len // cs, s_len // cs
    k_scale = dk ** -0.5
    qb, kb, vb = q_off // dk, q_off // dk + heads, (q_off + 2 * heads * dk) // dv

    def body(lg_ref, q_ref, k_ref, v_ref, c_ref, s_ref, o_ref, st_ref, state):
        h, d = pl.program_id(0), pl.program_id(1)
        _, dm, qdec, kdec, cd = _ret_decays(lg_ref[d, h], d)
        state[...] = jnp.zeros_like(state)

        def step(t, carry):
            c = _ret_chunk_index(t, d, n_chunks, lat_chunks)
            r = pl.ds(pl.multiple_of(c * cs, cs), cs)
            cc, ss = c_ref[r, :], s_ref[r, :]
            qc = _rope(q_ref[r, :].astype(F32), cc, ss)
            kc = _rope(k_ref[r, :].astype(F32), cc, ss) * k_scale
            vc = v_ref[r, :]
            st = state[...]
            st_ref[t] = st
            a = _dot(qc.astype(BF16), kc.astype(BF16), 1, 1) * dm
            oc = _dot(a.astype(BF16), vc, 1, 0) + _dot((qc * qdec).astype(BF16), st.astype(BF16), 1, 0)
            state[...] = st * cd + _dot((kc * kdec).astype(BF16), vc, 0, 0)

            @pl.when(d == 0)
            def _():
                o_ref[r, :] = oc

            @pl.when(d == 1)
            def _():
                o_ref[r, :] += oc

            return carry

        lax.fori_loop(0, n_chunks, step, 0)

    return pl.pallas_call(
        body, name=name, grid=(heads, 2),
        in_specs=[pl.BlockSpec(memory_space=pltpu.SMEM),
                  pl.BlockSpec((t_len, dk), lambda h, d: (0, qb + h)),
                  pl.BlockSpec((t_len, dk), lambda h, d: (0, kb + h)),
                  pl.BlockSpec((t_len, dv), lambda h, d: (0, vb + h)),
                  pl.BlockSpec((t_len, dk), lambda h, d: (0, 0)),
                  pl.BlockSpec((t_len, dk), lambda h, d: (0, 0))],
        out_specs=[pl.BlockSpec((t_len, dv), lambda h, d: (0, h)),
                   pl.BlockSpec((None, None, n_chunks, dk, dv), lambda h, d: (h, d, 0, 0, 0))],
        out_shape=[jax.ShapeDtypeStruct((t_len, heads * dv), F32),
                   jax.ShapeDtypeStruct((heads, 2, n_chunks, dk, dv), F32)],
        scratch_shapes=[pltpu.VMEM((dk, dv), F32)],
        compiler_params=_params(("parallel", "arbitrary")),
    )(lg, u, u, u, c2, s2)


def _ret_bwd(u, c2, s2, lg, states, do, *, s_len, heads, q_off, name):
    t_len = u.shape[0]
    cs, dk, dv = RET_CHUNK, RET_KEY_DIM, RET_VAL_DIM
    n_chunks, lat_chunks = t_len // cs, s_len // cs
    k_scale = dk ** -0.5
    qb, kb, vb = q_off // dk, q_off // dk + heads, (q_off + 2 * heads * dk) // dv

    def body(lg_ref, q_ref, k_ref, v_ref, c_ref, s_ref, st_ref, do_ref, dq_ref, dk_ref, dv_ref, dlg_ref, dstate, acc):
        h, d = pl.program_id(0), pl.program_id(1)
        p_col, dm, qdec, kdec, cd = _ret_decays(lg_ref[d, h], d)
        dstate[...] = jnp.zeros_like(dstate)
        acc[...] = jnp.zeros_like(acc)

        def step(i, carry):
            t = n_chunks - 1 - i
            c = _ret_chunk_index(t, d, n_chunks, lat_chunks)
            r = pl.ds(pl.multiple_of(c * cs, cs), cs)
            cc, ss = c_ref[r, :], s_ref[r, :]
            qc = _rope(q_ref[r, :].astype(F32), cc, ss)
            kc = _rope(k_ref[r, :].astype(F32), cc, ss) * k_scale
            vc = v_ref[r, :]
            doc = do_ref[r, :].astype(BF16)
            st = st_ref[t]
            dst = dstate[...]
            qb16, kb16 = qc.astype(BF16), kc.astype(BF16)
            a = _dot(qb16, kb16, 1, 1) * dm
            dam = (_dot(doc, vc, 1, 1) * dm).astype(BF16)
            dq_i = _dot(dam, kb16, 1, 0)
            dk_i = _dot(dam, qb16, 0, 0)
            dq_c = _dot(doc, st.astype(BF16), 1, 1) * qdec
            dst16 = dst.astype(BF16)
            dvc = _dot(a.astype(BF16), doc, 0, 0) + _dot((kc * kdec).astype(BF16), dst16, 1, 0)
            dk_s = _dot(vc, dst16, 1, 1) * kdec
            g = (jnp.sum(qc * (p_col * dq_i + (p_col + 1.0) * dq_c), axis=-1, keepdims=True)
                 + jnp.sum(kc * ((cs - 1.0 - p_col) * dk_s - p_col * dk_i), axis=-1, keepdims=True))
            g = jnp.sum(g, axis=0, keepdims=True) + cs * cd * jnp.sum(jnp.sum(dst * st, axis=-1, keepdims=True), axis=0, keepdims=True)
            acc[...] += jnp.broadcast_to(g, acc.shape)
            dstate[...] = dst * cd + _dot((qc * qdec).astype(BF16), doc, 0, 0)
            dq = _rope_t(dq_i + dq_c, cc, ss)
            dkk = _rope_t((dk_i + dk_s) * k_scale, cc, ss)

            @pl.when(d == 0)
            def _():
                dq_ref[r, :] = dq.astype(dq_ref.dtype)
                dk_ref[r, :] = dkk.astype(dk_ref.dtype)
                dv_ref[r, :] = dvc.astype(dv_ref.dtype)

            @pl.when(d == 1)
            def _():
                dq_ref[r, :] = (dq_ref[r, :].astype(F32) + dq).astype(dq_ref.dtype)
                dk_ref[r, :] = (dk_ref[r, :].astype(F32) + dkk).astype(dk_ref.dtype)
                dv_ref[r, :] = (dv_ref[r, :].astype(F32) + dvc).astype(dv_ref.dtype)

            return carry

        lax.fori_loop(0, n_chunks, step, 0)
        dlg_ref[...] = acc[...]

    return pl.pallas_call(
        body, name=name, grid=(heads, 2),
        in_specs=[pl.BlockSpec(memory_space=pltpu.SMEM),
                  pl.BlockSpec((t_len, dk), lambda h, d: (0, qb + h)),
                  pl.BlockSpec((t_len, dk), lambda h, d: (0, kb + h)),
                  pl.BlockSpec((t_len, dv), lambda h, d: (0, vb + h)),
                  pl.BlockSpec((t_len, dk), lambda h, d: (0, 0)),
                  pl.BlockSpec((t_len, dk), lambda h, d: (0, 0)),
                  pl.BlockSpec((None, None, n_chunks, dk, dv), lambda h, d: (h, d, 0, 0, 0)),
                  pl.BlockSpec((t_len, dv), lambda h, d: (0, h))],
        out_specs=[pl.BlockSpec((t_len, dk), lambda h, d: (0, h)),
                   pl.BlockSpec((t_len, dk), lambda h, d: (0, h)),
                   pl.BlockSpec((t_len, dv), lambda h, d: (0, h)),
                   pl.BlockSpec((None, None, 8, LANES), lambda h, d: (h, d, 0, 0))],
        out_shape=[jax.ShapeDtypeStruct((t_len, heads * dk), BF16),
                   jax.ShapeDtypeStruct((t_len, heads * dk), BF16),
                   jax.ShapeDtypeStruct((t_len, heads * dv), BF16),
                   jax.ShapeDtypeStruct((heads, 2, 8, LANES), F32)],
        scratch_shapes=[pltpu.VMEM((dk, dv), F32), pltpu.VMEM((8, LANES), F32)],
        compiler_params=_params(("parallel", "arbitrary")),
    )(lg, u, u, u, c2, s2, states, do)


def _mesh_pos():
    return lax.axis_index("x"), lax.axis_index("y"), lax.axis_index("c")


def _all_gather_small(buf, *, name):
    r = buf.shape[0]

    def body(x_ref, o_ref, send_sems, recv_sems, local_sem):
        x, y, c = _mesh_pos()
        me = 4 * x + 2 * y + c
        mine = pltpu.make_async_copy(x_ref, o_ref.at[me], local_sem)
        mine.start()
        copies = []
        for k in range(1, N_DEV):
            px, py, pc = x ^ ((k >> 2) & 1), y ^ ((k >> 1) & 1), c ^ (k & 1)
            cp = pltpu.make_async_remote_copy(
                src_ref=x_ref, dst_ref=o_ref.at[me], send_sem=send_sems.at[k - 1], recv_sem=recv_sems.at[k - 1],
                device_id=(px, py, pc), device_id_type=MESH)
            cp.start()
            copies.append((cp, 4 * px + 2 * py + pc))
        for k, (cp, peer) in enumerate(copies):
            pltpu.make_async_remote_copy(
                src_ref=x_ref, dst_ref=o_ref.at[peer], send_sem=send_sems.at[k], recv_sem=recv_sems.at[k],
                device_id=(x, y, c), device_id_type=MESH).wait_recv()
        for cp, _ in copies:
            cp.wait_send()
        mine.wait()

    return pl.pallas_call(
        body, name=name,
        in_specs=[pl.BlockSpec(memory_space=pltpu.VMEM)],
        out_specs=pl.BlockSpec(memory_space=pltpu.VMEM),
        out_shape=jax.ShapeDtypeStruct((N_DEV, r, LANES), F32),
        scratch_shapes=[pltpu.SemaphoreType.DMA((N_DEV - 1,)), pltpu.SemaphoreType.DMA((N_DEV - 1,)),
                        pltpu.SemaphoreType.DMA],
        compiler_params=pltpu.CompilerParams(vmem_limit_bytes=VMEM_LIMIT),
    )(buf)


def _cut(ref, shard_axis, *, chip=None, half=None, lead=None):
    shape = ref.shape[1:] if lead is not None else ref.shape
    idx = [slice(None), slice(None)]
    if chip is not None:
        w = shape[shard_axis] // N_CHIPS
        idx[shard_axis] = pl.ds(pl.multiple_of(chip * w, w), w)
    if half is not None:
        hw = shape[1 - shard_axis] // 2
        idx[1 - shard_axis] = pl.ds(pl.multiple_of(half * hw, hw), hw)
    if lead is not None:
        idx = [lead] + idx
    return ref.at[tuple(idx)]


def _wait_recv(ref, send_sem, recv_sem):
    pltpu.make_async_remote_copy(src_ref=ref, dst_ref=ref, send_sem=send_sem, recv_sem=recv_sem,
                                 device_id=_mesh_pos(), device_id_type=MESH).wait_recv()


def _gather_plan(axes):
    def plan(srcs, lands, send_sems, recv_sems):
        x, y, c = _mesh_pos()
        chip = 2 * x + y
        copies = []
        for i, ax in enumerate(axes):
            for k in range(1, N_CHIPS):
                px, py = x ^ (k >> 1), y ^ (k & 1)
                mine = _cut(lands[i], ax, chip=chip, half=c)
                j = i * (N_CHIPS - 1) + k - 1
                sems = dict(send_sem=send_sems.at[j], recv_sem=recv_sems.at[j], device_id=(px, py, c), device_id_type=MESH)
                send = pltpu.make_async_remote_copy(src_ref=mine, dst_ref=mine, **sems)
                recv = pltpu.make_async_remote_copy(src_ref=mine, dst_ref=_cut(lands[i], ax, chip=2 * px + py, half=c), **sems)
                copies.append((send, recv))
        return copies
    return plan


def _scatter_plan(axes):
    def plan(srcs, lands, send_sems, recv_sems):
        x, y, c = _mesh_pos()
        copies = []
        for i, ax in enumerate(axes):
            for k in range(1, N_CHIPS):
                px, py = x ^ (k >> 1), y ^ (k & 1)
                j = i * (N_CHIPS - 1) + k - 1
                cp = pltpu.make_async_remote_copy(
                    src_ref=_cut(srcs[i], ax, chip=2 * px + py), dst_ref=lands[i].at[k - 1],
                    send_sem=send_sems.at[j], recv_sem=recv_sems.at[j], device_id=(px, py, c), device_id_type=MESH)
                copies.append((cp, cp))
        return copies
    return plan


HBM = pl.BlockSpec(memory_space=pltpu.HBM)
SEM = pl.BlockSpec(memory_space=pltpu.SEMAPHORE)
EFFECT = pltpu.SideEffectType.DATAFLOW_SIDE_EFFECTING


def _in_hbm(arrays):
    return [pltpu.with_memory_space_constraint(a, pltpu.HBM) for a in arrays]


def _split_start(srcs, lands, plan, n_copies, *, name):
    bufs = list(srcs) + list(lands)
    ns, nb = len(srcs), len(bufs)

    def body(*refs):
        send_sems, recv_sems, token = refs[nb], refs[nb + 1], refs[-1]
        for send, _ in plan(refs[:ns], refs[ns:nb], send_sems, recv_sems):
            send.start()
        token[...] = jnp.zeros_like(token)

    sems = pltpu.SemaphoreType.DMA((n_copies,))
    res = pl.pallas_call(
        body, name=name, in_specs=[HBM] * nb,
        out_specs=[SEM, SEM] + [HBM] * nb + [pl.BlockSpec(memory_space=pltpu.VMEM)],
        out_shape=[sems, sems] + [pltpu.HBM(a.shape, a.dtype) for a in bufs] + [jax.ShapeDtypeStruct((8, LANES), F32)],
        input_output_aliases={j: 2 + j for j in range(nb)},
        compiler_params=pltpu.CompilerParams(has_side_effects=EFFECT),
    )(*_in_hbm(bufs))
    return res[0], res[1], res[2:2 + ns], res[2 + ns:2 + nb], res[-1]


def _split_wait(started, after, plan, *, name):
    send_sems, recv_sems, srcs, lands, _ = started
    bufs = list(srcs) + list(lands)
    ns, nb = len(srcs), len(bufs)

    def body(*refs):
        for send, recv in plan(refs[:ns], refs[ns:nb], refs[nb], refs[nb + 1]):
            send.wait_send()
            recv.wait_recv()

    res = pl.pallas_call(
        body, name=name, in_specs=[HBM] * nb + [SEM, SEM, ANY], out_specs=[HBM] * nb,
        out_shape=[pltpu.HBM(a.shape, a.dtype) for a in bufs],
        input_output_aliases={j: j for j in range(nb)},
        compiler_params=pltpu.CompilerParams(has_side_effects=EFFECT),
    )(*bufs, send_sems, recv_sems, after)
    return res[ns:]


def _cast_into_full(w3, layer, ax, chip, *, name):
    _, r, wd = w3.shape
    tr = _rows_per_tile(r, wd, 4 << 20)
    nt = r // tr
    full_shape = (r, wd * N_CHIPS) if ax == 1 else (r * N_CHIPS, wd)
    out_map = (lambda i, ch: (i, ch[0])) if ax == 1 else (lambda i, ch: (ch[0] * nt + i, 0))

    def body(chip_ref, w_ref, o_ref):
        o_ref[...] = w_ref[...].astype(o_ref.dtype)

    return pl.pallas_call(
        body, name=name,
        grid_spec=pltpu.PrefetchScalarGridSpec(
            num_scalar_prefetch=1, grid=(nt,),
            in_specs=[pl.BlockSpec((None, tr, wd), lambda i, ch: (layer, i, 0))],
            out_specs=pl.BlockSpec((tr, wd), out_map)),
        out_shape=jax.ShapeDtypeStruct(full_shape, BF16),
        compiler_params=_params(("parallel",)),
    )(jnp.reshape(chip, (1,)).astype(jnp.int32), w3)


def _forward_halves(fulls, axes, *, name):
    n = len(fulls)

    def body(*refs):
        bufs = refs[:n]
        send_sems, recv_sems = refs[2 * n:]
        x, y, c = _mesh_pos()
        sends = []
        for i in range(n):
            for k in range(1, N_CHIPS):
                landed = _cut(bufs[i], axes[i], chip=2 * (x ^ (k >> 1)) + (y ^ (k & 1)), half=c)
                cp = pltpu.make_async_remote_copy(
                    src_ref=landed, dst_ref=landed, send_sem=send_sems.at[i, k - 1], recv_sem=recv_sems.at[i, k - 1],
                    device_id=(x, y, 1 - c), device_id_type=MESH)
                cp.start()
                sends.append(cp)
        for i in range(n):
            for k in range(1, N_CHIPS):
                other = _cut(bufs[i], axes[i], chip=2 * (x ^ (k >> 1)) + (y ^ (k & 1)), half=1 - c)
                _wait_recv(other, send_sems.at[i, k - 1], recv_sems.at[i, k - 1])
        for cp in sends:
            cp.wait_send()

    pairs = pltpu.SemaphoreType.DMA((n, N_CHIPS - 1))
    return pl.pallas_call(
        body, name=name, in_specs=[ANY] * n, out_specs=[ANY] * n,
        out_shape=[jax.ShapeDtypeStruct(a.shape, a.dtype) for a in fulls],
        input_output_aliases={j: j for j in range(n)},
        scratch_shapes=[pairs, pairs],
    )(*fulls)


def _send_to_sibling(parts, *, name):
    n = len(parts)

    def body(*refs):
        ins, outs = refs[:n], refs[n:2 * n]
        send_sems, recv_sems = refs[2 * n:]
        x, y, c = _mesh_pos()
        sends = []
        for i in range(n):
            cp = pltpu.make_async_remote_copy(
                src_ref=ins[i], dst_ref=outs[i], send_sem=send_sems.at[i], recv_sem=recv_sems.at[i],
                device_id=(x, y, 1 - c), device_id_type=MESH)
            cp.start()
            sends.append(cp)
        for cp in sends:
            cp.wait()

    sems = pltpu.SemaphoreType.DMA((n,))
    return pl.pallas_call(
        body, name=name, in_specs=[ANY] * n, out_specs=[ANY] * n,
        out_shape=[jax.ShapeDtypeStruct(p.shape, p.dtype) for p in parts],
        scratch_shapes=[sems, sems],
    )(*parts)


def _adamw_math(w, g, m, v):
    m = ADAM_B1 * m + (1.0 - ADAM_B1) * g
    v = ADAM_B2 * v + (1.0 - ADAM_B2) * (g * g)
    m_hat = m / (1.0 - ADAM_B1 ** ADAM_STEP)
    v_hat = v / (1.0 - ADAM_B2 ** ADAM_STEP)
    delta = -ADAM_LR * (m_hat / (jnp.sqrt(v_hat) + ADAM_EPS) + ADAM_WD * w)
    return delta, m, v


def _adamw_layer(w3, m3, v3, p, q, layer, prev, *, name):
    nl, rows, width = w3.shape
    tr = _rows_per_tile(rows, width)

    def fn(*t):
        if q is None:
            w, m, v, g = t
        else:
            w, m, v, g, g2 = t
            g = g + g2
        delta, m, v = _adamw_math(w, g, m, v)
        return g, delta, m, v

    ins = [('t', w3, 0, width, layer), ('t', m3, 0, width, layer), ('t', v3, 0, width, layer), ('t', p, 0, width)]
    if q is not None:
        ins.append(('t', q, 0, width))
    outs = [('t', width, F32, layer, nl)] * 4
    aliases = None if prev is None else [(prev[i], i) for i in range(4)]
    return _ew(fn, ins, outs, rows=rows, tr=tr, name=name, aliases=aliases)


def _pack_rows(vec):
    n = vec.shape[0]
    r = -(-n // (8 * LANES)) * 8
    return jnp.pad(vec, (0, r * LANES - n)).reshape(r, LANES)


def kernel(x, c, ctx, c_ctx, ada_w, ada_b, norm_g, w_in, na_rpb, ret_decay_logit, w_proj_na, w_proj_ret, w_out, final_g, loss_target, m_c_ctx, m_ada_w, m_ada_b, m_norm_g, m_w_in, m_na_rpb, m_ret_decay_logit, m_w_proj_na, m_w_proj_ret, m_w_out, m_final_g, v_c_ctx, v_ada_w, v_ada_b, v_norm_g, v_w_in, v_na_rpb, v_ret_decay_logit, v_w_proj_na, v_w_proj_ret, v_w_out, v_final_g):
    depth = w_in.shape[0]
    s_len, d_model = x.shape[1], x.shape[2]
    l_len = ctx.shape[1]
    t_len = s_len + l_len
    na_heads = na_rpb.shape[1]
    ret_heads = ret_decay_logit.shape[2]
    w_na = na_heads * NA_HEAD_DIM
    w_qk = ret_heads * RET_KEY_DIM
    w_v = ret_heads * RET_VAL_DIM
    in_cols = w_in.shape[2] * N_CHIPS
    assert in_cols == 4 * w_na + 2 * w_qk + 2 * w_v + 2 * d_model
    assert x.shape[0] == 1 and s_len % (NA_WIN_ROWS * GRID_W) == 0 and l_len % RET_CHUNK == 0
    off = np.cumsum([0, w_na, w_na, w_na, w_na, w_qk, w_qk, w_v, w_v, d_model, d_model])
    o_naz, o_retq, o_retz, o_gna, o_gret = int(off[3]), int(off[4]), int(off[7]), int(off[8]), int(off[9])
    rows = s_len // GRID_W
    tr = _tile(l_len, 256, 8)
    n0 = s_len // tr
    mod_cols = 3 * d_model
    mod_shard = ada_w.shape[2]

    xi, yi, ci = _mesh_pos()
    me = 4 * xi + 2 * yi + ci
    chip = 2 * xi + yi

    big_axes = [1, 1, 0, 0]
    n_big = len(big_axes) * (N_CHIPS - 1)
    gather_plan, scatter_plan = _gather_plan(big_axes), _scatter_plan(big_axes)

    c_silu = c[0] * _sigmoid(c[0])
    cc_silu = c_ctx * _sigmoid(c_ctx)
    c_all = _all_gather_small(_pack_rows(c_silu), name="gather_c")[:, :d_model // LANES].reshape(N_DEV, d_model)
    a_rows = jnp.concatenate([c_all, cc_silu[None], jnp.zeros((16 - N_DEV - 1, d_model), F32)], axis=0)
    mod_part = jnp.stack([_mm(a_rows, ada_w, b_lead=l, out_dtype=F32, name="ada_fwd_%d" % l) for l in range(depth)])
    mod_all = _all_gather_small(_pack_rows(mod_part.reshape(-1)), name="gather_mod")
    n_mod = depth * 16 * mod_shard
    mod_all = mod_all.reshape(N_DEV, -1)[:, :n_mod].reshape(N_CHIPS, 2, depth, 16, mod_shard)[:, 0]
    mod_all = jnp.transpose(mod_all, (1, 2, 0, 3)).reshape(depth, 16, mod_cols) + ada_b[:, None, :]

    fulls = [[_cast_into_full(w, l, ax, chip, name="cast_%s_%d" % (tag, l))
              for w, ax, tag in zip((w_in, w_proj_na, w_proj_ret, w_out), big_axes, ("w_in", "w_proj_na", "w_proj_ret", "w_out"))]
             for l in range(depth)]
    mod_all, fulls = lax.optimization_barrier((mod_all, fulls))
    gathers = [_split_start([], fulls[l], gather_plan, n_big, name="gather_start_%d" % l) for l in range(depth)]
    start_token = sum(g[4][0, 0] for g in gathers)
    mod_lat = lax.dynamic_index_in_dim(mod_all, me, axis=1, keepdims=False)
    mod_ctx = mod_all[:, N_DEV]

    c2, s2 = _rope_tables(s_len, l_len)
    log_gamma = jax.nn.log_sigmoid(ret_decay_logit)
    x_all = jnp.concatenate([x[0], ctx[0]], axis=0)

    def grp(lat_vec, ctx_vec):
        return jnp.stack([lat_vec, ctx_vec])[:, None, :]

    saved, full_w = [], []
    for l in range(depth):
        shift, scale, gate = [grp(mod_lat[l, i * d_model:(i + 1) * d_model], mod_ctx[l, i * d_model:(i + 1) * d_model])
                              for i in range(3)]
        gs = norm_g[l][None, None, :] * (1.0 + scale) + start_token

        def modnorm(xt, gs_t, sh_t):
            r = lax.rsqrt(jnp.mean(xt * xt, axis=-1, keepdims=True) + NORM_EPS)
            return xt * r * gs_t + sh_t

        h, = _ew(modnorm, [('t', x_all, 0, d_model), ('g', gs), ('g', shift)], [('t', d_model, BF16)],
                 rows=t_len, tr=tr, n0=n0, name="modnorm_%d" % l)
        bias = _na_bias_table(na_rpb[l], rows, name="na_bias_%d" % l)
        h, bias = lax.optimization_barrier((h, bias))
        landed = _split_wait(gathers[l], h, gather_plan, name="gather_wait_%d" % l)
        full_w.append(_forward_halves(landed, big_axes, name="gather_forward_%d" % l))
        win_f, wpn_f, wpr_f, wout_f = full_w[l]
        u = _mm(h, win_f, name="in_proj_%d" % l)
        o_na = _na_fwd(u, bias, s_len=s_len, heads=na_heads, name="na_fwd_%d" % l)
        o_ret, states = _ret_fwd(u, c2, s2, log_gamma[l], s_len=s_len, heads=ret_heads, q_off=o_retq, name="ret_fwd_%d" % l)

        def act(o1, z1, o2, z2):
            a1 = o1.astype(F32) * _silu_parts(z1.astype(F32))[0]
            sz = _silu_parts(z2.astype(F32))[0]
            outs = []
            for hh in range(ret_heads):
                sl = slice(hh * RET_VAL_DIM, (hh + 1) * RET_VAL_DIM)
                oh = o2[:, sl]
                r = lax.rsqrt(jnp.mean(oh * oh, axis=-1, keepdims=True) + NORM_EPS)
                outs.append(oh * r * sz[:, sl])
            return a1, jnp.concatenate(outs, axis=-1)

        a_na, a_ret = _ew(act, [('t', o_na, 0, w_na), ('t', u, o_naz // w_na, w_na), ('t', o_ret, 0, w_v), ('t', u, o_retz // w_v, w_v)],
                          [('t', w_na, BF16), ('t', w_v, BF16)], rows=t_len, tr=tr, name="act_%d" % l)
        y_na = _mm(a_na, wpn_f, name="proj_na_%d" % l)
        y_ret = _mm(a_ret, wpr_f, name="proj_ret_%d" % l)

        def merge(y1, y2, g1, g2):
            return _sigmoid(g1.astype(F32)) * y1.astype(F32) + _sigmoid(g2.astype(F32)) * y2.astype(F32)

        merged, = _ew(merge, [('t', y_na, 0, d_model), ('t', y_ret, 0, d_model), ('t', u, o_gna // d_model, d_model), ('t', u, o_gret // d_model, d_model)],
                      [('t', d_model, BF16)], rows=t_len, tr=tr, name="merge_%d" % l)
        out = _mm(merged, wout_f, out_dtype=F32, name="out_proj_%d" % l)
        x_new, = _ew(lambda xt, ot, gt: xt + gt * ot, [('t', x_all, 0, d_model), ('t', out, 0, d_model), ('g', gate)],
                     [('t', d_model, F32)], rows=t_len, tr=tr, n0=n0, name="resid_%d" % l)
        saved.append(dict(x=x_all, h=h, u=u, bias=bias, o_na=o_na, o_ret=o_ret, states=states, a_na=a_na, a_ret=a_ret,
                          y_na=y_na, y_ret=y_ret, merged=merged, out=out, gate=gate, gs=gs, scale=scale))
        x_all = x_new

    def final(xt, tt, gt):
        r = lax.rsqrt(jnp.mean(xt * xt, axis=-1, keepdims=True) + NORM_EPS)
        xh = xt * r
        e = xh * gt - tt
        dy = e * (1.0 / d_model)
        dyg = dy * gt
        dx = r * (dyg - xh * jnp.mean(dyg * xh, axis=-1, keepdims=True))
        return dx, _rsum(dy * xh), _rsum(e * e)

    dx_lat, d_final_g, loss_cols = _ew(final, [('t', x_all, 0, d_model), ('t', loss_target[0], 0, d_model), ('g', final_g[None, None, :])],
                                       [('t', d_model, F32), ('r', d_model, 1), ('r', d_model, 1)], rows=s_len, tr=tr, name="final")
    loss_part = (0.5 / d_model) * jnp.sum(loss_cols)
    dx_all = jnp.concatenate([dx_lat, jnp.zeros((l_len, d_model), F32)], axis=0)

    big_w = [(w_in, m_w_in, v_w_in), (w_proj_na, m_w_proj_na, v_w_proj_na), (w_proj_ret, m_w_proj_ret, v_w_proj_ret), (w_out, m_w_out, v_w_out)]
    big_res = [None] * 4
    scatters = [None] * depth
    back_token = jnp.zeros((), F32)

    def start_scatter(l, h, du, g_wpn, g_wpr, g_wout):
        g_win = _mm(h, du, ta=True, tm=512, tk=t_len, name="in_proj_dw_%d" % l)
        grads_l = [g_win, g_wpn, g_wpr, g_wout]
        half_sz = [g.shape[1 - ax] // 2 for g, ax in zip(grads_l, big_axes)]
        mine = [lax.dynamic_slice_in_dim(g, ci * hs, hs, axis=1 - ax) for g, hs, ax in zip(grads_l, half_sz, big_axes)]
        to_send = [lax.dynamic_slice_in_dim(g, (1 - ci) * hs, hs, axis=1 - ax) for g, hs, ax in zip(grads_l, half_sz, big_axes)]
        theirs = _send_to_sibling(to_send, name="pair_exchange_%d" % l)
        pair = []
        for i in range(4):
            pr, pw = mine[i].shape
            s, = _ew(lambda a, b: a.astype(F32) + b.astype(F32), [('t', mine[i], 0, pw), ('t', theirs[i], 0, pw)], [('t', pw, BF16)],
                     rows=pr, tr=_rows_per_tile(pr, pw), name="sum_pair_%d_%d" % (i, l))
            pair.append(s)
        own = [lax.dynamic_slice_in_dim(s, chip * (s.shape[ax] // N_CHIPS), s.shape[ax] // N_CHIPS, axis=ax) for s, ax in zip(pair, big_axes)]
        lands = [lax.empty((N_CHIPS - 1,) + o.shape, BF16) for o in own]
        scatters[l] = (_split_start(pair, lands, scatter_plan, n_big, name="scatter_start_%d" % l), own)
        return scatters[l][0][4]
    small = dict(dmod_lat=[None] * depth, dmod_ctx=[None] * depth, dnorm_g=[None] * depth, drpb=[None] * depth, ddecay=[None] * depth)
    for l in reversed(range(depth)):
        sv = saved[l]
        win_f, wpn_f, wpr_f, wout_f = full_w[l]

        def resid_bwd(dxt, ot, gt):
            return gt * dxt, _rsum(dxt * ot)

        dout, dgate = _ew(resid_bwd, [('t', dx_all, 0, d_model), ('t', sv['out'], 0, d_model), ('g', sv['gate'] + back_token)],
                          [('t', d_model, BF16), ('r', d_model, 2)], rows=t_len, tr=tr, n0=n0, name="resid_bwd_%d" % l)
        dmerged = _mm(dout, wout_f, tb=True, name="out_proj_dx_%d" % l)
        g_wout = _mm(sv['merged'], dout, ta=True, tm=512, tk=t_len, name="out_proj_dw_%d" % l)

        def merge_bwd(dm, y1, y2, g1, g2):
            dm = dm.astype(F32)
            s1, s2_ = _sigmoid(g1.astype(F32)), _sigmoid(g2.astype(F32))
            return dm * s1, dm * s2_, dm * y1.astype(F32) * s1 * (1.0 - s1), dm * y2.astype(F32) * s2_ * (1.0 - s2_)

        u = sv['u']
        dy_na, dy_ret, dg_na, dg_ret = _ew(
            merge_bwd, [('t', dmerged, 0, d_model), ('t', sv['y_na'], 0, d_model), ('t', sv['y_ret'], 0, d_model),
                        ('t', u, o_gna // d_model, d_model), ('t', u, o_gret // d_model, d_model)],
            [('t', d_model, BF16)] * 4, rows=t_len, tr=tr, name="merge_bwd_%d" % l)
        da_na = _mm(dy_na, wpn_f, tb=True, name="proj_na_dx_%d" % l)
        g_wpn = _mm(sv['a_na'], dy_na, ta=True, tm=512, tk=t_len, name="proj_na_dw_%d" % l)
        da_ret = _mm(dy_ret, wpr_f, tb=True, name="proj_ret_dx_%d" % l)
        g_wpr = _mm(sv['a_ret'], dy_ret, ta=True, tm=512, tk=t_len, name="proj_ret_dw_%d" % l)

        def act_bwd(da1, o1, z1, da2, o2, z2):
            da1, da2 = da1.astype(F32), da2.astype(F32)
            si1, ds1 = _silu_parts(z1.astype(F32))
            si2, ds2 = _silu_parts(z2.astype(F32))
            do1 = da1 * si1
            dz1 = da1 * o1.astype(F32) * ds1
            dn = da2 * si2
            do2, dz2 = [], []
            for hh in range(ret_heads):
                sl = slice(hh * RET_VAL_DIM, (hh + 1) * RET_VAL_DIM)
                oh = o2[:, sl]
                r = lax.rsqrt(jnp.mean(oh * oh, axis=-1, keepdims=True) + NORM_EPS)
                nh = oh * r
                dz2.append(da2[:, sl] * nh * ds2[:, sl])
                do2.append(r * (dn[:, sl] - nh * jnp.mean(dn[:, sl] * nh, axis=-1, keepdims=True)))
            return do1, dz1, jnp.concatenate(do2, axis=-1), jnp.concatenate(dz2, axis=-1)

        do_na, dz_na, do_ret, dz_ret = _ew(
            act_bwd, [('t', da_na, 0, w_na), ('t', sv['o_na'], 0, w_na), ('t', u, o_naz // w_na, w_na),
                      ('t', da_ret, 0, w_v), ('t', sv['o_ret'], 0, w_v), ('t', u, o_retz // w_v, w_v)],
            [('t', w_na, BF16), ('t', w_na, BF16), ('t', w_v, BF16), ('t', w_v, BF16)], rows=t_len, tr=tr, name="act_bwd_%d" % l)
        dq_na, dk_na, dv_na, dbias = _na_bwd(u, sv['bias'], sv['o_na'], do_na, s_len=s_len, heads=na_heads, name="na_bwd_%d" % l)
        small['drpb'][l] = _rpb_grad(dbias, name="rpb_grad_%d" % l)
        dq_r, dk_r, dv_r, dlg = _ret_bwd(u, c2, s2, log_gamma[l], sv['states'], do_ret, s_len=s_len, heads=ret_heads,
                                         q_off=o_retq, name="ret_bwd_%d" % l)
        small['ddecay'][l] = jnp.transpose(dlg[:, :, 0, 0]) * _sigmoid(-ret_decay_logit[l])
        du = jnp.concatenate([dq_na, dk_na, dv_na, dz_na, dq_r, dk_r, dv_r, dz_ret, dg_na, dg_ret], axis=1)
        dh = _mm(du, win_f, tb=True, out_dtype=F32, tn=1024, name="in_proj_dx_%d" % l)

        def modnorm_bwd(xt, dht, dxt, gs_t):
            r = lax.rsqrt(jnp.mean(xt * xt, axis=-1, keepdims=True) + NORM_EPS)
            xh = xt * r
            dhg = dht * gs_t
            dx = r * (dhg - xh * jnp.mean(dhg * xh, axis=-1, keepdims=True)) + dxt
            return dx, _rsum(dht), _rsum(dht * xh)

        dx_all, dshift, dgs = _ew(modnorm_bwd, [('t', sv['x'], 0, d_model), ('t', dh, 0, d_model), ('t', dx_all, 0, d_model), ('g', sv['gs'])],
                                  [('t', d_model, F32), ('r', d_model, 2), ('r', d_model, 2)], rows=t_len, tr=tr, n0=n0, name="modnorm_bwd_%d" % l)
        dscale = dgs * norm_g[l][None, None, :]
        small['dnorm_g'][l] = jnp.sum(dgs * (1.0 + sv['scale']), axis=(0, 1))
        dmod = jnp.concatenate([dshift, dscale, dgate], axis=-1)[:, 0]
        small['dmod_lat'][l], small['dmod_ctx'][l] = dmod[0], dmod[1]

        pending = (sv['h'], du, g_wpn, g_wpr, g_wout)
        if l > 0:
            back_token = start_scatter(l, *pending)[0, 0]

    def finish_layer(l, after, big_res):
        started, own = scatters[l]
        recv = _split_wait(started, after, scatter_plan, name="scatter_wait_%d" % l)
        parts = []
        for i, rbuf in enumerate(recv):
            pr, pw = own[i].shape
            p, = _ew(lambda a, b, c_, d: ((a.astype(F32) + b.astype(F32)) + c_.astype(F32)) + d.astype(F32),
                     [('t', own[i], 0, pw)] + [('t', rbuf, 0, pw, k) for k in range(N_CHIPS - 1)], [('t', pw, F32)],
                     rows=pr, tr=_rows_per_tile(pr, pw), name="sum_chips_%d_%d" % (i, l))
            parts.append(p)
        others = _send_to_sibling(parts, name="share_halves_%d" % l)
        shard_g = [jnp.where(ci == 0, jnp.concatenate([p, o], axis=1 - ax), jnp.concatenate([o, p], axis=1 - ax))
                   for p, o, ax in zip(parts, others, big_axes)]
        for i in range(4):
            w3, m3, v3 = big_w[i]
            big_res[i] = _adamw_layer(w3, m3, v3, shard_g[i], None, l, big_res[i], name="adamw_big_%d_%d" % (i, l))
        return big_res

    grad_x = dx_all[:s_len][None]

    drpb = jnp.stack(small['drpb']).reshape(-1)
    ddecay = jnp.stack(small['ddecay']).reshape(-1)
    pieces = [jnp.stack(small['dmod_lat']).reshape(-1), jnp.stack(small['dmod_ctx']).reshape(-1),
              jnp.stack(small['dnorm_g']).reshape(-1), d_final_g.reshape(-1), drpb, ddecay, loss_part[None]]
    sizes = [int(p.shape[0]) for p in pieces]
    pads = [-(-s // LANES) * LANES for s in sizes]
    packed = jnp.concatenate([jnp.pad(p, (0, pd - s)) for p, s, pd in zip(pieces, sizes, pads)])
    gathered = _all_gather_small(_pack_rows(packed), name="gather_small_grads")
    r_small = gathered.shape[1]

    def sum8(*t):
        acc = t[0]
        for other in t[1:]:
            acc = acc + other
        return acc

    total, = _ew(sum8, [('t', gathered, 0, LANES, k) for k in range(N_DEV)], [('t', LANES, F32)], rows=r_small, tr=r_small, name="sum_devices")
    total = total.reshape(-1)
    starts = np.cumsum([0] + pads)
    g_mod_lat_sum, g_mod_ctx, g_norm_g, g_final_g, g_rpb, g_decay, loss = [total[starts[i]:starts[i] + sizes[i]] for i in range(len(pieces))]
    loss = loss[0]
    g_ada_b = (g_mod_lat_sum + g_mod_ctx).reshape(depth, mod_cols)
    g_mod_ctx = g_mod_ctx.reshape(depth, mod_cols)
    dmod_lat_all = gathered.reshape(N_DEV, -1)[:, :depth * mod_cols].reshape(N_DEV, depth, mod_cols)

    dcc_part = jnp.zeros((16, d_model), F32)
    ctx_cols = [lax.dynamic_slice_in_dim(g_mod_ctx[l], chip * mod_shard, mod_shard, axis=0) for l in range(depth)]
    for l in reversed(range(depth)):
        c_rows = jnp.concatenate([ctx_cols[l][None], jnp.zeros((15, mod_shard), F32)], axis=0)
        dcc_part = dcc_part + _mm(c_rows, ada_w, tb=True, b_lead=l, out_dtype=F32, name="ada_dc_%d" % l)
    dcc_all = _all_gather_small(_pack_rows(dcc_part[0]), name="gather_dcc")[:, :d_model // LANES].reshape(N_CHIPS, 2, d_model)[:, 0]
    dcc = ((dcc_all[0] + dcc_all[1]) + dcc_all[2]) + dcc_all[3]
    sg = _sigmoid(c_ctx)
    g_c_ctx = dcc * (sg * (1.0 + c_ctx * (1.0 - sg)))

    h0, du0, *rest0 = pending
    du0, dcc_all = lax.optimization_barrier((du0, dcc_all))
    tail_token = start_scatter(0, h0, du0, *rest0)
    for l in reversed(range(1, depth)):
        big_res = finish_layer(l, tail_token, big_res)

    ada_res = None
    for l in reversed(range(depth)):
        lat_cols = lax.dynamic_slice_in_dim(dmod_lat_all[:, l], chip * mod_shard, mod_shard, axis=1)
        d_rows = jnp.concatenate([lat_cols, ctx_cols[l][None], jnp.zeros((16 - N_DEV - 1, mod_shard), F32)], axis=0) + tail_token[0, 0]
        g_ada = _mm(a_rows, d_rows, ta=True, out_dtype=F32, tm=512, name="ada_dw_%d" % l)
        ada_res = _adamw_layer(ada_w, m_ada_w, v_ada_w, g_ada, None, l, ada_res, name="adamw_ada_%d" % l)

    small_w = [(c_ctx, m_c_ctx, v_c_ctx, g_c_ctx), (ada_b, m_ada_b, v_ada_b, g_ada_b),
               (norm_g, m_norm_g, v_norm_g, g_norm_g), (na_rpb, m_na_rpb, v_na_rpb, g_rpb),
               (ret_decay_logit, m_ret_decay_logit, v_ret_decay_logit, g_decay), (final_g, m_final_g, v_final_g, g_final_g)]
    sw_sizes = [int(np.prod(t[0].shape)) for t in small_w]
    sw_pads = [-(-s // LANES) * LANES for s in sw_sizes]

    def pack(j):
        return _pack_rows(jnp.concatenate([jnp.pad(t[j].reshape(-1), (0, pd - s)) for t, s, pd in zip(small_w, sw_sizes, sw_pads)]))

    pw_, pm_, pv_, pg_ = pack(0), pack(1), pack(2), pack(3)
    sw_out = _ew(lambda w, m, v, g: (g,) + _adamw_math(w, g, m, v),
                 [('t', pw_, 0, LANES), ('t', pm_, 0, LANES), ('t', pv_, 0, LANES), ('t', pg_, 0, LANES)],
                 [('t', LANES, F32)] * 4, rows=pw_.shape[0], tr=pw_.shape[0], name="adamw_small")
    sw_starts = np.cumsum([0] + sw_pads)
    after_tail = lax.optimization_barrier((sw_out[0], ada_res, big_res))[0]
    big_res = finish_layer(0, after_tail, big_res)

    def unpack(arr, i):
        return arr.reshape(-1)[sw_starts[i]:sw_starts[i] + sw_sizes[i]].reshape(small_w[i][0].shape)

    sm = [[unpack(sw_out[j], i) for i in range(len(small_w))] for j in range(4)]
    def ordered(j):
        return [sm[j][0], ada_res[j], sm[j][1], sm[j][2], big_res[0][j], sm[j][3], sm[j][4],
                big_res[1][j], big_res[2][j], big_res[3][j], sm[j][5]]

    return (loss, grad_x, *ordered(0), *ordered(1), *ordered(2), *ordered(3))
```

```python
import functools
import math

import numpy as np
import jax
import jax.numpy as jnp
from jax import lax
from jax.experimental import pallas as pl
from jax.experimental.pallas import tpu as pltpu

GRID_W = 64
NA_HEAD_DIM = 128
NA_WIN_ROWS = 8
NA_WIN_COLS = 16
RET_KEY_DIM = 128
RET_VAL_DIM = 256
RET_CHUNK = 128
ROPE_BASE = 10000.0
NORM_EPS = 1e-6
MASK_VALUE = -1e30
ADAM_LR = 0.001
ADAM_B1 = 0.9
ADAM_B2 = 0.999
ADAM_EPS = 1e-08
ADAM_WD = 0.01
ADAM_STEP = 10

N_CHIPS = 4
N_DEV = 8
LANES = 128
VMEM_LIMIT = 56 * 1024 * 1024
BF16 = jnp.bfloat16
F32 = jnp.float32
MESH = pl.DeviceIdType.MESH
ANY = pl.BlockSpec(memory_space=pl.ANY)


def _tile(dim, pref, align=LANES):
    if dim <= pref:
        return dim
    t = (pref // align) * align
    while t >= align:
        if dim % t == 0:
            return t
        t -= align
    return dim


def _rows_per_tile(rows, width, tile_bytes=1 << 20):
    return _tile(rows, max(8, tile_bytes // (4 * width)), 8)


def _params(sem):
    return pltpu.CompilerParams(dimension_semantics=sem, vmem_limit_bytes=VMEM_LIMIT)


def _sigmoid(x):
    return 1.0 / (1.0 + jnp.exp(-x))


def _dot(a, b, ca, cb):
    return lax.dot_general(a, b, (((ca,), (cb,)), ((), ())), preferred_element_type=F32)


def _mm(a, b, *, ta=False, tb=False, a_lead=None, b_lead=None, out_dtype=BF16, tm=768, tn=512, tk=2048, name):
    ash = a.shape[1:] if a_lead is not None else a.shape
    bsh = b.shape[1:] if b_lead is not None else b.shape
    m, k = (ash[1], ash[0]) if ta else ash
    n, k2 = bsh if tb else (bsh[1], bsh[0])
    assert k == k2, (name, ash, bsh)
    tm, tn, tk = _tile(m, tm), _tile(n, tn), _tile(k, tk)
    nk = k // tk

    def lead(spec_shape, imap, l):
        if l is None:
            return pl.BlockSpec(spec_shape, imap)
        return pl.BlockSpec((None,) + spec_shape, lambda i, j, kk: (l,) + imap(i, j, kk))

    a_spec = lead((tk, tm), lambda i, j, kk: (kk, i), a_lead) if ta else lead((tm, tk), lambda i, j, kk: (i, kk), a_lead)
    b_spec = lead((tn, tk), lambda i, j, kk: (j, kk), b_lead) if tb else lead((tk, tn), lambda i, j, kk: (kk, j), b_lead)
    ca, cb = (0 if ta else 1), (1 if tb else 0)

    def body(a_ref, b_ref, o_ref, *scratch):
        part = _dot(a_ref[...].astype(BF16), b_ref[...].astype(BF16), ca, cb)
        if nk == 1:
            o_ref[...] = part.astype(o_ref.dtype)
            return
        acc_ref, = scratch
        kk = pl.program_id(2)

        @pl.when(kk == 0)
        def _():
            acc_ref[...] = part

        @pl.when(kk > 0)
        def _():
            acc_ref[...] += part

        @pl.when(kk == nk - 1)
        def _():
            o_ref[...] = acc_ref[...].astype(o_ref.dtype)

    return pl.pallas_call(
        body, name=name, grid=(m // tm, n // tn, nk),
        in_specs=[a_spec, b_spec],
        out_specs=pl.BlockSpec((tm, tn), lambda i, j, kk: (i, j)),
        out_shape=jax.ShapeDtypeStruct((m, n), out_dtype),
        scratch_shapes=[] if nk == 1 else [pltpu.VMEM((tm, tn), F32)],
        compiler_params=_params(("parallel", "parallel", "arbitrary")),
    )(a, b)


def _ew(fn, ins, outs, *, rows, tr, name, n0=None, aliases=None):
    assert rows % tr == 0, (name, rows, tr)
    nt = rows // tr

    def grp(i):
        return 0 if n0 is None else jnp.where(i < n0, 0, 1)

    in_specs, args = [], []
    for spec in ins:
        if spec[0] == 't':
            arr, cb, w = spec[1], spec[2], spec[3]
            l = spec[4] if len(spec) > 4 else None
            if l is None:
                in_specs.append(pl.BlockSpec((tr, w), functools.partial(lambda i, cb: (i, cb), cb=cb)))
            else:
                in_specs.append(pl.BlockSpec((None, tr, w), functools.partial(lambda i, cb, l: (l, i, cb), cb=cb, l=l)))
            args.append(arr)
        else:
            arr = spec[1]
            g = arr.shape[0]
            if g == 1:
                in_specs.append(pl.BlockSpec((None, 1, arr.shape[2]), lambda i: (0, 0, 0)))
            else:
                in_specs.append(pl.BlockSpec((None, 1, arr.shape[2]), lambda i: (grp(i), 0, 0)))
            args.append(arr)
    out_specs, out_shapes, is_red = [], [], []
    for spec in outs:
        if spec[0] == 't':
            w, dt = spec[1], spec[2]
            if len(spec) > 3:
                l, nl = spec[3], spec[4]
                out_specs.append(pl.BlockSpec((None, tr, w), functools.partial(lambda i, l: (l, i, 0), l=l)))
                out_shapes.append(jax.ShapeDtypeStruct((nl, rows, w), dt))
            else:
                out_specs.append(pl.BlockSpec((tr, w), lambda i: (i, 0)))
                out_shapes.append(jax.ShapeDtypeStruct((rows, w), dt))
            is_red.append(False)
        else:
            w, g = spec[1], spec[2]
            if g == 1:
                out_specs.append(pl.BlockSpec((None, 1, w), lambda i: (0, 0, 0)))
            else:
                out_specs.append(pl.BlockSpec((None, 1, w), lambda i: (grp(i), 0, 0)))
            out_shapes.append(jax.ShapeDtypeStruct((g, 1, w), F32))
            is_red.append(True)
    n_in = len(ins)
    n_alias = 0 if aliases is None else len(aliases)

    def body(*refs):
        in_refs = refs[:n_in]
        out_refs = refs[n_in + n_alias:]
        res = fn(*[r[...] for r in in_refs])
        if not isinstance(res, (tuple, list)):
            res = (res,)
        i = pl.program_id(0)
        first = (i == 0) if n0 is None else ((i == 0) | (i == n0))
        for o_ref, val, red in zip(out_refs, res, is_red):
            if not red:
                o_ref[...] = val.astype(o_ref.dtype)
            else:
                @pl.when(first)
                def _(o_ref=o_ref, val=val):
                    o_ref[...] = val

                @pl.when(jnp.logical_not(first))
                def _(o_ref=o_ref, val=val):
                    o_ref[...] += val

    io_alias = {}
    if aliases is not None:
        for a_idx, (arr, o_idx) in enumerate(aliases):
            in_specs.append(ANY)
            args.append(arr)
            io_alias[n_in + a_idx] = o_idx
    has_red = any(is_red)
    return pl.pallas_call(
        body, name=name, grid=(nt,), in_specs=in_specs, out_specs=out_specs, out_shape=out_shapes,
        input_output_aliases=io_alias,
        compiler_params=_params(("arbitrary",) if has_red else ("parallel",)),
    )(*args)


def _rsum(v):
    return jnp.sum(v, axis=0, keepdims=True)


def _silu_parts(z):
    sg = _sigmoid(z)
    return z * sg, sg * (1.0 + z * (1.0 - sg))


def _na_bias_table(rpb, rows, *, name):
    kh, kw = NA_WIN_ROWS, NA_WIN_COLS
    assert rows >= kh
    heads = rpb.shape[0]
    e1, e2 = _na_onehots()
    rpb16 = jnp.pad(rpb, ((0, 0), (0, 16 - rpb.shape[1]), (0, LANES - rpb.shape[2])))

    def body(r_ref, e1_ref, e2_ref, o_ref):
        e1b = e1_ref[...].astype(BF16)
        y = sum(_dot(e1b, part, 0, 0) for part in _split3(r_ref[...]))
        e2b = e2_ref[...].astype(BF16)
        o_ref[...] = sum(_dot(part, e2b, 1, 1) for part in _split3(y))

    z = pl.pallas_call(
        body, name=name, grid=(heads,),
        in_specs=[pl.BlockSpec((None, 16, LANES), lambda h: (h, 0, 0)),
                  pl.BlockSpec(e1.shape, lambda h: (0, 0)), pl.BlockSpec(e2.shape, lambda h: (0, 0))],
        out_specs=pl.BlockSpec((None, kh * kh, GRID_W * GRID_W), lambda h: (h, 0, 0)),
        out_shape=jax.ShapeDtypeStruct((heads, kh * kh, GRID_W * GRID_W), F32),
        compiler_params=_params(("parallel",)),
    )(rpb16, e1, e2)
    cidx = np.arange(GRID_W)
    c0 = np.clip(cidx - kw // 2, 0, GRID_W - kw)
    col_in = (cidx[None, :] >= c0[:, None]) & (cidx[None, :] < c0[:, None] + kw)
    bias = z.reshape(heads, kh, kh, GRID_W, GRID_W).transpose(0, 1, 3, 2, 4)
    bias = jnp.where(col_in[None, None, :, None, :], bias, MASK_VALUE)
    return bias.reshape(heads, kh, GRID_W, kh * GRID_W)


def _na_onehots():
    kh, kw = NA_WIN_ROWS, NA_WIN_COLS
    cidx = np.arange(GRID_W)
    dc = cidx[None, :] - cidx[:, None] + (kw - 1)
    e2 = np.zeros((GRID_W * GRID_W, LANES), np.float32)
    ok = (dc >= 0) & (dc <= 2 * kw - 2)
    cq, ck = np.nonzero(ok)
    e2[cq * GRID_W + ck, dc[cq, ck]] = 1.0
    dr = np.arange(kh)[None, :] - np.arange(kh)[:, None] + (kh - 1)
    e1 = np.zeros((16, kh * kh), np.float32)
    dl, kr = np.nonzero(np.ones_like(dr))
    e1[dr[dl, kr], dl * kh + kr] = 1.0
    return jnp.asarray(e1), jnp.asarray(e2)


def _na_row_scores(q, kl, kc, bias, scale):
    s_loc = _dot(q, kl, 1, 1) * scale + bias
    s_ctx = _dot(q, kc, 1, 1) * scale
    m = jnp.maximum(jnp.max(s_loc, axis=-1, keepdims=True), jnp.max(s_ctx, axis=-1, keepdims=True))
    p_loc = jnp.exp(s_loc - m)
    p_ctx = jnp.exp(s_ctx - m)
    den = jnp.sum(p_loc, axis=-1, keepdims=True) + jnp.sum(p_ctx, axis=-1, keepdims=True)
    return p_loc, p_ctx, den


def _na_fwd(u, bias, *, s_len, heads, name):
    t_len = u.shape[0]
    rows = s_len // GRID_W
    nloc = NA_WIN_ROWS * GRID_W
    scale = NA_HEAD_DIM ** -0.5
    hd = NA_HEAD_DIM

    def body(q_ref, k_ref, v_ref, b_ref, o_ref):
        kc = k_ref[s_len:t_len, :]
        vc = v_ref[s_len:t_len, :]

        def row(r, carry):
            r0 = jnp.clip(r - NA_WIN_ROWS // 2, 0, rows - NA_WIN_ROWS)
            qs = pl.multiple_of(r * GRID_W, GRID_W)
            ks = pl.multiple_of(r0 * GRID_W, GRID_W)
            q = q_ref[pl.ds(qs, GRID_W), :]
            kl = k_ref[pl.ds(ks, nloc), :]
            vl = v_ref[pl.ds(ks, nloc), :]
            p_loc, p_ctx, den = _na_row_scores(q, kl, kc, b_ref[r - r0], scale)
            o = _dot(p_loc.astype(BF16), vl, 1, 0) + _dot(p_ctx.astype(BF16), vc, 1, 0)
            o_ref[pl.ds(qs, GRID_W), :] = (o / den).astype(o_ref.dtype)
            return carry

        lax.fori_loop(0, rows, row, 0, unroll=2)
        qc = q_ref[s_len:t_len, :]
        s = _dot(qc, kc, 1, 1) * scale
        p = jnp.exp(s - jnp.max(s, axis=-1, keepdims=True))
        o = _dot(p.astype(BF16), vc, 1, 0) / jnp.sum(p, axis=-1, keepdims=True)
        o_ref[s_len:t_len, :] = o.astype(o_ref.dtype)

    col = lambda off: pl.BlockSpec((t_len, hd), functools.partial(lambda h, off: (0, off + h), off=off))
    return pl.pallas_call(
        body, name=name, grid=(heads,),
        in_specs=[col(0), col(heads), col(2 * heads),
                  pl.BlockSpec((None, NA_WIN_ROWS, GRID_W, nloc), lambda h: (h, 0, 0, 0))],
        out_specs=pl.BlockSpec((t_len, hd), lambda h: (0, h)),
        out_shape=jax.ShapeDtypeStruct((t_len, heads * hd), BF16),
        compiler_params=_params(("parallel",)),
    )(u, u, u, bias)


def _na_bwd(u, bias, o, do, *, s_len, heads, name):
    t_len = u.shape[0]
    rows = s_len // GRID_W
    nloc = NA_WIN_ROWS * GRID_W
    scale = NA_HEAD_DIM ** -0.5
    hd = NA_HEAD_DIM

    def body(q_ref, k_ref, v_ref, b_ref, o_ref, do_ref, dq_ref, dk_ref, dv_ref, db_ref, dk_acc, dv_acc):
        kc = k_ref[s_len:t_len, :]
        vc = v_ref[s_len:t_len, :]
        dk_acc[...] = jnp.zeros_like(dk_acc)
        dv_acc[...] = jnp.zeros_like(dv_acc)
        db_ref[...] = jnp.zeros_like(db_ref)

        def row(r, carry):
            r0 = jnp.clip(r - NA_WIN_ROWS // 2, 0, rows - NA_WIN_ROWS)
            dl = r - r0
            qs = pl.multiple_of(r * GRID_W, GRID_W)
            ks = pl.multiple_of(r0 * GRID_W, GRID_W)
            q = q_ref[pl.ds(qs, GRID_W), :]
            kl = k_ref[pl.ds(ks, nloc), :]
            vl = v_ref[pl.ds(ks, nloc), :]
            dout = do_ref[pl.ds(qs, GRID_W), :]
            out = o_ref[pl.ds(qs, GRID_W), :]
            p_loc, p_ctx, den = _na_row_scores(q, kl, kc, b_ref[dl], scale)
            inv = 1.0 / den
            p_loc = p_loc * inv
            p_ctx = p_ctx * inv
            dlt = jnp.sum(dout.astype(F32) * out.astype(F32), axis=-1, keepdims=True)
            ds_loc = p_loc * (_dot(dout, vl, 1, 1) - dlt)
            ds_ctx = p_ctx * (_dot(dout, vc, 1, 1) - dlt)
            db_ref[dl] += ds_loc
            ds_loc_b = ds_loc.astype(BF16)
            ds_ctx_b = ds_ctx.astype(BF16)
            dq = (_dot(ds_loc_b, kl, 1, 0) + _dot(ds_ctx_b, kc, 1, 0)) * scale
            dq_ref[pl.ds(qs, GRID_W), :] = dq.astype(dq_ref.dtype)
            dk_acc[pl.ds(ks, nloc), :] += _dot(ds_loc_b, q, 0, 0) * scale
            dv_acc[pl.ds(ks, nloc), :] += _dot(p_loc.astype(BF16), dout, 0, 0)
            dk_acc[s_len:t_len, :] += _dot(ds_ctx_b, q, 0, 0) * scale
            dv_acc[s_len:t_len, :] += _dot(p_ctx.astype(BF16), dout, 0, 0)
            return carry

        lax.fori_loop(0, rows, row, 0, unroll=2)
        qc = q_ref[s_len:t_len, :]
        dout = do_ref[s_len:t_len, :]
        out = o_ref[s_len:t_len, :]
        s = _dot(qc, kc, 1, 1) * scale
        p = jnp.exp(s - jnp.max(s, axis=-1, keepdims=True))
        p = p / jnp.sum(p, axis=-1, keepdims=True)
        dlt = jnp.sum(dout.astype(F32) * out.astype(F32), axis=-1, keepdims=True)
        ds = (p * (_dot(dout, vc, 1, 1) - dlt)).astype(BF16)
        dq_ref[s_len:t_len, :] = (_dot(ds, kc, 1, 0) * scale).astype(dq_ref.dtype)
        dk_acc[s_len:t_len, :] += _dot(ds, qc, 0, 0) * scale
        dv_acc[s_len:t_len, :] += _dot(p.astype(BF16), dout, 0, 0)
        dk_ref[...] = dk_acc[...].astype(dk_ref.dtype)
        dv_ref[...] = dv_acc[...].astype(dv_ref.dtype)

    col = lambda off: pl.BlockSpec((t_len, hd), functools.partial(lambda h, off: (0, off + h), off=off))
    tbl = pl.BlockSpec((None, NA_WIN_ROWS, GRID_W, nloc), lambda h: (h, 0, 0, 0))
    tok = jax.ShapeDtypeStruct((t_len, heads * hd), BF16)
    return pl.pallas_call(
        body, name=name, grid=(heads,),
        in_specs=[col(0), col(heads), col(2 * heads), tbl, col(0), col(0)],
        out_specs=[col(0), col(0), col(0), tbl],
        out_shape=[tok, tok, tok, jax.ShapeDtypeStruct(bias.shape, F32)],
        scratch_shapes=[pltpu.VMEM((t_len, hd), F32), pltpu.VMEM((t_len, hd), F32)],
        compiler_params=_params(("parallel",)),
    )(u, u, u, bias, o, do)


def _split3(x):
    hi = x.astype(BF16)
    r1 = x - hi.astype(F32)
    mid = r1.astype(BF16)
    lo = (r1 - mid.astype(F32)).astype(BF16)
    return hi, mid, lo


def _rpb_grad(dbias, *, name):
    heads = dbias.shape[0]
    kh = NA_WIN_ROWS
    e1, e2 = _na_onehots()
    x = dbias.reshape(heads, kh, GRID_W, kh, GRID_W).transpose(0, 1, 3, 2, 4).reshape(heads, kh * kh, GRID_W * GRID_W)

    def body(x_ref, e1_ref, e2_ref, o_ref):
        e2b = e2_ref[...].astype(BF16)
        y = sum(_dot(part, e2b, 1, 0) for part in _split3(x_ref[...]))
        e1b = e1_ref[...].astype(BF16)
        o_ref[...] = sum(_dot(e1b, part, 1, 0) for part in _split3(y))

    out = pl.pallas_call(
        body, name=name, grid=(heads,),
        in_specs=[pl.BlockSpec((None, kh * kh, GRID_W * GRID_W), lambda h: (h, 0, 0)),
                  pl.BlockSpec(e1.shape, lambda h: (0, 0)), pl.BlockSpec(e2.shape, lambda h: (0, 0))],
        out_specs=pl.BlockSpec((None, 16, LANES), lambda h: (h, 0, 0)),
        out_shape=jax.ShapeDtypeStruct((heads, 16, LANES), F32),
        compiler_params=_params(("parallel",)),
    )(x, e1, e2)
    return out[:, :2 * kh - 1, :2 * NA_WIN_COLS - 1]


def _rope_tables(s_len, l_len):
    nf = RET_KEY_DIM // 4
    t = np.arange(s_len)
    row = (t // GRID_W).astype(np.float32)
    colp = (t % GRID_W).astype(np.float32)
    inv_freq = jnp.asarray(ROPE_BASE, F32) ** (-jnp.arange(nf, dtype=F32) / nf)
    ang = jnp.concatenate([jnp.asarray(row)[:, None] * inv_freq, jnp.asarray(colp)[:, None] * inv_freq], axis=-1)
    cos, sin = jnp.cos(ang), jnp.sin(ang)
    c2 = jnp.concatenate([cos, cos], axis=-1)
    s2 = jnp.concatenate([-sin, sin], axis=-1)
    c2 = jnp.concatenate([c2, jnp.ones((l_len, RET_KEY_DIM), F32)], axis=0)
    s2 = jnp.concatenate([s2, jnp.zeros((l_len, RET_KEY_DIM), F32)], axis=0)
    return c2, s2


def _rope(x, c2, s2):
    return x * c2 + pltpu.roll(x, RET_KEY_DIM // 2, 1) * s2


def _rope_t(d, c2, s2):
    return d * c2 + pltpu.roll(d * s2, RET_KEY_DIM // 2, 1)


def _ret_decays(lg, direction):
    cs = RET_CHUNK
    i_col = lax.broadcasted_iota(jnp.int32, (cs, 1), 0)
    p_col = jnp.where(direction == 0, i_col, cs - 1 - i_col).astype(F32)
    pi = lax.broadcasted_iota(jnp.int32, (cs, cs), 0)
    pj = lax.broadcasted_iota(jnp.int32, (cs, cs), 1)
    diff = jnp.where(direction == 0, pi - pj, pj - pi).astype(F32)
    dm = jnp.where(diff >= 0, jnp.exp(jnp.maximum(diff, 0.0) * lg), 0.0)
    qdec = jnp.exp((p_col + 1.0) * lg)
    kdec = jnp.exp((cs - 1.0 - p_col) * lg)
    cd = jnp.exp(jnp.full((1, 1), cs, F32) * lg)
    return p_col, dm, qdec, kdec, cd


def _ret_chunk_index(t, direction, n_chunks, lat_chunks):
    return jnp.where(direction == 0, lax.rem(t + lat_chunks, n_chunks), n_chunks - 1 - t)


def _ret_fwd(u, c2, s2, lg, *, s_len, heads, q_off, name):
    t_len = u.shape[0]
    cs, dk, dv = RET_CHUNK, RET_KEY_DIM, RET_VAL_DIM
    n_chunks, lat_chunks = t_len // cs, s_len // cs
    k_scale = dk ** -0.5
    qb, kb, vb = q_off // dk, q_off // dk + heads, (q_off + 2 * heads * dk) // dv

    def body(lg_ref, q_ref, k_ref, v_ref, c_ref, s_ref, o_ref, st_ref, state):
        h, d = pl.program_id(0), pl.program_id(1)
        _, dm, qdec, kdec, cd = _ret_decays(lg_ref[d, h], d)
        state[...] = jnp.zeros_like(state)

        def step(t, carry):
            c = _ret_chunk_index(t, d, n_chunks, lat_chunks)
            r = pl.ds(pl.multiple_of(c * cs, cs), cs)
            cc, ss = c_ref[r, :], s_ref[r, :]
            qc = _rope(q_ref[r, :].astype(F32), cc, ss)
            kc = _rope(k_ref[r, :].astype(F32), cc, ss) * k_scale
            vc = v_ref[r, :]
            st = state[...]
            st_ref[t] = st
            a = _dot(qc.astype(BF16), kc.astype(BF16), 1, 1) * dm
            oc = _dot(a.astype(BF16), vc, 1, 0) + _dot((qc * qdec).astype(BF16), st.astype(BF16), 1, 0)
            state[...] = st * cd + _dot((kc * kdec).astype(BF16), vc, 0, 0)

            @pl.when(d == 0)
            def _():
                o_ref[r, :] = oc

            @pl.when(d == 1)
            def _():
                o_ref[r, :] += oc

            return carry

        lax.fori_loop(0, n_chunks, step, 0, unroll=2)

    return pl.pallas_call(
        body, name=name, grid=(heads, 2),
        in_specs=[pl.BlockSpec(memory_space=pltpu.SMEM),
                  pl.BlockSpec((t_len, dk), lambda h, d: (0, qb + h)),
                  pl.BlockSpec((t_len, dk), lambda h, d: (0, kb + h)),
                  pl.BlockSpec((t_len, dv), lambda h, d: (0, vb + h)),
                  pl.BlockSpec((t_len, dk), lambda h, d: (0, 0)),
                  pl.BlockSpec((t_len, dk), lambda h, d: (0, 0))],
        out_specs=[pl.BlockSpec((t_len, dv), lambda h, d: (0, h)),
                   pl.BlockSpec((None, None, n_chunks, dk, dv), lambda h, d: (h, d, 0, 0, 0))],
        out_shape=[jax.ShapeDtypeStruct((t_len, heads * dv), F32),
                   jax.ShapeDtypeStruct((heads, 2, n_chunks, dk, dv), F32)],
        scratch_shapes=[pltpu.VMEM((dk, dv), F32)],
        compiler_params=_params(("parallel", "arbitrary")),
    )(lg, u, u, u, c2, s2)


def _ret_bwd(u, c2, s2, lg, states, do, *, s_len, heads, q_off, name):
    t_len = u.shape[0]
    cs, dk, dv = RET_CHUNK, RET_KEY_DIM, RET_VAL_DIM
    n_chunks, lat_chunks = t_len // cs, s_len // cs
    k_scale = dk ** -0.5
    qb, kb, vb = q_off // dk, q_off // dk + heads, (q_off + 2 * heads * dk) // dv

    def body(lg_ref, q_ref, k_ref, v_ref, c_ref, s_ref, st_ref, do_ref, dq_ref, dk_ref, dv_ref, dlg_ref, dstate, acc):
        h, d = pl.program_id(0), pl.program_id(1)
        p_col, dm, qdec, kdec, cd = _ret_decays(lg_ref[d, h], d)
        dstate[...] = jnp.zeros_like(dstate)
        acc[...] = jnp.zeros_like(acc)

        def step(i, carry):
            t = n_chunks - 1 - i
            c = _ret_chunk_index(t, d, n_chunks, lat_chunks)
            r = pl.ds(pl.multiple_of(c * cs, cs), cs)
            cc, ss = c_ref[r, :], s_ref[r, :]
            qc = _rope(q_ref[r, :].astype(F32), cc, ss)
            kc = _rope(k_ref[r, :].astype(F32), cc, ss) * k_scale
            vc = v_ref[r, :]
            doc = do_ref[r, :].astype(BF16)
            st = st_ref[t]
            dst = dstate[...]
            qb16, kb16 = qc.astype(BF16), kc.astype(BF16)
            a = _dot(qb16, kb16, 1, 1) * dm
            dam = (_dot(doc, vc, 1, 1) * dm).astype(BF16)
            dq_i = _dot(dam, kb16, 1, 0)
            dk_i = _dot(dam, qb16, 0, 0)
            dq_c = _dot(doc, st.astype(BF16), 1, 1) * qdec
            dst16 = dst.astype(BF16)
            dvc = _dot(a.astype(BF16), doc, 0, 0) + _dot((kc * kdec).astype(BF16), dst16, 1, 0)
            dk_s = _dot(vc, dst16, 1, 1) * kdec
            g = (jnp.sum(qc * (p_col * dq_i + (p_col + 1.0) * dq_c), axis=-1, keepdims=True)
                 + jnp.sum(kc * ((cs - 1.0 - p_col) * dk_s - p_col * dk_i), axis=-1, keepdims=True))
            g = jnp.sum(g, axis=0, keepdims=True) + cs * cd * jnp.sum(jnp.sum(dst * st, axis=-1, keepdims=True), axis=0, keepdims=True)
            acc[...] += jnp.broadcast_to(g, acc.shape)
            dstate[...] = dst * cd + _dot((qc * qdec).astype(BF16), doc, 0, 0)
            dq = _rope_t(dq_i + dq_c, cc, ss)
            dkk = _rope_t((dk_i + dk_s) * k_scale, cc, ss)

            @pl.when(d == 0)
            def _():
                dq_ref[r, :] = dq.astype(dq_ref.dtype)
                dk_ref[r, :] = dkk.astype(dk_ref.dtype)
                dv_ref[r, :] = dvc.astype(dv_ref.dtype)

            @pl.when(d == 1)
            def _():
                dq_ref[r, :] = (dq_ref[r, :].astype(F32) + dq).astype(dq_ref.dtype)
                dk_ref[r, :] = (dk_ref[r, :].astype(F32) + dkk).astype(dk_ref.dtype)
                dv_ref[r, :] = (dv_ref[r, :].astype(F32) + dvc).astype(dv_ref.dtype)

            return carry

        lax.fori_loop(0, n_chunks, step, 0, unroll=2)
        dlg_ref[...] = acc[...]

    return pl.pallas_call(
        body, name=name, grid=(heads, 2),
        in_specs=[pl.BlockSpec(memory_space=pltpu.SMEM),
                  pl.BlockSpec((t_len, dk), lambda h, d: (0, qb + h)),
                  pl.BlockSpec((t_len, dk), lambda h, d: (0, kb + h)),
                  pl.BlockSpec((t_len, dv), lambda h, d: (0, vb + h)),
                  pl.BlockSpec((t_len, dk), lambda h, d: (0, 0)),
                  pl.BlockSpec((t_len, dk), lambda h, d: (0, 0)),
                  pl.BlockSpec((None, None, n_chunks, dk, dv), lambda h, d: (h, d, 0, 0, 0)),
                  pl.BlockSpec((t_len, dv), lambda h, d: (0, h))],
        out_specs=[pl.BlockSpec((t_len, dk), lambda h, d: (0, h)),
                   pl.BlockSpec((t_len, dk), lambda h, d: (0, h)),
                   pl.BlockSpec((t_len, dv), lambda h, d: (0, h)),
                   pl.BlockSpec((None, None, 8, LANES), lambda h, d: (h, d, 0, 0))],
        out_shape=[jax.ShapeDtypeStruct((t_len, heads * dk), BF16),
                   jax.ShapeDtypeStruct((t_len, heads * dk), BF16),
                   jax.ShapeDtypeStruct((t_len, heads * dv), BF16),
                   jax.ShapeDtypeStruct((heads, 2, 8, LANES), F32)],
        scratch_shapes=[pltpu.VMEM((dk, dv), F32), pltpu.VMEM((8, LANES), F32)],
        compiler_params=_params(("parallel", "arbitrary")),
    )(lg, u, u, u, c2, s2, states, do)


def _mesh_pos():
    return lax.axis_index("x"), lax.axis_index("y"), lax.axis_index("c")


def _all_gather_small(buf, *, name):
    r = buf.shape[0]

    def body(x_ref, o_ref, send_sems, recv_sems, local_sem):
        x, y, c = _mesh_pos()
        me = 4 * x + 2 * y + c
        mine = pltpu.make_async_copy(x_ref, o_ref.at[me], local_sem)
        mine.start()
        copies = []
        for k in range(1, N_DEV):
            px, py, pc = x ^ ((k >> 2) & 1), y ^ ((k >> 1) & 1), c ^ (k & 1)
            cp = pltpu.make_async_remote_copy(
                src_ref=x_ref, dst_ref=o_ref.at[me], send_sem=send_sems.at[k - 1], recv_sem=recv_sems.at[k - 1],
                device_id=(px, py, pc), device_id_type=MESH)
            cp.start()
            copies.append((cp, 4 * px + 2 * py + pc))
        for k, (cp, peer) in enumerate(copies):
            pltpu.make_async_remote_copy(
                src_ref=x_ref, dst_ref=o_ref.at[peer], send_sem=send_sems.at[k], recv_sem=recv_sems.at[k],
                device_id=(x, y, c), device_id_type=MESH).wait_recv()
        for cp, _ in copies:
            cp.wait_send()
        mine.wait()

    return pl.pallas_call(
        body, name=name,
        in_specs=[pl.BlockSpec(memory_space=pltpu.VMEM)],
        out_specs=pl.BlockSpec(memory_space=pltpu.VMEM),
        out_shape=jax.ShapeDtypeStruct((N_DEV, r, LANES), F32),
        scratch_shapes=[pltpu.SemaphoreType.DMA((N_DEV - 1,)), pltpu.SemaphoreType.DMA((N_DEV - 1,)),
                        pltpu.SemaphoreType.DMA],
        compiler_params=pltpu.CompilerParams(vmem_limit_bytes=VMEM_LIMIT),
    )(buf)


def _cut(ref, shard_axis, *, chip=None, half=None, lead=None):
    shape = ref.shape[1:] if lead is not None else ref.shape
    idx = [slice(None), slice(None)]
    if chip is not None:
        w = shape[shard_axis] // N_CHIPS
        idx[shard_axis] = pl.ds(pl.multiple_of(chip * w, w), w)
    if half is not None:
        hw = shape[1 - shard_axis] // 2
        idx[1 - shard_axis] = pl.ds(pl.multiple_of(half * hw, hw), hw)
    if lead is not None:
        idx = [lead] + idx
    return ref.at[tuple(idx)]


def _wait_recv(ref, send_sem, recv_sem):
    pltpu.make_async_remote_copy(src_ref=ref, dst_ref=ref, send_sem=send_sem, recv_sem=recv_sem,
                                 device_id=_mesh_pos(), device_id_type=MESH).wait_recv()


def _gather_plan(axes):
    def plan(srcs, lands, send_sems, recv_sems):
        x, y, c = _mesh_pos()
        chip = 2 * x + y
        copies = []
        for i, ax in enumerate(axes):
            for k in range(1, N_CHIPS):
                px, py = x ^ (k >> 1), y ^ (k & 1)
                mine = _cut(lands[i], ax, chip=chip, half=c)
                j = i * (N_CHIPS - 1) + k - 1
                sems = dict(send_sem=send_sems.at[j], recv_sem=recv_sems.at[j], device_id=(px, py, c), device_id_type=MESH)
                send = pltpu.make_async_remote_copy(src_ref=mine, dst_ref=mine, **sems)
                recv = pltpu.make_async_remote_copy(src_ref=mine, dst_ref=_cut(lands[i], ax, chip=2 * px + py, half=c), **sems)
                copies.append((send, recv))
        return copies
    return plan


def _scatter_plan(axes):
    def plan(srcs, lands, send_sems, recv_sems):
        x, y, c = _mesh_pos()
        copies = []
        for i, ax in enumerate(axes):
            for k in range(1, N_CHIPS):
                px, py = x ^ (k >> 1), y ^ (k & 1)
                j = i * (N_CHIPS - 1) + k - 1
                cp = pltpu.make_async_remote_copy(
                    src_ref=_cut(srcs[i], ax, chip=2 * px + py), dst_ref=lands[i].at[k - 1],
                    send_sem=send_sems.at[j], recv_sem=recv_sems.at[j], device_id=(px, py, c), device_id_type=MESH)
                copies.append((cp, cp))
        return copies
    return plan


HBM = pl.BlockSpec(memory_space=pltpu.HBM)
SEM = pl.BlockSpec(memory_space=pltpu.SEMAPHORE)
EFFECT = pltpu.SideEffectType.DATAFLOW_SIDE_EFFECTING


def _in_hbm(arrays):
    return [pltpu.with_memory_space_constraint(a, pltpu.HBM) for a in arrays]


def _split_start(srcs, lands, plan, n_copies, *, name):
    bufs = list(srcs) + list(lands)
    ns, nb = len(srcs), len(bufs)

    def body(*refs):
        send_sems, recv_sems, token = refs[nb], refs[nb + 1], refs[-1]
        for send, _ in plan(refs[:ns], refs[ns:nb], send_sems, recv_sems):
            send.start()
        token[...] = jnp.zeros_like(token)

    sems = pltpu.SemaphoreType.DMA((n_copies,))
    res = pl.pallas_call(
        body, name=name, in_specs=[HBM] * nb,
        out_specs=[SEM, SEM] + [HBM] * nb + [pl.BlockSpec(memory_space=pltpu.VMEM)],
        out_shape=[sems, sems] + [pltpu.HBM(a.shape, a.dtype) for a in bufs] + [jax.ShapeDtypeStruct((8, LANES), F32)],
        input_output_aliases={j: 2 + j for j in range(nb)},
        compiler_params=pltpu.CompilerParams(has_side_effects=EFFECT),
    )(*_in_hbm(bufs))
    return res[0], res[1], res[2:2 + ns], res[2 + ns:2 + nb], res[-1]


def _split_wait(started, after, plan, *, name):
    send_sems, recv_sems, srcs, lands, _ = started
    bufs = list(srcs) + list(lands)
    ns, nb = len(srcs), len(bufs)

    def body(*refs):
        for send, recv in plan(refs[:ns], refs[ns:nb], refs[nb], refs[nb + 1]):
            send.wait_send()
            recv.wait_recv()

    res = pl.pallas_call(
        body, name=name, in_specs=[HBM] * nb + [SEM, SEM, ANY], out_specs=[HBM] * nb,
        out_shape=[pltpu.HBM(a.shape, a.dtype) for a in bufs],
        input_output_aliases={j: j for j in range(nb)},
        compiler_params=pltpu.CompilerParams(has_side_effects=EFFECT),
    )(*bufs, send_sems, recv_sems, after)
    return res[ns:]


def _cast_into_full(w3, layer, ax, chip, *, name):
    _, r, wd = w3.shape
    tr = _rows_per_tile(r, wd, 4 << 20)
    nt = r // tr
    full_shape = (r, wd * N_CHIPS) if ax == 1 else (r * N_CHIPS, wd)
    out_map = (lambda i, ch: (i, ch[0])) if ax == 1 else (lambda i, ch: (ch[0] * nt + i, 0))

    def body(chip_ref, w_ref, o_ref):
        o_ref[...] = w_ref[...].astype(o_ref.dtype)

    return pl.pallas_call(
        body, name=name,
        grid_spec=pltpu.PrefetchScalarGridSpec(
            num_scalar_prefetch=1, grid=(nt,),
            in_specs=[pl.BlockSpec((None, tr, wd), lambda i, ch: (layer, i, 0))],
            out_specs=pl.BlockSpec((tr, wd), out_map)),
        out_shape=jax.ShapeDtypeStruct(full_shape, BF16),
        compiler_params=_params(("parallel",)),
    )(jnp.reshape(chip, (1,)).astype(jnp.int32), w3)


def _forward_halves(fulls, axes, *, name):
    n = len(fulls)

    def body(*refs):
        bufs = refs[:n]
        send_sems, recv_sems = refs[2 * n:]
        x, y, c = _mesh_pos()
        sends = []
        for i in range(n):
            for k in range(1, N_CHIPS):
                landed = _cut(bufs[i], axes[i], chip=2 * (x ^ (k >> 1)) + (y ^ (k & 1)), half=c)
                cp = pltpu.make_async_remote_copy(
                    src_ref=landed, dst_ref=landed, send_sem=send_sems.at[i, k - 1], recv_sem=recv_sems.at[i, k - 1],
                    device_id=(x, y, 1 - c), device_id_type=MESH)
                cp.start()
                sends.append(cp)
        for i in range(n):
            for k in range(1, N_CHIPS):
                other = _cut(bufs[i], axes[i], chip=2 * (x ^ (k >> 1)) + (y ^ (k & 1)), half=1 - c)
                _wait_recv(other, send_sems.at[i, k - 1], recv_sems.at[i, k - 1])
        for cp in sends:
            cp.wait_send()

    pairs = pltpu.SemaphoreType.DMA((n, N_CHIPS - 1))
    return pl.pallas_call(
        body, name=name, in_specs=[ANY] * n, out_specs=[ANY] * n,
        out_shape=[jax.ShapeDtypeStruct(a.shape, a.dtype) for a in fulls],
        input_output_aliases={j: j for j in range(n)},
        scratch_shapes=[pairs, pairs],
    )(*fulls)


def _send_to_sibling(parts, *, name):
    n = len(parts)

    def body(*refs):
        ins, outs = refs[:n], refs[n:2 * n]
        send_sems, recv_sems = refs[2 * n:]
        x, y, c = _mesh_pos()
        sends = []
        for i in range(n):
            cp = pltpu.make_async_remote_copy(
                src_ref=ins[i], dst_ref=outs[i], send_sem=send_sems.at[i], recv_sem=recv_sems.at[i],
                device_id=(x, y, 1 - c), device_id_type=MESH)
            cp.start()
            sends.append(cp)
        for cp in sends:
            cp.wait()

    sems = pltpu.SemaphoreType.DMA((n,))
    return pl.pallas_call(
        body, name=name, in_specs=[ANY] * n, out_specs=[ANY] * n,
        out_shape=[jax.ShapeDtypeStruct(p.shape, p.dtype) for p in parts],
        scratch_shapes=[sems, sems],
    )(*parts)


def _adamw_math(w, g, m, v):
    m = ADAM_B1 * m + (1.0 - ADAM_B1) * g
    v = ADAM_B2 * v + (1.0 - ADAM_B2) * (g * g)
    m_hat = m / (1.0 - ADAM_B1 ** ADAM_STEP)
    v_hat = v / (1.0 - ADAM_B2 ** ADAM_STEP)
    delta = -ADAM_LR * (m_hat / (jnp.sqrt(v_hat) + ADAM_EPS) + ADAM_WD * w)
    return delta, m, v


def _adamw_layer(w3, m3, v3, p, q, layer, prev, *, name):
    nl, rows, width = w3.shape
    tr = _rows_per_tile(rows, width)

    def fn(*t):
        if q is None:
            w, m, v, g = t
        else:
            w, m, v, g, g2 = t
            g = g + g2
        delta, m, v = _adamw_math(w, g, m, v)
        return g, delta, m, v

    ins = [('t', w3, 0, width, layer), ('t', m3, 0, width, layer), ('t', v3, 0, width, layer), ('t', p, 0, width)]
    if q is not None:
        ins.append(('t', q, 0, width))
    outs = [('t', width, F32, layer, nl)] * 4
    aliases = None if prev is None else [(prev[i], i) for i in range(4)]
    return _ew(fn, ins, outs, rows=rows, tr=tr, name=name, aliases=aliases)


def _pack_rows(vec):
    n = vec.shape[0]
    r = -(-n // (8 * LANES)) * 8
    return jnp.pad(vec, (0, r * LANES - n)).reshape(r, LANES)


def kernel(x, c, ctx, c_ctx, ada_w, ada_b, norm_g, w_in, na_rpb, ret_decay_logit, w_proj_na, w_proj_ret, w_out, final_g, loss_target, m_c_ctx, m_ada_w, m_ada_b, m_norm_g, m_w_in, m_na_rpb, m_ret_decay_logit, m_w_proj_na, m_w_proj_ret, m_w_out, m_final_g, v_c_ctx, v_ada_w, v_ada_b, v_norm_g, v_w_in, v_na_rpb, v_ret_decay_logit, v_w_proj_na, v_w_proj_ret, v_w_out, v_final_g):
    depth = w_in.shape[0]
    s_len, d_model = x.shape[1], x.shape[2]
    l_len = ctx.shape[1]
    t_len = s_len + l_len
    na_heads = na_rpb.shape[1]
    ret_heads = ret_decay_logit.shape[2]
    w_na = na_heads * NA_HEAD_DIM
    w_qk = ret_heads * RET_KEY_DIM
    w_v = ret_heads * RET_VAL_DIM
    in_cols = w_in.shape[2] * N_CHIPS
    assert in_cols == 4 * w_na + 2 * w_qk + 2 * w_v + 2 * d_model
    assert x.shape[0] == 1 and s_len % (NA_WIN_ROWS * GRID_W) == 0 and l_len % RET_CHUNK == 0
    off = np.cumsum([0, w_na, w_na, w_na, w_na, w_qk, w_qk, w_v, w_v, d_model, d_model])
    o_naz, o_retq, o_retz, o_gna, o_gret = int(off[3]), int(off[4]), int(off[7]), int(off[8]), int(off[9])
    rows = s_len // GRID_W
    tr = _tile(l_len, 256, 8)
    n0 = s_len // tr
    mod_cols = 3 * d_model
    mod_shard = ada_w.shape[2]

    xi, yi, ci = _mesh_pos()
    me = 4 * xi + 2 * yi + ci
    chip = 2 * xi + yi

    big_axes = [1, 1, 0, 0]
    n_big = len(big_axes) * (N_CHIPS - 1)
    gather_plan, scatter_plan = _gather_plan(big_axes), _scatter_plan(big_axes)

    c_silu = c[0] * _sigmoid(c[0])
    cc_silu = c_ctx * _sigmoid(c_ctx)
    c_all = _all_gather_small(_pack_rows(c_silu), name="gather_c")[:, :d_model // LANES].reshape(N_DEV, d_model)
    a_rows = jnp.concatenate([c_all, cc_silu[None], jnp.zeros((16 - N_DEV - 1, d_model), F32)], axis=0)
    mod_part = jnp.stack([_mm(a_rows, ada_w, b_lead=l, out_dtype=F32, name="ada_fwd_%d" % l) for l in range(depth)])
    mod_all = _all_gather_small(_pack_rows(mod_part.reshape(-1)), name="gather_mod")
    n_mod = depth * 16 * mod_shard
    mod_all = mod_all.reshape(N_DEV, -1)[:, :n_mod].reshape(N_CHIPS, 2, depth, 16, mod_shard)[:, 0]
    mod_all = jnp.transpose(mod_all, (1, 2, 0, 3)).reshape(depth, 16, mod_cols) + ada_b[:, None, :]

    fulls = [[_cast_into_full(w, l, ax, chip, name="cast_%s_%d" % (tag, l))
              for w, ax, tag in zip((w_in, w_proj_na, w_proj_ret, w_out), big_axes, ("w_in", "w_proj_na", "w_proj_ret", "w_out"))]
             for l in range(depth)]
    mod_all, fulls = lax.optimization_barrier((mod_all, fulls))
    plan_in, plan_rest = _gather_plan(big_axes[:1]), _gather_plan(big_axes[1:])
    first_gather = _split_start([], fulls[0][:1], plan_in, N_CHIPS - 1, name="gather_start_0_in")
    start_token = first_gather[4][0, 0]
    mod_lat = lax.dynamic_index_in_dim(mod_all, me, axis=1, keepdims=False)
    mod_ctx = mod_all[:, N_DEV]

    c2, s2 = _rope_tables(s_len, l_len)
    log_gamma = jax.nn.log_sigmoid(ret_decay_logit)
    x_all = jnp.concatenate([x[0], ctx[0]], axis=0)

    def grp(lat_vec, ctx_vec):
        return jnp.stack([lat_vec, ctx_vec])[:, None, :]

    saved, full_w = [], []
    for l in range(depth):
        shift, scale, gate = [grp(mod_lat[l, i * d_model:(i + 1) * d_model], mod_ctx[l, i * d_model:(i + 1) * d_model])
                              for i in range(3)]
        gs = norm_g[l][None, None, :] * (1.0 + scale) + start_token

        def modnorm(xt, gs_t, sh_t):
            r = lax.rsqrt(jnp.mean(xt * xt, axis=-1, keepdims=True) + NORM_EPS)
            return xt * r * gs_t + sh_t

        h, = _ew(modnorm, [('t', x_all, 0, d_model), ('g', gs), ('g', shift)], [('t', d_model, BF16)],
                 rows=t_len, tr=tr, n0=n0, name="modnorm_%d" % l)
        bias = _na_bias_table(na_rpb[l], rows, name="na_bias_%d" % l)
        h, bias = lax.optimization_barrier((h, bias))
        if l == 0:
            landed_in = _split_wait(first_gather, h, plan_in, name="gather_wait_0_in")
            landed_in, rest0, later = lax.optimization_barrier((landed_in, fulls[0][1:], fulls[1:]))
            rest_gather = _split_start([], rest0, plan_rest, n_big - (N_CHIPS - 1), name="gather_start_0_rest")
            later_gathers = [_split_start([], later[j], gather_plan, n_big, name="gather_start_%d" % (j + 1)) for j in range(depth - 1)]
            win_f, = _forward_halves(landed_in, big_axes[:1], name="gather_forward_0_in")
            win_f, _ = lax.optimization_barrier((win_f, [rest_gather[4]] + [g[4] for g in later_gathers]))
        else:
            landed = _split_wait(later_gathers[l - 1], h, gather_plan, name="gather_wait_%d" % l)
            win_f, wpn_f, wpr_f, wout_f = _forward_halves(landed, big_axes, name="gather_forward_%d" % l)
        u = _mm(h, win_f, name="in_proj_%d" % l)
        o_na = _na_fwd(u, bias, s_len=s_len, heads=na_heads, name="na_fwd_%d" % l)
        o_ret, states = _ret_fwd(u, c2, s2, log_gamma[l], s_len=s_len, heads=ret_heads, q_off=o_retq, name="ret_fwd_%d" % l)

        def act(o1, z1, o2, z2):
            a1 = o1.astype(F32) * _silu_parts(z1.astype(F32))[0]
            sz = _silu_parts(z2.astype(F32))[0]
            outs = []
            for hh in range(ret_heads):
                sl = slice(hh * RET_VAL_DIM, (hh + 1) * RET_VAL_DIM)
                oh = o2[:, sl]
                r = lax.rsqrt(jnp.mean(oh * oh, axis=-1, keepdims=True) + NORM_EPS)
                outs.append(oh * r * sz[:, sl])
            return a1, jnp.concatenate(outs, axis=-1)

        a_na, a_ret = _ew(act, [('t', o_na, 0, w_na), ('t', u, o_naz // w_na, w_na), ('t', o_ret, 0, w_v), ('t', u, o_retz // w_v, w_v)],
                          [('t', w_na, BF16), ('t', w_v, BF16)], rows=t_len, tr=tr, name="act_%d" % l)
        if l == 0:
            landed_rest = _split_wait(rest_gather, a_na, plan_rest, name="gather_wait_0_rest")
            wpn_f, wpr_f, wout_f = _forward_halves(landed_rest, big_axes[1:], name="gather_forward_0_rest")
        full_w.append((win_f, wpn_f, wpr_f, wout_f))
        y_na = _mm(a_na, wpn_f, name="proj_na_%d" % l)
        y_ret = _mm(a_ret, wpr_f, name="proj_ret_%d" % l)

        def merge(y1, y2, g1, g2):
            return _sigmoid(g1.astype(F32)) * y1.astype(F32) + _sigmoid(g2.astype(F32)) * y2.astype(F32)

        merged, = _ew(merge, [('t', y_na, 0, d_model), ('t', y_ret, 0, d_model), ('t', u, o_gna // d_model, d_model), ('t', u, o_gret // d_model, d_model)],
                      [('t', d_model, BF16)], rows=t_len, tr=tr, name="merge_%d" % l)
        out = _mm(merged, wout_f, out_dtype=F32, name="out_proj_%d" % l)
        x_new, = _ew(lambda xt, ot, gt: xt + gt * ot, [('t', x_all, 0, d_model), ('t', out, 0, d_model), ('g', gate)],
                     [('t', d_model, F32)], rows=t_len, tr=tr, n0=n0, name="resid_%d" % l)
        saved.append(dict(x=x_all, h=h, u=u, bias=bias, o_na=o_na, o_ret=o_ret, states=states, a_na=a_na, a_ret=a_ret,
                          y_na=y_na, y_ret=y_ret, merged=merged, out=out, gate=gate, gs=gs, scale=scale))
        x_all = x_new

    def final(xt, tt, gt):
        r = lax.rsqrt(jnp.mean(xt * xt, axis=-1, keepdims=True) + NORM_EPS)
        xh = xt * r
        e = xh * gt - tt
        dy = e * (1.0 / d_model)
        dyg = dy * gt
        dx = r * (dyg - xh * jnp.mean(dyg * xh, axis=-1, keepdims=True))
        return dx, _rsum(dy * xh), _rsum(e * e)

    dx_lat, d_final_g, loss_cols = _ew(final, [('t', x_all, 0, d_model), ('t', loss_target[0], 0, d_model), ('g', final_g[None, None, :])],
                                       [('t', d_model, F32), ('r', d_model, 1), ('r', d_model, 1)], rows=s_len, tr=tr, name="final")
    loss_part = (0.5 / d_model) * jnp.sum(loss_cols)
    dx_all = jnp.concatenate([dx_lat, jnp.zeros((l_len, d_model), F32)], axis=0)

    big_w = [(w_in, m_w_in, v_w_in), (w_proj_na, m_w_proj_na, v_w_proj_na), (w_proj_ret, m_w_proj_ret, v_w_proj_ret), (w_out, m_w_out, v_w_out)]
    big_res = [None] * 4
    scatters = [None] * depth
    back_token = jnp.zeros((), F32)

    def start_scatter(l, h, du, g_wpn, g_wpr, g_wout):
        g_win = _mm(h, du, ta=True, tm=512, tk=t_len, name="in_proj_dw_%d" % l)
        grads_l = [g_win, g_wpn, g_wpr, g_wout]
        half_sz = [g.shape[1 - ax] // 2 for g, ax in zip(grads_l, big_axes)]
        mine = [lax.dynamic_slice_in_dim(g, ci * hs, hs, axis=1 - ax) for g, hs, ax in zip(grads_l, half_sz, big_axes)]
        to_send = [lax.dynamic_slice_in_dim(g, (1 - ci) * hs, hs, axis=1 - ax) for g, hs, ax in zip(grads_l, half_sz, big_axes)]
        theirs = _send_to_sibling(to_send, name="pair_exchange_%d" % l)
        pair = []
        for i in range(4):
            pr, pw = mine[i].shape
            s, = _ew(lambda a, b: a.astype(F32) + b.astype(F32), [('t', mine[i], 0, pw), ('t', theirs[i], 0, pw)], [('t', pw, BF16)],
                     rows=pr, tr=_rows_per_tile(pr, pw), name="sum_pair_%d_%d" % (i, l))
            pair.append(s)
        own = [lax.dynamic_slice_in_dim(s, chip * (s.shape[ax] // N_CHIPS), s.shape[ax] // N_CHIPS, axis=ax) for s, ax in zip(pair, big_axes)]
        lands = [lax.empty((N_CHIPS - 1,) + o.shape, BF16) for o in own]
        scatters[l] = (_split_start(pair, lands, scatter_plan, n_big, name="scatter_start_%d" % l), own)
        return scatters[l][0][4]
    small = dict(dmod_lat=[None] * depth, dmod_ctx=[None] * depth, dnorm_g=[None] * depth, drpb=[None] * depth, ddecay=[None] * depth)
    for l in reversed(range(depth)):
        sv = saved[l]
        win_f, wpn_f, wpr_f, wout_f = full_w[l]

        def resid_bwd(dxt, ot, gt):
            return gt * dxt, _rsum(dxt * ot)

        dout, dgate = _ew(resid_bwd, [('t', dx_all, 0, d_model), ('t', sv['out'], 0, d_model), ('g', sv['gate'] + back_token)],
                          [('t', d_model, BF16), ('r', d_model, 2)], rows=t_len, tr=tr, n0=n0, name="resid_bwd_%d" % l)
        dmerged = _mm(dout, wout_f, tb=True, name="out_proj_dx_%d" % l)
        g_wout = _mm(sv['merged'], dout, ta=True, tm=512, tk=t_len, name="out_proj_dw_%d" % l)

        def merge_bwd(dm, y1, y2, g1, g2):
            dm = dm.astype(F32)
            s1, s2_ = _sigmoid(g1.astype(F32)), _sigmoid(g2.astype(F32))
            return dm * s1, dm * s2_, dm * y1.astype(F32) * s1 * (1.0 - s1), dm * y2.astype(F32) * s2_ * (1.0 - s2_)

        u = sv['u']
        dy_na, dy_ret, dg_na, dg_ret = _ew(
            merge_bwd, [('t', dmerged, 0, d_model), ('t', sv['y_na'], 0, d_model), ('t', sv['y_ret'], 0, d_model),
                        ('t', u, o_gna // d_model, d_model), ('t', u, o_gret // d_model, d_model)],
            [('t', d_model, BF16)] * 4, rows=t_len, tr=tr, name="merge_bwd_%d" % l)
        da_na = _mm(dy_na, wpn_f, tb=True, name="proj_na_dx_%d" % l)
        g_wpn = _mm(sv['a_na'], dy_na, ta=True, tm=512, tk=t_len, name="proj_na_dw_%d" % l)
        da_ret = _mm(dy_ret, wpr_f, tb=True, name="proj_ret_dx_%d" % l)
        g_wpr = _mm(sv['a_ret'], dy_ret, ta=True, tm=512, tk=t_len, name="proj_ret_dw_%d" % l)

        def act_bwd(da1, o1, z1, da2, o2, z2):
            da1, da2 = da1.astype(F32), da2.astype(F32)
            si1, ds1 = _silu_parts(z1.astype(F32))
            si2, ds2 = _silu_parts(z2.astype(F32))
            do1 = da1 * si1
            dz1 = da1 * o1.astype(F32) * ds1
            dn = da2 * si2
            do2, dz2 = [], []
            for hh in range(ret_heads):
                sl = slice(hh * RET_VAL_DIM, (hh + 1) * RET_VAL_DIM)
                oh = o2[:, sl]
                r = lax.rsqrt(jnp.mean(oh * oh, axis=-1, keepdims=True) + NORM_EPS)
                nh = oh * r
                dz2.append(da2[:, sl] * nh * ds2[:, sl])
                do2.append(r * (dn[:, sl] - nh * jnp.mean(dn[:, sl] * nh, axis=-1, keepdims=True)))
            return do1, dz1, jnp.concatenate(do2, axis=-1), jnp.concatenate(dz2, axis=-1)

        do_na, dz_na, do_ret, dz_ret = _ew(
            act_bwd, [('t', da_na, 0, w_na), ('t', sv['o_na'], 0, w_na), ('t', u, o_naz // w_na, w_na),
                      ('t', da_ret, 0, w_v), ('t', sv['o_ret'], 0, w_v), ('t', u, o_retz // w_v, w_v)],
            [('t', w_na, BF16), ('t', w_na, BF16), ('t', w_v, BF16), ('t', w_v, BF16)], rows=t_len, tr=tr, name="act_bwd_%d" % l)
        dq_na, dk_na, dv_na, dbias = _na_bwd(u, sv['bias'], sv['o_na'], do_na, s_len=s_len, heads=na_heads, name="na_bwd_%d" % l)
        small['drpb'][l] = _rpb_grad(dbias, name="rpb_grad_%d" % l)
        dq_r, dk_r, dv_r, dlg = _ret_bwd(u, c2, s2, log_gamma[l], sv['states'], do_ret, s_len=s_len, heads=ret_heads,
                                         q_off=o_retq, name="ret_bwd_%d" % l)
        small['ddecay'][l] = jnp.transpose(dlg[:, :, 0, 0]) * _sigmoid(-ret_decay_logit[l])
        du = jnp.concatenate([dq_na, dk_na, dv_na, dz_na, dq_r, dk_r, dv_r, dz_ret, dg_na, dg_ret], axis=1)
        dh = _mm(du, win_f, tb=True, out_dtype=F32, tn=1024, name="in_proj_dx_%d" % l)

        def modnorm_bwd(xt, dht, dxt, gs_t):
            r = lax.rsqrt(jnp.mean(xt * xt, axis=-1, keepdims=True) + NORM_EPS)
            xh = xt * r
            dhg = dht * gs_t
            dx = r * (dhg - xh * jnp.mean(dhg * xh, axis=-1, keepdims=True)) + dxt
            return dx, _rsum(dht), _rsum(dht * xh)

        dx_all, dshift, dgs = _ew(modnorm_bwd, [('t', sv['x'], 0, d_model), ('t', dh, 0, d_model), ('t', dx_all, 0, d_model), ('g', sv['gs'])],
                                  [('t', d_model, F32), ('r', d_model, 2), ('r', d_model, 2)], rows=t_len, tr=tr, n0=n0, name="modnorm_bwd_%d" % l)
        dscale = dgs * norm_g[l][None, None, :]
        small['dnorm_g'][l] = jnp.sum(dgs * (1.0 + sv['scale']), axis=(0, 1))
        dmod = jnp.concatenate([dshift, dscale, dgate], axis=-1)[:, 0]
        small['dmod_lat'][l], small['dmod_ctx'][l] = dmod[0], dmod[1]

        pending = (sv['h'], du, g_wpn, g_wpr, g_wout)
        if l > 0:
            back_token = start_scatter(l, *pending)[0, 0]

    def finish_layer(l, after, big_res):
        started, own = scatters[l]
        recv = _split_wait(started, after, scatter_plan, name="scatter_wait_%d" % l)
        parts = []
        for i, rbuf in enumerate(recv):
            pr, pw = own[i].shape
            p, = _ew(lambda a, b, c_, d: ((a.astype(F32) + b.astype(F32)) + c_.astype(F32)) + d.astype(F32),
                     [('t', own[i], 0, pw)] + [('t', rbuf, 0, pw, k) for k in range(N_CHIPS - 1)], [('t', pw, F32)],
                     rows=pr, tr=_rows_per_tile(pr, pw), name="sum_chips_%d_%d" % (i, l))
            parts.append(p)
        others = _send_to_sibling(parts, name="share_halves_%d" % l)
        shard_g = [jnp.where(ci == 0, jnp.concatenate([p, o], axis=1 - ax), jnp.concatenate([o, p], axis=1 - ax))
                   for p, o, ax in zip(parts, others, big_axes)]
        for i in range(4):
            w3, m3, v3 = big_w[i]
            big_res[i] = _adamw_layer(w3, m3, v3, shard_g[i], None, l, big_res[i], name="adamw_big_%d_%d" % (i, l))
        return big_res

    grad_x = dx_all[:s_len][None]

    drpb = jnp.stack(small['drpb']).reshape(-1)
    ddecay = jnp.stack(small['ddecay']).reshape(-1)
    pieces = [jnp.stack(small['dmod_lat']).reshape(-1), jnp.stack(small['dmod_ctx']).reshape(-1),
              jnp.stack(small['dnorm_g']).reshape(-1), d_final_g.reshape(-1), drpb, ddecay, loss_part[None]]
    sizes = [int(p.shape[0]) for p in pieces]
    pads = [-(-s // LANES) * LANES for s in sizes]
    packed = jnp.concatenate([jnp.pad(p, (0, pd - s)) for p, s, pd in zip(pieces, sizes, pads)])
    gathered = _all_gather_small(_pack_rows(packed), name="gather_small_grads")
    r_small = gathered.shape[1]

    def sum8(*t):
        acc = t[0]
        for other in t[1:]:
            acc = acc + other
        return acc

    total, = _ew(sum8, [('t', gathered, 0, LANES, k) for k in range(N_DEV)], [('t', LANES, F32)], rows=r_small, tr=r_small, name="sum_devices")
    total = total.reshape(-1)
    starts = np.cumsum([0] + pads)
    g_mod_lat_sum, g_mod_ctx, g_norm_g, g_final_g, g_rpb, g_decay, loss = [total[starts[i]:starts[i] + sizes[i]] for i in range(len(pieces))]
    loss = loss[0]
    g_ada_b = (g_mod_lat_sum + g_mod_ctx).reshape(depth, mod_cols)
    g_mod_ctx = g_mod_ctx.reshape(depth, mod_cols)
    dmod_lat_all = gathered.reshape(N_DEV, -1)[:, :depth * mod_cols].reshape(N_DEV, depth, mod_cols)

    dcc_part = jnp.zeros((16, d_model), F32)
    ctx_cols = [lax.dynamic_slice_in_dim(g_mod_ctx[l], chip * mod_shard, mod_shard, axis=0) for l in range(depth)]
    for l in reversed(range(depth)):
        c_rows = jnp.concatenate([ctx_cols[l][None], jnp.zeros((15, mod_shard), F32)], axis=0)
        dcc_part = dcc_part + _mm(c_rows, ada_w, tb=True, b_lead=l, out_dtype=F32, name="ada_dc_%d" % l)
    dcc_all = _all_gather_small(_pack_rows(dcc_part[0]), name="gather_dcc")[:, :d_model // LANES].reshape(N_CHIPS, 2, d_model)[:, 0]
    dcc = ((dcc_all[0] + dcc_all[1]) + dcc_all[2]) + dcc_all[3]
    sg = _sigmoid(c_ctx)
    g_c_ctx = dcc * (sg * (1.0 + c_ctx * (1.0 - sg)))

    h0, du0, *rest0 = pending
    du0, dcc_all = lax.optimization_barrier((du0, dcc_all))
    tail_token = start_scatter(0, h0, du0, *rest0)
    for l in reversed(range(1, depth)):
        big_res = finish_layer(l, tail_token, big_res)

    ada_res = None
    for l in reversed(range(depth)):
        lat_cols = lax.dynamic_slice_in_dim(dmod_lat_all[:, l], chip * mod_shard, mod_shard, axis=1)
        d_rows = jnp.concatenate([lat_cols, ctx_cols[l][None], jnp.zeros((16 - N_DEV - 1, mod_shard), F32)], axis=0) + tail_token[0, 0]
        g_ada = _mm(a_rows, d_rows, ta=True, out_dtype=F32, tm=512, name="ada_dw_%d" % l)
        ada_res = _adamw_layer(ada_w, m_ada_w, v_ada_w, g_ada, None, l, ada_res, name="adamw_ada_%d" % l)

    small_w = [(c_ctx, m_c_ctx, v_c_ctx, g_c_ctx), (ada_b, m_ada_b, v_ada_b, g_ada_b),
               (norm_g, m_norm_g, v_norm_g, g_norm_g), (na_rpb, m_na_rpb, v_na_rpb, g_rpb),
               (ret_decay_logit, m_ret_decay_logit, v_ret_decay_logit, g_decay), (final_g, m_final_g, v_final_g, g_final_g)]
    sw_sizes = [int(np.prod(t[0].shape)) for t in small_w]
    sw_pads = [-(-s // LANES) * LANES for s in sw_sizes]

    def pack(j):
        return _pack_rows(jnp.concatenate([jnp.pad(t[j].reshape(-1), (0, pd - s)) for t, s, pd in zip(small_w, sw_sizes, sw_pads)]))

    pw_, pm_, pv_, pg_ = pack(0), pack(1), pack(2), pack(3)
    sw_out = _ew(lambda w, m, v, g: (g,) + _adamw_math(w, g, m, v),
                 [('t', pw_, 0, LANES), ('t', pm_, 0, LANES), ('t', pv_, 0, LANES), ('t', pg_, 0, LANES)],
                 [('t', LANES, F32)] * 4, rows=pw_.shape[0], tr=pw_.shape[0], name="adamw_small")
    sw_starts = np.cumsum([0] + sw_pads)
    after_tail = lax.optimization_barrier((sw_out[0], ada_res, big_res))[0]
    big_res = finish_layer(0, after_tail, big_res)

    def unpack(arr, i):
        return arr.reshape(-1)[sw_starts[i]:sw_starts[i] + sw_sizes[i]].reshape(small_w[i][0].shape)

    sm = [[unpack(sw_out[j], i) for i in range(len(small_w))] for j in range(4)]
    def ordered(j):
        return [sm[j][0], ada_res[j], sm[j][1], sm[j][2], big_res[0][j], sm[j][3], sm[j][4],
                big_res[1][j], big_res[2][j], big_res[3][j], sm[j][5]]

    return (loss, grad_x, *ordered(0), *ordered(1), *ordered(2), *ordered(3))
```

```python
import functools
import math

import numpy as np
import jax
import jax.numpy as jnp
from jax import lax
from jax.experimental import pallas as pl
from jax.experimental.pallas import tpu as pltpu

GRID_W = 64
NA_HEAD_DIM = 128
NA_WIN_ROWS = 8
NA_WIN_COLS = 16
RET_KEY_DIM = 128
RET_VAL_DIM = 256
RET_CHUNK = 128
ROPE_BASE = 10000.0
NORM_EPS = 1e-6
MASK_VALUE = -1e30
ADAM_LR = 0.001
ADAM_B1 = 0.9
ADAM_B2 = 0.999
ADAM_EPS = 1e-08
ADAM_WD = 0.01
ADAM_STEP = 10

N_CHIPS = 4
N_DEV = 8
LANES = 128
VMEM_LIMIT = 56 * 1024 * 1024
BF16 = jnp.bfloat16
F32 = jnp.float32
MESH = pl.DeviceIdType.MESH
ANY = pl.BlockSpec(memory_space=pl.ANY)


def _tile(dim, pref, align=LANES):
    if dim <= pref:
        return dim
    t = (pref // align) * align
    while t >= align:
        if dim % t == 0:
            return t
        t -= align
    return dim


def _rows_per_tile(rows, width, tile_bytes=1 << 20):
    return _tile(rows, max(8, tile_bytes // (4 * width)), 8)


def _params(sem):
    return pltpu.CompilerParams(dimension_semantics=sem, vmem_limit_bytes=VMEM_LIMIT)


def _sigmoid(x):
    return 1.0 / (1.0 + jnp.exp(-x))


def _dot(a, b, ca, cb):
    return lax.dot_general(a, b, (((ca,), (cb,)), ((), ())), preferred_element_type=F32)


def _mm(a, b, *, ta=False, tb=False, a_lead=None, b_lead=None, out_dtype=BF16, tm=768, tn=512, tk=2048, name):
    ash = a.shape[1:] if a_lead is not None else a.shape
    bsh = b.shape[1:] if b_lead is not None else b.shape
    m, k = (ash[1], ash[0]) if ta else ash
    n, k2 = bsh if tb else (bsh[1], bsh[0])
    assert k == k2, (name, ash, bsh)
    tm, tn, tk = _tile(m, tm), _tile(n, tn), _tile(k, tk)
    nk = k // tk

    def lead(spec_shape, imap, l):
        if l is None:
            return pl.BlockSpec(spec_shape, imap)
        return pl.BlockSpec((None,) + spec_shape, lambda i, j, kk: (l,) + imap(i, j, kk))

    a_spec = lead((tk, tm), lambda i, j, kk: (kk, i), a_lead) if ta else lead((tm, tk), lambda i, j, kk: (i, kk), a_lead)
    b_spec = lead((tn, tk), lambda i, j, kk: (j, kk), b_lead) if tb else lead((tk, tn), lambda i, j, kk: (kk, j), b_lead)
    ca, cb = (0 if ta else 1), (1 if tb else 0)

    def body(a_ref, b_ref, o_ref, *scratch):
        part = _dot(a_ref[...].astype(BF16), b_ref[...].astype(BF16), ca, cb)
        if nk == 1:
            o_ref[...] = part.astype(o_ref.dtype)
            return
        acc_ref, = scratch
        kk = pl.program_id(2)

        @pl.when(kk == 0)
        def _():
            acc_ref[...] = part

        @pl.when(kk > 0)
        def _():
            acc_ref[...] += part

        @pl.when(kk == nk - 1)
        def _():
            o_ref[...] = acc_ref[...].astype(o_ref.dtype)

    return pl.pallas_call(
        body, name=name, grid=(m // tm, n // tn, nk),
        in_specs=[a_spec, b_spec],
        out_specs=pl.BlockSpec((tm, tn), lambda i, j, kk: (i, j)),
        out_shape=jax.ShapeDtypeStruct((m, n), out_dtype),
        scratch_shapes=[] if nk == 1 else [pltpu.VMEM((tm, tn), F32)],
        compiler_params=_params(("parallel", "parallel", "arbitrary")),
    )(a, b)


def _ew(fn, ins, outs, *, rows, tr, name, n0=None, aliases=None):
    assert rows % tr == 0, (name, rows, tr)
    nt = rows // tr

    def grp(i):
        return 0 if n0 is None else jnp.where(i < n0, 0, 1)

    in_specs, args = [], []
    for spec in ins:
        if spec[0] == 't':
            arr, cb, w = spec[1], spec[2], spec[3]
            l = spec[4] if len(spec) > 4 else None
            if l is None:
                in_specs.append(pl.BlockSpec((tr, w), functools.partial(lambda i, cb: (i, cb), cb=cb)))
            else:
                in_specs.append(pl.BlockSpec((None, tr, w), functools.partial(lambda i, cb, l: (l, i, cb), cb=cb, l=l)))
            args.append(arr)
        else:
            arr = spec[1]
            g = arr.shape[0]
            if g == 1:
                in_specs.append(pl.BlockSpec((None, 1, arr.shape[2]), lambda i: (0, 0, 0)))
            else:
                in_specs.append(pl.BlockSpec((None, 1, arr.shape[2]), lambda i: (grp(i), 0, 0)))
            args.append(arr)
    out_specs, out_shapes, is_red = [], [], []
    for spec in outs:
        if spec[0] == 't':
            w, dt = spec[1], spec[2]
            if len(spec) > 3:
                l, nl = spec[3], spec[4]
                out_specs.append(pl.BlockSpec((None, tr, w), functools.partial(lambda i, l: (l, i, 0), l=l)))
                out_shapes.append(jax.ShapeDtypeStruct((nl, rows, w), dt))
            else:
                out_specs.append(pl.BlockSpec((tr, w), lambda i: (i, 0)))
                out_shapes.append(jax.ShapeDtypeStruct((rows, w), dt))
            is_red.append(False)
        else:
            w, g = spec[1], spec[2]
            if g == 1:
                out_specs.append(pl.BlockSpec((None, 1, w), lambda i: (0, 0, 0)))
            else:
                out_specs.append(pl.BlockSpec((None, 1, w), lambda i: (grp(i), 0, 0)))
            out_shapes.append(jax.ShapeDtypeStruct((g, 1, w), F32))
            is_red.append(True)
    n_in = len(ins)
    n_alias = 0 if aliases is None else len(aliases)

    def body(*refs):
        in_refs = refs[:n_in]
        out_refs = refs[n_in + n_alias:]
        res = fn(*[r[...] for r in in_refs])
        if not isinstance(res, (tuple, list)):
            res = (res,)
        i = pl.program_id(0)
        first = (i == 0) if n0 is None else ((i == 0) | (i == n0))
        for o_ref, val, red in zip(out_refs, res, is_red):
            if not red:
                o_ref[...] = val.astype(o_ref.dtype)
            else:
                @pl.when(first)
                def _(o_ref=o_ref, val=val):
                    o_ref[...] = val

                @pl.when(jnp.logical_not(first))
                def _(o_ref=o_ref, val=val):
                    o_ref[...] += val

    io_alias = {}
    if aliases is not None:
        for a_idx, (arr, o_idx) in enumerate(aliases):
            in_specs.append(ANY)
            args.append(arr)
            io_alias[n_in + a_idx] = o_idx
    has_red = any(is_red)
    return pl.pallas_call(
        body, name=name, grid=(nt,), in_specs=in_specs, out_specs=out_specs, out_shape=out_shapes,
        input_output_aliases=io_alias,
        compiler_params=_params(("arbitrary",) if has_red else ("parallel",)),
    )(*args)


def _rsum(v):
    return jnp.sum(v, axis=0, keepdims=True)


def _silu_parts(z):
    sg = _sigmoid(z)
    return z * sg, sg * (1.0 + z * (1.0 - sg))


def _na_bias_table(rpb, rows, *, name):
    kh, kw = NA_WIN_ROWS, NA_WIN_COLS
    assert rows >= kh
    heads = rpb.shape[0]
    e1, e2 = _na_onehots()
    rpb16 = jnp.pad(rpb, ((0, 0), (0, 16 - rpb.shape[1]), (0, LANES - rpb.shape[2])))

    def body(r_ref, e1_ref, e2_ref, o_ref):
        e1b = e1_ref[...].astype(BF16)
        y = sum(_dot(e1b, part, 0, 0) for part in _split3(r_ref[...]))
        e2b = e2_ref[...].astype(BF16)
        o_ref[...] = sum(_dot(part, e2b, 1, 1) for part in _split3(y))

    z = pl.pallas_call(
        body, name=name, grid=(heads,),
        in_specs=[pl.BlockSpec((None, 16, LANES), lambda h: (h, 0, 0)),
                  pl.BlockSpec(e1.shape, lambda h: (0, 0)), pl.BlockSpec(e2.shape, lambda h: (0, 0))],
        out_specs=pl.BlockSpec((None, kh * kh, GRID_W * GRID_W), lambda h: (h, 0, 0)),
        out_shape=jax.ShapeDtypeStruct((heads, kh * kh, GRID_W * GRID_W), F32),
        compiler_params=_params(("parallel",)),
    )(rpb16, e1, e2)
    cidx = np.arange(GRID_W)
    c0 = np.clip(cidx - kw // 2, 0, GRID_W - kw)
    col_in = (cidx[None, :] >= c0[:, None]) & (cidx[None, :] < c0[:, None] + kw)
    bias = z.reshape(heads, kh, kh, GRID_W, GRID_W).transpose(0, 1, 3, 2, 4)
    bias = jnp.where(col_in[None, None, :, None, :], bias, MASK_VALUE)
    return bias.reshape(heads, kh, GRID_W, kh * GRID_W)


def _na_onehots():
    kh, kw = NA_WIN_ROWS, NA_WIN_COLS
    cidx = np.arange(GRID_W)
    dc = cidx[None, :] - cidx[:, None] + (kw - 1)
    e2 = np.zeros((GRID_W * GRID_W, LANES), np.float32)
    ok = (dc >= 0) & (dc <= 2 * kw - 2)
    cq, ck = np.nonzero(ok)
    e2[cq * GRID_W + ck, dc[cq, ck]] = 1.0
    dr = np.arange(kh)[None, :] - np.arange(kh)[:, None] + (kh - 1)
    e1 = np.zeros((16, kh * kh), np.float32)
    dl, kr = np.nonzero(np.ones_like(dr))
    e1[dr[dl, kr], dl * kh + kr] = 1.0
    return jnp.asarray(e1), jnp.asarray(e2)


def _na_row_scores(q, kl, kc, bias, scale):
    s_loc = _dot(q, kl, 1, 1) * scale + bias
    s_ctx = _dot(q, kc, 1, 1) * scale
    m = jnp.maximum(jnp.max(s_loc, axis=-1, keepdims=True), jnp.max(s_ctx, axis=-1, keepdims=True))
    p_loc = jnp.exp(s_loc - m)
    p_ctx = jnp.exp(s_ctx - m)
    den = jnp.sum(p_loc, axis=-1, keepdims=True) + jnp.sum(p_ctx, axis=-1, keepdims=True)
    return p_loc, p_ctx, den


def _na_fwd(u, bias, *, s_len, heads, name):
    t_len = u.shape[0]
    rows = s_len // GRID_W
    nloc = NA_WIN_ROWS * GRID_W
    scale = NA_HEAD_DIM ** -0.5
    hd = NA_HEAD_DIM

    def body(q_ref, k_ref, v_ref, b_ref, o_ref):
        kc = k_ref[s_len:t_len, :]
        vc = v_ref[s_len:t_len, :]

        def row(r, carry):
            r0 = jnp.clip(r - NA_WIN_ROWS // 2, 0, rows - NA_WIN_ROWS)
            qs = pl.multiple_of(r * GRID_W, GRID_W)
            ks = pl.multiple_of(r0 * GRID_W, GRID_W)
            q = q_ref[pl.ds(qs, GRID_W), :]
            kl = k_ref[pl.ds(ks, nloc), :]
            vl = v_ref[pl.ds(ks, nloc), :]
            p_loc, p_ctx, den = _na_row_scores(q, kl, kc, b_ref[r - r0], scale)
            o = _dot(p_loc.astype(BF16), vl, 1, 0) + _dot(p_ctx.astype(BF16), vc, 1, 0)
            o_ref[pl.ds(qs, GRID_W), :] = (o / den).astype(o_ref.dtype)
            return carry

        lax.fori_loop(0, rows, row, 0, unroll=2)
        qc = q_ref[s_len:t_len, :]
        s = _dot(qc, kc, 1, 1) * scale
        p = jnp.exp(s - jnp.max(s, axis=-1, keepdims=True))
        o = _dot(p.astype(BF16), vc, 1, 0) / jnp.sum(p, axis=-1, keepdims=True)
        o_ref[s_len:t_len, :] = o.astype(o_ref.dtype)

    col = lambda off: pl.BlockSpec((t_len, hd), functools.partial(lambda h, off: (0, off + h), off=off))
    return pl.pallas_call(
        body, name=name, grid=(heads,),
        in_specs=[col(0), col(heads), col(2 * heads),
                  pl.BlockSpec((None, NA_WIN_ROWS, GRID_W, nloc), lambda h: (h, 0, 0, 0))],
        out_specs=pl.BlockSpec((t_len, hd), lambda h: (0, h)),
        out_shape=jax.ShapeDtypeStruct((t_len, heads * hd), BF16),
        compiler_params=_params(("parallel",)),
    )(u, u, u, bias)


def _na_bwd(u, bias, o, do, *, s_len, heads, name):
    t_len = u.shape[0]
    rows = s_len // GRID_W
    nloc = NA_WIN_ROWS * GRID_W
    scale = NA_HEAD_DIM ** -0.5
    hd = NA_HEAD_DIM

    def body(q_ref, k_ref, v_ref, b_ref, o_ref, do_ref, dq_ref, dk_ref, dv_ref, db_ref, dk_acc, dv_acc):
        kc = k_ref[s_len:t_len, :]
        vc = v_ref[s_len:t_len, :]
        dk_acc[...] = jnp.zeros_like(dk_acc)
        dv_acc[...] = jnp.zeros_like(dv_acc)
        db_ref[...] = jnp.zeros_like(db_ref)

        def row(r, carry):
            r0 = jnp.clip(r - NA_WIN_ROWS // 2, 0, rows - NA_WIN_ROWS)
            dl = r - r0
            qs = pl.multiple_of(r * GRID_W, GRID_W)
            ks = pl.multiple_of(r0 * GRID_W, GRID_W)
            q = q_ref[pl.ds(qs, GRID_W), :]
            kl = k_ref[pl.ds(ks, nloc), :]
            vl = v_ref[pl.ds(ks, nloc), :]
            dout = do_ref[pl.ds(qs, GRID_W), :]
            out = o_ref[pl.ds(qs, GRID_W), :]
            p_loc, p_ctx, den = _na_row_scores(q, kl, kc, b_ref[dl], scale)
            inv = 1.0 / den
            p_loc = p_loc * inv
            p_ctx = p_ctx * inv
            dlt = jnp.sum(dout.astype(F32) * out.astype(F32), axis=-1, keepdims=True)
            ds_loc = p_loc * (_dot(dout, vl, 1, 1) - dlt)
            ds_ctx = p_ctx * (_dot(dout, vc, 1, 1) - dlt)
            db_ref[dl] += ds_loc
            ds_loc_b = ds_loc.astype(BF16)
            ds_ctx_b = ds_ctx.astype(BF16)
            dq = (_dot(ds_loc_b, kl, 1, 0) + _dot(ds_ctx_b, kc, 1, 0)) * scale
            dq_ref[pl.ds(qs, GRID_W), :] = dq.astype(dq_ref.dtype)
            dk_acc[pl.ds(ks, nloc), :] += _dot(ds_loc_b, q, 0, 0) * scale
            dv_acc[pl.ds(ks, nloc), :] += _dot(p_loc.astype(BF16), dout, 0, 0)
            dk_acc[s_len:t_len, :] += _dot(ds_ctx_b, q, 0, 0) * scale
            dv_acc[s_len:t_len, :] += _dot(p_ctx.astype(BF16), dout, 0, 0)
            return carry

        lax.fori_loop(0, rows, row, 0, unroll=2)
        qc = q_ref[s_len:t_len, :]
        dout = do_ref[s_len:t_len, :]
        out = o_ref[s_len:t_len, :]
        s = _dot(qc, kc, 1, 1) * scale
        p = jnp.exp(s - jnp.max(s, axis=-1, keepdims=True))
        p = p / jnp.sum(p, axis=-1, keepdims=True)
        dlt = jnp.sum(dout.astype(F32) * out.astype(F32), axis=-1, keepdims=True)
        ds = (p * (_dot(dout, vc, 1, 1) - dlt)).astype(BF16)
        dq_ref[s_len:t_len, :] = (_dot(ds, kc, 1, 0) * scale).astype(dq_ref.dtype)
        dk_acc[s_len:t_len, :] += _dot(ds, qc, 0, 0) * scale
        dv_acc[s_len:t_len, :] += _dot(p.astype(BF16), dout, 0, 0)
        dk_ref[...] = dk_acc[...].astype(dk_ref.dtype)
        dv_ref[...] = dv_acc[...].astype(dv_ref.dtype)

    col = lambda off: pl.BlockSpec((t_len, hd), functools.partial(lambda h, off: (0, off + h), off=off))
    tbl = pl.BlockSpec((None, NA_WIN_ROWS, GRID_W, nloc), lambda h: (h, 0, 0, 0))
    tok = jax.ShapeDtypeStruct((t_len, heads * hd), BF16)
    return pl.pallas_call(
        body, name=name, grid=(heads,),
        in_specs=[col(0), col(heads), col(2 * heads), tbl, col(0), col(0)],
        out_specs=[col(0), col(0), col(0), tbl],
        out_shape=[tok, tok, tok, jax.ShapeDtypeStruct(bias.shape, F32)],
        scratch_shapes=[pltpu.VMEM((t_len, hd), F32), pltpu.VMEM((t_len, hd), F32)],
        compiler_params=_params(("parallel",)),
    )(u, u, u, bias, o, do)


def _split3(x):
    hi = x.astype(BF16)
    r1 = x - hi.astype(F32)
    mid = r1.astype(BF16)
    lo = (r1 - mid.astype(F32)).astype(BF16)
    return hi, mid, lo


def _rpb_grad(dbias, *, name):
    heads = dbias.shape[0]
    kh = NA_WIN_ROWS
    e1, e2 = _na_onehots()
    x = dbias.reshape(heads, kh, GRID_W, kh, GRID_W).transpose(0, 1, 3, 2, 4).reshape(heads, kh * kh, GRID_W * GRID_W)

    def body(x_ref, e1_ref, e2_ref, o_ref):
        e2b = e2_ref[...].astype(BF16)
        y = sum(_dot(part, e2b, 1, 0) for part in _split3(x_ref[...]))
        e1b = e1_ref[...].astype(BF16)
        o_ref[...] = sum(_dot(e1b, part, 1, 0) for part in _split3(y))

    out = pl.pallas_call(
        body, name=name, grid=(heads,),
        in_specs=[pl.BlockSpec((None, kh * kh, GRID_W * GRID_W), lambda h: (h, 0, 0)),
                  pl.BlockSpec(e1.shape, lambda h: (0, 0)), pl.BlockSpec(e2.shape, lambda h: (0, 0))],
        out_specs=pl.BlockSpec((None, 16, LANES), lambda h: (h, 0, 0)),
        out_shape=jax.ShapeDtypeStruct((heads, 16, LANES), F32),
        compiler_params=_params(("parallel",)),
    )(x, e1, e2)
    return out[:, :2 * kh - 1, :2 * NA_WIN_COLS - 1]


def _rope_tables(s_len, l_len):
    nf = RET_KEY_DIM // 4
    t = np.arange(s_len)
    row = (t // GRID_W).astype(np.float32)
    colp = (t % GRID_W).astype(np.float32)
    inv_freq = jnp.asarray(ROPE_BASE, F32) ** (-jnp.arange(nf, dtype=F32) / nf)
    ang = jnp.concatenate([jnp.asarray(row)[:, None] * inv_freq, jnp.asarray(colp)[:, None] * inv_freq], axis=-1)
    cos, sin = jnp.cos(ang), jnp.sin(ang)
    c2 = jnp.concatenate([cos, cos], axis=-1)
    s2 = jnp.concatenate([-sin, sin], axis=-1)
    c2 = jnp.concatenate([c2, jnp.ones((l_len, RET_KEY_DIM), F32)], axis=0)
    s2 = jnp.concatenate([s2, jnp.zeros((l_len, RET_KEY_DIM), F32)], axis=0)
    return c2, s2


def _rope(x, c2, s2):
    return x * c2 + pltpu.roll(x, RET_KEY_DIM // 2, 1) * s2


def _rope_t(d, c2, s2):
    return d * c2 + pltpu.roll(d * s2, RET_KEY_DIM // 2, 1)


def _ret_decays(lg, direction):
    cs = RET_CHUNK
    i_col = lax.broadcasted_iota(jnp.int32, (cs, 1), 0)
    p_col = jnp.where(direction == 0, i_col, cs - 1 - i_col).astype(F32)
    pi = lax.broadcasted_iota(jnp.int32, (cs, cs), 0)
    pj = lax.broadcasted_iota(jnp.int32, (cs, cs), 1)
    diff = jnp.where(direction == 0, pi - pj, pj - pi).astype(F32)
    dm = jnp.where(diff >= 0, jnp.exp(jnp.maximum(diff, 0.0) * lg), 0.0)
    qdec = jnp.exp((p_col + 1.0) * lg)
    kdec = jnp.exp((cs - 1.0 - p_col) * lg)
    cd = jnp.exp(jnp.full((1, 1), cs, F32) * lg)
    return p_col, dm, qdec, kdec, cd


def _ret_chunk_index(t, direction, n_chunks, lat_chunks):
    return jnp.where(direction == 0, lax.rem(t + lat_chunks, n_chunks), n_chunks - 1 - t)


def _ret_fwd(u, c2, s2, lg, *, s_len, heads, q_off, name):
    t_len = u.shape[0]
    cs, dk, dv = RET_CHUNK, RET_KEY_DIM, RET_VAL_DIM
    n_chunks, lat_chunks = t_len // cs, s_len // cs
    k_scale = dk ** -0.5
    qb, kb, vb = q_off // dk, q_off // dk + heads, (q_off + 2 * heads * dk) // dv

    def body(lg_ref, q_ref, k_ref, v_ref, c_ref, s_ref, o_ref, st_ref, state):
        h, d = pl.program_id(0), pl.program_id(1)
        _, dm, qdec, kdec, cd = _ret_decays(lg_ref[d, h], d)
        state[...] = jnp.zeros_like(state)

        def step(t, carry):
            c = _ret_chunk_index(t, d, n_chunks, lat_chunks)
            r = pl.ds(pl.multiple_of(c * cs, cs), cs)
            cc, ss = c_ref[r, :], s_ref[r, :]
            qc = _rope(q_ref[r, :].astype(F32), cc, ss)
            kc = _rope(k_ref[r, :].astype(F32), cc, ss) * k_scale
            vc = v_ref[r, :]
            st = state[...]
            st_ref[t] = st
            a = _dot(qc.astype(BF16), kc.astype(BF16), 1, 1) * dm
            oc = _dot(a.astype(BF16), vc, 1, 0) + _dot((qc * qdec).astype(BF16), st.astype(BF16), 1, 0)
            state[...] = st * cd + _dot((kc * kdec).astype(BF16), vc, 0, 0)

            @pl.when(d == 0)
            def _():
                o_ref[r, :] = oc

            @pl.when(d == 1)
            def _():
                o_ref[r, :] += oc

            return carry

        lax.fori_loop(0, n_chunks, step, 0, unroll=2)

    return pl.pallas_call(
        body, name=name, grid=(heads, 2),
        in_specs=[pl.BlockSpec(memory_space=pltpu.SMEM),
                  pl.BlockSpec((t_len, dk), lambda h, d: (0, qb + h)),
                  pl.BlockSpec((t_len, dk), lambda h, d: (0, kb + h)),
                  pl.BlockSpec((t_len, dv), lambda h, d: (0, vb + h)),
                  pl.BlockSpec((t_len, dk), lambda h, d: (0, 0)),
                  pl.BlockSpec((t_len, dk), lambda h, d: (0, 0))],
        out_specs=[pl.BlockSpec((t_len, dv), lambda h, d: (0, h)),
                   pl.BlockSpec((None, None, n_chunks, dk, dv), lambda h, d: (h, d, 0, 0, 0))],
        out_shape=[jax.ShapeDtypeStruct((t_len, heads * dv), F32),
                   jax.ShapeDtypeStruct((heads, 2, n_chunks, dk, dv), F32)],
        scratch_shapes=[pltpu.VMEM((dk, dv), F32)],
        compiler_params=_params(("parallel", "arbitrary")),
    )(lg, u, u, u, c2, s2)


def _ret_bwd(u, c2, s2, lg, states, do, *, s_len, heads, q_off, name):
    t_len = u.shape[0]
    cs, dk, dv = RET_CHUNK, RET_KEY_DIM, RET_VAL_DIM
    n_chunks, lat_chunks = t_len // cs, s_len // cs
    k_scale = dk ** -0.5
    qb, kb, vb = q_off // dk, q_off // dk + heads, (q_off + 2 * heads * dk) // dv

    def body(lg_ref, q_ref, k_ref, v_ref, c_ref, s_ref, st_ref, do_ref, dq_ref, dk_ref, dv_ref, dlg_ref, dstate, acc):
        h, d = pl.program_id(0), pl.program_id(1)
        p_col, dm, qdec, kdec, cd = _ret_decays(lg_ref[d, h], d)
        dstate[...] = jnp.zeros_like(dstate)
        acc[...] = jnp.zeros_like(acc)

        def step(i, carry):
            t = n_chunks - 1 - i
            c = _ret_chunk_index(t, d, n_chunks, lat_chunks)
            r = pl.ds(pl.multiple_of(c * cs, cs), cs)
            cc, ss = c_ref[r, :], s_ref[r, :]
            qc = _rope(q_ref[r, :].astype(F32), cc, ss)
            kc = _rope(k_ref[r, :].astype(F32), cc, ss) * k_scale
            vc = v_ref[r, :]
            doc = do_ref[r, :].astype(BF16)
            st = st_ref[t]
            dst = dstate[...]
            qb16, kb16 = qc.astype(BF16), kc.astype(BF16)
            a = _dot(qb16, kb16, 1, 1) * dm
            dam = (_dot(doc, vc, 1, 1) * dm).astype(BF16)
            dq_i = _dot(dam, kb16, 1, 0)
            dk_i = _dot(dam, qb16, 0, 0)
            dq_c = _dot(doc, st.astype(BF16), 1, 1) * qdec
            dst16 = dst.astype(BF16)
            dvc = _dot(a.astype(BF16), doc, 0, 0) + _dot((kc * kdec).astype(BF16), dst16, 1, 0)
            dk_s = _dot(vc, dst16, 1, 1) * kdec
            g = (jnp.sum(qc * (p_col * dq_i + (p_col + 1.0) * dq_c), axis=-1, keepdims=True)
                 + jnp.sum(kc * ((cs - 1.0 - p_col) * dk_s - p_col * dk_i), axis=-1, keepdims=True))
            g = jnp.sum(g, axis=0, keepdims=True) + cs * cd * jnp.sum(jnp.sum(dst * st, axis=-1, keepdims=True), axis=0, keepdims=True)
            acc[...] += jnp.broadcast_to(g, acc.shape)
            dstate[...] = dst * cd + _dot((qc * qdec).astype(BF16), doc, 0, 0)
            dq = _rope_t(dq_i + dq_c, cc, ss)
            dkk = _rope_t((dk_i + dk_s) * k_scale, cc, ss)

            @pl.when(d == 0)
            def _():
                dq_ref[r, :] = dq.astype(dq_ref.dtype)
                dk_ref[r, :] = dkk.astype(dk_ref.dtype)
                dv_ref[r, :] = dvc.astype(dv_ref.dtype)

            @pl.when(d == 1)
            def _():
                dq_ref[r, :] = (dq_ref[r, :].astype(F32) + dq).astype(dq_ref.dtype)
                dk_ref[r, :] = (dk_ref[r, :].astype(F32) + dkk).astype(dk_ref.dtype)
                dv_ref[r, :] = (dv_ref[r, :].astype(F32) + dvc).astype(dv_ref.dtype)

            return carry

        lax.fori_loop(0, n_chunks, step, 0, unroll=2)
        dlg_ref[...] = acc[...]

    return pl.pallas_call(
        body, name=name, grid=(heads, 2),
        in_specs=[pl.BlockSpec(memory_space=pltpu.SMEM),
                  pl.BlockSpec((t_len, dk), lambda h, d: (0, qb + h)),
                  pl.BlockSpec((t_len, dk), lambda h, d: (0, kb + h)),
                  pl.BlockSpec((t_len, dv), lambda h, d: (0, vb + h)),
                  pl.BlockSpec((t_len, dk), lambda h, d: (0, 0)),
                  pl.BlockSpec((t_len, dk), lambda h, d: (0, 0)),
                  pl.BlockSpec((None, None, n_chunks, dk, dv), lambda h, d: (h, d, 0, 0, 0)),
                  pl.BlockSpec((t_len, dv), lambda h, d: (0, h))],
        out_specs=[pl.BlockSpec((t_len, dk), lambda h, d: (0, h)),
                   pl.BlockSpec((t_len, dk), lambda h, d: (0, h)),
                   pl.BlockSpec((t_len, dv), lambda h, d: (0, h)),
                   pl.BlockSpec((None, None, 8, LANES), lambda h, d: (h, d, 0, 0))],
        out_shape=[jax.ShapeDtypeStruct((t_len, heads * dk), BF16),
                   jax.ShapeDtypeStruct((t_len, heads * dk), BF16),
                   jax.ShapeDtypeStruct((t_len, heads * dv), BF16),
                   jax.ShapeDtypeStruct((heads, 2, 8, LANES), F32)],
        scratch_shapes=[pltpu.VMEM((dk, dv), F32), pltpu.VMEM((8, LANES), F32)],
        compiler_params=_params(("parallel", "arbitrary")),
    )(lg, u, u, u, c2, s2, states, do)


def _mesh_pos():
    return lax.axis_index("x"), lax.axis_index("y"), lax.axis_index("c")


def _all_gather_small(buf, *, name):
    r = buf.shape[0]

    def body(x_ref, o_ref, send_sems, recv_sems, local_sem):
        x, y, c = _mesh_pos()
        me = 4 * x + 2 * y + c
        mine = pltpu.make_async_copy(x_ref, o_ref.at[me], local_sem)
        mine.start()
        copies = []
        for k in range(1, N_DEV):
            px, py, pc = x ^ ((k >> 2) & 1), y ^ ((k >> 1) & 1), c ^ (k & 1)
            cp = pltpu.make_async_remote_copy(
                src_ref=x_ref, dst_ref=o_ref.at[me], send_sem=send_sems.at[k - 1], recv_sem=recv_sems.at[k - 1],
                device_id=(px, py, pc), device_id_type=MESH)
            cp.start()
            copies.append((cp, 4 * px + 2 * py + pc))
        for k, (cp, peer) in enumerate(copies):
            pltpu.make_async_remote_copy(
                src_ref=x_ref, dst_ref=o_ref.at[peer], send_sem=send_sems.at[k], recv_sem=recv_sems.at[k],
                device_id=(x, y, c), device_id_type=MESH).wait_recv()
        for cp, _ in copies:
            cp.wait_send()
        mine.wait()

    return pl.pallas_call(
        body, name=name,
        in_specs=[pl.BlockSpec(memory_space=pltpu.VMEM)],
        out_specs=pl.BlockSpec(memory_space=pltpu.VMEM),
        out_shape=jax.ShapeDtypeStruct((N_DEV, r, LANES), F32),
        scratch_shapes=[pltpu.SemaphoreType.DMA((N_DEV - 1,)), pltpu.SemaphoreType.DMA((N_DEV - 1,)),
                        pltpu.SemaphoreType.DMA],
        compiler_params=pltpu.CompilerParams(vmem_limit_bytes=VMEM_LIMIT),
    )(buf)


def _cut(ref, shard_axis, *, chip=None, half=None, lead=None):
    shape = ref.shape[1:] if lead is not None else ref.shape
    idx = [slice(None), slice(None)]
    if chip is not None:
        w = shape[shard_axis] // N_CHIPS
        idx[shard_axis] = pl.ds(pl.multiple_of(chip * w, w), w)
    if half is not None:
        hw = shape[1 - shard_axis] // 2
        idx[1 - shard_axis] = pl.ds(pl.multiple_of(half * hw, hw), hw)
    if lead is not None:
        idx = [lead] + idx
    return ref.at[tuple(idx)]


def _wait_recv(ref, send_sem, recv_sem):
    pltpu.make_async_remote_copy(src_ref=ref, dst_ref=ref, send_sem=send_sem, recv_sem=recv_sem,
                                 device_id=_mesh_pos(), device_id_type=MESH).wait_recv()


def _gather_plan(axes):
    def plan(srcs, lands, send_sems, recv_sems):
        x, y, c = _mesh_pos()
        chip = 2 * x + y
        copies = []
        for i, ax in enumerate(axes):
            for k in range(1, N_CHIPS):
                px, py = x ^ (k >> 1), y ^ (k & 1)
                mine = _cut(lands[i], ax, chip=chip, half=c)
                j = i * (N_CHIPS - 1) + k - 1
                sems = dict(send_sem=send_sems.at[j], recv_sem=recv_sems.at[j], device_id=(px, py, c), device_id_type=MESH)
                send = pltpu.make_async_remote_copy(src_ref=mine, dst_ref=mine, **sems)
                recv = pltpu.make_async_remote_copy(src_ref=mine, dst_ref=_cut(lands[i], ax, chip=2 * px + py, half=c), **sems)
                copies.append((send, recv))
        return copies
    return plan


def _scatter_plan(axes):
    def plan(srcs, lands, send_sems, recv_sems):
        x, y, c = _mesh_pos()
        copies = []
        for i, ax in enumerate(axes):
            for k in range(1, N_CHIPS):
                px, py = x ^ (k >> 1), y ^ (k & 1)
                j = i * (N_CHIPS - 1) + k - 1
                cp = pltpu.make_async_remote_copy(
                    src_ref=_cut(srcs[i], ax, chip=2 * px + py), dst_ref=lands[i].at[k - 1],
                    send_sem=send_sems.at[j], recv_sem=recv_sems.at[j], device_id=(px, py, c), device_id_type=MESH)
                copies.append((cp, cp))
        return copies
    return plan


HBM = pl.BlockSpec(memory_space=pltpu.HBM)
SEM = pl.BlockSpec(memory_space=pltpu.SEMAPHORE)
EFFECT = pltpu.SideEffectType.DATAFLOW_SIDE_EFFECTING


def _in_hbm(arrays):
    return [pltpu.with_memory_space_constraint(a, pltpu.HBM) for a in arrays]


def _split_start(srcs, lands, plan, n_copies, *, name):
    bufs = list(srcs) + list(lands)
    ns, nb = len(srcs), len(bufs)

    def body(*refs):
        send_sems, recv_sems, token = refs[nb], refs[nb + 1], refs[-1]
        for send, _ in plan(refs[:ns], refs[ns:nb], send_sems, recv_sems):
            send.start()
        token[...] = jnp.zeros_like(token)

    sems = pltpu.SemaphoreType.DMA((n_copies,))
    res = pl.pallas_call(
        body, name=name, in_specs=[HBM] * nb,
        out_specs=[SEM, SEM] + [HBM] * nb + [pl.BlockSpec(memory_space=pltpu.VMEM)],
        out_shape=[sems, sems] + [pltpu.HBM(a.shape, a.dtype) for a in bufs] + [jax.ShapeDtypeStruct((8, LANES), F32)],
        input_output_aliases={j: 2 + j for j in range(nb)},
        compiler_params=pltpu.CompilerParams(has_side_effects=EFFECT),
    )(*_in_hbm(bufs))
    return res[0], res[1], res[2:2 + ns], res[2 + ns:2 + nb], res[-1]


def _split_wait(started, after, plan, *, name):
    send_sems, recv_sems, srcs, lands, _ = started
    bufs = list(srcs) + list(lands)
    ns, nb = len(srcs), len(bufs)

    def body(*refs):
        for send, recv in plan(refs[:ns], refs[ns:nb], refs[nb], refs[nb + 1]):
            send.wait_send()
            recv.wait_recv()

    res = pl.pallas_call(
        body, name=name, in_specs=[HBM] * nb + [SEM, SEM, ANY], out_specs=[HBM] * nb,
        out_shape=[pltpu.HBM(a.shape, a.dtype) for a in bufs],
        input_output_aliases={j: j for j in range(nb)},
        compiler_params=pltpu.CompilerParams(has_side_effects=EFFECT),
    )(*bufs, send_sems, recv_sems, after)
    return res[ns:]


def _cast_into_full(w3, layer, ax, chip, *, after=None, name):
    _, r, wd = w3.shape
    tr = _rows_per_tile(r, wd, 4 << 20)
    nt = r // tr
    full_shape = (r, wd * N_CHIPS) if ax == 1 else (r * N_CHIPS, wd)
    out_map = (lambda i, ch: (i, ch[0])) if ax == 1 else (lambda i, ch: (ch[0] * nt + i, 0))
    zero = jnp.zeros((1, wd), F32) + (0.0 if after is None else after)

    def body(chip_ref, w_ref, z_ref, o_ref):
        o_ref[...] = (w_ref[...] + z_ref[...]).astype(o_ref.dtype)

    return pl.pallas_call(
        body, name=name,
        grid_spec=pltpu.PrefetchScalarGridSpec(
            num_scalar_prefetch=1, grid=(nt,),
            in_specs=[pl.BlockSpec((None, tr, wd), lambda i, ch: (layer, i, 0)), pl.BlockSpec((1, wd), lambda i, ch: (0, 0))],
            out_specs=pl.BlockSpec((tr, wd), out_map)),
        out_shape=jax.ShapeDtypeStruct(full_shape, BF16),
        compiler_params=_params(("parallel",)),
    )(jnp.reshape(chip, (1,)).astype(jnp.int32), w3, zero)


def _forward_halves(fulls, axes, *, name):
    n = len(fulls)

    def body(*refs):
        bufs = refs[:n]
        send_sems, recv_sems = refs[2 * n:]
        x, y, c = _mesh_pos()
        sends = []
        for i in range(n):
            for k in range(1, N_CHIPS):
                landed = _cut(bufs[i], axes[i], chip=2 * (x ^ (k >> 1)) + (y ^ (k & 1)), half=c)
                cp = pltpu.make_async_remote_copy(
                    src_ref=landed, dst_ref=landed, send_sem=send_sems.at[i, k - 1], recv_sem=recv_sems.at[i, k - 1],
                    device_id=(x, y, 1 - c), device_id_type=MESH)
                cp.start()
                sends.append(cp)
        for i in range(n):
            for k in range(1, N_CHIPS):
                other = _cut(bufs[i], axes[i], chip=2 * (x ^ (k >> 1)) + (y ^ (k & 1)), half=1 - c)
                _wait_recv(other, send_sems.at[i, k - 1], recv_sems.at[i, k - 1])
        for cp in sends:
            cp.wait_send()

    pairs = pltpu.SemaphoreType.DMA((n, N_CHIPS - 1))
    return pl.pallas_call(
        body, name=name, in_specs=[ANY] * n, out_specs=[ANY] * n,
        out_shape=[jax.ShapeDtypeStruct(a.shape, a.dtype) for a in fulls],
        input_output_aliases={j: j for j in range(n)},
        scratch_shapes=[pairs, pairs],
    )(*fulls)


def _send_to_sibling(parts, *, name):
    n = len(parts)

    def body(*refs):
        ins, outs = refs[:n], refs[n:2 * n]
        send_sems, recv_sems = refs[2 * n:]
        x, y, c = _mesh_pos()
        sends = []
        for i in range(n):
            cp = pltpu.make_async_remote_copy(
                src_ref=ins[i], dst_ref=outs[i], send_sem=send_sems.at[i], recv_sem=recv_sems.at[i],
                device_id=(x, y, 1 - c), device_id_type=MESH)
            cp.start()
            sends.append(cp)
        for cp in sends:
            cp.wait()

    sems = pltpu.SemaphoreType.DMA((n,))
    return pl.pallas_call(
        body, name=name, in_specs=[ANY] * n, out_specs=[ANY] * n,
        out_shape=[jax.ShapeDtypeStruct(p.shape, p.dtype) for p in parts],
        scratch_shapes=[sems, sems],
    )(*parts)


def _adamw_math(w, g, m, v):
    m = ADAM_B1 * m + (1.0 - ADAM_B1) * g
    v = ADAM_B2 * v + (1.0 - ADAM_B2) * (g * g)
    m_hat = m / (1.0 - ADAM_B1 ** ADAM_STEP)
    v_hat = v / (1.0 - ADAM_B2 ** ADAM_STEP)
    delta = -ADAM_LR * (m_hat / (jnp.sqrt(v_hat) + ADAM_EPS) + ADAM_WD * w)
    return delta, m, v


def _adamw_layer(w3, m3, v3, p, q, layer, prev, *, name):
    nl, rows, width = w3.shape
    tr = _rows_per_tile(rows, width)

    def fn(*t):
        if q is None:
            w, m, v, g = t
        else:
            w, m, v, g, g2 = t
            g = g + g2
        delta, m, v = _adamw_math(w, g, m, v)
        return g, delta, m, v

    ins = [('t', w3, 0, width, layer), ('t', m3, 0, width, layer), ('t', v3, 0, width, layer), ('t', p, 0, width)]
    if q is not None:
        ins.append(('t', q, 0, width))
    outs = [('t', width, F32, layer, nl)] * 4
    aliases = None if prev is None else [(prev[i], i) for i in range(4)]
    return _ew(fn, ins, outs, rows=rows, tr=tr, name=name, aliases=aliases)


def _pack_rows(vec):
    n = vec.shape[0]
    r = -(-n // (8 * LANES)) * 8
    return jnp.pad(vec, (0, r * LANES - n)).reshape(r, LANES)


def kernel(x, c, ctx, c_ctx, ada_w, ada_b, norm_g, w_in, na_rpb, ret_decay_logit, w_proj_na, w_proj_ret, w_out, final_g, loss_target, m_c_ctx, m_ada_w, m_ada_b, m_norm_g, m_w_in, m_na_rpb, m_ret_decay_logit, m_w_proj_na, m_w_proj_ret, m_w_out, m_final_g, v_c_ctx, v_ada_w, v_ada_b, v_norm_g, v_w_in, v_na_rpb, v_ret_decay_logit, v_w_proj_na, v_w_proj_ret, v_w_out, v_final_g):
    depth = w_in.shape[0]
    s_len, d_model = x.shape[1], x.shape[2]
    l_len = ctx.shape[1]
    t_len = s_len + l_len
    na_heads = na_rpb.shape[1]
    ret_heads = ret_decay_logit.shape[2]
    w_na = na_heads * NA_HEAD_DIM
    w_qk = ret_heads * RET_KEY_DIM
    w_v = ret_heads * RET_VAL_DIM
    in_cols = w_in.shape[2] * N_CHIPS
    assert in_cols == 4 * w_na + 2 * w_qk + 2 * w_v + 2 * d_model
    assert x.shape[0] == 1 and s_len % (NA_WIN_ROWS * GRID_W) == 0 and l_len % RET_CHUNK == 0
    off = np.cumsum([0, w_na, w_na, w_na, w_na, w_qk, w_qk, w_v, w_v, d_model, d_model])
    o_naz, o_retq, o_retz, o_gna, o_gret = int(off[3]), int(off[4]), int(off[7]), int(off[8]), int(off[9])
    rows = s_len // GRID_W
    tr = _tile(l_len, 256, 8)
    n0 = s_len // tr
    mod_cols = 3 * d_model
    mod_shard = ada_w.shape[2]

    xi, yi, ci = _mesh_pos()
    me = 4 * xi + 2 * yi + ci
    chip = 2 * xi + yi

    big_axes = [1, 1, 0, 0]
    n_big = len(big_axes) * (N_CHIPS - 1)
    gather_plan, scatter_plan = _gather_plan(big_axes), _scatter_plan(big_axes)

    c_silu = c[0] * _sigmoid(c[0])
    cc_silu = c_ctx * _sigmoid(c_ctx)
    c_all = _all_gather_small(_pack_rows(c_silu), name="gather_c")[:, :d_model // LANES].reshape(N_DEV, d_model)
    a_rows = jnp.concatenate([c_all, cc_silu[None], jnp.zeros((16 - N_DEV - 1, d_model), F32)], axis=0)
    mod_part = jnp.stack([_mm(a_rows, ada_w, b_lead=l, out_dtype=F32, name="ada_fwd_%d" % l) for l in range(depth)])
    mod_all = _all_gather_small(_pack_rows(mod_part.reshape(-1)), name="gather_mod")
    n_mod = depth * 16 * mod_shard
    mod_all = mod_all.reshape(N_DEV, -1)[:, :n_mod].reshape(N_CHIPS, 2, depth, 16, mod_shard)[:, 0]
    mod_all = jnp.transpose(mod_all, (1, 2, 0, 3)).reshape(depth, 16, mod_cols) + ada_b[:, None, :]

    big_named = list(zip((w_in, w_proj_na, w_proj_ret, w_out), big_axes, ("w_in", "w_proj_na", "w_proj_ret", "w_out")))
    w_in0 = _cast_into_full(w_in, 0, big_axes[0], chip, name="cast_w_in_0")
    mod_all, w_in0 = lax.optimization_barrier((mod_all, w_in0))
    plan_in, plan_rest = _gather_plan(big_axes[:1]), _gather_plan(big_axes[1:])
    first_gather = _split_start([], [w_in0], plan_in, N_CHIPS - 1, name="gather_start_0_in")
    start_token = first_gather[4][0, 0]
    fulls = [[None if (l == 0 and tag == "w_in") else _cast_into_full(w, l, ax, chip, after=start_token, name="cast_%s_%d" % (tag, l))
              for w, ax, tag in big_named] for l in range(depth)]
    mod_lat = lax.dynamic_index_in_dim(mod_all, me, axis=1, keepdims=False)
    mod_ctx = mod_all[:, N_DEV]

    c2, s2 = _rope_tables(s_len, l_len)
    log_gamma = jax.nn.log_sigmoid(ret_decay_logit)
    x_all = jnp.concatenate([x[0], ctx[0]], axis=0)

    def grp(lat_vec, ctx_vec):
        return jnp.stack([lat_vec, ctx_vec])[:, None, :]

    saved, full_w = [], []
    for l in range(depth):
        shift, scale, gate = [grp(mod_lat[l, i * d_model:(i + 1) * d_model], mod_ctx[l, i * d_model:(i + 1) * d_model])
                              for i in range(3)]
        gs = norm_g[l][None, None, :] * (1.0 + scale) + start_token

        def modnorm(xt, gs_t, sh_t):
            r = lax.rsqrt(jnp.mean(xt * xt, axis=-1, keepdims=True) + NORM_EPS)
            return xt * r * gs_t + sh_t

        h, = _ew(modnorm, [('t', x_all, 0, d_model), ('g', gs), ('g', shift)], [('t', d_model, BF16)],
                 rows=t_len, tr=tr, n0=n0, name="modnorm_%d" % l)
        bias = _na_bias_table(na_rpb[l], rows, name="na_bias_%d" % l)
        h, bias = lax.optimization_barrier((h, bias))
        if l == 0:
            landed_in = _split_wait(first_gather, h, plan_in, name="gather_wait_0_in")
            landed_in, rest0, later = lax.optimization_barrier((landed_in, fulls[0][1:], fulls[1:]))
            rest_gather = _split_start([], rest0, plan_rest, n_big - (N_CHIPS - 1), name="gather_start_0_rest")
            later_gathers = [_split_start([], later[j], gather_plan, n_big, name="gather_start_%d" % (j + 1)) for j in range(depth - 1)]
            win_f, = _forward_halves(landed_in, big_axes[:1], name="gather_forward_0_in")
            win_f, tokens = lax.optimization_barrier((win_f, [rest_gather[4]] + [g[4] for g in later_gathers]))
            gate = gate + sum(t[0, 0] for t in tokens)
        else:
            landed = _split_wait(later_gathers[l - 1], h, gather_plan, name="gather_wait_%d" % l)
            win_f, wpn_f, wpr_f, wout_f = _forward_halves(landed, big_axes, name="gather_forward_%d" % l)
        u = _mm(h, win_f, name="in_proj_%d" % l)
        o_na = _na_fwd(u, bias, s_len=s_len, heads=na_heads, name="na_fwd_%d" % l)
        o_ret, states = _ret_fwd(u, c2, s2, log_gamma[l], s_len=s_len, heads=ret_heads, q_off=o_retq, name="ret_fwd_%d" % l)

        def act(o1, z1, o2, z2):
            a1 = o1.astype(F32) * _silu_parts(z1.astype(F32))[0]
            sz = _silu_parts(z2.astype(F32))[0]
            outs = []
            for hh in range(ret_heads):
                sl = slice(hh * RET_VAL_DIM, (hh + 1) * RET_VAL_DIM)
                oh = o2[:, sl]
                r = lax.rsqrt(jnp.mean(oh * oh, axis=-1, keepdims=True) + NORM_EPS)
                outs.append(oh * r * sz[:, sl])
            return a1, jnp.concatenate(outs, axis=-1)

        a_na, a_ret = _ew(act, [('t', o_na, 0, w_na), ('t', u, o_naz // w_na, w_na), ('t', o_ret, 0, w_v), ('t', u, o_retz // w_v, w_v)],
                          [('t', w_na, BF16), ('t', w_v, BF16)], rows=t_len, tr=tr, name="act_%d" % l)
        if l == 0:
            landed_rest = _split_wait(rest_gather, a_na, plan_rest, name="gather_wait_0_rest")
            wpn_f, wpr_f, wout_f = _forward_halves(landed_rest, big_axes[1:], name="gather_forward_0_rest")
        full_w.append((win_f, wpn_f, wpr_f, wout_f))
        y_na = _mm(a_na, wpn_f, name="proj_na_%d" % l)
        y_ret = _mm(a_ret, wpr_f, name="proj_ret_%d" % l)

        def merge(y1, y2, g1, g2):
            return _sigmoid(g1.astype(F32)) * y1.astype(F32) + _sigmoid(g2.astype(F32)) * y2.astype(F32)

        merged, = _ew(merge, [('t', y_na, 0, d_model), ('t', y_ret, 0, d_model), ('t', u, o_gna // d_model, d_model), ('t', u, o_gret // d_model, d_model)],
                      [('t', d_model, BF16)], rows=t_len, tr=tr, name="merge_%d" % l)
        out = _mm(merged, wout_f, out_dtype=F32, name="out_proj_%d" % l)
        x_new, = _ew(lambda xt, ot, gt: xt + gt * ot, [('t', x_all, 0, d_model), ('t', out, 0, d_model), ('g', gate)],
                     [('t', d_model, F32)], rows=t_len, tr=tr, n0=n0, name="resid_%d" % l)
        saved.append(dict(x=x_all, h=h, u=u, bias=bias, o_na=o_na, o_ret=o_ret, states=states, a_na=a_na, a_ret=a_ret,
                          y_na=y_na, y_ret=y_ret, merged=merged, out=out, gate=gate, gs=gs, scale=scale))
        x_all = x_new

    def final(xt, tt, gt):
        r = lax.rsqrt(jnp.mean(xt * xt, axis=-1, keepdims=True) + NORM_EPS)
        xh = xt * r
        e = xh * gt - tt
        dy = e * (1.0 / d_model)
        dyg = dy * gt
        dx = r * (dyg - xh * jnp.mean(dyg * xh, axis=-1, keepdims=True))
        return dx, _rsum(dy * xh), _rsum(e * e)

    dx_lat, d_final_g, loss_cols = _ew(final, [('t', x_all, 0, d_model), ('t', loss_target[0], 0, d_model), ('g', final_g[None, None, :])],
                                       [('t', d_model, F32), ('r', d_model, 1), ('r', d_model, 1)], rows=s_len, tr=tr, name="final")
    loss_part = (0.5 / d_model) * jnp.sum(loss_cols)
    dx_all = jnp.concatenate([dx_lat, jnp.zeros((l_len, d_model), F32)], axis=0)

    big_w = [(w_in, m_w_in, v_w_in), (w_proj_na, m_w_proj_na, v_w_proj_na), (w_proj_ret, m_w_proj_ret, v_w_proj_ret), (w_out, m_w_out, v_w_out)]
    big_res = [None] * 4
    scatters = {}
    back_token = jnp.zeros((), F32)

    def start_scatter(key, grads, axes):
        plan = _scatter_plan(axes)
        half_sz = [g.shape[1 - ax] // 2 for g, ax in zip(grads, axes)]
        mine = [lax.dynamic_slice_in_dim(g, ci * hs, hs, axis=1 - ax) for g, hs, ax in zip(grads, half_sz, axes)]
        to_send = [lax.dynamic_slice_in_dim(g, (1 - ci) * hs, hs, axis=1 - ax) for g, hs, ax in zip(grads, half_sz, axes)]
        theirs = _send_to_sibling(to_send, name="pair_exchange_%s" % key)
        pair = []
        for i in range(len(grads)):
            pr, pw = mine[i].shape
            s, = _ew(lambda a, b: a.astype(F32) + b.astype(F32), [('t', mine[i], 0, pw), ('t', theirs[i], 0, pw)], [('t', pw, BF16)],
                     rows=pr, tr=_rows_per_tile(pr, pw), name="sum_pair_%s_%d" % (key, i))
            pair.append(s)
        own = [lax.dynamic_slice_in_dim(s, chip * (s.shape[ax] // N_CHIPS), s.shape[ax] // N_CHIPS, axis=ax) for s, ax in zip(pair, axes)]
        lands = [lax.empty((N_CHIPS - 1,) + o.shape, BF16) for o in own]
        started = _split_start(pair, lands, plan, len(axes) * (N_CHIPS - 1), name="scatter_start_%s" % key)
        scatters[key] = (started, own, axes, plan)
        return started[4]

    def finish_scatter(key, after):
        started, own, axes, plan = scatters[key]
        recv = _split_wait(started, after, plan, name="scatter_wait_%s" % key)
        parts = []
        for i, rbuf in enumerate(recv):
            pr, pw = own[i].shape
            p, = _ew(lambda a, b, c_, d: ((a.astype(F32) + b.astype(F32)) + c_.astype(F32)) + d.astype(F32),
                     [('t', own[i], 0, pw)] + [('t', rbuf, 0, pw, k) for k in range(N_CHIPS - 1)], [('t', pw, F32)],
                     rows=pr, tr=_rows_per_tile(pr, pw), name="sum_chips_%s_%d" % (key, i))
            parts.append(p)
        others = _send_to_sibling(parts, name="share_halves_%s" % key)
        return [jnp.where(ci == 0, jnp.concatenate([p, o], axis=1 - ax), jnp.concatenate([o, p], axis=1 - ax))
                for p, o, ax in zip(parts, others, axes)]

    def adamw_big(l, idx, grads, big_res):
        for i, g in zip(idx, grads):
            w3, m3, v3 = big_w[i]
            big_res[i] = _adamw_layer(w3, m3, v3, g, None, l, big_res[i], name="adamw_big_%d_%d" % (i, l))
        return big_res

    small = dict(dmod_lat=[None] * depth, dmod_ctx=[None] * depth, dnorm_g=[None] * depth, drpb=[None] * depth, ddecay=[None] * depth)
    for l in reversed(range(depth)):
        sv = saved[l]
        win_f, wpn_f, wpr_f, wout_f = full_w[l]

        def resid_bwd(dxt, ot, gt):
            return gt * dxt, _rsum(dxt * ot)

        dout, dgate = _ew(resid_bwd, [('t', dx_all, 0, d_model), ('t', sv['out'], 0, d_model), ('g', sv['gate'] + back_token)],
                          [('t', d_model, BF16), ('r', d_model, 2)], rows=t_len, tr=tr, n0=n0, name="resid_bwd_%d" % l)
        dmerged = _mm(dout, wout_f, tb=True, name="out_proj_dx_%d" % l)
        g_wout = _mm(sv['merged'], dout, ta=True, tm=512, tk=t_len, name="out_proj_dw_%d" % l)

        def merge_bwd(dm, y1, y2, g1, g2):
            dm = dm.astype(F32)
            s1, s2_ = _sigmoid(g1.astype(F32)), _sigmoid(g2.astype(F32))
            return dm * s1, dm * s2_, dm * y1.astype(F32) * s1 * (1.0 - s1), dm * y2.astype(F32) * s2_ * (1.0 - s2_)

        u = sv['u']
        dy_na, dy_ret, dg_na, dg_ret = _ew(
            merge_bwd, [('t', dmerged, 0, d_model), ('t', sv['y_na'], 0, d_model), ('t', sv['y_ret'], 0, d_model),
                        ('t', u, o_gna // d_model, d_model), ('t', u, o_gret // d_model, d_model)],
            [('t', d_model, BF16)] * 4, rows=t_len, tr=tr, name="merge_bwd_%d" % l)
        da_na = _mm(dy_na, wpn_f, tb=True, name="proj_na_dx_%d" % l)
        g_wpn = _mm(sv['a_na'], dy_na, ta=True, tm=512, tk=t_len, name="proj_na_dw_%d" % l)
        da_ret = _mm(dy_ret, wpr_f, tb=True, name="proj_ret_dx_%d" % l)
        g_wpr = _mm(sv['a_ret'], dy_ret, ta=True, tm=512, tk=t_len, name="proj_ret_dw_%d" % l)
        lg_l = log_gamma[l]
        if l == 0:
            lg_l = lg_l + start_scatter("0_rest", [g_wpn, g_wpr, g_wout], big_axes[1:])[0, 0]

        def act_bwd(da1, o1, z1, da2, o2, z2):
            da1, da2 = da1.astype(F32), da2.astype(F32)
            si1, ds1 = _silu_parts(z1.astype(F32))
            si2, ds2 = _silu_parts(z2.astype(F32))
            do1 = da1 * si1
            dz1 = da1 * o1.astype(F32) * ds1
            dn = da2 * si2
            do2, dz2 = [], []
            for hh in range(ret_heads):
                sl = slice(hh * RET_VAL_DIM, (hh + 1) * RET_VAL_DIM)
                oh = o2[:, sl]
                r = lax.rsqrt(jnp.mean(oh * oh, axis=-1, keepdims=True) + NORM_EPS)
                nh = oh * r
                dz2.append(da2[:, sl] * nh * ds2[:, sl])
                do2.append(r * (dn[:, sl] - nh * jnp.mean(dn[:, sl] * nh, axis=-1, keepdims=True)))
            return do1, dz1, jnp.concatenate(do2, axis=-1), jnp.concatenate(dz2, axis=-1)

        do_na, dz_na, do_ret, dz_ret = _ew(
            act_bwd, [('t', da_na, 0, w_na), ('t', sv['o_na'], 0, w_na), ('t', u, o_naz // w_na, w_na),
                      ('t', da_ret, 0, w_v), ('t', sv['o_ret'], 0, w_v), ('t', u, o_retz // w_v, w_v)],
            [('t', w_na, BF16), ('t', w_na, BF16), ('t', w_v, BF16), ('t', w_v, BF16)], rows=t_len, tr=tr, name="act_bwd_%d" % l)
        dq_na, dk_na, dv_na, dbias = _na_bwd(u, sv['bias'], sv['o_na'], do_na, s_len=s_len, heads=na_heads, name="na_bwd_%d" % l)
        small['drpb'][l] = _rpb_grad(dbias, name="rpb_grad_%d" % l)
        dq_r, dk_r, dv_r, dlg = _ret_bwd(u, c2, s2, lg_l, sv['states'], do_ret, s_len=s_len, heads=ret_heads,
                                         q_off=o_retq, name="ret_bwd_%d" % l)
        small['ddecay'][l] = jnp.transpose(dlg[:, :, 0, 0]) * _sigmoid(-ret_decay_logit[l])
        du = jnp.concatenate([dq_na, dk_na, dv_na, dz_na, dq_r, dk_r, dv_r, dz_ret, dg_na, dg_ret], axis=1)
        dh = _mm(du, win_f, tb=True, out_dtype=F32, tn=1024, name="in_proj_dx_%d" % l)

        def modnorm_bwd(xt, dht, dxt, gs_t):
            r = lax.rsqrt(jnp.mean(xt * xt, axis=-1, keepdims=True) + NORM_EPS)
            xh = xt * r
            dhg = dht * gs_t
            dx = r * (dhg - xh * jnp.mean(dhg * xh, axis=-1, keepdims=True)) + dxt
            return dx, _rsum(dht), _rsum(dht * xh)

        dx_all, dshift, dgs = _ew(modnorm_bwd, [('t', sv['x'], 0, d_model), ('t', dh, 0, d_model), ('t', dx_all, 0, d_model), ('g', sv['gs'])],
                                  [('t', d_model, F32), ('r', d_model, 2), ('r', d_model, 2)], rows=t_len, tr=tr, n0=n0, name="modnorm_bwd_%d" % l)
        dscale = dgs * norm_g[l][None, None, :]
        small['dnorm_g'][l] = jnp.sum(dgs * (1.0 + sv['scale']), axis=(0, 1))
        dmod = jnp.concatenate([dshift, dscale, dgate], axis=-1)[:, 0]
        small['dmod_lat'][l], small['dmod_ctx'][l] = dmod[0], dmod[1]

        if l > 0:
            g_win = _mm(sv['h'], du, ta=True, tm=512, tk=t_len, name="in_proj_dw_%d" % l)
            back_token = start_scatter("%d_all" % l, [g_win, g_wpn, g_wpr, g_wout], big_axes)[0, 0]

    grad_x = dx_all[:s_len][None]

    drpb = jnp.stack(small['drpb']).reshape(-1)
    ddecay = jnp.stack(small['ddecay']).reshape(-1)
    pieces = [jnp.stack(small['dmod_lat']).reshape(-1), jnp.stack(small['dmod_ctx']).reshape(-1),
              jnp.stack(small['dnorm_g']).reshape(-1), d_final_g.reshape(-1), drpb, ddecay, loss_part[None]]
    sizes = [int(p.shape[0]) for p in pieces]
    pads = [-(-s // LANES) * LANES for s in sizes]
    packed = jnp.concatenate([jnp.pad(p, (0, pd - s)) for p, s, pd in zip(pieces, sizes, pads)])
    gathered = _all_gather_small(_pack_rows(packed), name="gather_small_grads")
    r_small = gathered.shape[1]

    def sum8(*t):
        acc = t[0]
        for other in t[1:]:
            acc = acc + other
        return acc

    total, = _ew(sum8, [('t', gathered, 0, LANES, k) for k in range(N_DEV)], [('t', LANES, F32)], rows=r_small, tr=r_small, name="sum_devices")
    total = total.reshape(-1)
    starts = np.cumsum([0] + pads)
    g_mod_lat_sum, g_mod_ctx, g_norm_g, g_final_g, g_rpb, g_decay, loss = [total[starts[i]:starts[i] + sizes[i]] for i in range(len(pieces))]
    loss = loss[0]
    g_ada_b = (g_mod_lat_sum + g_mod_ctx).reshape(depth, mod_cols)
    g_mod_ctx = g_mod_ctx.reshape(depth, mod_cols)
    dmod_lat_all = gathered.reshape(N_DEV, -1)[:, :depth * mod_cols].reshape(N_DEV, depth, mod_cols)

    dcc_part = jnp.zeros((16, d_model), F32)
    ctx_cols = [lax.dynamic_slice_in_dim(g_mod_ctx[l], chip * mod_shard, mod_shard, axis=0) for l in range(depth)]
    for l in reversed(range(depth)):
        c_rows = jnp.concatenate([ctx_cols[l][None], jnp.zeros((15, mod_shard), F32)], axis=0)
        dcc_part = dcc_part + _mm(c_rows, ada_w, tb=True, b_lead=l, out_dtype=F32, name="ada_dc_%d" % l)
    dcc_all = _all_gather_small(_pack_rows(dcc_part[0]), name="gather_dcc")[:, :d_model // LANES].reshape(N_CHIPS, 2, d_model)[:, 0]

    du0, dcc_all = lax.optimization_barrier((du, dcc_all))
    g_win0 = _mm(saved[0]['h'], du0, ta=True, tm=512, tk=t_len, name="in_proj_dw_0")
    tail_token = start_scatter("0_in", [g_win0], big_axes[:1])
    dcc = ((dcc_all[0] + dcc_all[1]) + dcc_all[2]) + dcc_all[3]
    sg = _sigmoid(c_ctx)
    g_c_ctx = dcc * (sg * (1.0 + c_ctx * (1.0 - sg)))
    for l in reversed(range(1, depth)):
        big_res = adamw_big(l, range(4), finish_scatter("%d_all" % l, tail_token), big_res)

    ada_res = None
    for l in reversed(range(depth)):
        lat_cols = lax.dynamic_slice_in_dim(dmod_lat_all[:, l], chip * mod_shard, mod_shard, axis=1)
        d_rows = jnp.concatenate([lat_cols, ctx_cols[l][None], jnp.zeros((16 - N_DEV - 1, mod_shard), F32)], axis=0) + tail_token[0, 0]
        g_ada = _mm(a_rows, d_rows, ta=True, out_dtype=F32, tm=512, name="ada_dw_%d" % l)
        ada_res = _adamw_layer(ada_w, m_ada_w, v_ada_w, g_ada, None, l, ada_res, name="adamw_ada_%d" % l)

    small_w = [(c_ctx, m_c_ctx, v_c_ctx, g_c_ctx), (ada_b, m_ada_b, v_ada_b, g_ada_b),
               (norm_g, m_norm_g, v_norm_g, g_norm_g), (na_rpb, m_na_rpb, v_na_rpb, g_rpb),
               (ret_decay_logit, m_ret_decay_logit, v_ret_decay_logit, g_decay), (final_g, m_final_g, v_final_g, g_final_g)]
    sw_sizes = [int(np.prod(t[0].shape)) for t in small_w]
    sw_pads = [-(-s // LANES) * LANES for s in sw_sizes]

    def pack(j):
        return _pack_rows(jnp.concatenate([jnp.pad(t[j].reshape(-1), (0, pd - s)) for t, s, pd in zip(small_w, sw_sizes, sw_pads)]))

    pw_, pm_, pv_, pg_ = pack(0), pack(1), pack(2), pack(3)
    sw_out = _ew(lambda w, m, v, g: (g,) + _adamw_math(w, g, m, v),
                 [('t', pw_, 0, LANES), ('t', pm_, 0, LANES), ('t', pv_, 0, LANES), ('t', pg_, 0, LANES)],
                 [('t', LANES, F32)] * 4, rows=pw_.shape[0], tr=pw_.shape[0], name="adamw_small")
    sw_starts = np.cumsum([0] + sw_pads)
    sw_out, ada_res, big_res = lax.optimization_barrier((sw_out, ada_res, big_res))
    big_res = adamw_big(0, range(1, 4), finish_scatter("0_rest", sw_out[0]), big_res)
    big_res = adamw_big(0, range(1), finish_scatter("0_in", sw_out[1]), big_res)

    def unpack(arr, i):
        return arr.reshape(-1)[sw_starts[i]:sw_starts[i] + sw_sizes[i]].reshape(small_w[i][0].shape)

    sm = [[unpack(sw_out[j], i) for i in range(len(small_w))] for j in range(4)]
    def ordered(j):
        return [sm[j][0], ada_res[j], sm[j][1], sm[j][2], big_res[0][j], sm[j][3], sm[j][4],
                big_res[1][j], big_res[2][j], big_res[3][j], sm[j][5]]

    return (loss, grad_x, *ordered(0), *ordered(1), *ordered(2), *ordered(3))
```

```python
import functools
import math

import numpy as np
import jax
import jax.numpy as jnp
from jax import lax
from jax.experimental import pallas as pl
from jax.experimental.pallas import tpu as pltpu

GRID_W = 64
NA_HEAD_DIM = 128
NA_WIN_ROWS = 8
NA_WIN_COLS = 16
NA_GROUP = 4
RET_GROUPS = (1, 2, 3)
RET_KEY_DIM = 128
RET_VAL_DIM = 256
RET_CHUNK = 128
ROPE_BASE = 10000.0
NORM_EPS = 1e-6
MASK_VALUE = -1e30
ADAM_LR = 0.001
ADAM_B1 = 0.9
ADAM_B2 = 0.999
ADAM_EPS = 1e-08
ADAM_WD = 0.01
ADAM_STEP = 10

N_CHIPS = 4
N_DEV = 8
LANES = 128
VMEM_LIMIT = 56 * 1024 * 1024
BF16 = jnp.bfloat16
F32 = jnp.float32
MESH = pl.DeviceIdType.MESH
ANY = pl.BlockSpec(memory_space=pl.ANY)


def _tile(dim, pref, align=LANES):
    if dim <= pref:
        return dim
    t = (pref // align) * align
    while t >= align:
        if dim % t == 0:
            return t
        t -= align
    return dim


def _rows_per_tile(rows, width, tile_bytes=1 << 20):
    return _tile(rows, max(8, tile_bytes // (4 * width)), 8)


def _params(sem):
    return pltpu.CompilerParams(dimension_semantics=sem, vmem_limit_bytes=VMEM_LIMIT)


def _sigmoid(x):
    return 1.0 / (1.0 + jnp.exp(-x))


def _dot(a, b, ca, cb):
    return lax.dot_general(a, b, (((ca,), (cb,)), ((), ())), preferred_element_type=F32)


def _mm(a, b, *, ta=False, tb=False, a_lead=None, b_lead=None, out_dtype=BF16, tm=768, tn=512, tk=2048, name):
    ash = a.shape[1:] if a_lead is not None else a.shape
    bsh = b.shape[1:] if b_lead is not None else b.shape
    m, k = (ash[1], ash[0]) if ta else ash
    n, k2 = bsh if tb else (bsh[1], bsh[0])
    assert k == k2, (name, ash, bsh)
    tm, tn, tk = _tile(m, tm), _tile(n, tn), _tile(k, tk)
    nk = k // tk

    def lead(spec_shape, imap, l):
        if l is None:
            return pl.BlockSpec(spec_shape, imap)
        return pl.BlockSpec((None,) + spec_shape, lambda i, j, kk: (l,) + imap(i, j, kk))

    a_spec = lead((tk, tm), lambda i, j, kk: (kk, i), a_lead) if ta else lead((tm, tk), lambda i, j, kk: (i, kk), a_lead)
    b_spec = lead((tn, tk), lambda i, j, kk: (j, kk), b_lead) if tb else lead((tk, tn), lambda i, j, kk: (kk, j), b_lead)
    ca, cb = (0 if ta else 1), (1 if tb else 0)

    def body(a_ref, b_ref, o_ref, *scratch):
        part = _dot(a_ref[...].astype(BF16), b_ref[...].astype(BF16), ca, cb)
        if nk == 1:
            o_ref[...] = part.astype(o_ref.dtype)
            return
        acc_ref, = scratch
        kk = pl.program_id(2)

        @pl.when(kk == 0)
        def _():
            acc_ref[...] = part

        @pl.when(kk > 0)
        def _():
            acc_ref[...] += part

        @pl.when(kk == nk - 1)
        def _():
            o_ref[...] = acc_ref[...].astype(o_ref.dtype)

    return pl.pallas_call(
        body, name=name, grid=(m // tm, n // tn, nk),
        in_specs=[a_spec, b_spec],
        out_specs=pl.BlockSpec((tm, tn), lambda i, j, kk: (i, j)),
        out_shape=jax.ShapeDtypeStruct((m, n), out_dtype),
        scratch_shapes=[] if nk == 1 else [pltpu.VMEM((tm, tn), F32)],
        compiler_params=_params(("parallel", "parallel", "arbitrary")),
    )(a, b)


def _ew(fn, ins, outs, *, rows, tr, name, n0=None, aliases=None):
    assert rows % tr == 0, (name, rows, tr)
    nt = rows // tr

    def grp(i):
        return 0 if n0 is None else jnp.where(i < n0, 0, 1)

    in_specs, args = [], []
    for spec in ins:
        if spec[0] == 't':
            arr, cb, w = spec[1], spec[2], spec[3]
            l = spec[4] if len(spec) > 4 else None
            if l is None:
                in_specs.append(pl.BlockSpec((tr, w), functools.partial(lambda i, cb: (i, cb), cb=cb)))
            else:
                in_specs.append(pl.BlockSpec((None, tr, w), functools.partial(lambda i, cb, l: (l, i, cb), cb=cb, l=l)))
            args.append(arr)
        else:
            arr = spec[1]
            g = arr.shape[0]
            if g == 1:
                in_specs.append(pl.BlockSpec((None, 1, arr.shape[2]), lambda i: (0, 0, 0)))
            else:
                in_specs.append(pl.BlockSpec((None, 1, arr.shape[2]), lambda i: (grp(i), 0, 0)))
            args.append(arr)
    out_specs, out_shapes, is_red = [], [], []
    for spec in outs:
        if spec[0] == 't':
            w, dt = spec[1], spec[2]
            if len(spec) > 3:
                l, nl = spec[3], spec[4]
                out_specs.append(pl.BlockSpec((None, tr, w), functools.partial(lambda i, l: (l, i, 0), l=l)))
                out_shapes.append(jax.ShapeDtypeStruct((nl, rows, w), dt))
            else:
                out_specs.append(pl.BlockSpec((tr, w), lambda i: (i, 0)))
                out_shapes.append(jax.ShapeDtypeStruct((rows, w), dt))
            is_red.append(False)
        else:
            w, g = spec[1], spec[2]
            if g == 1:
                out_specs.append(pl.BlockSpec((None, 1, w), lambda i: (0, 0, 0)))
            else:
                out_specs.append(pl.BlockSpec((None, 1, w), lambda i: (grp(i), 0, 0)))
            out_shapes.append(jax.ShapeDtypeStruct((g, 1, w), F32))
            is_red.append(True)
    n_in = len(ins)
    n_alias = 0 if aliases is None else len(aliases)

    def body(*refs):
        in_refs = refs[:n_in]
        out_refs = refs[n_in + n_alias:]
        res = fn(*[r[...] for r in in_refs])
        if not isinstance(res, (tuple, list)):
            res = (res,)
        i = pl.program_id(0)
        first = (i == 0) if n0 is None else ((i == 0) | (i == n0))
        for o_ref, val, red in zip(out_refs, res, is_red):
            if not red:
                o_ref[...] = val.astype(o_ref.dtype)
            else:
                @pl.when(first)
                def _(o_ref=o_ref, val=val):
                    o_ref[...] = val

                @pl.when(jnp.logical_not(first))
                def _(o_ref=o_ref, val=val):
                    o_ref[...] += val

    io_alias = {}
    if aliases is not None:
        for a_idx, (arr, o_idx) in enumerate(aliases):
            in_specs.append(ANY)
            args.append(arr)
            io_alias[n_in + a_idx] = o_idx
    has_red = any(is_red)
    return pl.pallas_call(
        body, name=name, grid=(nt,), in_specs=in_specs, out_specs=out_specs, out_shape=out_shapes,
        input_output_aliases=io_alias,
        compiler_params=_params(("arbitrary",) if has_red else ("parallel",)),
    )(*args)


def _rsum(v):
    return jnp.sum(v, axis=0, keepdims=True)


def _silu_parts(z):
    sg = _sigmoid(z)
    return z * sg, sg * (1.0 + z * (1.0 - sg))


def _na_bias_table(rpb, rows, *, name):
    kh, kw = NA_WIN_ROWS, NA_WIN_COLS
    assert rows >= kh
    heads = rpb.shape[0]
    e1, e2 = _na_onehots()
    rpb16 = jnp.pad(rpb, ((0, 0), (0, 16 - rpb.shape[1]), (0, LANES - rpb.shape[2])))

    def body(r_ref, e1_ref, e2_ref, o_ref):
        e1b = e1_ref[...].astype(BF16)
        y = sum(_dot(e1b, part, 0, 0) for part in _split3(r_ref[...]))
        e2b = e2_ref[...].astype(BF16)
        o_ref[...] = sum(_dot(part, e2b, 1, 1) for part in _split3(y))

    z = pl.pallas_call(
        body, name=name, grid=(heads,),
        in_specs=[pl.BlockSpec((None, 16, LANES), lambda h: (h, 0, 0)),
                  pl.BlockSpec(e1.shape, lambda h: (0, 0)), pl.BlockSpec(e2.shape, lambda h: (0, 0))],
        out_specs=pl.BlockSpec((None, kh * kh, GRID_W * GRID_W), lambda h: (h, 0, 0)),
        out_shape=jax.ShapeDtypeStruct((heads, kh * kh, GRID_W * GRID_W), F32),
        compiler_params=_params(("parallel",)),
    )(rpb16, e1, e2)
    cidx = np.arange(GRID_W)
    c0 = np.clip(cidx - kw // 2, 0, GRID_W - kw)
    col_in = (cidx[None, :] >= c0[:, None]) & (cidx[None, :] < c0[:, None] + kw)
    bias = z.reshape(heads, kh, kh, GRID_W, GRID_W).transpose(0, 1, 3, 2, 4)
    bias = jnp.where(col_in[None, None, :, None, :], bias, MASK_VALUE)
    return bias.reshape(heads, kh, GRID_W, kh * GRID_W)


def _na_onehots():
    kh, kw = NA_WIN_ROWS, NA_WIN_COLS
    cidx = np.arange(GRID_W)
    dc = cidx[None, :] - cidx[:, None] + (kw - 1)
    e2 = np.zeros((GRID_W * GRID_W, LANES), np.float32)
    ok = (dc >= 0) & (dc <= 2 * kw - 2)
    cq, ck = np.nonzero(ok)
    e2[cq * GRID_W + ck, dc[cq, ck]] = 1.0
    dr = np.arange(kh)[None, :] - np.arange(kh)[:, None] + (kh - 1)
    e1 = np.zeros((16, kh * kh), np.float32)
    dl, kr = np.nonzero(np.ones_like(dr))
    e1[dr[dl, kr], dl * kh + kr] = 1.0
    return jnp.asarray(e1), jnp.asarray(e2)


def _na_fwd(u, bias, *, s_len, heads, name):
    t_len = u.shape[0]
    rows = s_len // GRID_W
    nloc = NA_WIN_ROWS * GRID_W
    scale = NA_HEAD_DIM ** -0.5
    hd = NA_HEAD_DIM

    def body(q_ref, k_ref, v_ref, b_ref, o_ref):
        kc = k_ref[s_len:t_len, :]
        vc = v_ref[s_len:t_len, :]

        def group(g, carry):
            rs = [g * NA_GROUP + i for i in range(NA_GROUP)]
            r0s = [jnp.clip(r - NA_WIN_ROWS // 2, 0, rows - NA_WIN_ROWS) for r in rs]
            gs_ = pl.multiple_of(g * (NA_GROUP * GRID_W), NA_GROUP * GRID_W)
            kss = [pl.multiple_of(r0 * GRID_W, GRID_W) for r0 in r0s]
            q_all = q_ref[pl.ds(gs_, NA_GROUP * GRID_W), :]
            s_ctx = _dot(q_all, kc, 1, 1) * scale
            s_loc = [_dot(q_all[i * GRID_W:(i + 1) * GRID_W], k_ref[pl.ds(kss[i], nloc), :], 1, 1) * scale + b_ref[rs[i] - r0s[i]]
                     for i in range(NA_GROUP)]
            p_loc, p_ctx, inv = [], [], []
            for i in range(NA_GROUP):
                sc = s_ctx[i * GRID_W:(i + 1) * GRID_W]
                m = jnp.maximum(jnp.max(s_loc[i], axis=-1, keepdims=True), jnp.max(sc, axis=-1, keepdims=True))
                pl_, pc_ = jnp.exp(s_loc[i] - m), jnp.exp(sc - m)
                inv.append(1.0 / (jnp.sum(pl_, axis=-1, keepdims=True) + jnp.sum(pc_, axis=-1, keepdims=True)))
                p_loc.append(pl_.astype(BF16))
                p_ctx.append(pc_.astype(BF16))
            o_ctx = _dot(jnp.concatenate(p_ctx, axis=0), vc, 1, 0)
            o_loc = [_dot(p_loc[i], v_ref[pl.ds(kss[i], nloc), :], 1, 0) for i in range(NA_GROUP)]
            out = jnp.concatenate([(o_loc[i] + o_ctx[i * GRID_W:(i + 1) * GRID_W]) * inv[i] for i in range(NA_GROUP)], axis=0)
            o_ref[pl.ds(gs_, NA_GROUP * GRID_W), :] = out.astype(o_ref.dtype)
            return carry

        lax.fori_loop(0, rows // NA_GROUP, group, 0)
        qc = q_ref[s_len:t_len, :]
        s = _dot(qc, kc, 1, 1) * scale
        p = jnp.exp(s - jnp.max(s, axis=-1, keepdims=True))
        o = _dot(p.astype(BF16), vc, 1, 0) / jnp.sum(p, axis=-1, keepdims=True)
        o_ref[s_len:t_len, :] = o.astype(o_ref.dtype)

    col = lambda off: pl.BlockSpec((t_len, hd), functools.partial(lambda h, off: (0, off + h), off=off))
    return pl.pallas_call(
        body, name=name, grid=(heads,),
        in_specs=[col(0), col(heads), col(2 * heads),
                  pl.BlockSpec((None, NA_WIN_ROWS, GRID_W, nloc), lambda h: (h, 0, 0, 0))],
        out_specs=pl.BlockSpec((t_len, hd), lambda h: (0, h)),
        out_shape=jax.ShapeDtypeStruct((t_len, heads * hd), BF16),
        compiler_params=_params(("parallel",)),
    )(u, u, u, bias)


def _na_bwd(u, bias, o, do, *, s_len, heads, name):
    t_len = u.shape[0]
    rows = s_len // GRID_W
    nloc = NA_WIN_ROWS * GRID_W
    scale = NA_HEAD_DIM ** -0.5
    hd = NA_HEAD_DIM

    def body(q_ref, k_ref, v_ref, b_ref, o_ref, do_ref, dq_ref, dk_ref, dv_ref, db_ref, dk_acc, dv_acc):
        kc = k_ref[s_len:t_len, :]
        vc = v_ref[s_len:t_len, :]
        dk_acc[...] = jnp.zeros_like(dk_acc)
        dv_acc[...] = jnp.zeros_like(dv_acc)
        db_ref[...] = jnp.zeros_like(db_ref)

        def group(g, carry):
            n_g, rw = NA_GROUP, GRID_W
            rs = [g * n_g + i for i in range(n_g)]
            r0s = [jnp.clip(r - NA_WIN_ROWS // 2, 0, rows - NA_WIN_ROWS) for r in rs]
            dls = [r - r0 for r, r0 in zip(rs, r0s)]
            gs_ = pl.ds(pl.multiple_of(g * (n_g * rw), n_g * rw), n_g * rw)
            kss = [pl.ds(pl.multiple_of(r0 * rw, rw), nloc) for r0 in r0s]
            row_of = lambda a, i: a[i * rw:(i + 1) * rw]
            q_all, do_all = q_ref[gs_, :], do_ref[gs_, :]
            dlt_all = jnp.sum(do_all.astype(F32) * o_ref[gs_, :].astype(F32), axis=-1, keepdims=True)
            s_ctx = _dot(q_all, kc, 1, 1) * scale
            dp_ctx = _dot(do_all, vc, 1, 1)
            s_loc = [_dot(row_of(q_all, i), k_ref[kss[i], :], 1, 1) * scale + b_ref[dls[i]] for i in range(n_g)]
            dp_loc = [_dot(row_of(do_all, i), v_ref[kss[i], :], 1, 1) for i in range(n_g)]
            p_loc_b, ds_loc_b, p_ctx_b, ds_ctx_b = [], [], [], []
            for i in range(n_g):
                sc, dlt = row_of(s_ctx, i), row_of(dlt_all, i)
                m = jnp.maximum(jnp.max(s_loc[i], axis=-1, keepdims=True), jnp.max(sc, axis=-1, keepdims=True))
                pl_, pc_ = jnp.exp(s_loc[i] - m), jnp.exp(sc - m)
                inv = 1.0 / (jnp.sum(pl_, axis=-1, keepdims=True) + jnp.sum(pc_, axis=-1, keepdims=True))
                pl_, pc_ = pl_ * inv, pc_ * inv
                ds_l = pl_ * (dp_loc[i] - dlt)
                db_ref[dls[i]] += ds_l
                p_loc_b.append(pl_.astype(BF16))
                ds_loc_b.append(ds_l.astype(BF16))
                p_ctx_b.append(pc_.astype(BF16))
                ds_ctx_b.append((pc_ * (row_of(dp_ctx, i) - dlt)).astype(BF16))
            p_ctx_all, ds_ctx_all = jnp.concatenate(p_ctx_b, axis=0), jnp.concatenate(ds_ctx_b, axis=0)
            dq_ctx = _dot(ds_ctx_all, kc, 1, 0)
            dq_loc = [_dot(ds_loc_b[i], k_ref[kss[i], :], 1, 0) for i in range(n_g)]
            dk_loc = [_dot(ds_loc_b[i], row_of(q_all, i), 0, 0) for i in range(n_g)]
            dv_loc = [_dot(p_loc_b[i], row_of(do_all, i), 0, 0) for i in range(n_g)]
            dk_ctx = _dot(ds_ctx_all, q_all, 0, 0)
            dv_ctx = _dot(p_ctx_all, do_all, 0, 0)
            dq_ref[gs_, :] = ((jnp.concatenate(dq_loc, axis=0) + dq_ctx) * scale).astype(dq_ref.dtype)
            for i in range(n_g):
                dk_acc[kss[i], :] += dk_loc[i] * scale
                dv_acc[kss[i], :] += dv_loc[i]
            dk_acc[s_len:t_len, :] += dk_ctx * scale
            dv_acc[s_len:t_len, :] += dv_ctx
            return carry

        lax.fori_loop(0, rows // NA_GROUP, group, 0)
        qc = q_ref[s_len:t_len, :]
        dout = do_ref[s_len:t_len, :]
        out = o_ref[s_len:t_len, :]
        s = _dot(qc, kc, 1, 1) * scale
        p = jnp.exp(s - jnp.max(s, axis=-1, keepdims=True))
        p = p / jnp.sum(p, axis=-1, keepdims=True)
        dlt = jnp.sum(dout.astype(F32) * out.astype(F32), axis=-1, keepdims=True)
        ds = (p * (_dot(dout, vc, 1, 1) - dlt)).astype(BF16)
        dq_ref[s_len:t_len, :] = (_dot(ds, kc, 1, 0) * scale).astype(dq_ref.dtype)
        dk_acc[s_len:t_len, :] += _dot(ds, qc, 0, 0) * scale
        dv_acc[s_len:t_len, :] += _dot(p.astype(BF16), dout, 0, 0)
        dk_ref[...] = dk_acc[...].astype(dk_ref.dtype)
        dv_ref[...] = dv_acc[...].astype(dv_ref.dtype)

    col = lambda off: pl.BlockSpec((t_len, hd), functools.partial(lambda h, off: (0, off + h), off=off))
    tbl = pl.BlockSpec((None, NA_WIN_ROWS, GRID_W, nloc), lambda h: (h, 0, 0, 0))
    tok = jax.ShapeDtypeStruct((t_len, heads * hd), BF16)
    return pl.pallas_call(
        body, name=name, grid=(heads,),
        in_specs=[col(0), col(heads), col(2 * heads), tbl, col(0), col(0)],
        out_specs=[col(0), col(0), col(0), tbl],
        out_shape=[tok, tok, tok, jax.ShapeDtypeStruct(bias.shape, F32)],
        scratch_shapes=[pltpu.VMEM((t_len, hd), F32), pltpu.VMEM((t_len, hd), F32)],
        compiler_params=_params(("parallel",)),
    )(u, u, u, bias, o, do)


def _split3(x):
    hi = x.astype(BF16)
    r1 = x - hi.astype(F32)
    mid = r1.astype(BF16)
    lo = (r1 - mid.astype(F32)).astype(BF16)
    return hi, mid, lo


def _rpb_grad(dbias, *, name):
    heads = dbias.shape[0]
    kh = NA_WIN_ROWS
    e1, e2 = _na_onehots()
    x = dbias.reshape(heads, kh, GRID_W, kh, GRID_W).transpose(0, 1, 3, 2, 4).reshape(heads, kh * kh, GRID_W * GRID_W)

    def body(x_ref, e1_ref, e2_ref, o_ref):
        e2b = e2_ref[...].astype(BF16)
        y = sum(_dot(part, e2b, 1, 0) for part in _split3(x_ref[...]))
        e1b = e1_ref[...].astype(BF16)
        o_ref[...] = sum(_dot(e1b, part, 1, 0) for part in _split3(y))

    out = pl.pallas_call(
        body, name=name, grid=(heads,),
        in_specs=[pl.BlockSpec((None, kh * kh, GRID_W * GRID_W), lambda h: (h, 0, 0)),
                  pl.BlockSpec(e1.shape, lambda h: (0, 0)), pl.BlockSpec(e2.shape, lambda h: (0, 0))],
        out_specs=pl.BlockSpec((None, 16, LANES), lambda h: (h, 0, 0)),
        out_shape=jax.ShapeDtypeStruct((heads, 16, LANES), F32),
        compiler_params=_params(("parallel",)),
    )(x, e1, e2)
    return out[:, :2 * kh - 1, :2 * NA_WIN_COLS - 1]


def _rope_tables(s_len, l_len):
    nf = RET_KEY_DIM // 4
    t = np.arange(s_len)
    row = (t // GRID_W).astype(np.float32)
    colp = (t % GRID_W).astype(np.float32)
    inv_freq = jnp.asarray(ROPE_BASE, F32) ** (-jnp.arange(nf, dtype=F32) / nf)
    ang = jnp.concatenate([jnp.asarray(row)[:, None] * inv_freq, jnp.asarray(colp)[:, None] * inv_freq], axis=-1)
    cos, sin = jnp.cos(ang), jnp.sin(ang)
    c2 = jnp.concatenate([cos, cos], axis=-1)
    s2 = jnp.concatenate([-sin, sin], axis=-1)
    c2 = jnp.concatenate([c2, jnp.ones((l_len, RET_KEY_DIM), F32)], axis=0)
    s2 = jnp.concatenate([s2, jnp.zeros((l_len, RET_KEY_DIM), F32)], axis=0)
    return c2, s2


def _rope(x, c2, s2):
    return x * c2 + pltpu.roll(x, RET_KEY_DIM // 2, 1) * s2


def _rope_t(d, c2, s2):
    return d * c2 + pltpu.roll(d * s2, RET_KEY_DIM // 2, 1)


def _ret_decays(lg, direction):
    cs = RET_CHUNK
    i_col = lax.broadcasted_iota(jnp.int32, (cs, 1), 0)
    p_col = jnp.where(direction == 0, i_col, cs - 1 - i_col).astype(F32)
    pi = lax.broadcasted_iota(jnp.int32, (cs, cs), 0)
    pj = lax.broadcasted_iota(jnp.int32, (cs, cs), 1)
    diff = jnp.where(direction == 0, pi - pj, pj - pi).astype(F32)
    dm = jnp.where(diff >= 0, jnp.exp(jnp.maximum(diff, 0.0) * lg), 0.0)
    qdec = jnp.exp((p_col + 1.0) * lg)
    kdec = jnp.exp((cs - 1.0 - p_col) * lg)
    cd = jnp.exp(jnp.full((1, 1), cs, F32) * lg)
    return p_col, dm, qdec, kdec, cd


def _ret_chunk_index(t, direction, n_chunks, lat_chunks):
    return jnp.where(direction == 0, lax.rem(t + lat_chunks, n_chunks), n_chunks - 1 - t)


def _ret_fwd(u, c2, s2, lg, *, s_len, heads, q_off, name):
    t_len = u.shape[0]
    cs, dk, dv = RET_CHUNK, RET_KEY_DIM, RET_VAL_DIM
    n_chunks, lat_chunks = t_len // cs, s_len // cs
    k_scale = dk ** -0.5
    qb, kb, vb = q_off // dk, q_off // dk + heads, (q_off + 2 * heads * dk) // dv

    def body(lg_ref, q_ref, k_ref, v_ref, c_ref, s_ref, o_ref, st_ref, qd_s, kv_s):
        h, d = pl.program_id(0), pl.program_id(1)
        _, dm, qdec, kdec, cd = _ret_decays(lg_ref[d, h], d)
        n_g = max(g for g in RET_GROUPS if n_chunks % g == 0)
        rows_of = lambda c: pl.ds(pl.multiple_of(c * cs, cs), cs)

        def local(gi, carry):
            rws = [rows_of(gi * n_g + j) for j in range(n_g)]
            qcs = [_rope(q_ref[r, :].astype(F32), c_ref[r, :], s_ref[r, :]) for r in rws]
            kcs = [_rope(k_ref[r, :].astype(F32), c_ref[r, :], s_ref[r, :]) * k_scale for r in rws]
            vcs = [v_ref[r, :] for r in rws]
            a_raw = [_dot(qcs[j].astype(BF16), kcs[j].astype(BF16), 1, 1) for j in range(n_g)]
            kv = [_dot((kcs[j] * kdec).astype(BF16), vcs[j], 0, 0) for j in range(n_g)]
            inner = [_dot((a_raw[j] * dm).astype(BF16), vcs[j], 1, 0) for j in range(n_g)]
            for j in range(n_g):
                qd_s[rws[j], :] = (qcs[j] * qdec).astype(BF16)
                kv_s[gi * n_g + j] = kv[j]

            @pl.when(d == 0)
            def _():
                for j in range(n_g):
                    o_ref[rws[j], :] = inner[j]

            @pl.when(d == 1)
            def _():
                for j in range(n_g):
                    o_ref[rws[j], :] += inner[j]

            return carry

        lax.fori_loop(0, n_chunks // n_g, local, 0)

        def scan(t, st):
            st_ref[t] = st
            return st * cd + kv_s[_ret_chunk_index(t, d, n_chunks, lat_chunks)]

        lax.fori_loop(0, n_chunks, scan, jnp.zeros((dk, dv), F32))

        def cross(gi, carry):
            ts = [gi * n_g + j for j in range(n_g)]
            rws = [rows_of(_ret_chunk_index(t, d, n_chunks, lat_chunks)) for t in ts]
            outs = [_dot(qd_s[rws[j], :], st_ref[ts[j]].astype(BF16), 1, 0) for j in range(n_g)]
            for j in range(n_g):
                o_ref[rws[j], :] += outs[j]
            return carry

        lax.fori_loop(0, n_chunks // n_g, cross, 0)

    return pl.pallas_call(
        body, name=name, grid=(heads, 2),
        in_specs=[pl.BlockSpec(memory_space=pltpu.SMEM),
                  pl.BlockSpec((t_len, dk), lambda h, d: (0, qb + h)),
                  pl.BlockSpec((t_len, dk), lambda h, d: (0, kb + h)),
                  pl.BlockSpec((t_len, dv), lambda h, d: (0, vb + h)),
                  pl.BlockSpec((t_len, dk), lambda h, d: (0, 0)),
                  pl.BlockSpec((t_len, dk), lambda h, d: (0, 0))],
        out_specs=[pl.BlockSpec((t_len, dv), lambda h, d: (0, h)),
                   pl.BlockSpec((None, None, n_chunks, dk, dv), lambda h, d: (h, d, 0, 0, 0))],
        out_shape=[jax.ShapeDtypeStruct((t_len, heads * dv), F32),
                   jax.ShapeDtypeStruct((heads, 2, n_chunks, dk, dv), F32)],
        scratch_shapes=[pltpu.VMEM((t_len, dk), BF16), pltpu.VMEM((n_chunks, dk, dv), F32)],
        compiler_params=_params(("parallel", "arbitrary")),
    )(lg, u, u, u, c2, s2)


def _ret_bwd(u, c2, s2, lg, states, do, *, s_len, heads, q_off, name):
    t_len = u.shape[0]
    cs, dk, dv = RET_CHUNK, RET_KEY_DIM, RET_VAL_DIM
    n_chunks, lat_chunks = t_len // cs, s_len // cs
    k_scale = dk ** -0.5
    qb, kb, vb = q_off // dk, q_off // dk + heads, (q_off + 2 * heads * dk) // dv

    def body(lg_ref, q_ref, k_ref, v_ref, c_ref, s_ref, st_ref, do_ref, dq_ref, dk_ref, dv_ref, dlg_ref, acc, qdo_s, dst_s):
        h, d = pl.program_id(0), pl.program_id(1)
        p_col, dm, qdec, kdec, cd = _ret_decays(lg_ref[d, h], d)
        acc[...] = jnp.zeros_like(acc)
        n_g = max(g for g in RET_GROUPS[:2] if n_chunks % g == 0)
        rows_of = lambda c: pl.ds(pl.multiple_of(c * cs, cs), cs)
        chunk_of = lambda t: _ret_chunk_index(t, d, n_chunks, lat_chunks)

        def local(gi, carry):
            rws = [rows_of(gi * n_g + j) for j in range(n_g)]
            qds = [(_rope(q_ref[r, :].astype(F32), c_ref[r, :], s_ref[r, :]) * qdec).astype(BF16) for r in rws]
            prods = [_dot(qds[j], do_ref[rws[j], :].astype(BF16), 0, 0) for j in range(n_g)]
            for j in range(n_g):
                qdo_s[gi * n_g + j] = prods[j]
            return carry

        lax.fori_loop(0, n_chunks // n_g, local, 0)

        def scan(i, dst):
            t = n_chunks - 1 - i
            dst_s[t] = dst
            return dst * cd + qdo_s[chunk_of(t)]

        lax.fori_loop(0, n_chunks, scan, jnp.zeros((dk, dv), F32))

        def grads(gi, carry):
            ts = [gi * n_g + j for j in range(n_g)]
            rws = [rows_of(chunk_of(t)) for t in ts]
            ccs, sss = [c_ref[r, :] for r in rws], [s_ref[r, :] for r in rws]
            qcs = [_rope(q_ref[r, :].astype(F32), cc, ss) for r, cc, ss in zip(rws, ccs, sss)]
            kcs = [_rope(k_ref[r, :].astype(F32), cc, ss) * k_scale for r, cc, ss in zip(rws, ccs, sss)]
            vcs = [v_ref[r, :] for r in rws]
            docs = [do_ref[r, :].astype(BF16) for r in rws]
            sts = [st_ref[t] for t in ts]
            dsts = [dst_s[t] for t in ts]
            q16 = [x.astype(BF16) for x in qcs]
            k16 = [x.astype(BF16) for x in kcs]
            dst16 = [x.astype(BF16) for x in dsts]
            rng = range(n_g)
            a_raw = [_dot(q16[j], k16[j], 1, 1) for j in rng]
            da_raw = [_dot(docs[j], vcs[j], 1, 1) for j in rng]
            dq_c = [_dot(docs[j], sts[j].astype(BF16), 1, 1) * qdec for j in rng]
            dv_s = [_dot((kcs[j] * kdec).astype(BF16), dst16[j], 1, 0) for j in rng]
            dk_s = [_dot(vcs[j], dst16[j], 1, 1) * kdec for j in rng]
            a16 = [(a_raw[j] * dm).astype(BF16) for j in rng]
            dam = [(da_raw[j] * dm).astype(BF16) for j in rng]
            dq_i = [_dot(dam[j], k16[j], 1, 0) for j in rng]
            dk_i = [_dot(dam[j], q16[j], 0, 0) for j in rng]
            dv_i = [_dot(a16[j], docs[j], 0, 0) for j in rng]
            for j in rng:
                g = (jnp.sum(qcs[j] * (p_col * dq_i[j] + (p_col + 1.0) * dq_c[j]), axis=-1, keepdims=True)
                     + jnp.sum(kcs[j] * ((cs - 1.0 - p_col) * dk_s[j] - p_col * dk_i[j]), axis=-1, keepdims=True))
                g = (jnp.sum(g, axis=0, keepdims=True)
                     + cs * cd * jnp.sum(jnp.sum(dsts[j] * sts[j], axis=-1, keepdims=True), axis=0, keepdims=True))
                acc[...] += jnp.broadcast_to(g, acc.shape)
            dqs = [_rope_t(dq_i[j] + dq_c[j], ccs[j], sss[j]) for j in rng]
            dks = [_rope_t((dk_i[j] + dk_s[j]) * k_scale, ccs[j], sss[j]) for j in rng]
            dvs = [dv_i[j] + dv_s[j] for j in rng]

            @pl.when(d == 0)
            def _():
                for j in rng:
                    dq_ref[rws[j], :] = dqs[j].astype(dq_ref.dtype)
                    dk_ref[rws[j], :] = dks[j].astype(dk_ref.dtype)
                    dv_ref[rws[j], :] = dvs[j].astype(dv_ref.dtype)

            @pl.when(d == 1)
            def _():
                for j in rng:
                    dq_ref[rws[j], :] = (dq_ref[rws[j], :].astype(F32) + dqs[j]).astype(dq_ref.dtype)
                    dk_ref[rws[j], :] = (dk_ref[rws[j], :].astype(F32) + dks[j]).astype(dk_ref.dtype)
                    dv_ref[rws[j], :] = (dv_ref[rws[j], :].astype(F32) + dvs[j]).astype(dv_ref.dtype)

            return carry

        lax.fori_loop(0, n_chunks // n_g, grads, 0)
        dlg_ref[...] = acc[...]

    return pl.pallas_call(
        body, name=name, grid=(heads, 2),
        in_specs=[pl.BlockSpec(memory_space=pltpu.SMEM),
                  pl.BlockSpec((t_len, dk), lambda h, d: (0, qb + h)),
                  pl.BlockSpec((t_len, dk), lambda h, d: (0, kb + h)),
                  pl.BlockSpec((t_len, dv), lambda h, d: (0, vb + h)),
                  pl.BlockSpec((t_len, dk), lambda h, d: (0, 0)),
                  pl.BlockSpec((t_len, dk), lambda h, d: (0, 0)),
                  pl.BlockSpec((None, None, n_chunks, dk, dv), lambda h, d: (h, d, 0, 0, 0)),
                  pl.BlockSpec((t_len, dv), lambda h, d: (0, h))],
        out_specs=[pl.BlockSpec((t_len, dk), lambda h, d: (0, h)),
                   pl.BlockSpec((t_len, dk), lambda h, d: (0, h)),
                   pl.BlockSpec((t_len, dv), lambda h, d: (0, h)),
                   pl.BlockSpec((None, None, 8, LANES), lambda h, d: (h, d, 0, 0))],
        out_shape=[jax.ShapeDtypeStruct((t_len, heads * dk), BF16),
                   jax.ShapeDtypeStruct((t_len, heads * dk), BF16),
                   jax.ShapeDtypeStruct((t_len, heads * dv), BF16),
                   jax.ShapeDtypeStruct((heads, 2, 8, LANES), F32)],
        scratch_shapes=[pltpu.VMEM((8, LANES), F32), pltpu.VMEM((n_chunks, dk, dv), F32), pltpu.VMEM((n_chunks, dk, dv), F32)],
        compiler_params=_params(("parallel", "arbitrary")),
    )(lg, u, u, u, c2, s2, states, do)


def _mesh_pos():
    return lax.axis_index("x"), lax.axis_index("y"), lax.axis_index("c")


def _all_gather_small(buf, *, name):
    r = buf.shape[0]

    def body(x_ref, o_ref, send_sems, recv_sems, local_sem):
        x, y, c = _mesh_pos()
        me = 4 * x + 2 * y + c
        mine = pltpu.make_async_copy(x_ref, o_ref.at[me], local_sem)
        mine.start()
        copies = []
        for k in range(1, N_DEV):
            px, py, pc = x ^ ((k >> 2) & 1), y ^ ((k >> 1) & 1), c ^ (k & 1)
            cp = pltpu.make_async_remote_copy(
                src_ref=x_ref, dst_ref=o_ref.at[me], send_sem=send_sems.at[k - 1], recv_sem=recv_sems.at[k - 1],
                device_id=(px, py, pc), device_id_type=MESH)
            cp.start()
            copies.append((cp, 4 * px + 2 * py + pc))
        for k, (cp, peer) in enumerate(copies):
            pltpu.make_async_remote_copy(
                src_ref=x_ref, dst_ref=o_ref.at[peer], send_sem=send_sems.at[k], recv_sem=recv_sems.at[k],
                device_id=(x, y, c), device_id_type=MESH).wait_recv()
        for cp, _ in copies:
            cp.wait_send()
        mine.wait()

    return pl.pallas_call(
        body, name=name,
        in_specs=[pl.BlockSpec(memory_space=pltpu.VMEM)],
        out_specs=pl.BlockSpec(memory_space=pltpu.VMEM),
        out_shape=jax.ShapeDtypeStruct((N_DEV, r, LANES), F32),
        scratch_shapes=[pltpu.SemaphoreType.DMA((N_DEV - 1,)), pltpu.SemaphoreType.DMA((N_DEV - 1,)),
                        pltpu.SemaphoreType.DMA],
        compiler_params=pltpu.CompilerParams(vmem_limit_bytes=VMEM_LIMIT),
    )(buf)


def _cut(ref, shard_axis, *, chip=None, half=None, lead=None):
    shape = ref.shape[1:] if lead is not None else ref.shape
    idx = [slice(None), slice(None)]
    if chip is not None:
        w = shape[shard_axis] // N_CHIPS
        idx[shard_axis] = pl.ds(pl.multiple_of(chip * w, w), w)
    if half is not None:
        hw = shape[1 - shard_axis] // 2
        idx[1 - shard_axis] = pl.ds(pl.multiple_of(half * hw, hw), hw)
    if lead is not None:
        idx = [lead] + idx
    return ref.at[tuple(idx)]


def _wait_recv(ref, send_sem, recv_sem):
    pltpu.make_async_remote_copy(src_ref=ref, dst_ref=ref, send_sem=send_sem, recv_sem=recv_sem,
                                 device_id=_mesh_pos(), device_id_type=MESH).wait_recv()


def _gather_plan(axes):
    def plan(srcs, lands, send_sems, recv_sems):
        x, y, c = _mesh_pos()
        chip = 2 * x + y
        copies = []
        for i, ax in enumerate(axes):
            for k in range(1, N_CHIPS):
                px, py = x ^ (k >> 1), y ^ (k & 1)
                mine = _cut(lands[i], ax, chip=chip, half=c)
                j = i * (N_CHIPS - 1) + k - 1
                sems = dict(send_sem=send_sems.at[j], recv_sem=recv_sems.at[j], device_id=(px, py, c), device_id_type=MESH)
                send = pltpu.make_async_remote_copy(src_ref=mine, dst_ref=mine, **sems)
                recv = pltpu.make_async_remote_copy(src_ref=mine, dst_ref=_cut(lands[i], ax, chip=2 * px + py, half=c), **sems)
                copies.append((send, recv))
        return copies
    return plan


def _scatter_plan(axes):
    def plan(srcs, lands, send_sems, recv_sems):
        x, y, c = _mesh_pos()
        copies = []
        for i, ax in enumerate(axes):
            for k in range(1, N_CHIPS):
                px, py = x ^ (k >> 1), y ^ (k & 1)
                j = i * (N_CHIPS - 1) + k - 1
                cp = pltpu.make_async_remote_copy(
                    src_ref=_cut(srcs[i], ax, chip=2 * px + py), dst_ref=lands[i].at[k - 1],
                    send_sem=send_sems.at[j], recv_sem=recv_sems.at[j], device_id=(px, py, c), device_id_type=MESH)
                copies.append((cp, cp))
        return copies
    return plan


HBM = pl.BlockSpec(memory_space=pltpu.HBM)
SEM = pl.BlockSpec(memory_space=pltpu.SEMAPHORE)
EFFECT = pltpu.SideEffectType.DATAFLOW_SIDE_EFFECTING


def _in_hbm(arrays):
    return [pltpu.with_memory_space_constraint(a, pltpu.HBM) for a in arrays]


def _split_start(srcs, lands, plan, n_copies, *, name):
    bufs = list(srcs) + list(lands)
    ns, nb = len(srcs), len(bufs)

    def body(*refs):
        send_sems, recv_sems, token = refs[nb], refs[nb + 1], refs[-1]
        for send, _ in plan(refs[:ns], refs[ns:nb], send_sems, recv_sems):
            send.start()
        token[...] = jnp.zeros_like(token)

    sems = pltpu.SemaphoreType.DMA((n_copies,))
    res = pl.pallas_call(
        body, name=name, in_specs=[HBM] * nb,
        out_specs=[SEM, SEM] + [HBM] * nb + [pl.BlockSpec(memory_space=pltpu.VMEM)],
        out_shape=[sems, sems] + [pltpu.HBM(a.shape, a.dtype) for a in bufs] + [jax.ShapeDtypeStruct((8, LANES), F32)],
        input_output_aliases={j: 2 + j for j in range(nb)},
        compiler_params=pltpu.CompilerParams(has_side_effects=EFFECT),
    )(*_in_hbm(bufs))
    return res[0], res[1], res[2:2 + ns], res[2 + ns:2 + nb], res[-1]


def _split_wait(started, after, plan, *, name):
    send_sems, recv_sems, srcs, lands, _ = started
    bufs = list(srcs) + list(lands)
    ns, nb = len(srcs), len(bufs)

    def body(*refs):
        for send, recv in plan(refs[:ns], refs[ns:nb], refs[nb], refs[nb + 1]):
            send.wait_send()
            recv.wait_recv()

    res = pl.pallas_call(
        body, name=name, in_specs=[HBM] * nb + [SEM, SEM, ANY], out_specs=[HBM] * nb,
        out_shape=[pltpu.HBM(a.shape, a.dtype) for a in bufs],
        input_output_aliases={j: j for j in range(nb)},
        compiler_params=pltpu.CompilerParams(has_side_effects=EFFECT),
    )(*bufs, send_sems, recv_sems, after)
    return res[ns:]


def _cast_into_full(w3, layer, ax, chip, *, after=None, name):
    _, r, wd = w3.shape
    tr = _rows_per_tile(r, wd, 4 << 20)
    nt = r // tr
    full_shape = (r, wd * N_CHIPS) if ax == 1 else (r * N_CHIPS, wd)
    out_map = (lambda i, ch: (i, ch[0])) if ax == 1 else (lambda i, ch: (ch[0] * nt + i, 0))
    zero = jnp.zeros((1, wd), F32) + (0.0 if after is None else after)

    def body(chip_ref, w_ref, z_ref, o_ref):
        o_ref[...] = (w_ref[...] + z_ref[...]).astype(o_ref.dtype)

    return pl.pallas_call(
        body, name=name,
        grid_spec=pltpu.PrefetchScalarGridSpec(
            num_scalar_prefetch=1, grid=(nt,),
            in_specs=[pl.BlockSpec((None, tr, wd), lambda i, ch: (layer, i, 0)), pl.BlockSpec((1, wd), lambda i, ch: (0, 0))],
            out_specs=pl.BlockSpec((tr, wd), out_map)),
        out_shape=jax.ShapeDtypeStruct(full_shape, BF16),
        compiler_params=_params(("parallel",)),
    )(jnp.reshape(chip, (1,)).astype(jnp.int32), w3, zero)


def _forward_halves(fulls, axes, *, name):
    n = len(fulls)

    def body(*refs):
        bufs = refs[:n]
        send_sems, recv_sems = refs[2 * n:]
        x, y, c = _mesh_pos()
        sends = []
        for i in range(n):
            for k in range(1, N_CHIPS):
                landed = _cut(bufs[i], axes[i], chip=2 * (x ^ (k >> 1)) + (y ^ (k & 1)), half=c)
                cp = pltpu.make_async_remote_copy(
                    src_ref=landed, dst_ref=landed, send_sem=send_sems.at[i, k - 1], recv_sem=recv_sems.at[i, k - 1],
                    device_id=(x, y, 1 - c), device_id_type=MESH)
                cp.start()
                sends.append(cp)
        for i in range(n):
            for k in range(1, N_CHIPS):
                other = _cut(bufs[i], axes[i], chip=2 * (x ^ (k >> 1)) + (y ^ (k & 1)), half=1 - c)
                _wait_recv(other, send_sems.at[i, k - 1], recv_sems.at[i, k - 1])
        for cp in sends:
            cp.wait_send()

    pairs = pltpu.SemaphoreType.DMA((n, N_CHIPS - 1))
    return pl.pallas_call(
        body, name=name, in_specs=[ANY] * n, out_specs=[ANY] * n,
        out_shape=[jax.ShapeDtypeStruct(a.shape, a.dtype) for a in fulls],
        input_output_aliases={j: j for j in range(n)},
        scratch_shapes=[pairs, pairs],
    )(*fulls)


def _send_to_sibling(parts, *, name):
    n = len(parts)

    def body(*refs):
        ins, outs = refs[:n], refs[n:2 * n]
        send_sems, recv_sems = refs[2 * n:]
        x, y, c = _mesh_pos()
        sends = []
        for i in range(n):
            cp = pltpu.make_async_remote_copy(
                src_ref=ins[i], dst_ref=outs[i], send_sem=send_sems.at[i], recv_sem=recv_sems.at[i],
                device_id=(x, y, 1 - c), device_id_type=MESH)
            cp.start()
            sends.append(cp)
        for cp in sends:
            cp.wait()

    sems = pltpu.SemaphoreType.DMA((n,))
    return pl.pallas_call(
        body, name=name, in_specs=[ANY] * n, out_specs=[ANY] * n,
        out_shape=[jax.ShapeDtypeStruct(p.shape, p.dtype) for p in parts],
        scratch_shapes=[sems, sems],
    )(*parts)


def _adamw_math(w, g, m, v):
    m = ADAM_B1 * m + (1.0 - ADAM_B1) * g
    v = ADAM_B2 * v + (1.0 - ADAM_B2) * (g * g)
    m_hat = m / (1.0 - ADAM_B1 ** ADAM_STEP)
    v_hat = v / (1.0 - ADAM_B2 ** ADAM_STEP)
    delta = -ADAM_LR * (m_hat / (jnp.sqrt(v_hat) + ADAM_EPS) + ADAM_WD * w)
    return delta, m, v


def _adamw_layer(w3, m3, v3, p, q, layer, prev, *, name):
    nl, rows, width = w3.shape
    tr = _rows_per_tile(rows, width)

    def fn(*t):
        if q is None:
            w, m, v, g = t
        else:
            w, m, v, g, g2 = t
            g = g + g2
        delta, m, v = _adamw_math(w, g, m, v)
        return g, delta, m, v

    ins = [('t', w3, 0, width, layer), ('t', m3, 0, width, layer), ('t', v3, 0, width, layer), ('t', p, 0, width)]
    if q is not None:
        ins.append(('t', q, 0, width))
    outs = [('t', width, F32, layer, nl)] * 4
    aliases = None if prev is None else [(prev[i], i) for i in range(4)]
    return _ew(fn, ins, outs, rows=rows, tr=tr, name=name, aliases=aliases)


def _pack_rows(vec):
    n = vec.shape[0]
    r = -(-n // (8 * LANES)) * 8
    return jnp.pad(vec, (0, r * LANES - n)).reshape(r, LANES)


def kernel(x, c, ctx, c_ctx, ada_w, ada_b, norm_g, w_in, na_rpb, ret_decay_logit, w_proj_na, w_proj_ret, w_out, final_g, loss_target, m_c_ctx, m_ada_w, m_ada_b, m_norm_g, m_w_in, m_na_rpb, m_ret_decay_logit, m_w_proj_na, m_w_proj_ret, m_w_out, m_final_g, v_c_ctx, v_ada_w, v_ada_b, v_norm_g, v_w_in, v_na_rpb, v_ret_decay_logit, v_w_proj_na, v_w_proj_ret, v_w_out, v_final_g):
    depth = w_in.shape[0]
    s_len, d_model = x.shape[1], x.shape[2]
    l_len = ctx.shape[1]
    t_len = s_len + l_len
    na_heads = na_rpb.shape[1]
    ret_heads = ret_decay_logit.shape[2]
    w_na = na_heads * NA_HEAD_DIM
    w_qk = ret_heads * RET_KEY_DIM
    w_v = ret_heads * RET_VAL_DIM
    in_cols = w_in.shape[2] * N_CHIPS
    assert in_cols == 4 * w_na + 2 * w_qk + 2 * w_v + 2 * d_model
    assert x.shape[0] == 1 and s_len % (NA_WIN_ROWS * GRID_W) == 0 and l_len % RET_CHUNK == 0
    off = np.cumsum([0, w_na, w_na, w_na, w_na, w_qk, w_qk, w_v, w_v, d_model, d_model])
    o_naz, o_retq, o_retz, o_gna, o_gret = int(off[3]), int(off[4]), int(off[7]), int(off[8]), int(off[9])
    rows = s_len // GRID_W
    tr = _tile(l_len, 256, 8)
    n0 = s_len // tr
    mod_cols = 3 * d_model
    mod_shard = ada_w.shape[2]

    xi, yi, ci = _mesh_pos()
    me = 4 * xi + 2 * yi + ci
    chip = 2 * xi + yi

    big_axes = [1, 1, 0, 0]
    n_big = len(big_axes) * (N_CHIPS - 1)
    gather_plan, scatter_plan = _gather_plan(big_axes), _scatter_plan(big_axes)

    c_silu = c[0] * _sigmoid(c[0])
    cc_silu = c_ctx * _sigmoid(c_ctx)
    c_all = _all_gather_small(_pack_rows(c_silu), name="gather_c")[:, :d_model // LANES].reshape(N_DEV, d_model)
    a_rows = jnp.concatenate([c_all, cc_silu[None], jnp.zeros((16 - N_DEV - 1, d_model), F32)], axis=0)
    mod_part = jnp.stack([_mm(a_rows, ada_w, b_lead=l, out_dtype=F32, name="ada_fwd_%d" % l) for l in range(depth)])
    mod_all = _all_gather_small(_pack_rows(mod_part.reshape(-1)), name="gather_mod")
    n_mod = depth * 16 * mod_shard
    mod_all = mod_all.reshape(N_DEV, -1)[:, :n_mod].reshape(N_CHIPS, 2, depth, 16, mod_shard)[:, 0]
    mod_all = jnp.transpose(mod_all, (1, 2, 0, 3)).reshape(depth, 16, mod_cols) + ada_b[:, None, :]

    big_named = list(zip((w_in, w_proj_na, w_proj_ret, w_out), big_axes, ("w_in", "w_proj_na", "w_proj_ret", "w_out")))
    w_in0 = _cast_into_full(w_in, 0, big_axes[0], chip, name="cast_w_in_0")
    mod_all, w_in0 = lax.optimization_barrier((mod_all, w_in0))
    plan_in, plan_rest = _gather_plan(big_axes[:1]), _gather_plan(big_axes[1:])
    first_gather = _split_start([], [w_in0], plan_in, N_CHIPS - 1, name="gather_start_0_in")
    start_token = first_gather[4][0, 0]
    fulls = [[None if (l == 0 and tag == "w_in") else _cast_into_full(w, l, ax, chip, after=start_token, name="cast_%s_%d" % (tag, l))
              for w, ax, tag in big_named] for l in range(depth)]
    mod_lat = lax.dynamic_index_in_dim(mod_all, me, axis=1, keepdims=False)
    mod_ctx = mod_all[:, N_DEV]

    c2, s2 = _rope_tables(s_len, l_len)
    log_gamma = jax.nn.log_sigmoid(ret_decay_logit)
    x_all = jnp.concatenate([x[0], ctx[0]], axis=0)

    def grp(lat_vec, ctx_vec):
        return jnp.stack([lat_vec, ctx_vec])[:, None, :]

    saved, full_w = [], []
    for l in range(depth):
        shift, scale, gate = [grp(mod_lat[l, i * d_model:(i + 1) * d_model], mod_ctx[l, i * d_model:(i + 1) * d_model])
                              for i in range(3)]
        gs = norm_g[l][None, None, :] * (1.0 + scale) + start_token

        def modnorm(xt, gs_t, sh_t):
            r = lax.rsqrt(jnp.mean(xt * xt, axis=-1, keepdims=True) + NORM_EPS)
            return xt * r * gs_t + sh_t

        h, = _ew(modnorm, [('t', x_all, 0, d_model), ('g', gs), ('g', shift)], [('t', d_model, BF16)],
                 rows=t_len, tr=tr, n0=n0, name="modnorm_%d" % l)
        bias = _na_bias_table(na_rpb[l], rows, name="na_bias_%d" % l)
        h, bias = lax.optimization_barrier((h, bias))
        if l == 0:
            landed_in = _split_wait(first_gather, h, plan_in, name="gather_wait_0_in")
            landed_in, rest0, later = lax.optimization_barrier((landed_in, fulls[0][1:], fulls[1:]))
            rest_gather = _split_start([], rest0, plan_rest, n_big - (N_CHIPS - 1), name="gather_start_0_rest")
            later_gathers = [_split_start([], later[j], gather_plan, n_big, name="gather_start_%d" % (j + 1)) for j in range(depth - 1)]
            win_f, = _forward_halves(landed_in, big_axes[:1], name="gather_forward_0_in")
            win_f, tokens = lax.optimization_barrier((win_f, [rest_gather[4]] + [g[4] for g in later_gathers]))
            gate = gate + sum(t[0, 0] for t in tokens)
        else:
            landed = _split_wait(later_gathers[l - 1], h, gather_plan, name="gather_wait_%d" % l)
            win_f, wpn_f, wpr_f, wout_f = _forward_halves(landed, big_axes, name="gather_forward_%d" % l)
        u = _mm(h, win_f, name="in_proj_%d" % l)
        o_na = _na_fwd(u, bias, s_len=s_len, heads=na_heads, name="na_fwd_%d" % l)
        o_ret, states = _ret_fwd(u, c2, s2, log_gamma[l], s_len=s_len, heads=ret_heads, q_off=o_retq, name="ret_fwd_%d" % l)

        def act(o1, z1, o2, z2):
            a1 = o1.astype(F32) * _silu_parts(z1.astype(F32))[0]
            sz = _silu_parts(z2.astype(F32))[0]
            outs = []
            for hh in range(ret_heads):
                sl = slice(hh * RET_VAL_DIM, (hh + 1) * RET_VAL_DIM)
                oh = o2[:, sl]
                r = lax.rsqrt(jnp.mean(oh * oh, axis=-1, keepdims=True) + NORM_EPS)
                outs.append(oh * r * sz[:, sl])
            return a1, jnp.concatenate(outs, axis=-1)

        a_na, a_ret = _ew(act, [('t', o_na, 0, w_na), ('t', u, o_naz // w_na, w_na), ('t', o_ret, 0, w_v), ('t', u, o_retz // w_v, w_v)],
                          [('t', w_na, BF16), ('t', w_v, BF16)], rows=t_len, tr=tr, name="act_%d" % l)
        if l == 0:
            landed_rest = _split_wait(rest_gather, a_na, plan_rest, name="gather_wait_0_rest")
            wpn_f, wpr_f, wout_f = _forward_halves(landed_rest, big_axes[1:], name="gather_forward_0_rest")
        full_w.append((win_f, wpn_f, wpr_f, wout_f))
        y_na = _mm(a_na, wpn_f, name="proj_na_%d" % l)
        y_ret = _mm(a_ret, wpr_f, name="proj_ret_%d" % l)

        def merge(y1, y2, g1, g2):
            return _sigmoid(g1.astype(F32)) * y1.astype(F32) + _sigmoid(g2.astype(F32)) * y2.astype(F32)

        merged, = _ew(merge, [('t', y_na, 0, d_model), ('t', y_ret, 0, d_model), ('t', u, o_gna // d_model, d_model), ('t', u, o_gret // d_model, d_model)],
                      [('t', d_model, BF16)], rows=t_len, tr=tr, name="merge_%d" % l)
        out = _mm(merged, wout_f, out_dtype=F32, name="out_proj_%d" % l)
        x_new, = _ew(lambda xt, ot, gt: xt + gt * ot, [('t', x_all, 0, d_model), ('t', out, 0, d_model), ('g', gate)],
                     [('t', d_model, F32)], rows=t_len, tr=tr, n0=n0, name="resid_%d" % l)
        saved.append(dict(x=x_all, h=h, u=u, bias=bias, o_na=o_na, o_ret=o_ret, states=states, a_na=a_na, a_ret=a_ret,
                          y_na=y_na, y_ret=y_ret, merged=merged, out=out, gate=gate, gs=gs, scale=scale))
        x_all = x_new

    def final(xt, tt, gt):
        r = lax.rsqrt(jnp.mean(xt * xt, axis=-1, keepdims=True) + NORM_EPS)
        xh = xt * r
        e = xh * gt - tt
        dy = e * (1.0 / d_model)
        dyg = dy * gt
        dx = r * (dyg - xh * jnp.mean(dyg * xh, axis=-1, keepdims=True))
        return dx, _rsum(dy * xh), _rsum(e * e)

    dx_lat, d_final_g, loss_cols = _ew(final, [('t', x_all, 0, d_model), ('t', loss_target[0], 0, d_model), ('g', final_g[None, None, :])],
                                       [('t', d_model, F32), ('r', d_model, 1), ('r', d_model, 1)], rows=s_len, tr=tr, name="final")
    loss_part = (0.5 / d_model) * jnp.sum(loss_cols)
    dx_all = jnp.concatenate([dx_lat, jnp.zeros((l_len, d_model), F32)], axis=0)

    big_w = [(w_in, m_w_in, v_w_in), (w_proj_na, m_w_proj_na, v_w_proj_na), (w_proj_ret, m_w_proj_ret, v_w_proj_ret), (w_out, m_w_out, v_w_out)]
    big_res = [None] * 4
    scatters = {}
    back_token = jnp.zeros((), F32)

    def start_scatter(key, grads, axes):
        plan = _scatter_plan(axes)
        half_sz = [g.shape[1 - ax] // 2 for g, ax in zip(grads, axes)]
        mine = [lax.dynamic_slice_in_dim(g, ci * hs, hs, axis=1 - ax) for g, hs, ax in zip(grads, half_sz, axes)]
        to_send = [lax.dynamic_slice_in_dim(g, (1 - ci) * hs, hs, axis=1 - ax) for g, hs, ax in zip(grads, half_sz, axes)]
        theirs = _send_to_sibling(to_send, name="pair_exchange_%s" % key)
        pair = []
        for i in range(len(grads)):
            pr, pw = mine[i].shape
            s, = _ew(lambda a, b: a.astype(F32) + b.astype(F32), [('t', mine[i], 0, pw), ('t', theirs[i], 0, pw)], [('t', pw, BF16)],
                     rows=pr, tr=_rows_per_tile(pr, pw), name="sum_pair_%s_%d" % (key, i))
            pair.append(s)
        own = [lax.dynamic_slice_in_dim(s, chip * (s.shape[ax] // N_CHIPS), s.shape[ax] // N_CHIPS, axis=ax) for s, ax in zip(pair, axes)]
        lands = [lax.empty((N_CHIPS - 1,) + o.shape, BF16) for o in own]
        started = _split_start(pair, lands, plan, len(axes) * (N_CHIPS - 1), name="scatter_start_%s" % key)
        scatters[key] = (started, own, axes, plan)
        return started[4]

    def finish_scatter(key, after):
        started, own, axes, plan = scatters[key]
        recv = _split_wait(started, after, plan, name="scatter_wait_%s" % key)
        parts = []
        for i, rbuf in enumerate(recv):
            pr, pw = own[i].shape
            p, = _ew(lambda a, b, c_, d: ((a.astype(F32) + b.astype(F32)) + c_.astype(F32)) + d.astype(F32),
                     [('t', own[i], 0, pw)] + [('t', rbuf, 0, pw, k) for k in range(N_CHIPS - 1)], [('t', pw, F32)],
                     rows=pr, tr=_rows_per_tile(pr, pw), name="sum_chips_%s_%d" % (key, i))
            parts.append(p)
        others = _send_to_sibling(parts, name="share_halves_%s" % key)
        return [jnp.where(ci == 0, jnp.concatenate([p, o], axis=1 - ax), jnp.concatenate([o, p], axis=1 - ax))
                for p, o, ax in zip(parts, others, axes)]

    def adamw_big(l, idx, grads, big_res):
        for i, g in zip(idx, grads):
            w3, m3, v3 = big_w[i]
            big_res[i] = _adamw_layer(w3, m3, v3, g, None, l, big_res[i], name="adamw_big_%d_%d" % (i, l))
        return big_res

    small = dict(dmod_lat=[None] * depth, dmod_ctx=[None] * depth, dnorm_g=[None] * depth, drpb=[None] * depth, ddecay=[None] * depth)
    for l in reversed(range(depth)):
        sv = saved[l]
        win_f, wpn_f, wpr_f, wout_f = full_w[l]

        def resid_bwd(dxt, ot, gt):
            return gt * dxt, _rsum(dxt * ot)

        dout, dgate = _ew(resid_bwd, [('t', dx_all, 0, d_model), ('t', sv['out'], 0, d_model), ('g', sv['gate'] + back_token)],
                          [('t', d_model, BF16), ('r', d_model, 2)], rows=t_len, tr=tr, n0=n0, name="resid_bwd_%d" % l)
        dmerged = _mm(dout, wout_f, tb=True, name="out_proj_dx_%d" % l)
        g_wout = _mm(sv['merged'], dout, ta=True, tm=512, tk=t_len, name="out_proj_dw_%d" % l)

        def merge_bwd(dm, y1, y2, g1, g2):
            dm = dm.astype(F32)
            s1, s2_ = _sigmoid(g1.astype(F32)), _sigmoid(g2.astype(F32))
            return dm * s1, dm * s2_, dm * y1.astype(F32) * s1 * (1.0 - s1), dm * y2.astype(F32) * s2_ * (1.0 - s2_)

        u = sv['u']
        dy_na, dy_ret, dg_na, dg_ret = _ew(
            merge_bwd, [('t', dmerged, 0, d_model), ('t', sv['y_na'], 0, d_model), ('t', sv['y_ret'], 0, d_model),
                        ('t', u, o_gna // d_model, d_model), ('t', u, o_gret // d_model, d_model)],
            [('t', d_model, BF16)] * 4, rows=t_len, tr=tr, name="merge_bwd_%d" % l)
        da_na = _mm(dy_na, wpn_f, tb=True, name="proj_na_dx_%d" % l)
        g_wpn = _mm(sv['a_na'], dy_na, ta=True, tm=512, tk=t_len, name="proj_na_dw_%d" % l)
        da_ret = _mm(dy_ret, wpr_f, tb=True, name="proj_ret_dx_%d" % l)
        g_wpr = _mm(sv['a_ret'], dy_ret, ta=True, tm=512, tk=t_len, name="proj_ret_dw_%d" % l)
        lg_l = log_gamma[l]
        if l == 0:
            lg_l = lg_l + start_scatter("0_rest", [g_wpn, g_wpr, g_wout], big_axes[1:])[0, 0]

        def act_bwd(da1, o1, z1, da2, o2, z2):
            da1, da2 = da1.astype(F32), da2.astype(F32)
            si1, ds1 = _silu_parts(z1.astype(F32))
            si2, ds2 = _silu_parts(z2.astype(F32))
            do1 = da1 * si1
            dz1 = da1 * o1.astype(F32) * ds1
            dn = da2 * si2
            do2, dz2 = [], []
            for hh in range(ret_heads):
                sl = slice(hh * RET_VAL_DIM, (hh + 1) * RET_VAL_DIM)
                oh = o2[:, sl]
                r = lax.rsqrt(jnp.mean(oh * oh, axis=-1, keepdims=True) + NORM_EPS)
                nh = oh * r
                dz2.append(da2[:, sl] * nh * ds2[:, sl])
                do2.append(r * (dn[:, sl] - nh * jnp.mean(dn[:, sl] * nh, axis=-1, keepdims=True)))
            return do1, dz1, jnp.concatenate(do2, axis=-1), jnp.concatenate(dz2, axis=-1)

        do_na, dz_na, do_ret, dz_ret = _ew(
            act_bwd, [('t', da_na, 0, w_na), ('t', sv['o_na'], 0, w_na), ('t', u, o_naz // w_na, w_na),
                      ('t', da_ret, 0, w_v), ('t', sv['o_ret'], 0, w_v), ('t', u, o_retz // w_v, w_v)],
            [('t', w_na, BF16), ('t', w_na, BF16), ('t', w_v, BF16), ('t', w_v, BF16)], rows=t_len, tr=tr, name="act_bwd_%d" % l)
        dq_na, dk_na, dv_na, dbias = _na_bwd(u, sv['bias'], sv['o_na'], do_na, s_len=s_len, heads=na_heads, name="na_bwd_%d" % l)
        small['drpb'][l] = _rpb_grad(dbias, name="rpb_grad_%d" % l)
        dq_r, dk_r, dv_r, dlg = _ret_bwd(u, c2, s2, lg_l, sv['states'], do_ret, s_len=s_len, heads=ret_heads,
                                         q_off=o_retq, name="ret_bwd_%d" % l)
        small['ddecay'][l] = jnp.transpose(dlg[:, :, 0, 0]) * _sigmoid(-ret_decay_logit[l])
        du = jnp.concatenate([dq_na, dk_na, dv_na, dz_na, dq_r, dk_r, dv_r, dz_ret, dg_na, dg_ret], axis=1)
        dh = _mm(du, win_f, tb=True, out_dtype=F32, tn=1024, name="in_proj_dx_%d" % l)

        def modnorm_bwd(xt, dht, dxt, gs_t):
            r = lax.rsqrt(jnp.mean(xt * xt, axis=-1, keepdims=True) + NORM_EPS)
            xh = xt * r
            dhg = dht * gs_t
            dx = r * (dhg - xh * jnp.mean(dhg * xh, axis=-1, keepdims=True)) + dxt
            return dx, _rsum(dht), _rsum(dht * xh)

        dx_all, dshift, dgs = _ew(modnorm_bwd, [('t', sv['x'], 0, d_model), ('t', dh, 0, d_model), ('t', dx_all, 0, d_model), ('g', sv['gs'])],
                                  [('t', d_model, F32), ('r', d_model, 2), ('r', d_model, 2)], rows=t_len, tr=tr, n0=n0, name="modnorm_bwd_%d" % l)
        dscale = dgs * norm_g[l][None, None, :]
        small['dnorm_g'][l] = jnp.sum(dgs * (1.0 + sv['scale']), axis=(0, 1))
        dmod = jnp.concatenate([dshift, dscale, dgate], axis=-1)[:, 0]
        small['dmod_lat'][l], small['dmod_ctx'][l] = dmod[0], dmod[1]

        if l > 0:
            g_win = _mm(sv['h'], du, ta=True, tm=512, tk=t_len, name="in_proj_dw_%d" % l)
            back_token = start_scatter("%d_all" % l, [g_win, g_wpn, g_wpr, g_wout], big_axes)[0, 0]

    grad_x = dx_all[:s_len][None]

    drpb = jnp.stack(small['drpb']).reshape(-1)
    ddecay = jnp.stack(small['ddecay']).reshape(-1)
    pieces = [jnp.stack(small['dmod_lat']).reshape(-1), jnp.stack(small['dmod_ctx']).reshape(-1),
              jnp.stack(small['dnorm_g']).reshape(-1), d_final_g.reshape(-1), drpb, ddecay, loss_part[None]]
    sizes = [int(p.shape[0]) for p in pieces]
    pads = [-(-s // LANES) * LANES for s in sizes]
    packed = jnp.concatenate([jnp.pad(p, (0, pd - s)) for p, s, pd in zip(pieces, sizes, pads)])
    gathered = _all_gather_small(_pack_rows(packed), name="gather_small_grads")
    r_small = gathered.shape[1]

    def sum8(*t):
        acc = t[0]
        for other in t[1:]:
            acc = acc + other
        return acc

    total, = _ew(sum8, [('t', gathered, 0, LANES, k) for k in range(N_DEV)], [('t', LANES, F32)], rows=r_small, tr=r_small, name="sum_devices")
    total = total.reshape(-1)
    starts = np.cumsum([0] + pads)
    g_mod_lat_sum, g_mod_ctx, g_norm_g, g_final_g, g_rpb, g_decay, loss = [total[starts[i]:starts[i] + sizes[i]] for i in range(len(pieces))]
    loss = loss[0]
    g_ada_b = (g_mod_lat_sum + g_mod_ctx).reshape(depth, mod_cols)
    g_mod_ctx = g_mod_ctx.reshape(depth, mod_cols)
    dmod_lat_all = gathered.reshape(N_DEV, -1)[:, :depth * mod_cols].reshape(N_DEV, depth, mod_cols)

    dcc_part = jnp.zeros((16, d_model), F32)
    ctx_cols = [lax.dynamic_slice_in_dim(g_mod_ctx[l], chip * mod_shard, mod_shard, axis=0) for l in range(depth)]
    for l in reversed(range(depth)):
        c_rows = jnp.concatenate([ctx_cols[l][None], jnp.zeros((15, mod_shard), F32)], axis=0)
        dcc_part = dcc_part + _mm(c_rows, ada_w, tb=True, b_lead=l, out_dtype=F32, name="ada_dc_%d" % l)
    dcc_all = _all_gather_small(_pack_rows(dcc_part[0]), name="gather_dcc")[:, :d_model // LANES].reshape(N_CHIPS, 2, d_model)[:, 0]

    du0, dcc_all = lax.optimization_barrier((du, dcc_all))
    g_win0 = _mm(saved[0]['h'], du0, ta=True, tm=512, tk=t_len, name="in_proj_dw_0")
    tail_token = start_scatter("0_in", [g_win0], big_axes[:1])
    dcc = ((dcc_all[0] + dcc_all[1]) + dcc_all[2]) + dcc_all[3]
    sg = _sigmoid(c_ctx)
    g_c_ctx = dcc * (sg * (1.0 + c_ctx * (1.0 - sg)))
    for l in reversed(range(1, depth)):
        big_res = adamw_big(l, range(4), finish_scatter("%d_all" % l, tail_token), big_res)

    ada_res = None
    for l in reversed(range(depth)):
        lat_cols = lax.dynamic_slice_in_dim(dmod_lat_all[:, l], chip * mod_shard, mod_shard, axis=1)
        d_rows = jnp.concatenate([lat_cols, ctx_cols[l][None], jnp.zeros((16 - N_DEV - 1, mod_shard), F32)], axis=0) + tail_token[0, 0]
        g_ada = _mm(a_rows, d_rows, ta=True, out_dtype=F32, tm=512, name="ada_dw_%d" % l)
        ada_res = _adamw_layer(ada_w, m_ada_w, v_ada_w, g_ada, None, l, ada_res, name="adamw_ada_%d" % l)

    small_w = [(c_ctx, m_c_ctx, v_c_ctx, g_c_ctx), (ada_b, m_ada_b, v_ada_b, g_ada_b),
               (norm_g, m_norm_g, v_norm_g, g_norm_g), (na_rpb, m_na_rpb, v_na_rpb, g_rpb),
               (ret_decay_logit, m_ret_decay_logit, v_ret_decay_logit, g_decay), (final_g, m_final_g, v_final_g, g_final_g)]
    sw_sizes = [int(np.prod(t[0].shape)) for t in small_w]
    sw_pads = [-(-s // LANES) * LANES for s in sw_sizes]

    def pack(j):
        return _pack_rows(jnp.concatenate([jnp.pad(t[j].reshape(-1), (0, pd - s)) for t, s, pd in zip(small_w, sw_sizes, sw_pads)]))

    pw_, pm_, pv_, pg_ = pack(0), pack(1), pack(2), pack(3)
    sw_out = _ew(lambda w, m, v, g: (g,) + _adamw_math(w, g, m, v),
                 [('t', pw_, 0, LANES), ('t', pm_, 0, LANES), ('t', pv_, 0, LANES), ('t', pg_, 0, LANES)],
                 [('t', LANES, F32)] * 4, rows=pw_.shape[0], tr=pw_.shape[0], name="adamw_small")
    sw_starts = np.cumsum([0] + sw_pads)
    sw_out, ada_res, big_res = lax.optimization_barrier((sw_out, ada_res, big_res))
    big_res = adamw_big(0, range(1, 4), finish_scatter("0_rest", sw_out[0]), big_res)
    big_res = adamw_big(0, range(1), finish_scatter("0_in", sw_out[1]), big_res)

    def unpack(arr, i):
        return arr.reshape(-1)[sw_starts[i]:sw_starts[i] + sw_sizes[i]].reshape(small_w[i][0].shape)

    sm = [[unpack(sw_out[j], i) for i in range(len(small_w))] for j in range(4)]
    def ordered(j):
        return [sm[j][0], ada_res[j], sm[j][1], sm[j][2], big_res[0][j], sm[j][3], sm[j][4],
                big_res[1][j], big_res[2][j], big_res[3][j], sm[j][5]]

    return (loss, grad_x, *ordered(0), *ordered(1), *ordered(2), *ordered(3))
```

```python
import functools
import math

import numpy as np
import jax
import jax.numpy as jnp
from jax import lax
from jax.experimental import pallas as pl
from jax.experimental.pallas import tpu as pltpu

GRID_W = 64
NA_HEAD_DIM = 128
NA_WIN_ROWS = 8
NA_WIN_COLS = 16
NA_GROUP = 4
RET_GROUPS = (1, 2, 3)
RET_KEY_DIM = 128
RET_VAL_DIM = 256
RET_CHUNK = 128
ROPE_BASE = 10000.0
NORM_EPS = 1e-6
MASK_VALUE = -1e30
ADAM_LR = 0.001
ADAM_B1 = 0.9
ADAM_B2 = 0.999
ADAM_EPS = 1e-08
ADAM_WD = 0.01
ADAM_STEP = 10

N_CHIPS = 4
N_DEV = 8
LANES = 128
VMEM_LIMIT = 56 * 1024 * 1024
BF16 = jnp.bfloat16
F32 = jnp.float32
MESH = pl.DeviceIdType.MESH
ANY = pl.BlockSpec(memory_space=pl.ANY)


def _tile(dim, pref, align=LANES):
    if dim <= pref:
        return dim
    t = (pref // align) * align
    while t >= align:
        if dim % t == 0:
            return t
        t -= align
    return dim


def _rows_per_tile(rows, width, tile_bytes=1 << 20):
    return _tile(rows, max(8, tile_bytes // (4 * width)), 8)


def _params(sem):
    return pltpu.CompilerParams(dimension_semantics=sem, vmem_limit_bytes=VMEM_LIMIT)


def _sigmoid(x):
    return 1.0 / (1.0 + jnp.exp(-x))


def _dot(a, b, ca, cb):
    return lax.dot_general(a, b, (((ca,), (cb,)), ((), ())), preferred_element_type=F32)


def _mm(a, b, *, ta=False, tb=False, a_lead=None, b_lead=None, out_dtype=BF16, tm=768, tn=512, tk=2048, name):
    ash = a.shape[1:] if a_lead is not None else a.shape
    bsh = b.shape[1:] if b_lead is not None else b.shape
    m, k = (ash[1], ash[0]) if ta else ash
    n, k2 = bsh if tb else (bsh[1], bsh[0])
    assert k == k2, (name, ash, bsh)
    tm, tn, tk = _tile(m, tm), _tile(n, tn), _tile(k, tk)
    nk = k // tk

    def lead(spec_shape, imap, l):
        if l is None:
            return pl.BlockSpec(spec_shape, imap)
        return pl.BlockSpec((None,) + spec_shape, lambda i, j, kk: (l,) + imap(i, j, kk))

    a_spec = lead((tk, tm), lambda i, j, kk: (kk, i), a_lead) if ta else lead((tm, tk), lambda i, j, kk: (i, kk), a_lead)
    b_spec = lead((tn, tk), lambda i, j, kk: (j, kk), b_lead) if tb else lead((tk, tn), lambda i, j, kk: (kk, j), b_lead)
    ca, cb = (0 if ta else 1), (1 if tb else 0)

    def body(a_ref, b_ref, o_ref, *scratch):
        part = _dot(a_ref[...].astype(BF16), b_ref[...].astype(BF16), ca, cb)
        if nk == 1:
            o_ref[...] = part.astype(o_ref.dtype)
            return
        acc_ref, = scratch
        kk = pl.program_id(2)

        @pl.when(kk == 0)
        def _():
            acc_ref[...] = part

        @pl.when(kk > 0)
        def _():
            acc_ref[...] += part

        @pl.when(kk == nk - 1)
        def _():
            o_ref[...] = acc_ref[...].astype(o_ref.dtype)

    return pl.pallas_call(
        body, name=name, grid=(m // tm, n // tn, nk),
        in_specs=[a_spec, b_spec],
        out_specs=pl.BlockSpec((tm, tn), lambda i, j, kk: (i, j)),
        out_shape=jax.ShapeDtypeStruct((m, n), out_dtype),
        scratch_shapes=[] if nk == 1 else [pltpu.VMEM((tm, tn), F32)],
        compiler_params=_params(("parallel", "parallel", "arbitrary")),
    )(a, b)


def _ew(fn, ins, outs, *, rows, tr, name, n0=None, aliases=None):
    assert rows % tr == 0, (name, rows, tr)
    nt = rows // tr

    def grp(i):
        return 0 if n0 is None else jnp.where(i < n0, 0, 1)

    in_specs, args = [], []
    for spec in ins:
        if spec[0] == 't':
            arr, cb, w = spec[1], spec[2], spec[3]
            l = spec[4] if len(spec) > 4 else None
            if l is None:
                in_specs.append(pl.BlockSpec((tr, w), functools.partial(lambda i, cb: (i, cb), cb=cb)))
            else:
                in_specs.append(pl.BlockSpec((None, tr, w), functools.partial(lambda i, cb, l: (l, i, cb), cb=cb, l=l)))
            args.append(arr)
        else:
            arr = spec[1]
            g = arr.shape[0]
            if g == 1:
                in_specs.append(pl.BlockSpec((None, 1, arr.shape[2]), lambda i: (0, 0, 0)))
            else:
                in_specs.append(pl.BlockSpec((None, 1, arr.shape[2]), lambda i: (grp(i), 0, 0)))
            args.append(arr)
    out_specs, out_shapes, is_red = [], [], []
    for spec in outs:
        if spec[0] == 't':
            w, dt = spec[1], spec[2]
            if len(spec) > 3:
                l, nl = spec[3], spec[4]
                out_specs.append(pl.BlockSpec((None, tr, w), functools.partial(lambda i, l: (l, i, 0), l=l)))
                out_shapes.append(jax.ShapeDtypeStruct((nl, rows, w), dt))
            else:
                out_specs.append(pl.BlockSpec((tr, w), lambda i: (i, 0)))
                out_shapes.append(jax.ShapeDtypeStruct((rows, w), dt))
            is_red.append(False)
        else:
            w, g = spec[1], spec[2]
            if g == 1:
                out_specs.append(pl.BlockSpec((None, 1, w), lambda i: (0, 0, 0)))
            else:
                out_specs.append(pl.BlockSpec((None, 1, w), lambda i: (grp(i), 0, 0)))
            out_shapes.append(jax.ShapeDtypeStruct((g, 1, w), F32))
            is_red.append(True)
    n_in = len(ins)
    n_alias = 0 if aliases is None else len(aliases)

    def body(*refs):
        in_refs = refs[:n_in]
        out_refs = refs[n_in + n_alias:]
        res = fn(*[r[...] for r in in_refs])
        if not isinstance(res, (tuple, list)):
            res = (res,)
        i = pl.program_id(0)
        first = (i == 0) if n0 is None else ((i == 0) | (i == n0))
        for o_ref, val, red in zip(out_refs, res, is_red):
            if not red:
                o_ref[...] = val.astype(o_ref.dtype)
            else:
                @pl.when(first)
                def _(o_ref=o_ref, val=val):
                    o_ref[...] = val

                @pl.when(jnp.logical_not(first))
                def _(o_ref=o_ref, val=val):
                    o_ref[...] += val

    io_alias = {}
    if aliases is not None:
        for a_idx, (arr, o_idx) in enumerate(aliases):
            in_specs.append(ANY)
            args.append(arr)
            io_alias[n_in + a_idx] = o_idx
    has_red = any(is_red)
    return pl.pallas_call(
        body, name=name, grid=(nt,), in_specs=in_specs, out_specs=out_specs, out_shape=out_shapes,
        input_output_aliases=io_alias,
        compiler_params=_params(("arbitrary",) if has_red else ("parallel",)),
    )(*args)


def _half_spec(tr, width, ax, n_tiles):
    if ax == 1:
        return pl.BlockSpec((tr, width), lambda i, sel: (sel[0] * n_tiles + i, 0))
    return pl.BlockSpec((tr, width), lambda i, sel: (i, sel[0]))


def _sum_pair(g, theirs, ax, ci, *, name):
    pr, pw = theirs.shape
    tr = _rows_per_tile(pr, pw)
    nt = pr // tr

    def body(sel, a_ref, b_ref, o_ref):
        o_ref[...] = (a_ref[...].astype(F32) + b_ref[...].astype(F32)).astype(o_ref.dtype)

    return pl.pallas_call(
        body, name=name,
        grid_spec=pltpu.PrefetchScalarGridSpec(
            num_scalar_prefetch=1, grid=(nt,),
            in_specs=[_half_spec(tr, pw, ax, nt), pl.BlockSpec((tr, pw), lambda i, sel: (i, 0))],
            out_specs=pl.BlockSpec((tr, pw), lambda i, sel: (i, 0))),
        out_shape=jax.ShapeDtypeStruct((pr, pw), BF16),
        compiler_params=_params(("parallel",)),
    )(jnp.reshape(ci, (1,)).astype(jnp.int32), g, theirs)


def _sum_chips_into(own, recv, ax, ci, *, name):
    pr, pw = own.shape
    tr = _rows_per_tile(pr, pw)
    nt = pr // tr
    full_shape = (2 * pr, pw) if ax == 1 else (pr, 2 * pw)

    def body(sel, a_ref, r_ref, o_ref):
        acc = a_ref[...].astype(F32)
        for k in range(N_CHIPS - 1):
            acc = acc + r_ref[k].astype(F32)
        o_ref[...] = acc

    return pl.pallas_call(
        body, name=name,
        grid_spec=pltpu.PrefetchScalarGridSpec(
            num_scalar_prefetch=1, grid=(nt,),
            in_specs=[pl.BlockSpec((tr, pw), lambda i, sel: (i, 0)), pl.BlockSpec((N_CHIPS - 1, tr, pw), lambda i, sel: (0, i, 0))],
            out_specs=_half_spec(tr, pw, ax, nt)),
        out_shape=jax.ShapeDtypeStruct(full_shape, F32),
        compiler_params=_params(("parallel",)),
    )(jnp.reshape(ci, (1,)).astype(jnp.int32), own, recv)


def _rsum(v):
    return jnp.sum(v, axis=0, keepdims=True)


def _silu_parts(z):
    sg = _sigmoid(z)
    return z * sg, sg * (1.0 + z * (1.0 - sg))


def _na_bias_table(rpb, rows, *, name):
    kh, kw = NA_WIN_ROWS, NA_WIN_COLS
    assert rows >= kh
    heads = rpb.shape[0]
    e1, e2 = _na_onehots()
    rpb16 = jnp.pad(rpb, ((0, 0), (0, 16 - rpb.shape[1]), (0, LANES - rpb.shape[2])))

    def body(r_ref, e1_ref, e2_ref, o_ref):
        e1b = e1_ref[...].astype(BF16)
        y = sum(_dot(e1b, part, 0, 0) for part in _split3(r_ref[...]))
        e2b = e2_ref[...].astype(BF16)
        o_ref[...] = sum(_dot(part, e2b, 1, 1) for part in _split3(y))

    z = pl.pallas_call(
        body, name=name, grid=(heads,),
        in_specs=[pl.BlockSpec((None, 16, LANES), lambda h: (h, 0, 0)),
                  pl.BlockSpec(e1.shape, lambda h: (0, 0)), pl.BlockSpec(e2.shape, lambda h: (0, 0))],
        out_specs=pl.BlockSpec((None, kh * kh, GRID_W * GRID_W), lambda h: (h, 0, 0)),
        out_shape=jax.ShapeDtypeStruct((heads, kh * kh, GRID_W * GRID_W), F32),
        compiler_params=_params(("parallel",)),
    )(rpb16, e1, e2)
    cidx = np.arange(GRID_W)
    c0 = np.clip(cidx - kw // 2, 0, GRID_W - kw)
    col_in = (cidx[None, :] >= c0[:, None]) & (cidx[None, :] < c0[:, None] + kw)
    bias = z.reshape(heads, kh, kh, GRID_W, GRID_W).transpose(0, 1, 3, 2, 4)
    bias = jnp.where(col_in[None, None, :, None, :], bias, MASK_VALUE)
    return bias.reshape(heads, kh, GRID_W, kh * GRID_W)


def _na_onehots():
    kh, kw = NA_WIN_ROWS, NA_WIN_COLS
    cidx = np.arange(GRID_W)
    dc = cidx[None, :] - cidx[:, None] + (kw - 1)
    e2 = np.zeros((GRID_W * GRID_W, LANES), np.float32)
    ok = (dc >= 0) & (dc <= 2 * kw - 2)
    cq, ck = np.nonzero(ok)
    e2[cq * GRID_W + ck, dc[cq, ck]] = 1.0
    dr = np.arange(kh)[None, :] - np.arange(kh)[:, None] + (kh - 1)
    e1 = np.zeros((16, kh * kh), np.float32)
    dl, kr = np.nonzero(np.ones_like(dr))
    e1[dr[dl, kr], dl * kh + kr] = 1.0
    return jnp.asarray(e1), jnp.asarray(e2)


def _na_fwd(u, bias, *, s_len, heads, name):
    t_len = u.shape[0]
    rows = s_len // GRID_W
    nloc = NA_WIN_ROWS * GRID_W
    scale = NA_HEAD_DIM ** -0.5
    hd = NA_HEAD_DIM

    def body(q_ref, k_ref, v_ref, b_ref, o_ref):
        kc = k_ref[s_len:t_len, :]
        vc = v_ref[s_len:t_len, :]

        def group(g, carry):
            rs = [g * NA_GROUP + i for i in range(NA_GROUP)]
            r0s = [jnp.clip(r - NA_WIN_ROWS // 2, 0, rows - NA_WIN_ROWS) for r in rs]
            gs_ = pl.multiple_of(g * (NA_GROUP * GRID_W), NA_GROUP * GRID_W)
            kss = [pl.multiple_of(r0 * GRID_W, GRID_W) for r0 in r0s]
            q_all = q_ref[pl.ds(gs_, NA_GROUP * GRID_W), :]
            s_ctx = _dot(q_all, kc, 1, 1) * scale
            s_loc = [_dot(q_all[i * GRID_W:(i + 1) * GRID_W], k_ref[pl.ds(kss[i], nloc), :], 1, 1) * scale + b_ref[rs[i] - r0s[i]]
                     for i in range(NA_GROUP)]
            p_loc, p_ctx, inv = [], [], []
            for i in range(NA_GROUP):
                sc = s_ctx[i * GRID_W:(i + 1) * GRID_W]
                m = jnp.maximum(jnp.max(s_loc[i], axis=-1, keepdims=True), jnp.max(sc, axis=-1, keepdims=True))
                pl_, pc_ = jnp.exp(s_loc[i] - m), jnp.exp(sc - m)
                inv.append(1.0 / (jnp.sum(pl_, axis=-1, keepdims=True) + jnp.sum(pc_, axis=-1, keepdims=True)))
                p_loc.append(pl_.astype(BF16))
                p_ctx.append(pc_.astype(BF16))
            o_ctx = _dot(jnp.concatenate(p_ctx, axis=0), vc, 1, 0)
            o_loc = [_dot(p_loc[i], v_ref[pl.ds(kss[i], nloc), :], 1, 0) for i in range(NA_GROUP)]
            out = jnp.concatenate([(o_loc[i] + o_ctx[i * GRID_W:(i + 1) * GRID_W]) * inv[i] for i in range(NA_GROUP)], axis=0)
            o_ref[pl.ds(gs_, NA_GROUP * GRID_W), :] = out.astype(o_ref.dtype)
            return carry

        lax.fori_loop(0, rows // NA_GROUP, group, 0)
        qc = q_ref[s_len:t_len, :]
        s = _dot(qc, kc, 1, 1) * scale
        p = jnp.exp(s - jnp.max(s, axis=-1, keepdims=True))
        o = _dot(p.astype(BF16), vc, 1, 0) / jnp.sum(p, axis=-1, keepdims=True)
        o_ref[s_len:t_len, :] = o.astype(o_ref.dtype)

    col = lambda off: pl.BlockSpec((t_len, hd), functools.partial(lambda h, off: (0, off + h), off=off))
    return pl.pallas_call(
        body, name=name, grid=(heads,),
        in_specs=[col(0), col(heads), col(2 * heads),
                  pl.BlockSpec((None, NA_WIN_ROWS, GRID_W, nloc), lambda h: (h, 0, 0, 0))],
        out_specs=pl.BlockSpec((t_len, hd), lambda h: (0, h)),
        out_shape=jax.ShapeDtypeStruct((t_len, heads * hd), BF16),
        compiler_params=_params(("parallel",)),
    )(u, u, u, bias)


def _na_bwd(u, bias, o, do, *, s_len, heads, name):
    t_len = u.shape[0]
    rows = s_len // GRID_W
    nloc = NA_WIN_ROWS * GRID_W
    scale = NA_HEAD_DIM ** -0.5
    hd = NA_HEAD_DIM

    def body(q_ref, k_ref, v_ref, b_ref, o_ref, do_ref, dq_ref, dk_ref, dv_ref, db_ref, dk_acc, dv_acc):
        kc = k_ref[s_len:t_len, :]
        vc = v_ref[s_len:t_len, :]
        dk_acc[...] = jnp.zeros_like(dk_acc)
        dv_acc[...] = jnp.zeros_like(dv_acc)
        db_ref[...] = jnp.zeros_like(db_ref)

        def group(g, carry):
            n_g, rw = NA_GROUP, GRID_W
            rs = [g * n_g + i for i in range(n_g)]
            r0s = [jnp.clip(r - NA_WIN_ROWS // 2, 0, rows - NA_WIN_ROWS) for r in rs]
            dls = [r - r0 for r, r0 in zip(rs, r0s)]
            gs_ = pl.ds(pl.multiple_of(g * (n_g * rw), n_g * rw), n_g * rw)
            kss = [pl.ds(pl.multiple_of(r0 * rw, rw), nloc) for r0 in r0s]
            row_of = lambda a, i: a[i * rw:(i + 1) * rw]
            q_all, do_all = q_ref[gs_, :], do_ref[gs_, :]
            dlt_all = jnp.sum(do_all.astype(F32) * o_ref[gs_, :].astype(F32), axis=-1, keepdims=True)
            s_ctx = _dot(q_all, kc, 1, 1) * scale
            dp_ctx = _dot(do_all, vc, 1, 1)
            s_loc = [_dot(row_of(q_all, i), k_ref[kss[i], :], 1, 1) * scale + b_ref[dls[i]] for i in range(n_g)]
            dp_loc = [_dot(row_of(do_all, i), v_ref[kss[i], :], 1, 1) for i in range(n_g)]
            p_loc_b, ds_loc_b, p_ctx_b, ds_ctx_b = [], [], [], []
            for i in range(n_g):
                sc, dlt = row_of(s_ctx, i), row_of(dlt_all, i)
                m = jnp.maximum(jnp.max(s_loc[i], axis=-1, keepdims=True), jnp.max(sc, axis=-1, keepdims=True))
                pl_, pc_ = jnp.exp(s_loc[i] - m), jnp.exp(sc - m)
                inv = 1.0 / (jnp.sum(pl_, axis=-1, keepdims=True) + jnp.sum(pc_, axis=-1, keepdims=True))
                pl_, pc_ = pl_ * inv, pc_ * inv
                ds_l = pl_ * (dp_loc[i] - dlt)
                db_ref[dls[i]] += ds_l
                p_loc_b.append(pl_.astype(BF16))
                ds_loc_b.append(ds_l.astype(BF16))
                p_ctx_b.append(pc_.astype(BF16))
                ds_ctx_b.append((pc_ * (row_of(dp_ctx, i) - dlt)).astype(BF16))
            p_ctx_all, ds_ctx_all = jnp.concatenate(p_ctx_b, axis=0), jnp.concatenate(ds_ctx_b, axis=0)
            dq_ctx = _dot(ds_ctx_all, kc, 1, 0)
            dq_loc = [_dot(ds_loc_b[i], k_ref[kss[i], :], 1, 0) for i in range(n_g)]
            dk_loc = [_dot(ds_loc_b[i], row_of(q_all, i), 0, 0) for i in range(n_g)]
            dv_loc = [_dot(p_loc_b[i], row_of(do_all, i), 0, 0) for i in range(n_g)]
            dk_ctx = _dot(ds_ctx_all, q_all, 0, 0)
            dv_ctx = _dot(p_ctx_all, do_all, 0, 0)
            dq_ref[gs_, :] = ((jnp.concatenate(dq_loc, axis=0) + dq_ctx) * scale).astype(dq_ref.dtype)
            for i in range(n_g):
                dk_acc[kss[i], :] += dk_loc[i] * scale
                dv_acc[kss[i], :] += dv_loc[i]
            dk_acc[s_len:t_len, :] += dk_ctx * scale
            dv_acc[s_len:t_len, :] += dv_ctx
            return carry

        lax.fori_loop(0, rows // NA_GROUP, group, 0)
        qc = q_ref[s_len:t_len, :]
        dout = do_ref[s_len:t_len, :]
        out = o_ref[s_len:t_len, :]
        s = _dot(qc, kc, 1, 1) * scale
        p = jnp.exp(s - jnp.max(s, axis=-1, keepdims=True))
        p = p / jnp.sum(p, axis=-1, keepdims=True)
        dlt = jnp.sum(dout.astype(F32) * out.astype(F32), axis=-1, keepdims=True)
        ds = (p * (_dot(dout, vc, 1, 1) - dlt)).astype(BF16)
        dq_ref[s_len:t_len, :] = (_dot(ds, kc, 1, 0) * scale).astype(dq_ref.dtype)
        dk_acc[s_len:t_len, :] += _dot(ds, qc, 0, 0) * scale
        dv_acc[s_len:t_len, :] += _dot(p.astype(BF16), dout, 0, 0)
        dk_ref[...] = dk_acc[...].astype(dk_ref.dtype)
        dv_ref[...] = dv_acc[...].astype(dv_ref.dtype)

    col = lambda off: pl.BlockSpec((t_len, hd), functools.partial(lambda h, off: (0, off + h), off=off))
    tbl = pl.BlockSpec((None, NA_WIN_ROWS, GRID_W, nloc), lambda h: (h, 0, 0, 0))
    tok = jax.ShapeDtypeStruct((t_len, heads * hd), BF16)
    return pl.pallas_call(
        body, name=name, grid=(heads,),
        in_specs=[col(0), col(heads), col(2 * heads), tbl, col(0), col(0)],
        out_specs=[col(0), col(0), col(0), tbl],
        out_shape=[tok, tok, tok, jax.ShapeDtypeStruct(bias.shape, F32)],
        scratch_shapes=[pltpu.VMEM((t_len, hd), F32), pltpu.VMEM((t_len, hd), F32)],
        compiler_params=_params(("parallel",)),
    )(u, u, u, bias, o, do)


def _split3(x):
    hi = x.astype(BF16)
    r1 = x - hi.astype(F32)
    mid = r1.astype(BF16)
    lo = (r1 - mid.astype(F32)).astype(BF16)
    return hi, mid, lo


def _rpb_grad(dbias, *, name):
    heads = dbias.shape[0]
    kh = NA_WIN_ROWS
    e1, e2 = _na_onehots()
    x = dbias.reshape(heads, kh, GRID_W, kh, GRID_W).transpose(0, 1, 3, 2, 4).reshape(heads, kh * kh, GRID_W * GRID_W)

    def body(x_ref, e1_ref, e2_ref, o_ref):
        e2b = e2_ref[...].astype(BF16)
        y = sum(_dot(part, e2b, 1, 0) for part in _split3(x_ref[...]))
        e1b = e1_ref[...].astype(BF16)
        o_ref[...] = sum(_dot(e1b, part, 1, 0) for part in _split3(y))

    out = pl.pallas_call(
        body, name=name, grid=(heads,),
        in_specs=[pl.BlockSpec((None, kh * kh, GRID_W * GRID_W), lambda h: (h, 0, 0)),
                  pl.BlockSpec(e1.shape, lambda h: (0, 0)), pl.BlockSpec(e2.shape, lambda h: (0, 0))],
        out_specs=pl.BlockSpec((None, 16, LANES), lambda h: (h, 0, 0)),
        out_shape=jax.ShapeDtypeStruct((heads, 16, LANES), F32),
        compiler_params=_params(("parallel",)),
    )(x, e1, e2)
    return out[:, :2 * kh - 1, :2 * NA_WIN_COLS - 1]


def _rope_tables(s_len, l_len):
    nf = RET_KEY_DIM // 4
    t = np.arange(s_len)
    row = (t // GRID_W).astype(np.float32)
    colp = (t % GRID_W).astype(np.float32)
    inv_freq = jnp.asarray(ROPE_BASE, F32) ** (-jnp.arange(nf, dtype=F32) / nf)
    ang = jnp.concatenate([jnp.asarray(row)[:, None] * inv_freq, jnp.asarray(colp)[:, None] * inv_freq], axis=-1)
    cos, sin = jnp.cos(ang), jnp.sin(ang)
    c2 = jnp.concatenate([cos, cos], axis=-1)
    s2 = jnp.concatenate([-sin, sin], axis=-1)
    c2 = jnp.concatenate([c2, jnp.ones((l_len, RET_KEY_DIM), F32)], axis=0)
    s2 = jnp.concatenate([s2, jnp.zeros((l_len, RET_KEY_DIM), F32)], axis=0)
    return c2, s2


def _rope(x, c2, s2):
    return x * c2 + pltpu.roll(x, RET_KEY_DIM // 2, 1) * s2


def _rope_t(d, c2, s2):
    return d * c2 + pltpu.roll(d * s2, RET_KEY_DIM // 2, 1)


def _ret_decays(lg, direction):
    cs = RET_CHUNK
    i_col = lax.broadcasted_iota(jnp.int32, (cs, 1), 0)
    p_col = jnp.where(direction == 0, i_col, cs - 1 - i_col).astype(F32)
    pi = lax.broadcasted_iota(jnp.int32, (cs, cs), 0)
    pj = lax.broadcasted_iota(jnp.int32, (cs, cs), 1)
    diff = jnp.where(direction == 0, pi - pj, pj - pi).astype(F32)
    dm = jnp.where(diff >= 0, jnp.exp(jnp.maximum(diff, 0.0) * lg), 0.0)
    qdec = jnp.exp((p_col + 1.0) * lg)
    kdec = jnp.exp((cs - 1.0 - p_col) * lg)
    cd = jnp.exp(jnp.full((1, 1), cs, F32) * lg)
    return p_col, dm, qdec, kdec, cd


def _ret_chunk_index(t, direction, n_chunks, lat_chunks):
    return jnp.where(direction == 0, lax.rem(t + lat_chunks, n_chunks), n_chunks - 1 - t)


def _ret_fwd(u, c2, s2, lg, *, s_len, heads, q_off, name):
    t_len = u.shape[0]
    cs, dk, dv = RET_CHUNK, RET_KEY_DIM, RET_VAL_DIM
    n_chunks, lat_chunks = t_len // cs, s_len // cs
    k_scale = dk ** -0.5
    qb, kb, vb = q_off // dk, q_off // dk + heads, (q_off + 2 * heads * dk) // dv

    def body(lg_ref, q_ref, k_ref, v_ref, c_ref, s_ref, o_ref, st_ref, qd_s, kv_s):
        h, d = pl.program_id(0), pl.program_id(1)
        _, dm, qdec, kdec, cd = _ret_decays(lg_ref[d, h], d)
        n_g = max(g for g in RET_GROUPS if n_chunks % g == 0)
        rows_of = lambda c: pl.ds(pl.multiple_of(c * cs, cs), cs)

        def local(gi, carry):
            rws = [rows_of(gi * n_g + j) for j in range(n_g)]
            qcs = [_rope(q_ref[r, :].astype(F32), c_ref[r, :], s_ref[r, :]) for r in rws]
            kcs = [_rope(k_ref[r, :].astype(F32), c_ref[r, :], s_ref[r, :]) * k_scale for r in rws]
            vcs = [v_ref[r, :] for r in rws]
            a_raw = [_dot(qcs[j].astype(BF16), kcs[j].astype(BF16), 1, 1) for j in range(n_g)]
            kv = [_dot((kcs[j] * kdec).astype(BF16), vcs[j], 0, 0) for j in range(n_g)]
            inner = [_dot((a_raw[j] * dm).astype(BF16), vcs[j], 1, 0) for j in range(n_g)]
            for j in range(n_g):
                qd_s[rws[j], :] = (qcs[j] * qdec).astype(BF16)
                kv_s[gi * n_g + j] = kv[j]

            @pl.when(d == 0)
            def _():
                for j in range(n_g):
                    o_ref[rws[j], :] = inner[j]

            @pl.when(d == 1)
            def _():
                for j in range(n_g):
                    o_ref[rws[j], :] += inner[j]

            return carry

        lax.fori_loop(0, n_chunks // n_g, local, 0)

        def scan(t, st):
            st_ref[t] = st
            return st * cd + kv_s[_ret_chunk_index(t, d, n_chunks, lat_chunks)]

        lax.fori_loop(0, n_chunks, scan, jnp.zeros((dk, dv), F32))

        def cross(gi, carry):
            ts = [gi * n_g + j for j in range(n_g)]
            rws = [rows_of(_ret_chunk_index(t, d, n_chunks, lat_chunks)) for t in ts]
            outs = [_dot(qd_s[rws[j], :], st_ref[ts[j]].astype(BF16), 1, 0) for j in range(n_g)]
            for j in range(n_g):
                o_ref[rws[j], :] += outs[j]
            return carry

        lax.fori_loop(0, n_chunks // n_g, cross, 0)

    return pl.pallas_call(
        body, name=name, grid=(heads, 2),
        in_specs=[pl.BlockSpec(memory_space=pltpu.SMEM),
                  pl.BlockSpec((t_len, dk), lambda h, d: (0, qb + h)),
                  pl.BlockSpec((t_len, dk), lambda h, d: (0, kb + h)),
                  pl.BlockSpec((t_len, dv), lambda h, d: (0, vb + h)),
                  pl.BlockSpec((t_len, dk), lambda h, d: (0, 0)),
                  pl.BlockSpec((t_len, dk), lambda h, d: (0, 0))],
        out_specs=[pl.BlockSpec((t_len, dv), lambda h, d: (0, h)),
                   pl.BlockSpec((None, None, n_chunks, dk, dv), lambda h, d: (h, d, 0, 0, 0))],
        out_shape=[jax.ShapeDtypeStruct((t_len, heads * dv), F32),
                   jax.ShapeDtypeStruct((heads, 2, n_chunks, dk, dv), F32)],
        scratch_shapes=[pltpu.VMEM((t_len, dk), BF16), pltpu.VMEM((n_chunks, dk, dv), F32)],
        compiler_params=_params(("parallel", "arbitrary")),
    )(lg, u, u, u, c2, s2)


def _ret_bwd(u, c2, s2, lg, states, do, *, s_len, heads, q_off, name):
    t_len = u.shape[0]
    cs, dk, dv = RET_CHUNK, RET_KEY_DIM, RET_VAL_DIM
    n_chunks, lat_chunks = t_len // cs, s_len // cs
    k_scale = dk ** -0.5
    qb, kb, vb = q_off // dk, q_off // dk + heads, (q_off + 2 * heads * dk) // dv

    def body(lg_ref, q_ref, k_ref, v_ref, c_ref, s_ref, st_ref, do_ref, dq_ref, dk_ref, dv_ref, dlg_ref, acc, qdo_s, dst_s):
        h, d = pl.program_id(0), pl.program_id(1)
        p_col, dm, qdec, kdec, cd = _ret_decays(lg_ref[d, h], d)
        acc[...] = jnp.zeros_like(acc)
        n_g = max(g for g in RET_GROUPS[:2] if n_chunks % g == 0)
        rows_of = lambda c: pl.ds(pl.multiple_of(c * cs, cs), cs)
        chunk_of = lambda t: _ret_chunk_index(t, d, n_chunks, lat_chunks)

        def local(gi, carry):
            rws = [rows_of(gi * n_g + j) for j in range(n_g)]
            qds = [(_rope(q_ref[r, :].astype(F32), c_ref[r, :], s_ref[r, :]) * qdec).astype(BF16) for r in rws]
            prods = [_dot(qds[j], do_ref[rws[j], :].astype(BF16), 0, 0) for j in range(n_g)]
            for j in range(n_g):
                qdo_s[gi * n_g + j] = prods[j]
            return carry

        lax.fori_loop(0, n_chunks // n_g, local, 0)

        def scan(i, dst):
            t = n_chunks - 1 - i
            dst_s[t] = dst
            return dst * cd + qdo_s[chunk_of(t)]

        lax.fori_loop(0, n_chunks, scan, jnp.zeros((dk, dv), F32))

        def grads(gi, carry):
            ts = [gi * n_g + j for j in range(n_g)]
            rws = [rows_of(chunk_of(t)) for t in ts]
            ccs, sss = [c_ref[r, :] for r in rws], [s_ref[r, :] for r in rws]
            qcs = [_rope(q_ref[r, :].astype(F32), cc, ss) for r, cc, ss in zip(rws, ccs, sss)]
            kcs = [_rope(k_ref[r, :].astype(F32), cc, ss) * k_scale for r, cc, ss in zip(rws, ccs, sss)]
            vcs = [v_ref[r, :] for r in rws]
            docs = [do_ref[r, :].astype(BF16) for r in rws]
            sts = [st_ref[t] for t in ts]
            dsts = [dst_s[t] for t in ts]
            q16 = [x.astype(BF16) for x in qcs]
            k16 = [x.astype(BF16) for x in kcs]
            dst16 = [x.astype(BF16) for x in dsts]
            rng = range(n_g)
            a_raw = [_dot(q16[j], k16[j], 1, 1) for j in rng]
            da_raw = [_dot(docs[j], vcs[j], 1, 1) for j in rng]
            dq_c = [_dot(docs[j], sts[j].astype(BF16), 1, 1) * qdec for j in rng]
            dv_s = [_dot((kcs[j] * kdec).astype(BF16), dst16[j], 1, 0) for j in rng]
            dk_s = [_dot(vcs[j], dst16[j], 1, 1) * kdec for j in rng]
            a16 = [(a_raw[j] * dm).astype(BF16) for j in rng]
            dam = [(da_raw[j] * dm).astype(BF16) for j in rng]
            dq_i = [_dot(dam[j], k16[j], 1, 0) for j in rng]
            dk_i = [_dot(dam[j], q16[j], 0, 0) for j in rng]
            dv_i = [_dot(a16[j], docs[j], 0, 0) for j in rng]
            for j in rng:
                g = (jnp.sum(qcs[j] * (p_col * dq_i[j] + (p_col + 1.0) * dq_c[j]), axis=-1, keepdims=True)
                     + jnp.sum(kcs[j] * ((cs - 1.0 - p_col) * dk_s[j] - p_col * dk_i[j]), axis=-1, keepdims=True))
                g = (jnp.sum(g, axis=0, keepdims=True)
                     + cs * cd * jnp.sum(jnp.sum(dsts[j] * sts[j], axis=-1, keepdims=True), axis=0, keepdims=True))
                acc[...] += jnp.broadcast_to(g, acc.shape)
            dqs = [_rope_t(dq_i[j] + dq_c[j], ccs[j], sss[j]) for j in rng]
            dks = [_rope_t((dk_i[j] + dk_s[j]) * k_scale, ccs[j], sss[j]) for j in rng]
            dvs = [dv_i[j] + dv_s[j] for j in rng]

            @pl.when(d == 0)
            def _():
                for j in rng:
                    dq_ref[rws[j], :] = dqs[j].astype(dq_ref.dtype)
                    dk_ref[rws[j], :] = dks[j].astype(dk_ref.dtype)
                    dv_ref[rws[j], :] = dvs[j].astype(dv_ref.dtype)

            @pl.when(d == 1)
            def _():
                for j in rng:
                    dq_ref[rws[j], :] = (dq_ref[rws[j], :].astype(F32) + dqs[j]).astype(dq_ref.dtype)
                    dk_ref[rws[j], :] = (dk_ref[rws[j], :].astype(F32) + dks[j]).astype(dk_ref.dtype)
                    dv_ref[rws[j], :] = (dv_ref[rws[j], :].astype(F32) + dvs[j]).astype(dv_ref.dtype)

            return carry

        lax.fori_loop(0, n_chunks // n_g, grads, 0)
        dlg_ref[...] = acc[...]

    return pl.pallas_call(
        body, name=name, grid=(heads, 2),
        in_specs=[pl.BlockSpec(memory_space=pltpu.SMEM),
                  pl.BlockSpec((t_len, dk), lambda h, d: (0, qb + h)),
                  pl.BlockSpec((t_len, dk), lambda h, d: (0, kb + h)),
                  pl.BlockSpec((t_len, dv), lambda h, d: (0, vb + h)),
                  pl.BlockSpec((t_len, dk), lambda h, d: (0, 0)),
                  pl.BlockSpec((t_len, dk), lambda h, d: (0, 0)),
                  pl.BlockSpec((None, None, n_chunks, dk, dv), lambda h, d: (h, d, 0, 0, 0)),
                  pl.BlockSpec((t_len, dv), lambda h, d: (0, h))],
        out_specs=[pl.BlockSpec((t_len, dk), lambda h, d: (0, h)),
                   pl.BlockSpec((t_len, dk), lambda h, d: (0, h)),
                   pl.BlockSpec((t_len, dv), lambda h, d: (0, h)),
                   pl.BlockSpec((None, None, 8, LANES), lambda h, d: (h, d, 0, 0))],
        out_shape=[jax.ShapeDtypeStruct((t_len, heads * dk), BF16),
                   jax.ShapeDtypeStruct((t_len, heads * dk), BF16),
                   jax.ShapeDtypeStruct((t_len, heads * dv), BF16),
                   jax.ShapeDtypeStruct((heads, 2, 8, LANES), F32)],
        scratch_shapes=[pltpu.VMEM((8, LANES), F32), pltpu.VMEM((n_chunks, dk, dv), F32), pltpu.VMEM((n_chunks, dk, dv), F32)],
        compiler_params=_params(("parallel", "arbitrary")),
    )(lg, u, u, u, c2, s2, states, do)


def _mesh_pos():
    return lax.axis_index("x"), lax.axis_index("y"), lax.axis_index("c")


def _all_gather_small(buf, *, name):
    r = buf.shape[0]

    def body(x_ref, o_ref, send_sems, recv_sems, local_sem):
        x, y, c = _mesh_pos()
        me = 4 * x + 2 * y + c
        mine = pltpu.make_async_copy(x_ref, o_ref.at[me], local_sem)
        mine.start()
        copies = []
        for k in range(1, N_DEV):
            px, py, pc = x ^ ((k >> 2) & 1), y ^ ((k >> 1) & 1), c ^ (k & 1)
            cp = pltpu.make_async_remote_copy(
                src_ref=x_ref, dst_ref=o_ref.at[me], send_sem=send_sems.at[k - 1], recv_sem=recv_sems.at[k - 1],
                device_id=(px, py, pc), device_id_type=MESH)
            cp.start()
            copies.append((cp, 4 * px + 2 * py + pc))
        for k, (cp, peer) in enumerate(copies):
            pltpu.make_async_remote_copy(
                src_ref=x_ref, dst_ref=o_ref.at[peer], send_sem=send_sems.at[k], recv_sem=recv_sems.at[k],
                device_id=(x, y, c), device_id_type=MESH).wait_recv()
        for cp, _ in copies:
            cp.wait_send()
        mine.wait()

    return pl.pallas_call(
        body, name=name,
        in_specs=[pl.BlockSpec(memory_space=pltpu.VMEM)],
        out_specs=pl.BlockSpec(memory_space=pltpu.VMEM),
        out_shape=jax.ShapeDtypeStruct((N_DEV, r, LANES), F32),
        scratch_shapes=[pltpu.SemaphoreType.DMA((N_DEV - 1,)), pltpu.SemaphoreType.DMA((N_DEV - 1,)),
                        pltpu.SemaphoreType.DMA],
        compiler_params=pltpu.CompilerParams(vmem_limit_bytes=VMEM_LIMIT),
    )(buf)


def _cut(ref, shard_axis, *, chip=None, half=None, lead=None):
    shape = ref.shape[1:] if lead is not None else ref.shape
    idx = [slice(None), slice(None)]
    if chip is not None:
        w = shape[shard_axis] // N_CHIPS
        idx[shard_axis] = pl.ds(pl.multiple_of(chip * w, w), w)
    if half is not None:
        hw = shape[1 - shard_axis] // 2
        idx[1 - shard_axis] = pl.ds(pl.multiple_of(half * hw, hw), hw)
    if lead is not None:
        idx = [lead] + idx
    return ref.at[tuple(idx)]


def _wait_recv(ref, send_sem, recv_sem):
    pltpu.make_async_remote_copy(src_ref=ref, dst_ref=ref, send_sem=send_sem, recv_sem=recv_sem,
                                 device_id=_mesh_pos(), device_id_type=MESH).wait_recv()


def _gather_plan(axes):
    def plan(srcs, lands, send_sems, recv_sems):
        x, y, c = _mesh_pos()
        chip = 2 * x + y
        copies = []
        for i, ax in enumerate(axes):
            for k in range(1, N_CHIPS):
                px, py = x ^ (k >> 1), y ^ (k & 1)
                mine = _cut(lands[i], ax, chip=chip, half=c)
                j = i * (N_CHIPS - 1) + k - 1
                sems = dict(send_sem=send_sems.at[j], recv_sem=recv_sems.at[j], device_id=(px, py, c), device_id_type=MESH)
                send = pltpu.make_async_remote_copy(src_ref=mine, dst_ref=mine, **sems)
                recv = pltpu.make_async_remote_copy(src_ref=mine, dst_ref=_cut(lands[i], ax, chip=2 * px + py, half=c), **sems)
                copies.append((send, recv))
        return copies
    return plan


def _scatter_plan(axes):
    def plan(srcs, lands, send_sems, recv_sems):
        x, y, c = _mesh_pos()
        copies = []
        for i, ax in enumerate(axes):
            for k in range(1, N_CHIPS):
                px, py = x ^ (k >> 1), y ^ (k & 1)
                j = i * (N_CHIPS - 1) + k - 1
                cp = pltpu.make_async_remote_copy(
                    src_ref=_cut(srcs[i], ax, chip=2 * px + py), dst_ref=lands[i].at[k - 1],
                    send_sem=send_sems.at[j], recv_sem=recv_sems.at[j], device_id=(px, py, c), device_id_type=MESH)
                copies.append((cp, cp))
        return copies
    return plan


HBM = pl.BlockSpec(memory_space=pltpu.HBM)
SEM = pl.BlockSpec(memory_space=pltpu.SEMAPHORE)
EFFECT = pltpu.SideEffectType.DATAFLOW_SIDE_EFFECTING


def _in_hbm(arrays):
    return [pltpu.with_memory_space_constraint(a, pltpu.HBM) for a in arrays]


def _split_start(srcs, lands, plan, n_copies, *, name):
    bufs = list(srcs) + list(lands)
    ns, nb = len(srcs), len(bufs)

    def body(*refs):
        send_sems, recv_sems, token = refs[nb], refs[nb + 1], refs[-1]
        for send, _ in plan(refs[:ns], refs[ns:nb], send_sems, recv_sems):
            send.start()
        token[...] = jnp.zeros_like(token)

    sems = pltpu.SemaphoreType.DMA((n_copies,))
    res = pl.pallas_call(
        body, name=name, in_specs=[HBM] * nb,
        out_specs=[SEM, SEM] + [HBM] * nb + [pl.BlockSpec(memory_space=pltpu.VMEM)],
        out_shape=[sems, sems] + [pltpu.HBM(a.shape, a.dtype) for a in bufs] + [jax.ShapeDtypeStruct((8, LANES), F32)],
        input_output_aliases={j: 2 + j for j in range(nb)},
        compiler_params=pltpu.CompilerParams(has_side_effects=EFFECT),
    )(*_in_hbm(bufs))
    return res[0], res[1], res[2:2 + ns], res[2 + ns:2 + nb], res[-1]


def _split_wait(started, after, plan, *, name):
    send_sems, recv_sems, srcs, lands, _ = started
    bufs = list(srcs) + list(lands)
    ns, nb = len(srcs), len(bufs)

    def body(*refs):
        for send, recv in plan(refs[:ns], refs[ns:nb], refs[nb], refs[nb + 1]):
            send.wait_send()
            recv.wait_recv()

    res = pl.pallas_call(
        body, name=name, in_specs=[HBM] * nb + [SEM, SEM, ANY], out_specs=[HBM] * nb,
        out_shape=[pltpu.HBM(a.shape, a.dtype) for a in bufs],
        input_output_aliases={j: j for j in range(nb)},
        compiler_params=pltpu.CompilerParams(has_side_effects=EFFECT),
    )(*bufs, send_sems, recv_sems, after)
    return res[ns:]


def _cast_into_full(w3, layer, ax, chip, *, after=None, name):
    _, r, wd = w3.shape
    tr = _rows_per_tile(r, wd, 4 << 20)
    nt = r // tr
    full_shape = (r, wd * N_CHIPS) if ax == 1 else (r * N_CHIPS, wd)
    out_map = (lambda i, ch: (i, ch[0])) if ax == 1 else (lambda i, ch: (ch[0] * nt + i, 0))
    zero = jnp.zeros((1, wd), F32) + (0.0 if after is None else after)

    def body(chip_ref, w_ref, z_ref, o_ref):
        o_ref[...] = (w_ref[...] + z_ref[...]).astype(o_ref.dtype)

    return pl.pallas_call(
        body, name=name,
        grid_spec=pltpu.PrefetchScalarGridSpec(
            num_scalar_prefetch=1, grid=(nt,),
            in_specs=[pl.BlockSpec((None, tr, wd), lambda i, ch: (layer, i, 0)), pl.BlockSpec((1, wd), lambda i, ch: (0, 0))],
            out_specs=pl.BlockSpec((tr, wd), out_map)),
        out_shape=jax.ShapeDtypeStruct(full_shape, BF16),
        compiler_params=_params(("parallel",)),
    )(jnp.reshape(chip, (1,)).astype(jnp.int32), w3, zero)


def _forward_halves(fulls, axes, *, name):
    n = len(fulls)

    def body(*refs):
        bufs = refs[:n]
        send_sems, recv_sems = refs[2 * n:]
        x, y, c = _mesh_pos()
        sends = []
        for i in range(n):
            for k in range(1, N_CHIPS):
                landed = _cut(bufs[i], axes[i], chip=2 * (x ^ (k >> 1)) + (y ^ (k & 1)), half=c)
                cp = pltpu.make_async_remote_copy(
                    src_ref=landed, dst_ref=landed, send_sem=send_sems.at[i, k - 1], recv_sem=recv_sems.at[i, k - 1],
                    device_id=(x, y, 1 - c), device_id_type=MESH)
                cp.start()
                sends.append(cp)
        for i in range(n):
            for k in range(1, N_CHIPS):
                other = _cut(bufs[i], axes[i], chip=2 * (x ^ (k >> 1)) + (y ^ (k & 1)), half=1 - c)
                _wait_recv(other, send_sems.at[i, k - 1], recv_sems.at[i, k - 1])
        for cp in sends:
            cp.wait_send()

    pairs = pltpu.SemaphoreType.DMA((n, N_CHIPS - 1))
    return pl.pallas_call(
        body, name=name, in_specs=[ANY] * n, out_specs=[ANY] * n,
        out_shape=[jax.ShapeDtypeStruct(a.shape, a.dtype) for a in fulls],
        input_output_aliases={j: j for j in range(n)},
        scratch_shapes=[pairs, pairs],
    )(*fulls)


def _send_other_half(grads, axes, *, name):
    n = len(grads)
    half_shapes = []
    for g, ax in zip(grads, axes):
        shp = list(g.shape)
        shp[1 - ax] //= 2
        half_shapes.append(jax.ShapeDtypeStruct(tuple(shp), g.dtype))

    def body(*refs):
        ins, outs = refs[:n], refs[n:2 * n]
        send_sems, recv_sems = refs[2 * n:]
        x, y, c = _mesh_pos()
        sends = []
        for i in range(n):
            cp = pltpu.make_async_remote_copy(
                src_ref=_cut(ins[i], axes[i], half=1 - c), dst_ref=outs[i], send_sem=send_sems.at[i], recv_sem=recv_sems.at[i],
                device_id=(x, y, 1 - c), device_id_type=MESH)
            cp.start()
            sends.append(cp)
        for cp in sends:
            cp.wait()

    sems = pltpu.SemaphoreType.DMA((n,))
    return pl.pallas_call(
        body, name=name, in_specs=[ANY] * n, out_specs=[ANY] * n, out_shape=half_shapes, scratch_shapes=[sems, sems],
    )(*grads)


def _share_halves_in_place(bufs, axes, *, name):
    n = len(bufs)

    def body(*refs):
        ins = refs[:n]
        send_sems, recv_sems = refs[2 * n:]
        x, y, c = _mesh_pos()
        sends = []
        for i in range(n):
            mine = _cut(ins[i], axes[i], half=c)
            cp = pltpu.make_async_remote_copy(
                src_ref=mine, dst_ref=mine, send_sem=send_sems.at[i], recv_sem=recv_sems.at[i],
                device_id=(x, y, 1 - c), device_id_type=MESH)
            cp.start()
            sends.append(cp)
        for i in range(n):
            _wait_recv(_cut(ins[i], axes[i], half=1 - c), send_sems.at[i], recv_sems.at[i])
        for cp in sends:
            cp.wait_send()

    sems = pltpu.SemaphoreType.DMA((n,))
    return pl.pallas_call(
        body, name=name, in_specs=[ANY] * n, out_specs=[ANY] * n,
        out_shape=[jax.ShapeDtypeStruct(b.shape, b.dtype) for b in bufs],
        input_output_aliases={j: j for j in range(n)}, scratch_shapes=[sems, sems],
    )(*bufs)


def _adamw_math(w, g, m, v):
    m = ADAM_B1 * m + (1.0 - ADAM_B1) * g
    v = ADAM_B2 * v + (1.0 - ADAM_B2) * (g * g)
    m_hat = m / (1.0 - ADAM_B1 ** ADAM_STEP)
    v_hat = v / (1.0 - ADAM_B2 ** ADAM_STEP)
    delta = -ADAM_LR * (m_hat / (jnp.sqrt(v_hat) + ADAM_EPS) + ADAM_WD * w)
    return delta, m, v


def _adamw_layer(w3, m3, v3, p, q, layer, prev, *, name):
    nl, rows, width = w3.shape
    tr = _rows_per_tile(rows, width)

    def fn(*t):
        if q is None:
            w, m, v, g = t
        else:
            w, m, v, g, g2 = t
            g = g + g2
        delta, m, v = _adamw_math(w, g, m, v)
        return g, delta, m, v

    ins = [('t', w3, 0, width, layer), ('t', m3, 0, width, layer), ('t', v3, 0, width, layer), ('t', p, 0, width)]
    if q is not None:
        ins.append(('t', q, 0, width))
    outs = [('t', width, F32, layer, nl)] * 4
    aliases = None if prev is None else [(prev[i], i) for i in range(4)]
    return _ew(fn, ins, outs, rows=rows, tr=tr, name=name, aliases=aliases)


def _pack_rows(vec):
    n = vec.shape[0]
    r = -(-n // (8 * LANES)) * 8
    return jnp.pad(vec, (0, r * LANES - n)).reshape(r, LANES)


def kernel(x, c, ctx, c_ctx, ada_w, ada_b, norm_g, w_in, na_rpb, ret_decay_logit, w_proj_na, w_proj_ret, w_out, final_g, loss_target, m_c_ctx, m_ada_w, m_ada_b, m_norm_g, m_w_in, m_na_rpb, m_ret_decay_logit, m_w_proj_na, m_w_proj_ret, m_w_out, m_final_g, v_c_ctx, v_ada_w, v_ada_b, v_norm_g, v_w_in, v_na_rpb, v_ret_decay_logit, v_w_proj_na, v_w_proj_ret, v_w_out, v_final_g):
    depth = w_in.shape[0]
    s_len, d_model = x.shape[1], x.shape[2]
    l_len = ctx.shape[1]
    t_len = s_len + l_len
    na_heads = na_rpb.shape[1]
    ret_heads = ret_decay_logit.shape[2]
    w_na = na_heads * NA_HEAD_DIM
    w_qk = ret_heads * RET_KEY_DIM
    w_v = ret_heads * RET_VAL_DIM
    in_cols = w_in.shape[2] * N_CHIPS
    assert in_cols == 4 * w_na + 2 * w_qk + 2 * w_v + 2 * d_model
    assert x.shape[0] == 1 and s_len % (NA_WIN_ROWS * GRID_W) == 0 and l_len % RET_CHUNK == 0
    off = np.cumsum([0, w_na, w_na, w_na, w_na, w_qk, w_qk, w_v, w_v, d_model, d_model])
    o_naz, o_retq, o_retz, o_gna, o_gret = int(off[3]), int(off[4]), int(off[7]), int(off[8]), int(off[9])
    rows = s_len // GRID_W
    tr = _tile(l_len, 256, 8)
    n0 = s_len // tr
    mod_cols = 3 * d_model
    mod_shard = ada_w.shape[2]

    xi, yi, ci = _mesh_pos()
    me = 4 * xi + 2 * yi + ci
    chip = 2 * xi + yi

    big_axes = [1, 1, 0, 0]
    n_big = len(big_axes) * (N_CHIPS - 1)
    gather_plan, scatter_plan = _gather_plan(big_axes), _scatter_plan(big_axes)

    c_silu = c[0] * _sigmoid(c[0])
    cc_silu = c_ctx * _sigmoid(c_ctx)
    c_all = _all_gather_small(_pack_rows(c_silu), name="gather_c")[:, :d_model // LANES].reshape(N_DEV, d_model)
    a_rows = jnp.concatenate([c_all, cc_silu[None], jnp.zeros((16 - N_DEV - 1, d_model), F32)], axis=0)
    mod_part = jnp.stack([_mm(a_rows, ada_w, b_lead=l, out_dtype=F32, name="ada_fwd_%d" % l) for l in range(depth)])
    mod_all = _all_gather_small(_pack_rows(mod_part.reshape(-1)), name="gather_mod")
    n_mod = depth * 16 * mod_shard
    mod_all = mod_all.reshape(N_DEV, -1)[:, :n_mod].reshape(N_CHIPS, 2, depth, 16, mod_shard)[:, 0]
    mod_all = jnp.transpose(mod_all, (1, 2, 0, 3)).reshape(depth, 16, mod_cols) + ada_b[:, None, :]

    big_named = list(zip((w_in, w_proj_na, w_proj_ret, w_out), big_axes, ("w_in", "w_proj_na", "w_proj_ret", "w_out")))
    w_in0 = _cast_into_full(w_in, 0, big_axes[0], chip, name="cast_w_in_0")
    mod_all, w_in0 = lax.optimization_barrier((mod_all, w_in0))
    plan_in, plan_rest = _gather_plan(big_axes[:1]), _gather_plan(big_axes[1:])
    first_gather = _split_start([], [w_in0], plan_in, N_CHIPS - 1, name="gather_start_0_in")
    start_token = first_gather[4][0, 0]
    fulls = [[None if (l == 0 and tag == "w_in") else _cast_into_full(w, l, ax, chip, after=start_token, name="cast_%s_%d" % (tag, l))
              for w, ax, tag in big_named] for l in range(depth)]
    mod_lat = lax.dynamic_index_in_dim(mod_all, me, axis=1, keepdims=False)
    mod_ctx = mod_all[:, N_DEV]

    c2, s2 = _rope_tables(s_len, l_len)
    log_gamma = jax.nn.log_sigmoid(ret_decay_logit)
    x_all = jnp.concatenate([x[0], ctx[0]], axis=0)

    def grp(lat_vec, ctx_vec):
        return jnp.stack([lat_vec, ctx_vec])[:, None, :]

    saved, full_w = [], []
    for l in range(depth):
        shift, scale, gate = [grp(mod_lat[l, i * d_model:(i + 1) * d_model], mod_ctx[l, i * d_model:(i + 1) * d_model])
                              for i in range(3)]
        gs = norm_g[l][None, None, :] * (1.0 + scale) + start_token

        def modnorm(xt, gs_t, sh_t):
            r = lax.rsqrt(jnp.mean(xt * xt, axis=-1, keepdims=True) + NORM_EPS)
            return xt * r * gs_t + sh_t

        h, = _ew(modnorm, [('t', x_all, 0, d_model), ('g', gs), ('g', shift)], [('t', d_model, BF16)],
                 rows=t_len, tr=tr, n0=n0, name="modnorm_%d" % l)
        bias = _na_bias_table(na_rpb[l], rows, name="na_bias_%d" % l)
        h, bias = lax.optimization_barrier((h, bias))
        if l == 0:
            landed_in = _split_wait(first_gather, h, plan_in, name="gather_wait_0_in")
            landed_in, rest0, later = lax.optimization_barrier((landed_in, fulls[0][1:], fulls[1:]))
            rest_gather = _split_start([], rest0, plan_rest, n_big - (N_CHIPS - 1), name="gather_start_0_rest")
            later_gathers = [_split_start([], later[j], gather_plan, n_big, name="gather_start_%d" % (j + 1)) for j in range(depth - 1)]
            win_f, = _forward_halves(landed_in, big_axes[:1], name="gather_forward_0_in")
            win_f, tokens = lax.optimization_barrier((win_f, [rest_gather[4]] + [g[4] for g in later_gathers]))
            gate = gate + sum(t[0, 0] for t in tokens)
        else:
            landed = _split_wait(later_gathers[l - 1], h, gather_plan, name="gather_wait_%d" % l)
            win_f, wpn_f, wpr_f, wout_f = _forward_halves(landed, big_axes, name="gather_forward_%d" % l)
        u = _mm(h, win_f, tm=1152, tn=1024, name="in_proj_%d" % l)
        o_na = _na_fwd(u, bias, s_len=s_len, heads=na_heads, name="na_fwd_%d" % l)
        o_ret, states = _ret_fwd(u, c2, s2, log_gamma[l], s_len=s_len, heads=ret_heads, q_off=o_retq, name="ret_fwd_%d" % l)

        def act(o1, z1, o2, z2):
            a1 = o1.astype(F32) * _silu_parts(z1.astype(F32))[0]
            sz = _silu_parts(z2.astype(F32))[0]
            outs = []
            for hh in range(ret_heads):
                sl = slice(hh * RET_VAL_DIM, (hh + 1) * RET_VAL_DIM)
                oh = o2[:, sl]
                r = lax.rsqrt(jnp.mean(oh * oh, axis=-1, keepdims=True) + NORM_EPS)
                outs.append(oh * r * sz[:, sl])
            return a1, jnp.concatenate(outs, axis=-1)

        a_na, a_ret = _ew(act, [('t', o_na, 0, w_na), ('t', u, o_naz // w_na, w_na), ('t', o_ret, 0, w_v), ('t', u, o_retz // w_v, w_v)],
                          [('t', w_na, BF16), ('t', w_v, BF16)], rows=t_len, tr=tr, name="act_%d" % l)
        if l == 0:
            landed_rest = _split_wait(rest_gather, a_na, plan_rest, name="gather_wait_0_rest")
            wpn_f, wpr_f, wout_f = _forward_halves(landed_rest, big_axes[1:], name="gather_forward_0_rest")
        full_w.append((win_f, wpn_f, wpr_f, wout_f))
        y_na = _mm(a_na, wpn_f, name="proj_na_%d" % l)
        y_ret = _mm(a_ret, wpr_f, name="proj_ret_%d" % l)

        def merge(y1, y2, g1, g2):
            return _sigmoid(g1.astype(F32)) * y1.astype(F32) + _sigmoid(g2.astype(F32)) * y2.astype(F32)

        merged, = _ew(merge, [('t', y_na, 0, d_model), ('t', y_ret, 0, d_model), ('t', u, o_gna // d_model, d_model), ('t', u, o_gret // d_model, d_model)],
                      [('t', d_model, BF16)], rows=t_len, tr=tr, name="merge_%d" % l)
        out = _mm(merged, wout_f, out_dtype=F32, name="out_proj_%d" % l)
        x_new, = _ew(lambda xt, ot, gt: xt + gt * ot, [('t', x_all, 0, d_model), ('t', out, 0, d_model), ('g', gate)],
                     [('t', d_model, F32)], rows=t_len, tr=tr, n0=n0, name="resid_%d" % l)
        saved.append(dict(x=x_all, h=h, u=u, bias=bias, o_na=o_na, o_ret=o_ret, states=states, a_na=a_na, a_ret=a_ret,
                          y_na=y_na, y_ret=y_ret, merged=merged, out=out, gate=gate, gs=gs, scale=scale))
        x_all = x_new

    def final(xt, tt, gt):
        r = lax.rsqrt(jnp.mean(xt * xt, axis=-1, keepdims=True) + NORM_EPS)
        xh = xt * r
        e = xh * gt - tt
        dy = e * (1.0 / d_model)
        dyg = dy * gt
        dx = r * (dyg - xh * jnp.mean(dyg * xh, axis=-1, keepdims=True))
        return dx, _rsum(dy * xh), _rsum(e * e)

    dx_lat, d_final_g, loss_cols = _ew(final, [('t', x_all, 0, d_model), ('t', loss_target[0], 0, d_model), ('g', final_g[None, None, :])],
                                       [('t', d_model, F32), ('r', d_model, 1), ('r', d_model, 1)], rows=s_len, tr=tr, name="final")
    loss_part = (0.5 / d_model) * jnp.sum(loss_cols)
    dx_all = jnp.concatenate([dx_lat, jnp.zeros((l_len, d_model), F32)], axis=0)

    big_w = [(w_in, m_w_in, v_w_in), (w_proj_na, m_w_proj_na, v_w_proj_na), (w_proj_ret, m_w_proj_ret, v_w_proj_ret), (w_out, m_w_out, v_w_out)]
    big_res = [None] * 4
    scatters = {}
    back_token = jnp.zeros((), F32)

    def start_scatter(key, grads, axes):
        plan = _scatter_plan(axes)
        theirs = _send_other_half(grads, axes, name="pair_exchange_%s" % key)
        pair = [_sum_pair(g, t, ax, ci, name="sum_pair_%s_%d" % (key, i)) for i, (g, t, ax) in enumerate(zip(grads, theirs, axes))]
        own = [lax.dynamic_slice_in_dim(s, chip * (s.shape[ax] // N_CHIPS), s.shape[ax] // N_CHIPS, axis=ax) for s, ax in zip(pair, axes)]
        lands = [lax.empty((N_CHIPS - 1,) + o.shape, BF16) for o in own]
        started = _split_start(pair, lands, plan, len(axes) * (N_CHIPS - 1), name="scatter_start_%s" % key)
        scatters[key] = (started, own, axes, plan)
        return started[4]

    def finish_scatter(key, after):
        started, own, axes, plan = scatters[key]
        recv = _split_wait(started, after, plan, name="scatter_wait_%s" % key)
        bufs = [_sum_chips_into(own[i], rbuf, axes[i], ci, name="sum_chips_%s_%d" % (key, i)) for i, rbuf in enumerate(recv)]
        return _share_halves_in_place(bufs, axes, name="share_halves_%s" % key)

    def adamw_big(l, idx, grads, big_res):
        for i, g in zip(idx, grads):
            w3, m3, v3 = big_w[i]
            big_res[i] = _adamw_layer(w3, m3, v3, g, None, l, big_res[i], name="adamw_big_%d_%d" % (i, l))
        return big_res

    small = dict(dmod_lat=[None] * depth, dmod_ctx=[None] * depth, dnorm_g=[None] * depth, drpb=[None] * depth, ddecay=[None] * depth)
    for l in reversed(range(depth)):
        sv = saved[l]
        win_f, wpn_f, wpr_f, wout_f = full_w[l]

        def resid_bwd(dxt, ot, gt):
            return gt * dxt, _rsum(dxt * ot)

        dout, dgate = _ew(resid_bwd, [('t', dx_all, 0, d_model), ('t', sv['out'], 0, d_model), ('g', sv['gate'] + back_token)],
                          [('t', d_model, BF16), ('r', d_model, 2)], rows=t_len, tr=tr, n0=n0, name="resid_bwd_%d" % l)
        dmerged = _mm(dout, wout_f, tb=True, name="out_proj_dx_%d" % l)
        g_wout = _mm(sv['merged'], dout, ta=True, tm=512, tk=t_len, name="out_proj_dw_%d" % l)

        def merge_bwd(dm, y1, y2, g1, g2):
            dm = dm.astype(F32)
            s1, s2_ = _sigmoid(g1.astype(F32)), _sigmoid(g2.astype(F32))
            return dm * s1, dm * s2_, dm * y1.astype(F32) * s1 * (1.0 - s1), dm * y2.astype(F32) * s2_ * (1.0 - s2_)

        u = sv['u']
        dy_na, dy_ret, dg_na, dg_ret = _ew(
            merge_bwd, [('t', dmerged, 0, d_model), ('t', sv['y_na'], 0, d_model), ('t', sv['y_ret'], 0, d_model),
                        ('t', u, o_gna // d_model, d_model), ('t', u, o_gret // d_model, d_model)],
            [('t', d_model, BF16)] * 4, rows=t_len, tr=tr, name="merge_bwd_%d" % l)
        da_na = _mm(dy_na, wpn_f, tb=True, name="proj_na_dx_%d" % l)
        g_wpn = _mm(sv['a_na'], dy_na, ta=True, tm=512, tk=t_len, name="proj_na_dw_%d" % l)
        da_ret = _mm(dy_ret, wpr_f, tb=True, name="proj_ret_dx_%d" % l)
        g_wpr = _mm(sv['a_ret'], dy_ret, ta=True, tm=512, tk=t_len, name="proj_ret_dw_%d" % l)
        lg_l = log_gamma[l]
        if l == 0:
            lg_l = lg_l + start_scatter("0_rest", [g_wpn, g_wpr, g_wout], big_axes[1:])[0, 0]

        def act_bwd(da1, o1, z1, da2, o2, z2):
            da1, da2 = da1.astype(F32), da2.astype(F32)
            si1, ds1 = _silu_parts(z1.astype(F32))
            si2, ds2 = _silu_parts(z2.astype(F32))
            do1 = da1 * si1
            dz1 = da1 * o1.astype(F32) * ds1
            dn = da2 * si2
            do2, dz2 = [], []
            for hh in range(ret_heads):
                sl = slice(hh * RET_VAL_DIM, (hh + 1) * RET_VAL_DIM)
                oh = o2[:, sl]
                r = lax.rsqrt(jnp.mean(oh * oh, axis=-1, keepdims=True) + NORM_EPS)
                nh = oh * r
                dz2.append(da2[:, sl] * nh * ds2[:, sl])
                do2.append(r * (dn[:, sl] - nh * jnp.mean(dn[:, sl] * nh, axis=-1, keepdims=True)))
            return do1, dz1, jnp.concatenate(do2, axis=-1), jnp.concatenate(dz2, axis=-1)

        do_na, dz_na, do_ret, dz_ret = _ew(
            act_bwd, [('t', da_na, 0, w_na), ('t', sv['o_na'], 0, w_na), ('t', u, o_naz // w_na, w_na),
                      ('t', da_ret, 0, w_v), ('t', sv['o_ret'], 0, w_v), ('t', u, o_retz // w_v, w_v)],
            [('t', w_na, BF16), ('t', w_na, BF16), ('t', w_v, BF16), ('t', w_v, BF16)], rows=t_len, tr=tr, name="act_bwd_%d" % l)
        dq_na, dk_na, dv_na, dbias = _na_bwd(u, sv['bias'], sv['o_na'], do_na, s_len=s_len, heads=na_heads, name="na_bwd_%d" % l)
        small['drpb'][l] = _rpb_grad(dbias, name="rpb_grad_%d" % l)
        dq_r, dk_r, dv_r, dlg = _ret_bwd(u, c2, s2, lg_l, sv['states'], do_ret, s_len=s_len, heads=ret_heads,
                                         q_off=o_retq, name="ret_bwd_%d" % l)
        small['ddecay'][l] = jnp.transpose(dlg[:, :, 0, 0]) * _sigmoid(-ret_decay_logit[l])
        du_parts = [dq_na, dk_na, dv_na, dz_na, dq_r, dk_r, dv_r, dz_ret, dg_na, dg_ret]
        du, = _ew(lambda *t: jnp.concatenate(t, axis=-1), [('t', p, 0, p.shape[1]) for p in du_parts], [('t', in_cols, BF16)],
                  rows=t_len, tr=tr, name="du_concat_%d" % l)
        dh = _mm(du, win_f, tb=True, out_dtype=F32, tm=1152, tn=1024, name="in_proj_dx_%d" % l)

        def modnorm_bwd(xt, dht, dxt, gs_t):
            r = lax.rsqrt(jnp.mean(xt * xt, axis=-1, keepdims=True) + NORM_EPS)
            xh = xt * r
            dhg = dht * gs_t
            dx = r * (dhg - xh * jnp.mean(dhg * xh, axis=-1, keepdims=True)) + dxt
            return dx, _rsum(dht), _rsum(dht * xh)

        dx_all, dshift, dgs = _ew(modnorm_bwd, [('t', sv['x'], 0, d_model), ('t', dh, 0, d_model), ('t', dx_all, 0, d_model), ('g', sv['gs'])],
                                  [('t', d_model, F32), ('r', d_model, 2), ('r', d_model, 2)], rows=t_len, tr=tr, n0=n0, name="modnorm_bwd_%d" % l)
        dscale = dgs * norm_g[l][None, None, :]
        small['dnorm_g'][l] = jnp.sum(dgs * (1.0 + sv['scale']), axis=(0, 1))
        dmod = jnp.concatenate([dshift, dscale, dgate], axis=-1)[:, 0]
        small['dmod_lat'][l], small['dmod_ctx'][l] = dmod[0], dmod[1]

        if l > 0:
            g_win = _mm(sv['h'], du, ta=True, tm=1024, tn=1024, tk=t_len, name="in_proj_dw_%d" % l)
            back_token = start_scatter("%d_all" % l, [g_win, g_wpn, g_wpr, g_wout], big_axes)[0, 0]

    grad_x = dx_all[:s_len][None]

    drpb = jnp.stack(small['drpb']).reshape(-1)
    ddecay = jnp.stack(small['ddecay']).reshape(-1)
    pieces = [jnp.stack(small['dmod_lat']).reshape(-1), jnp.stack(small['dmod_ctx']).reshape(-1),
              jnp.stack(small['dnorm_g']).reshape(-1), d_final_g.reshape(-1), drpb, ddecay, loss_part[None]]
    sizes = [int(p.shape[0]) for p in pieces]
    pads = [-(-s // LANES) * LANES for s in sizes]
    packed = jnp.concatenate([jnp.pad(p, (0, pd - s)) for p, s, pd in zip(pieces, sizes, pads)])
    gathered = _all_gather_small(_pack_rows(packed), name="gather_small_grads")
    r_small = gathered.shape[1]

    def sum8(*t):
        acc = t[0]
        for other in t[1:]:
            acc = acc + other
        return acc

    total, = _ew(sum8, [('t', gathered, 0, LANES, k) for k in range(N_DEV)], [('t', LANES, F32)], rows=r_small, tr=r_small, name="sum_devices")
    total = total.reshape(-1)
    starts = np.cumsum([0] + pads)
    g_mod_lat_sum, g_mod_ctx, g_norm_g, g_final_g, g_rpb, g_decay, loss = [total[starts[i]:starts[i] + sizes[i]] for i in range(len(pieces))]
    loss = loss[0]
    g_ada_b = (g_mod_lat_sum + g_mod_ctx).reshape(depth, mod_cols)
    g_mod_ctx = g_mod_ctx.reshape(depth, mod_cols)
    dmod_lat_all = gathered.reshape(N_DEV, -1)[:, :depth * mod_cols].reshape(N_DEV, depth, mod_cols)

    dcc_part = jnp.zeros((16, d_model), F32)
    ctx_cols = [lax.dynamic_slice_in_dim(g_mod_ctx[l], chip * mod_shard, mod_shard, axis=0) for l in range(depth)]
    for l in reversed(range(depth)):
        c_rows = jnp.concatenate([ctx_cols[l][None], jnp.zeros((15, mod_shard), F32)], axis=0)
        dcc_part = dcc_part + _mm(c_rows, ada_w, tb=True, b_lead=l, out_dtype=F32, name="ada_dc_%d" % l)
    dcc_all = _all_gather_small(_pack_rows(dcc_part[0]), name="gather_dcc")[:, :d_model // LANES].reshape(N_CHIPS, 2, d_model)[:, 0]

    du0, dcc_all = lax.optimization_barrier((du, dcc_all))
    g_win0 = _mm(saved[0]['h'], du0, ta=True, tm=1024, tn=1024, tk=t_len, name="in_proj_dw_0")
    tail_token = start_scatter("0_in", [g_win0], big_axes[:1])
    dcc = ((dcc_all[0] + dcc_all[1]) + dcc_all[2]) + dcc_all[3]
    sg = _sigmoid(c_ctx)
    g_c_ctx = dcc * (sg * (1.0 + c_ctx * (1.0 - sg)))
    for l in reversed(range(1, depth)):
        big_res = adamw_big(l, range(4), finish_scatter("%d_all" % l, tail_token), big_res)

    ada_res = None
    for l in reversed(range(depth)):
        lat_cols = lax.dynamic_slice_in_dim(dmod_lat_all[:, l], chip * mod_shard, mod_shard, axis=1)
        d_rows = jnp.concatenate([lat_cols, ctx_cols[l][None], jnp.zeros((16 - N_DEV - 1, mod_shard), F32)], axis=0) + tail_token[0, 0]
        g_ada = _mm(a_rows, d_rows, ta=True, out_dtype=F32, tm=512, name="ada_dw_%d" % l)
        ada_res = _adamw_layer(ada_w, m_ada_w, v_ada_w, g_ada, None, l, ada_res, name="adamw_ada_%d" % l)

    small_w = [(c_ctx, m_c_ctx, v_c_ctx, g_c_ctx), (ada_b, m_ada_b, v_ada_b, g_ada_b),
               (norm_g, m_norm_g, v_norm_g, g_norm_g), (na_rpb, m_na_rpb, v_na_rpb, g_rpb),
               (ret_decay_logit, m_ret_decay_logit, v_ret_decay_logit, g_decay), (final_g, m_final_g, v_final_g, g_final_g)]
    sw_sizes = [int(np.prod(t[0].shape)) for t in small_w]
    sw_pads = [-(-s // LANES) * LANES for s in sw_sizes]

    def pack(j):
        return _pack_rows(jnp.concatenate([jnp.pad(t[j].reshape(-1), (0, pd - s)) for t, s, pd in zip(small_w, sw_sizes, sw_pads)]))

    pw_, pm_, pv_, pg_ = pack(0), pack(1), pack(2), pack(3)
    sw_out = _ew(lambda w, m, v, g: (g,) + _adamw_math(w, g, m, v),
                 [('t', pw_, 0, LANES), ('t', pm_, 0, LANES), ('t', pv_, 0, LANES), ('t', pg_, 0, LANES)],
                 [('t', LANES, F32)] * 4, rows=pw_.shape[0], tr=pw_.shape[0], name="adamw_small")
    sw_starts = np.cumsum([0] + sw_pads)
    sw_out, ada_res, big_res = lax.optimization_barrier((sw_out, ada_res, big_res))
    big_res = adamw_big(0, range(1, 4), finish_scatter("0_rest", sw_out[0]), big_res)
    big_res = adamw_big(0, range(1), finish_scatter("0_in", sw_out[1]), big_res)

    def unpack(arr, i):
        return arr.reshape(-1)[sw_starts[i]:sw_starts[i] + sw_sizes[i]].reshape(small_w[i][0].shape)

    sm = [[unpack(sw_out[j], i) for i in range(len(small_w))] for j in range(4)]
    def ordered(j):
        return [sm[j][0], ada_res[j], sm[j][1], sm[j][2], big_res[0][j], sm[j][3], sm[j][4],
                big_res[1][j], big_res[2][j], big_res[3][j], sm[j][5]]

    return (loss, grad_x, *ordered(0), *ordered(1), *ordered(2), *ordered(3))
```

```python
import functools
import math

import numpy as np
import jax
import jax.numpy as jnp
from jax import lax
from jax.experimental import pallas as pl
from jax.experimental.pallas import tpu as pltpu

GRID_W = 64
NA_HEAD_DIM = 128
NA_WIN_ROWS = 8
NA_WIN_COLS = 16
NA_GROUP = 8
RET_GROUPS = (1, 2, 3)
RET_KEY_DIM = 128
RET_VAL_DIM = 256
RET_CHUNK = 128
ROPE_BASE = 10000.0
NORM_EPS = 1e-6
MASK_VALUE = -1e30
ADAM_LR = 0.001
ADAM_B1 = 0.9
ADAM_B2 = 0.999
ADAM_EPS = 1e-08
ADAM_WD = 0.01
ADAM_STEP = 10

N_CHIPS = 4
N_DEV = 8
LANES = 128
VMEM_LIMIT = 56 * 1024 * 1024
BF16 = jnp.bfloat16
F32 = jnp.float32
MESH = pl.DeviceIdType.MESH
ANY = pl.BlockSpec(memory_space=pl.ANY)


def _tile(dim, pref, align=LANES):
    if dim <= pref:
        return dim
    t = (pref // align) * align
    while t >= align:
        if dim % t == 0:
            return t
        t -= align
    return dim


def _rows_per_tile(rows, width, tile_bytes=1 << 20):
    return _tile(rows, max(8, tile_bytes // (4 * width)), 8)


def _params(sem):
    return pltpu.CompilerParams(dimension_semantics=sem, vmem_limit_bytes=VMEM_LIMIT)


def _sigmoid(x):
    return 1.0 / (1.0 + jnp.exp(-x))


def _dot(a, b, ca, cb):
    return lax.dot_general(a, b, (((ca,), (cb,)), ((), ())), preferred_element_type=F32)


def _mm(a, b, *, ta=False, tb=False, a_lead=None, b_lead=None, out_dtype=BF16, tm=1152, tn=1024, tk=2048, name):
    ash = a.shape[1:] if a_lead is not None else a.shape
    bsh = b.shape[1:] if b_lead is not None else b.shape
    m, k = (ash[1], ash[0]) if ta else ash
    n, k2 = bsh if tb else (bsh[1], bsh[0])
    assert k == k2, (name, ash, bsh)
    tm, tn, tk = _tile(m, tm), _tile(n, tn), _tile(k, tk)
    nk = k // tk

    def lead(spec_shape, imap, l):
        if l is None:
            return pl.BlockSpec(spec_shape, imap)
        return pl.BlockSpec((None,) + spec_shape, lambda i, j, kk: (l,) + imap(i, j, kk))

    a_spec = lead((tk, tm), lambda i, j, kk: (kk, i), a_lead) if ta else lead((tm, tk), lambda i, j, kk: (i, kk), a_lead)
    b_spec = lead((tn, tk), lambda i, j, kk: (j, kk), b_lead) if tb else lead((tk, tn), lambda i, j, kk: (kk, j), b_lead)
    ca, cb = (0 if ta else 1), (1 if tb else 0)

    def body(a_ref, b_ref, o_ref, *scratch):
        part = _dot(a_ref[...].astype(BF16), b_ref[...].astype(BF16), ca, cb)
        if nk == 1:
            o_ref[...] = part.astype(o_ref.dtype)
            return
        acc_ref, = scratch
        kk = pl.program_id(2)

        @pl.when(kk == 0)
        def _():
            acc_ref[...] = part

        @pl.when(kk > 0)
        def _():
            acc_ref[...] += part

        @pl.when(kk == nk - 1)
        def _():
            o_ref[...] = acc_ref[...].astype(o_ref.dtype)

    return pl.pallas_call(
        body, name=name, grid=(m // tm, n // tn, nk),
        in_specs=[a_spec, b_spec],
        out_specs=pl.BlockSpec((tm, tn), lambda i, j, kk: (i, j)),
        out_shape=jax.ShapeDtypeStruct((m, n), out_dtype),
        scratch_shapes=[] if nk == 1 else [pltpu.VMEM((tm, tn), F32)],
        compiler_params=_params(("parallel", "parallel", "arbitrary")),
    )(a, b)


def _ew(fn, ins, outs, *, rows, tr, name, n0=None, aliases=None):
    assert rows % tr == 0, (name, rows, tr)
    nt = rows // tr

    def grp(i):
        return 0 if n0 is None else jnp.where(i < n0, 0, 1)

    in_specs, args = [], []
    for spec in ins:
        if spec[0] == 't':
            arr, cb, w = spec[1], spec[2], spec[3]
            l = spec[4] if len(spec) > 4 else None
            if l is None:
                in_specs.append(pl.BlockSpec((tr, w), functools.partial(lambda i, cb: (i, cb), cb=cb)))
            else:
                in_specs.append(pl.BlockSpec((None, tr, w), functools.partial(lambda i, cb, l: (l, i, cb), cb=cb, l=l)))
            args.append(arr)
        else:
            arr = spec[1]
            g = arr.shape[0]
            if g == 1:
                in_specs.append(pl.BlockSpec((None, 1, arr.shape[2]), lambda i: (0, 0, 0)))
            else:
                in_specs.append(pl.BlockSpec((None, 1, arr.shape[2]), lambda i: (grp(i), 0, 0)))
            args.append(arr)
    out_specs, out_shapes, is_red = [], [], []
    for spec in outs:
        if spec[0] == 't':
            w, dt = spec[1], spec[2]
            if len(spec) > 3:
                l, nl = spec[3], spec[4]
                out_specs.append(pl.BlockSpec((None, tr, w), functools.partial(lambda i, l: (l, i, 0), l=l)))
                out_shapes.append(jax.ShapeDtypeStruct((nl, rows, w), dt))
            else:
                out_specs.append(pl.BlockSpec((tr, w), lambda i: (i, 0)))
                out_shapes.append(jax.ShapeDtypeStruct((rows, w), dt))
            is_red.append(False)
        else:
            w, g = spec[1], spec[2]
            if g == 1:
                out_specs.append(pl.BlockSpec((None, 1, w), lambda i: (0, 0, 0)))
            else:
                out_specs.append(pl.BlockSpec((None, 1, w), lambda i: (grp(i), 0, 0)))
            out_shapes.append(jax.ShapeDtypeStruct((g, 1, w), F32))
            is_red.append(True)
    n_in = len(ins)
    n_alias = 0 if aliases is None else len(aliases)

    def body(*refs):
        in_refs = refs[:n_in]
        out_refs = refs[n_in + n_alias:]
        res = fn(*[r[...] for r in in_refs])
        if not isinstance(res, (tuple, list)):
            res = (res,)
        i = pl.program_id(0)
        first = (i == 0) if n0 is None else ((i == 0) | (i == n0))
        for o_ref, val, red in zip(out_refs, res, is_red):
            if not red:
                o_ref[...] = val.astype(o_ref.dtype)
            else:
                @pl.when(first)
                def _(o_ref=o_ref, val=val):
                    o_ref[...] = val

                @pl.when(jnp.logical_not(first))
                def _(o_ref=o_ref, val=val):
                    o_ref[...] += val

    io_alias = {}
    if aliases is not None:
        for a_idx, (arr, o_idx) in enumerate(aliases):
            in_specs.append(ANY)
            args.append(arr)
            io_alias[n_in + a_idx] = o_idx
    has_red = any(is_red)
    return pl.pallas_call(
        body, name=name, grid=(nt,), in_specs=in_specs, out_specs=out_specs, out_shape=out_shapes,
        input_output_aliases=io_alias,
        compiler_params=_params(("arbitrary",) if has_red else ("parallel",)),
    )(*args)


def _half_spec(tr, width, ax, n_tiles):
    if ax == 1:
        return pl.BlockSpec((tr, width), lambda i, sel: (sel[0] * n_tiles + i, 0))
    return pl.BlockSpec((tr, width), lambda i, sel: (i, sel[0]))


def _sum_pair(g, theirs, ax, ci, *, name):
    pr, pw = theirs.shape
    tr = _rows_per_tile(pr, pw)
    nt = pr // tr

    def body(sel, a_ref, b_ref, o_ref):
        o_ref[...] = (a_ref[...].astype(F32) + b_ref[...].astype(F32)).astype(o_ref.dtype)

    return pl.pallas_call(
        body, name=name,
        grid_spec=pltpu.PrefetchScalarGridSpec(
            num_scalar_prefetch=1, grid=(nt,),
            in_specs=[_half_spec(tr, pw, ax, nt), pl.BlockSpec((tr, pw), lambda i, sel: (i, 0))],
            out_specs=pl.BlockSpec((tr, pw), lambda i, sel: (i, 0))),
        out_shape=jax.ShapeDtypeStruct((pr, pw), BF16),
        compiler_params=_params(("parallel",)),
    )(jnp.reshape(ci, (1,)).astype(jnp.int32), g, theirs)


def _sum_chips_into(own, recv, ax, ci, *, name):
    pr, pw = own.shape
    tr = _rows_per_tile(pr, pw)
    nt = pr // tr
    full_shape = (2 * pr, pw) if ax == 1 else (pr, 2 * pw)

    def body(sel, a_ref, r_ref, o_ref):
        acc = a_ref[...].astype(F32)
        for k in range(N_CHIPS - 1):
            acc = acc + r_ref[k].astype(F32)
        o_ref[...] = acc

    return pl.pallas_call(
        body, name=name,
        grid_spec=pltpu.PrefetchScalarGridSpec(
            num_scalar_prefetch=1, grid=(nt,),
            in_specs=[pl.BlockSpec((tr, pw), lambda i, sel: (i, 0)), pl.BlockSpec((N_CHIPS - 1, tr, pw), lambda i, sel: (0, i, 0))],
            out_specs=_half_spec(tr, pw, ax, nt)),
        out_shape=jax.ShapeDtypeStruct(full_shape, F32),
        compiler_params=_params(("parallel",)),
    )(jnp.reshape(ci, (1,)).astype(jnp.int32), own, recv)


def _rsum(v):
    return jnp.sum(v, axis=0, keepdims=True)


def _silu_parts(z):
    sg = _sigmoid(z)
    return z * sg, sg * (1.0 + z * (1.0 - sg))


def _na_bias_table(rpb, rows, *, name):
    kh, kw = NA_WIN_ROWS, NA_WIN_COLS
    assert rows >= kh
    heads = rpb.shape[0]
    e1, e2 = _na_onehots()
    rpb16 = jnp.pad(rpb, ((0, 0), (0, 16 - rpb.shape[1]), (0, LANES - rpb.shape[2])))

    def body(r_ref, e1_ref, e2_ref, o_ref):
        e1b = e1_ref[...].astype(BF16)
        y = sum(_dot(e1b, part, 0, 0) for part in _split3(r_ref[...]))
        e2b = e2_ref[...].astype(BF16)
        o_ref[...] = sum(_dot(part, e2b, 1, 1) for part in _split3(y))

    z = pl.pallas_call(
        body, name=name, grid=(heads,),
        in_specs=[pl.BlockSpec((None, 16, LANES), lambda h: (h, 0, 0)),
                  pl.BlockSpec(e1.shape, lambda h: (0, 0)), pl.BlockSpec(e2.shape, lambda h: (0, 0))],
        out_specs=pl.BlockSpec((None, kh * kh, GRID_W * GRID_W), lambda h: (h, 0, 0)),
        out_shape=jax.ShapeDtypeStruct((heads, kh * kh, GRID_W * GRID_W), F32),
        compiler_params=_params(("parallel",)),
    )(rpb16, e1, e2)
    cidx = np.arange(GRID_W)
    c0 = np.clip(cidx - kw // 2, 0, GRID_W - kw)
    col_in = (cidx[None, :] >= c0[:, None]) & (cidx[None, :] < c0[:, None] + kw)
    bias = z.reshape(heads, kh, kh, GRID_W, GRID_W).transpose(0, 1, 3, 2, 4)
    bias = jnp.where(col_in[None, None, :, None, :], bias, MASK_VALUE)
    return bias.reshape(heads, kh, GRID_W, kh * GRID_W)


def _na_onehots():
    kh, kw = NA_WIN_ROWS, NA_WIN_COLS
    cidx = np.arange(GRID_W)
    dc = cidx[None, :] - cidx[:, None] + (kw - 1)
    e2 = np.zeros((GRID_W * GRID_W, LANES), np.float32)
    ok = (dc >= 0) & (dc <= 2 * kw - 2)
    cq, ck = np.nonzero(ok)
    e2[cq * GRID_W + ck, dc[cq, ck]] = 1.0
    dr = np.arange(kh)[None, :] - np.arange(kh)[:, None] + (kh - 1)
    e1 = np.zeros((16, kh * kh), np.float32)
    dl, kr = np.nonzero(np.ones_like(dr))
    e1[dr[dl, kr], dl * kh + kr] = 1.0
    return jnp.asarray(e1), jnp.asarray(e2)


def _na_fwd(u, bias, *, s_len, heads, name):
    t_len = u.shape[0]
    rows = s_len // GRID_W
    nloc = NA_WIN_ROWS * GRID_W
    scale = NA_HEAD_DIM ** -0.5
    hd = NA_HEAD_DIM

    def body(q_ref, k_ref, v_ref, b_ref, o_ref):
        kc = k_ref[s_len:t_len, :]
        vc = v_ref[s_len:t_len, :]

        def group(g, carry):
            rs = [g * NA_GROUP + i for i in range(NA_GROUP)]
            r0s = [jnp.clip(r - NA_WIN_ROWS // 2, 0, rows - NA_WIN_ROWS) for r in rs]
            gs_ = pl.multiple_of(g * (NA_GROUP * GRID_W), NA_GROUP * GRID_W)
            kss = [pl.multiple_of(r0 * GRID_W, GRID_W) for r0 in r0s]
            q_all = q_ref[pl.ds(gs_, NA_GROUP * GRID_W), :]
            s_ctx = _dot(q_all, kc, 1, 1) * scale
            s_loc = [_dot(q_all[i * GRID_W:(i + 1) * GRID_W], k_ref[pl.ds(kss[i], nloc), :], 1, 1) * scale + b_ref[rs[i] - r0s[i]]
                     for i in range(NA_GROUP)]
            p_loc, p_ctx, inv = [], [], []
            for i in range(NA_GROUP):
                sc = s_ctx[i * GRID_W:(i + 1) * GRID_W]
                m = jnp.maximum(jnp.max(s_loc[i], axis=-1, keepdims=True), jnp.max(sc, axis=-1, keepdims=True))
                pl_, pc_ = jnp.exp(s_loc[i] - m), jnp.exp(sc - m)
                inv.append(1.0 / (jnp.sum(pl_, axis=-1, keepdims=True) + jnp.sum(pc_, axis=-1, keepdims=True)))
                p_loc.append(pl_.astype(BF16))
                p_ctx.append(pc_.astype(BF16))
            o_ctx = _dot(jnp.concatenate(p_ctx, axis=0), vc, 1, 0)
            o_loc = [_dot(p_loc[i], v_ref[pl.ds(kss[i], nloc), :], 1, 0) for i in range(NA_GROUP)]
            out = jnp.concatenate([(o_loc[i] + o_ctx[i * GRID_W:(i + 1) * GRID_W]) * inv[i] for i in range(NA_GROUP)], axis=0)
            o_ref[pl.ds(gs_, NA_GROUP * GRID_W), :] = out.astype(o_ref.dtype)
            return carry

        lax.fori_loop(0, rows // NA_GROUP, group, 0)
        qc = q_ref[s_len:t_len, :]
        s = _dot(qc, kc, 1, 1) * scale
        p = jnp.exp(s - jnp.max(s, axis=-1, keepdims=True))
        o = _dot(p.astype(BF16), vc, 1, 0) / jnp.sum(p, axis=-1, keepdims=True)
        o_ref[s_len:t_len, :] = o.astype(o_ref.dtype)

    col = lambda off: pl.BlockSpec((t_len, hd), functools.partial(lambda h, off: (0, off + h), off=off))
    return pl.pallas_call(
        body, name=name, grid=(heads,),
        in_specs=[col(0), col(heads), col(2 * heads),
                  pl.BlockSpec((None, NA_WIN_ROWS, GRID_W, nloc), lambda h: (h, 0, 0, 0))],
        out_specs=pl.BlockSpec((t_len, hd), lambda h: (0, h)),
        out_shape=jax.ShapeDtypeStruct((t_len, heads * hd), BF16),
        compiler_params=_params(("parallel",)),
    )(u, u, u, bias)


def _na_bwd(u, bias, o, do, *, s_len, heads, name):
    t_len = u.shape[0]
    rows = s_len // GRID_W
    nloc = NA_WIN_ROWS * GRID_W
    scale = NA_HEAD_DIM ** -0.5
    hd = NA_HEAD_DIM

    def body(q_ref, k_ref, v_ref, b_ref, o_ref, do_ref, dq_ref, dk_ref, dv_ref, db_ref, dk_acc, dv_acc):
        kc = k_ref[s_len:t_len, :]
        vc = v_ref[s_len:t_len, :]
        dk_acc[...] = jnp.zeros_like(dk_acc)
        dv_acc[...] = jnp.zeros_like(dv_acc)
        db_ref[...] = jnp.zeros_like(db_ref)

        def group(g, carry):
            n_g, rw = NA_GROUP, GRID_W
            rs = [g * n_g + i for i in range(n_g)]
            r0s = [jnp.clip(r - NA_WIN_ROWS // 2, 0, rows - NA_WIN_ROWS) for r in rs]
            dls = [r - r0 for r, r0 in zip(rs, r0s)]
            gs_ = pl.ds(pl.multiple_of(g * (n_g * rw), n_g * rw), n_g * rw)
            kss = [pl.ds(pl.multiple_of(r0 * rw, rw), nloc) for r0 in r0s]
            row_of = lambda a, i: a[i * rw:(i + 1) * rw]
            q_all, do_all = q_ref[gs_, :], do_ref[gs_, :]
            dlt_all = jnp.sum(do_all.astype(F32) * o_ref[gs_, :].astype(F32), axis=-1, keepdims=True)
            s_ctx = _dot(q_all, kc, 1, 1) * scale
            dp_ctx = _dot(do_all, vc, 1, 1)
            s_loc = [_dot(row_of(q_all, i), k_ref[kss[i], :], 1, 1) * scale + b_ref[dls[i]] for i in range(n_g)]
            dp_loc = [_dot(row_of(do_all, i), v_ref[kss[i], :], 1, 1) for i in range(n_g)]
            p_loc_b, ds_loc_b, p_ctx_b, ds_ctx_b = [], [], [], []
            for i in range(n_g):
                sc, dlt = row_of(s_ctx, i), row_of(dlt_all, i)
                m = jnp.maximum(jnp.max(s_loc[i], axis=-1, keepdims=True), jnp.max(sc, axis=-1, keepdims=True))
                pl_, pc_ = jnp.exp(s_loc[i] - m), jnp.exp(sc - m)
                inv = 1.0 / (jnp.sum(pl_, axis=-1, keepdims=True) + jnp.sum(pc_, axis=-1, keepdims=True))
                pl_, pc_ = pl_ * inv, pc_ * inv
                ds_l = pl_ * (dp_loc[i] - dlt)
                db_ref[dls[i]] += ds_l
                p_loc_b.append(pl_.astype(BF16))
                ds_loc_b.append(ds_l.astype(BF16))
                p_ctx_b.append(pc_.astype(BF16))
                ds_ctx_b.append((pc_ * (row_of(dp_ctx, i) - dlt)).astype(BF16))
            p_ctx_all, ds_ctx_all = jnp.concatenate(p_ctx_b, axis=0), jnp.concatenate(ds_ctx_b, axis=0)
            dq_ctx = _dot(ds_ctx_all, kc, 1, 0)
            dq_loc = [_dot(ds_loc_b[i], k_ref[kss[i], :], 1, 0) for i in range(n_g)]
            dk_loc = [_dot(ds_loc_b[i], row_of(q_all, i), 0, 0) for i in range(n_g)]
            dv_loc = [_dot(p_loc_b[i], row_of(do_all, i), 0, 0) for i in range(n_g)]
            dk_ctx = _dot(ds_ctx_all, q_all, 0, 0)
            dv_ctx = _dot(p_ctx_all, do_all, 0, 0)
            dq_ref[gs_, :] = ((jnp.concatenate(dq_loc, axis=0) + dq_ctx) * scale).astype(dq_ref.dtype)
            for i in range(n_g):
                dk_acc[kss[i], :] += dk_loc[i] * scale
                dv_acc[kss[i], :] += dv_loc[i]
            dk_acc[s_len:t_len, :] += dk_ctx * scale
            dv_acc[s_len:t_len, :] += dv_ctx
            return carry

        lax.fori_loop(0, rows // NA_GROUP, group, 0)
        qc = q_ref[s_len:t_len, :]
        dout = do_ref[s_len:t_len, :]
        out = o_ref[s_len:t_len, :]
        s = _dot(qc, kc, 1, 1) * scale
        p = jnp.exp(s - jnp.max(s, axis=-1, keepdims=True))
        p = p / jnp.sum(p, axis=-1, keepdims=True)
        dlt = jnp.sum(dout.astype(F32) * out.astype(F32), axis=-1, keepdims=True)
        ds = (p * (_dot(dout, vc, 1, 1) - dlt)).astype(BF16)
        dq_ref[s_len:t_len, :] = (_dot(ds, kc, 1, 0) * scale).astype(dq_ref.dtype)
        dk_acc[s_len:t_len, :] += _dot(ds, qc, 0, 0) * scale
        dv_acc[s_len:t_len, :] += _dot(p.astype(BF16), dout, 0, 0)
        dk_ref[...] = dk_acc[...].astype(dk_ref.dtype)
        dv_ref[...] = dv_acc[...].astype(dv_ref.dtype)

    col = lambda off: pl.BlockSpec((t_len, hd), functools.partial(lambda h, off: (0, off + h), off=off))
    tbl = pl.BlockSpec((None, NA_WIN_ROWS, GRID_W, nloc), lambda h: (h, 0, 0, 0))
    tok = jax.ShapeDtypeStruct((t_len, heads * hd), BF16)
    return pl.pallas_call(
        body, name=name, grid=(heads,),
        in_specs=[col(0), col(heads), col(2 * heads), tbl, col(0), col(0)],
        out_specs=[col(0), col(0), col(0), tbl],
        out_shape=[tok, tok, tok, jax.ShapeDtypeStruct(bias.shape, F32)],
        scratch_shapes=[pltpu.VMEM((t_len, hd), F32), pltpu.VMEM((t_len, hd), F32)],
        compiler_params=_params(("parallel",)),
    )(u, u, u, bias, o, do)


def _split3(x):
    hi = x.astype(BF16)
    r1 = x - hi.astype(F32)
    mid = r1.astype(BF16)
    lo = (r1 - mid.astype(F32)).astype(BF16)
    return hi, mid, lo


def _rpb_grad(dbias, *, name):
    heads = dbias.shape[0]
    kh = NA_WIN_ROWS
    e1, e2 = _na_onehots()
    x = dbias.reshape(heads, kh, GRID_W, kh, GRID_W).transpose(0, 1, 3, 2, 4).reshape(heads, kh * kh, GRID_W * GRID_W)

    def body(x_ref, e1_ref, e2_ref, o_ref):
        e2b = e2_ref[...].astype(BF16)
        y = sum(_dot(part, e2b, 1, 0) for part in _split3(x_ref[...]))
        e1b = e1_ref[...].astype(BF16)
        o_ref[...] = sum(_dot(e1b, part, 1, 0) for part in _split3(y))

    out = pl.pallas_call(
        body, name=name, grid=(heads,),
        in_specs=[pl.BlockSpec((None, kh * kh, GRID_W * GRID_W), lambda h: (h, 0, 0)),
                  pl.BlockSpec(e1.shape, lambda h: (0, 0)), pl.BlockSpec(e2.shape, lambda h: (0, 0))],
        out_specs=pl.BlockSpec((None, 16, LANES), lambda h: (h, 0, 0)),
        out_shape=jax.ShapeDtypeStruct((heads, 16, LANES), F32),
        compiler_params=_params(("parallel",)),
    )(x, e1, e2)
    return out[:, :2 * kh - 1, :2 * NA_WIN_COLS - 1]


def _rope_tables(s_len, l_len):
    nf = RET_KEY_DIM // 4
    t = np.arange(s_len)
    row = (t // GRID_W).astype(np.float32)
    colp = (t % GRID_W).astype(np.float32)
    inv_freq = jnp.asarray(ROPE_BASE, F32) ** (-jnp.arange(nf, dtype=F32) / nf)
    ang = jnp.concatenate([jnp.asarray(row)[:, None] * inv_freq, jnp.asarray(colp)[:, None] * inv_freq], axis=-1)
    cos, sin = jnp.cos(ang), jnp.sin(ang)
    c2 = jnp.concatenate([cos, cos], axis=-1)
    s2 = jnp.concatenate([-sin, sin], axis=-1)
    c2 = jnp.concatenate([c2, jnp.ones((l_len, RET_KEY_DIM), F32)], axis=0)
    s2 = jnp.concatenate([s2, jnp.zeros((l_len, RET_KEY_DIM), F32)], axis=0)
    return c2, s2


def _rope(x, c2, s2):
    return x * c2 + pltpu.roll(x, RET_KEY_DIM // 2, 1) * s2


def _rope_t(d, c2, s2):
    return d * c2 + pltpu.roll(d * s2, RET_KEY_DIM // 2, 1)


def _ret_decays(lg, direction):
    cs = RET_CHUNK
    i_col = lax.broadcasted_iota(jnp.int32, (cs, 1), 0)
    p_col = jnp.where(direction == 0, i_col, cs - 1 - i_col).astype(F32)
    pi = lax.broadcasted_iota(jnp.int32, (cs, cs), 0)
    pj = lax.broadcasted_iota(jnp.int32, (cs, cs), 1)
    diff = jnp.where(direction == 0, pi - pj, pj - pi).astype(F32)
    dm = jnp.where(diff >= 0, jnp.exp(jnp.maximum(diff, 0.0) * lg), 0.0)
    qdec = jnp.exp((p_col + 1.0) * lg)
    kdec = jnp.exp((cs - 1.0 - p_col) * lg)
    cd = jnp.exp(jnp.full((1, 1), cs, F32) * lg)
    return p_col, dm, qdec, kdec, cd


def _ret_chunk_index(t, direction, n_chunks, lat_chunks):
    return jnp.where(direction == 0, lax.rem(t + lat_chunks, n_chunks), n_chunks - 1 - t)


def _ret_fwd(u, c2, s2, lg, *, s_len, heads, q_off, name):
    t_len = u.shape[0]
    cs, dk, dv = RET_CHUNK, RET_KEY_DIM, RET_VAL_DIM
    n_chunks, lat_chunks = t_len // cs, s_len // cs
    k_scale = dk ** -0.5
    qb, kb, vb = q_off // dk, q_off // dk + heads, (q_off + 2 * heads * dk) // dv

    def body(lg_ref, q_ref, k_ref, v_ref, c_ref, s_ref, o_ref, st_ref, qd_s, kv_s):
        h, d = pl.program_id(0), pl.program_id(1)
        _, dm, qdec, kdec, cd = _ret_decays(lg_ref[d, h], d)
        n_g = max(g for g in RET_GROUPS if n_chunks % g == 0)
        rows_of = lambda c: pl.ds(pl.multiple_of(c * cs, cs), cs)

        def local(gi, carry):
            rws = [rows_of(gi * n_g + j) for j in range(n_g)]
            qcs = [_rope(q_ref[r, :].astype(F32), c_ref[r, :], s_ref[r, :]) for r in rws]
            kcs = [_rope(k_ref[r, :].astype(F32), c_ref[r, :], s_ref[r, :]) * k_scale for r in rws]
            vcs = [v_ref[r, :] for r in rws]
            a_raw = [_dot(qcs[j].astype(BF16), kcs[j].astype(BF16), 1, 1) for j in range(n_g)]
            kv = [_dot((kcs[j] * kdec).astype(BF16), vcs[j], 0, 0) for j in range(n_g)]
            inner = [_dot((a_raw[j] * dm).astype(BF16), vcs[j], 1, 0) for j in range(n_g)]
            for j in range(n_g):
                qd_s[rws[j], :] = (qcs[j] * qdec).astype(BF16)
                kv_s[gi * n_g + j] = kv[j]

            @pl.when(d == 0)
            def _():
                for j in range(n_g):
                    o_ref[rws[j], :] = inner[j]

            @pl.when(d == 1)
            def _():
                for j in range(n_g):
                    o_ref[rws[j], :] += inner[j]

            return carry

        lax.fori_loop(0, n_chunks // n_g, local, 0)

        def scan(t, st):
            st_ref[t] = st
            return st * cd + kv_s[_ret_chunk_index(t, d, n_chunks, lat_chunks)]

        lax.fori_loop(0, n_chunks, scan, jnp.zeros((dk, dv), F32))

        def cross(gi, carry):
            ts = [gi * n_g + j for j in range(n_g)]
            rws = [rows_of(_ret_chunk_index(t, d, n_chunks, lat_chunks)) for t in ts]
            outs = [_dot(qd_s[rws[j], :], st_ref[ts[j]].astype(BF16), 1, 0) for j in range(n_g)]
            for j in range(n_g):
                o_ref[rws[j], :] += outs[j]
            return carry

        lax.fori_loop(0, n_chunks // n_g, cross, 0)

    return pl.pallas_call(
        body, name=name, grid=(heads, 2),
        in_specs=[pl.BlockSpec(memory_space=pltpu.SMEM),
                  pl.BlockSpec((t_len, dk), lambda h, d: (0, qb + h)),
                  pl.BlockSpec((t_len, dk), lambda h, d: (0, kb + h)),
                  pl.BlockSpec((t_len, dv), lambda h, d: (0, vb + h)),
                  pl.BlockSpec((t_len, dk), lambda h, d: (0, 0)),
                  pl.BlockSpec((t_len, dk), lambda h, d: (0, 0))],
        out_specs=[pl.BlockSpec((t_len, dv), lambda h, d: (0, h)),
                   pl.BlockSpec((None, None, n_chunks, dk, dv), lambda h, d: (h, d, 0, 0, 0))],
        out_shape=[jax.ShapeDtypeStruct((t_len, heads * dv), F32),
                   jax.ShapeDtypeStruct((heads, 2, n_chunks, dk, dv), F32)],
        scratch_shapes=[pltpu.VMEM((t_len, dk), BF16), pltpu.VMEM((n_chunks, dk, dv), F32)],
        compiler_params=_params(("parallel", "arbitrary")),
    )(lg, u, u, u, c2, s2)


def _ret_bwd(u, c2, s2, lg, states, do, *, s_len, heads, q_off, name):
    t_len = u.shape[0]
    cs, dk, dv = RET_CHUNK, RET_KEY_DIM, RET_VAL_DIM
    n_chunks, lat_chunks = t_len // cs, s_len // cs
    k_scale = dk ** -0.5
    qb, kb, vb = q_off // dk, q_off // dk + heads, (q_off + 2 * heads * dk) // dv

    def body(lg_ref, q_ref, k_ref, v_ref, c_ref, s_ref, st_ref, do_ref, dq_ref, dk_ref, dv_ref, dlg_ref, acc, qdo_s, dst_s):
        h, d = pl.program_id(0), pl.program_id(1)
        p_col, dm, qdec, kdec, cd = _ret_decays(lg_ref[d, h], d)
        acc[...] = jnp.zeros_like(acc)
        n_g = max(g for g in RET_GROUPS[:2] if n_chunks % g == 0)
        rows_of = lambda c: pl.ds(pl.multiple_of(c * cs, cs), cs)
        chunk_of = lambda t: _ret_chunk_index(t, d, n_chunks, lat_chunks)

        def local(gi, carry):
            rws = [rows_of(gi * n_g + j) for j in range(n_g)]
            qds = [(_rope(q_ref[r, :].astype(F32), c_ref[r, :], s_ref[r, :]) * qdec).astype(BF16) for r in rws]
            prods = [_dot(qds[j], do_ref[rws[j], :].astype(BF16), 0, 0) for j in range(n_g)]
            for j in range(n_g):
                qdo_s[gi * n_g + j] = prods[j]
            return carry

        lax.fori_loop(0, n_chunks // n_g, local, 0)

        def scan(i, dst):
            t = n_chunks - 1 - i
            dst_s[t] = dst
            return dst * cd + qdo_s[chunk_of(t)]

        lax.fori_loop(0, n_chunks, scan, jnp.zeros((dk, dv), F32))

        def grads(gi, carry):
            ts = [gi * n_g + j for j in range(n_g)]
            rws = [rows_of(chunk_of(t)) for t in ts]
            ccs, sss = [c_ref[r, :] for r in rws], [s_ref[r, :] for r in rws]
            qcs = [_rope(q_ref[r, :].astype(F32), cc, ss) for r, cc, ss in zip(rws, ccs, sss)]
            kcs = [_rope(k_ref[r, :].astype(F32), cc, ss) * k_scale for r, cc, ss in zip(rws, ccs, sss)]
            vcs = [v_ref[r, :] for r in rws]
            docs = [do_ref[r, :].astype(BF16) for r in rws]
            sts = [st_ref[t] for t in ts]
            dsts = [dst_s[t] for t in ts]
            q16 = [x.astype(BF16) for x in qcs]
            k16 = [x.astype(BF16) for x in kcs]
            dst16 = [x.astype(BF16) for x in dsts]
            rng = range(n_g)
            a_raw = [_dot(q16[j], k16[j], 1, 1) for j in rng]
            da_raw = [_dot(docs[j], vcs[j], 1, 1) for j in rng]
            dq_c = [_dot(docs[j], sts[j].astype(BF16), 1, 1) * qdec for j in rng]
            dv_s = [_dot((kcs[j] * kdec).astype(BF16), dst16[j], 1, 0) for j in rng]
            dk_s = [_dot(vcs[j], dst16[j], 1, 1) * kdec for j in rng]
            a16 = [(a_raw[j] * dm).astype(BF16) for j in rng]
            dam = [(da_raw[j] * dm).astype(BF16) for j in rng]
            dq_i = [_dot(dam[j], k16[j], 1, 0) for j in rng]
            dk_i = [_dot(dam[j], q16[j], 0, 0) for j in rng]
            dv_i = [_dot(a16[j], docs[j], 0, 0) for j in rng]
            for j in rng:
                g = (jnp.sum(qcs[j] * (p_col * dq_i[j] + (p_col + 1.0) * dq_c[j]), axis=-1, keepdims=True)
                     + jnp.sum(kcs[j] * ((cs - 1.0 - p_col) * dk_s[j] - p_col * dk_i[j]), axis=-1, keepdims=True))
                g = (jnp.sum(g, axis=0, keepdims=True)
                     + cs * cd * jnp.sum(jnp.sum(dsts[j] * sts[j], axis=-1, keepdims=True), axis=0, keepdims=True))
                acc[...] += jnp.broadcast_to(g, acc.shape)
            dqs = [_rope_t(dq_i[j] + dq_c[j], ccs[j], sss[j]) for j in rng]
            dks = [_rope_t((dk_i[j] + dk_s[j]) * k_scale, ccs[j], sss[j]) for j in rng]
            dvs = [dv_i[j] + dv_s[j] for j in rng]

            @pl.when(d == 0)
            def _():
                for j in rng:
                    dq_ref[rws[j], :] = dqs[j].astype(dq_ref.dtype)
                    dk_ref[rws[j], :] = dks[j].astype(dk_ref.dtype)
                    dv_ref[rws[j], :] = dvs[j].astype(dv_ref.dtype)

            @pl.when(d == 1)
            def _():
                for j in rng:
                    dq_ref[rws[j], :] = (dq_ref[rws[j], :].astype(F32) + dqs[j]).astype(dq_ref.dtype)
                    dk_ref[rws[j], :] = (dk_ref[rws[j], :].astype(F32) + dks[j]).astype(dk_ref.dtype)
                    dv_ref[rws[j], :] = (dv_ref[rws[j], :].astype(F32) + dvs[j]).astype(dv_ref.dtype)

            return carry

        lax.fori_loop(0, n_chunks // n_g, grads, 0)
        dlg_ref[...] = acc[...]

    return pl.pallas_call(
        body, name=name, grid=(heads, 2),
        in_specs=[pl.BlockSpec(memory_space=pltpu.SMEM),
                  pl.BlockSpec((t_len, dk), lambda h, d: (0, qb + h)),
                  pl.BlockSpec((t_len, dk), lambda h, d: (0, kb + h)),
                  pl.BlockSpec((t_len, dv), lambda h, d: (0, vb + h)),
                  pl.BlockSpec((t_len, dk), lambda h, d: (0, 0)),
                  pl.BlockSpec((t_len, dk), lambda h, d: (0, 0)),
                  pl.BlockSpec((None, None, n_chunks, dk, dv), lambda h, d: (h, d, 0, 0, 0)),
                  pl.BlockSpec((t_len, dv), lambda h, d: (0, h))],
        out_specs=[pl.BlockSpec((t_len, dk), lambda h, d: (0, h)),
                   pl.BlockSpec((t_len, dk), lambda h, d: (0, h)),
                   pl.BlockSpec((t_len, dv), lambda h, d: (0, h)),
                   pl.BlockSpec((None, None, 8, LANES), lambda h, d: (h, d, 0, 0))],
        out_shape=[jax.ShapeDtypeStruct((t_len, heads * dk), BF16),
                   jax.ShapeDtypeStruct((t_len, heads * dk), BF16),
                   jax.ShapeDtypeStruct((t_len, heads * dv), BF16),
                   jax.ShapeDtypeStruct((heads, 2, 8, LANES), F32)],
        scratch_shapes=[pltpu.VMEM((8, LANES), F32), pltpu.VMEM((n_chunks, dk, dv), F32), pltpu.VMEM((n_chunks, dk, dv), F32)],
        compiler_params=_params(("parallel", "arbitrary")),
    )(lg, u, u, u, c2, s2, states, do)


def _mesh_pos():
    return lax.axis_index("x"), lax.axis_index("y"), lax.axis_index("c")


def _all_gather_small(buf, *, name):
    r = buf.shape[0]

    def body(x_ref, o_ref, send_sems, recv_sems, local_sem):
        x, y, c = _mesh_pos()
        me = 4 * x + 2 * y + c
        mine = pltpu.make_async_copy(x_ref, o_ref.at[me], local_sem)
        mine.start()
        copies = []
        for k in range(1, N_DEV):
            px, py, pc = x ^ ((k >> 2) & 1), y ^ ((k >> 1) & 1), c ^ (k & 1)
            cp = pltpu.make_async_remote_copy(
                src_ref=x_ref, dst_ref=o_ref.at[me], send_sem=send_sems.at[k - 1], recv_sem=recv_sems.at[k - 1],
                device_id=(px, py, pc), device_id_type=MESH)
            cp.start()
            copies.append((cp, 4 * px + 2 * py + pc))
        for k, (cp, peer) in enumerate(copies):
            pltpu.make_async_remote_copy(
                src_ref=x_ref, dst_ref=o_ref.at[peer], send_sem=send_sems.at[k], recv_sem=recv_sems.at[k],
                device_id=(x, y, c), device_id_type=MESH).wait_recv()
        for cp, _ in copies:
            cp.wait_send()
        mine.wait()

    return pl.pallas_call(
        body, name=name,
        in_specs=[pl.BlockSpec(memory_space=pltpu.VMEM)],
        out_specs=pl.BlockSpec(memory_space=pltpu.VMEM),
        out_shape=jax.ShapeDtypeStruct((N_DEV, r, LANES), F32),
        scratch_shapes=[pltpu.SemaphoreType.DMA((N_DEV - 1,)), pltpu.SemaphoreType.DMA((N_DEV - 1,)),
                        pltpu.SemaphoreType.DMA],
        compiler_params=pltpu.CompilerParams(vmem_limit_bytes=VMEM_LIMIT),
    )(buf)


def _cut(ref, shard_axis, *, chip=None, half=None, lead=None):
    shape = ref.shape[1:] if lead is not None else ref.shape
    idx = [slice(None), slice(None)]
    if chip is not None:
        w = shape[shard_axis] // N_CHIPS
        idx[shard_axis] = pl.ds(pl.multiple_of(chip * w, w), w)
    if half is not None:
        hw = shape[1 - shard_axis] // 2
        idx[1 - shard_axis] = pl.ds(pl.multiple_of(half * hw, hw), hw)
    if lead is not None:
        idx = [lead] + idx
    return ref.at[tuple(idx)]


def _wait_recv(ref, send_sem, recv_sem):
    pltpu.make_async_remote_copy(src_ref=ref, dst_ref=ref, send_sem=send_sem, recv_sem=recv_sem,
                                 device_id=_mesh_pos(), device_id_type=MESH).wait_recv()


def _gather_plan(axes):
    def plan(srcs, lands, send_sems, recv_sems):
        x, y, c = _mesh_pos()
        chip = 2 * x + y
        copies = []
        for i, ax in enumerate(axes):
            for k in range(1, N_CHIPS):
                px, py = x ^ (k >> 1), y ^ (k & 1)
                mine = _cut(lands[i], ax, chip=chip, half=c)
                j = i * (N_CHIPS - 1) + k - 1
                sems = dict(send_sem=send_sems.at[j], recv_sem=recv_sems.at[j], device_id=(px, py, c), device_id_type=MESH)
                send = pltpu.make_async_remote_copy(src_ref=mine, dst_ref=mine, **sems)
                recv = pltpu.make_async_remote_copy(src_ref=mine, dst_ref=_cut(lands[i], ax, chip=2 * px + py, half=c), **sems)
                copies.append((send, recv))
        return copies
    return plan


def _pair_plan(axes):
    def plan(srcs, lands, send_sems, recv_sems):
        x, y, c = _mesh_pos()
        copies = []
        for i, ax in enumerate(axes):
            cp = pltpu.make_async_remote_copy(
                src_ref=_cut(srcs[i], ax, half=1 - c), dst_ref=lands[i], send_sem=send_sems.at[i], recv_sem=recv_sems.at[i],
                device_id=(x, y, 1 - c), device_id_type=MESH)
            copies.append((cp, cp))
        return copies
    return plan


def _scatter_plan(axes):
    def plan(srcs, lands, send_sems, recv_sems):
        x, y, c = _mesh_pos()
        copies = []
        for i, ax in enumerate(axes):
            for k in range(1, N_CHIPS):
                px, py = x ^ (k >> 1), y ^ (k & 1)
                j = i * (N_CHIPS - 1) + k - 1
                cp = pltpu.make_async_remote_copy(
                    src_ref=_cut(srcs[i], ax, chip=2 * px + py), dst_ref=lands[i].at[k - 1],
                    send_sem=send_sems.at[j], recv_sem=recv_sems.at[j], device_id=(px, py, c), device_id_type=MESH)
                copies.append((cp, cp))
        return copies
    return plan


HBM = pl.BlockSpec(memory_space=pltpu.HBM)
SEM = pl.BlockSpec(memory_space=pltpu.SEMAPHORE)
EFFECT = pltpu.SideEffectType.DATAFLOW_SIDE_EFFECTING


def _in_hbm(arrays):
    return [pltpu.with_memory_space_constraint(a, pltpu.HBM) for a in arrays]


def _split_start(srcs, lands, plan, n_copies, *, name):
    bufs = list(srcs) + list(lands)
    ns, nb = len(srcs), len(bufs)

    def body(*refs):
        send_sems, recv_sems, token = refs[nb], refs[nb + 1], refs[-1]
        for send, _ in plan(refs[:ns], refs[ns:nb], send_sems, recv_sems):
            send.start()
        token[...] = jnp.zeros_like(token)

    sems = pltpu.SemaphoreType.DMA((n_copies,))
    res = pl.pallas_call(
        body, name=name, in_specs=[HBM] * nb,
        out_specs=[SEM, SEM] + [HBM] * nb + [pl.BlockSpec(memory_space=pltpu.VMEM)],
        out_shape=[sems, sems] + [pltpu.HBM(a.shape, a.dtype) for a in bufs] + [jax.ShapeDtypeStruct((8, LANES), F32)],
        input_output_aliases={j: 2 + j for j in range(nb)},
        compiler_params=pltpu.CompilerParams(has_side_effects=EFFECT),
    )(*_in_hbm(bufs))
    return res[0], res[1], res[2:2 + ns], res[2 + ns:2 + nb], res[-1]


def _split_wait(started, after, plan, *, name, with_srcs=False):
    send_sems, recv_sems, srcs, lands, _ = started
    bufs = list(srcs) + list(lands)
    ns, nb = len(srcs), len(bufs)

    def body(*refs):
        for send, recv in plan(refs[:ns], refs[ns:nb], refs[nb], refs[nb + 1]):
            send.wait_send()
            recv.wait_recv()

    res = pl.pallas_call(
        body, name=name, in_specs=[HBM] * nb + [SEM, SEM, ANY], out_specs=[HBM] * nb,
        out_shape=[pltpu.HBM(a.shape, a.dtype) for a in bufs],
        input_output_aliases={j: j for j in range(nb)},
        compiler_params=pltpu.CompilerParams(has_side_effects=EFFECT),
    )(*bufs, send_sems, recv_sems, after)
    return (res[:ns], res[ns:]) if with_srcs else res[ns:]


def _cast_into_full(w3, layer, ax, chip, *, after=None, name):
    _, r, wd = w3.shape
    tr = _rows_per_tile(r, wd, 4 << 20)
    nt = r // tr
    full_shape = (r, wd * N_CHIPS) if ax == 1 else (r * N_CHIPS, wd)
    out_map = (lambda i, ch: (i, ch[0])) if ax == 1 else (lambda i, ch: (ch[0] * nt + i, 0))
    zero = jnp.zeros((1, wd), F32) + (0.0 if after is None else after)

    def body(chip_ref, w_ref, z_ref, o_ref):
        o_ref[...] = (w_ref[...] + z_ref[...]).astype(o_ref.dtype)

    return pl.pallas_call(
        body, name=name,
        grid_spec=pltpu.PrefetchScalarGridSpec(
            num_scalar_prefetch=1, grid=(nt,),
            in_specs=[pl.BlockSpec((None, tr, wd), lambda i, ch: (layer, i, 0)), pl.BlockSpec((1, wd), lambda i, ch: (0, 0))],
            out_specs=pl.BlockSpec((tr, wd), out_map)),
        out_shape=jax.ShapeDtypeStruct(full_shape, BF16),
        compiler_params=_params(("parallel",)),
    )(jnp.reshape(chip, (1,)).astype(jnp.int32), w3, zero)


def _forward_halves(fulls, axes, *, name):
    n = len(fulls)

    def body(*refs):
        bufs = refs[:n]
        send_sems, recv_sems = refs[2 * n:]
        x, y, c = _mesh_pos()
        sends = []
        for i in range(n):
            for k in range(1, N_CHIPS):
                landed = _cut(bufs[i], axes[i], chip=2 * (x ^ (k >> 1)) + (y ^ (k & 1)), half=c)
                cp = pltpu.make_async_remote_copy(
                    src_ref=landed, dst_ref=landed, send_sem=send_sems.at[i, k - 1], recv_sem=recv_sems.at[i, k - 1],
                    device_id=(x, y, 1 - c), device_id_type=MESH)
                cp.start()
                sends.append(cp)
        for i in range(n):
            for k in range(1, N_CHIPS):
                other = _cut(bufs[i], axes[i], chip=2 * (x ^ (k >> 1)) + (y ^ (k & 1)), half=1 - c)
                _wait_recv(other, send_sems.at[i, k - 1], recv_sems.at[i, k - 1])
        for cp in sends:
            cp.wait_send()

    pairs = pltpu.SemaphoreType.DMA((n, N_CHIPS - 1))
    return pl.pallas_call(
        body, name=name, in_specs=[ANY] * n, out_specs=[ANY] * n,
        out_shape=[jax.ShapeDtypeStruct(a.shape, a.dtype) for a in fulls],
        input_output_aliases={j: j for j in range(n)},
        scratch_shapes=[pairs, pairs],
    )(*fulls)


def _share_halves_in_place(bufs, axes, *, name):
    n = len(bufs)

    def body(*refs):
        ins = refs[:n]
        send_sems, recv_sems = refs[2 * n:]
        x, y, c = _mesh_pos()
        sends = []
        for i in range(n):
            mine = _cut(ins[i], axes[i], half=c)
            cp = pltpu.make_async_remote_copy(
                src_ref=mine, dst_ref=mine, send_sem=send_sems.at[i], recv_sem=recv_sems.at[i],
                device_id=(x, y, 1 - c), device_id_type=MESH)
            cp.start()
            sends.append(cp)
        for i in range(n):
            _wait_recv(_cut(ins[i], axes[i], half=1 - c), send_sems.at[i], recv_sems.at[i])
        for cp in sends:
            cp.wait_send()

    sems = pltpu.SemaphoreType.DMA((n,))
    return pl.pallas_call(
        body, name=name, in_specs=[ANY] * n, out_specs=[ANY] * n,
        out_shape=[jax.ShapeDtypeStruct(b.shape, b.dtype) for b in bufs],
        input_output_aliases={j: j for j in range(n)}, scratch_shapes=[sems, sems],
    )(*bufs)


def _adamw_math(w, g, m, v):
    m = ADAM_B1 * m + (1.0 - ADAM_B1) * g
    v = ADAM_B2 * v + (1.0 - ADAM_B2) * (g * g)
    m_hat = m / (1.0 - ADAM_B1 ** ADAM_STEP)
    v_hat = v / (1.0 - ADAM_B2 ** ADAM_STEP)
    delta = -ADAM_LR * (m_hat / (jnp.sqrt(v_hat) + ADAM_EPS) + ADAM_WD * w)
    return delta, m, v


def _adamw_layer(w3, m3, v3, p, q, layer, prev, *, name):
    nl, rows, width = w3.shape
    tr = _rows_per_tile(rows, width)

    def fn(*t):
        if q is None:
            w, m, v, g = t
        else:
            w, m, v, g, g2 = t
            g = g + g2
        delta, m, v = _adamw_math(w, g, m, v)
        return g, delta, m, v

    ins = [('t', w3, 0, width, layer), ('t', m3, 0, width, layer), ('t', v3, 0, width, layer), ('t', p, 0, width)]
    if q is not None:
        ins.append(('t', q, 0, width))
    outs = [('t', width, F32, layer, nl)] * 4
    aliases = None if prev is None else [(prev[i], i) for i in range(4)]
    return _ew(fn, ins, outs, rows=rows, tr=tr, name=name, aliases=aliases)


def _pack_rows(vec):
    n = vec.shape[0]
    r = -(-n // (8 * LANES)) * 8
    return jnp.pad(vec, (0, r * LANES - n)).reshape(r, LANES)


def kernel(x, c, ctx, c_ctx, ada_w, ada_b, norm_g, w_in, na_rpb, ret_decay_logit, w_proj_na, w_proj_ret, w_out, final_g, loss_target, m_c_ctx, m_ada_w, m_ada_b, m_norm_g, m_w_in, m_na_rpb, m_ret_decay_logit, m_w_proj_na, m_w_proj_ret, m_w_out, m_final_g, v_c_ctx, v_ada_w, v_ada_b, v_norm_g, v_w_in, v_na_rpb, v_ret_decay_logit, v_w_proj_na, v_w_proj_ret, v_w_out, v_final_g):
    depth = w_in.shape[0]
    s_len, d_model = x.shape[1], x.shape[2]
    l_len = ctx.shape[1]
    t_len = s_len + l_len
    na_heads = na_rpb.shape[1]
    ret_heads = ret_decay_logit.shape[2]
    w_na = na_heads * NA_HEAD_DIM
    w_qk = ret_heads * RET_KEY_DIM
    w_v = ret_heads * RET_VAL_DIM
    in_cols = w_in.shape[2] * N_CHIPS
    assert in_cols == 4 * w_na + 2 * w_qk + 2 * w_v + 2 * d_model
    assert x.shape[0] == 1 and s_len % (NA_WIN_ROWS * GRID_W) == 0 and l_len % RET_CHUNK == 0
    off = np.cumsum([0, w_na, w_na, w_na, w_na, w_qk, w_qk, w_v, w_v, d_model, d_model])
    o_naz, o_retq, o_retz, o_gna, o_gret = int(off[3]), int(off[4]), int(off[7]), int(off[8]), int(off[9])
    rows = s_len // GRID_W
    tr = _tile(l_len, 256, 8)
    n0 = s_len // tr
    mod_cols = 3 * d_model
    mod_shard = ada_w.shape[2]

    xi, yi, ci = _mesh_pos()
    me = 4 * xi + 2 * yi + ci
    chip = 2 * xi + yi

    big_axes = [1, 1, 0, 0]
    n_big = len(big_axes) * (N_CHIPS - 1)
    gather_plan, scatter_plan = _gather_plan(big_axes), _scatter_plan(big_axes)

    c_silu = c[0] * _sigmoid(c[0])
    cc_silu = c_ctx * _sigmoid(c_ctx)
    c_all = _all_gather_small(_pack_rows(c_silu), name="gather_c")[:, :d_model // LANES].reshape(N_DEV, d_model)
    a_rows = jnp.concatenate([c_all, cc_silu[None], jnp.zeros((16 - N_DEV - 1, d_model), F32)], axis=0)
    mod_part = jnp.stack([_mm(a_rows, ada_w, b_lead=l, out_dtype=F32, name="ada_fwd_%d" % l) for l in range(depth)])
    mod_all = _all_gather_small(_pack_rows(mod_part.reshape(-1)), name="gather_mod")
    n_mod = depth * 16 * mod_shard
    mod_all = mod_all.reshape(N_DEV, -1)[:, :n_mod].reshape(N_CHIPS, 2, depth, 16, mod_shard)[:, 0]
    mod_all = jnp.transpose(mod_all, (1, 2, 0, 3)).reshape(depth, 16, mod_cols) + ada_b[:, None, :]

    big_named = list(zip((w_in, w_proj_na, w_proj_ret, w_out), big_axes, ("w_in", "w_proj_na", "w_proj_ret", "w_out")))
    w_in0 = _cast_into_full(w_in, 0, big_axes[0], chip, name="cast_w_in_0")
    mod_all, w_in0 = lax.optimization_barrier((mod_all, w_in0))
    plan_in, plan_rest = _gather_plan(big_axes[:1]), _gather_plan(big_axes[1:])
    first_gather = _split_start([], [w_in0], plan_in, N_CHIPS - 1, name="gather_start_0_in")
    start_token = first_gather[4][0, 0]
    fulls = [[None if (l == 0 and tag == "w_in") else _cast_into_full(w, l, ax, chip, after=start_token, name="cast_%s_%d" % (tag, l))
              for w, ax, tag in big_named] for l in range(depth)]
    mod_lat = lax.dynamic_index_in_dim(mod_all, me, axis=1, keepdims=False)
    mod_ctx = mod_all[:, N_DEV]

    c2, s2 = _rope_tables(s_len, l_len)
    log_gamma = jax.nn.log_sigmoid(ret_decay_logit)
    x_all = jnp.concatenate([x[0], ctx[0]], axis=0)

    def grp(lat_vec, ctx_vec):
        return jnp.stack([lat_vec, ctx_vec])[:, None, :]

    saved, full_w = [], []
    for l in range(depth):
        shift, scale, gate = [grp(mod_lat[l, i * d_model:(i + 1) * d_model], mod_ctx[l, i * d_model:(i + 1) * d_model])
                              for i in range(3)]
        gs = norm_g[l][None, None, :] * (1.0 + scale) + start_token

        def modnorm(xt, gs_t, sh_t):
            r = lax.rsqrt(jnp.mean(xt * xt, axis=-1, keepdims=True) + NORM_EPS)
            return xt * r * gs_t + sh_t

        h, = _ew(modnorm, [('t', x_all, 0, d_model), ('g', gs), ('g', shift)], [('t', d_model, BF16)],
                 rows=t_len, tr=tr, n0=n0, name="modnorm_%d" % l)
        bias = _na_bias_table(na_rpb[l], rows, name="na_bias_%d" % l)
        h, bias = lax.optimization_barrier((h, bias))
        if l == 0:
            landed_in = _split_wait(first_gather, h, plan_in, name="gather_wait_0_in")
            landed_in, rest0, later = lax.optimization_barrier((landed_in, fulls[0][1:], fulls[1:]))
            rest_gather = _split_start([], rest0, plan_rest, n_big - (N_CHIPS - 1), name="gather_start_0_rest")
            later_gathers = [_split_start([], later[j], gather_plan, n_big, name="gather_start_%d" % (j + 1)) for j in range(depth - 1)]
            win_f, = _forward_halves(landed_in, big_axes[:1], name="gather_forward_0_in")
            win_f, tokens = lax.optimization_barrier((win_f, [rest_gather[4]] + [g[4] for g in later_gathers]))
            gate = gate + sum(t[0, 0] for t in tokens)
        else:
            landed = _split_wait(later_gathers[l - 1], h, gather_plan, name="gather_wait_%d" % l)
            win_f, wpn_f, wpr_f, wout_f = _forward_halves(landed, big_axes, name="gather_forward_%d" % l)
        u = _mm(h, win_f, tm=1152, tn=1024, name="in_proj_%d" % l)
        o_na = _na_fwd(u, bias, s_len=s_len, heads=na_heads, name="na_fwd_%d" % l)
        o_ret, states = _ret_fwd(u, c2, s2, log_gamma[l], s_len=s_len, heads=ret_heads, q_off=o_retq, name="ret_fwd_%d" % l)

        def act(o1, z1, o2, z2):
            a1 = o1.astype(F32) * _silu_parts(z1.astype(F32))[0]
            sz = _silu_parts(z2.astype(F32))[0]
            outs = []
            for hh in range(ret_heads):
                sl = slice(hh * RET_VAL_DIM, (hh + 1) * RET_VAL_DIM)
                oh = o2[:, sl]
                r = lax.rsqrt(jnp.mean(oh * oh, axis=-1, keepdims=True) + NORM_EPS)
                outs.append(oh * r * sz[:, sl])
            return a1, jnp.concatenate(outs, axis=-1)

        a_na, a_ret = _ew(act, [('t', o_na, 0, w_na), ('t', u, o_naz // w_na, w_na), ('t', o_ret, 0, w_v), ('t', u, o_retz // w_v, w_v)],
                          [('t', w_na, BF16), ('t', w_v, BF16)], rows=t_len, tr=tr, name="act_%d" % l)
        if l == 0:
            landed_rest = _split_wait(rest_gather, a_na, plan_rest, name="gather_wait_0_rest")
            wpn_f, wpr_f, wout_f = _forward_halves(landed_rest, big_axes[1:], name="gather_forward_0_rest")
        full_w.append((win_f, wpn_f, wpr_f, wout_f))
        y_na = _mm(a_na, wpn_f, name="proj_na_%d" % l)
        y_ret = _mm(a_ret, wpr_f, name="proj_ret_%d" % l)

        def merge(y1, y2, g1, g2):
            return _sigmoid(g1.astype(F32)) * y1.astype(F32) + _sigmoid(g2.astype(F32)) * y2.astype(F32)

        merged, = _ew(merge, [('t', y_na, 0, d_model), ('t', y_ret, 0, d_model), ('t', u, o_gna // d_model, d_model), ('t', u, o_gret // d_model, d_model)],
                      [('t', d_model, BF16)], rows=t_len, tr=tr, name="merge_%d" % l)
        out = _mm(merged, wout_f, out_dtype=F32, name="out_proj_%d" % l)
        x_new, = _ew(lambda xt, ot, gt: xt + gt * ot, [('t', x_all, 0, d_model), ('t', out, 0, d_model), ('g', gate)],
                     [('t', d_model, F32)], rows=t_len, tr=tr, n0=n0, name="resid_%d" % l)
        saved.append(dict(x=x_all, h=h, u=u, bias=bias, o_na=o_na, o_ret=o_ret, states=states, a_na=a_na, a_ret=a_ret,
                          y_na=y_na, y_ret=y_ret, merged=merged, out=out, gate=gate, gs=gs, scale=scale))
        x_all = x_new

    def final(xt, tt, gt):
        r = lax.rsqrt(jnp.mean(xt * xt, axis=-1, keepdims=True) + NORM_EPS)
        xh = xt * r
        e = xh * gt - tt
        dy = e * (1.0 / d_model)
        dyg = dy * gt
        dx = r * (dyg - xh * jnp.mean(dyg * xh, axis=-1, keepdims=True))
        return dx, _rsum(dy * xh), _rsum(e * e)

    dx_lat, d_final_g, loss_cols = _ew(final, [('t', x_all, 0, d_model), ('t', loss_target[0], 0, d_model), ('g', final_g[None, None, :])],
                                       [('t', d_model, F32), ('r', d_model, 1), ('r', d_model, 1)], rows=s_len, tr=tr, name="final")
    loss_part = (0.5 / d_model) * jnp.sum(loss_cols)
    dx_all = jnp.concatenate([dx_lat, jnp.zeros((l_len, d_model), F32)], axis=0)

    big_w = [(w_in, m_w_in, v_w_in), (w_proj_na, m_w_proj_na, v_w_proj_na), (w_proj_ret, m_w_proj_ret, v_w_proj_ret), (w_out, m_w_out, v_w_out)]
    big_res = [None] * 4
    scatters = {}
    back_token = jnp.zeros((), F32)

    pairs = {}

    def start_pair(key, grads, axes):
        plan = _pair_plan(axes)
        lands = []
        for g, ax in zip(grads, axes):
            shp = list(g.shape)
            shp[1 - ax] //= 2
            lands.append(lax.empty(tuple(shp), BF16))
        pairs[key] = (_split_start(grads, lands, plan, len(axes), name="pair_start_%s" % key), axes, plan)
        return pairs[key][0][4]

    def start_scatter(key, after):
        started, axes, pair_plan = pairs[key]
        grads, theirs = _split_wait(started, after, pair_plan, name="pair_wait_%s" % key, with_srcs=True)
        plan = _scatter_plan(axes)
        pair = [_sum_pair(g, t, ax, ci, name="sum_pair_%s_%d" % (key, i)) for i, (g, t, ax) in enumerate(zip(grads, theirs, axes))]
        own = [lax.dynamic_slice_in_dim(s, chip * (s.shape[ax] // N_CHIPS), s.shape[ax] // N_CHIPS, axis=ax) for s, ax in zip(pair, axes)]
        lands = [lax.empty((N_CHIPS - 1,) + o.shape, BF16) for o in own]
        started = _split_start(pair, lands, plan, len(axes) * (N_CHIPS - 1), name="scatter_start_%s" % key)
        scatters[key] = (started, own, axes, plan)
        return started[4]

    def finish_scatter(key, after):
        started, own, axes, plan = scatters[key]
        recv = _split_wait(started, after, plan, name="scatter_wait_%s" % key)
        bufs = [_sum_chips_into(own[i], rbuf, axes[i], ci, name="sum_chips_%s_%d" % (key, i)) for i, rbuf in enumerate(recv)]
        return _share_halves_in_place(bufs, axes, name="share_halves_%s" % key)

    def adamw_big(l, idx, grads, big_res):
        for i, g in zip(idx, grads):
            w3, m3, v3 = big_w[i]
            big_res[i] = _adamw_layer(w3, m3, v3, g, None, l, big_res[i], name="adamw_big_%d_%d" % (i, l))
        return big_res

    small = dict(dmod_lat=[None] * depth, dmod_ctx=[None] * depth, dnorm_g=[None] * depth, drpb=[None] * depth, ddecay=[None] * depth)
    for l in reversed(range(depth)):
        sv = saved[l]
        win_f, wpn_f, wpr_f, wout_f = full_w[l]

        def resid_bwd(dxt, ot, gt):
            return gt * dxt, _rsum(dxt * ot)

        dout, dgate = _ew(resid_bwd, [('t', dx_all, 0, d_model), ('t', sv['out'], 0, d_model), ('g', sv['gate'] + back_token)],
                          [('t', d_model, BF16), ('r', d_model, 2)], rows=t_len, tr=tr, n0=n0, name="resid_bwd_%d" % l)
        dmerged = _mm(dout, wout_f, tb=True, name="out_proj_dx_%d" % l)
        g_wout = _mm(sv['merged'], dout, ta=True, tm=1024, tk=t_len, name="out_proj_dw_%d" % l)

        def merge_bwd(dm, y1, y2, g1, g2):
            dm = dm.astype(F32)
            s1, s2_ = _sigmoid(g1.astype(F32)), _sigmoid(g2.astype(F32))
            return dm * s1, dm * s2_, dm * y1.astype(F32) * s1 * (1.0 - s1), dm * y2.astype(F32) * s2_ * (1.0 - s2_)

        u = sv['u']
        dy_na, dy_ret, dg_na, dg_ret = _ew(
            merge_bwd, [('t', dmerged, 0, d_model), ('t', sv['y_na'], 0, d_model), ('t', sv['y_ret'], 0, d_model),
                        ('t', u, o_gna // d_model, d_model), ('t', u, o_gret // d_model, d_model)],
            [('t', d_model, BF16)] * 4, rows=t_len, tr=tr, name="merge_bwd_%d" % l)
        da_na = _mm(dy_na, wpn_f, tb=True, name="proj_na_dx_%d" % l)
        g_wpn = _mm(sv['a_na'], dy_na, ta=True, tm=1024, tk=t_len, name="proj_na_dw_%d" % l)
        da_ret = _mm(dy_ret, wpr_f, tb=True, name="proj_ret_dx_%d" % l)
        g_wpr = _mm(sv['a_ret'], dy_ret, ta=True, tm=1024, tk=t_len, name="proj_ret_dw_%d" % l)
        lg_l = log_gamma[l]
        if l == 0:
            pair_token = start_pair("0_rest", [g_wpn, g_wpr, g_wout], big_axes[1:])

        def act_bwd(da1, o1, z1, da2, o2, z2):
            da1, da2 = da1.astype(F32), da2.astype(F32)
            si1, ds1 = _silu_parts(z1.astype(F32))
            si2, ds2 = _silu_parts(z2.astype(F32))
            do1 = da1 * si1
            dz1 = da1 * o1.astype(F32) * ds1
            dn = da2 * si2
            do2, dz2 = [], []
            for hh in range(ret_heads):
                sl = slice(hh * RET_VAL_DIM, (hh + 1) * RET_VAL_DIM)
                oh = o2[:, sl]
                r = lax.rsqrt(jnp.mean(oh * oh, axis=-1, keepdims=True) + NORM_EPS)
                nh = oh * r
                dz2.append(da2[:, sl] * nh * ds2[:, sl])
                do2.append(r * (dn[:, sl] - nh * jnp.mean(dn[:, sl] * nh, axis=-1, keepdims=True)))
            return do1, dz1, jnp.concatenate(do2, axis=-1), jnp.concatenate(dz2, axis=-1)

        do_na, dz_na, do_ret, dz_ret = _ew(
            act_bwd, [('t', da_na, 0, w_na), ('t', sv['o_na'], 0, w_na), ('t', u, o_naz // w_na, w_na),
                      ('t', da_ret, 0, w_v), ('t', sv['o_ret'], 0, w_v), ('t', u, o_retz // w_v, w_v)],
            [('t', w_na, BF16), ('t', w_na, BF16), ('t', w_v, BF16), ('t', w_v, BF16)], rows=t_len, tr=tr, name="act_bwd_%d" % l)
        dq_na, dk_na, dv_na, dbias = _na_bwd(u, sv['bias'], sv['o_na'], do_na, s_len=s_len, heads=na_heads, name="na_bwd_%d" % l)
        small['drpb'][l] = _rpb_grad(dbias, name="rpb_grad_%d" % l)
        if l == 0:
            lg_l = lg_l + start_scatter("0_rest", dq_na)[0, 0] + pair_token[0, 0]
        dq_r, dk_r, dv_r, dlg = _ret_bwd(u, c2, s2, lg_l, sv['states'], do_ret, s_len=s_len, heads=ret_heads,
                                         q_off=o_retq, name="ret_bwd_%d" % l)
        small['ddecay'][l] = jnp.transpose(dlg[:, :, 0, 0]) * _sigmoid(-ret_decay_logit[l])
        du_parts = [dq_na, dk_na, dv_na, dz_na, dq_r, dk_r, dv_r, dz_ret, dg_na, dg_ret]
        du, = _ew(lambda *t: jnp.concatenate(t, axis=-1), [('t', p, 0, p.shape[1]) for p in du_parts], [('t', in_cols, BF16)],
                  rows=t_len, tr=tr, name="du_concat_%d" % l)
        if l > 0:
            g_win = _mm(sv['h'], du, ta=True, tm=1024, tn=1024, tk=t_len, name="in_proj_dw_%d" % l)
            du, pair_token = lax.optimization_barrier((du, start_pair("%d_all" % l, [g_win, g_wpn, g_wpr, g_wout], big_axes)))
        dh = _mm(du, win_f, tb=True, out_dtype=F32, tm=1152, tn=1024, name="in_proj_dx_%d" % l)

        def modnorm_bwd(xt, dht, dxt, gs_t):
            r = lax.rsqrt(jnp.mean(xt * xt, axis=-1, keepdims=True) + NORM_EPS)
            xh = xt * r
            dhg = dht * gs_t
            dx = r * (dhg - xh * jnp.mean(dhg * xh, axis=-1, keepdims=True)) + dxt
            return dx, _rsum(dht), _rsum(dht * xh)

        dx_all, dshift, dgs = _ew(modnorm_bwd, [('t', sv['x'], 0, d_model), ('t', dh, 0, d_model), ('t', dx_all, 0, d_model), ('g', sv['gs'])],
                                  [('t', d_model, F32), ('r', d_model, 2), ('r', d_model, 2)], rows=t_len, tr=tr, n0=n0, name="modnorm_bwd_%d" % l)
        dscale = dgs * norm_g[l][None, None, :]
        small['dnorm_g'][l] = jnp.sum(dgs * (1.0 + sv['scale']), axis=(0, 1))
        dmod = jnp.concatenate([dshift, dscale, dgate], axis=-1)[:, 0]
        small['dmod_lat'][l], small['dmod_ctx'][l] = dmod[0], dmod[1]

        if l > 0:
            back_token = start_scatter("%d_all" % l, dx_all)[0, 0] + pair_token[0, 0]

    grad_x = dx_all[:s_len][None]

    drpb = jnp.stack(small['drpb']).reshape(-1)
    ddecay = jnp.stack(small['ddecay']).reshape(-1)
    pieces = [jnp.stack(small['dmod_lat']).reshape(-1), jnp.stack(small['dmod_ctx']).reshape(-1),
              jnp.stack(small['dnorm_g']).reshape(-1), d_final_g.reshape(-1), drpb, ddecay, loss_part[None]]
    sizes = [int(p.shape[0]) for p in pieces]
    pads = [-(-s // LANES) * LANES for s in sizes]
    packed = jnp.concatenate([jnp.pad(p, (0, pd - s)) for p, s, pd in zip(pieces, sizes, pads)])
    gathered = _all_gather_small(_pack_rows(packed), name="gather_small_grads")
    r_small = gathered.shape[1]

    def sum8(*t):
        acc = t[0]
        for other in t[1:]:
            acc = acc + other
        return acc

    total, = _ew(sum8, [('t', gathered, 0, LANES, k) for k in range(N_DEV)], [('t', LANES, F32)], rows=r_small, tr=r_small, name="sum_devices")
    total = total.reshape(-1)
    starts = np.cumsum([0] + pads)
    g_mod_lat_sum, g_mod_ctx, g_norm_g, g_final_g, g_rpb, g_decay, loss = [total[starts[i]:starts[i] + sizes[i]] for i in range(len(pieces))]
    loss = loss[0]
    g_ada_b = (g_mod_lat_sum + g_mod_ctx).reshape(depth, mod_cols)
    g_mod_ctx = g_mod_ctx.reshape(depth, mod_cols)
    dmod_lat_all = gathered.reshape(N_DEV, -1)[:, :depth * mod_cols].reshape(N_DEV, depth, mod_cols)

    dcc_part = jnp.zeros((16, d_model), F32)
    ctx_cols = [lax.dynamic_slice_in_dim(g_mod_ctx[l], chip * mod_shard, mod_shard, axis=0) for l in range(depth)]
    for l in reversed(range(depth)):
        c_rows = jnp.concatenate([ctx_cols[l][None], jnp.zeros((15, mod_shard), F32)], axis=0)
        dcc_part = dcc_part + _mm(c_rows, ada_w, tb=True, b_lead=l, out_dtype=F32, name="ada_dc_%d" % l)
    dcc_all = _all_gather_small(_pack_rows(dcc_part[0]), name="gather_dcc")[:, :d_model // LANES].reshape(N_CHIPS, 2, d_model)[:, 0]

    du0, dcc_all = lax.optimization_barrier((du, dcc_all))
    g_win0 = _mm(saved[0]['h'], du0, ta=True, tm=1024, tn=1024, tk=t_len, name="in_proj_dw_0")
    tail_token = start_scatter("0_in", start_pair("0_in", [g_win0], big_axes[:1]))
    dcc = ((dcc_all[0] + dcc_all[1]) + dcc_all[2]) + dcc_all[3]
    sg = _sigmoid(c_ctx)
    g_c_ctx = dcc * (sg * (1.0 + c_ctx * (1.0 - sg)))
    for l in reversed(range(1, depth)):
        big_res = adamw_big(l, range(4), finish_scatter("%d_all" % l, tail_token), big_res)

    ada_res = None
    for l in reversed(range(depth)):
        lat_cols = lax.dynamic_slice_in_dim(dmod_lat_all[:, l], chip * mod_shard, mod_shard, axis=1)
        d_rows = jnp.concatenate([lat_cols, ctx_cols[l][None], jnp.zeros((16 - N_DEV - 1, mod_shard), F32)], axis=0) + tail_token[0, 0]
        g_ada = _mm(a_rows, d_rows, ta=True, out_dtype=F32, tm=512, name="ada_dw_%d" % l)
        ada_res = _adamw_layer(ada_w, m_ada_w, v_ada_w, g_ada, None, l, ada_res, name="adamw_ada_%d" % l)

    small_w = [(c_ctx, m_c_ctx, v_c_ctx, g_c_ctx), (ada_b, m_ada_b, v_ada_b, g_ada_b),
               (norm_g, m_norm_g, v_norm_g, g_norm_g), (na_rpb, m_na_rpb, v_na_rpb, g_rpb),
               (ret_decay_logit, m_ret_decay_logit, v_ret_decay_logit, g_decay), (final_g, m_final_g, v_final_g, g_final_g)]
    sw_sizes = [int(np.prod(t[0].shape)) for t in small_w]
    sw_pads = [-(-s // LANES) * LANES for s in sw_sizes]

    def pack(j):
        return _pack_rows(jnp.concatenate([jnp.pad(t[j].reshape(-1), (0, pd - s)) for t, s, pd in zip(small_w, sw_sizes, sw_pads)]))

    pw_, pm_, pv_, pg_ = pack(0), pack(1), pack(2), pack(3)
    sw_out = _ew(lambda w, m, v, g: (g,) + _adamw_math(w, g, m, v),
                 [('t', pw_, 0, LANES), ('t', pm_, 0, LANES), ('t', pv_, 0, LANES), ('t', pg_, 0, LANES)],
                 [('t', LANES, F32)] * 4, rows=pw_.shape[0], tr=pw_.shape[0], name="adamw_small")
    sw_starts = np.cumsum([0] + sw_pads)
    sw_out, ada_res, big_res = lax.optimization_barrier((sw_out, ada_res, big_res))
    big_res = adamw_big(0, range(1, 4), finish_scatter("0_rest", sw_out[0]), big_res)
    big_res = adamw_big(0, range(1), finish_scatter("0_in", sw_out[1]), big_res)

    def unpack(arr, i):
        return arr.reshape(-1)[sw_starts[i]:sw_starts[i] + sw_sizes[i]].reshape(small_w[i][0].shape)

    sm = [[unpack(sw_out[j], i) for i in range(len(small_w))] for j in range(4)]
    def ordered(j):
        return [sm[j][0], ada_res[j], sm[j][1], sm[j][2], big_res[0][j], sm[j][3], sm[j][4],
                big_res[1][j], big_res[2][j], big_res[3][j], sm[j][5]]

    return (loss, grad_x, *ordered(0), *ordered(1), *ordered(2), *ordered(3))
```

```python
import functools
import math

import numpy as np
import jax
import jax.numpy as jnp
from jax import lax
from jax.experimental import pallas as pl
from jax.experimental.pallas import tpu as pltpu

GRID_W = 64
NA_HEAD_DIM = 128
NA_WIN_ROWS = 8
NA_WIN_COLS = 16
NA_GROUP = 8
RET_GROUPS = (1, 2, 3)
RET_KEY_DIM = 128
RET_VAL_DIM = 256
RET_CHUNK = 128
ROPE_BASE = 10000.0
NORM_EPS = 1e-6
MASK_VALUE = -1e30
ADAM_LR = 0.001
ADAM_B1 = 0.9
ADAM_B2 = 0.999
ADAM_EPS = 1e-08
ADAM_WD = 0.01
ADAM_STEP = 10

N_CHIPS = 4
N_DEV = 8
LANES = 128
VMEM_LIMIT = 56 * 1024 * 1024
BF16 = jnp.bfloat16
F32 = jnp.float32
MESH = pl.DeviceIdType.MESH
ANY = pl.BlockSpec(memory_space=pl.ANY)


def _tile(dim, pref, align=LANES):
    if dim <= pref:
        return dim
    t = (pref // align) * align
    while t >= align:
        if dim % t == 0:
            return t
        t -= align
    return dim


def _rows_per_tile(rows, width, tile_bytes=1 << 20):
    return _tile(rows, max(8, tile_bytes // (4 * width)), 8)


def _params(sem):
    return pltpu.CompilerParams(dimension_semantics=sem, vmem_limit_bytes=VMEM_LIMIT)


def _sigmoid(x):
    return 1.0 / (1.0 + jnp.exp(-x))


def _dot(a, b, ca, cb):
    return lax.dot_general(a, b, (((ca,), (cb,)), ((), ())), preferred_element_type=F32)


def _mm(a, b, *, ta=False, tb=False, a_lead=None, b_lead=None, out_dtype=BF16, tm=1152, tn=1024, tk=2048, name):
    ash = a.shape[1:] if a_lead is not None else a.shape
    bsh = b.shape[1:] if b_lead is not None else b.shape
    m, k = (ash[1], ash[0]) if ta else ash
    n, k2 = bsh if tb else (bsh[1], bsh[0])
    assert k == k2, (name, ash, bsh)
    tm, tn, tk = _tile(m, tm), _tile(n, tn), _tile(k, tk)
    nk = k // tk

    def lead(spec_shape, imap, l):
        if l is None:
            return pl.BlockSpec(spec_shape, imap)
        return pl.BlockSpec((None,) + spec_shape, lambda i, j, kk: (l,) + imap(i, j, kk))

    a_spec = lead((tk, tm), lambda i, j, kk: (kk, i), a_lead) if ta else lead((tm, tk), lambda i, j, kk: (i, kk), a_lead)
    b_spec = lead((tn, tk), lambda i, j, kk: (j, kk), b_lead) if tb else lead((tk, tn), lambda i, j, kk: (kk, j), b_lead)
    ca, cb = (0 if ta else 1), (1 if tb else 0)

    def body(a_ref, b_ref, o_ref, *scratch):
        part = _dot(a_ref[...].astype(BF16), b_ref[...].astype(BF16), ca, cb)
        if nk == 1:
            o_ref[...] = part.astype(o_ref.dtype)
            return
        acc_ref, = scratch
        kk = pl.program_id(2)

        @pl.when(kk == 0)
        def _():
            acc_ref[...] = part

        @pl.when(kk > 0)
        def _():
            acc_ref[...] += part

        @pl.when(kk == nk - 1)
        def _():
            o_ref[...] = acc_ref[...].astype(o_ref.dtype)

    return pl.pallas_call(
        body, name=name, grid=(m // tm, n // tn, nk),
        in_specs=[a_spec, b_spec],
        out_specs=pl.BlockSpec((tm, tn), lambda i, j, kk: (i, j)),
        out_shape=jax.ShapeDtypeStruct((m, n), out_dtype),
        scratch_shapes=[] if nk == 1 else [pltpu.VMEM((tm, tn), F32)],
        compiler_params=_params(("parallel", "parallel", "arbitrary")),
    )(a, b)


def _ew(fn, ins, outs, *, rows, tr, name, n0=None, aliases=None):
    assert rows % tr == 0, (name, rows, tr)
    nt = rows // tr

    def grp(i):
        return 0 if n0 is None else jnp.where(i < n0, 0, 1)

    in_specs, args = [], []
    for spec in ins:
        if spec[0] == 't':
            arr, cb, w = spec[1], spec[2], spec[3]
            l = spec[4] if len(spec) > 4 else None
            if l is None:
                in_specs.append(pl.BlockSpec((tr, w), functools.partial(lambda i, cb: (i, cb), cb=cb)))
            else:
                in_specs.append(pl.BlockSpec((None, tr, w), functools.partial(lambda i, cb, l: (l, i, cb), cb=cb, l=l)))
            args.append(arr)
        else:
            arr = spec[1]
            g = arr.shape[0]
            if g == 1:
                in_specs.append(pl.BlockSpec((None, 1, arr.shape[2]), lambda i: (0, 0, 0)))
            else:
                in_specs.append(pl.BlockSpec((None, 1, arr.shape[2]), lambda i: (grp(i), 0, 0)))
            args.append(arr)
    out_specs, out_shapes, is_red = [], [], []
    for spec in outs:
        if spec[0] == 't':
            w, dt = spec[1], spec[2]
            if len(spec) > 3:
                l, nl = spec[3], spec[4]
                out_specs.append(pl.BlockSpec((None, tr, w), functools.partial(lambda i, l: (l, i, 0), l=l)))
                out_shapes.append(jax.ShapeDtypeStruct((nl, rows, w), dt))
            else:
                out_specs.append(pl.BlockSpec((tr, w), lambda i: (i, 0)))
                out_shapes.append(jax.ShapeDtypeStruct((rows, w), dt))
            is_red.append(False)
        else:
            w, g = spec[1], spec[2]
            if g == 1:
                out_specs.append(pl.BlockSpec((None, 1, w), lambda i: (0, 0, 0)))
            else:
                out_specs.append(pl.BlockSpec((None, 1, w), lambda i: (grp(i), 0, 0)))
            out_shapes.append(jax.ShapeDtypeStruct((g, 1, w), F32))
            is_red.append(True)
    n_in = len(ins)
    n_alias = 0 if aliases is None else len(aliases)

    def body(*refs):
        in_refs = refs[:n_in]
        out_refs = refs[n_in + n_alias:]
        res = fn(*[r[...] for r in in_refs])
        if not isinstance(res, (tuple, list)):
            res = (res,)
        i = pl.program_id(0)
        first = (i == 0) if n0 is None else ((i == 0) | (i == n0))
        for o_ref, val, red in zip(out_refs, res, is_red):
            if not red:
                o_ref[...] = val.astype(o_ref.dtype)
            else:
                @pl.when(first)
                def _(o_ref=o_ref, val=val):
                    o_ref[...] = val

                @pl.when(jnp.logical_not(first))
                def _(o_ref=o_ref, val=val):
                    o_ref[...] += val

    io_alias = {}
    if aliases is not None:
        for a_idx, (arr, o_idx) in enumerate(aliases):
            in_specs.append(ANY)
            args.append(arr)
            io_alias[n_in + a_idx] = o_idx
    has_red = any(is_red)
    return pl.pallas_call(
        body, name=name, grid=(nt,), in_specs=in_specs, out_specs=out_specs, out_shape=out_shapes,
        input_output_aliases=io_alias,
        compiler_params=_params(("arbitrary",) if has_red else ("parallel",)),
    )(*args)


def _half_spec(tr, width, ax, n_tiles):
    if ax == 1:
        return pl.BlockSpec((tr, width), lambda i, sel: (sel[0] * n_tiles + i, 0))
    return pl.BlockSpec((tr, width), lambda i, sel: (i, sel[0]))


def _sum_pair(g, theirs, ax, ci, *, name):
    pr, pw = theirs.shape
    tr = _rows_per_tile(pr, pw)
    nt = pr // tr

    def body(sel, a_ref, b_ref, o_ref):
        o_ref[...] = (a_ref[...].astype(F32) + b_ref[...].astype(F32)).astype(o_ref.dtype)

    return pl.pallas_call(
        body, name=name,
        grid_spec=pltpu.PrefetchScalarGridSpec(
            num_scalar_prefetch=1, grid=(nt,),
            in_specs=[_half_spec(tr, pw, ax, nt), pl.BlockSpec((tr, pw), lambda i, sel: (i, 0))],
            out_specs=pl.BlockSpec((tr, pw), lambda i, sel: (i, 0))),
        out_shape=jax.ShapeDtypeStruct((pr, pw), BF16),
        compiler_params=_params(("parallel",)),
    )(jnp.reshape(ci, (1,)).astype(jnp.int32), g, theirs)


def _sum_chips_into(own, recv, ax, ci, *, name):
    pr, pw = own.shape
    tr = _rows_per_tile(pr, pw)
    nt = pr // tr
    full_shape = (2 * pr, pw) if ax == 1 else (pr, 2 * pw)

    def body(sel, a_ref, r_ref, o_ref):
        acc = a_ref[...].astype(F32)
        for k in range(N_CHIPS - 1):
            acc = acc + r_ref[k].astype(F32)
        o_ref[...] = acc

    return pl.pallas_call(
        body, name=name,
        grid_spec=pltpu.PrefetchScalarGridSpec(
            num_scalar_prefetch=1, grid=(nt,),
            in_specs=[pl.BlockSpec((tr, pw), lambda i, sel: (i, 0)), pl.BlockSpec((N_CHIPS - 1, tr, pw), lambda i, sel: (0, i, 0))],
            out_specs=_half_spec(tr, pw, ax, nt)),
        out_shape=jax.ShapeDtypeStruct(full_shape, F32),
        compiler_params=_params(("parallel",)),
    )(jnp.reshape(ci, (1,)).astype(jnp.int32), own, recv)


def _rsum(v):
    return jnp.sum(v, axis=0, keepdims=True)


def _silu_parts(z):
    sg = _sigmoid(z)
    return z * sg, sg * (1.0 + z * (1.0 - sg))


def _na_bias_table(rpb, rows, *, name):
    kh, kw = NA_WIN_ROWS, NA_WIN_COLS
    assert rows >= kh
    heads = rpb.shape[0]
    e1, e2 = _na_onehots()
    rpb16 = jnp.pad(rpb, ((0, 0), (0, 16 - rpb.shape[1]), (0, LANES - rpb.shape[2])))

    def body(r_ref, e1_ref, e2_ref, o_ref):
        e1b = e1_ref[...].astype(BF16)
        y = sum(_dot(e1b, part, 0, 0) for part in _split3(r_ref[...]))
        e2b = e2_ref[...].astype(BF16)
        o_ref[...] = sum(_dot(part, e2b, 1, 1) for part in _split3(y))

    z = pl.pallas_call(
        body, name=name, grid=(heads,),
        in_specs=[pl.BlockSpec((None, 16, LANES), lambda h: (h, 0, 0)),
                  pl.BlockSpec(e1.shape, lambda h: (0, 0)), pl.BlockSpec(e2.shape, lambda h: (0, 0))],
        out_specs=pl.BlockSpec((None, kh * kh, GRID_W * GRID_W), lambda h: (h, 0, 0)),
        out_shape=jax.ShapeDtypeStruct((heads, kh * kh, GRID_W * GRID_W), F32),
        compiler_params=_params(("parallel",)),
    )(rpb16, e1, e2)
    cidx = np.arange(GRID_W)
    c0 = np.clip(cidx - kw // 2, 0, GRID_W - kw)
    col_in = (cidx[None, :] >= c0[:, None]) & (cidx[None, :] < c0[:, None] + kw)
    bias = z.reshape(heads, kh, kh, GRID_W, GRID_W).transpose(0, 1, 3, 2, 4)
    bias = jnp.where(col_in[None, None, :, None, :], bias, MASK_VALUE)
    return bias.reshape(heads, kh, GRID_W, kh * GRID_W)


def _na_onehots():
    kh, kw = NA_WIN_ROWS, NA_WIN_COLS
    cidx = np.arange(GRID_W)
    dc = cidx[None, :] - cidx[:, None] + (kw - 1)
    e2 = np.zeros((GRID_W * GRID_W, LANES), np.float32)
    ok = (dc >= 0) & (dc <= 2 * kw - 2)
    cq, ck = np.nonzero(ok)
    e2[cq * GRID_W + ck, dc[cq, ck]] = 1.0
    dr = np.arange(kh)[None, :] - np.arange(kh)[:, None] + (kh - 1)
    e1 = np.zeros((16, kh * kh), np.float32)
    dl, kr = np.nonzero(np.ones_like(dr))
    e1[dr[dl, kr], dl * kh + kr] = 1.0
    return jnp.asarray(e1), jnp.asarray(e2)


def _na_fwd(u, bias, *, s_len, heads, name):
    t_len = u.shape[0]
    rows = s_len // GRID_W
    nloc = NA_WIN_ROWS * GRID_W
    scale = NA_HEAD_DIM ** -0.5
    hd = NA_HEAD_DIM

    def body(q_ref, k_ref, v_ref, b_ref, o_ref):
        kc = k_ref[s_len:t_len, :]
        vc = v_ref[s_len:t_len, :]

        def group(g, carry):
            rs = [g * NA_GROUP + i for i in range(NA_GROUP)]
            r0s = [jnp.clip(r - NA_WIN_ROWS // 2, 0, rows - NA_WIN_ROWS) for r in rs]
            gs_ = pl.multiple_of(g * (NA_GROUP * GRID_W), NA_GROUP * GRID_W)
            kss = [pl.multiple_of(r0 * GRID_W, GRID_W) for r0 in r0s]
            q_all = q_ref[pl.ds(gs_, NA_GROUP * GRID_W), :]
            s_ctx = _dot(q_all, kc, 1, 1) * scale
            s_loc = [_dot(q_all[i * GRID_W:(i + 1) * GRID_W], k_ref[pl.ds(kss[i], nloc), :], 1, 1) * scale + b_ref[rs[i] - r0s[i]]
                     for i in range(NA_GROUP)]
            p_loc, p_ctx, inv = [], [], []
            for i in range(NA_GROUP):
                sc = s_ctx[i * GRID_W:(i + 1) * GRID_W]
                m = jnp.maximum(jnp.max(s_loc[i], axis=-1, keepdims=True), jnp.max(sc, axis=-1, keepdims=True))
                pl_, pc_ = jnp.exp(s_loc[i] - m), jnp.exp(sc - m)
                inv.append(1.0 / (jnp.sum(pl_, axis=-1, keepdims=True) + jnp.sum(pc_, axis=-1, keepdims=True)))
                p_loc.append(pl_.astype(BF16))
                p_ctx.append(pc_.astype(BF16))
            o_ctx = _dot(jnp.concatenate(p_ctx, axis=0), vc, 1, 0)
            o_loc = [_dot(p_loc[i], v_ref[pl.ds(kss[i], nloc), :], 1, 0) for i in range(NA_GROUP)]
            out = jnp.concatenate([(o_loc[i] + o_ctx[i * GRID_W:(i + 1) * GRID_W]) * inv[i] for i in range(NA_GROUP)], axis=0)
            o_ref[pl.ds(gs_, NA_GROUP * GRID_W), :] = out.astype(o_ref.dtype)
            return carry

        lax.fori_loop(0, rows // NA_GROUP, group, 0)
        qc = q_ref[s_len:t_len, :]
        s = _dot(qc, kc, 1, 1) * scale
        p = jnp.exp(s - jnp.max(s, axis=-1, keepdims=True))
        o = _dot(p.astype(BF16), vc, 1, 0) / jnp.sum(p, axis=-1, keepdims=True)
        o_ref[s_len:t_len, :] = o.astype(o_ref.dtype)

    col = lambda off: pl.BlockSpec((t_len, hd), functools.partial(lambda h, off: (0, off + h), off=off))
    return pl.pallas_call(
        body, name=name, grid=(heads,),
        in_specs=[col(0), col(heads), col(2 * heads),
                  pl.BlockSpec((None, NA_WIN_ROWS, GRID_W, nloc), lambda h: (h, 0, 0, 0))],
        out_specs=pl.BlockSpec((t_len, hd), lambda h: (0, h)),
        out_shape=jax.ShapeDtypeStruct((t_len, heads * hd), BF16),
        compiler_params=_params(("parallel",)),
    )(u, u, u, bias)


def _na_bwd(u, bias, o, do, *, s_len, heads, name):
    t_len = u.shape[0]
    rows = s_len // GRID_W
    nloc = NA_WIN_ROWS * GRID_W
    scale = NA_HEAD_DIM ** -0.5
    hd = NA_HEAD_DIM

    def body(q_ref, k_ref, v_ref, b_ref, o_ref, do_ref, dq_ref, dk_ref, dv_ref, db_ref, dk_acc, dv_acc):
        kc = k_ref[s_len:t_len, :]
        vc = v_ref[s_len:t_len, :]
        dk_acc[...] = jnp.zeros_like(dk_acc)
        dv_acc[...] = jnp.zeros_like(dv_acc)
        db_ref[...] = jnp.zeros_like(db_ref)

        def group(g, carry):
            n_g, rw = NA_GROUP, GRID_W
            rs = [g * n_g + i for i in range(n_g)]
            r0s = [jnp.clip(r - NA_WIN_ROWS // 2, 0, rows - NA_WIN_ROWS) for r in rs]
            dls = [r - r0 for r, r0 in zip(rs, r0s)]
            gs_ = pl.ds(pl.multiple_of(g * (n_g * rw), n_g * rw), n_g * rw)
            kss = [pl.ds(pl.multiple_of(r0 * rw, rw), nloc) for r0 in r0s]
            row_of = lambda a, i: a[i * rw:(i + 1) * rw]
            q_all, do_all = q_ref[gs_, :], do_ref[gs_, :]
            dlt_all = jnp.sum(do_all.astype(F32) * o_ref[gs_, :].astype(F32), axis=-1, keepdims=True)
            s_ctx = _dot(q_all, kc, 1, 1) * scale
            dp_ctx = _dot(do_all, vc, 1, 1)
            s_loc = [_dot(row_of(q_all, i), k_ref[kss[i], :], 1, 1) * scale + b_ref[dls[i]] for i in range(n_g)]
            dp_loc = [_dot(row_of(do_all, i), v_ref[kss[i], :], 1, 1) for i in range(n_g)]
            p_loc_b, ds_loc_b, p_ctx_b, ds_ctx_b = [], [], [], []
            for i in range(n_g):
                sc, dlt = row_of(s_ctx, i), row_of(dlt_all, i)
                m = jnp.maximum(jnp.max(s_loc[i], axis=-1, keepdims=True), jnp.max(sc, axis=-1, keepdims=True))
                pl_, pc_ = jnp.exp(s_loc[i] - m), jnp.exp(sc - m)
                inv = 1.0 / (jnp.sum(pl_, axis=-1, keepdims=True) + jnp.sum(pc_, axis=-1, keepdims=True))
                pl_, pc_ = pl_ * inv, pc_ * inv
                ds_l = pl_ * (dp_loc[i] - dlt)
                db_ref[dls[i]] += ds_l
                p_loc_b.append(pl_.astype(BF16))
                ds_loc_b.append(ds_l.astype(BF16))
                p_ctx_b.append(pc_.astype(BF16))
                ds_ctx_b.append((pc_ * (row_of(dp_ctx, i) - dlt)).astype(BF16))
            p_ctx_all, ds_ctx_all = jnp.concatenate(p_ctx_b, axis=0), jnp.concatenate(ds_ctx_b, axis=0)
            dq_ctx = _dot(ds_ctx_all, kc, 1, 0)
            dq_loc = [_dot(ds_loc_b[i], k_ref[kss[i], :], 1, 0) for i in range(n_g)]
            dk_loc = [_dot(ds_loc_b[i], row_of(q_all, i), 0, 0) for i in range(n_g)]
            dv_loc = [_dot(p_loc_b[i], row_of(do_all, i), 0, 0) for i in range(n_g)]
            dk_ctx = _dot(ds_ctx_all, q_all, 0, 0)
            dv_ctx = _dot(p_ctx_all, do_all, 0, 0)
            dq_ref[gs_, :] = ((jnp.concatenate(dq_loc, axis=0) + dq_ctx) * scale).astype(dq_ref.dtype)
            for i in range(n_g):
                dk_acc[kss[i], :] += dk_loc[i] * scale
                dv_acc[kss[i], :] += dv_loc[i]
            dk_acc[s_len:t_len, :] += dk_ctx * scale
            dv_acc[s_len:t_len, :] += dv_ctx
            return carry

        lax.fori_loop(0, rows // NA_GROUP, group, 0)
        qc = q_ref[s_len:t_len, :]
        dout = do_ref[s_len:t_len, :]
        out = o_ref[s_len:t_len, :]
        s = _dot(qc, kc, 1, 1) * scale
        p = jnp.exp(s - jnp.max(s, axis=-1, keepdims=True))
        p = p / jnp.sum(p, axis=-1, keepdims=True)
        dlt = jnp.sum(dout.astype(F32) * out.astype(F32), axis=-1, keepdims=True)
        ds = (p * (_dot(dout, vc, 1, 1) - dlt)).astype(BF16)
        dq_ref[s_len:t_len, :] = (_dot(ds, kc, 1, 0) * scale).astype(dq_ref.dtype)
        dk_acc[s_len:t_len, :] += _dot(ds, qc, 0, 0) * scale
        dv_acc[s_len:t_len, :] += _dot(p.astype(BF16), dout, 0, 0)
        dk_ref[...] = dk_acc[...].astype(dk_ref.dtype)
        dv_ref[...] = dv_acc[...].astype(dv_ref.dtype)

    col = lambda off: pl.BlockSpec((t_len, hd), functools.partial(lambda h, off: (0, off + h), off=off))
    tbl = pl.BlockSpec((None, NA_WIN_ROWS, GRID_W, nloc), lambda h: (h, 0, 0, 0))
    tok = jax.ShapeDtypeStruct((t_len, heads * hd), BF16)
    return pl.pallas_call(
        body, name=name, grid=(heads,),
        in_specs=[col(0), col(heads), col(2 * heads), tbl, col(0), col(0)],
        out_specs=[col(0), col(0), col(0), tbl],
        out_shape=[tok, tok, tok, jax.ShapeDtypeStruct(bias.shape, F32)],
        scratch_shapes=[pltpu.VMEM((t_len, hd), F32), pltpu.VMEM((t_len, hd), F32)],
        compiler_params=_params(("parallel",)),
    )(u, u, u, bias, o, do)


def _split3(x):
    hi = x.astype(BF16)
    r1 = x - hi.astype(F32)
    mid = r1.astype(BF16)
    lo = (r1 - mid.astype(F32)).astype(BF16)
    return hi, mid, lo


def _rpb_grad(dbias, *, name):
    heads = dbias.shape[0]
    kh = NA_WIN_ROWS
    e1, e2 = _na_onehots()
    x = dbias.reshape(heads, kh, GRID_W, kh, GRID_W).transpose(0, 1, 3, 2, 4).reshape(heads, kh * kh, GRID_W * GRID_W)

    def body(x_ref, e1_ref, e2_ref, o_ref):
        e2b = e2_ref[...].astype(BF16)
        y = sum(_dot(part, e2b, 1, 0) for part in _split3(x_ref[...]))
        e1b = e1_ref[...].astype(BF16)
        o_ref[...] = sum(_dot(e1b, part, 1, 0) for part in _split3(y))

    out = pl.pallas_call(
        body, name=name, grid=(heads,),
        in_specs=[pl.BlockSpec((None, kh * kh, GRID_W * GRID_W), lambda h: (h, 0, 0)),
                  pl.BlockSpec(e1.shape, lambda h: (0, 0)), pl.BlockSpec(e2.shape, lambda h: (0, 0))],
        out_specs=pl.BlockSpec((None, 16, LANES), lambda h: (h, 0, 0)),
        out_shape=jax.ShapeDtypeStruct((heads, 16, LANES), F32),
        compiler_params=_params(("parallel",)),
    )(x, e1, e2)
    return out[:, :2 * kh - 1, :2 * NA_WIN_COLS - 1]


def _rope_tables(s_len, l_len):
    nf = RET_KEY_DIM // 4
    t = np.arange(s_len)
    row = (t // GRID_W).astype(np.float32)
    colp = (t % GRID_W).astype(np.float32)
    inv_freq = jnp.asarray(ROPE_BASE, F32) ** (-jnp.arange(nf, dtype=F32) / nf)
    ang = jnp.concatenate([jnp.asarray(row)[:, None] * inv_freq, jnp.asarray(colp)[:, None] * inv_freq], axis=-1)
    cos, sin = jnp.cos(ang), jnp.sin(ang)
    c2 = jnp.concatenate([cos, cos], axis=-1)
    s2 = jnp.concatenate([-sin, sin], axis=-1)
    c2 = jnp.concatenate([c2, jnp.ones((l_len, RET_KEY_DIM), F32)], axis=0)
    s2 = jnp.concatenate([s2, jnp.zeros((l_len, RET_KEY_DIM), F32)], axis=0)
    return c2, s2


def _rope(x, c2, s2):
    return x * c2 + pltpu.roll(x, RET_KEY_DIM // 2, 1) * s2


def _rope_t(d, c2, s2):
    return d * c2 + pltpu.roll(d * s2, RET_KEY_DIM // 2, 1)


def _ret_decays(lg, direction):
    cs = RET_CHUNK
    i_col = lax.broadcasted_iota(jnp.int32, (cs, 1), 0)
    p_col = jnp.where(direction == 0, i_col, cs - 1 - i_col).astype(F32)
    pi = lax.broadcasted_iota(jnp.int32, (cs, cs), 0)
    pj = lax.broadcasted_iota(jnp.int32, (cs, cs), 1)
    diff = jnp.where(direction == 0, pi - pj, pj - pi).astype(F32)
    dm = jnp.where(diff >= 0, jnp.exp(jnp.maximum(diff, 0.0) * lg), 0.0)
    qdec = jnp.exp((p_col + 1.0) * lg)
    kdec = jnp.exp((cs - 1.0 - p_col) * lg)
    cd = jnp.exp(jnp.full((1, 1), cs, F32) * lg)
    return p_col, dm, qdec, kdec, cd


def _ret_chunk_index(t, direction, n_chunks, lat_chunks):
    return jnp.where(direction == 0, lax.rem(t + lat_chunks, n_chunks), n_chunks - 1 - t)


def _ret_fwd(u, c2, s2, lg, *, s_len, heads, q_off, name):
    t_len = u.shape[0]
    cs, dk, dv = RET_CHUNK, RET_KEY_DIM, RET_VAL_DIM
    n_chunks, lat_chunks = t_len // cs, s_len // cs
    k_scale = dk ** -0.5
    qb, kb, vb = q_off // dk, q_off // dk + heads, (q_off + 2 * heads * dk) // dv

    def body(lg_ref, q_ref, k_ref, v_ref, c_ref, s_ref, o_ref, st_ref, qd_s, kv_s):
        h, d = pl.program_id(0), pl.program_id(1)
        _, dm, qdec, kdec, cd = _ret_decays(lg_ref[d, h], d)
        n_g = max(g for g in RET_GROUPS if n_chunks % g == 0)
        rows_of = lambda c: pl.ds(pl.multiple_of(c * cs, cs), cs)

        def local(gi, carry):
            rws = [rows_of(gi * n_g + j) for j in range(n_g)]
            qcs = [_rope(q_ref[r, :].astype(F32), c_ref[r, :], s_ref[r, :]) for r in rws]
            kcs = [_rope(k_ref[r, :].astype(F32), c_ref[r, :], s_ref[r, :]) * k_scale for r in rws]
            vcs = [v_ref[r, :] for r in rws]
            a_raw = [_dot(qcs[j].astype(BF16), kcs[j].astype(BF16), 1, 1) for j in range(n_g)]
            kv = [_dot((kcs[j] * kdec).astype(BF16), vcs[j], 0, 0) for j in range(n_g)]
            inner = [_dot((a_raw[j] * dm).astype(BF16), vcs[j], 1, 0) for j in range(n_g)]
            for j in range(n_g):
                qd_s[rws[j], :] = (qcs[j] * qdec).astype(BF16)
                kv_s[gi * n_g + j] = kv[j]

            @pl.when(d == 0)
            def _():
                for j in range(n_g):
                    o_ref[rws[j], :] = inner[j]

            @pl.when(d == 1)
            def _():
                for j in range(n_g):
                    o_ref[rws[j], :] += inner[j]

            return carry

        lax.fori_loop(0, n_chunks // n_g, local, 0)

        def scan(t, st):
            st_ref[t] = st
            return st * cd + kv_s[_ret_chunk_index(t, d, n_chunks, lat_chunks)]

        lax.fori_loop(0, n_chunks, scan, jnp.zeros((dk, dv), F32))

        def cross(gi, carry):
            ts = [gi * n_g + j for j in range(n_g)]
            rws = [rows_of(_ret_chunk_index(t, d, n_chunks, lat_chunks)) for t in ts]
            outs = [_dot(qd_s[rws[j], :], st_ref[ts[j]].astype(BF16), 1, 0) for j in range(n_g)]
            for j in range(n_g):
                o_ref[rws[j], :] += outs[j]
            return carry

        lax.fori_loop(0, n_chunks // n_g, cross, 0)

    return pl.pallas_call(
        body, name=name, grid=(heads, 2),
        in_specs=[pl.BlockSpec(memory_space=pltpu.SMEM),
                  pl.BlockSpec((t_len, dk), lambda h, d: (0, qb + h)),
                  pl.BlockSpec((t_len, dk), lambda h, d: (0, kb + h)),
                  pl.BlockSpec((t_len, dv), lambda h, d: (0, vb + h)),
                  pl.BlockSpec((t_len, dk), lambda h, d: (0, 0)),
                  pl.BlockSpec((t_len, dk), lambda h, d: (0, 0))],
        out_specs=[pl.BlockSpec((t_len, dv), lambda h, d: (0, h)),
                   pl.BlockSpec((None, None, n_chunks, dk, dv), lambda h, d: (h, d, 0, 0, 0))],
        out_shape=[jax.ShapeDtypeStruct((t_len, heads * dv), F32),
                   jax.ShapeDtypeStruct((heads, 2, n_chunks, dk, dv), F32)],
        scratch_shapes=[pltpu.VMEM((t_len, dk), BF16), pltpu.VMEM((n_chunks, dk, dv), F32)],
        compiler_params=_params(("parallel", "arbitrary")),
    )(lg, u, u, u, c2, s2)


def _ret_bwd(u, c2, s2, lg, states, do, *, s_len, heads, q_off, name):
    t_len = u.shape[0]
    cs, dk, dv = RET_CHUNK, RET_KEY_DIM, RET_VAL_DIM
    n_chunks, lat_chunks = t_len // cs, s_len // cs
    k_scale = dk ** -0.5
    qb, kb, vb = q_off // dk, q_off // dk + heads, (q_off + 2 * heads * dk) // dv

    def body(lg_ref, q_ref, k_ref, v_ref, c_ref, s_ref, st_ref, do_ref, dq_ref, dk_ref, dv_ref, dlg_ref, acc, qdo_s, dst_s):
        h, d = pl.program_id(0), pl.program_id(1)
        p_col, dm, qdec, kdec, cd = _ret_decays(lg_ref[d, h], d)
        acc[...] = jnp.zeros_like(acc)
        n_g = max(g for g in RET_GROUPS[:2] if n_chunks % g == 0)
        rows_of = lambda c: pl.ds(pl.multiple_of(c * cs, cs), cs)
        chunk_of = lambda t: _ret_chunk_index(t, d, n_chunks, lat_chunks)

        def local(gi, carry):
            rws = [rows_of(gi * n_g + j) for j in range(n_g)]
            qds = [(_rope(q_ref[r, :].astype(F32), c_ref[r, :], s_ref[r, :]) * qdec).astype(BF16) for r in rws]
            prods = [_dot(qds[j], do_ref[rws[j], :].astype(BF16), 0, 0) for j in range(n_g)]
            for j in range(n_g):
                qdo_s[gi * n_g + j] = prods[j]
            return carry

        lax.fori_loop(0, n_chunks // n_g, local, 0)

        def scan(i, dst):
            t = n_chunks - 1 - i
            dst_s[t] = dst
            return dst * cd + qdo_s[chunk_of(t)]

        lax.fori_loop(0, n_chunks, scan, jnp.zeros((dk, dv), F32))

        def grads(gi, carry):
            ts = [gi * n_g + j for j in range(n_g)]
            rws = [rows_of(chunk_of(t)) for t in ts]
            ccs, sss = [c_ref[r, :] for r in rws], [s_ref[r, :] for r in rws]
            qcs = [_rope(q_ref[r, :].astype(F32), cc, ss) for r, cc, ss in zip(rws, ccs, sss)]
            kcs = [_rope(k_ref[r, :].astype(F32), cc, ss) * k_scale for r, cc, ss in zip(rws, ccs, sss)]
            vcs = [v_ref[r, :] for r in rws]
            docs = [do_ref[r, :].astype(BF16) for r in rws]
            sts = [st_ref[t] for t in ts]
            dsts = [dst_s[t] for t in ts]
            q16 = [x.astype(BF16) for x in qcs]
            k16 = [x.astype(BF16) for x in kcs]
            dst16 = [x.astype(BF16) for x in dsts]
            rng = range(n_g)
            a_raw = [_dot(q16[j], k16[j], 1, 1) for j in rng]
            da_raw = [_dot(docs[j], vcs[j], 1, 1) for j in rng]
            dq_c = [_dot(docs[j], sts[j].astype(BF16), 1, 1) * qdec for j in rng]
            dv_s = [_dot((kcs[j] * kdec).astype(BF16), dst16[j], 1, 0) for j in rng]
            dk_s = [_dot(vcs[j], dst16[j], 1, 1) * kdec for j in rng]
            a16 = [(a_raw[j] * dm).astype(BF16) for j in rng]
            dam = [(da_raw[j] * dm).astype(BF16) for j in rng]
            dq_i = [_dot(dam[j], k16[j], 1, 0) for j in rng]
            dk_i = [_dot(dam[j], q16[j], 0, 0) for j in rng]
            dv_i = [_dot(a16[j], docs[j], 0, 0) for j in rng]
            for j in rng:
                g = (jnp.sum(qcs[j] * (p_col * dq_i[j] + (p_col + 1.0) * dq_c[j]), axis=-1, keepdims=True)
                     + jnp.sum(kcs[j] * ((cs - 1.0 - p_col) * dk_s[j] - p_col * dk_i[j]), axis=-1, keepdims=True))
                g = (jnp.sum(g, axis=0, keepdims=True)
                     + cs * cd * jnp.sum(jnp.sum(dsts[j] * sts[j], axis=-1, keepdims=True), axis=0, keepdims=True))
                acc[...] += jnp.broadcast_to(g, acc.shape)
            dqs = [_rope_t(dq_i[j] + dq_c[j], ccs[j], sss[j]) for j in rng]
            dks = [_rope_t((dk_i[j] + dk_s[j]) * k_scale, ccs[j], sss[j]) for j in rng]
            dvs = [dv_i[j] + dv_s[j] for j in rng]

            @pl.when(d == 0)
            def _():
                for j in rng:
                    dq_ref[rws[j], :] = dqs[j].astype(dq_ref.dtype)
                    dk_ref[rws[j], :] = dks[j].astype(dk_ref.dtype)
                    dv_ref[rws[j], :] = dvs[j].astype(dv_ref.dtype)

            @pl.when(d == 1)
            def _():
                for j in rng:
                    dq_ref[rws[j], :] = (dq_ref[rws[j], :].astype(F32) + dqs[j]).astype(dq_ref.dtype)
                    dk_ref[rws[j], :] = (dk_ref[rws[j], :].astype(F32) + dks[j]).astype(dk_ref.dtype)
                    dv_ref[rws[j], :] = (dv_ref[rws[j], :].astype(F32) + dvs[j]).astype(dv_ref.dtype)

            return carry

        lax.fori_loop(0, n_chunks // n_g, grads, 0)
        dlg_ref[...] = acc[...]

    return pl.pallas_call(
        body, name=name, grid=(heads, 2),
        in_specs=[pl.BlockSpec(memory_space=pltpu.SMEM),
                  pl.BlockSpec((t_len, dk), lambda h, d: (0, qb + h)),
                  pl.BlockSpec((t_len, dk), lambda h, d: (0, kb + h)),
                  pl.BlockSpec((t_len, dv), lambda h, d: (0, vb + h)),
                  pl.BlockSpec((t_len, dk), lambda h, d: (0, 0)),
                  pl.BlockSpec((t_len, dk), lambda h, d: (0, 0)),
                  pl.BlockSpec((None, None, n_chunks, dk, dv), lambda h, d: (h, d, 0, 0, 0)),
                  pl.BlockSpec((t_len, dv), lambda h, d: (0, h))],
        out_specs=[pl.BlockSpec((t_len, dk), lambda h, d: (0, h)),
                   pl.BlockSpec((t_len, dk), lambda h, d: (0, h)),
                   pl.BlockSpec((t_len, dv), lambda h, d: (0, h)),
                   pl.BlockSpec((None, None, 8, LANES), lambda h, d: (h, d, 0, 0))],
        out_shape=[jax.ShapeDtypeStruct((t_len, heads * dk), BF16),
                   jax.ShapeDtypeStruct((t_len, heads * dk), BF16),
                   jax.ShapeDtypeStruct((t_len, heads * dv), BF16),
                   jax.ShapeDtypeStruct((heads, 2, 8, LANES), F32)],
        scratch_shapes=[pltpu.VMEM((8, LANES), F32), pltpu.VMEM((n_chunks, dk, dv), F32), pltpu.VMEM((n_chunks, dk, dv), F32)],
        compiler_params=_params(("parallel", "arbitrary")),
    )(lg, u, u, u, c2, s2, states, do)


def _mesh_pos():
    return lax.axis_index("x"), lax.axis_index("y"), lax.axis_index("c")


def _all_gather_small(buf, *, name):
    r = buf.shape[0]

    def body(x_ref, o_ref, send_sems, recv_sems, local_sem):
        x, y, c = _mesh_pos()
        me = 4 * x + 2 * y + c
        mine = pltpu.make_async_copy(x_ref, o_ref.at[me], local_sem)
        mine.start()
        copies = []
        for k in range(1, N_DEV):
            px, py, pc = x ^ ((k >> 2) & 1), y ^ ((k >> 1) & 1), c ^ (k & 1)
            cp = pltpu.make_async_remote_copy(
                src_ref=x_ref, dst_ref=o_ref.at[me], send_sem=send_sems.at[k - 1], recv_sem=recv_sems.at[k - 1],
                device_id=(px, py, pc), device_id_type=MESH)
            cp.start()
            copies.append((cp, 4 * px + 2 * py + pc))
        for k, (cp, peer) in enumerate(copies):
            pltpu.make_async_remote_copy(
                src_ref=x_ref, dst_ref=o_ref.at[peer], send_sem=send_sems.at[k], recv_sem=recv_sems.at[k],
                device_id=(x, y, c), device_id_type=MESH).wait_recv()
        for cp, _ in copies:
            cp.wait_send()
        mine.wait()

    return pl.pallas_call(
        body, name=name,
        in_specs=[pl.BlockSpec(memory_space=pltpu.VMEM)],
        out_specs=pl.BlockSpec(memory_space=pltpu.VMEM),
        out_shape=jax.ShapeDtypeStruct((N_DEV, r, LANES), F32),
        scratch_shapes=[pltpu.SemaphoreType.DMA((N_DEV - 1,)), pltpu.SemaphoreType.DMA((N_DEV - 1,)),
                        pltpu.SemaphoreType.DMA],
        compiler_params=pltpu.CompilerParams(vmem_limit_bytes=VMEM_LIMIT),
    )(buf)


def _cut(ref, shard_axis, *, chip=None, half=None, lead=None):
    shape = ref.shape[1:] if lead is not None else ref.shape
    idx = [slice(None), slice(None)]
    if chip is not None:
        w = shape[shard_axis] // N_CHIPS
        idx[shard_axis] = pl.ds(pl.multiple_of(chip * w, w), w)
    if half is not None:
        hw = shape[1 - shard_axis] // 2
        idx[1 - shard_axis] = pl.ds(pl.multiple_of(half * hw, hw), hw)
    if lead is not None:
        idx = [lead] + idx
    return ref.at[tuple(idx)]


def _wait_recv(ref, send_sem, recv_sem):
    pltpu.make_async_remote_copy(src_ref=ref, dst_ref=ref, send_sem=send_sem, recv_sem=recv_sem,
                                 device_id=_mesh_pos(), device_id_type=MESH).wait_recv()


def _gather_plan(axes):
    def plan(srcs, lands, send_sems, recv_sems):
        x, y, c = _mesh_pos()
        chip = 2 * x + y
        copies = []
        for i, ax in enumerate(axes):
            for k in range(1, N_CHIPS):
                px, py = x ^ (k >> 1), y ^ (k & 1)
                mine = _cut(lands[i], ax, chip=chip, half=c)
                j = i * (N_CHIPS - 1) + k - 1
                sems = dict(send_sem=send_sems.at[j], recv_sem=recv_sems.at[j], device_id=(px, py, c), device_id_type=MESH)
                send = pltpu.make_async_remote_copy(src_ref=mine, dst_ref=mine, **sems)
                recv = pltpu.make_async_remote_copy(src_ref=mine, dst_ref=_cut(lands[i], ax, chip=2 * px + py, half=c), **sems)
                copies.append((send, recv))
        return copies
    return plan


def _pair_plan(axes):
    def plan(srcs, lands, send_sems, recv_sems):
        x, y, c = _mesh_pos()
        copies = []
        for i, ax in enumerate(axes):
            cp = pltpu.make_async_remote_copy(
                src_ref=_cut(srcs[i], ax, half=1 - c), dst_ref=lands[i], send_sem=send_sems.at[i], recv_sem=recv_sems.at[i],
                device_id=(x, y, 1 - c), device_id_type=MESH)
            copies.append((cp, cp))
        return copies
    return plan


def _scatter_plan(axes):
    def plan(srcs, lands, send_sems, recv_sems):
        x, y, c = _mesh_pos()
        copies = []
        for i, ax in enumerate(axes):
            for k in range(1, N_CHIPS):
                px, py = x ^ (k >> 1), y ^ (k & 1)
                j = i * (N_CHIPS - 1) + k - 1
                cp = pltpu.make_async_remote_copy(
                    src_ref=_cut(srcs[i], ax, chip=2 * px + py), dst_ref=lands[i].at[k - 1],
                    send_sem=send_sems.at[j], recv_sem=recv_sems.at[j], device_id=(px, py, c), device_id_type=MESH)
                copies.append((cp, cp))
        return copies
    return plan


HBM = pl.BlockSpec(memory_space=pltpu.HBM)
SEM = pl.BlockSpec(memory_space=pltpu.SEMAPHORE)
EFFECT = pltpu.SideEffectType.DATAFLOW_SIDE_EFFECTING


def _in_hbm(arrays):
    return [pltpu.with_memory_space_constraint(a, pltpu.HBM) for a in arrays]


def _split_start(srcs, lands, plan, n_copies, *, name):
    bufs = list(srcs) + list(lands)
    ns, nb = len(srcs), len(bufs)

    def body(*refs):
        send_sems, recv_sems, token = refs[nb], refs[nb + 1], refs[-1]
        for send, _ in plan(refs[:ns], refs[ns:nb], send_sems, recv_sems):
            send.start()
        token[...] = jnp.zeros_like(token)

    sems = pltpu.SemaphoreType.DMA((n_copies,))
    res = pl.pallas_call(
        body, name=name, in_specs=[HBM] * nb,
        out_specs=[SEM, SEM] + [HBM] * nb + [pl.BlockSpec(memory_space=pltpu.VMEM)],
        out_shape=[sems, sems] + [pltpu.HBM(a.shape, a.dtype) for a in bufs] + [jax.ShapeDtypeStruct((8, LANES), F32)],
        input_output_aliases={j: 2 + j for j in range(nb)},
        compiler_params=pltpu.CompilerParams(has_side_effects=EFFECT),
    )(*_in_hbm(bufs))
    return res[0], res[1], res[2:2 + ns], res[2 + ns:2 + nb], res[-1]


def _split_wait(started, after, plan, *, name, with_srcs=False):
    send_sems, recv_sems, srcs, lands, _ = started
    bufs = list(srcs) + list(lands)
    ns, nb = len(srcs), len(bufs)

    def body(*refs):
        for send, recv in plan(refs[:ns], refs[ns:nb], refs[nb], refs[nb + 1]):
            send.wait_send()
            recv.wait_recv()

    res = pl.pallas_call(
        body, name=name, in_specs=[HBM] * nb + [SEM, SEM, ANY], out_specs=[HBM] * nb,
        out_shape=[pltpu.HBM(a.shape, a.dtype) for a in bufs],
        input_output_aliases={j: j for j in range(nb)},
        compiler_params=pltpu.CompilerParams(has_side_effects=EFFECT),
    )(*bufs, send_sems, recv_sems, after)
    return (res[:ns], res[ns:]) if with_srcs else res[ns:]


def _cast_into_full(w3, layer, ax, chip, *, after=None, name):
    _, r, wd = w3.shape
    tr = _rows_per_tile(r, wd, 4 << 20)
    nt = r // tr
    full_shape = (r, wd * N_CHIPS) if ax == 1 else (r * N_CHIPS, wd)
    out_map = (lambda i, ch: (i, ch[0])) if ax == 1 else (lambda i, ch: (ch[0] * nt + i, 0))
    zero = jnp.zeros((1, wd), F32) + (0.0 if after is None else after)

    def body(chip_ref, w_ref, z_ref, o_ref):
        o_ref[...] = (w_ref[...] + z_ref[...]).astype(o_ref.dtype)

    return pl.pallas_call(
        body, name=name,
        grid_spec=pltpu.PrefetchScalarGridSpec(
            num_scalar_prefetch=1, grid=(nt,),
            in_specs=[pl.BlockSpec((None, tr, wd), lambda i, ch: (layer, i, 0)), pl.BlockSpec((1, wd), lambda i, ch: (0, 0))],
            out_specs=pl.BlockSpec((tr, wd), out_map)),
        out_shape=jax.ShapeDtypeStruct(full_shape, BF16),
        compiler_params=_params(("parallel",)),
    )(jnp.reshape(chip, (1,)).astype(jnp.int32), w3, zero)


def _forward_halves(fulls, axes, *, name):
    n = len(fulls)

    def body(*refs):
        bufs = refs[:n]
        send_sems, recv_sems = refs[2 * n:]
        x, y, c = _mesh_pos()
        sends = []
        for i in range(n):
            for k in range(1, N_CHIPS):
                landed = _cut(bufs[i], axes[i], chip=2 * (x ^ (k >> 1)) + (y ^ (k & 1)), half=c)
                cp = pltpu.make_async_remote_copy(
                    src_ref=landed, dst_ref=landed, send_sem=send_sems.at[i, k - 1], recv_sem=recv_sems.at[i, k - 1],
                    device_id=(x, y, 1 - c), device_id_type=MESH)
                cp.start()
                sends.append(cp)
        for i in range(n):
            for k in range(1, N_CHIPS):
                other = _cut(bufs[i], axes[i], chip=2 * (x ^ (k >> 1)) + (y ^ (k & 1)), half=1 - c)
                _wait_recv(other, send_sems.at[i, k - 1], recv_sems.at[i, k - 1])
        for cp in sends:
            cp.wait_send()

    pairs = pltpu.SemaphoreType.DMA((n, N_CHIPS - 1))
    return pl.pallas_call(
        body, name=name, in_specs=[ANY] * n, out_specs=[ANY] * n,
        out_shape=[jax.ShapeDtypeStruct(a.shape, a.dtype) for a in fulls],
        input_output_aliases={j: j for j in range(n)},
        scratch_shapes=[pairs, pairs],
    )(*fulls)


def _share_halves_in_place(bufs, axes, *, name):
    n = len(bufs)

    def body(*refs):
        ins = refs[:n]
        send_sems, recv_sems = refs[2 * n:]
        x, y, c = _mesh_pos()
        sends = []
        for i in range(n):
            mine = _cut(ins[i], axes[i], half=c)
            cp = pltpu.make_async_remote_copy(
                src_ref=mine, dst_ref=mine, send_sem=send_sems.at[i], recv_sem=recv_sems.at[i],
                device_id=(x, y, 1 - c), device_id_type=MESH)
            cp.start()
            sends.append(cp)
        for i in range(n):
            _wait_recv(_cut(ins[i], axes[i], half=1 - c), send_sems.at[i], recv_sems.at[i])
        for cp in sends:
            cp.wait_send()

    sems = pltpu.SemaphoreType.DMA((n,))
    return pl.pallas_call(
        body, name=name, in_specs=[ANY] * n, out_specs=[ANY] * n,
        out_shape=[jax.ShapeDtypeStruct(b.shape, b.dtype) for b in bufs],
        input_output_aliases={j: j for j in range(n)}, scratch_shapes=[sems, sems],
    )(*bufs)


def _adamw_math(w, g, m, v):
    m = ADAM_B1 * m + (1.0 - ADAM_B1) * g
    v = ADAM_B2 * v + (1.0 - ADAM_B2) * (g * g)
    m_hat = m / (1.0 - ADAM_B1 ** ADAM_STEP)
    v_hat = v / (1.0 - ADAM_B2 ** ADAM_STEP)
    delta = -ADAM_LR * (m_hat / (jnp.sqrt(v_hat) + ADAM_EPS) + ADAM_WD * w)
    return delta, m, v


def _adamw_layer(w3, m3, v3, p, q, layer, prev, *, name):
    nl, rows, width = w3.shape
    tr = _rows_per_tile(rows, width)

    def fn(*t):
        if q is None:
            w, m, v, g = t
        else:
            w, m, v, g, g2 = t
            g = g + g2
        delta, m, v = _adamw_math(w, g, m, v)
        return g, delta, m, v

    ins = [('t', w3, 0, width, layer), ('t', m3, 0, width, layer), ('t', v3, 0, width, layer), ('t', p, 0, width)]
    if q is not None:
        ins.append(('t', q, 0, width))
    outs = [('t', width, F32, layer, nl)] * 4
    aliases = None if prev is None else [(prev[i], i) for i in range(4)]
    return _ew(fn, ins, outs, rows=rows, tr=tr, name=name, aliases=aliases)


def _pack_rows(vec):
    n = vec.shape[0]
    r = -(-n // (8 * LANES)) * 8
    return jnp.pad(vec, (0, r * LANES - n)).reshape(r, LANES)


def kernel(x, c, ctx, c_ctx, ada_w, ada_b, norm_g, w_in, na_rpb, ret_decay_logit, w_proj_na, w_proj_ret, w_out, final_g, loss_target, m_c_ctx, m_ada_w, m_ada_b, m_norm_g, m_w_in, m_na_rpb, m_ret_decay_logit, m_w_proj_na, m_w_proj_ret, m_w_out, m_final_g, v_c_ctx, v_ada_w, v_ada_b, v_norm_g, v_w_in, v_na_rpb, v_ret_decay_logit, v_w_proj_na, v_w_proj_ret, v_w_out, v_final_g):
    depth = w_in.shape[0]
    s_len, d_model = x.shape[1], x.shape[2]
    l_len = ctx.shape[1]
    t_len = s_len + l_len
    na_heads = na_rpb.shape[1]
    ret_heads = ret_decay_logit.shape[2]
    w_na = na_heads * NA_HEAD_DIM
    w_qk = ret_heads * RET_KEY_DIM
    w_v = ret_heads * RET_VAL_DIM
    in_cols = w_in.shape[2] * N_CHIPS
    assert in_cols == 4 * w_na + 2 * w_qk + 2 * w_v + 2 * d_model
    assert x.shape[0] == 1 and s_len % (NA_WIN_ROWS * GRID_W) == 0 and l_len % RET_CHUNK == 0
    off = np.cumsum([0, w_na, w_na, w_na, w_na, w_qk, w_qk, w_v, w_v, d_model, d_model])
    o_naz, o_retq, o_retz, o_gna, o_gret = int(off[3]), int(off[4]), int(off[7]), int(off[8]), int(off[9])
    rows = s_len // GRID_W
    tr = _tile(l_len, 256, 8)
    n0 = s_len // tr
    mod_cols = 3 * d_model
    mod_shard = ada_w.shape[2]

    xi, yi, ci = _mesh_pos()
    me = 4 * xi + 2 * yi + ci
    chip = 2 * xi + yi

    big_axes = [1, 1, 0, 0]
    n_big = len(big_axes) * (N_CHIPS - 1)
    gather_plan, scatter_plan = _gather_plan(big_axes), _scatter_plan(big_axes)

    c_silu = c[0] * _sigmoid(c[0])
    cc_silu = c_ctx * _sigmoid(c_ctx)
    c_all = _all_gather_small(_pack_rows(c_silu), name="gather_c")[:, :d_model // LANES].reshape(N_DEV, d_model)
    a_rows = jnp.concatenate([c_all, cc_silu[None], jnp.zeros((16 - N_DEV - 1, d_model), F32)], axis=0)
    mod_part = jnp.stack([_mm(a_rows, ada_w, b_lead=l, out_dtype=F32, name="ada_fwd_%d" % l) for l in range(depth)])
    mod_all = _all_gather_small(_pack_rows(mod_part.reshape(-1)), name="gather_mod")
    n_mod = depth * 16 * mod_shard
    mod_all = mod_all.reshape(N_DEV, -1)[:, :n_mod].reshape(N_CHIPS, 2, depth, 16, mod_shard)[:, 0]
    mod_all = jnp.transpose(mod_all, (1, 2, 0, 3)).reshape(depth, 16, mod_cols) + ada_b[:, None, :]

    big_named = list(zip((w_in, w_proj_na, w_proj_ret, w_out), big_axes, ("w_in", "w_proj_na", "w_proj_ret", "w_out")))
    w_in0 = _cast_into_full(w_in, 0, big_axes[0], chip, name="cast_w_in_0")
    mod_all, w_in0 = lax.optimization_barrier((mod_all, w_in0))
    plan_in, plan_rest = _gather_plan(big_axes[:1]), _gather_plan(big_axes[1:])
    first_gather = _split_start([], [w_in0], plan_in, N_CHIPS - 1, name="gather_start_0_in")
    start_token = first_gather[4][0, 0]
    fulls = [[None if (l == 0 and tag == "w_in") else _cast_into_full(w, l, ax, chip, after=start_token, name="cast_%s_%d" % (tag, l))
              for w, ax, tag in big_named] for l in range(depth)]
    mod_lat = lax.dynamic_index_in_dim(mod_all, me, axis=1, keepdims=False)
    mod_ctx = mod_all[:, N_DEV]

    c2, s2 = _rope_tables(s_len, l_len)
    log_gamma = jax.nn.log_sigmoid(ret_decay_logit)
    x_all = jnp.concatenate([x[0], ctx[0]], axis=0)

    def grp(lat_vec, ctx_vec):
        return jnp.stack([lat_vec, ctx_vec])[:, None, :]

    saved, full_w = [], []
    for l in range(depth):
        shift, scale, gate = [grp(mod_lat[l, i * d_model:(i + 1) * d_model], mod_ctx[l, i * d_model:(i + 1) * d_model])
                              for i in range(3)]
        gs = norm_g[l][None, None, :] * (1.0 + scale) + start_token

        def modnorm(xt, gs_t, sh_t):
            r = lax.rsqrt(jnp.mean(xt * xt, axis=-1, keepdims=True) + NORM_EPS)
            return xt * r * gs_t + sh_t

        h, = _ew(modnorm, [('t', x_all, 0, d_model), ('g', gs), ('g', shift)], [('t', d_model, BF16)],
                 rows=t_len, tr=tr, n0=n0, name="modnorm_%d" % l)
        bias = _na_bias_table(na_rpb[l], rows, name="na_bias_%d" % l)
        h, bias = lax.optimization_barrier((h, bias))
        if l == 0:
            landed_in = _split_wait(first_gather, h, plan_in, name="gather_wait_0_in")
            landed_in, rest0, later = lax.optimization_barrier((landed_in, fulls[0][1:], fulls[1:]))
            rest_gather = _split_start([], rest0, plan_rest, n_big - (N_CHIPS - 1), name="gather_start_0_rest")
            later_gathers = [_split_start([], later[j], gather_plan, n_big, name="gather_start_%d" % (j + 1)) for j in range(depth - 1)]
            win_f, = _forward_halves(landed_in, big_axes[:1], name="gather_forward_0_in")
            win_f, tokens = lax.optimization_barrier((win_f, [rest_gather[4]] + [g[4] for g in later_gathers]))
            gate = gate + sum(t[0, 0] for t in tokens)
        else:
            landed = _split_wait(later_gathers[l - 1], h, gather_plan, name="gather_wait_%d" % l)
            win_f, wpn_f, wpr_f, wout_f = _forward_halves(landed, big_axes, name="gather_forward_%d" % l)
        u = _mm(h, win_f, tm=1152, tn=1024, name="in_proj_%d" % l)
        o_na = _na_fwd(u, bias, s_len=s_len, heads=na_heads, name="na_fwd_%d" % l)
        o_ret, states = _ret_fwd(u, c2, s2, log_gamma[l], s_len=s_len, heads=ret_heads, q_off=o_retq, name="ret_fwd_%d" % l)

        def act(o1, z1, o2, z2):
            a1 = o1.astype(F32) * _silu_parts(z1.astype(F32))[0]
            sz = _silu_parts(z2.astype(F32))[0]
            outs = []
            for hh in range(ret_heads):
                sl = slice(hh * RET_VAL_DIM, (hh + 1) * RET_VAL_DIM)
                oh = o2[:, sl]
                r = lax.rsqrt(jnp.mean(oh * oh, axis=-1, keepdims=True) + NORM_EPS)
                outs.append(oh * r * sz[:, sl])
            return a1, jnp.concatenate(outs, axis=-1)

        a_na, a_ret = _ew(act, [('t', o_na, 0, w_na), ('t', u, o_naz // w_na, w_na), ('t', o_ret, 0, w_v), ('t', u, o_retz // w_v, w_v)],
                          [('t', w_na, BF16), ('t', w_v, BF16)], rows=t_len, tr=tr, name="act_%d" % l)
        if l == 0:
            landed_rest = _split_wait(rest_gather, a_na, plan_rest, name="gather_wait_0_rest")
            wpn_f, wpr_f, wout_f = _forward_halves(landed_rest, big_axes[1:], name="gather_forward_0_rest")
        full_w.append((win_f, wpn_f, wpr_f, wout_f))
        y_na = _mm(a_na, wpn_f, name="proj_na_%d" % l)
        y_ret = _mm(a_ret, wpr_f, name="proj_ret_%d" % l)

        def merge(y1, y2, g1, g2):
            return _sigmoid(g1.astype(F32)) * y1.astype(F32) + _sigmoid(g2.astype(F32)) * y2.astype(F32)

        merged, = _ew(merge, [('t', y_na, 0, d_model), ('t', y_ret, 0, d_model), ('t', u, o_gna // d_model, d_model), ('t', u, o_gret // d_model, d_model)],
                      [('t', d_model, BF16)], rows=t_len, tr=tr, name="merge_%d" % l)
        out = _mm(merged, wout_f, out_dtype=F32, name="out_proj_%d" % l)
        x_new, = _ew(lambda xt, ot, gt: xt + gt * ot, [('t', x_all, 0, d_model), ('t', out, 0, d_model), ('g', gate)],
                     [('t', d_model, F32)], rows=t_len, tr=tr, n0=n0, name="resid_%d" % l)
        saved.append(dict(x=x_all, h=h, u=u, bias=bias, o_na=o_na, o_ret=o_ret, states=states, a_na=a_na, a_ret=a_ret,
                          y_na=y_na, y_ret=y_ret, merged=merged, out=out, gate=gate, gs=gs, scale=scale))
        x_all = x_new

    def final(xt, tt, gt):
        r = lax.rsqrt(jnp.mean(xt * xt, axis=-1, keepdims=True) + NORM_EPS)
        xh = xt * r
        e = xh * gt - tt
        dy = e * (1.0 / d_model)
        dyg = dy * gt
        dx = r * (dyg - xh * jnp.mean(dyg * xh, axis=-1, keepdims=True))
        return dx, _rsum(dy * xh), _rsum(e * e)

    dx_lat, d_final_g, loss_cols = _ew(final, [('t', x_all, 0, d_model), ('t', loss_target[0], 0, d_model), ('g', final_g[None, None, :])],
                                       [('t', d_model, F32), ('r', d_model, 1), ('r', d_model, 1)], rows=s_len, tr=tr, name="final")
    loss_part = (0.5 / d_model) * jnp.sum(loss_cols)
    dx_all = jnp.concatenate([dx_lat, jnp.zeros((l_len, d_model), F32)], axis=0)

    big_w = [(w_in, m_w_in, v_w_in), (w_proj_na, m_w_proj_na, v_w_proj_na), (w_proj_ret, m_w_proj_ret, v_w_proj_ret), (w_out, m_w_out, v_w_out)]
    big_res = [None] * 4
    scatters = {}
    back_token = jnp.zeros((), F32)

    pairs = {}

    def start_pair(key, grads, axes):
        plan = _pair_plan(axes)
        lands = []
        for g, ax in zip(grads, axes):
            shp = list(g.shape)
            shp[1 - ax] //= 2
            lands.append(lax.empty(tuple(shp), BF16))
        pairs[key] = (_split_start(grads, lands, plan, len(axes), name="pair_start_%s" % key), axes, plan)
        return pairs[key][0][4]

    def start_scatter(key, after):
        started, axes, pair_plan = pairs[key]
        grads, theirs = _split_wait(started, after, pair_plan, name="pair_wait_%s" % key, with_srcs=True)
        plan = _scatter_plan(axes)
        pair = [_sum_pair(g, t, ax, ci, name="sum_pair_%s_%d" % (key, i)) for i, (g, t, ax) in enumerate(zip(grads, theirs, axes))]
        own = [lax.dynamic_slice_in_dim(s, chip * (s.shape[ax] // N_CHIPS), s.shape[ax] // N_CHIPS, axis=ax) for s, ax in zip(pair, axes)]
        lands = [lax.empty((N_CHIPS - 1,) + o.shape, BF16) for o in own]
        started = _split_start(pair, lands, plan, len(axes) * (N_CHIPS - 1), name="scatter_start_%s" % key)
        scatters[key] = (started, own, axes, plan)
        return started[4]

    def finish_scatter(key, after):
        started, own, axes, plan = scatters[key]
        recv = _split_wait(started, after, plan, name="scatter_wait_%s" % key)
        bufs = [_sum_chips_into(own[i], rbuf, axes[i], ci, name="sum_chips_%s_%d" % (key, i)) for i, rbuf in enumerate(recv)]
        return _share_halves_in_place(bufs, axes, name="share_halves_%s" % key)

    def adamw_big(l, idx, grads, big_res):
        for i, g in zip(idx, grads):
            w3, m3, v3 = big_w[i]
            big_res[i] = _adamw_layer(w3, m3, v3, g, None, l, big_res[i], name="adamw_big_%d_%d" % (i, l))
        return big_res

    small = dict(dmod_lat=[None] * depth, dmod_ctx=[None] * depth, dnorm_g=[None] * depth, drpb=[None] * depth, ddecay=[None] * depth)
    for l in reversed(range(depth)):
        sv = saved[l]
        win_f, wpn_f, wpr_f, wout_f = full_w[l]

        def resid_bwd(dxt, ot, gt):
            return gt * dxt, _rsum(dxt * ot)

        dout, dgate = _ew(resid_bwd, [('t', dx_all, 0, d_model), ('t', sv['out'], 0, d_model), ('g', sv['gate'] + back_token)],
                          [('t', d_model, BF16), ('r', d_model, 2)], rows=t_len, tr=tr, n0=n0, name="resid_bwd_%d" % l)
        dmerged = _mm(dout, wout_f, tb=True, name="out_proj_dx_%d" % l)
        g_wout = _mm(sv['merged'], dout, ta=True, tm=1024, tk=t_len, name="out_proj_dw_%d" % l)

        def merge_bwd(dm, y1, y2, g1, g2):
            dm = dm.astype(F32)
            s1, s2_ = _sigmoid(g1.astype(F32)), _sigmoid(g2.astype(F32))
            return dm * s1, dm * s2_, dm * y1.astype(F32) * s1 * (1.0 - s1), dm * y2.astype(F32) * s2_ * (1.0 - s2_)

        u = sv['u']
        dy_na, dy_ret, dg_na, dg_ret = _ew(
            merge_bwd, [('t', dmerged, 0, d_model), ('t', sv['y_na'], 0, d_model), ('t', sv['y_ret'], 0, d_model),
                        ('t', u, o_gna // d_model, d_model), ('t', u, o_gret // d_model, d_model)],
            [('t', d_model, BF16)] * 4, rows=t_len, tr=tr, name="merge_bwd_%d" % l)
        da_na = _mm(dy_na, wpn_f, tb=True, name="proj_na_dx_%d" % l)
        g_wpn = _mm(sv['a_na'], dy_na, ta=True, tm=1024, tk=t_len, name="proj_na_dw_%d" % l)
        da_ret = _mm(dy_ret, wpr_f, tb=True, name="proj_ret_dx_%d" % l)
        g_wpr = _mm(sv['a_ret'], dy_ret, ta=True, tm=1024, tk=t_len, name="proj_ret_dw_%d" % l)
        lg_l = log_gamma[l]
        if l == 0:
            pair_token = start_pair("0_rest", [g_wpn, g_wpr, g_wout], big_axes[1:])

        def act_bwd(da1, o1, z1, da2, o2, z2):
            da1, da2 = da1.astype(F32), da2.astype(F32)
            si1, ds1 = _silu_parts(z1.astype(F32))
            si2, ds2 = _silu_parts(z2.astype(F32))
            do1 = da1 * si1
            dz1 = da1 * o1.astype(F32) * ds1
            dn = da2 * si2
            do2, dz2 = [], []
            for hh in range(ret_heads):
                sl = slice(hh * RET_VAL_DIM, (hh + 1) * RET_VAL_DIM)
                oh = o2[:, sl]
                r = lax.rsqrt(jnp.mean(oh * oh, axis=-1, keepdims=True) + NORM_EPS)
                nh = oh * r
                dz2.append(da2[:, sl] * nh * ds2[:, sl])
                do2.append(r * (dn[:, sl] - nh * jnp.mean(dn[:, sl] * nh, axis=-1, keepdims=True)))
            return do1, dz1, jnp.concatenate(do2, axis=-1), jnp.concatenate(dz2, axis=-1)

        do_na, dz_na, do_ret, dz_ret = _ew(
            act_bwd, [('t', da_na, 0, w_na), ('t', sv['o_na'], 0, w_na), ('t', u, o_naz // w_na, w_na),
                      ('t', da_ret, 0, w_v), ('t', sv['o_ret'], 0, w_v), ('t', u, o_retz // w_v, w_v)],
            [('t', w_na, BF16), ('t', w_na, BF16), ('t', w_v, BF16), ('t', w_v, BF16)], rows=t_len, tr=tr, name="act_bwd_%d" % l)
        dq_na, dk_na, dv_na, dbias = _na_bwd(u, sv['bias'], sv['o_na'], do_na, s_len=s_len, heads=na_heads, name="na_bwd_%d" % l)
        small['drpb'][l] = _rpb_grad(dbias, name="rpb_grad_%d" % l)
        if l == 0:
            lg_l = lg_l + start_scatter("0_rest", dq_na)[0, 0] + pair_token[0, 0]
        dq_r, dk_r, dv_r, dlg = _ret_bwd(u, c2, s2, lg_l, sv['states'], do_ret, s_len=s_len, heads=ret_heads,
                                         q_off=o_retq, name="ret_bwd_%d" % l)
        small['ddecay'][l] = jnp.transpose(dlg[:, :, 0, 0]) * _sigmoid(-ret_decay_logit[l])
        du_parts = [dq_na, dk_na, dv_na, dz_na, dq_r, dk_r, dv_r, dz_ret, dg_na, dg_ret]
        du, = _ew(lambda *t: jnp.concatenate(t, axis=-1), [('t', p, 0, p.shape[1]) for p in du_parts], [('t', in_cols, BF16)],
                  rows=t_len, tr=tr, name="du_concat_%d" % l)
        g_win = _mm(sv['h'], du, ta=True, tm=1024, tn=1024, tk=t_len, name="in_proj_dw_%d" % l)
        if l > 0:
            du, pair_token = lax.optimization_barrier((du, start_pair("%d_all" % l, [g_win, g_wpn, g_wpr, g_wout], big_axes)))
        else:
            du, in_token = lax.optimization_barrier((du, start_pair("0_in", [g_win], big_axes[:1])))
        dh = _mm(du, win_f, tb=True, out_dtype=F32, tm=1152, tn=1024, name="in_proj_dx_%d" % l)

        def modnorm_bwd(xt, dht, dxt, gs_t):
            r = lax.rsqrt(jnp.mean(xt * xt, axis=-1, keepdims=True) + NORM_EPS)
            xh = xt * r
            dhg = dht * gs_t
            dx = r * (dhg - xh * jnp.mean(dhg * xh, axis=-1, keepdims=True)) + dxt
            return dx, _rsum(dht), _rsum(dht * xh)

        dx_all, dshift, dgs = _ew(modnorm_bwd, [('t', sv['x'], 0, d_model), ('t', dh, 0, d_model), ('t', dx_all, 0, d_model), ('g', sv['gs'])],
                                  [('t', d_model, F32), ('r', d_model, 2), ('r', d_model, 2)], rows=t_len, tr=tr, n0=n0, name="modnorm_bwd_%d" % l)
        dscale = dgs * norm_g[l][None, None, :]
        small['dnorm_g'][l] = jnp.sum(dgs * (1.0 + sv['scale']), axis=(0, 1))
        dmod = jnp.concatenate([dshift, dscale, dgate], axis=-1)[:, 0]
        small['dmod_lat'][l], small['dmod_ctx'][l] = dmod[0], dmod[1]

        if l > 0:
            back_token = start_scatter("%d_all" % l, dx_all)[0, 0] + pair_token[0, 0]

    grad_x = dx_all[:s_len][None]

    drpb = jnp.stack(small['drpb']).reshape(-1)
    ddecay = jnp.stack(small['ddecay']).reshape(-1)
    pieces = [jnp.stack(small['dmod_lat']).reshape(-1), jnp.stack(small['dmod_ctx']).reshape(-1),
              jnp.stack(small['dnorm_g']).reshape(-1), d_final_g.reshape(-1), drpb, ddecay, loss_part[None]]
    sizes = [int(p.shape[0]) for p in pieces]
    pads = [-(-s // LANES) * LANES for s in sizes]
    packed = jnp.concatenate([jnp.pad(p, (0, pd - s)) for p, s, pd in zip(pieces, sizes, pads)])
    gathered = _all_gather_small(_pack_rows(packed), name="gather_small_grads")
    r_small = gathered.shape[1]

    def sum8(*t):
        acc = t[0]
        for other in t[1:]:
            acc = acc + other
        return acc

    total, = _ew(sum8, [('t', gathered, 0, LANES, k) for k in range(N_DEV)], [('t', LANES, F32)], rows=r_small, tr=r_small, name="sum_devices")
    total = total.reshape(-1)
    starts = np.cumsum([0] + pads)
    g_mod_lat_sum, g_mod_ctx, g_norm_g, g_final_g, g_rpb, g_decay, loss = [total[starts[i]:starts[i] + sizes[i]] for i in range(len(pieces))]
    loss = loss[0]
    g_ada_b = (g_mod_lat_sum + g_mod_ctx).reshape(depth, mod_cols)
    g_mod_ctx = g_mod_ctx.reshape(depth, mod_cols)
    dmod_lat_all = gathered.reshape(N_DEV, -1)[:, :depth * mod_cols].reshape(N_DEV, depth, mod_cols)

    dcc_part = jnp.zeros((16, d_model), F32)
    ctx_cols = [lax.dynamic_slice_in_dim(g_mod_ctx[l], chip * mod_shard, mod_shard, axis=0) for l in range(depth)]
    for l in reversed(range(depth)):
        c_rows = jnp.concatenate([ctx_cols[l][None], jnp.zeros((15, mod_shard), F32)], axis=0)
        dcc_part = dcc_part + _mm(c_rows, ada_w, tb=True, b_lead=l, out_dtype=F32, name="ada_dc_%d" % l)
    dcc_all = _all_gather_small(_pack_rows(dcc_part[0]), name="gather_dcc")[:, :d_model // LANES].reshape(N_CHIPS, 2, d_model)[:, 0]

    tail_token = start_scatter("0_in", dcc_all) + in_token
    dcc = ((dcc_all[0] + dcc_all[1]) + dcc_all[2]) + dcc_all[3]
    sg = _sigmoid(c_ctx)
    g_c_ctx = dcc * (sg * (1.0 + c_ctx * (1.0 - sg)))
    for l in reversed(range(1, depth)):
        big_res = adamw_big(l, range(4), finish_scatter("%d_all" % l, tail_token), big_res)

    ada_res = None
    for l in reversed(range(depth)):
        lat_cols = lax.dynamic_slice_in_dim(dmod_lat_all[:, l], chip * mod_shard, mod_shard, axis=1)
        d_rows = jnp.concatenate([lat_cols, ctx_cols[l][None], jnp.zeros((16 - N_DEV - 1, mod_shard), F32)], axis=0) + tail_token[0, 0]
        g_ada = _mm(a_rows, d_rows, ta=True, out_dtype=F32, tm=512, name="ada_dw_%d" % l)
        ada_res = _adamw_layer(ada_w, m_ada_w, v_ada_w, g_ada, None, l, ada_res, name="adamw_ada_%d" % l)

    small_w = [(c_ctx, m_c_ctx, v_c_ctx, g_c_ctx), (ada_b, m_ada_b, v_ada_b, g_ada_b),
               (norm_g, m_norm_g, v_norm_g, g_norm_g), (na_rpb, m_na_rpb, v_na_rpb, g_rpb),
               (ret_decay_logit, m_ret_decay_logit, v_ret_decay_logit, g_decay), (final_g, m_final_g, v_final_g, g_final_g)]
    sw_sizes = [int(np.prod(t[0].shape)) for t in small_w]
    sw_pads = [-(-s // LANES) * LANES for s in sw_sizes]

    def pack(j):
        return _pack_rows(jnp.concatenate([jnp.pad(t[j].reshape(-1), (0, pd - s)) for t, s, pd in zip(small_w, sw_sizes, sw_pads)]))

    pw_, pm_, pv_, pg_ = pack(0), pack(1), pack(2), pack(3)
    sw_out = _ew(lambda w, m, v, g: (g,) + _adamw_math(w, g, m, v),
                 [('t', pw_, 0, LANES), ('t', pm_, 0, LANES), ('t', pv_, 0, LANES), ('t', pg_, 0, LANES)],
                 [('t', LANES, F32)] * 4, rows=pw_.shape[0], tr=pw_.shape[0], name="adamw_small")
    sw_starts = np.cumsum([0] + sw_pads)
    sw_out, ada_res, big_res = lax.optimization_barrier((sw_out, ada_res, big_res))
    big_res = adamw_big(0, range(1, 4), finish_scatter("0_rest", sw_out[0]), big_res)
    big_res = adamw_big(0, range(1), finish_scatter("0_in", sw_out[1]), big_res)

    def unpack(arr, i):
        return arr.reshape(-1)[sw_starts[i]:sw_starts[i] + sw_sizes[i]].reshape(small_w[i][0].shape)

    sm = [[unpack(sw_out[j], i) for i in range(len(small_w))] for j in range(4)]
    def ordered(j):
        return [sm[j][0], ada_res[j], sm[j][1], sm[j][2], big_res[0][j], sm[j][3], sm[j][4],
                big_res[1][j], big_res[2][j], big_res[3][j], sm[j][5]]

    return (loss, grad_x, *ordered(0), *ordered(1), *ordered(2), *ordered(3))
```

```python
import functools
import math

import numpy as np
import jax
import jax.numpy as jnp
from jax import lax
from jax.experimental import pallas as pl
from jax.experimental.pallas import tpu as pltpu

GRID_W = 64
NA_HEAD_DIM = 128
NA_WIN_ROWS = 8
NA_WIN_COLS = 16
NA_GROUP = 8
RET_GROUPS = (1, 2, 3)
RET_KEY_DIM = 128
RET_VAL_DIM = 256
RET_CHUNK = 128
ROPE_BASE = 10000.0
NORM_EPS = 1e-6
MASK_VALUE = -1e30
ADAM_LR = 0.001
ADAM_B1 = 0.9
ADAM_B2 = 0.999
ADAM_EPS = 1e-08
ADAM_WD = 0.01
ADAM_STEP = 10

N_CHIPS = 4
N_DEV = 8
LANES = 128
VMEM_LIMIT = 56 * 1024 * 1024
BF16 = jnp.bfloat16
F32 = jnp.float32
MESH = pl.DeviceIdType.MESH
ANY = pl.BlockSpec(memory_space=pl.ANY)


def _tile(dim, pref, align=LANES):
    if dim <= pref:
        return dim
    t = (pref // align) * align
    while t >= align:
        if dim % t == 0:
            return t
        t -= align
    return dim


def _rows_per_tile(rows, width, tile_bytes=1 << 20):
    return _tile(rows, max(8, tile_bytes // (4 * width)), 8)


def _params(sem):
    return pltpu.CompilerParams(dimension_semantics=sem, vmem_limit_bytes=VMEM_LIMIT)


def _sigmoid(x):
    return 1.0 / (1.0 + jnp.exp(-x))


def _dot(a, b, ca, cb):
    return lax.dot_general(a, b, (((ca,), (cb,)), ((), ())), preferred_element_type=F32)


def _mm(a, b, *, ta=False, tb=False, a_lead=None, b_lead=None, out_dtype=BF16, tm=1152, tn=1024, tk=2048, name):
    ash = a.shape[1:] if a_lead is not None else a.shape
    bsh = b.shape[1:] if b_lead is not None else b.shape
    m, k = (ash[1], ash[0]) if ta else ash
    n, k2 = bsh if tb else (bsh[1], bsh[0])
    assert k == k2, (name, ash, bsh)
    tm, tn, tk = _tile(m, tm), _tile(n, tn), _tile(k, tk)
    nk = k // tk

    def lead(spec_shape, imap, l):
        if l is None:
            return pl.BlockSpec(spec_shape, imap)
        return pl.BlockSpec((None,) + spec_shape, lambda i, j, kk: (l,) + imap(i, j, kk))

    a_spec = lead((tk, tm), lambda i, j, kk: (kk, i), a_lead) if ta else lead((tm, tk), lambda i, j, kk: (i, kk), a_lead)
    b_spec = lead((tn, tk), lambda i, j, kk: (j, kk), b_lead) if tb else lead((tk, tn), lambda i, j, kk: (kk, j), b_lead)
    ca, cb = (0 if ta else 1), (1 if tb else 0)

    def body(a_ref, b_ref, o_ref, *scratch):
        part = _dot(a_ref[...].astype(BF16), b_ref[...].astype(BF16), ca, cb)
        if nk == 1:
            o_ref[...] = part.astype(o_ref.dtype)
            return
        acc_ref, = scratch
        kk = pl.program_id(2)

        @pl.when(kk == 0)
        def _():
            acc_ref[...] = part

        @pl.when(kk > 0)
        def _():
            acc_ref[...] += part

        @pl.when(kk == nk - 1)
        def _():
            o_ref[...] = acc_ref[...].astype(o_ref.dtype)

    return pl.pallas_call(
        body, name=name, grid=(m // tm, n // tn, nk),
        in_specs=[a_spec, b_spec],
        out_specs=pl.BlockSpec((tm, tn), lambda i, j, kk: (i, j)),
        out_shape=jax.ShapeDtypeStruct((m, n), out_dtype),
        scratch_shapes=[] if nk == 1 else [pltpu.VMEM((tm, tn), F32)],
        compiler_params=_params(("parallel", "parallel", "arbitrary")),
    )(a, b)


def _ew(fn, ins, outs, *, rows, tr, name, n0=None, aliases=None):
    assert rows % tr == 0, (name, rows, tr)
    nt = rows // tr

    def grp(i):
        return 0 if n0 is None else jnp.where(i < n0, 0, 1)

    in_specs, args = [], []
    for spec in ins:
        if spec[0] == 't':
            arr, cb, w = spec[1], spec[2], spec[3]
            l = spec[4] if len(spec) > 4 else None
            if l is None:
                in_specs.append(pl.BlockSpec((tr, w), functools.partial(lambda i, cb: (i, cb), cb=cb)))
            else:
                in_specs.append(pl.BlockSpec((None, tr, w), functools.partial(lambda i, cb, l: (l, i, cb), cb=cb, l=l)))
            args.append(arr)
        else:
            arr = spec[1]
            g = arr.shape[0]
            if g == 1:
                in_specs.append(pl.BlockSpec((None, 1, arr.shape[2]), lambda i: (0, 0, 0)))
            else:
                in_specs.append(pl.BlockSpec((None, 1, arr.shape[2]), lambda i: (grp(i), 0, 0)))
            args.append(arr)
    out_specs, out_shapes, is_red = [], [], []
    for spec in outs:
        if spec[0] == 't':
            w, dt = spec[1], spec[2]
            if len(spec) > 3:
                l, nl = spec[3], spec[4]
                out_specs.append(pl.BlockSpec((None, tr, w), functools.partial(lambda i, l: (l, i, 0), l=l)))
                out_shapes.append(jax.ShapeDtypeStruct((nl, rows, w), dt))
            else:
                out_specs.append(pl.BlockSpec((tr, w), lambda i: (i, 0)))
                out_shapes.append(jax.ShapeDtypeStruct((rows, w), dt))
            is_red.append(False)
        else:
            w, g = spec[1], spec[2]
            if g == 1:
                out_specs.append(pl.BlockSpec((None, 1, w), lambda i: (0, 0, 0)))
            else:
                out_specs.append(pl.BlockSpec((None, 1, w), lambda i: (grp(i), 0, 0)))
            out_shapes.append(jax.ShapeDtypeStruct((g, 1, w), F32))
            is_red.append(True)
    n_in = len(ins)
    n_alias = 0 if aliases is None else len(aliases)

    def body(*refs):
        in_refs = refs[:n_in]
        out_refs = refs[n_in + n_alias:]
        res = fn(*[r[...] for r in in_refs])
        if not isinstance(res, (tuple, list)):
            res = (res,)
        i = pl.program_id(0)
        first = (i == 0) if n0 is None else ((i == 0) | (i == n0))
        for o_ref, val, red in zip(out_refs, res, is_red):
            if not red:
                o_ref[...] = val.astype(o_ref.dtype)
            else:
                @pl.when(first)
                def _(o_ref=o_ref, val=val):
                    o_ref[...] = val

                @pl.when(jnp.logical_not(first))
                def _(o_ref=o_ref, val=val):
                    o_ref[...] += val

    io_alias = {}
    if aliases is not None:
        for a_idx, (arr, o_idx) in enumerate(aliases):
            in_specs.append(ANY)
            args.append(arr)
            io_alias[n_in + a_idx] = o_idx
    has_red = any(is_red)
    return pl.pallas_call(
        body, name=name, grid=(nt,), in_specs=in_specs, out_specs=out_specs, out_shape=out_shapes,
        input_output_aliases=io_alias,
        compiler_params=_params(("arbitrary",) if has_red else ("parallel",)),
    )(*args)


def _half_spec(tr, width, ax, n_tiles):
    if ax == 1:
        return pl.BlockSpec((tr, width), lambda i, sel: (sel[0] * n_tiles + i, 0))
    return pl.BlockSpec((tr, width), lambda i, sel: (i, sel[0]))


def _sum_pair(g, theirs, ax, ci, *, name):
    pr, pw = theirs.shape
    tr = _rows_per_tile(pr, pw)
    nt = pr // tr

    def body(sel, a_ref, b_ref, o_ref):
        o_ref[...] = (a_ref[...].astype(F32) + b_ref[...].astype(F32)).astype(o_ref.dtype)

    return pl.pallas_call(
        body, name=name,
        grid_spec=pltpu.PrefetchScalarGridSpec(
            num_scalar_prefetch=1, grid=(nt,),
            in_specs=[_half_spec(tr, pw, ax, nt), pl.BlockSpec((tr, pw), lambda i, sel: (i, 0))],
            out_specs=pl.BlockSpec((tr, pw), lambda i, sel: (i, 0))),
        out_shape=jax.ShapeDtypeStruct((pr, pw), BF16),
        compiler_params=_params(("parallel",)),
    )(jnp.reshape(ci, (1,)).astype(jnp.int32), g, theirs)


def _sum_chips_into(own, recv, ax, ci, *, name):
    pr, pw = own.shape
    tr = _rows_per_tile(pr, pw)
    nt = pr // tr
    full_shape = (2 * pr, pw) if ax == 1 else (pr, 2 * pw)

    def body(sel, a_ref, r_ref, o_ref):
        acc = a_ref[...].astype(F32)
        for k in range(N_CHIPS - 1):
            acc = acc + r_ref[k].astype(F32)
        o_ref[...] = acc

    return pl.pallas_call(
        body, name=name,
        grid_spec=pltpu.PrefetchScalarGridSpec(
            num_scalar_prefetch=1, grid=(nt,),
            in_specs=[pl.BlockSpec((tr, pw), lambda i, sel: (i, 0)), pl.BlockSpec((N_CHIPS - 1, tr, pw), lambda i, sel: (0, i, 0))],
            out_specs=_half_spec(tr, pw, ax, nt)),
        out_shape=jax.ShapeDtypeStruct(full_shape, F32),
        compiler_params=_params(("parallel",)),
    )(jnp.reshape(ci, (1,)).astype(jnp.int32), own, recv)


def _rsum(v):
    return jnp.sum(v, axis=0, keepdims=True)


def _silu_parts(z):
    sg = _sigmoid(z)
    return z * sg, sg * (1.0 + z * (1.0 - sg))


def _na_bias_table(rpb, rows, *, name):
    kh, kw = NA_WIN_ROWS, NA_WIN_COLS
    assert rows >= kh
    heads = rpb.shape[0]
    e1, e2 = _na_onehots()
    rpb16 = jnp.pad(rpb, ((0, 0), (0, 16 - rpb.shape[1]), (0, LANES - rpb.shape[2])))

    def body(r_ref, e1_ref, e2_ref, o_ref):
        e1b = e1_ref[...].astype(BF16)
        y = sum(_dot(e1b, part, 0, 0) for part in _split3(r_ref[...]))
        e2b = e2_ref[...].astype(BF16)
        o_ref[...] = sum(_dot(part, e2b, 1, 1) for part in _split3(y))

    z = pl.pallas_call(
        body, name=name, grid=(heads,),
        in_specs=[pl.BlockSpec((None, 16, LANES), lambda h: (h, 0, 0)),
                  pl.BlockSpec(e1.shape, lambda h: (0, 0)), pl.BlockSpec(e2.shape, lambda h: (0, 0))],
        out_specs=pl.BlockSpec((None, kh * kh, GRID_W * GRID_W), lambda h: (h, 0, 0)),
        out_shape=jax.ShapeDtypeStruct((heads, kh * kh, GRID_W * GRID_W), F32),
        compiler_params=_params(("parallel",)),
    )(rpb16, e1, e2)
    cidx = np.arange(GRID_W)
    c0 = np.clip(cidx - kw // 2, 0, GRID_W - kw)
    col_in = (cidx[None, :] >= c0[:, None]) & (cidx[None, :] < c0[:, None] + kw)
    bias = z.reshape(heads, kh, kh, GRID_W, GRID_W).transpose(0, 1, 3, 2, 4)
    bias = jnp.where(col_in[None, None, :, None, :], bias, MASK_VALUE)
    return bias.reshape(heads, kh, GRID_W, kh * GRID_W)


def _na_onehots():
    kh, kw = NA_WIN_ROWS, NA_WIN_COLS
    cidx = np.arange(GRID_W)
    dc = cidx[None, :] - cidx[:, None] + (kw - 1)
    e2 = np.zeros((GRID_W * GRID_W, LANES), np.float32)
    ok = (dc >= 0) & (dc <= 2 * kw - 2)
    cq, ck = np.nonzero(ok)
    e2[cq * GRID_W + ck, dc[cq, ck]] = 1.0
    dr = np.arange(kh)[None, :] - np.arange(kh)[:, None] + (kh - 1)
    e1 = np.zeros((16, kh * kh), np.float32)
    dl, kr = np.nonzero(np.ones_like(dr))
    e1[dr[dl, kr], dl * kh + kr] = 1.0
    return jnp.asarray(e1), jnp.asarray(e2)


def _na_fwd(u, bias, *, s_len, heads, name):
    t_len = u.shape[0]
    rows = s_len // GRID_W
    nloc = NA_WIN_ROWS * GRID_W
    scale = NA_HEAD_DIM ** -0.5
    hd = NA_HEAD_DIM

    def body(q_ref, k_ref, v_ref, b_ref, o_ref):
        kc = k_ref[s_len:t_len, :]
        vc = v_ref[s_len:t_len, :]

        def group(g, carry):
            rs = [g * NA_GROUP + i for i in range(NA_GROUP)]
            r0s = [jnp.clip(r - NA_WIN_ROWS // 2, 0, rows - NA_WIN_ROWS) for r in rs]
            gs_ = pl.multiple_of(g * (NA_GROUP * GRID_W), NA_GROUP * GRID_W)
            kss = [pl.multiple_of(r0 * GRID_W, GRID_W) for r0 in r0s]
            q_all = q_ref[pl.ds(gs_, NA_GROUP * GRID_W), :]
            s_ctx = _dot(q_all, kc, 1, 1) * scale
            s_loc = [_dot(q_all[i * GRID_W:(i + 1) * GRID_W], k_ref[pl.ds(kss[i], nloc), :], 1, 1) * scale + b_ref[rs[i] - r0s[i]]
                     for i in range(NA_GROUP)]
            p_loc, p_ctx, inv = [], [], []
            for i in range(NA_GROUP):
                sc = s_ctx[i * GRID_W:(i + 1) * GRID_W]
                m = jnp.maximum(jnp.max(s_loc[i], axis=-1, keepdims=True), jnp.max(sc, axis=-1, keepdims=True))
                pl_, pc_ = jnp.exp(s_loc[i] - m), jnp.exp(sc - m)
                inv.append(1.0 / (jnp.sum(pl_, axis=-1, keepdims=True) + jnp.sum(pc_, axis=-1, keepdims=True)))
                p_loc.append(pl_.astype(BF16))
                p_ctx.append(pc_.astype(BF16))
            o_ctx = _dot(jnp.concatenate(p_ctx, axis=0), vc, 1, 0)
            o_loc = [_dot(p_loc[i], v_ref[pl.ds(kss[i], nloc), :], 1, 0) for i in range(NA_GROUP)]
            out = jnp.concatenate([(o_loc[i] + o_ctx[i * GRID_W:(i + 1) * GRID_W]) * inv[i] for i in range(NA_GROUP)], axis=0)
            o_ref[pl.ds(gs_, NA_GROUP * GRID_W), :] = out.astype(o_ref.dtype)
            return carry

        lax.fori_loop(0, rows // NA_GROUP, group, 0)
        qc = q_ref[s_len:t_len, :]
        s = _dot(qc, kc, 1, 1) * scale
        p = jnp.exp(s - jnp.max(s, axis=-1, keepdims=True))
        o = _dot(p.astype(BF16), vc, 1, 0) / jnp.sum(p, axis=-1, keepdims=True)
        o_ref[s_len:t_len, :] = o.astype(o_ref.dtype)

    col = lambda off: pl.BlockSpec((t_len, hd), functools.partial(lambda h, off: (0, off + h), off=off))
    return pl.pallas_call(
        body, name=name, grid=(heads,),
        in_specs=[col(0), col(heads), col(2 * heads),
                  pl.BlockSpec((None, NA_WIN_ROWS, GRID_W, nloc), lambda h: (h, 0, 0, 0))],
        out_specs=pl.BlockSpec((t_len, hd), lambda h: (0, h)),
        out_shape=jax.ShapeDtypeStruct((t_len, heads * hd), BF16),
        compiler_params=_params(("parallel",)),
    )(u, u, u, bias)


def _na_bwd(u, bias, o, do, *, s_len, heads, name):
    t_len = u.shape[0]
    rows = s_len // GRID_W
    nloc = NA_WIN_ROWS * GRID_W
    scale = NA_HEAD_DIM ** -0.5
    hd = NA_HEAD_DIM

    def body(q_ref, k_ref, v_ref, b_ref, o_ref, do_ref, dq_ref, dk_ref, dv_ref, db_ref, dk_acc, dv_acc):
        kc = k_ref[s_len:t_len, :]
        vc = v_ref[s_len:t_len, :]
        dk_acc[...] = jnp.zeros_like(dk_acc)
        dv_acc[...] = jnp.zeros_like(dv_acc)
        db_ref[...] = jnp.zeros_like(db_ref)

        def group(g, carry):
            n_g, rw = NA_GROUP, GRID_W
            rs = [g * n_g + i for i in range(n_g)]
            r0s = [jnp.clip(r - NA_WIN_ROWS // 2, 0, rows - NA_WIN_ROWS) for r in rs]
            dls = [r - r0 for r, r0 in zip(rs, r0s)]
            gs_ = pl.ds(pl.multiple_of(g * (n_g * rw), n_g * rw), n_g * rw)
            kss = [pl.ds(pl.multiple_of(r0 * rw, rw), nloc) for r0 in r0s]
            row_of = lambda a, i: a[i * rw:(i + 1) * rw]
            q_all, do_all = q_ref[gs_, :], do_ref[gs_, :]
            dlt_all = jnp.sum(do_all.astype(F32) * o_ref[gs_, :].astype(F32), axis=-1, keepdims=True)
            s_ctx = _dot(q_all, kc, 1, 1) * scale
            dp_ctx = _dot(do_all, vc, 1, 1)
            s_loc = [_dot(row_of(q_all, i), k_ref[kss[i], :], 1, 1) * scale + b_ref[dls[i]] for i in range(n_g)]
            dp_loc = [_dot(row_of(do_all, i), v_ref[kss[i], :], 1, 1) for i in range(n_g)]
            p_loc_b, ds_loc_b, p_ctx_b, ds_ctx_b = [], [], [], []
            for i in range(n_g):
                sc, dlt = row_of(s_ctx, i), row_of(dlt_all, i)
                m = jnp.maximum(jnp.max(s_loc[i], axis=-1, keepdims=True), jnp.max(sc, axis=-1, keepdims=True))
                pl_, pc_ = jnp.exp(s_loc[i] - m), jnp.exp(sc - m)
                inv = 1.0 / (jnp.sum(pl_, axis=-1, keepdims=True) + jnp.sum(pc_, axis=-1, keepdims=True))
                pl_, pc_ = pl_ * inv, pc_ * inv
                ds_l = pl_ * (dp_loc[i] - dlt)
                db_ref[dls[i]] += ds_l
                p_loc_b.append(pl_.astype(BF16))
                ds_loc_b.append(ds_l.astype(BF16))
                p_ctx_b.append(pc_.astype(BF16))
                ds_ctx_b.append((pc_ * (row_of(dp_ctx, i) - dlt)).astype(BF16))
            p_ctx_all, ds_ctx_all = jnp.concatenate(p_ctx_b, axis=0), jnp.concatenate(ds_ctx_b, axis=0)
            dq_ctx = _dot(ds_ctx_all, kc, 1, 0)
            dq_loc = [_dot(ds_loc_b[i], k_ref[kss[i], :], 1, 0) for i in range(n_g)]
            dk_loc = [_dot(ds_loc_b[i], row_of(q_all, i), 0, 0) for i in range(n_g)]
            dv_loc = [_dot(p_loc_b[i], row_of(do_all, i), 0, 0) for i in range(n_g)]
            dk_ctx = _dot(ds_ctx_all, q_all, 0, 0)
            dv_ctx = _dot(p_ctx_all, do_all, 0, 0)
            dq_ref[gs_, :] = ((jnp.concatenate(dq_loc, axis=0) + dq_ctx) * scale).astype(dq_ref.dtype)
            for i in range(n_g):
                dk_acc[kss[i], :] += dk_loc[i] * scale
                dv_acc[kss[i], :] += dv_loc[i]
            dk_acc[s_len:t_len, :] += dk_ctx * scale
            dv_acc[s_len:t_len, :] += dv_ctx
            return carry

        lax.fori_loop(0, rows // NA_GROUP, group, 0)
        qc = q_ref[s_len:t_len, :]
        dout = do_ref[s_len:t_len, :]
        out = o_ref[s_len:t_len, :]
        s = _dot(qc, kc, 1, 1) * scale
        p = jnp.exp(s - jnp.max(s, axis=-1, keepdims=True))
        p = p / jnp.sum(p, axis=-1, keepdims=True)
        dlt = jnp.sum(dout.astype(F32) * out.astype(F32), axis=-1, keepdims=True)
        ds = (p * (_dot(dout, vc, 1, 1) - dlt)).astype(BF16)
        dq_ref[s_len:t_len, :] = (_dot(ds, kc, 1, 0) * scale).astype(dq_ref.dtype)
        dk_acc[s_len:t_len, :] += _dot(ds, qc, 0, 0) * scale
        dv_acc[s_len:t_len, :] += _dot(p.astype(BF16), dout, 0, 0)
        dk_ref[...] = dk_acc[...].astype(dk_ref.dtype)
        dv_ref[...] = dv_acc[...].astype(dv_ref.dtype)

    col = lambda off: pl.BlockSpec((t_len, hd), functools.partial(lambda h, off: (0, off + h), off=off))
    tbl = pl.BlockSpec((None, NA_WIN_ROWS, GRID_W, nloc), lambda h: (h, 0, 0, 0))
    tok = jax.ShapeDtypeStruct((t_len, heads * hd), BF16)
    return pl.pallas_call(
        body, name=name, grid=(heads,),
        in_specs=[col(0), col(heads), col(2 * heads), tbl, col(0), col(0)],
        out_specs=[col(0), col(0), col(0), tbl],
        out_shape=[tok, tok, tok, jax.ShapeDtypeStruct(bias.shape, F32)],
        scratch_shapes=[pltpu.VMEM((t_len, hd), F32), pltpu.VMEM((t_len, hd), F32)],
        compiler_params=_params(("parallel",)),
    )(u, u, u, bias, o, do)


def _split3(x):
    hi = x.astype(BF16)
    r1 = x - hi.astype(F32)
    mid = r1.astype(BF16)
    lo = (r1 - mid.astype(F32)).astype(BF16)
    return hi, mid, lo


def _rpb_grad(dbias, *, name):
    heads = dbias.shape[0]
    kh = NA_WIN_ROWS
    e1, e2 = _na_onehots()
    x = dbias.reshape(heads, kh, GRID_W, kh, GRID_W).transpose(0, 1, 3, 2, 4).reshape(heads, kh * kh, GRID_W * GRID_W)

    def body(x_ref, e1_ref, e2_ref, o_ref):
        e2b = e2_ref[...].astype(BF16)
        y = sum(_dot(part, e2b, 1, 0) for part in _split3(x_ref[...]))
        e1b = e1_ref[...].astype(BF16)
        o_ref[...] = sum(_dot(e1b, part, 1, 0) for part in _split3(y))

    out = pl.pallas_call(
        body, name=name, grid=(heads,),
        in_specs=[pl.BlockSpec((None, kh * kh, GRID_W * GRID_W), lambda h: (h, 0, 0)),
                  pl.BlockSpec(e1.shape, lambda h: (0, 0)), pl.BlockSpec(e2.shape, lambda h: (0, 0))],
        out_specs=pl.BlockSpec((None, 16, LANES), lambda h: (h, 0, 0)),
        out_shape=jax.ShapeDtypeStruct((heads, 16, LANES), F32),
        compiler_params=_params(("parallel",)),
    )(x, e1, e2)
    return out[:, :2 * kh - 1, :2 * NA_WIN_COLS - 1]


def _rope_tables(s_len, l_len):
    nf = RET_KEY_DIM // 4
    t = np.arange(s_len)
    row = (t // GRID_W).astype(np.float32)
    colp = (t % GRID_W).astype(np.float32)
    inv_freq = jnp.asarray(ROPE_BASE, F32) ** (-jnp.arange(nf, dtype=F32) / nf)
    ang = jnp.concatenate([jnp.asarray(row)[:, None] * inv_freq, jnp.asarray(colp)[:, None] * inv_freq], axis=-1)
    cos, sin = jnp.cos(ang), jnp.sin(ang)
    c2 = jnp.concatenate([cos, cos], axis=-1)
    s2 = jnp.concatenate([-sin, sin], axis=-1)
    c2 = jnp.concatenate([c2, jnp.ones((l_len, RET_KEY_DIM), F32)], axis=0)
    s2 = jnp.concatenate([s2, jnp.zeros((l_len, RET_KEY_DIM), F32)], axis=0)
    return c2, s2


def _rope(x, c2, s2):
    return x * c2 + pltpu.roll(x, RET_KEY_DIM // 2, 1) * s2


def _rope_t(d, c2, s2):
    return d * c2 + pltpu.roll(d * s2, RET_KEY_DIM // 2, 1)


def _ret_decays(lg, direction):
    cs = RET_CHUNK
    i_col = lax.broadcasted_iota(jnp.int32, (cs, 1), 0)
    p_col = jnp.where(direction == 0, i_col, cs - 1 - i_col).astype(F32)
    pi = lax.broadcasted_iota(jnp.int32, (cs, cs), 0)
    pj = lax.broadcasted_iota(jnp.int32, (cs, cs), 1)
    diff = jnp.where(direction == 0, pi - pj, pj - pi).astype(F32)
    dm = jnp.where(diff >= 0, jnp.exp(jnp.maximum(diff, 0.0) * lg), 0.0)
    qdec = jnp.exp((p_col + 1.0) * lg)
    kdec = jnp.exp((cs - 1.0 - p_col) * lg)
    cd = jnp.exp(jnp.full((1, 1), cs, F32) * lg)
    return p_col, dm, qdec, kdec, cd


def _ret_chunk_index(t, direction, n_chunks, lat_chunks):
    return jnp.where(direction == 0, lax.rem(t + lat_chunks, n_chunks), n_chunks - 1 - t)


def _ret_fwd(u, c2, s2, lg, *, s_len, heads, q_off, name):
    t_len = u.shape[0]
    cs, dk, dv = RET_CHUNK, RET_KEY_DIM, RET_VAL_DIM
    n_chunks, lat_chunks = t_len // cs, s_len // cs
    k_scale = dk ** -0.5
    qb, kb, vb = q_off // dk, q_off // dk + heads, (q_off + 2 * heads * dk) // dv

    def body(lg_ref, q_ref, k_ref, v_ref, c_ref, s_ref, o_ref, st_ref, qd_s, kv_s):
        h, d = pl.program_id(0), pl.program_id(1)
        _, dm, qdec, kdec, cd = _ret_decays(lg_ref[d, h], d)
        n_g = max(g for g in RET_GROUPS if n_chunks % g == 0)
        rows_of = lambda c: pl.ds(pl.multiple_of(c * cs, cs), cs)

        def local(gi, carry):
            rws = [rows_of(gi * n_g + j) for j in range(n_g)]
            qcs = [_rope(q_ref[r, :].astype(F32), c_ref[r, :], s_ref[r, :]) for r in rws]
            kcs = [_rope(k_ref[r, :].astype(F32), c_ref[r, :], s_ref[r, :]) * k_scale for r in rws]
            vcs = [v_ref[r, :] for r in rws]
            a_raw = [_dot(qcs[j].astype(BF16), kcs[j].astype(BF16), 1, 1) for j in range(n_g)]
            kv = [_dot((kcs[j] * kdec).astype(BF16), vcs[j], 0, 0) for j in range(n_g)]
            inner = [_dot((a_raw[j] * dm).astype(BF16), vcs[j], 1, 0) for j in range(n_g)]
            for j in range(n_g):
                qd_s[rws[j], :] = (qcs[j] * qdec).astype(BF16)
                kv_s[gi * n_g + j] = kv[j]

            @pl.when(d == 0)
            def _():
                for j in range(n_g):
                    o_ref[rws[j], :] = inner[j]

            @pl.when(d == 1)
            def _():
                for j in range(n_g):
                    o_ref[rws[j], :] += inner[j]

            return carry

        lax.fori_loop(0, n_chunks // n_g, local, 0)

        def scan(t, st):
            st_ref[t] = st
            return st * cd + kv_s[_ret_chunk_index(t, d, n_chunks, lat_chunks)]

        lax.fori_loop(0, n_chunks, scan, jnp.zeros((dk, dv), F32))

        def cross(gi, carry):
            ts = [gi * n_g + j for j in range(n_g)]
            rws = [rows_of(_ret_chunk_index(t, d, n_chunks, lat_chunks)) for t in ts]
            outs = [_dot(qd_s[rws[j], :], st_ref[ts[j]].astype(BF16), 1, 0) for j in range(n_g)]
            for j in range(n_g):
                o_ref[rws[j], :] += outs[j]
            return carry

        lax.fori_loop(0, n_chunks // n_g, cross, 0)

    return pl.pallas_call(
        body, name=name, grid=(heads, 2),
        in_specs=[pl.BlockSpec(memory_space=pltpu.SMEM),
                  pl.BlockSpec((t_len, dk), lambda h, d: (0, qb + h)),
                  pl.BlockSpec((t_len, dk), lambda h, d: (0, kb + h)),
                  pl.BlockSpec((t_len, dv), lambda h, d: (0, vb + h)),
                  pl.BlockSpec((t_len, dk), lambda h, d: (0, 0)),
                  pl.BlockSpec((t_len, dk), lambda h, d: (0, 0))],
        out_specs=[pl.BlockSpec((t_len, dv), lambda h, d: (0, h)),
                   pl.BlockSpec((None, None, n_chunks, dk, dv), lambda h, d: (h, d, 0, 0, 0))],
        out_shape=[jax.ShapeDtypeStruct((t_len, heads * dv), F32),
                   jax.ShapeDtypeStruct((heads, 2, n_chunks, dk, dv), F32)],
        scratch_shapes=[pltpu.VMEM((t_len, dk), BF16), pltpu.VMEM((n_chunks, dk, dv), F32)],
        compiler_params=_params(("parallel", "arbitrary")),
    )(lg, u, u, u, c2, s2)


def _ret_bwd(u, c2, s2, lg, states, do, *, s_len, heads, q_off, name):
    t_len = u.shape[0]
    cs, dk, dv = RET_CHUNK, RET_KEY_DIM, RET_VAL_DIM
    n_chunks, lat_chunks = t_len // cs, s_len // cs
    k_scale = dk ** -0.5
    qb, kb, vb = q_off // dk, q_off // dk + heads, (q_off + 2 * heads * dk) // dv

    def body(lg_ref, q_ref, k_ref, v_ref, c_ref, s_ref, st_ref, do_ref, dq_ref, dk_ref, dv_ref, dlg_ref, acc, qdo_s, dst_s):
        h, d = pl.program_id(0), pl.program_id(1)
        p_col, dm, qdec, kdec, cd = _ret_decays(lg_ref[d, h], d)
        acc[...] = jnp.zeros_like(acc)
        n_g = max(g for g in RET_GROUPS[:2] if n_chunks % g == 0)
        rows_of = lambda c: pl.ds(pl.multiple_of(c * cs, cs), cs)
        chunk_of = lambda t: _ret_chunk_index(t, d, n_chunks, lat_chunks)

        def local(gi, carry):
            rws = [rows_of(gi * n_g + j) for j in range(n_g)]
            qds = [(_rope(q_ref[r, :].astype(F32), c_ref[r, :], s_ref[r, :]) * qdec).astype(BF16) for r in rws]
            prods = [_dot(qds[j], do_ref[rws[j], :].astype(BF16), 0, 0) for j in range(n_g)]
            for j in range(n_g):
                qdo_s[gi * n_g + j] = prods[j]
            return carry

        lax.fori_loop(0, n_chunks // n_g, local, 0)

        def scan(i, dst):
            t = n_chunks - 1 - i
            dst_s[t] = dst
            return dst * cd + qdo_s[chunk_of(t)]

        lax.fori_loop(0, n_chunks, scan, jnp.zeros((dk, dv), F32))

        def grads(gi, carry):
            ts = [gi * n_g + j for j in range(n_g)]
            rws = [rows_of(chunk_of(t)) for t in ts]
            ccs, sss = [c_ref[r, :] for r in rws], [s_ref[r, :] for r in rws]
            qcs = [_rope(q_ref[r, :].astype(F32), cc, ss) for r, cc, ss in zip(rws, ccs, sss)]
            kcs = [_rope(k_ref[r, :].astype(F32), cc, ss) * k_scale for r, cc, ss in zip(rws, ccs, sss)]
            vcs = [v_ref[r, :] for r in rws]
            docs = [do_ref[r, :].astype(BF16) for r in rws]
            sts = [st_ref[t] for t in ts]
            dsts = [dst_s[t] for t in ts]
            q16 = [x.astype(BF16) for x in qcs]
            k16 = [x.astype(BF16) for x in kcs]
            dst16 = [x.astype(BF16) for x in dsts]
            rng = range(n_g)
            a_raw = [_dot(q16[j], k16[j], 1, 1) for j in rng]
            da_raw = [_dot(docs[j], vcs[j], 1, 1) for j in rng]
            dq_c = [_dot(docs[j], sts[j].astype(BF16), 1, 1) * qdec for j in rng]
            dv_s = [_dot((kcs[j] * kdec).astype(BF16), dst16[j], 1, 0) for j in rng]
            dk_s = [_dot(vcs[j], dst16[j], 1, 1) * kdec for j in rng]
            a16 = [(a_raw[j] * dm).astype(BF16) for j in rng]
            dam = [(da_raw[j] * dm).astype(BF16) for j in rng]
            dq_i = [_dot(dam[j], k16[j], 1, 0) for j in rng]
            dk_i = [_dot(dam[j], q16[j], 0, 0) for j in rng]
            dv_i = [_dot(a16[j], docs[j], 0, 0) for j in rng]
            for j in rng:
                g = (jnp.sum(qcs[j] * (p_col * dq_i[j] + (p_col + 1.0) * dq_c[j]), axis=-1, keepdims=True)
                     + jnp.sum(kcs[j] * ((cs - 1.0 - p_col) * dk_s[j] - p_col * dk_i[j]), axis=-1, keepdims=True))
                g = (jnp.sum(g, axis=0, keepdims=True)
                     + cs * cd * jnp.sum(jnp.sum(dsts[j] * sts[j], axis=-1, keepdims=True), axis=0, keepdims=True))
                acc[...] += jnp.broadcast_to(g, acc.shape)
            dqs = [_rope_t(dq_i[j] + dq_c[j], ccs[j], sss[j]) for j in rng]
            dks = [_rope_t((dk_i[j] + dk_s[j]) * k_scale, ccs[j], sss[j]) for j in rng]
            dvs = [dv_i[j] + dv_s[j] for j in rng]

            @pl.when(d == 0)
            def _():
                for j in rng:
                    dq_ref[rws[j], :] = dqs[j].astype(dq_ref.dtype)
                    dk_ref[rws[j], :] = dks[j].astype(dk_ref.dtype)
                    dv_ref[rws[j], :] = dvs[j].astype(dv_ref.dtype)

            @pl.when(d == 1)
            def _():
                for j in rng:
                    dq_ref[rws[j], :] = (dq_ref[rws[j], :].astype(F32) + dqs[j]).astype(dq_ref.dtype)
                    dk_ref[rws[j], :] = (dk_ref[rws[j], :].astype(F32) + dks[j]).astype(dk_ref.dtype)
                    dv_ref[rws[j], :] = (dv_ref[rws[j], :].astype(F32) + dvs[j]).astype(dv_ref.dtype)

            return carry

        lax.fori_loop(0, n_chunks // n_g, grads, 0)
        dlg_ref[...] = acc[...]

    return pl.pallas_call(
        body, name=name, grid=(heads, 2),
        in_specs=[pl.BlockSpec(memory_space=pltpu.SMEM),
                  pl.BlockSpec((t_len, dk), lambda h, d: (0, qb + h)),
                  pl.BlockSpec((t_len, dk), lambda h, d: (0, kb + h)),
                  pl.BlockSpec((t_len, dv), lambda h, d: (0, vb + h)),
                  pl.BlockSpec((t_len, dk), lambda h, d: (0, 0)),
                  pl.BlockSpec((t_len, dk), lambda h, d: (0, 0)),
                  pl.BlockSpec((None, None, n_chunks, dk, dv), lambda h, d: (h, d, 0, 0, 0)),
                  pl.BlockSpec((t_len, dv), lambda h, d: (0, h))],
        out_specs=[pl.BlockSpec((t_len, dk), lambda h, d: (0, h)),
                   pl.BlockSpec((t_len, dk), lambda h, d: (0, h)),
                   pl.BlockSpec((t_len, dv), lambda h, d: (0, h)),
                   pl.BlockSpec((None, None, 8, LANES), lambda h, d: (h, d, 0, 0))],
        out_shape=[jax.ShapeDtypeStruct((t_len, heads * dk), BF16),
                   jax.ShapeDtypeStruct((t_len, heads * dk), BF16),
                   jax.ShapeDtypeStruct((t_len, heads * dv), BF16),
                   jax.ShapeDtypeStruct((heads, 2, 8, LANES), F32)],
        scratch_shapes=[pltpu.VMEM((8, LANES), F32), pltpu.VMEM((n_chunks, dk, dv), F32), pltpu.VMEM((n_chunks, dk, dv), F32)],
        compiler_params=_params(("parallel", "arbitrary")),
    )(lg, u, u, u, c2, s2, states, do)


def _mesh_pos():
    return lax.axis_index("x"), lax.axis_index("y"), lax.axis_index("c")


def _all_gather_small(buf, *, name):
    r = buf.shape[0]

    def body(x_ref, o_ref, send_sems, recv_sems, local_sem):
        x, y, c = _mesh_pos()
        me = 4 * x + 2 * y + c
        mine = pltpu.make_async_copy(x_ref, o_ref.at[me], local_sem)
        mine.start()
        copies = []
        for k in range(1, N_DEV):
            px, py, pc = x ^ ((k >> 2) & 1), y ^ ((k >> 1) & 1), c ^ (k & 1)
            cp = pltpu.make_async_remote_copy(
                src_ref=x_ref, dst_ref=o_ref.at[me], send_sem=send_sems.at[k - 1], recv_sem=recv_sems.at[k - 1],
                device_id=(px, py, pc), device_id_type=MESH)
            cp.start()
            copies.append((cp, 4 * px + 2 * py + pc))
        for k, (cp, peer) in enumerate(copies):
            pltpu.make_async_remote_copy(
                src_ref=x_ref, dst_ref=o_ref.at[peer], send_sem=send_sems.at[k], recv_sem=recv_sems.at[k],
                device_id=(x, y, c), device_id_type=MESH).wait_recv()
        for cp, _ in copies:
            cp.wait_send()
        mine.wait()

    return pl.pallas_call(
        body, name=name,
        in_specs=[pl.BlockSpec(memory_space=pltpu.VMEM)],
        out_specs=pl.BlockSpec(memory_space=pltpu.VMEM),
        out_shape=jax.ShapeDtypeStruct((N_DEV, r, LANES), F32),
        scratch_shapes=[pltpu.SemaphoreType.DMA((N_DEV - 1,)), pltpu.SemaphoreType.DMA((N_DEV - 1,)),
                        pltpu.SemaphoreType.DMA],
        compiler_params=pltpu.CompilerParams(vmem_limit_bytes=VMEM_LIMIT),
    )(buf)


def _cut(ref, shard_axis, *, chip=None, half=None, lead=None):
    shape = ref.shape[1:] if lead is not None else ref.shape
    idx = [slice(None), slice(None)]
    if chip is not None:
        w = shape[shard_axis] // N_CHIPS
        idx[shard_axis] = pl.ds(pl.multiple_of(chip * w, w), w)
    if half is not None:
        hw = shape[1 - shard_axis] // 2
        idx[1 - shard_axis] = pl.ds(pl.multiple_of(half * hw, hw), hw)
    if lead is not None:
        idx = [lead] + idx
    return ref.at[tuple(idx)]


def _wait_recv(ref, send_sem, recv_sem):
    pltpu.make_async_remote_copy(src_ref=ref, dst_ref=ref, send_sem=send_sem, recv_sem=recv_sem,
                                 device_id=_mesh_pos(), device_id_type=MESH).wait_recv()


def _gather_plan(axes):
    def plan(srcs, lands, send_sems, recv_sems):
        x, y, c = _mesh_pos()
        chip = 2 * x + y
        copies = []
        for i, ax in enumerate(axes):
            for k in range(1, N_CHIPS):
                px, py = x ^ (k >> 1), y ^ (k & 1)
                mine = _cut(lands[i], ax, chip=chip, half=c)
                j = i * (N_CHIPS - 1) + k - 1
                sems = dict(send_sem=send_sems.at[j], recv_sem=recv_sems.at[j], device_id=(px, py, c), device_id_type=MESH)
                send = pltpu.make_async_remote_copy(src_ref=mine, dst_ref=mine, **sems)
                recv = pltpu.make_async_remote_copy(src_ref=mine, dst_ref=_cut(lands[i], ax, chip=2 * px + py, half=c), **sems)
                copies.append((send, recv))
        return copies
    return plan


def _gather_near_plan(axes):
    def plan(srcs, lands, send_sems, recv_sems):
        x, y, c = _mesh_pos()
        copies = []
        for i, ax in enumerate(axes):
            mine = _cut(lands[i], ax, chip=2 * x + y, half=c)
            for k, (px, py) in enumerate(((1 - x, y), (x, 1 - y))):
                sems = dict(send_sem=send_sems.at[2 * i + k], recv_sem=recv_sems.at[2 * i + k], device_id=(px, py, c), device_id_type=MESH)
                send = pltpu.make_async_remote_copy(src_ref=mine, dst_ref=mine, **sems)
                recv = pltpu.make_async_remote_copy(src_ref=mine, dst_ref=_cut(lands[i], ax, chip=2 * px + py, half=c), **sems)
                copies.append((send, recv))
        return copies
    return plan


def _gather_far_plan(axes):
    def plan(srcs, lands, send_sems, recv_sems):
        x, y, c = _mesh_pos()
        from_chip = 2 * (x ^ (1 - c)) + (y ^ c)
        to = (x ^ c, y ^ (1 - c), c)
        diag = 2 * (1 - x) + (1 - y)
        copies = []
        for i, ax in enumerate(axes):
            passed = _cut(lands[i], ax, chip=from_chip, half=c)
            sems = dict(send_sem=send_sems.at[i], recv_sem=recv_sems.at[i], device_id=to, device_id_type=MESH)
            send = pltpu.make_async_remote_copy(src_ref=passed, dst_ref=passed, **sems)
            recv = pltpu.make_async_remote_copy(src_ref=passed, dst_ref=_cut(lands[i], ax, chip=diag, half=c), **sems)
            copies.append((send, recv))
        return copies
    return plan


def _pair_plan(axes):
    def plan(srcs, lands, send_sems, recv_sems):
        x, y, c = _mesh_pos()
        copies = []
        for i, ax in enumerate(axes):
            cp = pltpu.make_async_remote_copy(
                src_ref=_cut(srcs[i], ax, half=1 - c), dst_ref=lands[i], send_sem=send_sems.at[i], recv_sem=recv_sems.at[i],
                device_id=(x, y, 1 - c), device_id_type=MESH)
            copies.append((cp, cp))
        return copies
    return plan


def _scatter_plan(axes):
    def plan(srcs, lands, send_sems, recv_sems):
        x, y, c = _mesh_pos()
        copies = []
        for i, ax in enumerate(axes):
            for k in range(1, N_CHIPS):
                px, py = x ^ (k >> 1), y ^ (k & 1)
                j = i * (N_CHIPS - 1) + k - 1
                cp = pltpu.make_async_remote_copy(
                    src_ref=_cut(srcs[i], ax, chip=2 * px + py), dst_ref=lands[i].at[k - 1],
                    send_sem=send_sems.at[j], recv_sem=recv_sems.at[j], device_id=(px, py, c), device_id_type=MESH)
                copies.append((cp, cp))
        return copies
    return plan


HBM = pl.BlockSpec(memory_space=pltpu.HBM)
SEM = pl.BlockSpec(memory_space=pltpu.SEMAPHORE)
EFFECT = pltpu.SideEffectType.DATAFLOW_SIDE_EFFECTING


def _in_hbm(arrays):
    return [pltpu.with_memory_space_constraint(a, pltpu.HBM) for a in arrays]


def _split_start(srcs, lands, plan, n_copies, *, name):
    bufs = list(srcs) + list(lands)
    ns, nb = len(srcs), len(bufs)

    def body(*refs):
        send_sems, recv_sems, token = refs[nb], refs[nb + 1], refs[-1]
        for send, _ in plan(refs[:ns], refs[ns:nb], send_sems, recv_sems):
            send.start()
        token[...] = jnp.zeros_like(token)

    sems = pltpu.SemaphoreType.DMA((n_copies,))
    res = pl.pallas_call(
        body, name=name, in_specs=[HBM] * nb,
        out_specs=[SEM, SEM] + [HBM] * nb + [pl.BlockSpec(memory_space=pltpu.VMEM)],
        out_shape=[sems, sems] + [pltpu.HBM(a.shape, a.dtype) for a in bufs] + [jax.ShapeDtypeStruct((8, LANES), F32)],
        input_output_aliases={j: 2 + j for j in range(nb)},
        compiler_params=pltpu.CompilerParams(has_side_effects=EFFECT),
    )(*_in_hbm(bufs))
    return res[0], res[1], res[2:2 + ns], res[2 + ns:2 + nb], res[-1]


def _split_wait(started, after, plan, *, name, with_srcs=False):
    send_sems, recv_sems, srcs, lands, _ = started
    bufs = list(srcs) + list(lands)
    ns, nb = len(srcs), len(bufs)

    def body(*refs):
        for send, recv in plan(refs[:ns], refs[ns:nb], refs[nb], refs[nb + 1]):
            send.wait_send()
            recv.wait_recv()

    res = pl.pallas_call(
        body, name=name, in_specs=[HBM] * nb + [SEM, SEM, ANY], out_specs=[HBM] * nb,
        out_shape=[pltpu.HBM(a.shape, a.dtype) for a in bufs],
        input_output_aliases={j: j for j in range(nb)},
        compiler_params=pltpu.CompilerParams(has_side_effects=EFFECT),
    )(*bufs, send_sems, recv_sems, after)
    return (res[:ns], res[ns:]) if with_srcs else res[ns:]


def _cast_into_full(w3, layer, ax, chip, *, after=None, name):
    _, r, wd = w3.shape
    tr = _rows_per_tile(r, wd, 4 << 20)
    nt = r // tr
    full_shape = (r, wd * N_CHIPS) if ax == 1 else (r * N_CHIPS, wd)
    out_map = (lambda i, ch: (i, ch[0])) if ax == 1 else (lambda i, ch: (ch[0] * nt + i, 0))
    zero = jnp.zeros((1, wd), F32) + (0.0 if after is None else after)

    def body(chip_ref, w_ref, z_ref, o_ref):
        o_ref[...] = (w_ref[...] + z_ref[...]).astype(o_ref.dtype)

    return pl.pallas_call(
        body, name=name,
        grid_spec=pltpu.PrefetchScalarGridSpec(
            num_scalar_prefetch=1, grid=(nt,),
            in_specs=[pl.BlockSpec((None, tr, wd), lambda i, ch: (layer, i, 0)), pl.BlockSpec((1, wd), lambda i, ch: (0, 0))],
            out_specs=pl.BlockSpec((tr, wd), out_map)),
        out_shape=jax.ShapeDtypeStruct(full_shape, BF16),
        compiler_params=_params(("parallel",)),
    )(jnp.reshape(chip, (1,)).astype(jnp.int32), w3, zero)


def _forward_halves(fulls, axes, *, name):
    n = len(fulls)

    def body(*refs):
        bufs = refs[:n]
        send_sems, recv_sems = refs[2 * n:]
        x, y, c = _mesh_pos()
        sends = []
        for i in range(n):
            for k in range(1, N_CHIPS):
                landed = _cut(bufs[i], axes[i], chip=2 * (x ^ (k >> 1)) + (y ^ (k & 1)), half=c)
                cp = pltpu.make_async_remote_copy(
                    src_ref=landed, dst_ref=landed, send_sem=send_sems.at[i, k - 1], recv_sem=recv_sems.at[i, k - 1],
                    device_id=(x, y, 1 - c), device_id_type=MESH)
                cp.start()
                sends.append(cp)
        for i in range(n):
            for k in range(1, N_CHIPS):
                other = _cut(bufs[i], axes[i], chip=2 * (x ^ (k >> 1)) + (y ^ (k & 1)), half=1 - c)
                _wait_recv(other, send_sems.at[i, k - 1], recv_sems.at[i, k - 1])
        for cp in sends:
            cp.wait_send()

    pairs = pltpu.SemaphoreType.DMA((n, N_CHIPS - 1))
    return pl.pallas_call(
        body, name=name, in_specs=[ANY] * n, out_specs=[ANY] * n,
        out_shape=[jax.ShapeDtypeStruct(a.shape, a.dtype) for a in fulls],
        input_output_aliases={j: j for j in range(n)},
        scratch_shapes=[pairs, pairs],
    )(*fulls)


def _share_halves_in_place(bufs, axes, *, name):
    n = len(bufs)

    def body(*refs):
        ins = refs[:n]
        send_sems, recv_sems = refs[2 * n:]
        x, y, c = _mesh_pos()
        sends = []
        for i in range(n):
            mine = _cut(ins[i], axes[i], half=c)
            cp = pltpu.make_async_remote_copy(
                src_ref=mine, dst_ref=mine, send_sem=send_sems.at[i], recv_sem=recv_sems.at[i],
                device_id=(x, y, 1 - c), device_id_type=MESH)
            cp.start()
            sends.append(cp)
        for i in range(n):
            _wait_recv(_cut(ins[i], axes[i], half=1 - c), send_sems.at[i], recv_sems.at[i])
        for cp in sends:
            cp.wait_send()

    sems = pltpu.SemaphoreType.DMA((n,))
    return pl.pallas_call(
        body, name=name, in_specs=[ANY] * n, out_specs=[ANY] * n,
        out_shape=[jax.ShapeDtypeStruct(b.shape, b.dtype) for b in bufs],
        input_output_aliases={j: j for j in range(n)}, scratch_shapes=[sems, sems],
    )(*bufs)


def _adamw_math(w, g, m, v):
    m = ADAM_B1 * m + (1.0 - ADAM_B1) * g
    v = ADAM_B2 * v + (1.0 - ADAM_B2) * (g * g)
    m_hat = m / (1.0 - ADAM_B1 ** ADAM_STEP)
    v_hat = v / (1.0 - ADAM_B2 ** ADAM_STEP)
    delta = -ADAM_LR * (m_hat / (jnp.sqrt(v_hat) + ADAM_EPS) + ADAM_WD * w)
    return delta, m, v


def _adamw_layer(w3, m3, v3, p, q, layer, prev, *, name):
    nl, rows, width = w3.shape
    tr = _rows_per_tile(rows, width)

    def fn(*t):
        if q is None:
            w, m, v, g = t
        else:
            w, m, v, g, g2 = t
            g = g + g2
        delta, m, v = _adamw_math(w, g, m, v)
        return g, delta, m, v

    ins = [('t', w3, 0, width, layer), ('t', m3, 0, width, layer), ('t', v3, 0, width, layer), ('t', p, 0, width)]
    if q is not None:
        ins.append(('t', q, 0, width))
    outs = [('t', width, F32, layer, nl)] * 4
    aliases = None if prev is None else [(prev[i], i) for i in range(4)]
    return _ew(fn, ins, outs, rows=rows, tr=tr, name=name, aliases=aliases)


def _pack_rows(vec):
    n = vec.shape[0]
    r = -(-n // (8 * LANES)) * 8
    return jnp.pad(vec, (0, r * LANES - n)).reshape(r, LANES)


def kernel(x, c, ctx, c_ctx, ada_w, ada_b, norm_g, w_in, na_rpb, ret_decay_logit, w_proj_na, w_proj_ret, w_out, final_g, loss_target, m_c_ctx, m_ada_w, m_ada_b, m_norm_g, m_w_in, m_na_rpb, m_ret_decay_logit, m_w_proj_na, m_w_proj_ret, m_w_out, m_final_g, v_c_ctx, v_ada_w, v_ada_b, v_norm_g, v_w_in, v_na_rpb, v_ret_decay_logit, v_w_proj_na, v_w_proj_ret, v_w_out, v_final_g):
    depth = w_in.shape[0]
    s_len, d_model = x.shape[1], x.shape[2]
    l_len = ctx.shape[1]
    t_len = s_len + l_len
    na_heads = na_rpb.shape[1]
    ret_heads = ret_decay_logit.shape[2]
    w_na = na_heads * NA_HEAD_DIM
    w_qk = ret_heads * RET_KEY_DIM
    w_v = ret_heads * RET_VAL_DIM
    in_cols = w_in.shape[2] * N_CHIPS
    assert in_cols == 4 * w_na + 2 * w_qk + 2 * w_v + 2 * d_model
    assert x.shape[0] == 1 and s_len % (NA_WIN_ROWS * GRID_W) == 0 and l_len % RET_CHUNK == 0
    off = np.cumsum([0, w_na, w_na, w_na, w_na, w_qk, w_qk, w_v, w_v, d_model, d_model])
    o_naz, o_retq, o_retz, o_gna, o_gret = int(off[3]), int(off[4]), int(off[7]), int(off[8]), int(off[9])
    rows = s_len // GRID_W
    tr = _tile(l_len, 256, 8)
    n0 = s_len // tr
    mod_cols = 3 * d_model
    mod_shard = ada_w.shape[2]

    xi, yi, ci = _mesh_pos()
    me = 4 * xi + 2 * yi + ci
    chip = 2 * xi + yi

    big_axes = [1, 1, 0, 0]
    n_big = len(big_axes) * (N_CHIPS - 1)
    gather_plan, scatter_plan = _gather_plan(big_axes), _scatter_plan(big_axes)

    c_silu = c[0] * _sigmoid(c[0])
    cc_silu = c_ctx * _sigmoid(c_ctx)
    c_all = _all_gather_small(_pack_rows(c_silu), name="gather_c")[:, :d_model // LANES].reshape(N_DEV, d_model)
    a_rows = jnp.concatenate([c_all, cc_silu[None], jnp.zeros((16 - N_DEV - 1, d_model), F32)], axis=0)
    mod_part = jnp.stack([_mm(a_rows, ada_w, b_lead=l, out_dtype=F32, name="ada_fwd_%d" % l) for l in range(depth)])
    mod_all = _all_gather_small(_pack_rows(mod_part.reshape(-1)), name="gather_mod")
    n_mod = depth * 16 * mod_shard
    mod_all = mod_all.reshape(N_DEV, -1)[:, :n_mod].reshape(N_CHIPS, 2, depth, 16, mod_shard)[:, 0]
    mod_all = jnp.transpose(mod_all, (1, 2, 0, 3)).reshape(depth, 16, mod_cols) + ada_b[:, None, :]

    big_named = list(zip((w_in, w_proj_na, w_proj_ret, w_out), big_axes, ("w_in", "w_proj_na", "w_proj_ret", "w_out")))
    w_in0 = _cast_into_full(w_in, 0, big_axes[0], chip, name="cast_w_in_0")
    mod_all, w_in0 = lax.optimization_barrier((mod_all, w_in0))
    plan_near, plan_far, plan_rest = _gather_near_plan(big_axes[:1]), _gather_far_plan(big_axes[:1]), _gather_plan(big_axes[1:])
    first_gather = _split_start([], [w_in0], plan_near, 2, name="gather_start_0_in")
    start_token = first_gather[4][0, 0]
    fulls = [[None if (l == 0 and tag == "w_in") else _cast_into_full(w, l, ax, chip, after=start_token, name="cast_%s_%d" % (tag, l))
              for w, ax, tag in big_named] for l in range(depth)]
    mod_lat = lax.dynamic_index_in_dim(mod_all, me, axis=1, keepdims=False)
    mod_ctx = mod_all[:, N_DEV]

    c2, s2 = _rope_tables(s_len, l_len)
    log_gamma = jax.nn.log_sigmoid(ret_decay_logit)
    x_all = jnp.concatenate([x[0], ctx[0]], axis=0)

    def grp(lat_vec, ctx_vec):
        return jnp.stack([lat_vec, ctx_vec])[:, None, :]

    saved, full_w = [], []
    for l in range(depth):
        shift, scale, gate = [grp(mod_lat[l, i * d_model:(i + 1) * d_model], mod_ctx[l, i * d_model:(i + 1) * d_model])
                              for i in range(3)]
        gs = norm_g[l][None, None, :] * (1.0 + scale) + start_token

        def modnorm(xt, gs_t, sh_t):
            r = lax.rsqrt(jnp.mean(xt * xt, axis=-1, keepdims=True) + NORM_EPS)
            return xt * r * gs_t + sh_t

        h, = _ew(modnorm, [('t', x_all, 0, d_model), ('g', gs), ('g', shift)], [('t', d_model, BF16)],
                 rows=t_len, tr=tr, n0=n0, name="modnorm_%d" % l)
        bias = _na_bias_table(na_rpb[l], rows, name="na_bias_%d" % l)
        h, bias = lax.optimization_barrier((h, bias))
        if l == 0:
            landed_near = _split_wait(first_gather, h, plan_near, name="gather_wait_0_in")
            passing = _split_start([], landed_near, plan_far, 1, name="gather_pass_0_in")
            landed_in = _split_wait(passing, passing[4], plan_far, name="gather_wait_0_in_far")
            landed_in, rest0, later = lax.optimization_barrier((landed_in, fulls[0][1:], fulls[1:]))
            rest_gather = _split_start([], rest0, plan_rest, n_big - (N_CHIPS - 1), name="gather_start_0_rest")
            later_gathers = [_split_start([], later[j], gather_plan, n_big, name="gather_start_%d" % (j + 1)) for j in range(depth - 1)]
            win_f, = _forward_halves(landed_in, big_axes[:1], name="gather_forward_0_in")
            win_f, tokens = lax.optimization_barrier((win_f, [rest_gather[4]] + [g[4] for g in later_gathers]))
            gate = gate + sum(t[0, 0] for t in tokens)
        else:
            landed = _split_wait(later_gathers[l - 1], h, gather_plan, name="gather_wait_%d" % l)
            win_f, wpn_f, wpr_f, wout_f = _forward_halves(landed, big_axes, name="gather_forward_%d" % l)
        u = _mm(h, win_f, tm=1152, tn=1024, name="in_proj_%d" % l)
        o_na = _na_fwd(u, bias, s_len=s_len, heads=na_heads, name="na_fwd_%d" % l)
        o_ret, states = _ret_fwd(u, c2, s2, log_gamma[l], s_len=s_len, heads=ret_heads, q_off=o_retq, name="ret_fwd_%d" % l)

        def act(o1, z1, o2, z2):
            a1 = o1.astype(F32) * _silu_parts(z1.astype(F32))[0]
            sz = _silu_parts(z2.astype(F32))[0]
            outs = []
            for hh in range(ret_heads):
                sl = slice(hh * RET_VAL_DIM, (hh + 1) * RET_VAL_DIM)
                oh = o2[:, sl]
                r = lax.rsqrt(jnp.mean(oh * oh, axis=-1, keepdims=True) + NORM_EPS)
                outs.append(oh * r * sz[:, sl])
            return a1, jnp.concatenate(outs, axis=-1)

        a_na, a_ret = _ew(act, [('t', o_na, 0, w_na), ('t', u, o_naz // w_na, w_na), ('t', o_ret, 0, w_v), ('t', u, o_retz // w_v, w_v)],
                          [('t', w_na, BF16), ('t', w_v, BF16)], rows=t_len, tr=tr, name="act_%d" % l)
        if l == 0:
            landed_rest = _split_wait(rest_gather, a_na, plan_rest, name="gather_wait_0_rest")
            wpn_f, wpr_f, wout_f = _forward_halves(landed_rest, big_axes[1:], name="gather_forward_0_rest")
        full_w.append((win_f, wpn_f, wpr_f, wout_f))
        y_na = _mm(a_na, wpn_f, name="proj_na_%d" % l)
        y_ret = _mm(a_ret, wpr_f, name="proj_ret_%d" % l)

        def merge(y1, y2, g1, g2):
            return _sigmoid(g1.astype(F32)) * y1.astype(F32) + _sigmoid(g2.astype(F32)) * y2.astype(F32)

        merged, = _ew(merge, [('t', y_na, 0, d_model), ('t', y_ret, 0, d_model), ('t', u, o_gna // d_model, d_model), ('t', u, o_gret // d_model, d_model)],
                      [('t', d_model, BF16)], rows=t_len, tr=tr, name="merge_%d" % l)
        out = _mm(merged, wout_f, out_dtype=F32, name="out_proj_%d" % l)
        x_new, = _ew(lambda xt, ot, gt: xt + gt * ot, [('t', x_all, 0, d_model), ('t', out, 0, d_model), ('g', gate)],
                     [('t', d_model, F32)], rows=t_len, tr=tr, n0=n0, name="resid_%d" % l)
        saved.append(dict(x=x_all, h=h, u=u, bias=bias, o_na=o_na, o_ret=o_ret, states=states, a_na=a_na, a_ret=a_ret,
                          y_na=y_na, y_ret=y_ret, merged=merged, out=out, gate=gate, gs=gs, scale=scale))
        x_all = x_new

    def final(xt, tt, gt):
        r = lax.rsqrt(jnp.mean(xt * xt, axis=-1, keepdims=True) + NORM_EPS)
        xh = xt * r
        e = xh * gt - tt
        dy = e * (1.0 / d_model)
        dyg = dy * gt
        dx = r * (dyg - xh * jnp.mean(dyg * xh, axis=-1, keepdims=True))
        return dx, _rsum(dy * xh), _rsum(e * e)

    dx_lat, d_final_g, loss_cols = _ew(final, [('t', x_all, 0, d_model), ('t', loss_target[0], 0, d_model), ('g', final_g[None, None, :])],
                                       [('t', d_model, F32), ('r', d_model, 1), ('r', d_model, 1)], rows=s_len, tr=tr, name="final")
    loss_part = (0.5 / d_model) * jnp.sum(loss_cols)
    dx_all = jnp.concatenate([dx_lat, jnp.zeros((l_len, d_model), F32)], axis=0)

    big_w = [(w_in, m_w_in, v_w_in), (w_proj_na, m_w_proj_na, v_w_proj_na), (w_proj_ret, m_w_proj_ret, v_w_proj_ret), (w_out, m_w_out, v_w_out)]
    big_res = [None] * 4
    scatters = {}
    back_token = jnp.zeros((), F32)

    pairs = {}

    def start_pair(key, grads, axes):
        plan = _pair_plan(axes)
        lands = []
        for g, ax in zip(grads, axes):
            shp = list(g.shape)
            shp[1 - ax] //= 2
            lands.append(lax.empty(tuple(shp), BF16))
        pairs[key] = (_split_start(grads, lands, plan, len(axes), name="pair_start_%s" % key), axes, plan)
        return pairs[key][0][4]

    def start_scatter(key, after):
        started, axes, pair_plan = pairs[key]
        grads, theirs = _split_wait(started, after, pair_plan, name="pair_wait_%s" % key, with_srcs=True)
        plan = _scatter_plan(axes)
        pair = [_sum_pair(g, t, ax, ci, name="sum_pair_%s_%d" % (key, i)) for i, (g, t, ax) in enumerate(zip(grads, theirs, axes))]
        own = [lax.dynamic_slice_in_dim(s, chip * (s.shape[ax] // N_CHIPS), s.shape[ax] // N_CHIPS, axis=ax) for s, ax in zip(pair, axes)]
        lands = [lax.empty((N_CHIPS - 1,) + o.shape, BF16) for o in own]
        started = _split_start(pair, lands, plan, len(axes) * (N_CHIPS - 1), name="scatter_start_%s" % key)
        scatters[key] = (started, own, axes, plan)
        return started[4]

    def finish_scatter(key, after):
        started, own, axes, plan = scatters[key]
        recv = _split_wait(started, after, plan, name="scatter_wait_%s" % key)
        bufs = [_sum_chips_into(own[i], rbuf, axes[i], ci, name="sum_chips_%s_%d" % (key, i)) for i, rbuf in enumerate(recv)]
        return _share_halves_in_place(bufs, axes, name="share_halves_%s" % key)

    def adamw_big(l, idx, grads, big_res):
        for i, g in zip(idx, grads):
            w3, m3, v3 = big_w[i]
            big_res[i] = _adamw_layer(w3, m3, v3, g, None, l, big_res[i], name="adamw_big_%d_%d" % (i, l))
        return big_res

    small = dict(dmod_lat=[None] * depth, dmod_ctx=[None] * depth, dnorm_g=[None] * depth, drpb=[None] * depth, ddecay=[None] * depth)
    for l in reversed(range(depth)):
        sv = saved[l]
        win_f, wpn_f, wpr_f, wout_f = full_w[l]

        def resid_bwd(dxt, ot, gt):
            return gt * dxt, _rsum(dxt * ot)

        dout, dgate = _ew(resid_bwd, [('t', dx_all, 0, d_model), ('t', sv['out'], 0, d_model), ('g', sv['gate'] + back_token)],
                          [('t', d_model, BF16), ('r', d_model, 2)], rows=t_len, tr=tr, n0=n0, name="resid_bwd_%d" % l)
        dmerged = _mm(dout, wout_f, tb=True, name="out_proj_dx_%d" % l)
        g_wout = _mm(sv['merged'], dout, ta=True, tm=1024, tk=t_len, name="out_proj_dw_%d" % l)

        def merge_bwd(dm, y1, y2, g1, g2):
            dm = dm.astype(F32)
            s1, s2_ = _sigmoid(g1.astype(F32)), _sigmoid(g2.astype(F32))
            return dm * s1, dm * s2_, dm * y1.astype(F32) * s1 * (1.0 - s1), dm * y2.astype(F32) * s2_ * (1.0 - s2_)

        u = sv['u']
        dy_na, dy_ret, dg_na, dg_ret = _ew(
            merge_bwd, [('t', dmerged, 0, d_model), ('t', sv['y_na'], 0, d_model), ('t', sv['y_ret'], 0, d_model),
                        ('t', u, o_gna // d_model, d_model), ('t', u, o_gret // d_model, d_model)],
            [('t', d_model, BF16)] * 4, rows=t_len, tr=tr, name="merge_bwd_%d" % l)
        da_na = _mm(dy_na, wpn_f, tb=True, name="proj_na_dx_%d" % l)
        g_wpn = _mm(sv['a_na'], dy_na, ta=True, tm=1024, tk=t_len, name="proj_na_dw_%d" % l)
        da_ret = _mm(dy_ret, wpr_f, tb=True, name="proj_ret_dx_%d" % l)
        g_wpr = _mm(sv['a_ret'], dy_ret, ta=True, tm=1024, tk=t_len, name="proj_ret_dw_%d" % l)
        lg_l = log_gamma[l]
        if l == 0:
            pair_token = start_pair("0_rest", [g_wpn, g_wpr, g_wout], big_axes[1:])

        def act_bwd(da1, o1, z1, da2, o2, z2):
            da1, da2 = da1.astype(F32), da2.astype(F32)
            si1, ds1 = _silu_parts(z1.astype(F32))
            si2, ds2 = _silu_parts(z2.astype(F32))
            do1 = da1 * si1
            dz1 = da1 * o1.astype(F32) * ds1
            dn = da2 * si2
            do2, dz2 = [], []
            for hh in range(ret_heads):
                sl = slice(hh * RET_VAL_DIM, (hh + 1) * RET_VAL_DIM)
                oh = o2[:, sl]
                r = lax.rsqrt(jnp.mean(oh * oh, axis=-1, keepdims=True) + NORM_EPS)
                nh = oh * r
                dz2.append(da2[:, sl] * nh * ds2[:, sl])
                do2.append(r * (dn[:, sl] - nh * jnp.mean(dn[:, sl] * nh, axis=-1, keepdims=True)))
            return do1, dz1, jnp.concatenate(do2, axis=-1), jnp.concatenate(dz2, axis=-1)

        do_na, dz_na, do_ret, dz_ret = _ew(
            act_bwd, [('t', da_na, 0, w_na), ('t', sv['o_na'], 0, w_na), ('t', u, o_naz // w_na, w_na),
                      ('t', da_ret, 0, w_v), ('t', sv['o_ret'], 0, w_v), ('t', u, o_retz // w_v, w_v)],
            [('t', w_na, BF16), ('t', w_na, BF16), ('t', w_v, BF16), ('t', w_v, BF16)], rows=t_len, tr=tr, name="act_bwd_%d" % l)
        dq_na, dk_na, dv_na, dbias = _na_bwd(u, sv['bias'], sv['o_na'], do_na, s_len=s_len, heads=na_heads, name="na_bwd_%d" % l)
        small['drpb'][l] = _rpb_grad(dbias, name="rpb_grad_%d" % l)
        if l == 0:
            lg_l = lg_l + start_scatter("0_rest", dq_na)[0, 0] + pair_token[0, 0]
        dq_r, dk_r, dv_r, dlg = _ret_bwd(u, c2, s2, lg_l, sv['states'], do_ret, s_len=s_len, heads=ret_heads,
                                         q_off=o_retq, name="ret_bwd_%d" % l)
        small['ddecay'][l] = jnp.transpose(dlg[:, :, 0, 0]) * _sigmoid(-ret_decay_logit[l])
        du_parts = [dq_na, dk_na, dv_na, dz_na, dq_r, dk_r, dv_r, dz_ret, dg_na, dg_ret]
        du, = _ew(lambda *t: jnp.concatenate(t, axis=-1), [('t', p, 0, p.shape[1]) for p in du_parts], [('t', in_cols, BF16)],
                  rows=t_len, tr=tr, name="du_concat_%d" % l)
        g_win = _mm(sv['h'], du, ta=True, tm=1024, tn=1024, tk=t_len, name="in_proj_dw_%d" % l)
        if l > 0:
            du, pair_token = lax.optimization_barrier((du, start_pair("%d_all" % l, [g_win, g_wpn, g_wpr, g_wout], big_axes)))
        else:
            du, in_token = lax.optimization_barrier((du, start_pair("0_in", [g_win], big_axes[:1])))
        dh = _mm(du, win_f, tb=True, out_dtype=F32, tm=1152, tn=1024, name="in_proj_dx_%d" % l)

        def modnorm_bwd(xt, dht, dxt, gs_t):
            r = lax.rsqrt(jnp.mean(xt * xt, axis=-1, keepdims=True) + NORM_EPS)
            xh = xt * r
            dhg = dht * gs_t
            dx = r * (dhg - xh * jnp.mean(dhg * xh, axis=-1, keepdims=True)) + dxt
            return dx, _rsum(dht), _rsum(dht * xh)

        dx_all, dshift, dgs = _ew(modnorm_bwd, [('t', sv['x'], 0, d_model), ('t', dh, 0, d_model), ('t', dx_all, 0, d_model), ('g', sv['gs'])],
                                  [('t', d_model, F32), ('r', d_model, 2), ('r', d_model, 2)], rows=t_len, tr=tr, n0=n0, name="modnorm_bwd_%d" % l)
        dscale = dgs * norm_g[l][None, None, :]
        small['dnorm_g'][l] = jnp.sum(dgs * (1.0 + sv['scale']), axis=(0, 1))
        dmod = jnp.concatenate([dshift, dscale, dgate], axis=-1)[:, 0]
        small['dmod_lat'][l], small['dmod_ctx'][l] = dmod[0], dmod[1]

        if l > 0:
            back_token = start_scatter("%d_all" % l, dx_all)[0, 0] + pair_token[0, 0]

    grad_x = dx_all[:s_len][None]

    drpb = jnp.stack(small['drpb']).reshape(-1)
    ddecay = jnp.stack(small['ddecay']).reshape(-1)
    pieces = [jnp.stack(small['dmod_lat']).reshape(-1), jnp.stack(small['dmod_ctx']).reshape(-1),
              jnp.stack(small['dnorm_g']).reshape(-1), d_final_g.reshape(-1), drpb, ddecay, loss_part[None]]
    sizes = [int(p.shape[0]) for p in pieces]
    pads = [-(-s // LANES) * LANES for s in sizes]
    packed = jnp.concatenate([jnp.pad(p, (0, pd - s)) for p, s, pd in zip(pieces, sizes, pads)])
    gathered = _all_gather_small(_pack_rows(packed), name="gather_small_grads")
    r_small = gathered.shape[1]

    def sum8(*t):
        acc = t[0]
        for other in t[1:]:
            acc = acc + other
        return acc

    total, = _ew(sum8, [('t', gathered, 0, LANES, k) for k in range(N_DEV)], [('t', LANES, F32)], rows=r_small, tr=r_small, name="sum_devices")
    total = total.reshape(-1)
    starts = np.cumsum([0] + pads)
    g_mod_lat_sum, g_mod_ctx, g_norm_g, g_final_g, g_rpb, g_decay, loss = [total[starts[i]:starts[i] + sizes[i]] for i in range(len(pieces))]
    loss = loss[0]
    g_ada_b = (g_mod_lat_sum + g_mod_ctx).reshape(depth, mod_cols)
    g_mod_ctx = g_mod_ctx.reshape(depth, mod_cols)
    dmod_lat_all = gathered.reshape(N_DEV, -1)[:, :depth * mod_cols].reshape(N_DEV, depth, mod_cols)

    dcc_part = jnp.zeros((16, d_model), F32)
    ctx_cols = [lax.dynamic_slice_in_dim(g_mod_ctx[l], chip * mod_shard, mod_shard, axis=0) for l in range(depth)]
    for l in reversed(range(depth)):
        c_rows = jnp.concatenate([ctx_cols[l][None], jnp.zeros((15, mod_shard), F32)], axis=0)
        dcc_part = dcc_part + _mm(c_rows, ada_w, tb=True, b_lead=l, out_dtype=F32, name="ada_dc_%d" % l)
    dcc_all = _all_gather_small(_pack_rows(dcc_part[0]), name="gather_dcc")[:, :d_model // LANES].reshape(N_CHIPS, 2, d_model)[:, 0]

    tail_token = start_scatter("0_in", dcc_all) + in_token
    dcc = ((dcc_all[0] + dcc_all[1]) + dcc_all[2]) + dcc_all[3]
    sg = _sigmoid(c_ctx)
    g_c_ctx = dcc * (sg * (1.0 + c_ctx * (1.0 - sg)))
    for l in reversed(range(1, depth)):
        big_res = adamw_big(l, range(4), finish_scatter("%d_all" % l, tail_token), big_res)

    ada_res = None
    for l in reversed(range(depth)):
        lat_cols = lax.dynamic_slice_in_dim(dmod_lat_all[:, l], chip * mod_shard, mod_shard, axis=1)
        d_rows = jnp.concatenate([lat_cols, ctx_cols[l][None], jnp.zeros((16 - N_DEV - 1, mod_shard), F32)], axis=0) + tail_token[0, 0]
        g_ada = _mm(a_rows, d_rows, ta=True, out_dtype=F32, tm=512, name="ada_dw_%d" % l)
        ada_res = _adamw_layer(ada_w, m_ada_w, v_ada_w, g_ada, None, l, ada_res, name="adamw_ada_%d" % l)

    small_w = [(c_ctx, m_c_ctx, v_c_ctx, g_c_ctx), (ada_b, m_ada_b, v_ada_b, g_ada_b),
               (norm_g, m_norm_g, v_norm_g, g_norm_g), (na_rpb, m_na_rpb, v_na_rpb, g_rpb),
               (ret_decay_logit, m_ret_decay_logit, v_ret_decay_logit, g_decay), (final_g, m_final_g, v_final_g, g_final_g)]
    sw_sizes = [int(np.prod(t[0].shape)) for t in small_w]
    sw_pads = [-(-s // LANES) * LANES for s in sw_sizes]

    def pack(j):
        return _pack_rows(jnp.concatenate([jnp.pad(t[j].reshape(-1), (0, pd - s)) for t, s, pd in zip(small_w, sw_sizes, sw_pads)]))

    pw_, pm_, pv_, pg_ = pack(0), pack(1), pack(2), pack(3)
    sw_out = _ew(lambda w, m, v, g: (g,) + _adamw_math(w, g, m, v),
                 [('t', pw_, 0, LANES), ('t', pm_, 0, LANES), ('t', pv_, 0, LANES), ('t', pg_, 0, LANES)],
                 [('t', LANES, F32)] * 4, rows=pw_.shape[0], tr=pw_.shape[0], name="adamw_small")
    sw_starts = np.cumsum([0] + sw_pads)
    sw_out, ada_res, big_res = lax.optimization_barrier((sw_out, ada_res, big_res))
    big_res = adamw_big(0, range(1, 4), finish_scatter("0_rest", sw_out[0]), big_res)
    big_res = adamw_big(0, range(1), finish_scatter("0_in", sw_out[1]), big_res)

    def unpack(arr, i):
        return arr.reshape(-1)[sw_starts[i]:sw_starts[i] + sw_sizes[i]].reshape(small_w[i][0].shape)

    sm = [[unpack(sw_out[j], i) for i in range(len(small_w))] for j in range(4)]
    def ordered(j):
        return [sm[j][0], ada_res[j], sm[j][1], sm[j][2], big_res[0][j], sm[j][3], sm[j][4],
                big_res[1][j], big_res[2][j], big_res[3][j], sm[j][5]]

    return (loss, grad_x, *ordered(0), *ordered(1), *ordered(2), *ordered(3))
```

```python
import functools
import math

import numpy as np
import jax
import jax.numpy as jnp
from jax import lax
from jax.experimental import pallas as pl
from jax.experimental.pallas import tpu as pltpu

GRID_W = 64
NA_HEAD_DIM = 128
NA_WIN_ROWS = 8
NA_WIN_COLS = 16
NA_GROUP = 8
RET_GROUPS = (1, 2, 3)
RET_KEY_DIM = 128
RET_VAL_DIM = 256
RET_CHUNK = 128
ROPE_BASE = 10000.0
NORM_EPS = 1e-6
MASK_VALUE = -1e30
ADAM_LR = 0.001
ADAM_B1 = 0.9
ADAM_B2 = 0.999
ADAM_EPS = 1e-08
ADAM_WD = 0.01
ADAM_STEP = 10

N_CHIPS = 4
N_DEV = 8
LANES = 128
VMEM_LIMIT = 56 * 1024 * 1024
BF16 = jnp.bfloat16
F32 = jnp.float32
MESH = pl.DeviceIdType.MESH
ANY = pl.BlockSpec(memory_space=pl.ANY)


def _tile(dim, pref, align=LANES):
    if dim <= pref:
        return dim
    t = (pref // align) * align
    while t >= align:
        if dim % t == 0:
            return t
        t -= align
    return dim


def _rows_per_tile(rows, width, tile_bytes=1 << 20):
    return _tile(rows, max(8, tile_bytes // (4 * width)), 8)


def _params(sem):
    return pltpu.CompilerParams(dimension_semantics=sem, vmem_limit_bytes=VMEM_LIMIT)


def _sigmoid(x):
    return 1.0 / (1.0 + jnp.exp(-x))


def _dot(a, b, ca, cb):
    return lax.dot_general(a, b, (((ca,), (cb,)), ((), ())), preferred_element_type=F32)


def _mm(a, b, *, ta=False, tb=False, a_lead=None, b_lead=None, out_dtype=BF16, tm=1152, tn=1024, tk=2048, name):
    ash = a.shape[1:] if a_lead is not None else a.shape
    bsh = b.shape[1:] if b_lead is not None else b.shape
    m, k = (ash[1], ash[0]) if ta else ash
    n, k2 = bsh if tb else (bsh[1], bsh[0])
    assert k == k2, (name, ash, bsh)
    tm, tn, tk = _tile(m, tm), _tile(n, tn), _tile(k, tk)
    nk = k // tk

    def lead(spec_shape, imap, l):
        if l is None:
            return pl.BlockSpec(spec_shape, imap)
        return pl.BlockSpec((None,) + spec_shape, lambda i, j, kk: (l,) + imap(i, j, kk))

    a_spec = lead((tk, tm), lambda i, j, kk: (kk, i), a_lead) if ta else lead((tm, tk), lambda i, j, kk: (i, kk), a_lead)
    b_spec = lead((tn, tk), lambda i, j, kk: (j, kk), b_lead) if tb else lead((tk, tn), lambda i, j, kk: (kk, j), b_lead)
    ca, cb = (0 if ta else 1), (1 if tb else 0)

    def body(a_ref, b_ref, o_ref, *scratch):
        part = _dot(a_ref[...].astype(BF16), b_ref[...].astype(BF16), ca, cb)
        if nk == 1:
            o_ref[...] = part.astype(o_ref.dtype)
            return
        acc_ref, = scratch
        kk = pl.program_id(2)

        @pl.when(kk == 0)
        def _():
            acc_ref[...] = part

        @pl.when(kk > 0)
        def _():
            acc_ref[...] += part

        @pl.when(kk == nk - 1)
        def _():
            o_ref[...] = acc_ref[...].astype(o_ref.dtype)

    return pl.pallas_call(
        body, name=name, grid=(m // tm, n // tn, nk),
        in_specs=[a_spec, b_spec],
        out_specs=pl.BlockSpec((tm, tn), lambda i, j, kk: (i, j)),
        out_shape=jax.ShapeDtypeStruct((m, n), out_dtype),
        scratch_shapes=[] if nk == 1 else [pltpu.VMEM((tm, tn), F32)],
        compiler_params=_params(("parallel", "parallel", "arbitrary")),
    )(a, b)


def _ew(fn, ins, outs, *, rows, tr, name, n0=None, aliases=None):
    assert rows % tr == 0, (name, rows, tr)
    nt = rows // tr

    def grp(i):
        return 0 if n0 is None else jnp.where(i < n0, 0, 1)

    in_specs, args = [], []
    for spec in ins:
        if spec[0] == 't':
            arr, cb, w = spec[1], spec[2], spec[3]
            l = spec[4] if len(spec) > 4 else None
            if l is None:
                in_specs.append(pl.BlockSpec((tr, w), functools.partial(lambda i, cb: (i, cb), cb=cb)))
            else:
                in_specs.append(pl.BlockSpec((None, tr, w), functools.partial(lambda i, cb, l: (l, i, cb), cb=cb, l=l)))
            args.append(arr)
        else:
            arr = spec[1]
            g = arr.shape[0]
            if g == 1:
                in_specs.append(pl.BlockSpec((None, 1, arr.shape[2]), lambda i: (0, 0, 0)))
            else:
                in_specs.append(pl.BlockSpec((None, 1, arr.shape[2]), lambda i: (grp(i), 0, 0)))
            args.append(arr)
    out_specs, out_shapes, is_red = [], [], []
    for spec in outs:
        if spec[0] == 't':
            w, dt = spec[1], spec[2]
            if len(spec) > 3:
                l, nl = spec[3], spec[4]
                out_specs.append(pl.BlockSpec((None, tr, w), functools.partial(lambda i, l: (l, i, 0), l=l)))
                out_shapes.append(jax.ShapeDtypeStruct((nl, rows, w), dt))
            else:
                out_specs.append(pl.BlockSpec((tr, w), lambda i: (i, 0)))
                out_shapes.append(jax.ShapeDtypeStruct((rows, w), dt))
            is_red.append(False)
        else:
            w, g = spec[1], spec[2]
            if g == 1:
                out_specs.append(pl.BlockSpec((None, 1, w), lambda i: (0, 0, 0)))
            else:
                out_specs.append(pl.BlockSpec((None, 1, w), lambda i: (grp(i), 0, 0)))
            out_shapes.append(jax.ShapeDtypeStruct((g, 1, w), F32))
            is_red.append(True)
    n_in = len(ins)
    n_alias = 0 if aliases is None else len(aliases)

    def body(*refs):
        in_refs = refs[:n_in]
        out_refs = refs[n_in + n_alias:]
        res = fn(*[r[...] for r in in_refs])
        if not isinstance(res, (tuple, list)):
            res = (res,)
        i = pl.program_id(0)
        first = (i == 0) if n0 is None else ((i == 0) | (i == n0))
        for o_ref, val, red in zip(out_refs, res, is_red):
            if not red:
                o_ref[...] = val.astype(o_ref.dtype)
            else:
                @pl.when(first)
                def _(o_ref=o_ref, val=val):
                    o_ref[...] = val

                @pl.when(jnp.logical_not(first))
                def _(o_ref=o_ref, val=val):
                    o_ref[...] += val

    io_alias = {}
    if aliases is not None:
        for a_idx, (arr, o_idx) in enumerate(aliases):
            in_specs.append(ANY)
            args.append(arr)
            io_alias[n_in + a_idx] = o_idx
    has_red = any(is_red)
    return pl.pallas_call(
        body, name=name, grid=(nt,), in_specs=in_specs, out_specs=out_specs, out_shape=out_shapes,
        input_output_aliases=io_alias,
        compiler_params=_params(("arbitrary",) if has_red else ("parallel",)),
    )(*args)


def _half_spec(tr, width, ax, n_tiles):
    if ax == 1:
        return pl.BlockSpec((tr, width), lambda i, sel: (sel[0] * n_tiles + i, 0))
    return pl.BlockSpec((tr, width), lambda i, sel: (i, sel[0]))


def _sum_pair(g, theirs, ax, ci, *, name):
    pr, pw = theirs.shape
    tr = _rows_per_tile(pr, pw)
    nt = pr // tr

    def body(sel, a_ref, b_ref, o_ref):
        o_ref[...] = (a_ref[...].astype(F32) + b_ref[...].astype(F32)).astype(o_ref.dtype)

    return pl.pallas_call(
        body, name=name,
        grid_spec=pltpu.PrefetchScalarGridSpec(
            num_scalar_prefetch=1, grid=(nt,),
            in_specs=[_half_spec(tr, pw, ax, nt), pl.BlockSpec((tr, pw), lambda i, sel: (i, 0))],
            out_specs=pl.BlockSpec((tr, pw), lambda i, sel: (i, 0))),
        out_shape=jax.ShapeDtypeStruct((pr, pw), BF16),
        compiler_params=_params(("parallel",)),
    )(jnp.reshape(ci, (1,)).astype(jnp.int32), g, theirs)


def _sum_chips_into(own, recv, ax, ci, *, name):
    pr, pw = own.shape
    tr = _rows_per_tile(pr, pw)
    nt = pr // tr
    full_shape = (2 * pr, pw) if ax == 1 else (pr, 2 * pw)

    def body(sel, a_ref, r_ref, o_ref):
        acc = a_ref[...].astype(F32)
        for k in range(N_CHIPS - 1):
            acc = acc + r_ref[k].astype(F32)
        o_ref[...] = acc

    return pl.pallas_call(
        body, name=name,
        grid_spec=pltpu.PrefetchScalarGridSpec(
            num_scalar_prefetch=1, grid=(nt,),
            in_specs=[pl.BlockSpec((tr, pw), lambda i, sel: (i, 0)), pl.BlockSpec((N_CHIPS - 1, tr, pw), lambda i, sel: (0, i, 0))],
            out_specs=_half_spec(tr, pw, ax, nt)),
        out_shape=jax.ShapeDtypeStruct(full_shape, F32),
        compiler_params=_params(("parallel",)),
    )(jnp.reshape(ci, (1,)).astype(jnp.int32), own, recv)


def _rsum(v):
    return jnp.sum(v, axis=0, keepdims=True)


def _silu_parts(z):
    sg = _sigmoid(z)
    return z * sg, sg * (1.0 + z * (1.0 - sg))


def _na_bias_table(rpb, rows, *, name):
    kh, kw = NA_WIN_ROWS, NA_WIN_COLS
    assert rows >= kh
    heads = rpb.shape[0]
    e1, e2 = _na_onehots()
    rpb16 = jnp.pad(rpb, ((0, 0), (0, 16 - rpb.shape[1]), (0, LANES - rpb.shape[2])))

    def body(r_ref, e1_ref, e2_ref, o_ref):
        e1b = e1_ref[...].astype(BF16)
        y = sum(_dot(e1b, part, 0, 0) for part in _split3(r_ref[...]))
        e2b = e2_ref[...].astype(BF16)
        o_ref[...] = sum(_dot(part, e2b, 1, 1) for part in _split3(y))

    z = pl.pallas_call(
        body, name=name, grid=(heads,),
        in_specs=[pl.BlockSpec((None, 16, LANES), lambda h: (h, 0, 0)),
                  pl.BlockSpec(e1.shape, lambda h: (0, 0)), pl.BlockSpec(e2.shape, lambda h: (0, 0))],
        out_specs=pl.BlockSpec((None, kh * kh, GRID_W * GRID_W), lambda h: (h, 0, 0)),
        out_shape=jax.ShapeDtypeStruct((heads, kh * kh, GRID_W * GRID_W), F32),
        compiler_params=_params(("parallel",)),
    )(rpb16, e1, e2)
    cidx = np.arange(GRID_W)
    c0 = np.clip(cidx - kw // 2, 0, GRID_W - kw)
    col_in = (cidx[None, :] >= c0[:, None]) & (cidx[None, :] < c0[:, None] + kw)
    bias = z.reshape(heads, kh, kh, GRID_W, GRID_W).transpose(0, 1, 3, 2, 4)
    bias = jnp.where(col_in[None, None, :, None, :], bias, MASK_VALUE)
    return bias.reshape(heads, kh, GRID_W, kh * GRID_W)


def _na_onehots():
    kh, kw = NA_WIN_ROWS, NA_WIN_COLS
    cidx = np.arange(GRID_W)
    dc = cidx[None, :] - cidx[:, None] + (kw - 1)
    e2 = np.zeros((GRID_W * GRID_W, LANES), np.float32)
    ok = (dc >= 0) & (dc <= 2 * kw - 2)
    cq, ck = np.nonzero(ok)
    e2[cq * GRID_W + ck, dc[cq, ck]] = 1.0
    dr = np.arange(kh)[None, :] - np.arange(kh)[:, None] + (kh - 1)
    e1 = np.zeros((16, kh * kh), np.float32)
    dl, kr = np.nonzero(np.ones_like(dr))
    e1[dr[dl, kr], dl * kh + kr] = 1.0
    return jnp.asarray(e1), jnp.asarray(e2)


def _na_fwd(u, bias, *, s_len, heads, name):
    t_len = u.shape[0]
    rows = s_len // GRID_W
    nloc = NA_WIN_ROWS * GRID_W
    scale = NA_HEAD_DIM ** -0.5
    hd = NA_HEAD_DIM

    def body(q_ref, k_ref, v_ref, b_ref, o_ref):
        kc = k_ref[s_len:t_len, :]
        vc = v_ref[s_len:t_len, :]

        def group(g, carry):
            rs = [g * NA_GROUP + i for i in range(NA_GROUP)]
            r0s = [jnp.clip(r - NA_WIN_ROWS // 2, 0, rows - NA_WIN_ROWS) for r in rs]
            gs_ = pl.multiple_of(g * (NA_GROUP * GRID_W), NA_GROUP * GRID_W)
            kss = [pl.multiple_of(r0 * GRID_W, GRID_W) for r0 in r0s]
            q_all = q_ref[pl.ds(gs_, NA_GROUP * GRID_W), :]
            s_ctx = _dot(q_all, kc, 1, 1) * scale
            s_loc = [_dot(q_all[i * GRID_W:(i + 1) * GRID_W], k_ref[pl.ds(kss[i], nloc), :], 1, 1) * scale + b_ref[rs[i] - r0s[i]]
                     for i in range(NA_GROUP)]
            p_loc, p_ctx, inv = [], [], []
            for i in range(NA_GROUP):
                sc = s_ctx[i * GRID_W:(i + 1) * GRID_W]
                m = jnp.maximum(jnp.max(s_loc[i], axis=-1, keepdims=True), jnp.max(sc, axis=-1, keepdims=True))
                pl_, pc_ = jnp.exp(s_loc[i] - m), jnp.exp(sc - m)
                inv.append(1.0 / (jnp.sum(pl_, axis=-1, keepdims=True) + jnp.sum(pc_, axis=-1, keepdims=True)))
                p_loc.append(pl_.astype(BF16))
                p_ctx.append(pc_.astype(BF16))
            o_ctx = _dot(jnp.concatenate(p_ctx, axis=0), vc, 1, 0)
            o_loc = [_dot(p_loc[i], v_ref[pl.ds(kss[i], nloc), :], 1, 0) for i in range(NA_GROUP)]
            out = jnp.concatenate([(o_loc[i] + o_ctx[i * GRID_W:(i + 1) * GRID_W]) * inv[i] for i in range(NA_GROUP)], axis=0)
            o_ref[pl.ds(gs_, NA_GROUP * GRID_W), :] = out.astype(o_ref.dtype)
            return carry

        lax.fori_loop(0, rows // NA_GROUP, group, 0)
        qc = q_ref[s_len:t_len, :]
        s = _dot(qc, kc, 1, 1) * scale
        p = jnp.exp(s - jnp.max(s, axis=-1, keepdims=True))
        o = _dot(p.astype(BF16), vc, 1, 0) / jnp.sum(p, axis=-1, keepdims=True)
        o_ref[s_len:t_len, :] = o.astype(o_ref.dtype)

    col = lambda off: pl.BlockSpec((t_len, hd), functools.partial(lambda h, off: (0, off + h), off=off))
    return pl.pallas_call(
        body, name=name, grid=(heads,),
        in_specs=[col(0), col(heads), col(2 * heads),
                  pl.BlockSpec((None, NA_WIN_ROWS, GRID_W, nloc), lambda h: (h, 0, 0, 0))],
        out_specs=pl.BlockSpec((t_len, hd), lambda h: (0, h)),
        out_shape=jax.ShapeDtypeStruct((t_len, heads * hd), BF16),
        compiler_params=_params(("parallel",)),
    )(u, u, u, bias)


def _na_bwd(u, bias, o, do, *, s_len, heads, name):
    t_len = u.shape[0]
    rows = s_len // GRID_W
    nloc = NA_WIN_ROWS * GRID_W
    scale = NA_HEAD_DIM ** -0.5
    hd = NA_HEAD_DIM

    def body(q_ref, k_ref, v_ref, b_ref, o_ref, do_ref, dq_ref, dk_ref, dv_ref, db_ref, dk_acc, dv_acc):
        kc = k_ref[s_len:t_len, :]
        vc = v_ref[s_len:t_len, :]
        dk_acc[...] = jnp.zeros_like(dk_acc)
        dv_acc[...] = jnp.zeros_like(dv_acc)
        db_ref[...] = jnp.zeros_like(db_ref)

        def group(g, carry):
            n_g, rw = NA_GROUP, GRID_W
            rs = [g * n_g + i for i in range(n_g)]
            r0s = [jnp.clip(r - NA_WIN_ROWS // 2, 0, rows - NA_WIN_ROWS) for r in rs]
            dls = [r - r0 for r, r0 in zip(rs, r0s)]
            gs_ = pl.ds(pl.multiple_of(g * (n_g * rw), n_g * rw), n_g * rw)
            kss = [pl.ds(pl.multiple_of(r0 * rw, rw), nloc) for r0 in r0s]
            row_of = lambda a, i: a[i * rw:(i + 1) * rw]
            q_all, do_all = q_ref[gs_, :], do_ref[gs_, :]
            dlt_all = jnp.sum(do_all.astype(F32) * o_ref[gs_, :].astype(F32), axis=-1, keepdims=True)
            s_ctx = _dot(q_all, kc, 1, 1) * scale
            dp_ctx = _dot(do_all, vc, 1, 1)
            s_loc = [_dot(row_of(q_all, i), k_ref[kss[i], :], 1, 1) * scale + b_ref[dls[i]] for i in range(n_g)]
            dp_loc = [_dot(row_of(do_all, i), v_ref[kss[i], :], 1, 1) for i in range(n_g)]
            p_loc_b, ds_loc_b, p_ctx_b, ds_ctx_b = [], [], [], []
            for i in range(n_g):
                sc, dlt = row_of(s_ctx, i), row_of(dlt_all, i)
                m = jnp.maximum(jnp.max(s_loc[i], axis=-1, keepdims=True), jnp.max(sc, axis=-1, keepdims=True))
                pl_, pc_ = jnp.exp(s_loc[i] - m), jnp.exp(sc - m)
                inv = 1.0 / (jnp.sum(pl_, axis=-1, keepdims=True) + jnp.sum(pc_, axis=-1, keepdims=True))
                pl_, pc_ = pl_ * inv, pc_ * inv
                ds_l = pl_ * (dp_loc[i] - dlt)
                db_ref[dls[i]] += ds_l
                p_loc_b.append(pl_.astype(BF16))
                ds_loc_b.append(ds_l.astype(BF16))
                p_ctx_b.append(pc_.astype(BF16))
                ds_ctx_b.append((pc_ * (row_of(dp_ctx, i) - dlt)).astype(BF16))
            p_ctx_all, ds_ctx_all = jnp.concatenate(p_ctx_b, axis=0), jnp.concatenate(ds_ctx_b, axis=0)
            dq_ctx = _dot(ds_ctx_all, kc, 1, 0)
            dq_loc = [_dot(ds_loc_b[i], k_ref[kss[i], :], 1, 0) for i in range(n_g)]
            dk_loc = [_dot(ds_loc_b[i], row_of(q_all, i), 0, 0) for i in range(n_g)]
            dv_loc = [_dot(p_loc_b[i], row_of(do_all, i), 0, 0) for i in range(n_g)]
            dk_ctx = _dot(ds_ctx_all, q_all, 0, 0)
            dv_ctx = _dot(p_ctx_all, do_all, 0, 0)
            dq_ref[gs_, :] = ((jnp.concatenate(dq_loc, axis=0) + dq_ctx) * scale).astype(dq_ref.dtype)
            for i in range(n_g):
                dk_acc[kss[i], :] += dk_loc[i] * scale
                dv_acc[kss[i], :] += dv_loc[i]
            dk_acc[s_len:t_len, :] += dk_ctx * scale
            dv_acc[s_len:t_len, :] += dv_ctx
            return carry

        lax.fori_loop(0, rows // NA_GROUP, group, 0)
        qc = q_ref[s_len:t_len, :]
        dout = do_ref[s_len:t_len, :]
        out = o_ref[s_len:t_len, :]
        s = _dot(qc, kc, 1, 1) * scale
        p = jnp.exp(s - jnp.max(s, axis=-1, keepdims=True))
        p = p / jnp.sum(p, axis=-1, keepdims=True)
        dlt = jnp.sum(dout.astype(F32) * out.astype(F32), axis=-1, keepdims=True)
        ds = (p * (_dot(dout, vc, 1, 1) - dlt)).astype(BF16)
        dq_ref[s_len:t_len, :] = (_dot(ds, kc, 1, 0) * scale).astype(dq_ref.dtype)
        dk_acc[s_len:t_len, :] += _dot(ds, qc, 0, 0) * scale
        dv_acc[s_len:t_len, :] += _dot(p.astype(BF16), dout, 0, 0)
        dk_ref[...] = dk_acc[...].astype(dk_ref.dtype)
        dv_ref[...] = dv_acc[...].astype(dv_ref.dtype)

    col = lambda off: pl.BlockSpec((t_len, hd), functools.partial(lambda h, off: (0, off + h), off=off))
    tbl = pl.BlockSpec((None, NA_WIN_ROWS, GRID_W, nloc), lambda h: (h, 0, 0, 0))
    tok = jax.ShapeDtypeStruct((t_len, heads * hd), BF16)
    return pl.pallas_call(
        body, name=name, grid=(heads,),
        in_specs=[col(0), col(heads), col(2 * heads), tbl, col(0), col(0)],
        out_specs=[col(0), col(0), col(0), tbl],
        out_shape=[tok, tok, tok, jax.ShapeDtypeStruct(bias.shape, F32)],
        scratch_shapes=[pltpu.VMEM((t_len, hd), F32), pltpu.VMEM((t_len, hd), F32)],
        compiler_params=_params(("parallel",)),
    )(u, u, u, bias, o, do)


def _split3(x):
    hi = x.astype(BF16)
    r1 = x - hi.astype(F32)
    mid = r1.astype(BF16)
    lo = (r1 - mid.astype(F32)).astype(BF16)
    return hi, mid, lo


def _rpb_grad(dbias, *, name):
    heads = dbias.shape[0]
    kh = NA_WIN_ROWS
    e1, e2 = _na_onehots()
    x = dbias.reshape(heads, kh, GRID_W, kh, GRID_W).transpose(0, 1, 3, 2, 4).reshape(heads, kh * kh, GRID_W * GRID_W)

    def body(x_ref, e1_ref, e2_ref, o_ref):
        e2b = e2_ref[...].astype(BF16)
        y = sum(_dot(part, e2b, 1, 0) for part in _split3(x_ref[...]))
        e1b = e1_ref[...].astype(BF16)
        o_ref[...] = sum(_dot(e1b, part, 1, 0) for part in _split3(y))

    out = pl.pallas_call(
        body, name=name, grid=(heads,),
        in_specs=[pl.BlockSpec((None, kh * kh, GRID_W * GRID_W), lambda h: (h, 0, 0)),
                  pl.BlockSpec(e1.shape, lambda h: (0, 0)), pl.BlockSpec(e2.shape, lambda h: (0, 0))],
        out_specs=pl.BlockSpec((None, 16, LANES), lambda h: (h, 0, 0)),
        out_shape=jax.ShapeDtypeStruct((heads, 16, LANES), F32),
        compiler_params=_params(("parallel",)),
    )(x, e1, e2)
    return out[:, :2 * kh - 1, :2 * NA_WIN_COLS - 1]


def _rope_tables(s_len, l_len):
    nf = RET_KEY_DIM // 4
    t = np.arange(s_len)
    row = (t // GRID_W).astype(np.float32)
    colp = (t % GRID_W).astype(np.float32)
    inv_freq = jnp.asarray(ROPE_BASE, F32) ** (-jnp.arange(nf, dtype=F32) / nf)
    ang = jnp.concatenate([jnp.asarray(row)[:, None] * inv_freq, jnp.asarray(colp)[:, None] * inv_freq], axis=-1)
    cos, sin = jnp.cos(ang), jnp.sin(ang)
    c2 = jnp.concatenate([cos, cos], axis=-1)
    s2 = jnp.concatenate([-sin, sin], axis=-1)
    c2 = jnp.concatenate([c2, jnp.ones((l_len, RET_KEY_DIM), F32)], axis=0)
    s2 = jnp.concatenate([s2, jnp.zeros((l_len, RET_KEY_DIM), F32)], axis=0)
    return c2, s2


def _rope(x, c2, s2):
    return x * c2 + pltpu.roll(x, RET_KEY_DIM // 2, 1) * s2


def _rope_t(d, c2, s2):
    return d * c2 + pltpu.roll(d * s2, RET_KEY_DIM // 2, 1)


def _ret_decays(lg, direction):
    cs = RET_CHUNK
    i_col = lax.broadcasted_iota(jnp.int32, (cs, 1), 0)
    p_col = jnp.where(direction == 0, i_col, cs - 1 - i_col).astype(F32)
    pi = lax.broadcasted_iota(jnp.int32, (cs, cs), 0)
    pj = lax.broadcasted_iota(jnp.int32, (cs, cs), 1)
    diff = jnp.where(direction == 0, pi - pj, pj - pi).astype(F32)
    dm = jnp.where(diff >= 0, jnp.exp(jnp.maximum(diff, 0.0) * lg), 0.0)
    qdec = jnp.exp((p_col + 1.0) * lg)
    kdec = jnp.exp((cs - 1.0 - p_col) * lg)
    cd = jnp.exp(jnp.full((1, 1), cs, F32) * lg)
    return p_col, dm, qdec, kdec, cd


def _ret_chunk_index(t, direction, n_chunks, lat_chunks):
    return jnp.where(direction == 0, lax.rem(t + lat_chunks, n_chunks), n_chunks - 1 - t)


def _ret_fwd(u, c2, s2, lg, *, s_len, heads, q_off, name):
    t_len = u.shape[0]
    cs, dk, dv = RET_CHUNK, RET_KEY_DIM, RET_VAL_DIM
    n_chunks, lat_chunks = t_len // cs, s_len // cs
    k_scale = dk ** -0.5
    qb, kb, vb = q_off // dk, q_off // dk + heads, (q_off + 2 * heads * dk) // dv

    def body(lg_ref, q_ref, k_ref, v_ref, c_ref, s_ref, o_ref, st_ref, qd_s, kv_s):
        h, d = pl.program_id(0), pl.program_id(1)
        _, dm, qdec, kdec, cd = _ret_decays(lg_ref[d, h], d)
        n_g = max(g for g in RET_GROUPS if n_chunks % g == 0)
        rows_of = lambda c: pl.ds(pl.multiple_of(c * cs, cs), cs)

        def local(gi, carry):
            rws = [rows_of(gi * n_g + j) for j in range(n_g)]
            qcs = [_rope(q_ref[r, :].astype(F32), c_ref[r, :], s_ref[r, :]) for r in rws]
            kcs = [_rope(k_ref[r, :].astype(F32), c_ref[r, :], s_ref[r, :]) * k_scale for r in rws]
            vcs = [v_ref[r, :] for r in rws]
            a_raw = [_dot(qcs[j].astype(BF16), kcs[j].astype(BF16), 1, 1) for j in range(n_g)]
            kv = [_dot((kcs[j] * kdec).astype(BF16), vcs[j], 0, 0) for j in range(n_g)]
            inner = [_dot((a_raw[j] * dm).astype(BF16), vcs[j], 1, 0) for j in range(n_g)]
            for j in range(n_g):
                qd_s[rws[j], :] = (qcs[j] * qdec).astype(BF16)
                kv_s[gi * n_g + j] = kv[j]

            @pl.when(d == 0)
            def _():
                for j in range(n_g):
                    o_ref[rws[j], :] = inner[j]

            @pl.when(d == 1)
            def _():
                for j in range(n_g):
                    o_ref[rws[j], :] += inner[j]

            return carry

        lax.fori_loop(0, n_chunks // n_g, local, 0)

        def scan(t, st):
            st_ref[t] = st
            return st * cd + kv_s[_ret_chunk_index(t, d, n_chunks, lat_chunks)]

        lax.fori_loop(0, n_chunks, scan, jnp.zeros((dk, dv), F32))

        def cross(gi, carry):
            ts = [gi * n_g + j for j in range(n_g)]
            rws = [rows_of(_ret_chunk_index(t, d, n_chunks, lat_chunks)) for t in ts]
            outs = [_dot(qd_s[rws[j], :], st_ref[ts[j]].astype(BF16), 1, 0) for j in range(n_g)]
            for j in range(n_g):
                o_ref[rws[j], :] += outs[j]
            return carry

        lax.fori_loop(0, n_chunks // n_g, cross, 0)

    return pl.pallas_call(
        body, name=name, grid=(heads, 2),
        in_specs=[pl.BlockSpec(memory_space=pltpu.SMEM),
                  pl.BlockSpec((t_len, dk), lambda h, d: (0, qb + h)),
                  pl.BlockSpec((t_len, dk), lambda h, d: (0, kb + h)),
                  pl.BlockSpec((t_len, dv), lambda h, d: (0, vb + h)),
                  pl.BlockSpec((t_len, dk), lambda h, d: (0, 0)),
                  pl.BlockSpec((t_len, dk), lambda h, d: (0, 0))],
        out_specs=[pl.BlockSpec((t_len, dv), lambda h, d: (0, h)),
                   pl.BlockSpec((None, None, n_chunks, dk, dv), lambda h, d: (h, d, 0, 0, 0))],
        out_shape=[jax.ShapeDtypeStruct((t_len, heads * dv), F32),
                   jax.ShapeDtypeStruct((heads, 2, n_chunks, dk, dv), F32)],
        scratch_shapes=[pltpu.VMEM((t_len, dk), BF16), pltpu.VMEM((n_chunks, dk, dv), F32)],
        compiler_params=_params(("parallel", "arbitrary")),
    )(lg, u, u, u, c2, s2)


def _ret_bwd(u, c2, s2, lg, states, do, *, s_len, heads, q_off, name):
    t_len = u.shape[0]
    cs, dk, dv = RET_CHUNK, RET_KEY_DIM, RET_VAL_DIM
    n_chunks, lat_chunks = t_len // cs, s_len // cs
    k_scale = dk ** -0.5
    qb, kb, vb = q_off // dk, q_off // dk + heads, (q_off + 2 * heads * dk) // dv

    def body(lg_ref, q_ref, k_ref, v_ref, c_ref, s_ref, st_ref, do_ref, dq_ref, dk_ref, dv_ref, dlg_ref, acc, qdo_s, dst_s):
        h, d = pl.program_id(0), pl.program_id(1)
        p_col, dm, qdec, kdec, cd = _ret_decays(lg_ref[d, h], d)
        acc[...] = jnp.zeros_like(acc)
        n_g = max(g for g in RET_GROUPS[:2] if n_chunks % g == 0)
        rows_of = lambda c: pl.ds(pl.multiple_of(c * cs, cs), cs)
        chunk_of = lambda t: _ret_chunk_index(t, d, n_chunks, lat_chunks)

        def local(gi, carry):
            rws = [rows_of(gi * n_g + j) for j in range(n_g)]
            qds = [(_rope(q_ref[r, :].astype(F32), c_ref[r, :], s_ref[r, :]) * qdec).astype(BF16) for r in rws]
            prods = [_dot(qds[j], do_ref[rws[j], :].astype(BF16), 0, 0) for j in range(n_g)]
            for j in range(n_g):
                qdo_s[gi * n_g + j] = prods[j]
            return carry

        lax.fori_loop(0, n_chunks // n_g, local, 0)

        def scan(i, dst):
            t = n_chunks - 1 - i
            dst_s[t] = dst
            return dst * cd + qdo_s[chunk_of(t)]

        lax.fori_loop(0, n_chunks, scan, jnp.zeros((dk, dv), F32))

        def grads(gi, carry):
            ts = [gi * n_g + j for j in range(n_g)]
            rws = [rows_of(chunk_of(t)) for t in ts]
            ccs, sss = [c_ref[r, :] for r in rws], [s_ref[r, :] for r in rws]
            qcs = [_rope(q_ref[r, :].astype(F32), cc, ss) for r, cc, ss in zip(rws, ccs, sss)]
            kcs = [_rope(k_ref[r, :].astype(F32), cc, ss) * k_scale for r, cc, ss in zip(rws, ccs, sss)]
            vcs = [v_ref[r, :] for r in rws]
            docs = [do_ref[r, :].astype(BF16) for r in rws]
            sts = [st_ref[t] for t in ts]
            dsts = [dst_s[t] for t in ts]
            q16 = [x.astype(BF16) for x in qcs]
            k16 = [x.astype(BF16) for x in kcs]
            dst16 = [x.astype(BF16) for x in dsts]
            rng = range(n_g)
            a_raw = [_dot(q16[j], k16[j], 1, 1) for j in rng]
            da_raw = [_dot(docs[j], vcs[j], 1, 1) for j in rng]
            dq_c = [_dot(docs[j], sts[j].astype(BF16), 1, 1) * qdec for j in rng]
            dv_s = [_dot((kcs[j] * kdec).astype(BF16), dst16[j], 1, 0) for j in rng]
            dk_s = [_dot(vcs[j], dst16[j], 1, 1) * kdec for j in rng]
            a16 = [(a_raw[j] * dm).astype(BF16) for j in rng]
            dam = [(da_raw[j] * dm).astype(BF16) for j in rng]
            dq_i = [_dot(dam[j], k16[j], 1, 0) for j in rng]
            dk_i = [_dot(dam[j], q16[j], 0, 0) for j in rng]
            dv_i = [_dot(a16[j], docs[j], 0, 0) for j in rng]
            for j in rng:
                g = (jnp.sum(qcs[j] * (p_col * dq_i[j] + (p_col + 1.0) * dq_c[j]), axis=-1, keepdims=True)
                     + jnp.sum(kcs[j] * ((cs - 1.0 - p_col) * dk_s[j] - p_col * dk_i[j]), axis=-1, keepdims=True))
                g = (jnp.sum(g, axis=0, keepdims=True)
                     + cs * cd * jnp.sum(jnp.sum(dsts[j] * sts[j], axis=-1, keepdims=True), axis=0, keepdims=True))
                acc[...] += jnp.broadcast_to(g, acc.shape)
            dqs = [_rope_t(dq_i[j] + dq_c[j], ccs[j], sss[j]) for j in rng]
            dks = [_rope_t((dk_i[j] + dk_s[j]) * k_scale, ccs[j], sss[j]) for j in rng]
            dvs = [dv_i[j] + dv_s[j] for j in rng]

            @pl.when(d == 0)
            def _():
                for j in rng:
                    dq_ref[rws[j], :] = dqs[j].astype(dq_ref.dtype)
                    dk_ref[rws[j], :] = dks[j].astype(dk_ref.dtype)
                    dv_ref[rws[j], :] = dvs[j].astype(dv_ref.dtype)

            @pl.when(d == 1)
            def _():
                for j in rng:
                    dq_ref[rws[j], :] = (dq_ref[rws[j], :].astype(F32) + dqs[j]).astype(dq_ref.dtype)
                    dk_ref[rws[j], :] = (dk_ref[rws[j], :].astype(F32) + dks[j]).astype(dk_ref.dtype)
                    dv_ref[rws[j], :] = (dv_ref[rws[j], :].astype(F32) + dvs[j]).astype(dv_ref.dtype)

            return carry

        lax.fori_loop(0, n_chunks // n_g, grads, 0)
        dlg_ref[...] = acc[...]

    return pl.pallas_call(
        body, name=name, grid=(heads, 2),
        in_specs=[pl.BlockSpec(memory_space=pltpu.SMEM),
                  pl.BlockSpec((t_len, dk), lambda h, d: (0, qb + h)),
                  pl.BlockSpec((t_len, dk), lambda h, d: (0, kb + h)),
                  pl.BlockSpec((t_len, dv), lambda h, d: (0, vb + h)),
                  pl.BlockSpec((t_len, dk), lambda h, d: (0, 0)),
                  pl.BlockSpec((t_len, dk), lambda h, d: (0, 0)),
                  pl.BlockSpec((None, None, n_chunks, dk, dv), lambda h, d: (h, d, 0, 0, 0)),
                  pl.BlockSpec((t_len, dv), lambda h, d: (0, h))],
        out_specs=[pl.BlockSpec((t_len, dk), lambda h, d: (0, h)),
                   pl.BlockSpec((t_len, dk), lambda h, d: (0, h)),
                   pl.BlockSpec((t_len, dv), lambda h, d: (0, h)),
                   pl.BlockSpec((None, None, 8, LANES), lambda h, d: (h, d, 0, 0))],
        out_shape=[jax.ShapeDtypeStruct((t_len, heads * dk), BF16),
                   jax.ShapeDtypeStruct((t_len, heads * dk), BF16),
                   jax.ShapeDtypeStruct((t_len, heads * dv), BF16),
                   jax.ShapeDtypeStruct((heads, 2, 8, LANES), F32)],
        scratch_shapes=[pltpu.VMEM((8, LANES), F32), pltpu.VMEM((n_chunks, dk, dv), F32), pltpu.VMEM((n_chunks, dk, dv), F32)],
        compiler_params=_params(("parallel", "arbitrary")),
    )(lg, u, u, u, c2, s2, states, do)


def _mesh_pos():
    return lax.axis_index("x"), lax.axis_index("y"), lax.axis_index("c")


def _all_gather_small(buf, *, name):
    r = buf.shape[0]

    def body(x_ref, o_ref, send_sems, recv_sems, local_sem):
        x, y, c = _mesh_pos()
        me = 4 * x + 2 * y + c
        mine = pltpu.make_async_copy(x_ref, o_ref.at[me], local_sem)
        mine.start()
        copies = []
        for k in range(1, N_DEV):
            px, py, pc = x ^ ((k >> 2) & 1), y ^ ((k >> 1) & 1), c ^ (k & 1)
            cp = pltpu.make_async_remote_copy(
                src_ref=x_ref, dst_ref=o_ref.at[me], send_sem=send_sems.at[k - 1], recv_sem=recv_sems.at[k - 1],
                device_id=(px, py, pc), device_id_type=MESH)
            cp.start()
            copies.append((cp, 4 * px + 2 * py + pc))
        for k, (cp, peer) in enumerate(copies):
            pltpu.make_async_remote_copy(
                src_ref=x_ref, dst_ref=o_ref.at[peer], send_sem=send_sems.at[k], recv_sem=recv_sems.at[k],
                device_id=(x, y, c), device_id_type=MESH).wait_recv()
        for cp, _ in copies:
            cp.wait_send()
        mine.wait()

    return pl.pallas_call(
        body, name=name,
        in_specs=[pl.BlockSpec(memory_space=pltpu.VMEM)],
        out_specs=pl.BlockSpec(memory_space=pltpu.VMEM),
        out_shape=jax.ShapeDtypeStruct((N_DEV, r, LANES), F32),
        scratch_shapes=[pltpu.SemaphoreType.DMA((N_DEV - 1,)), pltpu.SemaphoreType.DMA((N_DEV - 1,)),
                        pltpu.SemaphoreType.DMA],
        compiler_params=pltpu.CompilerParams(vmem_limit_bytes=VMEM_LIMIT),
    )(buf)


def _cut(ref, shard_axis, *, chip=None, half=None, lead=None):
    shape = ref.shape[1:] if lead is not None else ref.shape
    idx = [slice(None), slice(None)]
    if chip is not None:
        w = shape[shard_axis] // N_CHIPS
        idx[shard_axis] = pl.ds(pl.multiple_of(chip * w, w), w)
    if half is not None:
        hw = shape[1 - shard_axis] // 2
        idx[1 - shard_axis] = pl.ds(pl.multiple_of(half * hw, hw), hw)
    if lead is not None:
        idx = [lead] + idx
    return ref.at[tuple(idx)]


def _wait_recv(ref, send_sem, recv_sem):
    pltpu.make_async_remote_copy(src_ref=ref, dst_ref=ref, send_sem=send_sem, recv_sem=recv_sem,
                                 device_id=_mesh_pos(), device_id_type=MESH).wait_recv()


def _gather_plan(axes):
    def plan(srcs, lands, send_sems, recv_sems):
        x, y, c = _mesh_pos()
        chip = 2 * x + y
        copies = []
        for i, ax in enumerate(axes):
            for k in range(1, N_CHIPS):
                px, py = x ^ (k >> 1), y ^ (k & 1)
                mine = _cut(lands[i], ax, chip=chip, half=c)
                j = i * (N_CHIPS - 1) + k - 1
                sems = dict(send_sem=send_sems.at[j], recv_sem=recv_sems.at[j], device_id=(px, py, c), device_id_type=MESH)
                send = pltpu.make_async_remote_copy(src_ref=mine, dst_ref=mine, **sems)
                recv = pltpu.make_async_remote_copy(src_ref=mine, dst_ref=_cut(lands[i], ax, chip=2 * px + py, half=c), **sems)
                copies.append((send, recv))
        return copies
    return plan


def _gather_near_plan(axes):
    def plan(srcs, lands, send_sems, recv_sems):
        x, y, c = _mesh_pos()
        copies = []
        for i, ax in enumerate(axes):
            mine = _cut(lands[i], ax, chip=2 * x + y, half=c)
            for k, (px, py) in enumerate(((1 - x, y), (x, 1 - y))):
                sems = dict(send_sem=send_sems.at[2 * i + k], recv_sem=recv_sems.at[2 * i + k], device_id=(px, py, c), device_id_type=MESH)
                send = pltpu.make_async_remote_copy(src_ref=mine, dst_ref=mine, **sems)
                recv = pltpu.make_async_remote_copy(src_ref=mine, dst_ref=_cut(lands[i], ax, chip=2 * px + py, half=c), **sems)
                copies.append((send, recv))
        return copies
    return plan


def _gather_far_plan(axes):
    def plan(srcs, lands, send_sems, recv_sems):
        x, y, c = _mesh_pos()
        from_chip = 2 * (x ^ (1 - c)) + (y ^ c)
        to = (x ^ c, y ^ (1 - c), c)
        diag = 2 * (1 - x) + (1 - y)
        copies = []
        for i, ax in enumerate(axes):
            passed = _cut(lands[i], ax, chip=from_chip, half=c)
            sems = dict(send_sem=send_sems.at[i], recv_sem=recv_sems.at[i], device_id=to, device_id_type=MESH)
            send = pltpu.make_async_remote_copy(src_ref=passed, dst_ref=passed, **sems)
            recv = pltpu.make_async_remote_copy(src_ref=passed, dst_ref=_cut(lands[i], ax, chip=diag, half=c), **sems)
            copies.append((send, recv))
        return copies
    return plan


def _pair_plan(axes):
    def plan(srcs, lands, send_sems, recv_sems):
        x, y, c = _mesh_pos()
        copies = []
        for i, ax in enumerate(axes):
            cp = pltpu.make_async_remote_copy(
                src_ref=_cut(srcs[i], ax, half=1 - c), dst_ref=lands[i], send_sem=send_sems.at[i], recv_sem=recv_sems.at[i],
                device_id=(x, y, 1 - c), device_id_type=MESH)
            copies.append((cp, cp))
        return copies
    return plan


def _scatter_plan(axes):
    def plan(srcs, lands, send_sems, recv_sems):
        x, y, c = _mesh_pos()
        copies = []
        for i, ax in enumerate(axes):
            for k in range(1, N_CHIPS):
                px, py = x ^ (k >> 1), y ^ (k & 1)
                j = i * (N_CHIPS - 1) + k - 1
                cp = pltpu.make_async_remote_copy(
                    src_ref=_cut(srcs[i], ax, chip=2 * px + py), dst_ref=lands[i].at[k - 1],
                    send_sem=send_sems.at[j], recv_sem=recv_sems.at[j], device_id=(px, py, c), device_id_type=MESH)
                copies.append((cp, cp))
        return copies
    return plan


HBM = pl.BlockSpec(memory_space=pltpu.HBM)
SEM = pl.BlockSpec(memory_space=pltpu.SEMAPHORE)
EFFECT = pltpu.SideEffectType.DATAFLOW_SIDE_EFFECTING


def _in_hbm(arrays):
    return [pltpu.with_memory_space_constraint(a, pltpu.HBM) for a in arrays]


def _split_start(srcs, lands, plan, n_copies, *, name):
    bufs = list(srcs) + list(lands)
    ns, nb = len(srcs), len(bufs)

    def body(*refs):
        send_sems, recv_sems, token = refs[nb], refs[nb + 1], refs[-1]
        for send, _ in plan(refs[:ns], refs[ns:nb], send_sems, recv_sems):
            send.start()
        token[...] = jnp.zeros_like(token)

    sems = pltpu.SemaphoreType.DMA((n_copies,))
    res = pl.pallas_call(
        body, name=name, in_specs=[HBM] * nb,
        out_specs=[SEM, SEM] + [HBM] * nb + [pl.BlockSpec(memory_space=pltpu.VMEM)],
        out_shape=[sems, sems] + [pltpu.HBM(a.shape, a.dtype) for a in bufs] + [jax.ShapeDtypeStruct((8, LANES), F32)],
        input_output_aliases={j: 2 + j for j in range(nb)},
        compiler_params=pltpu.CompilerParams(has_side_effects=EFFECT),
    )(*_in_hbm(bufs))
    return res[0], res[1], res[2:2 + ns], res[2 + ns:2 + nb], res[-1]


def _split_wait(started, after, plan, *, name, with_srcs=False):
    send_sems, recv_sems, srcs, lands, _ = started
    bufs = list(srcs) + list(lands)
    ns, nb = len(srcs), len(bufs)

    def body(*refs):
        for send, recv in plan(refs[:ns], refs[ns:nb], refs[nb], refs[nb + 1]):
            send.wait_send()
            recv.wait_recv()

    res = pl.pallas_call(
        body, name=name, in_specs=[HBM] * nb + [SEM, SEM, ANY], out_specs=[HBM] * nb,
        out_shape=[pltpu.HBM(a.shape, a.dtype) for a in bufs],
        input_output_aliases={j: j for j in range(nb)},
        compiler_params=pltpu.CompilerParams(has_side_effects=EFFECT),
    )(*bufs, send_sems, recv_sems, after)
    return (res[:ns], res[ns:]) if with_srcs else res[ns:]


def _cast_into_full(w3, layer, ax, chip, *, after=None, name):
    _, r, wd = w3.shape
    tr = _rows_per_tile(r, wd, 4 << 20)
    nt = r // tr
    full_shape = (r, wd * N_CHIPS) if ax == 1 else (r * N_CHIPS, wd)
    out_map = (lambda i, ch: (i, ch[0])) if ax == 1 else (lambda i, ch: (ch[0] * nt + i, 0))
    zero = jnp.zeros((1, wd), F32) + (0.0 if after is None else after)

    def body(chip_ref, w_ref, z_ref, o_ref):
        o_ref[...] = (w_ref[...] + z_ref[...]).astype(o_ref.dtype)

    return pl.pallas_call(
        body, name=name,
        grid_spec=pltpu.PrefetchScalarGridSpec(
            num_scalar_prefetch=1, grid=(nt,),
            in_specs=[pl.BlockSpec((None, tr, wd), lambda i, ch: (layer, i, 0)), pl.BlockSpec((1, wd), lambda i, ch: (0, 0))],
            out_specs=pl.BlockSpec((tr, wd), out_map)),
        out_shape=jax.ShapeDtypeStruct(full_shape, BF16),
        compiler_params=_params(("parallel",)),
    )(jnp.reshape(chip, (1,)).astype(jnp.int32), w3, zero)


def _forward_halves(fulls, axes, *, name):
    n = len(fulls)

    def body(*refs):
        bufs = refs[:n]
        send_sems, recv_sems = refs[2 * n:]
        x, y, c = _mesh_pos()
        sends = []
        for i in range(n):
            for k in range(1, N_CHIPS):
                landed = _cut(bufs[i], axes[i], chip=2 * (x ^ (k >> 1)) + (y ^ (k & 1)), half=c)
                cp = pltpu.make_async_remote_copy(
                    src_ref=landed, dst_ref=landed, send_sem=send_sems.at[i, k - 1], recv_sem=recv_sems.at[i, k - 1],
                    device_id=(x, y, 1 - c), device_id_type=MESH)
                cp.start()
                sends.append(cp)
        for i in range(n):
            for k in range(1, N_CHIPS):
                other = _cut(bufs[i], axes[i], chip=2 * (x ^ (k >> 1)) + (y ^ (k & 1)), half=1 - c)
                _wait_recv(other, send_sems.at[i, k - 1], recv_sems.at[i, k - 1])
        for cp in sends:
            cp.wait_send()

    pairs = pltpu.SemaphoreType.DMA((n, N_CHIPS - 1))
    return pl.pallas_call(
        body, name=name, in_specs=[ANY] * n, out_specs=[ANY] * n,
        out_shape=[jax.ShapeDtypeStruct(a.shape, a.dtype) for a in fulls],
        input_output_aliases={j: j for j in range(n)},
        scratch_shapes=[pairs, pairs],
    )(*fulls)


def _share_halves_in_place(bufs, axes, *, name):
    n = len(bufs)

    def body(*refs):
        ins = refs[:n]
        send_sems, recv_sems = refs[2 * n:]
        x, y, c = _mesh_pos()
        sends = []
        for i in range(n):
            mine = _cut(ins[i], axes[i], half=c)
            cp = pltpu.make_async_remote_copy(
                src_ref=mine, dst_ref=mine, send_sem=send_sems.at[i], recv_sem=recv_sems.at[i],
                device_id=(x, y, 1 - c), device_id_type=MESH)
            cp.start()
            sends.append(cp)
        for i in range(n):
            _wait_recv(_cut(ins[i], axes[i], half=1 - c), send_sems.at[i], recv_sems.at[i])
        for cp in sends:
            cp.wait_send()

    sems = pltpu.SemaphoreType.DMA((n,))
    return pl.pallas_call(
        body, name=name, in_specs=[ANY] * n, out_specs=[ANY] * n,
        out_shape=[jax.ShapeDtypeStruct(b.shape, b.dtype) for b in bufs],
        input_output_aliases={j: j for j in range(n)}, scratch_shapes=[sems, sems],
    )(*bufs)


def _adamw_math(w, g, m, v):
    m = ADAM_B1 * m + (1.0 - ADAM_B1) * g
    v = ADAM_B2 * v + (1.0 - ADAM_B2) * (g * g)
    m_hat = m / (1.0 - ADAM_B1 ** ADAM_STEP)
    v_hat = v / (1.0 - ADAM_B2 ** ADAM_STEP)
    delta = -ADAM_LR * (m_hat / (jnp.sqrt(v_hat) + ADAM_EPS) + ADAM_WD * w)
    return delta, m, v


def _adamw_layer(w3, m3, v3, p, q, layer, prev, *, name):
    nl, rows, width = w3.shape
    tr = _rows_per_tile(rows, width)

    def fn(*t):
        if q is None:
            w, m, v, g = t
        else:
            w, m, v, g, g2 = t
            g = g + g2
        delta, m, v = _adamw_math(w, g, m, v)
        return g, delta, m, v

    ins = [('t', w3, 0, width, layer), ('t', m3, 0, width, layer), ('t', v3, 0, width, layer), ('t', p, 0, width)]
    if q is not None:
        ins.append(('t', q, 0, width))
    outs = [('t', width, F32, layer, nl)] * 4
    aliases = None if prev is None else [(prev[i], i) for i in range(4)]
    return _ew(fn, ins, outs, rows=rows, tr=tr, name=name, aliases=aliases)


def _pack_rows(vec):
    n = vec.shape[0]
    r = -(-n // (8 * LANES)) * 8
    return jnp.pad(vec, (0, r * LANES - n)).reshape(r, LANES)


def kernel(x, c, ctx, c_ctx, ada_w, ada_b, norm_g, w_in, na_rpb, ret_decay_logit, w_proj_na, w_proj_ret, w_out, final_g, loss_target, m_c_ctx, m_ada_w, m_ada_b, m_norm_g, m_w_in, m_na_rpb, m_ret_decay_logit, m_w_proj_na, m_w_proj_ret, m_w_out, m_final_g, v_c_ctx, v_ada_w, v_ada_b, v_norm_g, v_w_in, v_na_rpb, v_ret_decay_logit, v_w_proj_na, v_w_proj_ret, v_w_out, v_final_g):
    depth = w_in.shape[0]
    s_len, d_model = x.shape[1], x.shape[2]
    l_len = ctx.shape[1]
    t_len = s_len + l_len
    na_heads = na_rpb.shape[1]
    ret_heads = ret_decay_logit.shape[2]
    w_na = na_heads * NA_HEAD_DIM
    w_qk = ret_heads * RET_KEY_DIM
    w_v = ret_heads * RET_VAL_DIM
    in_cols = w_in.shape[2] * N_CHIPS
    assert in_cols == 4 * w_na + 2 * w_qk + 2 * w_v + 2 * d_model
    assert x.shape[0] == 1 and s_len % (NA_WIN_ROWS * GRID_W) == 0 and l_len % RET_CHUNK == 0
    off = np.cumsum([0, w_na, w_na, w_na, w_na, w_qk, w_qk, w_v, w_v, d_model, d_model])
    o_naz, o_retq, o_retz, o_gna, o_gret = int(off[3]), int(off[4]), int(off[7]), int(off[8]), int(off[9])
    rows = s_len // GRID_W
    tr = _tile(l_len, 256, 8)
    n0 = s_len // tr
    mod_cols = 3 * d_model
    mod_shard = ada_w.shape[2]

    xi, yi, ci = _mesh_pos()
    me = 4 * xi + 2 * yi + ci
    chip = 2 * xi + yi

    big_axes = [1, 1, 0, 0]
    n_big = len(big_axes) * (N_CHIPS - 1)
    gather_plan, scatter_plan = _gather_plan(big_axes), _scatter_plan(big_axes)

    c_silu = c[0] * _sigmoid(c[0])
    cc_silu = c_ctx * _sigmoid(c_ctx)
    c_all = _all_gather_small(_pack_rows(c_silu), name="gather_c")[:, :d_model // LANES].reshape(N_DEV, d_model)
    a_rows = jnp.concatenate([c_all, cc_silu[None], jnp.zeros((16 - N_DEV - 1, d_model), F32)], axis=0)
    mod_part = jnp.stack([_mm(a_rows, ada_w, b_lead=l, out_dtype=F32, name="ada_fwd_%d" % l) for l in range(depth)])
    mod_all = _all_gather_small(_pack_rows(mod_part.reshape(-1)), name="gather_mod")
    n_mod = depth * 16 * mod_shard
    mod_all = mod_all.reshape(N_DEV, -1)[:, :n_mod].reshape(N_CHIPS, 2, depth, 16, mod_shard)[:, 0]
    mod_all = jnp.transpose(mod_all, (1, 2, 0, 3)).reshape(depth, 16, mod_cols) + ada_b[:, None, :]

    big_named = list(zip((w_in, w_proj_na, w_proj_ret, w_out), big_axes, ("w_in", "w_proj_na", "w_proj_ret", "w_out")))
    w_in0 = _cast_into_full(w_in, 0, big_axes[0], chip, name="cast_w_in_0")
    mod_all, w_in0 = lax.optimization_barrier((mod_all, w_in0))
    plan_near, plan_far, plan_rest = _gather_near_plan(big_axes[:1]), _gather_far_plan(big_axes[:1]), _gather_plan(big_axes[1:])
    near_all, far_all = _gather_near_plan(big_axes), _gather_far_plan(big_axes)
    first_gather = _split_start([], [w_in0], plan_near, 2, name="gather_start_0_in")
    start_token = first_gather[4][0, 0]
    fulls = [[None if (l == 0 and tag == "w_in") else _cast_into_full(w, l, ax, chip, after=start_token, name="cast_%s_%d" % (tag, l))
              for w, ax, tag in big_named] for l in range(depth)]
    mod_lat = lax.dynamic_index_in_dim(mod_all, me, axis=1, keepdims=False)
    mod_ctx = mod_all[:, N_DEV]

    c2, s2 = _rope_tables(s_len, l_len)
    log_gamma = jax.nn.log_sigmoid(ret_decay_logit)
    x_all = jnp.concatenate([x[0], ctx[0]], axis=0)

    def grp(lat_vec, ctx_vec):
        return jnp.stack([lat_vec, ctx_vec])[:, None, :]

    saved, full_w = [], []
    for l in range(depth):
        shift, scale, gate = [grp(mod_lat[l, i * d_model:(i + 1) * d_model], mod_ctx[l, i * d_model:(i + 1) * d_model])
                              for i in range(3)]
        gs = norm_g[l][None, None, :] * (1.0 + scale) + start_token

        def modnorm(xt, gs_t, sh_t):
            r = lax.rsqrt(jnp.mean(xt * xt, axis=-1, keepdims=True) + NORM_EPS)
            return xt * r * gs_t + sh_t

        h, = _ew(modnorm, [('t', x_all, 0, d_model), ('g', gs), ('g', shift)], [('t', d_model, BF16)],
                 rows=t_len, tr=tr, n0=n0, name="modnorm_%d" % l)
        bias = _na_bias_table(na_rpb[l], rows, name="na_bias_%d" % l)
        if l == 0:
            h, bias, fulls = lax.optimization_barrier((h, bias, fulls))
            landed_near = _split_wait(first_gather, h, plan_near, name="gather_wait_0_in")
            passing = _split_start([], landed_near, plan_far, 1, name="gather_pass_0_in")
            landed_in = _split_wait(passing, passing[4], plan_far, name="gather_wait_0_in_far")
            landed_in, rest0, later = lax.optimization_barrier((landed_in, fulls[0][1:], fulls[1:]))
            rest_gather = _split_start([], rest0, plan_rest, n_big - (N_CHIPS - 1), name="gather_start_0_rest")
            later_gathers = [_split_start([], later[j], near_all, 2 * len(big_axes), name="gather_start_%d" % (j + 1))
                             for j in range(depth - 1)]
            win_f, = _forward_halves(landed_in, big_axes[:1], name="gather_forward_0_in")
            win_f, tokens = lax.optimization_barrier((win_f, [rest_gather[4]] + [g[4] for g in later_gathers]))
            gate = gate + sum(t[0, 0] for t in tokens)
        else:
            h, bias = lax.optimization_barrier((h, bias))
            landed = _split_wait(later_passes[l - 1], h, far_all, name="gather_wait_%d" % l)
            win_f, wpn_f, wpr_f, wout_f = _forward_halves(landed, big_axes, name="gather_forward_%d" % l)
        u = _mm(h, win_f, tm=1152, tn=1024, name="in_proj_%d" % l)
        o_na = _na_fwd(u, bias, s_len=s_len, heads=na_heads, name="na_fwd_%d" % l)
        o_ret, states = _ret_fwd(u, c2, s2, log_gamma[l], s_len=s_len, heads=ret_heads, q_off=o_retq, name="ret_fwd_%d" % l)

        def act(o1, z1, o2, z2):
            a1 = o1.astype(F32) * _silu_parts(z1.astype(F32))[0]
            sz = _silu_parts(z2.astype(F32))[0]
            outs = []
            for hh in range(ret_heads):
                sl = slice(hh * RET_VAL_DIM, (hh + 1) * RET_VAL_DIM)
                oh = o2[:, sl]
                r = lax.rsqrt(jnp.mean(oh * oh, axis=-1, keepdims=True) + NORM_EPS)
                outs.append(oh * r * sz[:, sl])
            return a1, jnp.concatenate(outs, axis=-1)

        a_na, a_ret = _ew(act, [('t', o_na, 0, w_na), ('t', u, o_naz // w_na, w_na), ('t', o_ret, 0, w_v), ('t', u, o_retz // w_v, w_v)],
                          [('t', w_na, BF16), ('t', w_v, BF16)], rows=t_len, tr=tr, name="act_%d" % l)
        if l == 0:
            landed_rest = _split_wait(rest_gather, a_na, plan_rest, name="gather_wait_0_rest")
            later_passes = [_split_start([], _split_wait(later_gathers[j], a_na, near_all, name="gather_near_%d" % (j + 1)),
                                         far_all, len(big_axes), name="gather_pass_%d" % (j + 1)) for j in range(depth - 1)]
            landed_rest, tokens = lax.optimization_barrier((landed_rest, [g[4] for g in later_passes]))
            gate = gate + sum(t[0, 0] for t in tokens)
            wpn_f, wpr_f, wout_f = _forward_halves(landed_rest, big_axes[1:], name="gather_forward_0_rest")
        full_w.append((win_f, wpn_f, wpr_f, wout_f))
        y_na = _mm(a_na, wpn_f, name="proj_na_%d" % l)
        y_ret = _mm(a_ret, wpr_f, name="proj_ret_%d" % l)

        def merge(y1, y2, g1, g2):
            return _sigmoid(g1.astype(F32)) * y1.astype(F32) + _sigmoid(g2.astype(F32)) * y2.astype(F32)

        merged, = _ew(merge, [('t', y_na, 0, d_model), ('t', y_ret, 0, d_model), ('t', u, o_gna // d_model, d_model), ('t', u, o_gret // d_model, d_model)],
                      [('t', d_model, BF16)], rows=t_len, tr=tr, name="merge_%d" % l)
        out = _mm(merged, wout_f, out_dtype=F32, name="out_proj_%d" % l)
        x_new, = _ew(lambda xt, ot, gt: xt + gt * ot, [('t', x_all, 0, d_model), ('t', out, 0, d_model), ('g', gate)],
                     [('t', d_model, F32)], rows=t_len, tr=tr, n0=n0, name="resid_%d" % l)
        saved.append(dict(x=x_all, h=h, u=u, bias=bias, o_na=o_na, o_ret=o_ret, states=states, a_na=a_na, a_ret=a_ret,
                          y_na=y_na, y_ret=y_ret, merged=merged, out=out, gate=gate, gs=gs, scale=scale))
        x_all = x_new

    def final(xt, tt, gt):
        r = lax.rsqrt(jnp.mean(xt * xt, axis=-1, keepdims=True) + NORM_EPS)
        xh = xt * r
        e = xh * gt - tt
        dy = e * (1.0 / d_model)
        dyg = dy * gt
        dx = r * (dyg - xh * jnp.mean(dyg * xh, axis=-1, keepdims=True))
        return dx, _rsum(dy * xh), _rsum(e * e)

    dx_lat, d_final_g, loss_cols = _ew(final, [('t', x_all, 0, d_model), ('t', loss_target[0], 0, d_model), ('g', final_g[None, None, :])],
                                       [('t', d_model, F32), ('r', d_model, 1), ('r', d_model, 1)], rows=s_len, tr=tr, name="final")
    loss_part = (0.5 / d_model) * jnp.sum(loss_cols)
    dx_all = jnp.concatenate([dx_lat, jnp.zeros((l_len, d_model), F32)], axis=0)

    big_w = [(w_in, m_w_in, v_w_in), (w_proj_na, m_w_proj_na, v_w_proj_na), (w_proj_ret, m_w_proj_ret, v_w_proj_ret), (w_out, m_w_out, v_w_out)]
    big_res = [None] * 4
    scatters = {}
    back_token = jnp.zeros((), F32)

    pairs = {}

    def start_pair(key, grads, axes):
        plan = _pair_plan(axes)
        lands = []
        for g, ax in zip(grads, axes):
            shp = list(g.shape)
            shp[1 - ax] //= 2
            lands.append(lax.empty(tuple(shp), BF16))
        pairs[key] = (_split_start(grads, lands, plan, len(axes), name="pair_start_%s" % key), axes, plan)
        return pairs[key][0][4]

    def start_scatter(key, after):
        started, axes, pair_plan = pairs[key]
        grads, theirs = _split_wait(started, after, pair_plan, name="pair_wait_%s" % key, with_srcs=True)
        plan = _scatter_plan(axes)
        pair = [_sum_pair(g, t, ax, ci, name="sum_pair_%s_%d" % (key, i)) for i, (g, t, ax) in enumerate(zip(grads, theirs, axes))]
        own = [lax.dynamic_slice_in_dim(s, chip * (s.shape[ax] // N_CHIPS), s.shape[ax] // N_CHIPS, axis=ax) for s, ax in zip(pair, axes)]
        lands = [lax.empty((N_CHIPS - 1,) + o.shape, BF16) for o in own]
        started = _split_start(pair, lands, plan, len(axes) * (N_CHIPS - 1), name="scatter_start_%s" % key)
        scatters[key] = (started, own, axes, plan)
        return started[4]

    def finish_scatter(key, after):
        started, own, axes, plan = scatters[key]
        recv = _split_wait(started, after, plan, name="scatter_wait_%s" % key)
        bufs = [_sum_chips_into(own[i], rbuf, axes[i], ci, name="sum_chips_%s_%d" % (key, i)) for i, rbuf in enumerate(recv)]
        return _share_halves_in_place(bufs, axes, name="share_halves_%s" % key)

    def adamw_big(l, idx, grads, big_res):
        for i, g in zip(idx, grads):
            w3, m3, v3 = big_w[i]
            big_res[i] = _adamw_layer(w3, m3, v3, g, None, l, big_res[i], name="adamw_big_%d_%d" % (i, l))
        return big_res

    small = dict(dmod_lat=[None] * depth, dmod_ctx=[None] * depth, dnorm_g=[None] * depth, drpb=[None] * depth, ddecay=[None] * depth)
    for l in reversed(range(depth)):
        sv = saved[l]
        win_f, wpn_f, wpr_f, wout_f = full_w[l]

        def resid_bwd(dxt, ot, gt):
            return gt * dxt, _rsum(dxt * ot)

        dout, dgate = _ew(resid_bwd, [('t', dx_all, 0, d_model), ('t', sv['out'], 0, d_model), ('g', sv['gate'] + back_token)],
                          [('t', d_model, BF16), ('r', d_model, 2)], rows=t_len, tr=tr, n0=n0, name="resid_bwd_%d" % l)
        dmerged = _mm(dout, wout_f, tb=True, name="out_proj_dx_%d" % l)
        g_wout = _mm(sv['merged'], dout, ta=True, tm=1024, tk=t_len, name="out_proj_dw_%d" % l)

        def merge_bwd(dm, y1, y2, g1, g2):
            dm = dm.astype(F32)
            s1, s2_ = _sigmoid(g1.astype(F32)), _sigmoid(g2.astype(F32))
            return dm * s1, dm * s2_, dm * y1.astype(F32) * s1 * (1.0 - s1), dm * y2.astype(F32) * s2_ * (1.0 - s2_)

        u = sv['u']
        dy_na, dy_ret, dg_na, dg_ret = _ew(
            merge_bwd, [('t', dmerged, 0, d_model), ('t', sv['y_na'], 0, d_model), ('t', sv['y_ret'], 0, d_model),
                        ('t', u, o_gna // d_model, d_model), ('t', u, o_gret // d_model, d_model)],
            [('t', d_model, BF16)] * 4, rows=t_len, tr=tr, name="merge_bwd_%d" % l)
        da_na = _mm(dy_na, wpn_f, tb=True, name="proj_na_dx_%d" % l)
        g_wpn = _mm(sv['a_na'], dy_na, ta=True, tm=1024, tk=t_len, name="proj_na_dw_%d" % l)
        da_ret = _mm(dy_ret, wpr_f, tb=True, name="proj_ret_dx_%d" % l)
        g_wpr = _mm(sv['a_ret'], dy_ret, ta=True, tm=1024, tk=t_len, name="proj_ret_dw_%d" % l)
        lg_l = log_gamma[l]
        if l == 0:
            pair_token = start_pair("0_rest", [g_wpn, g_wpr, g_wout], big_axes[1:])

        def act_bwd(da1, o1, z1, da2, o2, z2):
            da1, da2 = da1.astype(F32), da2.astype(F32)
            si1, ds1 = _silu_parts(z1.astype(F32))
            si2, ds2 = _silu_parts(z2.astype(F32))
            do1 = da1 * si1
            dz1 = da1 * o1.astype(F32) * ds1
            dn = da2 * si2
            do2, dz2 = [], []
            for hh in range(ret_heads):
                sl = slice(hh * RET_VAL_DIM, (hh + 1) * RET_VAL_DIM)
                oh = o2[:, sl]
                r = lax.rsqrt(jnp.mean(oh * oh, axis=-1, keepdims=True) + NORM_EPS)
                nh = oh * r
                dz2.append(da2[:, sl] * nh * ds2[:, sl])
                do2.append(r * (dn[:, sl] - nh * jnp.mean(dn[:, sl] * nh, axis=-1, keepdims=True)))
            return do1, dz1, jnp.concatenate(do2, axis=-1), jnp.concatenate(dz2, axis=-1)

        do_na, dz_na, do_ret, dz_ret = _ew(
            act_bwd, [('t', da_na, 0, w_na), ('t', sv['o_na'], 0, w_na), ('t', u, o_naz // w_na, w_na),
                      ('t', da_ret, 0, w_v), ('t', sv['o_ret'], 0, w_v), ('t', u, o_retz // w_v, w_v)],
            [('t', w_na, BF16), ('t', w_na, BF16), ('t', w_v, BF16), ('t', w_v, BF16)], rows=t_len, tr=tr, name="act_bwd_%d" % l)
        dq_na, dk_na, dv_na, dbias = _na_bwd(u, sv['bias'], sv['o_na'], do_na, s_len=s_len, heads=na_heads, name="na_bwd_%d" % l)
        small['drpb'][l] = _rpb_grad(dbias, name="rpb_grad_%d" % l)
        if l == 0:
            lg_l = lg_l + start_scatter("0_rest", dq_na)[0, 0] + pair_token[0, 0]
        dq_r, dk_r, dv_r, dlg = _ret_bwd(u, c2, s2, lg_l, sv['states'], do_ret, s_len=s_len, heads=ret_heads,
                                         q_off=o_retq, name="ret_bwd_%d" % l)
        small['ddecay'][l] = jnp.transpose(dlg[:, :, 0, 0]) * _sigmoid(-ret_decay_logit[l])
        du_parts = [dq_na, dk_na, dv_na, dz_na, dq_r, dk_r, dv_r, dz_ret, dg_na, dg_ret]
        du, = _ew(lambda *t: jnp.concatenate(t, axis=-1), [('t', p, 0, p.shape[1]) for p in du_parts], [('t', in_cols, BF16)],
                  rows=t_len, tr=tr, name="du_concat_%d" % l)
        g_win = _mm(sv['h'], du, ta=True, tm=1024, tn=1024, tk=t_len, name="in_proj_dw_%d" % l)
        if l > 0:
            du, pair_token = lax.optimization_barrier((du, start_pair("%d_all" % l, [g_win, g_wpn, g_wpr, g_wout], big_axes)))
        else:
            du, in_token = lax.optimization_barrier((du, start_pair("0_in", [g_win], big_axes[:1])))
        dh = _mm(du, win_f, tb=True, out_dtype=F32, tm=1152, tn=1024, name="in_proj_dx_%d" % l)

        def modnorm_bwd(xt, dht, dxt, gs_t):
            r = lax.rsqrt(jnp.mean(xt * xt, axis=-1, keepdims=True) + NORM_EPS)
            xh = xt * r
            dhg = dht * gs_t
            dx = r * (dhg - xh * jnp.mean(dhg * xh, axis=-1, keepdims=True)) + dxt
            return dx, _rsum(dht), _rsum(dht * xh)

        dx_all, dshift, dgs = _ew(modnorm_bwd, [('t', sv['x'], 0, d_model), ('t', dh, 0, d_model), ('t', dx_all, 0, d_model), ('g', sv['gs'])],
                                  [('t', d_model, F32), ('r', d_model, 2), ('r', d_model, 2)], rows=t_len, tr=tr, n0=n0, name="modnorm_bwd_%d" % l)
        dscale = dgs * norm_g[l][None, None, :]
        small['dnorm_g'][l] = jnp.sum(dgs * (1.0 + sv['scale']), axis=(0, 1))
        dmod = jnp.concatenate([dshift, dscale, dgate], axis=-1)[:, 0]
        small['dmod_lat'][l], small['dmod_ctx'][l] = dmod[0], dmod[1]

        if l > 0:
            back_token = start_scatter("%d_all" % l, dx_all)[0, 0] + pair_token[0, 0]

    grad_x = dx_all[:s_len][None]

    drpb = jnp.stack(small['drpb']).reshape(-1)
    ddecay = jnp.stack(small['ddecay']).reshape(-1)
    pieces = [jnp.stack(small['dmod_lat']).reshape(-1), jnp.stack(small['dmod_ctx']).reshape(-1),
              jnp.stack(small['dnorm_g']).reshape(-1), d_final_g.reshape(-1), drpb, ddecay, loss_part[None]]
    sizes = [int(p.shape[0]) for p in pieces]
    pads = [-(-s // LANES) * LANES for s in sizes]
    packed = jnp.concatenate([jnp.pad(p, (0, pd - s)) for p, s, pd in zip(pieces, sizes, pads)])
    gathered = _all_gather_small(_pack_rows(packed), name="gather_small_grads")
    r_small = gathered.shape[1]

    def sum8(*t):
        acc = t[0]
        for other in t[1:]:
            acc = acc + other
        return acc

    total, = _ew(sum8, [('t', gathered, 0, LANES, k) for k in range(N_DEV)], [('t', LANES, F32)], rows=r_small, tr=r_small, name="sum_devices")
    total = total.reshape(-1)
    starts = np.cumsum([0] + pads)
    g_mod_lat_sum, g_mod_ctx, g_norm_g, g_final_g, g_rpb, g_decay, loss = [total[starts[i]:starts[i] + sizes[i]] for i in range(len(pieces))]
    loss = loss[0]
    g_ada_b = (g_mod_lat_sum + g_mod_ctx).reshape(depth, mod_cols)
    g_mod_ctx = g_mod_ctx.reshape(depth, mod_cols)
    dmod_lat_all = gathered.reshape(N_DEV, -1)[:, :depth * mod_cols].reshape(N_DEV, depth, mod_cols)

    dcc_part = jnp.zeros((16, d_model), F32)
    ctx_cols = [lax.dynamic_slice_in_dim(g_mod_ctx[l], chip * mod_shard, mod_shard, axis=0) for l in range(depth)]
    for l in reversed(range(depth)):
        c_rows = jnp.concatenate([ctx_cols[l][None], jnp.zeros((15, mod_shard), F32)], axis=0)
        dcc_part = dcc_part + _mm(c_rows, ada_w, tb=True, b_lead=l, out_dtype=F32, name="ada_dc_%d" % l)
    dcc_all = _all_gather_small(_pack_rows(dcc_part[0]), name="gather_dcc")[:, :d_model // LANES].reshape(N_CHIPS, 2, d_model)[:, 0]

    tail_token = start_scatter("0_in", dcc_all) + in_token
    dcc = ((dcc_all[0] + dcc_all[1]) + dcc_all[2]) + dcc_all[3]
    sg = _sigmoid(c_ctx)
    g_c_ctx = dcc * (sg * (1.0 + c_ctx * (1.0 - sg)))
    for l in reversed(range(1, depth)):
        big_res = adamw_big(l, range(4), finish_scatter("%d_all" % l, tail_token), big_res)

    ada_res = None
    for l in reversed(range(depth)):
        lat_cols = lax.dynamic_slice_in_dim(dmod_lat_all[:, l], chip * mod_shard, mod_shard, axis=1)
        d_rows = jnp.concatenate([lat_cols, ctx_cols[l][None], jnp.zeros((16 - N_DEV - 1, mod_shard), F32)], axis=0) + tail_token[0, 0]
        g_ada = _mm(a_rows, d_rows, ta=True, out_dtype=F32, tm=512, name="ada_dw_%d" % l)
        ada_res = _adamw_layer(ada_w, m_ada_w, v_ada_w, g_ada, None, l, ada_res, name="adamw_ada_%d" % l)

    small_w = [(c_ctx, m_c_ctx, v_c_ctx, g_c_ctx), (ada_b, m_ada_b, v_ada_b, g_ada_b),
               (norm_g, m_norm_g, v_norm_g, g_norm_g), (na_rpb, m_na_rpb, v_na_rpb, g_rpb),
               (ret_decay_logit, m_ret_decay_logit, v_ret_decay_logit, g_decay), (final_g, m_final_g, v_final_g, g_final_g)]
    sw_sizes = [int(np.prod(t[0].shape)) for t in small_w]
    sw_pads = [-(-s // LANES) * LANES for s in sw_sizes]

    def pack(j):
        return _pack_rows(jnp.concatenate([jnp.pad(t[j].reshape(-1), (0, pd - s)) for t, s, pd in zip(small_w, sw_sizes, sw_pads)]))

    pw_, pm_, pv_, pg_ = pack(0), pack(1), pack(2), pack(3)
    sw_out = _ew(lambda w, m, v, g: (g,) + _adamw_math(w, g, m, v),
                 [('t', pw_, 0, LANES), ('t', pm_, 0, LANES), ('t', pv_, 0, LANES), ('t', pg_, 0, LANES)],
                 [('t', LANES, F32)] * 4, rows=pw_.shape[0], tr=pw_.shape[0], name="adamw_small")
    sw_starts = np.cumsum([0] + sw_pads)
    sw_out, ada_res, big_res = lax.optimization_barrier((sw_out, ada_res, big_res))
    big_res = adamw_big(0, range(1, 4), finish_scatter("0_rest", sw_out[0]), big_res)
    big_res = adamw_big(0, range(1), finish_scatter("0_in", sw_out[1]), big_res)

    def unpack(arr, i):
        return arr.reshape(-1)[sw_starts[i]:sw_starts[i] + sw_sizes[i]].reshape(small_w[i][0].shape)

    sm = [[unpack(sw_out[j], i) for i in range(len(small_w))] for j in range(4)]
    def ordered(j):
        return [sm[j][0], ada_res[j], sm[j][1], sm[j][2], big_res[0][j], sm[j][3], sm[j][4],
                big_res[1][j], big_res[2][j], big_res[3][j], sm[j][5]]

    return (loss, grad_x, *ordered(0), *ordered(1), *ordered(2), *ordered(3))
```

```python
import functools
import math

import numpy as np
import jax
import jax.numpy as jnp
from jax import lax
from jax.experimental import pallas as pl
from jax.experimental.pallas import tpu as pltpu

GRID_W = 64
NA_HEAD_DIM = 128
NA_WIN_ROWS = 8
NA_WIN_COLS = 16
NA_GROUP = 8
RET_GROUPS = (1, 2, 3)
RET_KEY_DIM = 128
RET_VAL_DIM = 256
RET_CHUNK = 128
ROPE_BASE = 10000.0
NORM_EPS = 1e-6
MASK_VALUE = -1e30
ADAM_LR = 0.001
ADAM_B1 = 0.9
ADAM_B2 = 0.999
ADAM_EPS = 1e-08
ADAM_WD = 0.01
ADAM_STEP = 10

N_CHIPS = 4
N_DEV = 8
LANES = 128
VMEM_LIMIT = 56 * 1024 * 1024
BF16 = jnp.bfloat16
F32 = jnp.float32
MESH = pl.DeviceIdType.MESH
ANY = pl.BlockSpec(memory_space=pl.ANY)


def _tile(dim, pref, align=LANES):
    if dim <= pref:
        return dim
    t = (pref // align) * align
    while t >= align:
        if dim % t == 0:
            return t
        t -= align
    return dim


def _rows_per_tile(rows, width, tile_bytes=1 << 20):
    return _tile(rows, max(8, tile_bytes // (4 * width)), 8)


def _params(sem):
    return pltpu.CompilerParams(dimension_semantics=sem, vmem_limit_bytes=VMEM_LIMIT)


def _sigmoid(x):
    return 1.0 / (1.0 + jnp.exp(-x))


def _dot(a, b, ca, cb):
    return lax.dot_general(a, b, (((ca,), (cb,)), ((), ())), preferred_element_type=F32)


def _mm(a, b, *, ta=False, tb=False, a_lead=None, b_lead=None, out_dtype=BF16, tm=1152, tn=1024, tk=2048, name):
    ash = a.shape[1:] if a_lead is not None else a.shape
    bsh = b.shape[1:] if b_lead is not None else b.shape
    m, k = (ash[1], ash[0]) if ta else ash
    n, k2 = bsh if tb else (bsh[1], bsh[0])
    assert k == k2, (name, ash, bsh)
    tm, tn, tk = _tile(m, tm), _tile(n, tn), _tile(k, tk)
    nk = k // tk

    def lead(spec_shape, imap, l):
        if l is None:
            return pl.BlockSpec(spec_shape, imap)
        return pl.BlockSpec((None,) + spec_shape, lambda i, j, kk: (l,) + imap(i, j, kk))

    a_spec = lead((tk, tm), lambda i, j, kk: (kk, i), a_lead) if ta else lead((tm, tk), lambda i, j, kk: (i, kk), a_lead)
    b_spec = lead((tn, tk), lambda i, j, kk: (j, kk), b_lead) if tb else lead((tk, tn), lambda i, j, kk: (kk, j), b_lead)
    ca, cb = (0 if ta else 1), (1 if tb else 0)

    def body(a_ref, b_ref, o_ref, *scratch):
        part = _dot(a_ref[...].astype(BF16), b_ref[...].astype(BF16), ca, cb)
        if nk == 1:
            o_ref[...] = part.astype(o_ref.dtype)
            return
        acc_ref, = scratch
        kk = pl.program_id(2)

        @pl.when(kk == 0)
        def _():
            acc_ref[...] = part

        @pl.when(kk > 0)
        def _():
            acc_ref[...] += part

        @pl.when(kk == nk - 1)
        def _():
            o_ref[...] = acc_ref[...].astype(o_ref.dtype)

    return pl.pallas_call(
        body, name=name, grid=(m // tm, n // tn, nk),
        in_specs=[a_spec, b_spec],
        out_specs=pl.BlockSpec((tm, tn), lambda i, j, kk: (i, j)),
        out_shape=jax.ShapeDtypeStruct((m, n), out_dtype),
        scratch_shapes=[] if nk == 1 else [pltpu.VMEM((tm, tn), F32)],
        compiler_params=_params(("parallel", "parallel", "arbitrary")),
    )(a, b)


def _ew(fn, ins, outs, *, rows, tr, name, n0=None, aliases=None):
    assert rows % tr == 0, (name, rows, tr)
    nt = rows // tr

    def grp(i):
        return 0 if n0 is None else jnp.where(i < n0, 0, 1)

    in_specs, args = [], []
    for spec in ins:
        if spec[0] == 't':
            arr, cb, w = spec[1], spec[2], spec[3]
            l = spec[4] if len(spec) > 4 else None
            if l is None:
                in_specs.append(pl.BlockSpec((tr, w), functools.partial(lambda i, cb: (i, cb), cb=cb)))
            else:
                in_specs.append(pl.BlockSpec((None, tr, w), functools.partial(lambda i, cb, l: (l, i, cb), cb=cb, l=l)))
            args.append(arr)
        else:
            arr = spec[1]
            g = arr.shape[0]
            if g == 1:
                in_specs.append(pl.BlockSpec((None, 1, arr.shape[2]), lambda i: (0, 0, 0)))
            else:
                in_specs.append(pl.BlockSpec((None, 1, arr.shape[2]), lambda i: (grp(i), 0, 0)))
            args.append(arr)
    out_specs, out_shapes, is_red = [], [], []
    for spec in outs:
        if spec[0] == 't':
            w, dt = spec[1], spec[2]
            if len(spec) > 3:
                l, nl = spec[3], spec[4]
                out_specs.append(pl.BlockSpec((None, tr, w), functools.partial(lambda i, l: (l, i, 0), l=l)))
                out_shapes.append(jax.ShapeDtypeStruct((nl, rows, w), dt))
            else:
                out_specs.append(pl.BlockSpec((tr, w), lambda i: (i, 0)))
                out_shapes.append(jax.ShapeDtypeStruct((rows, w), dt))
            is_red.append(False)
        else:
            w, g = spec[1], spec[2]
            if g == 1:
                out_specs.append(pl.BlockSpec((None, 1, w), lambda i: (0, 0, 0)))
            else:
                out_specs.append(pl.BlockSpec((None, 1, w), lambda i: (grp(i), 0, 0)))
            out_shapes.append(jax.ShapeDtypeStruct((g, 1, w), F32))
            is_red.append(True)
    n_in = len(ins)
    n_alias = 0 if aliases is None else len(aliases)

    def body(*refs):
        in_refs = refs[:n_in]
        out_refs = refs[n_in + n_alias:]
        res = fn(*[r[...] for r in in_refs])
        if not isinstance(res, (tuple, list)):
            res = (res,)
        i = pl.program_id(0)
        first = (i == 0) if n0 is None else ((i == 0) | (i == n0))
        for o_ref, val, red in zip(out_refs, res, is_red):
            if not red:
                o_ref[...] = val.astype(o_ref.dtype)
            else:
                @pl.when(first)
                def _(o_ref=o_ref, val=val):
                    o_ref[...] = val

                @pl.when(jnp.logical_not(first))
                def _(o_ref=o_ref, val=val):
                    o_ref[...] += val

    io_alias = {}
    if aliases is not None:
        for a_idx, (arr, o_idx) in enumerate(aliases):
            in_specs.append(ANY)
            args.append(arr)
            io_alias[n_in + a_idx] = o_idx
    has_red = any(is_red)
    return pl.pallas_call(
        body, name=name, grid=(nt,), in_specs=in_specs, out_specs=out_specs, out_shape=out_shapes,
        input_output_aliases=io_alias,
        compiler_params=_params(("arbitrary",) if has_red else ("parallel",)),
    )(*args)


def _half_spec(tr, width, ax, n_tiles):
    if ax == 1:
        return pl.BlockSpec((tr, width), lambda i, sel: (sel[0] * n_tiles + i, 0))
    return pl.BlockSpec((tr, width), lambda i, sel: (i, sel[0]))


def _sum_pair(g, theirs, ax, ci, *, name):
    pr, pw = theirs.shape
    tr = _rows_per_tile(pr, pw)
    nt = pr // tr

    def body(sel, a_ref, b_ref, o_ref):
        o_ref[...] = (a_ref[...].astype(F32) + b_ref[...].astype(F32)).astype(o_ref.dtype)

    return pl.pallas_call(
        body, name=name,
        grid_spec=pltpu.PrefetchScalarGridSpec(
            num_scalar_prefetch=1, grid=(nt,),
            in_specs=[_half_spec(tr, pw, ax, nt), pl.BlockSpec((tr, pw), lambda i, sel: (i, 0))],
            out_specs=pl.BlockSpec((tr, pw), lambda i, sel: (i, 0))),
        out_shape=jax.ShapeDtypeStruct((pr, pw), BF16),
        compiler_params=_params(("parallel",)),
    )(jnp.reshape(ci, (1,)).astype(jnp.int32), g, theirs)


def _sum_chips_into(own, recv, ax, ci, *, name):
    pr, pw = own.shape
    tr = _rows_per_tile(pr, pw)
    nt = pr // tr
    full_shape = (2 * pr, pw) if ax == 1 else (pr, 2 * pw)

    def body(sel, a_ref, r_ref, o_ref):
        acc = a_ref[...].astype(F32)
        for k in range(N_CHIPS - 1):
            acc = acc + r_ref[k].astype(F32)
        o_ref[...] = acc

    return pl.pallas_call(
        body, name=name,
        grid_spec=pltpu.PrefetchScalarGridSpec(
            num_scalar_prefetch=1, grid=(nt,),
            in_specs=[pl.BlockSpec((tr, pw), lambda i, sel: (i, 0)), pl.BlockSpec((N_CHIPS - 1, tr, pw), lambda i, sel: (0, i, 0))],
            out_specs=_half_spec(tr, pw, ax, nt)),
        out_shape=jax.ShapeDtypeStruct(full_shape, F32),
        compiler_params=_params(("parallel",)),
    )(jnp.reshape(ci, (1,)).astype(jnp.int32), own, recv)


def _rsum(v):
    return jnp.sum(v, axis=0, keepdims=True)


def _silu_parts(z):
    sg = _sigmoid(z)
    return z * sg, sg * (1.0 + z * (1.0 - sg))


def _na_bias_table(rpb, rows, *, name):
    kh, kw = NA_WIN_ROWS, NA_WIN_COLS
    assert rows >= kh
    heads = rpb.shape[0]
    e1, e2 = _na_onehots()
    rpb16 = jnp.pad(rpb, ((0, 0), (0, 16 - rpb.shape[1]), (0, LANES - rpb.shape[2])))

    def body(r_ref, e1_ref, e2_ref, o_ref):
        e1b = e1_ref[...].astype(BF16)
        y = sum(_dot(e1b, part, 0, 0) for part in _split3(r_ref[...]))
        e2b = e2_ref[...].astype(BF16)
        o_ref[...] = sum(_dot(part, e2b, 1, 1) for part in _split3(y))

    z = pl.pallas_call(
        body, name=name, grid=(heads,),
        in_specs=[pl.BlockSpec((None, 16, LANES), lambda h: (h, 0, 0)),
                  pl.BlockSpec(e1.shape, lambda h: (0, 0)), pl.BlockSpec(e2.shape, lambda h: (0, 0))],
        out_specs=pl.BlockSpec((None, kh * kh, GRID_W * GRID_W), lambda h: (h, 0, 0)),
        out_shape=jax.ShapeDtypeStruct((heads, kh * kh, GRID_W * GRID_W), F32),
        compiler_params=_params(("parallel",)),
    )(rpb16, e1, e2)
    return z


def _na_bias_layout(z, after):
    heads = z.shape[0]
    kh, kw = NA_WIN_ROWS, NA_WIN_COLS
    cidx = np.arange(GRID_W)
    c0 = np.clip(cidx - kw // 2, 0, GRID_W - kw)
    col_in = (cidx[None, :] >= c0[:, None]) & (cidx[None, :] < c0[:, None] + kw)
    bias = z.reshape(heads, kh, kh, GRID_W, GRID_W).transpose(0, 1, 3, 2, 4)
    bias = jnp.where(col_in[None, None, :, None, :], bias + after, MASK_VALUE)
    return bias.reshape(heads, kh, GRID_W, kh * GRID_W)


def _na_onehots():
    kh, kw = NA_WIN_ROWS, NA_WIN_COLS
    cidx = np.arange(GRID_W)
    dc = cidx[None, :] - cidx[:, None] + (kw - 1)
    e2 = np.zeros((GRID_W * GRID_W, LANES), np.float32)
    ok = (dc >= 0) & (dc <= 2 * kw - 2)
    cq, ck = np.nonzero(ok)
    e2[cq * GRID_W + ck, dc[cq, ck]] = 1.0
    dr = np.arange(kh)[None, :] - np.arange(kh)[:, None] + (kh - 1)
    e1 = np.zeros((16, kh * kh), np.float32)
    dl, kr = np.nonzero(np.ones_like(dr))
    e1[dr[dl, kr], dl * kh + kr] = 1.0
    return jnp.asarray(e1), jnp.asarray(e2)


def _na_fwd(u, bias, *, s_len, heads, name):
    t_len = u.shape[0]
    rows = s_len // GRID_W
    nloc = NA_WIN_ROWS * GRID_W
    scale = NA_HEAD_DIM ** -0.5
    hd = NA_HEAD_DIM

    def body(q_ref, k_ref, v_ref, b_ref, o_ref):
        kc = k_ref[s_len:t_len, :]
        vc = v_ref[s_len:t_len, :]

        def group(g, carry):
            rs = [g * NA_GROUP + i for i in range(NA_GROUP)]
            r0s = [jnp.clip(r - NA_WIN_ROWS // 2, 0, rows - NA_WIN_ROWS) for r in rs]
            gs_ = pl.multiple_of(g * (NA_GROUP * GRID_W), NA_GROUP * GRID_W)
            kss = [pl.multiple_of(r0 * GRID_W, GRID_W) for r0 in r0s]
            q_all = q_ref[pl.ds(gs_, NA_GROUP * GRID_W), :]
            s_ctx = _dot(q_all, kc, 1, 1) * scale
            s_loc = [_dot(q_all[i * GRID_W:(i + 1) * GRID_W], k_ref[pl.ds(kss[i], nloc), :], 1, 1) * scale + b_ref[rs[i] - r0s[i]]
                     for i in range(NA_GROUP)]
            p_loc, p_ctx, inv = [], [], []
            for i in range(NA_GROUP):
                sc = s_ctx[i * GRID_W:(i + 1) * GRID_W]
                m = jnp.maximum(jnp.max(s_loc[i], axis=-1, keepdims=True), jnp.max(sc, axis=-1, keepdims=True))
                pl_, pc_ = jnp.exp(s_loc[i] - m), jnp.exp(sc - m)
                inv.append(1.0 / (jnp.sum(pl_, axis=-1, keepdims=True) + jnp.sum(pc_, axis=-1, keepdims=True)))
                p_loc.append(pl_.astype(BF16))
                p_ctx.append(pc_.astype(BF16))
            o_ctx = _dot(jnp.concatenate(p_ctx, axis=0), vc, 1, 0)
            o_loc = [_dot(p_loc[i], v_ref[pl.ds(kss[i], nloc), :], 1, 0) for i in range(NA_GROUP)]
            out = jnp.concatenate([(o_loc[i] + o_ctx[i * GRID_W:(i + 1) * GRID_W]) * inv[i] for i in range(NA_GROUP)], axis=0)
            o_ref[pl.ds(gs_, NA_GROUP * GRID_W), :] = out.astype(o_ref.dtype)
            return carry

        lax.fori_loop(0, rows // NA_GROUP, group, 0)
        qc = q_ref[s_len:t_len, :]
        s = _dot(qc, kc, 1, 1) * scale
        p = jnp.exp(s - jnp.max(s, axis=-1, keepdims=True))
        o = _dot(p.astype(BF16), vc, 1, 0) / jnp.sum(p, axis=-1, keepdims=True)
        o_ref[s_len:t_len, :] = o.astype(o_ref.dtype)

    col = lambda off: pl.BlockSpec((t_len, hd), functools.partial(lambda h, off: (0, off + h), off=off))
    return pl.pallas_call(
        body, name=name, grid=(heads,),
        in_specs=[col(0), col(heads), col(2 * heads),
                  pl.BlockSpec((None, NA_WIN_ROWS, GRID_W, nloc), lambda h: (h, 0, 0, 0))],
        out_specs=pl.BlockSpec((t_len, hd), lambda h: (0, h)),
        out_shape=jax.ShapeDtypeStruct((t_len, heads * hd), BF16),
        compiler_params=_params(("parallel",)),
    )(u, u, u, bias)


def _na_bwd(u, bias, o, do, *, s_len, heads, name):
    t_len = u.shape[0]
    rows = s_len // GRID_W
    nloc = NA_WIN_ROWS * GRID_W
    scale = NA_HEAD_DIM ** -0.5
    hd = NA_HEAD_DIM

    def body(q_ref, k_ref, v_ref, b_ref, o_ref, do_ref, dq_ref, dk_ref, dv_ref, db_ref, dk_acc, dv_acc):
        kc = k_ref[s_len:t_len, :]
        vc = v_ref[s_len:t_len, :]
        dk_acc[...] = jnp.zeros_like(dk_acc)
        dv_acc[...] = jnp.zeros_like(dv_acc)
        db_ref[...] = jnp.zeros_like(db_ref)

        def group(g, carry):
            n_g, rw = NA_GROUP, GRID_W
            rs = [g * n_g + i for i in range(n_g)]
            r0s = [jnp.clip(r - NA_WIN_ROWS // 2, 0, rows - NA_WIN_ROWS) for r in rs]
            dls = [r - r0 for r, r0 in zip(rs, r0s)]
            gs_ = pl.ds(pl.multiple_of(g * (n_g * rw), n_g * rw), n_g * rw)
            kss = [pl.ds(pl.multiple_of(r0 * rw, rw), nloc) for r0 in r0s]
            row_of = lambda a, i: a[i * rw:(i + 1) * rw]
            q_all, do_all = q_ref[gs_, :], do_ref[gs_, :]
            dlt_all = jnp.sum(do_all.astype(F32) * o_ref[gs_, :].astype(F32), axis=-1, keepdims=True)
            s_ctx = _dot(q_all, kc, 1, 1) * scale
            dp_ctx = _dot(do_all, vc, 1, 1)
            s_loc = [_dot(row_of(q_all, i), k_ref[kss[i], :], 1, 1) * scale + b_ref[dls[i]] for i in range(n_g)]
            dp_loc = [_dot(row_of(do_all, i), v_ref[kss[i], :], 1, 1) for i in range(n_g)]
            p_loc_b, ds_loc_b, p_ctx_b, ds_ctx_b = [], [], [], []
            for i in range(n_g):
                sc, dlt = row_of(s_ctx, i), row_of(dlt_all, i)
                m = jnp.maximum(jnp.max(s_loc[i], axis=-1, keepdims=True), jnp.max(sc, axis=-1, keepdims=True))
                pl_, pc_ = jnp.exp(s_loc[i] - m), jnp.exp(sc - m)
                inv = 1.0 / (jnp.sum(pl_, axis=-1, keepdims=True) + jnp.sum(pc_, axis=-1, keepdims=True))
                pl_, pc_ = pl_ * inv, pc_ * inv
                ds_l = pl_ * (dp_loc[i] - dlt)
                db_ref[dls[i]] += ds_l
                p_loc_b.append(pl_.astype(BF16))
                ds_loc_b.append(ds_l.astype(BF16))
                p_ctx_b.append(pc_.astype(BF16))
                ds_ctx_b.append((pc_ * (row_of(dp_ctx, i) - dlt)).astype(BF16))
            p_ctx_all, ds_ctx_all = jnp.concatenate(p_ctx_b, axis=0), jnp.concatenate(ds_ctx_b, axis=0)
            dq_ctx = _dot(ds_ctx_all, kc, 1, 0)
            dq_loc = [_dot(ds_loc_b[i], k_ref[kss[i], :], 1, 0) for i in range(n_g)]
            dk_loc = [_dot(ds_loc_b[i], row_of(q_all, i), 0, 0) for i in range(n_g)]
            dv_loc = [_dot(p_loc_b[i], row_of(do_all, i), 0, 0) for i in range(n_g)]
            dk_ctx = _dot(ds_ctx_all, q_all, 0, 0)
            dv_ctx = _dot(p_ctx_all, do_all, 0, 0)
            dq_ref[gs_, :] = ((jnp.concatenate(dq_loc, axis=0) + dq_ctx) * scale).astype(dq_ref.dtype)
            for i in range(n_g):
                dk_acc[kss[i], :] += dk_loc[i] * scale
                dv_acc[kss[i], :] += dv_loc[i]
            dk_acc[s_len:t_len, :] += dk_ctx * scale
            dv_acc[s_len:t_len, :] += dv_ctx
            return carry

        lax.fori_loop(0, rows // NA_GROUP, group, 0)
        qc = q_ref[s_len:t_len, :]
        dout = do_ref[s_len:t_len, :]
        out = o_ref[s_len:t_len, :]
        s = _dot(qc, kc, 1, 1) * scale
        p = jnp.exp(s - jnp.max(s, axis=-1, keepdims=True))
        p = p / jnp.sum(p, axis=-1, keepdims=True)
        dlt = jnp.sum(dout.astype(F32) * out.astype(F32), axis=-1, keepdims=True)
        ds = (p * (_dot(dout, vc, 1, 1) - dlt)).astype(BF16)
        dq_ref[s_len:t_len, :] = (_dot(ds, kc, 1, 0) * scale).astype(dq_ref.dtype)
        dk_acc[s_len:t_len, :] += _dot(ds, qc, 0, 0) * scale
        dv_acc[s_len:t_len, :] += _dot(p.astype(BF16), dout, 0, 0)
        dk_ref[...] = dk_acc[...].astype(dk_ref.dtype)
        dv_ref[...] = dv_acc[...].astype(dv_ref.dtype)

    col = lambda off: pl.BlockSpec((t_len, hd), functools.partial(lambda h, off: (0, off + h), off=off))
    tbl = pl.BlockSpec((None, NA_WIN_ROWS, GRID_W, nloc), lambda h: (h, 0, 0, 0))
    tok = jax.ShapeDtypeStruct((t_len, heads * hd), BF16)
    return pl.pallas_call(
        body, name=name, grid=(heads,),
        in_specs=[col(0), col(heads), col(2 * heads), tbl, col(0), col(0)],
        out_specs=[col(0), col(0), col(0), tbl],
        out_shape=[tok, tok, tok, jax.ShapeDtypeStruct(bias.shape, F32)],
        scratch_shapes=[pltpu.VMEM((t_len, hd), F32), pltpu.VMEM((t_len, hd), F32)],
        compiler_params=_params(("parallel",)),
    )(u, u, u, bias, o, do)


def _split3(x):
    hi = x.astype(BF16)
    r1 = x - hi.astype(F32)
    mid = r1.astype(BF16)
    lo = (r1 - mid.astype(F32)).astype(BF16)
    return hi, mid, lo


def _rpb_grad(dbias, *, name):
    heads = dbias.shape[0]
    kh = NA_WIN_ROWS
    e1, e2 = _na_onehots()
    x = dbias.reshape(heads, kh, GRID_W, kh, GRID_W).transpose(0, 1, 3, 2, 4).reshape(heads, kh * kh, GRID_W * GRID_W)

    def body(x_ref, e1_ref, e2_ref, o_ref):
        e2b = e2_ref[...].astype(BF16)
        y = sum(_dot(part, e2b, 1, 0) for part in _split3(x_ref[...]))
        e1b = e1_ref[...].astype(BF16)
        o_ref[...] = sum(_dot(e1b, part, 1, 0) for part in _split3(y))

    out = pl.pallas_call(
        body, name=name, grid=(heads,),
        in_specs=[pl.BlockSpec((None, kh * kh, GRID_W * GRID_W), lambda h: (h, 0, 0)),
                  pl.BlockSpec(e1.shape, lambda h: (0, 0)), pl.BlockSpec(e2.shape, lambda h: (0, 0))],
        out_specs=pl.BlockSpec((None, 16, LANES), lambda h: (h, 0, 0)),
        out_shape=jax.ShapeDtypeStruct((heads, 16, LANES), F32),
        compiler_params=_params(("parallel",)),
    )(x, e1, e2)
    return out[:, :2 * kh - 1, :2 * NA_WIN_COLS - 1]


def _rope_tables(s_len, l_len):
    nf = RET_KEY_DIM // 4
    t = np.arange(s_len)
    row = (t // GRID_W).astype(np.float32)
    colp = (t % GRID_W).astype(np.float32)
    inv_freq = jnp.asarray(ROPE_BASE, F32) ** (-jnp.arange(nf, dtype=F32) / nf)
    ang = jnp.concatenate([jnp.asarray(row)[:, None] * inv_freq, jnp.asarray(colp)[:, None] * inv_freq], axis=-1)
    cos, sin = jnp.cos(ang), jnp.sin(ang)
    c2 = jnp.concatenate([cos, cos], axis=-1)
    s2 = jnp.concatenate([-sin, sin], axis=-1)
    c2 = jnp.concatenate([c2, jnp.ones((l_len, RET_KEY_DIM), F32)], axis=0)
    s2 = jnp.concatenate([s2, jnp.zeros((l_len, RET_KEY_DIM), F32)], axis=0)
    return c2, s2


def _rope(x, c2, s2):
    return x * c2 + pltpu.roll(x, RET_KEY_DIM // 2, 1) * s2


def _rope_t(d, c2, s2):
    return d * c2 + pltpu.roll(d * s2, RET_KEY_DIM // 2, 1)


def _ret_decays(lg, direction):
    cs = RET_CHUNK
    i_col = lax.broadcasted_iota(jnp.int32, (cs, 1), 0)
    p_col = jnp.where(direction == 0, i_col, cs - 1 - i_col).astype(F32)
    pi = lax.broadcasted_iota(jnp.int32, (cs, cs), 0)
    pj = lax.broadcasted_iota(jnp.int32, (cs, cs), 1)
    diff = jnp.where(direction == 0, pi - pj, pj - pi).astype(F32)
    dm = jnp.where(diff >= 0, jnp.exp(jnp.maximum(diff, 0.0) * lg), 0.0)
    qdec = jnp.exp((p_col + 1.0) * lg)
    kdec = jnp.exp((cs - 1.0 - p_col) * lg)
    cd = jnp.exp(jnp.full((1, 1), cs, F32) * lg)
    return p_col, dm, qdec, kdec, cd


def _ret_chunk_index(t, direction, n_chunks, lat_chunks):
    return jnp.where(direction == 0, lax.rem(t + lat_chunks, n_chunks), n_chunks - 1 - t)


def _ret_fwd(u, c2, s2, lg, *, s_len, heads, q_off, name):
    t_len = u.shape[0]
    cs, dk, dv = RET_CHUNK, RET_KEY_DIM, RET_VAL_DIM
    n_chunks, lat_chunks = t_len // cs, s_len // cs
    k_scale = dk ** -0.5
    qb, kb, vb = q_off // dk, q_off // dk + heads, (q_off + 2 * heads * dk) // dv

    def body(lg_ref, q_ref, k_ref, v_ref, c_ref, s_ref, o_ref, st_ref, qd_s, kv_s):
        h, d = pl.program_id(0), pl.program_id(1)
        _, dm, qdec, kdec, cd = _ret_decays(lg_ref[d, h], d)
        n_g = max(g for g in RET_GROUPS if n_chunks % g == 0)
        rows_of = lambda c: pl.ds(pl.multiple_of(c * cs, cs), cs)

        def local(gi, carry):
            rws = [rows_of(gi * n_g + j) for j in range(n_g)]
            qcs = [_rope(q_ref[r, :].astype(F32), c_ref[r, :], s_ref[r, :]) for r in rws]
            kcs = [_rope(k_ref[r, :].astype(F32), c_ref[r, :], s_ref[r, :]) * k_scale for r in rws]
            vcs = [v_ref[r, :] for r in rws]
            a_raw = [_dot(qcs[j].astype(BF16), kcs[j].astype(BF16), 1, 1) for j in range(n_g)]
            kv = [_dot((kcs[j] * kdec).astype(BF16), vcs[j], 0, 0) for j in range(n_g)]
            inner = [_dot((a_raw[j] * dm).astype(BF16), vcs[j], 1, 0) for j in range(n_g)]
            for j in range(n_g):
                qd_s[rws[j], :] = (qcs[j] * qdec).astype(BF16)
                kv_s[gi * n_g + j] = kv[j]

            @pl.when(d == 0)
            def _():
                for j in range(n_g):
                    o_ref[rws[j], :] = inner[j]

            @pl.when(d == 1)
            def _():
                for j in range(n_g):
                    o_ref[rws[j], :] += inner[j]

            return carry

        lax.fori_loop(0, n_chunks // n_g, local, 0)

        def scan(t, st):
            st_ref[t] = st
            return st * cd + kv_s[_ret_chunk_index(t, d, n_chunks, lat_chunks)]

        lax.fori_loop(0, n_chunks, scan, jnp.zeros((dk, dv), F32))

        def cross(gi, carry):
            ts = [gi * n_g + j for j in range(n_g)]
            rws = [rows_of(_ret_chunk_index(t, d, n_chunks, lat_chunks)) for t in ts]
            outs = [_dot(qd_s[rws[j], :], st_ref[ts[j]].astype(BF16), 1, 0) for j in range(n_g)]
            for j in range(n_g):
                o_ref[rws[j], :] += outs[j]
            return carry

        lax.fori_loop(0, n_chunks // n_g, cross, 0)

    return pl.pallas_call(
        body, name=name, grid=(heads, 2),
        in_specs=[pl.BlockSpec(memory_space=pltpu.SMEM),
                  pl.BlockSpec((t_len, dk), lambda h, d: (0, qb + h)),
                  pl.BlockSpec((t_len, dk), lambda h, d: (0, kb + h)),
                  pl.BlockSpec((t_len, dv), lambda h, d: (0, vb + h)),
                  pl.BlockSpec((t_len, dk), lambda h, d: (0, 0)),
                  pl.BlockSpec((t_len, dk), lambda h, d: (0, 0))],
        out_specs=[pl.BlockSpec((t_len, dv), lambda h, d: (0, h)),
                   pl.BlockSpec((None, None, n_chunks, dk, dv), lambda h, d: (h, d, 0, 0, 0))],
        out_shape=[jax.ShapeDtypeStruct((t_len, heads * dv), F32),
                   jax.ShapeDtypeStruct((heads, 2, n_chunks, dk, dv), F32)],
        scratch_shapes=[pltpu.VMEM((t_len, dk), BF16), pltpu.VMEM((n_chunks, dk, dv), F32)],
        compiler_params=_params(("parallel", "arbitrary")),
    )(lg, u, u, u, c2, s2)


def _ret_bwd(u, c2, s2, lg, states, do, *, s_len, heads, q_off, name):
    t_len = u.shape[0]
    cs, dk, dv = RET_CHUNK, RET_KEY_DIM, RET_VAL_DIM
    n_chunks, lat_chunks = t_len // cs, s_len // cs
    k_scale = dk ** -0.5
    qb, kb, vb = q_off // dk, q_off // dk + heads, (q_off + 2 * heads * dk) // dv

    def body(lg_ref, q_ref, k_ref, v_ref, c_ref, s_ref, st_ref, do_ref, dq_ref, dk_ref, dv_ref, dlg_ref, acc, qdo_s, dst_s):
        h, d = pl.program_id(0), pl.program_id(1)
        p_col, dm, qdec, kdec, cd = _ret_decays(lg_ref[d, h], d)
        acc[...] = jnp.zeros_like(acc)
        n_g = max(g for g in RET_GROUPS[:2] if n_chunks % g == 0)
        rows_of = lambda c: pl.ds(pl.multiple_of(c * cs, cs), cs)
        chunk_of = lambda t: _ret_chunk_index(t, d, n_chunks, lat_chunks)

        def local(gi, carry):
            rws = [rows_of(gi * n_g + j) for j in range(n_g)]
            qds = [(_rope(q_ref[r, :].astype(F32), c_ref[r, :], s_ref[r, :]) * qdec).astype(BF16) for r in rws]
            prods = [_dot(qds[j], do_ref[rws[j], :].astype(BF16), 0, 0) for j in range(n_g)]
            for j in range(n_g):
                qdo_s[gi * n_g + j] = prods[j]
            return carry

        lax.fori_loop(0, n_chunks // n_g, local, 0)

        def scan(i, dst):
            t = n_chunks - 1 - i
            dst_s[t] = dst
            return dst * cd + qdo_s[chunk_of(t)]

        lax.fori_loop(0, n_chunks, scan, jnp.zeros((dk, dv), F32))

        def grads(gi, carry):
            ts = [gi * n_g + j for j in range(n_g)]
            rws = [rows_of(chunk_of(t)) for t in ts]
            ccs, sss = [c_ref[r, :] for r in rws], [s_ref[r, :] for r in rws]
            qcs = [_rope(q_ref[r, :].astype(F32), cc, ss) for r, cc, ss in zip(rws, ccs, sss)]
            kcs = [_rope(k_ref[r, :].astype(F32), cc, ss) * k_scale for r, cc, ss in zip(rws, ccs, sss)]
            vcs = [v_ref[r, :] for r in rws]
            docs = [do_ref[r, :].astype(BF16) for r in rws]
            sts = [st_ref[t] for t in ts]
            dsts = [dst_s[t] for t in ts]
            q16 = [x.astype(BF16) for x in qcs]
            k16 = [x.astype(BF16) for x in kcs]
            dst16 = [x.astype(BF16) for x in dsts]
            rng = range(n_g)
            a_raw = [_dot(q16[j], k16[j], 1, 1) for j in rng]
            da_raw = [_dot(docs[j], vcs[j], 1, 1) for j in rng]
            dq_c = [_dot(docs[j], sts[j].astype(BF16), 1, 1) * qdec for j in rng]
            dv_s = [_dot((kcs[j] * kdec).astype(BF16), dst16[j], 1, 0) for j in rng]
            dk_s = [_dot(vcs[j], dst16[j], 1, 1) * kdec for j in rng]
            a16 = [(a_raw[j] * dm).astype(BF16) for j in rng]
            dam = [(da_raw[j] * dm).astype(BF16) for j in rng]
            dq_i = [_dot(dam[j], k16[j], 1, 0) for j in rng]
            dk_i = [_dot(dam[j], q16[j], 0, 0) for j in rng]
            dv_i = [_dot(a16[j], docs[j], 0, 0) for j in rng]
            for j in rng:
                g = (jnp.sum(qcs[j] * (p_col * dq_i[j] + (p_col + 1.0) * dq_c[j]), axis=-1, keepdims=True)
                     + jnp.sum(kcs[j] * ((cs - 1.0 - p_col) * dk_s[j] - p_col * dk_i[j]), axis=-1, keepdims=True))
                g = (jnp.sum(g, axis=0, keepdims=True)
                     + cs * cd * jnp.sum(jnp.sum(dsts[j] * sts[j], axis=-1, keepdims=True), axis=0, keepdims=True))
                acc[...] += jnp.broadcast_to(g, acc.shape)
            dqs = [_rope_t(dq_i[j] + dq_c[j], ccs[j], sss[j]) for j in rng]
            dks = [_rope_t((dk_i[j] + dk_s[j]) * k_scale, ccs[j], sss[j]) for j in rng]
            dvs = [dv_i[j] + dv_s[j] for j in rng]

            @pl.when(d == 0)
            def _():
                for j in rng:
                    dq_ref[rws[j], :] = dqs[j].astype(dq_ref.dtype)
                    dk_ref[rws[j], :] = dks[j].astype(dk_ref.dtype)
                    dv_ref[rws[j], :] = dvs[j].astype(dv_ref.dtype)

            @pl.when(d == 1)
            def _():
                for j in rng:
                    dq_ref[rws[j], :] = (dq_ref[rws[j], :].astype(F32) + dqs[j]).astype(dq_ref.dtype)
                    dk_ref[rws[j], :] = (dk_ref[rws[j], :].astype(F32) + dks[j]).astype(dk_ref.dtype)
                    dv_ref[rws[j], :] = (dv_ref[rws[j], :].astype(F32) + dvs[j]).astype(dv_ref.dtype)

            return carry

        lax.fori_loop(0, n_chunks // n_g, grads, 0)
        dlg_ref[...] = acc[...]

    return pl.pallas_call(
        body, name=name, grid=(heads, 2),
        in_specs=[pl.BlockSpec(memory_space=pltpu.SMEM),
                  pl.BlockSpec((t_len, dk), lambda h, d: (0, qb + h)),
                  pl.BlockSpec((t_len, dk), lambda h, d: (0, kb + h)),
                  pl.BlockSpec((t_len, dv), lambda h, d: (0, vb + h)),
                  pl.BlockSpec((t_len, dk), lambda h, d: (0, 0)),
                  pl.BlockSpec((t_len, dk), lambda h, d: (0, 0)),
                  pl.BlockSpec((None, None, n_chunks, dk, dv), lambda h, d: (h, d, 0, 0, 0)),
                  pl.BlockSpec((t_len, dv), lambda h, d: (0, h))],
        out_specs=[pl.BlockSpec((t_len, dk), lambda h, d: (0, h)),
                   pl.BlockSpec((t_len, dk), lambda h, d: (0, h)),
                   pl.BlockSpec((t_len, dv), lambda h, d: (0, h)),
                   pl.BlockSpec((None, None, 8, LANES), lambda h, d: (h, d, 0, 0))],
        out_shape=[jax.ShapeDtypeStruct((t_len, heads * dk), BF16),
                   jax.ShapeDtypeStruct((t_len, heads * dk), BF16),
                   jax.ShapeDtypeStruct((t_len, heads * dv), BF16),
                   jax.ShapeDtypeStruct((heads, 2, 8, LANES), F32)],
        scratch_shapes=[pltpu.VMEM((8, LANES), F32), pltpu.VMEM((n_chunks, dk, dv), F32), pltpu.VMEM((n_chunks, dk, dv), F32)],
        compiler_params=_params(("parallel", "arbitrary")),
    )(lg, u, u, u, c2, s2, states, do)


def _mesh_pos():
    return lax.axis_index("x"), lax.axis_index("y"), lax.axis_index("c")


def _all_gather_small(buf, *, name):
    r = buf.shape[0]

    def body(x_ref, o_ref, send_sems, recv_sems, local_sem):
        x, y, c = _mesh_pos()
        me = 4 * x + 2 * y + c
        mine = pltpu.make_async_copy(x_ref, o_ref.at[me], local_sem)
        mine.start()
        copies = []
        for k in range(1, N_DEV):
            px, py, pc = x ^ ((k >> 2) & 1), y ^ ((k >> 1) & 1), c ^ (k & 1)
            cp = pltpu.make_async_remote_copy(
                src_ref=x_ref, dst_ref=o_ref.at[me], send_sem=send_sems.at[k - 1], recv_sem=recv_sems.at[k - 1],
                device_id=(px, py, pc), device_id_type=MESH)
            cp.start()
            copies.append((cp, 4 * px + 2 * py + pc))
        for k, (cp, peer) in enumerate(copies):
            pltpu.make_async_remote_copy(
                src_ref=x_ref, dst_ref=o_ref.at[peer], send_sem=send_sems.at[k], recv_sem=recv_sems.at[k],
                device_id=(x, y, c), device_id_type=MESH).wait_recv()
        for cp, _ in copies:
            cp.wait_send()
        mine.wait()

    return pl.pallas_call(
        body, name=name,
        in_specs=[pl.BlockSpec(memory_space=pltpu.VMEM)],
        out_specs=pl.BlockSpec(memory_space=pltpu.VMEM),
        out_shape=jax.ShapeDtypeStruct((N_DEV, r, LANES), F32),
        scratch_shapes=[pltpu.SemaphoreType.DMA((N_DEV - 1,)), pltpu.SemaphoreType.DMA((N_DEV - 1,)),
                        pltpu.SemaphoreType.DMA],
        compiler_params=pltpu.CompilerParams(vmem_limit_bytes=VMEM_LIMIT),
    )(buf)


def _cut(ref, shard_axis, *, chip=None, half=None, lead=None):
    shape = ref.shape[1:] if lead is not None else ref.shape
    idx = [slice(None), slice(None)]
    if chip is not None:
        w = shape[shard_axis] // N_CHIPS
        idx[shard_axis] = pl.ds(pl.multiple_of(chip * w, w), w)
    if half is not None:
        hw = shape[1 - shard_axis] // 2
        idx[1 - shard_axis] = pl.ds(pl.multiple_of(half * hw, hw), hw)
    if lead is not None:
        idx = [lead] + idx
    return ref.at[tuple(idx)]


def _wait_recv(ref, send_sem, recv_sem):
    pltpu.make_async_remote_copy(src_ref=ref, dst_ref=ref, send_sem=send_sem, recv_sem=recv_sem,
                                 device_id=_mesh_pos(), device_id_type=MESH).wait_recv()


def _gather_plan(axes):
    def plan(srcs, lands, send_sems, recv_sems):
        x, y, c = _mesh_pos()
        chip = 2 * x + y
        copies = []
        for i, ax in enumerate(axes):
            for k in range(1, N_CHIPS):
                px, py = x ^ (k >> 1), y ^ (k & 1)
                mine = _cut(lands[i], ax, chip=chip, half=c)
                j = i * (N_CHIPS - 1) + k - 1
                sems = dict(send_sem=send_sems.at[j], recv_sem=recv_sems.at[j], device_id=(px, py, c), device_id_type=MESH)
                send = pltpu.make_async_remote_copy(src_ref=mine, dst_ref=mine, **sems)
                recv = pltpu.make_async_remote_copy(src_ref=mine, dst_ref=_cut(lands[i], ax, chip=2 * px + py, half=c), **sems)
                copies.append((send, recv))
        return copies
    return plan


def _gather_near_plan(axes):
    def plan(srcs, lands, send_sems, recv_sems):
        x, y, c = _mesh_pos()
        copies = []
        for i, ax in enumerate(axes):
            mine = _cut(lands[i], ax, chip=2 * x + y, half=c)
            for k, (px, py) in enumerate(((1 - x, y), (x, 1 - y))):
                sems = dict(send_sem=send_sems.at[2 * i + k], recv_sem=recv_sems.at[2 * i + k], device_id=(px, py, c), device_id_type=MESH)
                send = pltpu.make_async_remote_copy(src_ref=mine, dst_ref=mine, **sems)
                recv = pltpu.make_async_remote_copy(src_ref=mine, dst_ref=_cut(lands[i], ax, chip=2 * px + py, half=c), **sems)
                copies.append((send, recv))
        return copies
    return plan


def _gather_far_plan(axes):
    def plan(srcs, lands, send_sems, recv_sems):
        x, y, c = _mesh_pos()
        from_chip = 2 * (x ^ (1 - c)) + (y ^ c)
        to = (x ^ c, y ^ (1 - c), c)
        diag = 2 * (1 - x) + (1 - y)
        copies = []
        for i, ax in enumerate(axes):
            passed = _cut(lands[i], ax, chip=from_chip, half=c)
            sems = dict(send_sem=send_sems.at[i], recv_sem=recv_sems.at[i], device_id=to, device_id_type=MESH)
            send = pltpu.make_async_remote_copy(src_ref=passed, dst_ref=passed, **sems)
            recv = pltpu.make_async_remote_copy(src_ref=passed, dst_ref=_cut(lands[i], ax, chip=diag, half=c), **sems)
            copies.append((send, recv))
        return copies
    return plan


def _pair_plan(axes):
    def plan(srcs, lands, send_sems, recv_sems):
        x, y, c = _mesh_pos()
        copies = []
        for i, ax in enumerate(axes):
            cp = pltpu.make_async_remote_copy(
                src_ref=_cut(srcs[i], ax, half=1 - c), dst_ref=lands[i], send_sem=send_sems.at[i], recv_sem=recv_sems.at[i],
                device_id=(x, y, 1 - c), device_id_type=MESH)
            copies.append((cp, cp))
        return copies
    return plan


def _scatter_plan(axes):
    def plan(srcs, lands, send_sems, recv_sems):
        x, y, c = _mesh_pos()
        copies = []
        for i, ax in enumerate(axes):
            for k in range(1, N_CHIPS):
                px, py = x ^ (k >> 1), y ^ (k & 1)
                j = i * (N_CHIPS - 1) + k - 1
                cp = pltpu.make_async_remote_copy(
                    src_ref=_cut(srcs[i], ax, chip=2 * px + py), dst_ref=lands[i].at[k - 1],
                    send_sem=send_sems.at[j], recv_sem=recv_sems.at[j], device_id=(px, py, c), device_id_type=MESH)
                copies.append((cp, cp))
        return copies
    return plan


HBM = pl.BlockSpec(memory_space=pltpu.HBM)
SEM = pl.BlockSpec(memory_space=pltpu.SEMAPHORE)
EFFECT = pltpu.SideEffectType.DATAFLOW_SIDE_EFFECTING


def _in_hbm(arrays):
    return [pltpu.with_memory_space_constraint(a, pltpu.HBM) for a in arrays]


def _split_start(srcs, lands, plan, n_copies, *, name):
    bufs = list(srcs) + list(lands)
    ns, nb = len(srcs), len(bufs)

    def body(*refs):
        send_sems, recv_sems, token = refs[nb], refs[nb + 1], refs[-1]
        for send, _ in plan(refs[:ns], refs[ns:nb], send_sems, recv_sems):
            send.start()
        token[...] = jnp.zeros_like(token)

    sems = pltpu.SemaphoreType.DMA((n_copies,))
    res = pl.pallas_call(
        body, name=name, in_specs=[HBM] * nb,
        out_specs=[SEM, SEM] + [HBM] * nb + [pl.BlockSpec(memory_space=pltpu.VMEM)],
        out_shape=[sems, sems] + [pltpu.HBM(a.shape, a.dtype) for a in bufs] + [jax.ShapeDtypeStruct((8, LANES), F32)],
        input_output_aliases={j: 2 + j for j in range(nb)},
        compiler_params=pltpu.CompilerParams(has_side_effects=EFFECT),
    )(*_in_hbm(bufs))
    return res[0], res[1], res[2:2 + ns], res[2 + ns:2 + nb], res[-1]


def _split_wait(started, after, plan, *, name, with_srcs=False):
    send_sems, recv_sems, srcs, lands, _ = started
    bufs = list(srcs) + list(lands)
    ns, nb = len(srcs), len(bufs)

    def body(*refs):
        for send, recv in plan(refs[:ns], refs[ns:nb], refs[nb], refs[nb + 1]):
            send.wait_send()
            recv.wait_recv()

    res = pl.pallas_call(
        body, name=name, in_specs=[HBM] * nb + [SEM, SEM, ANY], out_specs=[HBM] * nb,
        out_shape=[pltpu.HBM(a.shape, a.dtype) for a in bufs],
        input_output_aliases={j: j for j in range(nb)},
        compiler_params=pltpu.CompilerParams(has_side_effects=EFFECT),
    )(*bufs, send_sems, recv_sems, after)
    return (res[:ns], res[ns:]) if with_srcs else res[ns:]


def _cast_into_full(w3, layer, ax, chip, *, after=None, name):
    _, r, wd = w3.shape
    tr = _rows_per_tile(r, wd, 4 << 20)
    nt = r // tr
    full_shape = (r, wd * N_CHIPS) if ax == 1 else (r * N_CHIPS, wd)
    out_map = (lambda i, ch: (i, ch[0])) if ax == 1 else (lambda i, ch: (ch[0] * nt + i, 0))
    zero = jnp.zeros((1, wd), F32) + (0.0 if after is None else after)

    def body(chip_ref, w_ref, z_ref, o_ref):
        o_ref[...] = (w_ref[...] + z_ref[...]).astype(o_ref.dtype)

    return pl.pallas_call(
        body, name=name,
        grid_spec=pltpu.PrefetchScalarGridSpec(
            num_scalar_prefetch=1, grid=(nt,),
            in_specs=[pl.BlockSpec((None, tr, wd), lambda i, ch: (layer, i, 0)), pl.BlockSpec((1, wd), lambda i, ch: (0, 0))],
            out_specs=pl.BlockSpec((tr, wd), out_map)),
        out_shape=jax.ShapeDtypeStruct(full_shape, BF16),
        compiler_params=_params(("parallel",)),
    )(jnp.reshape(chip, (1,)).astype(jnp.int32), w3, zero)


def _forward_halves(fulls, axes, *, name):
    n = len(fulls)

    def body(*refs):
        bufs = refs[:n]
        send_sems, recv_sems = refs[2 * n:]
        x, y, c = _mesh_pos()
        sends = []
        for i in range(n):
            for k in range(1, N_CHIPS):
                landed = _cut(bufs[i], axes[i], chip=2 * (x ^ (k >> 1)) + (y ^ (k & 1)), half=c)
                cp = pltpu.make_async_remote_copy(
                    src_ref=landed, dst_ref=landed, send_sem=send_sems.at[i, k - 1], recv_sem=recv_sems.at[i, k - 1],
                    device_id=(x, y, 1 - c), device_id_type=MESH)
                cp.start()
                sends.append(cp)
        for i in range(n):
            for k in range(1, N_CHIPS):
                other = _cut(bufs[i], axes[i], chip=2 * (x ^ (k >> 1)) + (y ^ (k & 1)), half=1 - c)
                _wait_recv(other, send_sems.at[i, k - 1], recv_sems.at[i, k - 1])
        for cp in sends:
            cp.wait_send()

    pairs = pltpu.SemaphoreType.DMA((n, N_CHIPS - 1))
    return pl.pallas_call(
        body, name=name, in_specs=[ANY] * n, out_specs=[ANY] * n,
        out_shape=[jax.ShapeDtypeStruct(a.shape, a.dtype) for a in fulls],
        input_output_aliases={j: j for j in range(n)},
        scratch_shapes=[pairs, pairs],
    )(*fulls)


def _share_halves_in_place(bufs, axes, *, name):
    n = len(bufs)

    def body(*refs):
        ins = refs[:n]
        send_sems, recv_sems = refs[2 * n:]
        x, y, c = _mesh_pos()
        sends = []
        for i in range(n):
            mine = _cut(ins[i], axes[i], half=c)
            cp = pltpu.make_async_remote_copy(
                src_ref=mine, dst_ref=mine, send_sem=send_sems.at[i], recv_sem=recv_sems.at[i],
                device_id=(x, y, 1 - c), device_id_type=MESH)
            cp.start()
            sends.append(cp)
        for i in range(n):
            _wait_recv(_cut(ins[i], axes[i], half=1 - c), send_sems.at[i], recv_sems.at[i])
        for cp in sends:
            cp.wait_send()

    sems = pltpu.SemaphoreType.DMA((n,))
    return pl.pallas_call(
        body, name=name, in_specs=[ANY] * n, out_specs=[ANY] * n,
        out_shape=[jax.ShapeDtypeStruct(b.shape, b.dtype) for b in bufs],
        input_output_aliases={j: j for j in range(n)}, scratch_shapes=[sems, sems],
    )(*bufs)


def _adamw_math(w, g, m, v):
    m = ADAM_B1 * m + (1.0 - ADAM_B1) * g
    v = ADAM_B2 * v + (1.0 - ADAM_B2) * (g * g)
    m_hat = m / (1.0 - ADAM_B1 ** ADAM_STEP)
    v_hat = v / (1.0 - ADAM_B2 ** ADAM_STEP)
    delta = -ADAM_LR * (m_hat / (jnp.sqrt(v_hat) + ADAM_EPS) + ADAM_WD * w)
    return delta, m, v


def _adamw_layer(w3, m3, v3, p, q, layer, prev, *, name):
    nl, rows, width = w3.shape
    tr = _rows_per_tile(rows, width)

    def fn(*t):
        if q is None:
            w, m, v, g = t
        else:
            w, m, v, g, g2 = t
            g = g + g2
        delta, m, v = _adamw_math(w, g, m, v)
        return g, delta, m, v

    ins = [('t', w3, 0, width, layer), ('t', m3, 0, width, layer), ('t', v3, 0, width, layer), ('t', p, 0, width)]
    if q is not None:
        ins.append(('t', q, 0, width))
    outs = [('t', width, F32, layer, nl)] * 4
    aliases = None if prev is None else [(prev[i], i) for i in range(4)]
    return _ew(fn, ins, outs, rows=rows, tr=tr, name=name, aliases=aliases)


def _pack_rows(vec):
    n = vec.shape[0]
    r = -(-n // (8 * LANES)) * 8
    return jnp.pad(vec, (0, r * LANES - n)).reshape(r, LANES)


def kernel(x, c, ctx, c_ctx, ada_w, ada_b, norm_g, w_in, na_rpb, ret_decay_logit, w_proj_na, w_proj_ret, w_out, final_g, loss_target, m_c_ctx, m_ada_w, m_ada_b, m_norm_g, m_w_in, m_na_rpb, m_ret_decay_logit, m_w_proj_na, m_w_proj_ret, m_w_out, m_final_g, v_c_ctx, v_ada_w, v_ada_b, v_norm_g, v_w_in, v_na_rpb, v_ret_decay_logit, v_w_proj_na, v_w_proj_ret, v_w_out, v_final_g):
    depth = w_in.shape[0]
    s_len, d_model = x.shape[1], x.shape[2]
    l_len = ctx.shape[1]
    t_len = s_len + l_len
    na_heads = na_rpb.shape[1]
    ret_heads = ret_decay_logit.shape[2]
    w_na = na_heads * NA_HEAD_DIM
    w_qk = ret_heads * RET_KEY_DIM
    w_v = ret_heads * RET_VAL_DIM
    in_cols = w_in.shape[2] * N_CHIPS
    assert in_cols == 4 * w_na + 2 * w_qk + 2 * w_v + 2 * d_model
    assert x.shape[0] == 1 and s_len % (NA_WIN_ROWS * GRID_W) == 0 and l_len % RET_CHUNK == 0
    off = np.cumsum([0, w_na, w_na, w_na, w_na, w_qk, w_qk, w_v, w_v, d_model, d_model])
    o_naz, o_retq, o_retz, o_gna, o_gret = int(off[3]), int(off[4]), int(off[7]), int(off[8]), int(off[9])
    rows = s_len // GRID_W
    tr = _tile(l_len, 256, 8)
    n0 = s_len // tr
    mod_cols = 3 * d_model
    mod_shard = ada_w.shape[2]

    xi, yi, ci = _mesh_pos()
    me = 4 * xi + 2 * yi + ci
    chip = 2 * xi + yi

    big_axes = [1, 1, 0, 0]
    n_big = len(big_axes) * (N_CHIPS - 1)
    gather_plan, scatter_plan = _gather_plan(big_axes), _scatter_plan(big_axes)

    c_silu = c[0] * _sigmoid(c[0])
    cc_silu = c_ctx * _sigmoid(c_ctx)
    c_all = _all_gather_small(_pack_rows(c_silu), name="gather_c")[:, :d_model // LANES].reshape(N_DEV, d_model)
    a_rows = jnp.concatenate([c_all, cc_silu[None], jnp.zeros((16 - N_DEV - 1, d_model), F32)], axis=0)
    mod_part = jnp.stack([_mm(a_rows, ada_w, b_lead=l, out_dtype=F32, name="ada_fwd_%d" % l) for l in range(depth)])
    mod_all = _all_gather_small(_pack_rows(mod_part.reshape(-1)), name="gather_mod")
    n_mod = depth * 16 * mod_shard
    mod_all = mod_all.reshape(N_DEV, -1)[:, :n_mod].reshape(N_CHIPS, 2, depth, 16, mod_shard)[:, 0]
    mod_all = jnp.transpose(mod_all, (1, 2, 0, 3)).reshape(depth, 16, mod_cols) + ada_b[:, None, :]

    big_named = list(zip((w_in, w_proj_na, w_proj_ret, w_out), big_axes, ("w_in", "w_proj_na", "w_proj_ret", "w_out")))
    w_in0 = _cast_into_full(w_in, 0, big_axes[0], chip, name="cast_w_in_0")
    mod_all, w_in0 = lax.optimization_barrier((mod_all, w_in0))
    plan_near, plan_far, plan_rest = _gather_near_plan(big_axes[:1]), _gather_far_plan(big_axes[:1]), _gather_plan(big_axes[1:])
    near_all, far_all = _gather_near_plan(big_axes), _gather_far_plan(big_axes)
    first_gather = _split_start([], [w_in0], plan_near, 2, name="gather_start_0_in")
    start_token = first_gather[4][0, 0]
    fulls = [[None if (l == 0 and tag == "w_in") else _cast_into_full(w, l, ax, chip, after=start_token, name="cast_%s_%d" % (tag, l))
              for w, ax, tag in big_named] for l in range(depth)]
    mod_lat = lax.dynamic_index_in_dim(mod_all, me, axis=1, keepdims=False)
    mod_ctx = mod_all[:, N_DEV]
    bias_z = [_na_bias_table(na_rpb[l], s_len // GRID_W, name="na_bias_%d" % l) for l in range(depth)]
    bias_z, fulls = lax.optimization_barrier((bias_z, fulls))
    landed_near = _split_wait(first_gather, bias_z[-1], plan_near, name="gather_wait_0_in")
    passing = _split_start([], landed_near, plan_far, 1, name="gather_pass_0_in")
    front_token = passing[4][0, 0]

    c2, s2 = _rope_tables(s_len, l_len)
    log_gamma = jax.nn.log_sigmoid(ret_decay_logit)
    x_all = jnp.concatenate([x[0], ctx[0]], axis=0)

    def grp(lat_vec, ctx_vec):
        return jnp.stack([lat_vec, ctx_vec])[:, None, :]

    saved, full_w = [], []
    for l in range(depth):
        shift, scale, gate = [grp(mod_lat[l, i * d_model:(i + 1) * d_model], mod_ctx[l, i * d_model:(i + 1) * d_model])
                              for i in range(3)]
        gs = norm_g[l][None, None, :] * (1.0 + scale) + front_token

        def modnorm(xt, gs_t, sh_t):
            r = lax.rsqrt(jnp.mean(xt * xt, axis=-1, keepdims=True) + NORM_EPS)
            return xt * r * gs_t + sh_t

        h, = _ew(modnorm, [('t', x_all, 0, d_model), ('g', gs), ('g', shift)], [('t', d_model, BF16)],
                 rows=t_len, tr=tr, n0=n0, name="modnorm_%d" % l)
        bias = _na_bias_layout(bias_z[l], front_token)
        if l == 0:
            h, bias = lax.optimization_barrier((h, bias))
            landed_in = _split_wait(passing, h, plan_far, name="gather_wait_0_in_far")
            landed_in, rest0, later = lax.optimization_barrier((landed_in, fulls[0][1:], fulls[1:]))
            rest_gather = _split_start([], rest0, plan_rest, n_big - (N_CHIPS - 1), name="gather_start_0_rest")
            later_gathers = [_split_start([], later[j], near_all, 2 * len(big_axes), name="gather_start_%d" % (j + 1))
                             for j in range(depth - 1)]
            win_f, = _forward_halves(landed_in, big_axes[:1], name="gather_forward_0_in")
            win_f, tokens = lax.optimization_barrier((win_f, [rest_gather[4]] + [g[4] for g in later_gathers]))
            gate = gate + sum(t[0, 0] for t in tokens)
        else:
            h, bias = lax.optimization_barrier((h, bias))
            landed = _split_wait(later_passes[l - 1], h, far_all, name="gather_wait_%d" % l)
            win_f, wpn_f, wpr_f, wout_f = _forward_halves(landed, big_axes, name="gather_forward_%d" % l)
        u = _mm(h, win_f, tm=1152, tn=1024, name="in_proj_%d" % l)
        o_na = _na_fwd(u, bias, s_len=s_len, heads=na_heads, name="na_fwd_%d" % l)
        o_ret, states = _ret_fwd(u, c2, s2, log_gamma[l], s_len=s_len, heads=ret_heads, q_off=o_retq, name="ret_fwd_%d" % l)

        def act(o1, z1, o2, z2):
            a1 = o1.astype(F32) * _silu_parts(z1.astype(F32))[0]
            sz = _silu_parts(z2.astype(F32))[0]
            outs = []
            for hh in range(ret_heads):
                sl = slice(hh * RET_VAL_DIM, (hh + 1) * RET_VAL_DIM)
                oh = o2[:, sl]
                r = lax.rsqrt(jnp.mean(oh * oh, axis=-1, keepdims=True) + NORM_EPS)
                outs.append(oh * r * sz[:, sl])
            return a1, jnp.concatenate(outs, axis=-1)

        a_na, a_ret = _ew(act, [('t', o_na, 0, w_na), ('t', u, o_naz // w_na, w_na), ('t', o_ret, 0, w_v), ('t', u, o_retz // w_v, w_v)],
                          [('t', w_na, BF16), ('t', w_v, BF16)], rows=t_len, tr=tr, name="act_%d" % l)
        if l == 0:
            landed_rest = _split_wait(rest_gather, a_na, plan_rest, name="gather_wait_0_rest")
            later_passes = [_split_start([], _split_wait(later_gathers[j], a_na, near_all, name="gather_near_%d" % (j + 1)),
                                         far_all, len(big_axes), name="gather_pass_%d" % (j + 1)) for j in range(depth - 1)]
            landed_rest, tokens = lax.optimization_barrier((landed_rest, [g[4] for g in later_passes]))
            gate = gate + sum(t[0, 0] for t in tokens)
            wpn_f, wpr_f, wout_f = _forward_halves(landed_rest, big_axes[1:], name="gather_forward_0_rest")
        full_w.append((win_f, wpn_f, wpr_f, wout_f))
        y_na = _mm(a_na, wpn_f, name="proj_na_%d" % l)
        y_ret = _mm(a_ret, wpr_f, name="proj_ret_%d" % l)

        def merge(y1, y2, g1, g2):
            return _sigmoid(g1.astype(F32)) * y1.astype(F32) + _sigmoid(g2.astype(F32)) * y2.astype(F32)

        merged, = _ew(merge, [('t', y_na, 0, d_model), ('t', y_ret, 0, d_model), ('t', u, o_gna // d_model, d_model), ('t', u, o_gret // d_model, d_model)],
                      [('t', d_model, BF16)], rows=t_len, tr=tr, name="merge_%d" % l)
        out = _mm(merged, wout_f, out_dtype=F32, name="out_proj_%d" % l)
        x_new, = _ew(lambda xt, ot, gt: xt + gt * ot, [('t', x_all, 0, d_model), ('t', out, 0, d_model), ('g', gate)],
                     [('t', d_model, F32)], rows=t_len, tr=tr, n0=n0, name="resid_%d" % l)
        saved.append(dict(x=x_all, h=h, u=u, bias=bias, o_na=o_na, o_ret=o_ret, states=states, a_na=a_na, a_ret=a_ret,
                          y_na=y_na, y_ret=y_ret, merged=merged, out=out, gate=gate, gs=gs, scale=scale))
        x_all = x_new

    def final(xt, tt, gt):
        r = lax.rsqrt(jnp.mean(xt * xt, axis=-1, keepdims=True) + NORM_EPS)
        xh = xt * r
        e = xh * gt - tt
        dy = e * (1.0 / d_model)
        dyg = dy * gt
        dx = r * (dyg - xh * jnp.mean(dyg * xh, axis=-1, keepdims=True))
        return dx, _rsum(dy * xh), _rsum(e * e)

    dx_lat, d_final_g, loss_cols = _ew(final, [('t', x_all, 0, d_model), ('t', loss_target[0], 0, d_model), ('g', final_g[None, None, :])],
                                       [('t', d_model, F32), ('r', d_model, 1), ('r', d_model, 1)], rows=s_len, tr=tr, name="final")
    loss_part = (0.5 / d_model) * jnp.sum(loss_cols)
    dx_all = jnp.concatenate([dx_lat, jnp.zeros((l_len, d_model), F32)], axis=0)

    big_w = [(w_in, m_w_in, v_w_in), (w_proj_na, m_w_proj_na, v_w_proj_na), (w_proj_ret, m_w_proj_ret, v_w_proj_ret), (w_out, m_w_out, v_w_out)]
    big_res = [None] * 4
    scatters = {}
    back_token = jnp.zeros((), F32)

    pairs = {}

    def start_pair(key, grads, axes):
        plan = _pair_plan(axes)
        lands = []
        for g, ax in zip(grads, axes):
            shp = list(g.shape)
            shp[1 - ax] //= 2
            lands.append(lax.empty(tuple(shp), BF16))
        pairs[key] = (_split_start(grads, lands, plan, len(axes), name="pair_start_%s" % key), axes, plan)
        return pairs[key][0][4]

    def start_scatter(key, after):
        started, axes, pair_plan = pairs[key]
        grads, theirs = _split_wait(started, after, pair_plan, name="pair_wait_%s" % key, with_srcs=True)
        plan = _scatter_plan(axes)
        pair = [_sum_pair(g, t, ax, ci, name="sum_pair_%s_%d" % (key, i)) for i, (g, t, ax) in enumerate(zip(grads, theirs, axes))]
        own = [lax.dynamic_slice_in_dim(s, chip * (s.shape[ax] // N_CHIPS), s.shape[ax] // N_CHIPS, axis=ax) for s, ax in zip(pair, axes)]
        lands = [lax.empty((N_CHIPS - 1,) + o.shape, BF16) for o in own]
        started = _split_start(pair, lands, plan, len(axes) * (N_CHIPS - 1), name="scatter_start_%s" % key)
        scatters[key] = (started, own, axes, plan)
        return started[4]

    def finish_scatter(key, after):
        started, own, axes, plan = scatters[key]
        recv = _split_wait(started, after, plan, name="scatter_wait_%s" % key)
        bufs = [_sum_chips_into(own[i], rbuf, axes[i], ci, name="sum_chips_%s_%d" % (key, i)) for i, rbuf in enumerate(recv)]
        return _share_halves_in_place(bufs, axes, name="share_halves_%s" % key)

    def adamw_big(l, idx, grads, big_res):
        for i, g in zip(idx, grads):
            w3, m3, v3 = big_w[i]
            big_res[i] = _adamw_layer(w3, m3, v3, g, None, l, big_res[i], name="adamw_big_%d_%d" % (i, l))
        return big_res

    small = dict(dmod_lat=[None] * depth, dmod_ctx=[None] * depth, dnorm_g=[None] * depth, drpb=[None] * depth, ddecay=[None] * depth)
    for l in reversed(range(depth)):
        sv = saved[l]
        win_f, wpn_f, wpr_f, wout_f = full_w[l]

        def resid_bwd(dxt, ot, gt):
            return gt * dxt, _rsum(dxt * ot)

        dout, dgate = _ew(resid_bwd, [('t', dx_all, 0, d_model), ('t', sv['out'], 0, d_model), ('g', sv['gate'] + back_token)],
                          [('t', d_model, BF16), ('r', d_model, 2)], rows=t_len, tr=tr, n0=n0, name="resid_bwd_%d" % l)
        dmerged = _mm(dout, wout_f, tb=True, name="out_proj_dx_%d" % l)
        g_wout = _mm(sv['merged'], dout, ta=True, tm=1024, tk=t_len, name="out_proj_dw_%d" % l)

        def merge_bwd(dm, y1, y2, g1, g2):
            dm = dm.astype(F32)
            s1, s2_ = _sigmoid(g1.astype(F32)), _sigmoid(g2.astype(F32))
            return dm * s1, dm * s2_, dm * y1.astype(F32) * s1 * (1.0 - s1), dm * y2.astype(F32) * s2_ * (1.0 - s2_)

        u = sv['u']
        dy_na, dy_ret, dg_na, dg_ret = _ew(
            merge_bwd, [('t', dmerged, 0, d_model), ('t', sv['y_na'], 0, d_model), ('t', sv['y_ret'], 0, d_model),
                        ('t', u, o_gna // d_model, d_model), ('t', u, o_gret // d_model, d_model)],
            [('t', d_model, BF16)] * 4, rows=t_len, tr=tr, name="merge_bwd_%d" % l)
        da_na = _mm(dy_na, wpn_f, tb=True, name="proj_na_dx_%d" % l)
        g_wpn = _mm(sv['a_na'], dy_na, ta=True, tm=1024, tk=t_len, name="proj_na_dw_%d" % l)
        da_ret = _mm(dy_ret, wpr_f, tb=True, name="proj_ret_dx_%d" % l)
        g_wpr = _mm(sv['a_ret'], dy_ret, ta=True, tm=1024, tk=t_len, name="proj_ret_dw_%d" % l)
        lg_l = log_gamma[l]
        if l == 0:
            pair_token = start_pair("0_rest", [g_wpn, g_wpr, g_wout], big_axes[1:])

        def act_bwd(da1, o1, z1, da2, o2, z2):
            da1, da2 = da1.astype(F32), da2.astype(F32)
            si1, ds1 = _silu_parts(z1.astype(F32))
            si2, ds2 = _silu_parts(z2.astype(F32))
            do1 = da1 * si1
            dz1 = da1 * o1.astype(F32) * ds1
            dn = da2 * si2
            do2, dz2 = [], []
            for hh in range(ret_heads):
                sl = slice(hh * RET_VAL_DIM, (hh + 1) * RET_VAL_DIM)
                oh = o2[:, sl]
                r = lax.rsqrt(jnp.mean(oh * oh, axis=-1, keepdims=True) + NORM_EPS)
                nh = oh * r
                dz2.append(da2[:, sl] * nh * ds2[:, sl])
                do2.append(r * (dn[:, sl] - nh * jnp.mean(dn[:, sl] * nh, axis=-1, keepdims=True)))
            return do1, dz1, jnp.concatenate(do2, axis=-1), jnp.concatenate(dz2, axis=-1)

        do_na, dz_na, do_ret, dz_ret = _ew(
            act_bwd, [('t', da_na, 0, w_na), ('t', sv['o_na'], 0, w_na), ('t', u, o_naz // w_na, w_na),
                      ('t', da_ret, 0, w_v), ('t', sv['o_ret'], 0, w_v), ('t', u, o_retz // w_v, w_v)],
            [('t', w_na, BF16), ('t', w_na, BF16), ('t', w_v, BF16), ('t', w_v, BF16)], rows=t_len, tr=tr, name="act_bwd_%d" % l)
        dq_na, dk_na, dv_na, dbias = _na_bwd(u, sv['bias'], sv['o_na'], do_na, s_len=s_len, heads=na_heads, name="na_bwd_%d" % l)
        small['drpb'][l] = _rpb_grad(dbias, name="rpb_grad_%d" % l)
        if l == 0:
            lg_l = lg_l + start_scatter("0_rest", dq_na)[0, 0] + pair_token[0, 0]
        dq_r, dk_r, dv_r, dlg = _ret_bwd(u, c2, s2, lg_l, sv['states'], do_ret, s_len=s_len, heads=ret_heads,
                                         q_off=o_retq, name="ret_bwd_%d" % l)
        small['ddecay'][l] = jnp.transpose(dlg[:, :, 0, 0]) * _sigmoid(-ret_decay_logit[l])
        du_parts = [dq_na, dk_na, dv_na, dz_na, dq_r, dk_r, dv_r, dz_ret, dg_na, dg_ret]
        du, = _ew(lambda *t: jnp.concatenate(t, axis=-1), [('t', p, 0, p.shape[1]) for p in du_parts], [('t', in_cols, BF16)],
                  rows=t_len, tr=tr, name="du_concat_%d" % l)
        g_win = _mm(sv['h'], du, ta=True, tm=1024, tn=1024, tk=t_len, name="in_proj_dw_%d" % l)
        if l > 0:
            du, pair_token = lax.optimization_barrier((du, start_pair("%d_all" % l, [g_win, g_wpn, g_wpr, g_wout], big_axes)))
        else:
            du, in_token = lax.optimization_barrier((du, start_pair("0_in", [g_win], big_axes[:1])))
        dh = _mm(du, win_f, tb=True, out_dtype=F32, tm=1152, tn=1024, name="in_proj_dx_%d" % l)

        def modnorm_bwd(xt, dht, dxt, gs_t):
            r = lax.rsqrt(jnp.mean(xt * xt, axis=-1, keepdims=True) + NORM_EPS)
            xh = xt * r
            dhg = dht * gs_t
            dx = r * (dhg - xh * jnp.mean(dhg * xh, axis=-1, keepdims=True)) + dxt
            return dx, _rsum(dht), _rsum(dht * xh)

        dx_all, dshift, dgs = _ew(modnorm_bwd, [('t', sv['x'], 0, d_model), ('t', dh, 0, d_model), ('t', dx_all, 0, d_model), ('g', sv['gs'])],
                                  [('t', d_model, F32), ('r', d_model, 2), ('r', d_model, 2)], rows=t_len, tr=tr, n0=n0, name="modnorm_bwd_%d" % l)
        dscale = dgs * norm_g[l][None, None, :]
        small['dnorm_g'][l] = jnp.sum(dgs * (1.0 + sv['scale']), axis=(0, 1))
        dmod = jnp.concatenate([dshift, dscale, dgate], axis=-1)[:, 0]
        small['dmod_lat'][l], small['dmod_ctx'][l] = dmod[0], dmod[1]

        if l > 0:
            back_token = start_scatter("%d_all" % l, dx_all)[0, 0] + pair_token[0, 0]

    grad_x = dx_all[:s_len][None]

    drpb = jnp.stack(small['drpb']).reshape(-1)
    ddecay = jnp.stack(small['ddecay']).reshape(-1)
    pieces = [jnp.stack(small['dmod_lat']).reshape(-1), jnp.stack(small['dmod_ctx']).reshape(-1),
              jnp.stack(small['dnorm_g']).reshape(-1), d_final_g.reshape(-1), drpb, ddecay, loss_part[None]]
    sizes = [int(p.shape[0]) for p in pieces]
    pads = [-(-s // LANES) * LANES for s in sizes]
    packed = jnp.concatenate([jnp.pad(p, (0, pd - s)) for p, s, pd in zip(pieces, sizes, pads)])
    gathered = _all_gather_small(_pack_rows(packed), name="gather_small_grads")
    r_small = gathered.shape[1]

    def sum8(*t):
        acc = t[0]
        for other in t[1:]:
            acc = acc + other
        return acc

    total, = _ew(sum8, [('t', gathered, 0, LANES, k) for k in range(N_DEV)], [('t', LANES, F32)], rows=r_small, tr=r_small, name="sum_devices")
    total = total.reshape(-1)
    starts = np.cumsum([0] + pads)
    g_mod_lat_sum, g_mod_ctx, g_norm_g, g_final_g, g_rpb, g_decay, loss = [total[starts[i]:starts[i] + sizes[i]] for i in range(len(pieces))]
    loss = loss[0]
    g_ada_b = (g_mod_lat_sum + g_mod_ctx).reshape(depth, mod_cols)
    g_mod_ctx = g_mod_ctx.reshape(depth, mod_cols)
    dmod_lat_all = gathered.reshape(N_DEV, -1)[:, :depth * mod_cols].reshape(N_DEV, depth, mod_cols)

    dcc_part = jnp.zeros((16, d_model), F32)
    ctx_cols = [lax.dynamic_slice_in_dim(g_mod_ctx[l], chip * mod_shard, mod_shard, axis=0) for l in range(depth)]
    for l in reversed(range(depth)):
        c_rows = jnp.concatenate([ctx_cols[l][None], jnp.zeros((15, mod_shard), F32)], axis=0)
        dcc_part = dcc_part + _mm(c_rows, ada_w, tb=True, b_lead=l, out_dtype=F32, name="ada_dc_%d" % l)
    dcc_all = _all_gather_small(_pack_rows(dcc_part[0]), name="gather_dcc")[:, :d_model // LANES].reshape(N_CHIPS, 2, d_model)[:, 0]

    tail_token = start_scatter("0_in", dcc_all) + in_token
    dcc = ((dcc_all[0] + dcc_all[1]) + dcc_all[2]) + dcc_all[3]
    sg = _sigmoid(c_ctx)
    g_c_ctx = dcc * (sg * (1.0 + c_ctx * (1.0 - sg)))
    for l in reversed(range(1, depth)):
        big_res = adamw_big(l, range(4), finish_scatter("%d_all" % l, tail_token), big_res)

    ada_res = None
    for l in reversed(range(depth)):
        lat_cols = lax.dynamic_slice_in_dim(dmod_lat_all[:, l], chip * mod_shard, mod_shard, axis=1)
        d_rows = jnp.concatenate([lat_cols, ctx_cols[l][None], jnp.zeros((16 - N_DEV - 1, mod_shard), F32)], axis=0) + tail_token[0, 0]
        g_ada = _mm(a_rows, d_rows, ta=True, out_dtype=F32, tm=512, name="ada_dw_%d" % l)
        ada_res = _adamw_layer(ada_w, m_ada_w, v_ada_w, g_ada, None, l, ada_res, name="adamw_ada_%d" % l)

    small_w = [(c_ctx, m_c_ctx, v_c_ctx, g_c_ctx), (ada_b, m_ada_b, v_ada_b, g_ada_b),
               (norm_g, m_norm_g, v_norm_g, g_norm_g), (na_rpb, m_na_rpb, v_na_rpb, g_rpb),
               (ret_decay_logit, m_ret_decay_logit, v_ret_decay_logit, g_decay), (final_g, m_final_g, v_final_g, g_final_g)]
    sw_sizes = [int(np.prod(t[0].shape)) for t in small_w]
    sw_pads = [-(-s // LANES) * LANES for s in sw_sizes]

    def pack(j):
        return _pack_rows(jnp.concatenate([jnp.pad(t[j].reshape(-1), (0, pd - s)) for t, s, pd in zip(small_w, sw_sizes, sw_pads)]))

    pw_, pm_, pv_, pg_ = pack(0), pack(1), pack(2), pack(3)
    sw_out = _ew(lambda w, m, v, g: (g,) + _adamw_math(w, g, m, v),
                 [('t', pw_, 0, LANES), ('t', pm_, 0, LANES), ('t', pv_, 0, LANES), ('t', pg_, 0, LANES)],
                 [('t', LANES, F32)] * 4, rows=pw_.shape[0], tr=pw_.shape[0], name="adamw_small")
    sw_starts = np.cumsum([0] + sw_pads)
    sw_out, ada_res, big_res = lax.optimization_barrier((sw_out, ada_res, big_res))
    big_res = adamw_big(0, range(1, 4), finish_scatter("0_rest", sw_out[0]), big_res)
    big_res = adamw_big(0, range(1), finish_scatter("0_in", sw_out[1]), big_res)

    def unpack(arr, i):
        return arr.reshape(-1)[sw_starts[i]:sw_starts[i] + sw_sizes[i]].reshape(small_w[i][0].shape)

    sm = [[unpack(sw_out[j], i) for i in range(len(small_w))] for j in range(4)]
    def ordered(j):
        return [sm[j][0], ada_res[j], sm[j][1], sm[j][2], big_res[0][j], sm[j][3], sm[j][4],
                big_res[1][j], big_res[2][j], big_res[3][j], sm[j][5]]

    return (loss, grad_x, *ordered(0), *ordered(1), *ordered(2), *ordered(3))
```

```python
import functools
import math

import numpy as np
import jax
import jax.numpy as jnp
from jax import lax
from jax.experimental import pallas as pl
from jax.experimental.pallas import tpu as pltpu

GRID_W = 64
NA_HEAD_DIM = 128
NA_WIN_ROWS = 8
NA_WIN_COLS = 16
NA_GROUP = 8
RET_GROUPS = (1, 2, 3)
RET_KEY_DIM = 128
RET_VAL_DIM = 256
RET_CHUNK = 128
ROPE_BASE = 10000.0
NORM_EPS = 1e-6
MASK_VALUE = -1e30
ADAM_LR = 0.001
ADAM_B1 = 0.9
ADAM_B2 = 0.999
ADAM_EPS = 1e-08
ADAM_WD = 0.01
ADAM_STEP = 10

N_CHIPS = 4
N_DEV = 8
LANES = 128
VMEM_LIMIT = 56 * 1024 * 1024
BF16 = jnp.bfloat16
F32 = jnp.float32
MESH = pl.DeviceIdType.MESH
ANY = pl.BlockSpec(memory_space=pl.ANY)


def _tile(dim, pref, align=LANES):
    if dim <= pref:
        return dim
    t = (pref // align) * align
    while t >= align:
        if dim % t == 0:
            return t
        t -= align
    return dim


def _rows_per_tile(rows, width, tile_bytes=1 << 20):
    return _tile(rows, max(8, tile_bytes // (4 * width)), 8)


def _params(sem):
    return pltpu.CompilerParams(dimension_semantics=sem, vmem_limit_bytes=VMEM_LIMIT)


def _sigmoid(x):
    return 1.0 / (1.0 + jnp.exp(-x))


def _dot(a, b, ca, cb):
    return lax.dot_general(a, b, (((ca,), (cb,)), ((), ())), preferred_element_type=F32)


def _mm(a, b, *, ta=False, tb=False, a_lead=None, b_lead=None, out_dtype=BF16, tm=1152, tn=1024, tk=2048, name):
    ash = a.shape[1:] if a_lead is not None else a.shape
    bsh = b.shape[1:] if b_lead is not None else b.shape
    m, k = (ash[1], ash[0]) if ta else ash
    n, k2 = bsh if tb else (bsh[1], bsh[0])
    assert k == k2, (name, ash, bsh)
    tm, tn, tk = _tile(m, tm), _tile(n, tn), _tile(k, tk)
    nk = k // tk

    def lead(spec_shape, imap, l):
        if l is None:
            return pl.BlockSpec(spec_shape, imap)
        return pl.BlockSpec((None,) + spec_shape, lambda i, j, kk: (l,) + imap(i, j, kk))

    a_spec = lead((tk, tm), lambda i, j, kk: (kk, i), a_lead) if ta else lead((tm, tk), lambda i, j, kk: (i, kk), a_lead)
    b_spec = lead((tn, tk), lambda i, j, kk: (j, kk), b_lead) if tb else lead((tk, tn), lambda i, j, kk: (kk, j), b_lead)
    ca, cb = (0 if ta else 1), (1 if tb else 0)

    def body(a_ref, b_ref, o_ref, *scratch):
        part = _dot(a_ref[...].astype(BF16), b_ref[...].astype(BF16), ca, cb)
        if nk == 1:
            o_ref[...] = part.astype(o_ref.dtype)
            return
        acc_ref, = scratch
        kk = pl.program_id(2)

        @pl.when(kk == 0)
        def _():
            acc_ref[...] = part

        @pl.when(kk > 0)
        def _():
            acc_ref[...] += part

        @pl.when(kk == nk - 1)
        def _():
            o_ref[...] = acc_ref[...].astype(o_ref.dtype)

    return pl.pallas_call(
        body, name=name, grid=(m // tm, n // tn, nk),
        in_specs=[a_spec, b_spec],
        out_specs=pl.BlockSpec((tm, tn), lambda i, j, kk: (i, j)),
        out_shape=jax.ShapeDtypeStruct((m, n), out_dtype),
        scratch_shapes=[] if nk == 1 else [pltpu.VMEM((tm, tn), F32)],
        compiler_params=_params(("parallel", "parallel", "arbitrary")),
    )(a, b)


def _ew(fn, ins, outs, *, rows, tr, name, n0=None, aliases=None):
    assert rows % tr == 0, (name, rows, tr)
    nt = rows // tr

    def grp(i):
        return 0 if n0 is None else jnp.where(i < n0, 0, 1)

    in_specs, args = [], []
    for spec in ins:
        if spec[0] == 't':
            arr, cb, w = spec[1], spec[2], spec[3]
            l = spec[4] if len(spec) > 4 else None
            if l is None:
                in_specs.append(pl.BlockSpec((tr, w), functools.partial(lambda i, cb: (i, cb), cb=cb)))
            else:
                in_specs.append(pl.BlockSpec((None, tr, w), functools.partial(lambda i, cb, l: (l, i, cb), cb=cb, l=l)))
            args.append(arr)
        else:
            arr = spec[1]
            g = arr.shape[0]
            if g == 1:
                in_specs.append(pl.BlockSpec((None, 1, arr.shape[2]), lambda i: (0, 0, 0)))
            else:
                in_specs.append(pl.BlockSpec((None, 1, arr.shape[2]), lambda i: (grp(i), 0, 0)))
            args.append(arr)
    out_specs, out_shapes, is_red = [], [], []
    for spec in outs:
        if spec[0] == 't':
            w, dt = spec[1], spec[2]
            if len(spec) > 3:
                l, nl = spec[3], spec[4]
                out_specs.append(pl.BlockSpec((None, tr, w), functools.partial(lambda i, l: (l, i, 0), l=l)))
                out_shapes.append(jax.ShapeDtypeStruct((nl, rows, w), dt))
            else:
                out_specs.append(pl.BlockSpec((tr, w), lambda i: (i, 0)))
                out_shapes.append(jax.ShapeDtypeStruct((rows, w), dt))
            is_red.append(False)
        else:
            w, g = spec[1], spec[2]
            if g == 1:
                out_specs.append(pl.BlockSpec((None, 1, w), lambda i: (0, 0, 0)))
            else:
                out_specs.append(pl.BlockSpec((None, 1, w), lambda i: (grp(i), 0, 0)))
            out_shapes.append(jax.ShapeDtypeStruct((g, 1, w), F32))
            is_red.append(True)
    n_in = len(ins)
    n_alias = 0 if aliases is None else len(aliases)

    def body(*refs):
        in_refs = refs[:n_in]
        out_refs = refs[n_in + n_alias:]
        res = fn(*[r[...] for r in in_refs])
        if not isinstance(res, (tuple, list)):
            res = (res,)
        i = pl.program_id(0)
        first = (i == 0) if n0 is None else ((i == 0) | (i == n0))
        for o_ref, val, red in zip(out_refs, res, is_red):
            if not red:
                o_ref[...] = val.astype(o_ref.dtype)
            else:
                @pl.when(first)
                def _(o_ref=o_ref, val=val):
                    o_ref[...] = val

                @pl.when(jnp.logical_not(first))
                def _(o_ref=o_ref, val=val):
                    o_ref[...] += val

    io_alias = {}
    if aliases is not None:
        for a_idx, (arr, o_idx) in enumerate(aliases):
            in_specs.append(ANY)
            args.append(arr)
            io_alias[n_in + a_idx] = o_idx
    has_red = any(is_red)
    return pl.pallas_call(
        body, name=name, grid=(nt,), in_specs=in_specs, out_specs=out_specs, out_shape=out_shapes,
        input_output_aliases=io_alias,
        compiler_params=_params(("arbitrary",) if has_red else ("parallel",)),
    )(*args)


def _half_spec(tr, width, ax, n_tiles):
    if ax == 1:
        return pl.BlockSpec((tr, width), lambda i, sel: (sel[0] * n_tiles + i, 0))
    return pl.BlockSpec((tr, width), lambda i, sel: (i, sel[0]))


def _sum_pair(g, theirs, ax, ci, *, name):
    pr, pw = theirs.shape
    tr = _rows_per_tile(pr, pw)
    nt = pr // tr

    def body(sel, a_ref, b_ref, o_ref):
        o_ref[...] = (a_ref[...].astype(F32) + b_ref[...].astype(F32)).astype(o_ref.dtype)

    return pl.pallas_call(
        body, name=name,
        grid_spec=pltpu.PrefetchScalarGridSpec(
            num_scalar_prefetch=1, grid=(nt,),
            in_specs=[_half_spec(tr, pw, ax, nt), pl.BlockSpec((tr, pw), lambda i, sel: (i, 0))],
            out_specs=pl.BlockSpec((tr, pw), lambda i, sel: (i, 0))),
        out_shape=jax.ShapeDtypeStruct((pr, pw), BF16),
        compiler_params=_params(("parallel",)),
    )(jnp.reshape(ci, (1,)).astype(jnp.int32), g, theirs)


def _sum_chips_into(own, recv, ax, ci, *, name):
    pr, pw = own.shape
    tr = _rows_per_tile(pr, pw)
    nt = pr // tr
    full_shape = (2 * pr, pw) if ax == 1 else (pr, 2 * pw)

    def body(sel, a_ref, r_ref, o_ref):
        acc = a_ref[...].astype(F32)
        for k in range(N_CHIPS - 1):
            acc = acc + r_ref[k].astype(F32)
        o_ref[...] = acc

    return pl.pallas_call(
        body, name=name,
        grid_spec=pltpu.PrefetchScalarGridSpec(
            num_scalar_prefetch=1, grid=(nt,),
            in_specs=[pl.BlockSpec((tr, pw), lambda i, sel: (i, 0)), pl.BlockSpec((N_CHIPS - 1, tr, pw), lambda i, sel: (0, i, 0))],
            out_specs=_half_spec(tr, pw, ax, nt)),
        out_shape=jax.ShapeDtypeStruct(full_shape, F32),
        compiler_params=_params(("parallel",)),
    )(jnp.reshape(ci, (1,)).astype(jnp.int32), own, recv)


def _rsum(v):
    return jnp.sum(v, axis=0, keepdims=True)


def _silu_parts(z):
    sg = _sigmoid(z)
    return z * sg, sg * (1.0 + z * (1.0 - sg))


def _na_bias_table(rpb, rows, *, name):
    kh, kw = NA_WIN_ROWS, NA_WIN_COLS
    assert rows >= kh
    heads = rpb.shape[0]
    e1, e2 = _na_onehots()
    rpb16 = jnp.pad(rpb, ((0, 0), (0, 16 - rpb.shape[1]), (0, LANES - rpb.shape[2])))

    def body(r_ref, e1_ref, e2_ref, o_ref):
        e1b = e1_ref[...].astype(BF16)
        y = sum(_dot(e1b, part, 0, 0) for part in _split3(r_ref[...]))
        e2b = e2_ref[...].astype(BF16)
        o_ref[...] = sum(_dot(part, e2b, 1, 1) for part in _split3(y))

    z = pl.pallas_call(
        body, name=name, grid=(heads,),
        in_specs=[pl.BlockSpec((None, 16, LANES), lambda h: (h, 0, 0)),
                  pl.BlockSpec(e1.shape, lambda h: (0, 0)), pl.BlockSpec(e2.shape, lambda h: (0, 0))],
        out_specs=pl.BlockSpec((None, kh * kh, GRID_W * GRID_W), lambda h: (h, 0, 0)),
        out_shape=jax.ShapeDtypeStruct((heads, kh * kh, GRID_W * GRID_W), F32),
        compiler_params=_params(("parallel",)),
    )(rpb16, e1, e2)
    return z


def _na_bias_layout(z):
    heads = z.shape[0]
    kh, kw = NA_WIN_ROWS, NA_WIN_COLS
    cidx = np.arange(GRID_W)
    c0 = np.clip(cidx - kw // 2, 0, GRID_W - kw)
    col_in = (cidx[None, :] >= c0[:, None]) & (cidx[None, :] < c0[:, None] + kw)
    bias = z.reshape(heads, kh, kh, GRID_W, GRID_W).transpose(0, 1, 3, 2, 4)
    bias = jnp.where(col_in[None, None, :, None, :], bias, MASK_VALUE)
    return bias.reshape(heads, kh, GRID_W, kh * GRID_W)


def _na_onehots():
    kh, kw = NA_WIN_ROWS, NA_WIN_COLS
    cidx = np.arange(GRID_W)
    dc = cidx[None, :] - cidx[:, None] + (kw - 1)
    e2 = np.zeros((GRID_W * GRID_W, LANES), np.float32)
    ok = (dc >= 0) & (dc <= 2 * kw - 2)
    cq, ck = np.nonzero(ok)
    e2[cq * GRID_W + ck, dc[cq, ck]] = 1.0
    dr = np.arange(kh)[None, :] - np.arange(kh)[:, None] + (kh - 1)
    e1 = np.zeros((16, kh * kh), np.float32)
    dl, kr = np.nonzero(np.ones_like(dr))
    e1[dr[dl, kr], dl * kh + kr] = 1.0
    return jnp.asarray(e1), jnp.asarray(e2)


def _na_fwd(u, bias, *, s_len, heads, name):
    t_len = u.shape[0]
    rows = s_len // GRID_W
    nloc = NA_WIN_ROWS * GRID_W
    scale = NA_HEAD_DIM ** -0.5
    hd = NA_HEAD_DIM

    def body(q_ref, k_ref, v_ref, b_ref, o_ref):
        kc = k_ref[s_len:t_len, :]
        vc = v_ref[s_len:t_len, :]

        def group(g, carry):
            rs = [g * NA_GROUP + i for i in range(NA_GROUP)]
            r0s = [jnp.clip(r - NA_WIN_ROWS // 2, 0, rows - NA_WIN_ROWS) for r in rs]
            gs_ = pl.multiple_of(g * (NA_GROUP * GRID_W), NA_GROUP * GRID_W)
            kss = [pl.multiple_of(r0 * GRID_W, GRID_W) for r0 in r0s]
            q_all = q_ref[pl.ds(gs_, NA_GROUP * GRID_W), :]
            s_ctx = _dot(q_all, kc, 1, 1) * scale
            s_loc = [_dot(q_all[i * GRID_W:(i + 1) * GRID_W], k_ref[pl.ds(kss[i], nloc), :], 1, 1) * scale + b_ref[rs[i] - r0s[i]]
                     for i in range(NA_GROUP)]
            p_loc, p_ctx, inv = [], [], []
            for i in range(NA_GROUP):
                sc = s_ctx[i * GRID_W:(i + 1) * GRID_W]
                m = jnp.maximum(jnp.max(s_loc[i], axis=-1, keepdims=True), jnp.max(sc, axis=-1, keepdims=True))
                pl_, pc_ = jnp.exp(s_loc[i] - m), jnp.exp(sc - m)
                inv.append(1.0 / (jnp.sum(pl_, axis=-1, keepdims=True) + jnp.sum(pc_, axis=-1, keepdims=True)))
                p_loc.append(pl_.astype(BF16))
                p_ctx.append(pc_.astype(BF16))
            o_ctx = _dot(jnp.concatenate(p_ctx, axis=0), vc, 1, 0)
            o_loc = [_dot(p_loc[i], v_ref[pl.ds(kss[i], nloc), :], 1, 0) for i in range(NA_GROUP)]
            out = jnp.concatenate([(o_loc[i] + o_ctx[i * GRID_W:(i + 1) * GRID_W]) * inv[i] for i in range(NA_GROUP)], axis=0)
            o_ref[pl.ds(gs_, NA_GROUP * GRID_W), :] = out.astype(o_ref.dtype)
            return carry

        lax.fori_loop(0, rows // NA_GROUP, group, 0)
        qc = q_ref[s_len:t_len, :]
        s = _dot(qc, kc, 1, 1) * scale
        p = jnp.exp(s - jnp.max(s, axis=-1, keepdims=True))
        o = _dot(p.astype(BF16), vc, 1, 0) / jnp.sum(p, axis=-1, keepdims=True)
        o_ref[s_len:t_len, :] = o.astype(o_ref.dtype)

    col = lambda off: pl.BlockSpec((t_len, hd), functools.partial(lambda h, off: (0, off + h), off=off))
    return pl.pallas_call(
        body, name=name, grid=(heads,),
        in_specs=[col(0), col(heads), col(2 * heads),
                  pl.BlockSpec((None, NA_WIN_ROWS, GRID_W, nloc), lambda h: (h, 0, 0, 0))],
        out_specs=pl.BlockSpec((t_len, hd), lambda h: (0, h)),
        out_shape=jax.ShapeDtypeStruct((t_len, heads * hd), BF16),
        compiler_params=_params(("parallel",)),
    )(u, u, u, bias)


def _na_bwd(u, bias, o, do, *, s_len, heads, name):
    t_len = u.shape[0]
    rows = s_len // GRID_W
    nloc = NA_WIN_ROWS * GRID_W
    scale = NA_HEAD_DIM ** -0.5
    hd = NA_HEAD_DIM

    def body(q_ref, k_ref, v_ref, b_ref, o_ref, do_ref, dq_ref, dk_ref, dv_ref, db_ref, dk_acc, dv_acc):
        kc = k_ref[s_len:t_len, :]
        vc = v_ref[s_len:t_len, :]
        dk_acc[...] = jnp.zeros_like(dk_acc)
        dv_acc[...] = jnp.zeros_like(dv_acc)
        db_ref[...] = jnp.zeros_like(db_ref)

        def group(g, carry):
            n_g, rw = NA_GROUP, GRID_W
            rs = [g * n_g + i for i in range(n_g)]
            r0s = [jnp.clip(r - NA_WIN_ROWS // 2, 0, rows - NA_WIN_ROWS) for r in rs]
            dls = [r - r0 for r, r0 in zip(rs, r0s)]
            gs_ = pl.ds(pl.multiple_of(g * (n_g * rw), n_g * rw), n_g * rw)
            kss = [pl.ds(pl.multiple_of(r0 * rw, rw), nloc) for r0 in r0s]
            row_of = lambda a, i: a[i * rw:(i + 1) * rw]
            q_all, do_all = q_ref[gs_, :], do_ref[gs_, :]
            dlt_all = jnp.sum(do_all.astype(F32) * o_ref[gs_, :].astype(F32), axis=-1, keepdims=True)
            s_ctx = _dot(q_all, kc, 1, 1) * scale
            dp_ctx = _dot(do_all, vc, 1, 1)
            s_loc = [_dot(row_of(q_all, i), k_ref[kss[i], :], 1, 1) * scale + b_ref[dls[i]] for i in range(n_g)]
            dp_loc = [_dot(row_of(do_all, i), v_ref[kss[i], :], 1, 1) for i in range(n_g)]
            p_loc_b, ds_loc_b, p_ctx_b, ds_ctx_b = [], [], [], []
            for i in range(n_g):
                sc, dlt = row_of(s_ctx, i), row_of(dlt_all, i)
                m = jnp.maximum(jnp.max(s_loc[i], axis=-1, keepdims=True), jnp.max(sc, axis=-1, keepdims=True))
                pl_, pc_ = jnp.exp(s_loc[i] - m), jnp.exp(sc - m)
                inv = 1.0 / (jnp.sum(pl_, axis=-1, keepdims=True) + jnp.sum(pc_, axis=-1, keepdims=True))
                pl_, pc_ = pl_ * inv, pc_ * inv
                ds_l = pl_ * (dp_loc[i] - dlt)
                db_ref[dls[i]] += ds_l
                p_loc_b.append(pl_.astype(BF16))
                ds_loc_b.append(ds_l.astype(BF16))
                p_ctx_b.append(pc_.astype(BF16))
                ds_ctx_b.append((pc_ * (row_of(dp_ctx, i) - dlt)).astype(BF16))
            p_ctx_all, ds_ctx_all = jnp.concatenate(p_ctx_b, axis=0), jnp.concatenate(ds_ctx_b, axis=0)
            dq_ctx = _dot(ds_ctx_all, kc, 1, 0)
            dq_loc = [_dot(ds_loc_b[i], k_ref[kss[i], :], 1, 0) for i in range(n_g)]
            dk_loc = [_dot(ds_loc_b[i], row_of(q_all, i), 0, 0) for i in range(n_g)]
            dv_loc = [_dot(p_loc_b[i], row_of(do_all, i), 0, 0) for i in range(n_g)]
            dk_ctx = _dot(ds_ctx_all, q_all, 0, 0)
            dv_ctx = _dot(p_ctx_all, do_all, 0, 0)
            dq_ref[gs_, :] = ((jnp.concatenate(dq_loc, axis=0) + dq_ctx) * scale).astype(dq_ref.dtype)
            for i in range(n_g):
                dk_acc[kss[i], :] += dk_loc[i] * scale
                dv_acc[kss[i], :] += dv_loc[i]
            dk_acc[s_len:t_len, :] += dk_ctx * scale
            dv_acc[s_len:t_len, :] += dv_ctx
            return carry

        lax.fori_loop(0, rows // NA_GROUP, group, 0)
        qc = q_ref[s_len:t_len, :]
        dout = do_ref[s_len:t_len, :]
        out = o_ref[s_len:t_len, :]
        s = _dot(qc, kc, 1, 1) * scale
        p = jnp.exp(s - jnp.max(s, axis=-1, keepdims=True))
        p = p / jnp.sum(p, axis=-1, keepdims=True)
        dlt = jnp.sum(dout.astype(F32) * out.astype(F32), axis=-1, keepdims=True)
        ds = (p * (_dot(dout, vc, 1, 1) - dlt)).astype(BF16)
        dq_ref[s_len:t_len, :] = (_dot(ds, kc, 1, 0) * scale).astype(dq_ref.dtype)
        dk_acc[s_len:t_len, :] += _dot(ds, qc, 0, 0) * scale
        dv_acc[s_len:t_len, :] += _dot(p.astype(BF16), dout, 0, 0)
        dk_ref[...] = dk_acc[...].astype(dk_ref.dtype)
        dv_ref[...] = dv_acc[...].astype(dv_ref.dtype)

    col = lambda off: pl.BlockSpec((t_len, hd), functools.partial(lambda h, off: (0, off + h), off=off))
    tbl = pl.BlockSpec((None, NA_WIN_ROWS, GRID_W, nloc), lambda h: (h, 0, 0, 0))
    tok = jax.ShapeDtypeStruct((t_len, heads * hd), BF16)
    return pl.pallas_call(
        body, name=name, grid=(heads,),
        in_specs=[col(0), col(heads), col(2 * heads), tbl, col(0), col(0)],
        out_specs=[col(0), col(0), col(0), tbl],
        out_shape=[tok, tok, tok, jax.ShapeDtypeStruct(bias.shape, F32)],
        scratch_shapes=[pltpu.VMEM((t_len, hd), F32), pltpu.VMEM((t_len, hd), F32)],
        compiler_params=_params(("parallel",)),
    )(u, u, u, bias, o, do)


def _split3(x):
    hi = x.astype(BF16)
    r1 = x - hi.astype(F32)
    mid = r1.astype(BF16)
    lo = (r1 - mid.astype(F32)).astype(BF16)
    return hi, mid, lo


def _rpb_grad(dbias, *, name):
    heads = dbias.shape[0]
    kh = NA_WIN_ROWS
    e1, e2 = _na_onehots()
    x = dbias.reshape(heads, kh, GRID_W, kh, GRID_W).transpose(0, 1, 3, 2, 4).reshape(heads, kh * kh, GRID_W * GRID_W)

    def body(x_ref, e1_ref, e2_ref, o_ref):
        e2b = e2_ref[...].astype(BF16)
        y = sum(_dot(part, e2b, 1, 0) for part in _split3(x_ref[...]))
        e1b = e1_ref[...].astype(BF16)
        o_ref[...] = sum(_dot(e1b, part, 1, 0) for part in _split3(y))

    out = pl.pallas_call(
        body, name=name, grid=(heads,),
        in_specs=[pl.BlockSpec((None, kh * kh, GRID_W * GRID_W), lambda h: (h, 0, 0)),
                  pl.BlockSpec(e1.shape, lambda h: (0, 0)), pl.BlockSpec(e2.shape, lambda h: (0, 0))],
        out_specs=pl.BlockSpec((None, 16, LANES), lambda h: (h, 0, 0)),
        out_shape=jax.ShapeDtypeStruct((heads, 16, LANES), F32),
        compiler_params=_params(("parallel",)),
    )(x, e1, e2)
    return out[:, :2 * kh - 1, :2 * NA_WIN_COLS - 1]


def _rope_tables(s_len, l_len):
    nf = RET_KEY_DIM // 4
    t = np.arange(s_len)
    row = (t // GRID_W).astype(np.float32)
    colp = (t % GRID_W).astype(np.float32)
    inv_freq = jnp.asarray(ROPE_BASE, F32) ** (-jnp.arange(nf, dtype=F32) / nf)
    ang = jnp.concatenate([jnp.asarray(row)[:, None] * inv_freq, jnp.asarray(colp)[:, None] * inv_freq], axis=-1)
    cos, sin = jnp.cos(ang), jnp.sin(ang)
    c2 = jnp.concatenate([cos, cos], axis=-1)
    s2 = jnp.concatenate([-sin, sin], axis=-1)
    c2 = jnp.concatenate([c2, jnp.ones((l_len, RET_KEY_DIM), F32)], axis=0)
    s2 = jnp.concatenate([s2, jnp.zeros((l_len, RET_KEY_DIM), F32)], axis=0)
    return c2, s2


def _rope(x, c2, s2):
    return x * c2 + pltpu.roll(x, RET_KEY_DIM // 2, 1) * s2


def _rope_t(d, c2, s2):
    return d * c2 + pltpu.roll(d * s2, RET_KEY_DIM // 2, 1)


def _ret_decays(lg, direction):
    cs = RET_CHUNK
    i_col = lax.broadcasted_iota(jnp.int32, (cs, 1), 0)
    p_col = jnp.where(direction == 0, i_col, cs - 1 - i_col).astype(F32)
    pi = lax.broadcasted_iota(jnp.int32, (cs, cs), 0)
    pj = lax.broadcasted_iota(jnp.int32, (cs, cs), 1)
    diff = jnp.where(direction == 0, pi - pj, pj - pi).astype(F32)
    dm = jnp.where(diff >= 0, jnp.exp(jnp.maximum(diff, 0.0) * lg), 0.0)
    qdec = jnp.exp((p_col + 1.0) * lg)
    kdec = jnp.exp((cs - 1.0 - p_col) * lg)
    cd = jnp.exp(jnp.full((1, 1), cs, F32) * lg)
    return p_col, dm, qdec, kdec, cd


def _ret_chunk_index(t, direction, n_chunks, lat_chunks):
    return jnp.where(direction == 0, lax.rem(t + lat_chunks, n_chunks), n_chunks - 1 - t)


def _ret_fwd(u, c2, s2, lg, *, s_len, heads, q_off, name):
    t_len = u.shape[0]
    cs, dk, dv = RET_CHUNK, RET_KEY_DIM, RET_VAL_DIM
    n_chunks, lat_chunks = t_len // cs, s_len // cs
    k_scale = dk ** -0.5
    qb, kb, vb = q_off // dk, q_off // dk + heads, (q_off + 2 * heads * dk) // dv

    def body(lg_ref, q_ref, k_ref, v_ref, c_ref, s_ref, o_ref, st_ref, qd_s, kv_s):
        h, d = pl.program_id(0), pl.program_id(1)
        _, dm, qdec, kdec, cd = _ret_decays(lg_ref[d, h], d)
        n_g = max(g for g in RET_GROUPS if n_chunks % g == 0)
        rows_of = lambda c: pl.ds(pl.multiple_of(c * cs, cs), cs)

        def local(gi, carry):
            rws = [rows_of(gi * n_g + j) for j in range(n_g)]
            qcs = [_rope(q_ref[r, :].astype(F32), c_ref[r, :], s_ref[r, :]) for r in rws]
            kcs = [_rope(k_ref[r, :].astype(F32), c_ref[r, :], s_ref[r, :]) * k_scale for r in rws]
            vcs = [v_ref[r, :] for r in rws]
            a_raw = [_dot(qcs[j].astype(BF16), kcs[j].astype(BF16), 1, 1) for j in range(n_g)]
            kv = [_dot((kcs[j] * kdec).astype(BF16), vcs[j], 0, 0) for j in range(n_g)]
            inner = [_dot((a_raw[j] * dm).astype(BF16), vcs[j], 1, 0) for j in range(n_g)]
            for j in range(n_g):
                qd_s[rws[j], :] = (qcs[j] * qdec).astype(BF16)
                kv_s[gi * n_g + j] = kv[j]

            @pl.when(d == 0)
            def _():
                for j in range(n_g):
                    o_ref[rws[j], :] = inner[j]

            @pl.when(d == 1)
            def _():
                for j in range(n_g):
                    o_ref[rws[j], :] += inner[j]

            return carry

        lax.fori_loop(0, n_chunks // n_g, local, 0)

        def scan(t, st):
            st_ref[t] = st
            return st * cd + kv_s[_ret_chunk_index(t, d, n_chunks, lat_chunks)]

        lax.fori_loop(0, n_chunks, scan, jnp.zeros((dk, dv), F32))

        def cross(gi, carry):
            ts = [gi * n_g + j for j in range(n_g)]
            rws = [rows_of(_ret_chunk_index(t, d, n_chunks, lat_chunks)) for t in ts]
            outs = [_dot(qd_s[rws[j], :], st_ref[ts[j]].astype(BF16), 1, 0) for j in range(n_g)]
            for j in range(n_g):
                o_ref[rws[j], :] += outs[j]
            return carry

        lax.fori_loop(0, n_chunks // n_g, cross, 0)

    return pl.pallas_call(
        body, name=name, grid=(heads, 2),
        in_specs=[pl.BlockSpec(memory_space=pltpu.SMEM),
                  pl.BlockSpec((t_len, dk), lambda h, d: (0, qb + h)),
                  pl.BlockSpec((t_len, dk), lambda h, d: (0, kb + h)),
                  pl.BlockSpec((t_len, dv), lambda h, d: (0, vb + h)),
                  pl.BlockSpec((t_len, dk), lambda h, d: (0, 0)),
                  pl.BlockSpec((t_len, dk), lambda h, d: (0, 0))],
        out_specs=[pl.BlockSpec((t_len, dv), lambda h, d: (0, h)),
                   pl.BlockSpec((None, None, n_chunks, dk, dv), lambda h, d: (h, d, 0, 0, 0))],
        out_shape=[jax.ShapeDtypeStruct((t_len, heads * dv), F32),
                   jax.ShapeDtypeStruct((heads, 2, n_chunks, dk, dv), F32)],
        scratch_shapes=[pltpu.VMEM((t_len, dk), BF16), pltpu.VMEM((n_chunks, dk, dv), F32)],
        compiler_params=_params(("parallel", "arbitrary")),
    )(lg, u, u, u, c2, s2)


def _ret_bwd(u, c2, s2, lg, states, do, *, s_len, heads, q_off, name):
    t_len = u.shape[0]
    cs, dk, dv = RET_CHUNK, RET_KEY_DIM, RET_VAL_DIM
    n_chunks, lat_chunks = t_len // cs, s_len // cs
    k_scale = dk ** -0.5
    qb, kb, vb = q_off // dk, q_off // dk + heads, (q_off + 2 * heads * dk) // dv

    def body(lg_ref, q_ref, k_ref, v_ref, c_ref, s_ref, st_ref, do_ref, dq_ref, dk_ref, dv_ref, dlg_ref, acc, qdo_s, dst_s):
        h, d = pl.program_id(0), pl.program_id(1)
        p_col, dm, qdec, kdec, cd = _ret_decays(lg_ref[d, h], d)
        acc[...] = jnp.zeros_like(acc)
        n_g = max(g for g in RET_GROUPS[:2] if n_chunks % g == 0)
        rows_of = lambda c: pl.ds(pl.multiple_of(c * cs, cs), cs)
        chunk_of = lambda t: _ret_chunk_index(t, d, n_chunks, lat_chunks)

        def local(gi, carry):
            rws = [rows_of(gi * n_g + j) for j in range(n_g)]
            qds = [(_rope(q_ref[r, :].astype(F32), c_ref[r, :], s_ref[r, :]) * qdec).astype(BF16) for r in rws]
            prods = [_dot(qds[j], do_ref[rws[j], :].astype(BF16), 0, 0) for j in range(n_g)]
            for j in range(n_g):
                qdo_s[gi * n_g + j] = prods[j]
            return carry

        lax.fori_loop(0, n_chunks // n_g, local, 0)

        def scan(i, dst):
            t = n_chunks - 1 - i
            dst_s[t] = dst
            return dst * cd + qdo_s[chunk_of(t)]

        lax.fori_loop(0, n_chunks, scan, jnp.zeros((dk, dv), F32))

        def grads(gi, carry):
            ts = [gi * n_g + j for j in range(n_g)]
            rws = [rows_of(chunk_of(t)) for t in ts]
            ccs, sss = [c_ref[r, :] for r in rws], [s_ref[r, :] for r in rws]
            qcs = [_rope(q_ref[r, :].astype(F32), cc, ss) for r, cc, ss in zip(rws, ccs, sss)]
            kcs = [_rope(k_ref[r, :].astype(F32), cc, ss) * k_scale for r, cc, ss in zip(rws, ccs, sss)]
            vcs = [v_ref[r, :] for r in rws]
            docs = [do_ref[r, :].astype(BF16) for r in rws]
            sts = [st_ref[t] for t in ts]
            dsts = [dst_s[t] for t in ts]
            q16 = [x.astype(BF16) for x in qcs]
            k16 = [x.astype(BF16) for x in kcs]
            dst16 = [x.astype(BF16) for x in dsts]
            rng = range(n_g)
            a_raw = [_dot(q16[j], k16[j], 1, 1) for j in rng]
            da_raw = [_dot(docs[j], vcs[j], 1, 1) for j in rng]
            dq_c = [_dot(docs[j], sts[j].astype(BF16), 1, 1) * qdec for j in rng]
            dv_s = [_dot((kcs[j] * kdec).astype(BF16), dst16[j], 1, 0) for j in rng]
            dk_s = [_dot(vcs[j], dst16[j], 1, 1) * kdec for j in rng]
            a16 = [(a_raw[j] * dm).astype(BF16) for j in rng]
            dam = [(da_raw[j] * dm).astype(BF16) for j in rng]
            dq_i = [_dot(dam[j], k16[j], 1, 0) for j in rng]
            dk_i = [_dot(dam[j], q16[j], 0, 0) for j in rng]
            dv_i = [_dot(a16[j], docs[j], 0, 0) for j in rng]
            for j in rng:
                g = (jnp.sum(qcs[j] * (p_col * dq_i[j] + (p_col + 1.0) * dq_c[j]), axis=-1, keepdims=True)
                     + jnp.sum(kcs[j] * ((cs - 1.0 - p_col) * dk_s[j] - p_col * dk_i[j]), axis=-1, keepdims=True))
                g = (jnp.sum(g, axis=0, keepdims=True)
                     + cs * cd * jnp.sum(jnp.sum(dsts[j] * sts[j], axis=-1, keepdims=True), axis=0, keepdims=True))
                acc[...] += jnp.broadcast_to(g, acc.shape)
            dqs = [_rope_t(dq_i[j] + dq_c[j], ccs[j], sss[j]) for j in rng]
            dks = [_rope_t((dk_i[j] + dk_s[j]) * k_scale, ccs[j], sss[j]) for j in rng]
            dvs = [dv_i[j] + dv_s[j] for j in rng]

            @pl.when(d == 0)
            def _():
                for j in rng:
                    dq_ref[rws[j], :] = dqs[j].astype(dq_ref.dtype)
                    dk_ref[rws[j], :] = dks[j].astype(dk_ref.dtype)
                    dv_ref[rws[j], :] = dvs[j].astype(dv_ref.dtype)

            @pl.when(d == 1)
            def _():
                for j in rng:
                    dq_ref[rws[j], :] = (dq_ref[rws[j], :].astype(F32) + dqs[j]).astype(dq_ref.dtype)
                    dk_ref[rws[j], :] = (dk_ref[rws[j], :].astype(F32) + dks[j]).astype(dk_ref.dtype)
                    dv_ref[rws[j], :] = (dv_ref[rws[j], :].astype(F32) + dvs[j]).astype(dv_ref.dtype)

            return carry

        lax.fori_loop(0, n_chunks // n_g, grads, 0)
        dlg_ref[...] = acc[...]

    return pl.pallas_call(
        body, name=name, grid=(heads, 2),
        in_specs=[pl.BlockSpec(memory_space=pltpu.SMEM),
                  pl.BlockSpec((t_len, dk), lambda h, d: (0, qb + h)),
                  pl.BlockSpec((t_len, dk), lambda h, d: (0, kb + h)),
                  pl.BlockSpec((t_len, dv), lambda h, d: (0, vb + h)),
                  pl.BlockSpec((t_len, dk), lambda h, d: (0, 0)),
                  pl.BlockSpec((t_len, dk), lambda h, d: (0, 0)),
                  pl.BlockSpec((None, None, n_chunks, dk, dv), lambda h, d: (h, d, 0, 0, 0)),
                  pl.BlockSpec((t_len, dv), lambda h, d: (0, h))],
        out_specs=[pl.BlockSpec((t_len, dk), lambda h, d: (0, h)),
                   pl.BlockSpec((t_len, dk), lambda h, d: (0, h)),
                   pl.BlockSpec((t_len, dv), lambda h, d: (0, h)),
                   pl.BlockSpec((None, None, 8, LANES), lambda h, d: (h, d, 0, 0))],
        out_shape=[jax.ShapeDtypeStruct((t_len, heads * dk), BF16),
                   jax.ShapeDtypeStruct((t_len, heads * dk), BF16),
                   jax.ShapeDtypeStruct((t_len, heads * dv), BF16),
                   jax.ShapeDtypeStruct((heads, 2, 8, LANES), F32)],
        scratch_shapes=[pltpu.VMEM((8, LANES), F32), pltpu.VMEM((n_chunks, dk, dv), F32), pltpu.VMEM((n_chunks, dk, dv), F32)],
        compiler_params=_params(("parallel", "arbitrary")),
    )(lg, u, u, u, c2, s2, states, do)


def _mesh_pos():
    return lax.axis_index("x"), lax.axis_index("y"), lax.axis_index("c")


def _all_gather_small(buf, *, name):
    r = buf.shape[0]

    def body(x_ref, o_ref, send_sems, recv_sems, local_sem):
        x, y, c = _mesh_pos()
        me = 4 * x + 2 * y + c
        mine = pltpu.make_async_copy(x_ref, o_ref.at[me], local_sem)
        mine.start()
        copies = []
        for k in range(1, N_DEV):
            px, py, pc = x ^ ((k >> 2) & 1), y ^ ((k >> 1) & 1), c ^ (k & 1)
            cp = pltpu.make_async_remote_copy(
                src_ref=x_ref, dst_ref=o_ref.at[me], send_sem=send_sems.at[k - 1], recv_sem=recv_sems.at[k - 1],
                device_id=(px, py, pc), device_id_type=MESH)
            cp.start()
            copies.append((cp, 4 * px + 2 * py + pc))
        for k, (cp, peer) in enumerate(copies):
            pltpu.make_async_remote_copy(
                src_ref=x_ref, dst_ref=o_ref.at[peer], send_sem=send_sems.at[k], recv_sem=recv_sems.at[k],
                device_id=(x, y, c), device_id_type=MESH).wait_recv()
        for cp, _ in copies:
            cp.wait_send()
        mine.wait()

    return pl.pallas_call(
        body, name=name,
        in_specs=[pl.BlockSpec(memory_space=pltpu.VMEM)],
        out_specs=pl.BlockSpec(memory_space=pltpu.VMEM),
        out_shape=jax.ShapeDtypeStruct((N_DEV, r, LANES), F32),
        scratch_shapes=[pltpu.SemaphoreType.DMA((N_DEV - 1,)), pltpu.SemaphoreType.DMA((N_DEV - 1,)),
                        pltpu.SemaphoreType.DMA],
        compiler_params=pltpu.CompilerParams(vmem_limit_bytes=VMEM_LIMIT),
    )(buf)


def _cut(ref, shard_axis, *, chip=None, half=None, lead=None):
    shape = ref.shape[1:] if lead is not None else ref.shape
    idx = [slice(None), slice(None)]
    if chip is not None:
        w = shape[shard_axis] // N_CHIPS
        idx[shard_axis] = pl.ds(pl.multiple_of(chip * w, w), w)
    if half is not None:
        hw = shape[1 - shard_axis] // 2
        idx[1 - shard_axis] = pl.ds(pl.multiple_of(half * hw, hw), hw)
    if lead is not None:
        idx = [lead] + idx
    return ref.at[tuple(idx)]


def _wait_recv(ref, send_sem, recv_sem):
    pltpu.make_async_remote_copy(src_ref=ref, dst_ref=ref, send_sem=send_sem, recv_sem=recv_sem,
                                 device_id=_mesh_pos(), device_id_type=MESH).wait_recv()


def _gather_plan(axes):
    def plan(srcs, lands, send_sems, recv_sems):
        x, y, c = _mesh_pos()
        chip = 2 * x + y
        copies = []
        for i, ax in enumerate(axes):
            for k in range(1, N_CHIPS):
                px, py = x ^ (k >> 1), y ^ (k & 1)
                mine = _cut(lands[i], ax, chip=chip, half=c)
                j = i * (N_CHIPS - 1) + k - 1
                sems = dict(send_sem=send_sems.at[j], recv_sem=recv_sems.at[j], device_id=(px, py, c), device_id_type=MESH)
                send = pltpu.make_async_remote_copy(src_ref=mine, dst_ref=mine, **sems)
                recv = pltpu.make_async_remote_copy(src_ref=mine, dst_ref=_cut(lands[i], ax, chip=2 * px + py, half=c), **sems)
                copies.append((send, recv))
        return copies
    return plan


def _gather_near_plan(axes):
    def plan(srcs, lands, send_sems, recv_sems):
        x, y, c = _mesh_pos()
        copies = []
        for i, ax in enumerate(axes):
            mine = _cut(lands[i], ax, chip=2 * x + y, half=c)
            for k, (px, py) in enumerate(((1 - x, y), (x, 1 - y))):
                sems = dict(send_sem=send_sems.at[2 * i + k], recv_sem=recv_sems.at[2 * i + k], device_id=(px, py, c), device_id_type=MESH)
                send = pltpu.make_async_remote_copy(src_ref=mine, dst_ref=mine, **sems)
                recv = pltpu.make_async_remote_copy(src_ref=mine, dst_ref=_cut(lands[i], ax, chip=2 * px + py, half=c), **sems)
                copies.append((send, recv))
        return copies
    return plan


def _gather_far_plan(axes):
    def plan(srcs, lands, send_sems, recv_sems):
        x, y, c = _mesh_pos()
        from_chip = 2 * (x ^ (1 - c)) + (y ^ c)
        to = (x ^ c, y ^ (1 - c), c)
        diag = 2 * (1 - x) + (1 - y)
        copies = []
        for i, ax in enumerate(axes):
            passed = _cut(lands[i], ax, chip=from_chip, half=c)
            sems = dict(send_sem=send_sems.at[i], recv_sem=recv_sems.at[i], device_id=to, device_id_type=MESH)
            send = pltpu.make_async_remote_copy(src_ref=passed, dst_ref=passed, **sems)
            recv = pltpu.make_async_remote_copy(src_ref=passed, dst_ref=_cut(lands[i], ax, chip=diag, half=c), **sems)
            copies.append((send, recv))
        return copies
    return plan


def _pair_plan(axes):
    def plan(srcs, lands, send_sems, recv_sems):
        x, y, c = _mesh_pos()
        copies = []
        for i, ax in enumerate(axes):
            cp = pltpu.make_async_remote_copy(
                src_ref=_cut(srcs[i], ax, half=1 - c), dst_ref=lands[i], send_sem=send_sems.at[i], recv_sem=recv_sems.at[i],
                device_id=(x, y, 1 - c), device_id_type=MESH)
            copies.append((cp, cp))
        return copies
    return plan


def _scatter_plan(axes):
    def plan(srcs, lands, send_sems, recv_sems):
        x, y, c = _mesh_pos()
        copies = []
        for i, ax in enumerate(axes):
            for k in range(1, N_CHIPS):
                px, py = x ^ (k >> 1), y ^ (k & 1)
                j = i * (N_CHIPS - 1) + k - 1
                cp = pltpu.make_async_remote_copy(
                    src_ref=_cut(srcs[i], ax, chip=2 * px + py), dst_ref=lands[i].at[k - 1],
                    send_sem=send_sems.at[j], recv_sem=recv_sems.at[j], device_id=(px, py, c), device_id_type=MESH)
                copies.append((cp, cp))
        return copies
    return plan


HBM = pl.BlockSpec(memory_space=pltpu.HBM)
SEM = pl.BlockSpec(memory_space=pltpu.SEMAPHORE)
EFFECT = pltpu.SideEffectType.DATAFLOW_SIDE_EFFECTING


def _in_hbm(arrays):
    return [pltpu.with_memory_space_constraint(a, pltpu.HBM) for a in arrays]


def _split_start(srcs, lands, plan, n_copies, *, name):
    bufs = list(srcs) + list(lands)
    ns, nb = len(srcs), len(bufs)

    def body(*refs):
        send_sems, recv_sems, token = refs[nb], refs[nb + 1], refs[-1]
        for send, _ in plan(refs[:ns], refs[ns:nb], send_sems, recv_sems):
            send.start()
        token[...] = jnp.zeros_like(token)

    sems = pltpu.SemaphoreType.DMA((n_copies,))
    res = pl.pallas_call(
        body, name=name, in_specs=[HBM] * nb,
        out_specs=[SEM, SEM] + [HBM] * nb + [pl.BlockSpec(memory_space=pltpu.VMEM)],
        out_shape=[sems, sems] + [pltpu.HBM(a.shape, a.dtype) for a in bufs] + [jax.ShapeDtypeStruct((8, LANES), F32)],
        input_output_aliases={j: 2 + j for j in range(nb)},
        compiler_params=pltpu.CompilerParams(has_side_effects=EFFECT),
    )(*_in_hbm(bufs))
    return res[0], res[1], res[2:2 + ns], res[2 + ns:2 + nb], res[-1]


def _split_wait(started, after, plan, *, name, with_srcs=False):
    send_sems, recv_sems, srcs, lands, _ = started
    bufs = list(srcs) + list(lands)
    ns, nb = len(srcs), len(bufs)

    def body(*refs):
        for send, recv in plan(refs[:ns], refs[ns:nb], refs[nb], refs[nb + 1]):
            send.wait_send()
            recv.wait_recv()

    res = pl.pallas_call(
        body, name=name, in_specs=[HBM] * nb + [SEM, SEM, ANY], out_specs=[HBM] * nb,
        out_shape=[pltpu.HBM(a.shape, a.dtype) for a in bufs],
        input_output_aliases={j: j for j in range(nb)},
        compiler_params=pltpu.CompilerParams(has_side_effects=EFFECT),
    )(*bufs, send_sems, recv_sems, after)
    return (res[:ns], res[ns:]) if with_srcs else res[ns:]


def _cast_into_full(w3, layer, ax, chip, *, after=None, name):
    _, r, wd = w3.shape
    tr = _rows_per_tile(r, wd, 4 << 20)
    nt = r // tr
    full_shape = (r, wd * N_CHIPS) if ax == 1 else (r * N_CHIPS, wd)
    out_map = (lambda i, ch: (i, ch[0])) if ax == 1 else (lambda i, ch: (ch[0] * nt + i, 0))
    zero = jnp.zeros((1, wd), F32) + (0.0 if after is None else after)

    def body(chip_ref, w_ref, z_ref, o_ref):
        o_ref[...] = (w_ref[...] + z_ref[...]).astype(o_ref.dtype)

    return pl.pallas_call(
        body, name=name,
        grid_spec=pltpu.PrefetchScalarGridSpec(
            num_scalar_prefetch=1, grid=(nt,),
            in_specs=[pl.BlockSpec((None, tr, wd), lambda i, ch: (layer, i, 0)), pl.BlockSpec((1, wd), lambda i, ch: (0, 0))],
            out_specs=pl.BlockSpec((tr, wd), out_map)),
        out_shape=jax.ShapeDtypeStruct(full_shape, BF16),
        compiler_params=_params(("parallel",)),
    )(jnp.reshape(chip, (1,)).astype(jnp.int32), w3, zero)


def _forward_halves(fulls, axes, *, name):
    n = len(fulls)

    def body(*refs):
        bufs = refs[:n]
        send_sems, recv_sems = refs[2 * n:]
        x, y, c = _mesh_pos()
        sends = []
        for i in range(n):
            for k in range(1, N_CHIPS):
                landed = _cut(bufs[i], axes[i], chip=2 * (x ^ (k >> 1)) + (y ^ (k & 1)), half=c)
                cp = pltpu.make_async_remote_copy(
                    src_ref=landed, dst_ref=landed, send_sem=send_sems.at[i, k - 1], recv_sem=recv_sems.at[i, k - 1],
                    device_id=(x, y, 1 - c), device_id_type=MESH)
                cp.start()
                sends.append(cp)
        for i in range(n):
            for k in range(1, N_CHIPS):
                other = _cut(bufs[i], axes[i], chip=2 * (x ^ (k >> 1)) + (y ^ (k & 1)), half=1 - c)
                _wait_recv(other, send_sems.at[i, k - 1], recv_sems.at[i, k - 1])
        for cp in sends:
            cp.wait_send()

    pairs = pltpu.SemaphoreType.DMA((n, N_CHIPS - 1))
    return pl.pallas_call(
        body, name=name, in_specs=[ANY] * n, out_specs=[ANY] * n,
        out_shape=[jax.ShapeDtypeStruct(a.shape, a.dtype) for a in fulls],
        input_output_aliases={j: j for j in range(n)},
        scratch_shapes=[pairs, pairs],
    )(*fulls)


def _share_halves_in_place(bufs, axes, *, name):
    n = len(bufs)

    def body(*refs):
        ins = refs[:n]
        send_sems, recv_sems = refs[2 * n:]
        x, y, c = _mesh_pos()
        sends = []
        for i in range(n):
            mine = _cut(ins[i], axes[i], half=c)
            cp = pltpu.make_async_remote_copy(
                src_ref=mine, dst_ref=mine, send_sem=send_sems.at[i], recv_sem=recv_sems.at[i],
                device_id=(x, y, 1 - c), device_id_type=MESH)
            cp.start()
            sends.append(cp)
        for i in range(n):
            _wait_recv(_cut(ins[i], axes[i], half=1 - c), send_sems.at[i], recv_sems.at[i])
        for cp in sends:
            cp.wait_send()

    sems = pltpu.SemaphoreType.DMA((n,))
    return pl.pallas_call(
        body, name=name, in_specs=[ANY] * n, out_specs=[ANY] * n,
        out_shape=[jax.ShapeDtypeStruct(b.shape, b.dtype) for b in bufs],
        input_output_aliases={j: j for j in range(n)}, scratch_shapes=[sems, sems],
    )(*bufs)


def _adamw_math(w, g, m, v):
    m = ADAM_B1 * m + (1.0 - ADAM_B1) * g
    v = ADAM_B2 * v + (1.0 - ADAM_B2) * (g * g)
    m_hat = m / (1.0 - ADAM_B1 ** ADAM_STEP)
    v_hat = v / (1.0 - ADAM_B2 ** ADAM_STEP)
    delta = -ADAM_LR * (m_hat / (jnp.sqrt(v_hat) + ADAM_EPS) + ADAM_WD * w)
    return delta, m, v


def _adamw_layer(w3, m3, v3, p, q, layer, prev, *, name):
    nl, rows, width = w3.shape
    tr = _rows_per_tile(rows, width)

    def fn(*t):
        if q is None:
            w, m, v, g = t
        else:
            w, m, v, g, g2 = t
            g = g + g2
        delta, m, v = _adamw_math(w, g, m, v)
        return g, delta, m, v

    ins = [('t', w3, 0, width, layer), ('t', m3, 0, width, layer), ('t', v3, 0, width, layer), ('t', p, 0, width)]
    if q is not None:
        ins.append(('t', q, 0, width))
    outs = [('t', width, F32, layer, nl)] * 4
    aliases = None if prev is None else [(prev[i], i) for i in range(4)]
    return _ew(fn, ins, outs, rows=rows, tr=tr, name=name, aliases=aliases)


def _pack_rows(vec):
    n = vec.shape[0]
    r = -(-n // (8 * LANES)) * 8
    return jnp.pad(vec, (0, r * LANES - n)).reshape(r, LANES)


def kernel(x, c, ctx, c_ctx, ada_w, ada_b, norm_g, w_in, na_rpb, ret_decay_logit, w_proj_na, w_proj_ret, w_out, final_g, loss_target, m_c_ctx, m_ada_w, m_ada_b, m_norm_g, m_w_in, m_na_rpb, m_ret_decay_logit, m_w_proj_na, m_w_proj_ret, m_w_out, m_final_g, v_c_ctx, v_ada_w, v_ada_b, v_norm_g, v_w_in, v_na_rpb, v_ret_decay_logit, v_w_proj_na, v_w_proj_ret, v_w_out, v_final_g):
    depth = w_in.shape[0]
    s_len, d_model = x.shape[1], x.shape[2]
    l_len = ctx.shape[1]
    t_len = s_len + l_len
    na_heads = na_rpb.shape[1]
    ret_heads = ret_decay_logit.shape[2]
    w_na = na_heads * NA_HEAD_DIM
    w_qk = ret_heads * RET_KEY_DIM
    w_v = ret_heads * RET_VAL_DIM
    in_cols = w_in.shape[2] * N_CHIPS
    assert in_cols == 4 * w_na + 2 * w_qk + 2 * w_v + 2 * d_model
    assert x.shape[0] == 1 and s_len % (NA_WIN_ROWS * GRID_W) == 0 and l_len % RET_CHUNK == 0
    off = np.cumsum([0, w_na, w_na, w_na, w_na, w_qk, w_qk, w_v, w_v, d_model, d_model])
    o_naz, o_retq, o_retz, o_gna, o_gret = int(off[3]), int(off[4]), int(off[7]), int(off[8]), int(off[9])
    rows = s_len // GRID_W
    tr = _tile(l_len, 256, 8)
    n0 = s_len // tr
    mod_cols = 3 * d_model
    mod_shard = ada_w.shape[2]

    xi, yi, ci = _mesh_pos()
    me = 4 * xi + 2 * yi + ci
    chip = 2 * xi + yi

    big_axes = [1, 1, 0, 0]
    n_big = len(big_axes) * (N_CHIPS - 1)
    gather_plan, scatter_plan = _gather_plan(big_axes), _scatter_plan(big_axes)

    c_silu = c[0] * _sigmoid(c[0])
    cc_silu = c_ctx * _sigmoid(c_ctx)
    c_all = _all_gather_small(_pack_rows(c_silu), name="gather_c")[:, :d_model // LANES].reshape(N_DEV, d_model)
    a_rows = jnp.concatenate([c_all, cc_silu[None], jnp.zeros((16 - N_DEV - 1, d_model), F32)], axis=0)
    mod_part = jnp.stack([_mm(a_rows, ada_w, b_lead=l, out_dtype=F32, name="ada_fwd_%d" % l) for l in range(depth)])
    mod_all = _all_gather_small(_pack_rows(mod_part.reshape(-1)), name="gather_mod")
    n_mod = depth * 16 * mod_shard
    mod_all = mod_all.reshape(N_DEV, -1)[:, :n_mod].reshape(N_CHIPS, 2, depth, 16, mod_shard)[:, 0]
    mod_all = jnp.transpose(mod_all, (1, 2, 0, 3)).reshape(depth, 16, mod_cols) + ada_b[:, None, :]

    big_named = list(zip((w_in, w_proj_na, w_proj_ret, w_out), big_axes, ("w_in", "w_proj_na", "w_proj_ret", "w_out")))
    w_in0 = _cast_into_full(w_in, 0, big_axes[0], chip, name="cast_w_in_0")
    mod_all, w_in0 = lax.optimization_barrier((mod_all, w_in0))
    plan_near, plan_far, plan_rest = _gather_near_plan(big_axes[:1]), _gather_far_plan(big_axes[:1]), _gather_plan(big_axes[1:])
    near_all, far_all = _gather_near_plan(big_axes), _gather_far_plan(big_axes)
    first_gather = _split_start([], [w_in0], plan_near, 2, name="gather_start_0_in")
    start_token = first_gather[4][0, 0]
    fulls = [[None if (l == 0 and tag == "w_in") else _cast_into_full(w, l, ax, chip, after=start_token, name="cast_%s_%d" % (tag, l))
              for w, ax, tag in big_named] for l in range(depth)]
    mod_lat = lax.dynamic_index_in_dim(mod_all, me, axis=1, keepdims=False)
    mod_ctx = mod_all[:, N_DEV]
    bias_z = [_na_bias_table(na_rpb[l], s_len // GRID_W, name="na_bias_%d" % l) for l in range(depth)]
    bias_z, fulls = lax.optimization_barrier((bias_z, fulls))
    landed_near = _split_wait(first_gather, bias_z[-1], plan_near, name="gather_wait_0_in")
    passing = _split_start([], landed_near, plan_far, 1, name="gather_pass_0_in")
    front_token = passing[4][0, 0]

    c2, s2 = _rope_tables(s_len, l_len)
    log_gamma = jax.nn.log_sigmoid(ret_decay_logit)
    x_all = jnp.concatenate([x[0], ctx[0]], axis=0)

    def grp(lat_vec, ctx_vec):
        return jnp.stack([lat_vec, ctx_vec])[:, None, :]

    saved, full_w = [], []
    for l in range(depth):
        shift, scale, gate = [grp(mod_lat[l, i * d_model:(i + 1) * d_model], mod_ctx[l, i * d_model:(i + 1) * d_model])
                              for i in range(3)]
        z_l, token_l = lax.optimization_barrier((bias_z[l], front_token))
        gs = norm_g[l][None, None, :] * (1.0 + scale) + token_l

        def modnorm(xt, gs_t, sh_t):
            r = lax.rsqrt(jnp.mean(xt * xt, axis=-1, keepdims=True) + NORM_EPS)
            return xt * r * gs_t + sh_t

        h, = _ew(modnorm, [('t', x_all, 0, d_model), ('g', gs), ('g', shift)], [('t', d_model, BF16)],
                 rows=t_len, tr=tr, n0=n0, name="modnorm_%d" % l)
        bias = _na_bias_layout(z_l)
        if l == 0:
            h, bias = lax.optimization_barrier((h, bias))
            landed_in = _split_wait(passing, h, plan_far, name="gather_wait_0_in_far")
            landed_in, rest0, later = lax.optimization_barrier((landed_in, fulls[0][1:], fulls[1:]))
            rest_gather = _split_start([], rest0, plan_rest, n_big - (N_CHIPS - 1), name="gather_start_0_rest")
            later_gathers = [_split_start([], later[j], near_all, 2 * len(big_axes), name="gather_start_%d" % (j + 1))
                             for j in range(depth - 1)]
            win_f, = _forward_halves(landed_in, big_axes[:1], name="gather_forward_0_in")
            win_f, tokens = lax.optimization_barrier((win_f, [rest_gather[4]] + [g[4] for g in later_gathers]))
            gate = gate + sum(t[0, 0] for t in tokens)
        else:
            h, bias = lax.optimization_barrier((h, bias))
            landed = _split_wait(later_passes[l - 1], h, far_all, name="gather_wait_%d" % l)
            win_f, wpn_f, wpr_f, wout_f = _forward_halves(landed, big_axes, name="gather_forward_%d" % l)
        u = _mm(h, win_f, tm=1152, tn=1024, name="in_proj_%d" % l)
        o_na = _na_fwd(u, bias, s_len=s_len, heads=na_heads, name="na_fwd_%d" % l)
        o_ret, states = _ret_fwd(u, c2, s2, log_gamma[l], s_len=s_len, heads=ret_heads, q_off=o_retq, name="ret_fwd_%d" % l)

        def act(o1, z1, o2, z2):
            a1 = o1.astype(F32) * _silu_parts(z1.astype(F32))[0]
            sz = _silu_parts(z2.astype(F32))[0]
            outs = []
            for hh in range(ret_heads):
                sl = slice(hh * RET_VAL_DIM, (hh + 1) * RET_VAL_DIM)
                oh = o2[:, sl]
                r = lax.rsqrt(jnp.mean(oh * oh, axis=-1, keepdims=True) + NORM_EPS)
                outs.append(oh * r * sz[:, sl])
            return a1, jnp.concatenate(outs, axis=-1)

        a_na, a_ret = _ew(act, [('t', o_na, 0, w_na), ('t', u, o_naz // w_na, w_na), ('t', o_ret, 0, w_v), ('t', u, o_retz // w_v, w_v)],
                          [('t', w_na, BF16), ('t', w_v, BF16)], rows=t_len, tr=tr, name="act_%d" % l)
        if l == 0:
            landed_rest = _split_wait(rest_gather, a_na, plan_rest, name="gather_wait_0_rest")
            later_passes = [_split_start([], _split_wait(later_gathers[j], a_na, near_all, name="gather_near_%d" % (j + 1)),
                                         far_all, len(big_axes), name="gather_pass_%d" % (j + 1)) for j in range(depth - 1)]
            landed_rest, tokens = lax.optimization_barrier((landed_rest, [g[4] for g in later_passes]))
            gate = gate + sum(t[0, 0] for t in tokens)
            wpn_f, wpr_f, wout_f = _forward_halves(landed_rest, big_axes[1:], name="gather_forward_0_rest")
        full_w.append((win_f, wpn_f, wpr_f, wout_f))
        y_na = _mm(a_na, wpn_f, name="proj_na_%d" % l)
        y_ret = _mm(a_ret, wpr_f, name="proj_ret_%d" % l)

        def merge(y1, y2, g1, g2):
            return _sigmoid(g1.astype(F32)) * y1.astype(F32) + _sigmoid(g2.astype(F32)) * y2.astype(F32)

        merged, = _ew(merge, [('t', y_na, 0, d_model), ('t', y_ret, 0, d_model), ('t', u, o_gna // d_model, d_model), ('t', u, o_gret // d_model, d_model)],
                      [('t', d_model, BF16)], rows=t_len, tr=tr, name="merge_%d" % l)
        out = _mm(merged, wout_f, out_dtype=F32, name="out_proj_%d" % l)
        x_new, = _ew(lambda xt, ot, gt: xt + gt * ot, [('t', x_all, 0, d_model), ('t', out, 0, d_model), ('g', gate)],
                     [('t', d_model, F32)], rows=t_len, tr=tr, n0=n0, name="resid_%d" % l)
        saved.append(dict(x=x_all, h=h, u=u, bias=bias, o_na=o_na, o_ret=o_ret, states=states, a_na=a_na, a_ret=a_ret,
                          y_na=y_na, y_ret=y_ret, merged=merged, out=out, gate=gate, gs=gs, scale=scale))
        x_all = x_new

    def final(xt, tt, gt):
        r = lax.rsqrt(jnp.mean(xt * xt, axis=-1, keepdims=True) + NORM_EPS)
        xh = xt * r
        e = xh * gt - tt
        dy = e * (1.0 / d_model)
        dyg = dy * gt
        dx = r * (dyg - xh * jnp.mean(dyg * xh, axis=-1, keepdims=True))
        return dx, _rsum(dy * xh), _rsum(e * e)

    dx_lat, d_final_g, loss_cols = _ew(final, [('t', x_all, 0, d_model), ('t', loss_target[0], 0, d_model), ('g', final_g[None, None, :])],
                                       [('t', d_model, F32), ('r', d_model, 1), ('r', d_model, 1)], rows=s_len, tr=tr, name="final")
    loss_part = (0.5 / d_model) * jnp.sum(loss_cols)
    dx_all = jnp.concatenate([dx_lat, jnp.zeros((l_len, d_model), F32)], axis=0)

    big_w = [(w_in, m_w_in, v_w_in), (w_proj_na, m_w_proj_na, v_w_proj_na), (w_proj_ret, m_w_proj_ret, v_w_proj_ret), (w_out, m_w_out, v_w_out)]
    big_res = [None] * 4
    scatters = {}
    back_token = jnp.zeros((), F32)

    pairs = {}

    def start_pair(key, grads, axes):
        plan = _pair_plan(axes)
        lands = []
        for g, ax in zip(grads, axes):
            shp = list(g.shape)
            shp[1 - ax] //= 2
            lands.append(lax.empty(tuple(shp), BF16))
        pairs[key] = (_split_start(grads, lands, plan, len(axes), name="pair_start_%s" % key), axes, plan)
        return pairs[key][0][4]

    def start_scatter(key, after):
        started, axes, pair_plan = pairs[key]
        grads, theirs = _split_wait(started, after, pair_plan, name="pair_wait_%s" % key, with_srcs=True)
        plan = _scatter_plan(axes)
        pair = [_sum_pair(g, t, ax, ci, name="sum_pair_%s_%d" % (key, i)) for i, (g, t, ax) in enumerate(zip(grads, theirs, axes))]
        own = [lax.dynamic_slice_in_dim(s, chip * (s.shape[ax] // N_CHIPS), s.shape[ax] // N_CHIPS, axis=ax) for s, ax in zip(pair, axes)]
        lands = [lax.empty((N_CHIPS - 1,) + o.shape, BF16) for o in own]
        started = _split_start(pair, lands, plan, len(axes) * (N_CHIPS - 1), name="scatter_start_%s" % key)
        scatters[key] = (started, own, axes, plan)
        return started[4]

    def finish_scatter(key, after):
        started, own, axes, plan = scatters[key]
        recv = _split_wait(started, after, plan, name="scatter_wait_%s" % key)
        bufs = [_sum_chips_into(own[i], rbuf, axes[i], ci, name="sum_chips_%s_%d" % (key, i)) for i, rbuf in enumerate(recv)]
        return _share_halves_in_place(bufs, axes, name="share_halves_%s" % key)

    def adamw_big(l, idx, grads, big_res):
        for i, g in zip(idx, grads):
            w3, m3, v3 = big_w[i]
            big_res[i] = _adamw_layer(w3, m3, v3, g, None, l, big_res[i], name="adamw_big_%d_%d" % (i, l))
        return big_res

    small = dict(dmod_lat=[None] * depth, dmod_ctx=[None] * depth, dnorm_g=[None] * depth, drpb=[None] * depth, ddecay=[None] * depth)
    for l in reversed(range(depth)):
        sv = saved[l]
        win_f, wpn_f, wpr_f, wout_f = full_w[l]

        def resid_bwd(dxt, ot, gt):
            return gt * dxt, _rsum(dxt * ot)

        dout, dgate = _ew(resid_bwd, [('t', dx_all, 0, d_model), ('t', sv['out'], 0, d_model), ('g', sv['gate'] + back_token)],
                          [('t', d_model, BF16), ('r', d_model, 2)], rows=t_len, tr=tr, n0=n0, name="resid_bwd_%d" % l)
        dmerged = _mm(dout, wout_f, tb=True, name="out_proj_dx_%d" % l)
        g_wout = _mm(sv['merged'], dout, ta=True, tm=1024, tk=t_len, name="out_proj_dw_%d" % l)

        def merge_bwd(dm, y1, y2, g1, g2):
            dm = dm.astype(F32)
            s1, s2_ = _sigmoid(g1.astype(F32)), _sigmoid(g2.astype(F32))
            return dm * s1, dm * s2_, dm * y1.astype(F32) * s1 * (1.0 - s1), dm * y2.astype(F32) * s2_ * (1.0 - s2_)

        u = sv['u']
        dy_na, dy_ret, dg_na, dg_ret = _ew(
            merge_bwd, [('t', dmerged, 0, d_model), ('t', sv['y_na'], 0, d_model), ('t', sv['y_ret'], 0, d_model),
                        ('t', u, o_gna // d_model, d_model), ('t', u, o_gret // d_model, d_model)],
            [('t', d_model, BF16)] * 4, rows=t_len, tr=tr, name="merge_bwd_%d" % l)
        da_na = _mm(dy_na, wpn_f, tb=True, name="proj_na_dx_%d" % l)
        g_wpn = _mm(sv['a_na'], dy_na, ta=True, tm=1024, tk=t_len, name="proj_na_dw_%d" % l)
        da_ret = _mm(dy_ret, wpr_f, tb=True, name="proj_ret_dx_%d" % l)
        g_wpr = _mm(sv['a_ret'], dy_ret, ta=True, tm=1024, tk=t_len, name="proj_ret_dw_%d" % l)
        lg_l = log_gamma[l]
        if l == 0:
            pair_token = start_pair("0_rest", [g_wpn, g_wpr, g_wout], big_axes[1:])

        def act_bwd(da1, o1, z1, da2, o2, z2):
            da1, da2 = da1.astype(F32), da2.astype(F32)
            si1, ds1 = _silu_parts(z1.astype(F32))
            si2, ds2 = _silu_parts(z2.astype(F32))
            do1 = da1 * si1
            dz1 = da1 * o1.astype(F32) * ds1
            dn = da2 * si2
            do2, dz2 = [], []
            for hh in range(ret_heads):
                sl = slice(hh * RET_VAL_DIM, (hh + 1) * RET_VAL_DIM)
                oh = o2[:, sl]
                r = lax.rsqrt(jnp.mean(oh * oh, axis=-1, keepdims=True) + NORM_EPS)
                nh = oh * r
                dz2.append(da2[:, sl] * nh * ds2[:, sl])
                do2.append(r * (dn[:, sl] - nh * jnp.mean(dn[:, sl] * nh, axis=-1, keepdims=True)))
            return do1, dz1, jnp.concatenate(do2, axis=-1), jnp.concatenate(dz2, axis=-1)

        do_na, dz_na, do_ret, dz_ret = _ew(
            act_bwd, [('t', da_na, 0, w_na), ('t', sv['o_na'], 0, w_na), ('t', u, o_naz // w_na, w_na),
                      ('t', da_ret, 0, w_v), ('t', sv['o_ret'], 0, w_v), ('t', u, o_retz // w_v, w_v)],
            [('t', w_na, BF16), ('t', w_na, BF16), ('t', w_v, BF16), ('t', w_v, BF16)], rows=t_len, tr=tr, name="act_bwd_%d" % l)
        dq_na, dk_na, dv_na, dbias = _na_bwd(u, sv['bias'], sv['o_na'], do_na, s_len=s_len, heads=na_heads, name="na_bwd_%d" % l)
        small['drpb'][l] = _rpb_grad(dbias, name="rpb_grad_%d" % l)
        if l == 0:
            lg_l = lg_l + start_scatter("0_rest", dq_na)[0, 0] + pair_token[0, 0]
        dq_r, dk_r, dv_r, dlg = _ret_bwd(u, c2, s2, lg_l, sv['states'], do_ret, s_len=s_len, heads=ret_heads,
                                         q_off=o_retq, name="ret_bwd_%d" % l)
        small['ddecay'][l] = jnp.transpose(dlg[:, :, 0, 0]) * _sigmoid(-ret_decay_logit[l])
        du_parts = [dq_na, dk_na, dv_na, dz_na, dq_r, dk_r, dv_r, dz_ret, dg_na, dg_ret]
        du, = _ew(lambda *t: jnp.concatenate(t, axis=-1), [('t', p, 0, p.shape[1]) for p in du_parts], [('t', in_cols, BF16)],
                  rows=t_len, tr=tr, name="du_concat_%d" % l)
        g_win = _mm(sv['h'], du, ta=True, tm=1024, tn=1024, tk=t_len, name="in_proj_dw_%d" % l)
        if l > 0:
            du, pair_token = lax.optimization_barrier((du, start_pair("%d_all" % l, [g_win, g_wpn, g_wpr, g_wout], big_axes)))
        else:
            du, in_token = lax.optimization_barrier((du, start_pair("0_in", [g_win], big_axes[:1])))
        dh = _mm(du, win_f, tb=True, out_dtype=F32, tm=1152, tn=1024, name="in_proj_dx_%d" % l)

        def modnorm_bwd(xt, dht, dxt, gs_t):
            r = lax.rsqrt(jnp.mean(xt * xt, axis=-1, keepdims=True) + NORM_EPS)
            xh = xt * r
            dhg = dht * gs_t
            dx = r * (dhg - xh * jnp.mean(dhg * xh, axis=-1, keepdims=True)) + dxt
            return dx, _rsum(dht), _rsum(dht * xh)

        dx_all, dshift, dgs = _ew(modnorm_bwd, [('t', sv['x'], 0, d_model), ('t', dh, 0, d_model), ('t', dx_all, 0, d_model), ('g', sv['gs'])],
                                  [('t', d_model, F32), ('r', d_model, 2), ('r', d_model, 2)], rows=t_len, tr=tr, n0=n0, name="modnorm_bwd_%d" % l)
        dscale = dgs * norm_g[l][None, None, :]
        small['dnorm_g'][l] = jnp.sum(dgs * (1.0 + sv['scale']), axis=(0, 1))
        dmod = jnp.concatenate([dshift, dscale, dgate], axis=-1)[:, 0]
        small['dmod_lat'][l], small['dmod_ctx'][l] = dmod[0], dmod[1]

        if l > 0:
            back_token = start_scatter("%d_all" % l, dx_all)[0, 0] + pair_token[0, 0]

    grad_x = dx_all[:s_len][None]

    drpb = jnp.stack(small['drpb']).reshape(-1)
    ddecay = jnp.stack(small['ddecay']).reshape(-1)
    pieces = [jnp.stack(small['dmod_lat']).reshape(-1), jnp.stack(small['dmod_ctx']).reshape(-1),
              jnp.stack(small['dnorm_g']).reshape(-1), d_final_g.reshape(-1), drpb, ddecay, loss_part[None]]
    sizes = [int(p.shape[0]) for p in pieces]
    pads = [-(-s // LANES) * LANES for s in sizes]
    packed = jnp.concatenate([jnp.pad(p, (0, pd - s)) for p, s, pd in zip(pieces, sizes, pads)])
    gathered = _all_gather_small(_pack_rows(packed), name="gather_small_grads")
    r_small = gathered.shape[1]

    def sum8(*t):
        acc = t[0]
        for other in t[1:]:
            acc = acc + other
        return acc

    total, = _ew(sum8, [('t', gathered, 0, LANES, k) for k in range(N_DEV)], [('t', LANES, F32)], rows=r_small, tr=r_small, name="sum_devices")
    total = total.reshape(-1)
    starts = np.cumsum([0] + pads)
    g_mod_lat_sum, g_mod_ctx, g_norm_g, g_final_g, g_rpb, g_decay, loss = [total[starts[i]:starts[i] + sizes[i]] for i in range(len(pieces))]
    loss = loss[0]
    g_ada_b = (g_mod_lat_sum + g_mod_ctx).reshape(depth, mod_cols)
    g_mod_ctx = g_mod_ctx.reshape(depth, mod_cols)
    dmod_lat_all = gathered.reshape(N_DEV, -1)[:, :depth * mod_cols].reshape(N_DEV, depth, mod_cols)

    dcc_part = jnp.zeros((16, d_model), F32)
    ctx_cols = [lax.dynamic_slice_in_dim(g_mod_ctx[l], chip * mod_shard, mod_shard, axis=0) for l in range(depth)]
    for l in reversed(range(depth)):
        c_rows = jnp.concatenate([ctx_cols[l][None], jnp.zeros((15, mod_shard), F32)], axis=0)
        dcc_part = dcc_part + _mm(c_rows, ada_w, tb=True, b_lead=l, out_dtype=F32, name="ada_dc_%d" % l)
    dcc_all = _all_gather_small(_pack_rows(dcc_part[0]), name="gather_dcc")[:, :d_model // LANES].reshape(N_CHIPS, 2, d_model)[:, 0]

    tail_token = start_scatter("0_in", dcc_all) + in_token
    dcc = ((dcc_all[0] + dcc_all[1]) + dcc_all[2]) + dcc_all[3]
    sg = _sigmoid(c_ctx)
    g_c_ctx = dcc * (sg * (1.0 + c_ctx * (1.0 - sg)))
    for l in reversed(range(1, depth)):
        big_res = adamw_big(l, range(4), finish_scatter("%d_all" % l, tail_token), big_res)

    ada_res = None
    for l in reversed(range(depth)):
        lat_cols = lax.dynamic_slice_in_dim(dmod_lat_all[:, l], chip * mod_shard, mod_shard, axis=1)
        d_rows = jnp.concatenate([lat_cols, ctx_cols[l][None], jnp.zeros((16 - N_DEV - 1, mod_shard), F32)], axis=0) + tail_token[0, 0]
        g_ada = _mm(a_rows, d_rows, ta=True, out_dtype=F32, tm=512, name="ada_dw_%d" % l)
        ada_res = _adamw_layer(ada_w, m_ada_w, v_ada_w, g_ada, None, l, ada_res, name="adamw_ada_%d" % l)

    small_w = [(c_ctx, m_c_ctx, v_c_ctx, g_c_ctx), (ada_b, m_ada_b, v_ada_b, g_ada_b),
               (norm_g, m_norm_g, v_norm_g, g_norm_g), (na_rpb, m_na_rpb, v_na_rpb, g_rpb),
               (ret_decay_logit, m_ret_decay_logit, v_ret_decay_logit, g_decay), (final_g, m_final_g, v_final_g, g_final_g)]
    sw_sizes = [int(np.prod(t[0].shape)) for t in small_w]
    sw_pads = [-(-s // LANES) * LANES for s in sw_sizes]

    def pack(j):
        return _pack_rows(jnp.concatenate([jnp.pad(t[j].reshape(-1), (0, pd - s)) for t, s, pd in zip(small_w, sw_sizes, sw_pads)]))

    pw_, pm_, pv_, pg_ = pack(0), pack(1), pack(2), pack(3)
    sw_out = _ew(lambda w, m, v, g: (g,) + _adamw_math(w, g, m, v),
                 [('t', pw_, 0, LANES), ('t', pm_, 0, LANES), ('t', pv_, 0, LANES), ('t', pg_, 0, LANES)],
                 [('t', LANES, F32)] * 4, rows=pw_.shape[0], tr=pw_.shape[0], name="adamw_small")
    sw_starts = np.cumsum([0] + sw_pads)
    sw_out, ada_res, big_res = lax.optimization_barrier((sw_out, ada_res, big_res))
    big_res = adamw_big(0, range(1, 4), finish_scatter("0_rest", sw_out[0]), big_res)
    big_res = adamw_big(0, range(1), finish_scatter("0_in", sw_out[1]), big_res)

    def unpack(arr, i):
        return arr.reshape(-1)[sw_starts[i]:sw_starts[i] + sw_sizes[i]].reshape(small_w[i][0].shape)

    sm = [[unpack(sw_out[j], i) for i in range(len(small_w))] for j in range(4)]
    def ordered(j):
        return [sm[j][0], ada_res[j], sm[j][1], sm[j][2], big_res[0][j], sm[j][3], sm[j][4],
                big_res[1][j], big_res[2][j], big_res[3][j], sm[j][5]]

    return (loss, grad_x, *ordered(0), *ordered(1), *ordered(2), *ordered(3))
```

```python
import functools
import math

import numpy as np
import jax
import jax.numpy as jnp
from jax import lax
from jax.experimental import pallas as pl
from jax.experimental.pallas import tpu as pltpu

GRID_W = 64
NA_HEAD_DIM = 128
NA_WIN_ROWS = 8
NA_WIN_COLS = 16
NA_GROUP = 8
RET_GROUPS = (1, 2, 3)
RET_KEY_DIM = 128
RET_VAL_DIM = 256
RET_CHUNK = 128
ROPE_BASE = 10000.0
NORM_EPS = 1e-6
MASK_VALUE = -1e30
ADAM_LR = 0.001
ADAM_B1 = 0.9
ADAM_B2 = 0.999
ADAM_EPS = 1e-08
ADAM_WD = 0.01
ADAM_STEP = 10

N_CHIPS = 4
N_DEV = 8
LANES = 128
VMEM_LIMIT = 56 * 1024 * 1024
BF16 = jnp.bfloat16
F32 = jnp.float32
MESH = pl.DeviceIdType.MESH
ANY = pl.BlockSpec(memory_space=pl.ANY)


def _tile(dim, pref, align=LANES):
    if dim <= pref:
        return dim
    t = (pref // align) * align
    while t >= align:
        if dim % t == 0:
            return t
        t -= align
    return dim


def _rows_per_tile(rows, width, tile_bytes=1 << 20):
    return _tile(rows, max(8, tile_bytes // (4 * width)), 8)


def _params(sem):
    return pltpu.CompilerParams(dimension_semantics=sem, vmem_limit_bytes=VMEM_LIMIT)


def _sigmoid(x):
    return 1.0 / (1.0 + jnp.exp(-x))


def _dot(a, b, ca, cb):
    return lax.dot_general(a, b, (((ca,), (cb,)), ((), ())), preferred_element_type=F32)


def _mm(a, b, *, ta=False, tb=False, a_lead=None, b_lead=None, out_dtype=BF16, tm=1152, tn=1024, tk=2048, name):
    ash = a.shape[1:] if a_lead is not None else a.shape
    bsh = b.shape[1:] if b_lead is not None else b.shape
    m, k = (ash[1], ash[0]) if ta else ash
    n, k2 = bsh if tb else (bsh[1], bsh[0])
    assert k == k2, (name, ash, bsh)
    tm, tn, tk = _tile(m, tm), _tile(n, tn), _tile(k, tk)
    nk = k // tk

    def lead(spec_shape, imap, l):
        if l is None:
            return pl.BlockSpec(spec_shape, imap)
        return pl.BlockSpec((None,) + spec_shape, lambda i, j, kk: (l,) + imap(i, j, kk))

    a_spec = lead((tk, tm), lambda i, j, kk: (kk, i), a_lead) if ta else lead((tm, tk), lambda i, j, kk: (i, kk), a_lead)
    b_spec = lead((tn, tk), lambda i, j, kk: (j, kk), b_lead) if tb else lead((tk, tn), lambda i, j, kk: (kk, j), b_lead)
    ca, cb = (0 if ta else 1), (1 if tb else 0)

    def body(a_ref, b_ref, o_ref, *scratch):
        part = _dot(a_ref[...].astype(BF16), b_ref[...].astype(BF16), ca, cb)
        if nk == 1:
            o_ref[...] = part.astype(o_ref.dtype)
            return
        acc_ref, = scratch
        kk = pl.program_id(2)

        @pl.when(kk == 0)
        def _():
            acc_ref[...] = part

        @pl.when(kk > 0)
        def _():
            acc_ref[...] += part

        @pl.when(kk == nk - 1)
        def _():
            o_ref[...] = acc_ref[...].astype(o_ref.dtype)

    return pl.pallas_call(
        body, name=name, grid=(m // tm, n // tn, nk),
        in_specs=[a_spec, b_spec],
        out_specs=pl.BlockSpec((tm, tn), lambda i, j, kk: (i, j)),
        out_shape=jax.ShapeDtypeStruct((m, n), out_dtype),
        scratch_shapes=[] if nk == 1 else [pltpu.VMEM((tm, tn), F32)],
        compiler_params=_params(("parallel", "parallel", "arbitrary")),
    )(a, b)


def _ew(fn, ins, outs, *, rows, tr, name, n0=None, aliases=None):
    assert rows % tr == 0, (name, rows, tr)
    nt = rows // tr

    def grp(i):
        return 0 if n0 is None else jnp.where(i < n0, 0, 1)

    in_specs, args = [], []
    for spec in ins:
        if spec[0] == 't':
            arr, cb, w = spec[1], spec[2], spec[3]
            l = spec[4] if len(spec) > 4 else None
            if l is None:
                in_specs.append(pl.BlockSpec((tr, w), functools.partial(lambda i, cb: (i, cb), cb=cb)))
            else:
                in_specs.append(pl.BlockSpec((None, tr, w), functools.partial(lambda i, cb, l: (l, i, cb), cb=cb, l=l)))
            args.append(arr)
        else:
            arr = spec[1]
            g = arr.shape[0]
            if g == 1:
                in_specs.append(pl.BlockSpec((None, 1, arr.shape[2]), lambda i: (0, 0, 0)))
            else:
                in_specs.append(pl.BlockSpec((None, 1, arr.shape[2]), lambda i: (grp(i), 0, 0)))
            args.append(arr)
    out_specs, out_shapes, is_red = [], [], []
    for spec in outs:
        if spec[0] == 't':
            w, dt = spec[1], spec[2]
            if len(spec) > 3:
                l, nl = spec[3], spec[4]
                out_specs.append(pl.BlockSpec((None, tr, w), functools.partial(lambda i, l: (l, i, 0), l=l)))
                out_shapes.append(jax.ShapeDtypeStruct((nl, rows, w), dt))
            else:
                out_specs.append(pl.BlockSpec((tr, w), lambda i: (i, 0)))
                out_shapes.append(jax.ShapeDtypeStruct((rows, w), dt))
            is_red.append(False)
        else:
            w, g = spec[1], spec[2]
            if g == 1:
                out_specs.append(pl.BlockSpec((None, 1, w), lambda i: (0, 0, 0)))
            else:
                out_specs.append(pl.BlockSpec((None, 1, w), lambda i: (grp(i), 0, 0)))
            out_shapes.append(jax.ShapeDtypeStruct((g, 1, w), F32))
            is_red.append(True)
    n_in = len(ins)
    n_alias = 0 if aliases is None else len(aliases)

    def body(*refs):
        in_refs = refs[:n_in]
        out_refs = refs[n_in + n_alias:]
        res = fn(*[r[...] for r in in_refs])
        if not isinstance(res, (tuple, list)):
            res = (res,)
        i = pl.program_id(0)
        first = (i == 0) if n0 is None else ((i == 0) | (i == n0))
        for o_ref, val, red in zip(out_refs, res, is_red):
            if not red:
                o_ref[...] = val.astype(o_ref.dtype)
            else:
                @pl.when(first)
                def _(o_ref=o_ref, val=val):
                    o_ref[...] = val

                @pl.when(jnp.logical_not(first))
                def _(o_ref=o_ref, val=val):
                    o_ref[...] += val

    io_alias = {}
    if aliases is not None:
        for a_idx, (arr, o_idx) in enumerate(aliases):
            in_specs.append(ANY)
            args.append(arr)
            io_alias[n_in + a_idx] = o_idx
    has_red = any(is_red)
    return pl.pallas_call(
        body, name=name, grid=(nt,), in_specs=in_specs, out_specs=out_specs, out_shape=out_shapes,
        input_output_aliases=io_alias,
        compiler_params=_params(("arbitrary",) if has_red else ("parallel",)),
    )(*args)


def _half_spec(tr, width, ax, n_tiles):
    if ax == 1:
        return pl.BlockSpec((tr, width), lambda i, sel: (sel[0] * n_tiles + i, 0))
    return pl.BlockSpec((tr, width), lambda i, sel: (i, sel[0]))


def _sum_pair(g, theirs, ax, ci, *, name):
    pr, pw = theirs.shape
    tr = _rows_per_tile(pr, pw)
    nt = pr // tr

    def body(sel, a_ref, b_ref, o_ref):
        o_ref[...] = (a_ref[...].astype(F32) + b_ref[...].astype(F32)).astype(o_ref.dtype)

    return pl.pallas_call(
        body, name=name,
        grid_spec=pltpu.PrefetchScalarGridSpec(
            num_scalar_prefetch=1, grid=(nt,),
            in_specs=[_half_spec(tr, pw, ax, nt), pl.BlockSpec((tr, pw), lambda i, sel: (i, 0))],
            out_specs=pl.BlockSpec((tr, pw), lambda i, sel: (i, 0))),
        out_shape=jax.ShapeDtypeStruct((pr, pw), BF16),
        compiler_params=_params(("parallel",)),
    )(jnp.reshape(ci, (1,)).astype(jnp.int32), g, theirs)


def _sum_chips_into(own, recv, ax, ci, *, name):
    pr, pw = own.shape
    tr = _rows_per_tile(pr, pw)
    nt = pr // tr
    full_shape = (2 * pr, pw) if ax == 1 else (pr, 2 * pw)

    def body(sel, a_ref, r_ref, o_ref):
        acc = a_ref[...].astype(F32)
        for k in range(N_CHIPS - 1):
            acc = acc + r_ref[k].astype(F32)
        o_ref[...] = acc

    return pl.pallas_call(
        body, name=name,
        grid_spec=pltpu.PrefetchScalarGridSpec(
            num_scalar_prefetch=1, grid=(nt,),
            in_specs=[pl.BlockSpec((tr, pw), lambda i, sel: (i, 0)), pl.BlockSpec((N_CHIPS - 1, tr, pw), lambda i, sel: (0, i, 0))],
            out_specs=_half_spec(tr, pw, ax, nt)),
        out_shape=jax.ShapeDtypeStruct(full_shape, F32),
        compiler_params=_params(("parallel",)),
    )(jnp.reshape(ci, (1,)).astype(jnp.int32), own, recv)


def _rsum(v):
    return jnp.sum(v, axis=0, keepdims=True)


def _silu_parts(z):
    sg = _sigmoid(z)
    return z * sg, sg * (1.0 + z * (1.0 - sg))


def _na_bias_table(rpb, rows, *, name):
    kh, kw = NA_WIN_ROWS, NA_WIN_COLS
    assert rows >= kh
    heads = rpb.shape[0]
    e1, e2 = _na_onehots()
    rpb16 = jnp.pad(rpb, ((0, 0), (0, 16 - rpb.shape[1]), (0, LANES - rpb.shape[2])))

    def body(r_ref, e1_ref, e2_ref, o_ref):
        e1b = e1_ref[...].astype(BF16)
        y = sum(_dot(e1b, part, 0, 0) for part in _split3(r_ref[...]))
        e2b = e2_ref[...].astype(BF16)
        o_ref[...] = sum(_dot(part, e2b, 1, 1) for part in _split3(y))

    z = pl.pallas_call(
        body, name=name, grid=(heads,),
        in_specs=[pl.BlockSpec((None, 16, LANES), lambda h: (h, 0, 0)),
                  pl.BlockSpec(e1.shape, lambda h: (0, 0)), pl.BlockSpec(e2.shape, lambda h: (0, 0))],
        out_specs=pl.BlockSpec((None, kh * kh, GRID_W * GRID_W), lambda h: (h, 0, 0)),
        out_shape=jax.ShapeDtypeStruct((heads, kh * kh, GRID_W * GRID_W), F32),
        compiler_params=_params(("parallel",)),
    )(rpb16, e1, e2)
    return z


def _na_bias_layout(z):
    heads = z.shape[0]
    kh, kw = NA_WIN_ROWS, NA_WIN_COLS
    cidx = np.arange(GRID_W)
    c0 = np.clip(cidx - kw // 2, 0, GRID_W - kw)
    col_in = (cidx[None, :] >= c0[:, None]) & (cidx[None, :] < c0[:, None] + kw)
    bias = z.reshape(heads, kh, kh, GRID_W, GRID_W).transpose(0, 1, 3, 2, 4)
    bias = jnp.where(col_in[None, None, :, None, :], bias, MASK_VALUE)
    return bias.reshape(heads, kh, GRID_W, kh * GRID_W)


def _na_onehots():
    kh, kw = NA_WIN_ROWS, NA_WIN_COLS
    cidx = np.arange(GRID_W)
    dc = cidx[None, :] - cidx[:, None] + (kw - 1)
    e2 = np.zeros((GRID_W * GRID_W, LANES), np.float32)
    ok = (dc >= 0) & (dc <= 2 * kw - 2)
    cq, ck = np.nonzero(ok)
    e2[cq * GRID_W + ck, dc[cq, ck]] = 1.0
    dr = np.arange(kh)[None, :] - np.arange(kh)[:, None] + (kh - 1)
    e1 = np.zeros((16, kh * kh), np.float32)
    dl, kr = np.nonzero(np.ones_like(dr))
    e1[dr[dl, kr], dl * kh + kr] = 1.0
    return jnp.asarray(e1), jnp.asarray(e2)


def _na_fwd(u, bias, *, s_len, heads, name):
    t_len = u.shape[0]
    rows = s_len // GRID_W
    nloc = NA_WIN_ROWS * GRID_W
    scale = NA_HEAD_DIM ** -0.5
    hd = NA_HEAD_DIM

    def body(q_ref, k_ref, v_ref, b_ref, o_ref):
        kc = k_ref[s_len:t_len, :]
        vc = v_ref[s_len:t_len, :]

        def group(g, carry):
            rs = [g * NA_GROUP + i for i in range(NA_GROUP)]
            r0s = [jnp.clip(r - NA_WIN_ROWS // 2, 0, rows - NA_WIN_ROWS) for r in rs]
            gs_ = pl.multiple_of(g * (NA_GROUP * GRID_W), NA_GROUP * GRID_W)
            kss = [pl.multiple_of(r0 * GRID_W, GRID_W) for r0 in r0s]
            q_all = q_ref[pl.ds(gs_, NA_GROUP * GRID_W), :]
            s_ctx = _dot(q_all, kc, 1, 1) * scale
            s_loc = [_dot(q_all[i * GRID_W:(i + 1) * GRID_W], k_ref[pl.ds(kss[i], nloc), :], 1, 1) * scale + b_ref[rs[i] - r0s[i]]
                     for i in range(NA_GROUP)]
            p_loc, p_ctx, inv = [], [], []
            for i in range(NA_GROUP):
                sc = s_ctx[i * GRID_W:(i + 1) * GRID_W]
                m = jnp.maximum(jnp.max(s_loc[i], axis=-1, keepdims=True), jnp.max(sc, axis=-1, keepdims=True))
                pl_, pc_ = jnp.exp(s_loc[i] - m), jnp.exp(sc - m)
                inv.append(1.0 / (jnp.sum(pl_, axis=-1, keepdims=True) + jnp.sum(pc_, axis=-1, keepdims=True)))
                p_loc.append(pl_.astype(BF16))
                p_ctx.append(pc_.astype(BF16))
            o_ctx = _dot(jnp.concatenate(p_ctx, axis=0), vc, 1, 0)
            o_loc = [_dot(p_loc[i], v_ref[pl.ds(kss[i], nloc), :], 1, 0) for i in range(NA_GROUP)]
            out = jnp.concatenate([(o_loc[i] + o_ctx[i * GRID_W:(i + 1) * GRID_W]) * inv[i] for i in range(NA_GROUP)], axis=0)
            o_ref[pl.ds(gs_, NA_GROUP * GRID_W), :] = out.astype(o_ref.dtype)
            return carry

        lax.fori_loop(0, rows // NA_GROUP, group, 0)
        qc = q_ref[s_len:t_len, :]
        s = _dot(qc, kc, 1, 1) * scale
        p = jnp.exp(s - jnp.max(s, axis=-1, keepdims=True))
        o = _dot(p.astype(BF16), vc, 1, 0) / jnp.sum(p, axis=-1, keepdims=True)
        o_ref[s_len:t_len, :] = o.astype(o_ref.dtype)

    col = lambda off: pl.BlockSpec((t_len, hd), functools.partial(lambda h, off: (0, off + h), off=off))
    return pl.pallas_call(
        body, name=name, grid=(heads,),
        in_specs=[col(0), col(heads), col(2 * heads),
                  pl.BlockSpec((None, NA_WIN_ROWS, GRID_W, nloc), lambda h: (h, 0, 0, 0))],
        out_specs=pl.BlockSpec((t_len, hd), lambda h: (0, h)),
        out_shape=jax.ShapeDtypeStruct((t_len, heads * hd), BF16),
        compiler_params=_params(("parallel",)),
    )(u, u, u, bias)


def _na_bwd(u, bias, o, do, *, s_len, heads, name):
    t_len = u.shape[0]
    rows = s_len // GRID_W
    nloc = NA_WIN_ROWS * GRID_W
    scale = NA_HEAD_DIM ** -0.5
    hd = NA_HEAD_DIM

    def body(q_ref, k_ref, v_ref, b_ref, o_ref, do_ref, dq_ref, dk_ref, dv_ref, db_ref, dk_acc, dv_acc):
        kc = k_ref[s_len:t_len, :]
        vc = v_ref[s_len:t_len, :]
        dk_acc[...] = jnp.zeros_like(dk_acc)
        dv_acc[...] = jnp.zeros_like(dv_acc)
        db_ref[...] = jnp.zeros_like(db_ref)

        def group(g, carry):
            n_g, rw = NA_GROUP, GRID_W
            rs = [g * n_g + i for i in range(n_g)]
            r0s = [jnp.clip(r - NA_WIN_ROWS // 2, 0, rows - NA_WIN_ROWS) for r in rs]
            dls = [r - r0 for r, r0 in zip(rs, r0s)]
            gs_ = pl.ds(pl.multiple_of(g * (n_g * rw), n_g * rw), n_g * rw)
            kss = [pl.ds(pl.multiple_of(r0 * rw, rw), nloc) for r0 in r0s]
            row_of = lambda a, i: a[i * rw:(i + 1) * rw]
            q_all, do_all = q_ref[gs_, :], do_ref[gs_, :]
            dlt_all = jnp.sum(do_all.astype(F32) * o_ref[gs_, :].astype(F32), axis=-1, keepdims=True)
            s_ctx = _dot(q_all, kc, 1, 1) * scale
            dp_ctx = _dot(do_all, vc, 1, 1)
            s_loc = [_dot(row_of(q_all, i), k_ref[kss[i], :], 1, 1) * scale + b_ref[dls[i]] for i in range(n_g)]
            dp_loc = [_dot(row_of(do_all, i), v_ref[kss[i], :], 1, 1) for i in range(n_g)]
            p_loc_b, ds_loc_b, p_ctx_b, ds_ctx_b = [], [], [], []
            for i in range(n_g):
                sc, dlt = row_of(s_ctx, i), row_of(dlt_all, i)
                m = jnp.maximum(jnp.max(s_loc[i], axis=-1, keepdims=True), jnp.max(sc, axis=-1, keepdims=True))
                pl_, pc_ = jnp.exp(s_loc[i] - m), jnp.exp(sc - m)
                inv = 1.0 / (jnp.sum(pl_, axis=-1, keepdims=True) + jnp.sum(pc_, axis=-1, keepdims=True))
                pl_, pc_ = pl_ * inv, pc_ * inv
                ds_l = pl_ * (dp_loc[i] - dlt)
                db_ref[dls[i]] += ds_l
                p_loc_b.append(pl_.astype(BF16))
                ds_loc_b.append(ds_l.astype(BF16))
                p_ctx_b.append(pc_.astype(BF16))
                ds_ctx_b.append((pc_ * (row_of(dp_ctx, i) - dlt)).astype(BF16))
            p_ctx_all, ds_ctx_all = jnp.concatenate(p_ctx_b, axis=0), jnp.concatenate(ds_ctx_b, axis=0)
            dq_ctx = _dot(ds_ctx_all, kc, 1, 0)
            dq_loc = [_dot(ds_loc_b[i], k_ref[kss[i], :], 1, 0) for i in range(n_g)]
            dk_loc = [_dot(ds_loc_b[i], row_of(q_all, i), 0, 0) for i in range(n_g)]
            dv_loc = [_dot(p_loc_b[i], row_of(do_all, i), 0, 0) for i in range(n_g)]
            dk_ctx = _dot(ds_ctx_all, q_all, 0, 0)
            dv_ctx = _dot(p_ctx_all, do_all, 0, 0)
            dq_ref[gs_, :] = ((jnp.concatenate(dq_loc, axis=0) + dq_ctx) * scale).astype(dq_ref.dtype)
            for i in range(n_g):
                dk_acc[kss[i], :] += dk_loc[i] * scale
                dv_acc[kss[i], :] += dv_loc[i]
            dk_acc[s_len:t_len, :] += dk_ctx * scale
            dv_acc[s_len:t_len, :] += dv_ctx
            return carry

        lax.fori_loop(0, rows // NA_GROUP, group, 0)
        qc = q_ref[s_len:t_len, :]
        dout = do_ref[s_len:t_len, :]
        out = o_ref[s_len:t_len, :]
        s = _dot(qc, kc, 1, 1) * scale
        p = jnp.exp(s - jnp.max(s, axis=-1, keepdims=True))
        p = p / jnp.sum(p, axis=-1, keepdims=True)
        dlt = jnp.sum(dout.astype(F32) * out.astype(F32), axis=-1, keepdims=True)
        ds = (p * (_dot(dout, vc, 1, 1) - dlt)).astype(BF16)
        dq_ref[s_len:t_len, :] = (_dot(ds, kc, 1, 0) * scale).astype(dq_ref.dtype)
        dk_acc[s_len:t_len, :] += _dot(ds, qc, 0, 0) * scale
        dv_acc[s_len:t_len, :] += _dot(p.astype(BF16), dout, 0, 0)
        dk_ref[...] = dk_acc[...].astype(dk_ref.dtype)
        dv_ref[...] = dv_acc[...].astype(dv_ref.dtype)

    col = lambda off: pl.BlockSpec((t_len, hd), functools.partial(lambda h, off: (0, off + h), off=off))
    tbl = pl.BlockSpec((None, NA_WIN_ROWS, GRID_W, nloc), lambda h: (h, 0, 0, 0))
    tok = jax.ShapeDtypeStruct((t_len, heads * hd), BF16)
    return pl.pallas_call(
        body, name=name, grid=(heads,),
        in_specs=[col(0), col(heads), col(2 * heads), tbl, col(0), col(0)],
        out_specs=[col(0), col(0), col(0), tbl],
        out_shape=[tok, tok, tok, jax.ShapeDtypeStruct(bias.shape, F32)],
        scratch_shapes=[pltpu.VMEM((t_len, hd), F32), pltpu.VMEM((t_len, hd), F32)],
        compiler_params=_params(("parallel",)),
    )(u, u, u, bias, o, do)


def _split3(x):
    hi = x.astype(BF16)
    r1 = x - hi.astype(F32)
    mid = r1.astype(BF16)
    lo = (r1 - mid.astype(F32)).astype(BF16)
    return hi, mid, lo


def _rpb_grad(dbias, *, name):
    heads = dbias.shape[0]
    kh = NA_WIN_ROWS
    e1, e2 = _na_onehots()
    x = dbias.reshape(heads, kh, GRID_W, kh, GRID_W).transpose(0, 1, 3, 2, 4).reshape(heads, kh * kh, GRID_W * GRID_W)

    def body(x_ref, e1_ref, e2_ref, o_ref):
        e2b = e2_ref[...].astype(BF16)
        y = sum(_dot(part, e2b, 1, 0) for part in _split3(x_ref[...]))
        e1b = e1_ref[...].astype(BF16)
        o_ref[...] = sum(_dot(e1b, part, 1, 0) for part in _split3(y))

    out = pl.pallas_call(
        body, name=name, grid=(heads,),
        in_specs=[pl.BlockSpec((None, kh * kh, GRID_W * GRID_W), lambda h: (h, 0, 0)),
                  pl.BlockSpec(e1.shape, lambda h: (0, 0)), pl.BlockSpec(e2.shape, lambda h: (0, 0))],
        out_specs=pl.BlockSpec((None, 16, LANES), lambda h: (h, 0, 0)),
        out_shape=jax.ShapeDtypeStruct((heads, 16, LANES), F32),
        compiler_params=_params(("parallel",)),
    )(x, e1, e2)
    return out[:, :2 * kh - 1, :2 * NA_WIN_COLS - 1]


def _rope_tables(s_len, l_len):
    nf = RET_KEY_DIM // 4
    t = np.arange(s_len)
    row = (t // GRID_W).astype(np.float32)
    colp = (t % GRID_W).astype(np.float32)
    inv_freq = jnp.asarray(ROPE_BASE, F32) ** (-jnp.arange(nf, dtype=F32) / nf)
    ang = jnp.concatenate([jnp.asarray(row)[:, None] * inv_freq, jnp.asarray(colp)[:, None] * inv_freq], axis=-1)
    cos, sin = jnp.cos(ang), jnp.sin(ang)
    c2 = jnp.concatenate([cos, cos], axis=-1)
    s2 = jnp.concatenate([-sin, sin], axis=-1)
    c2 = jnp.concatenate([c2, jnp.ones((l_len, RET_KEY_DIM), F32)], axis=0)
    s2 = jnp.concatenate([s2, jnp.zeros((l_len, RET_KEY_DIM), F32)], axis=0)
    return c2, s2


def _rope(x, c2, s2):
    return x * c2 + pltpu.roll(x, RET_KEY_DIM // 2, 1) * s2


def _rope_t(d, c2, s2):
    return d * c2 + pltpu.roll(d * s2, RET_KEY_DIM // 2, 1)


def _ret_decays(lg, direction):
    cs = RET_CHUNK
    i_col = lax.broadcasted_iota(jnp.int32, (cs, 1), 0)
    p_col = jnp.where(direction == 0, i_col, cs - 1 - i_col).astype(F32)
    pi = lax.broadcasted_iota(jnp.int32, (cs, cs), 0)
    pj = lax.broadcasted_iota(jnp.int32, (cs, cs), 1)
    diff = jnp.where(direction == 0, pi - pj, pj - pi).astype(F32)
    dm = jnp.where(diff >= 0, jnp.exp(jnp.maximum(diff, 0.0) * lg), 0.0)
    qdec = jnp.exp((p_col + 1.0) * lg)
    kdec = jnp.exp((cs - 1.0 - p_col) * lg)
    cd = jnp.exp(jnp.full((1, 1), cs, F32) * lg)
    return p_col, dm, qdec, kdec, cd


def _ret_chunk_index(t, direction, n_chunks, lat_chunks):
    return jnp.where(direction == 0, lax.rem(t + lat_chunks, n_chunks), n_chunks - 1 - t)


def _ret_fwd(u, c2, s2, lg, *, s_len, heads, q_off, name):
    t_len = u.shape[0]
    cs, dk, dv = RET_CHUNK, RET_KEY_DIM, RET_VAL_DIM
    n_chunks, lat_chunks = t_len // cs, s_len // cs
    k_scale = dk ** -0.5
    qb, kb, vb = q_off // dk, q_off // dk + heads, (q_off + 2 * heads * dk) // dv

    def body(lg_ref, q_ref, k_ref, v_ref, c_ref, s_ref, o_ref, st_ref, qd_s, kv_s):
        h, d = pl.program_id(0), pl.program_id(1)
        _, dm, qdec, kdec, cd = _ret_decays(lg_ref[d, h], d)
        n_g = max(g for g in RET_GROUPS if n_chunks % g == 0)
        rows_of = lambda c: pl.ds(pl.multiple_of(c * cs, cs), cs)

        def local(gi, carry):
            rws = [rows_of(gi * n_g + j) for j in range(n_g)]
            qcs = [_rope(q_ref[r, :].astype(F32), c_ref[r, :], s_ref[r, :]) for r in rws]
            kcs = [_rope(k_ref[r, :].astype(F32), c_ref[r, :], s_ref[r, :]) * k_scale for r in rws]
            vcs = [v_ref[r, :] for r in rws]
            a_raw = [_dot(qcs[j].astype(BF16), kcs[j].astype(BF16), 1, 1) for j in range(n_g)]
            kv = [_dot((kcs[j] * kdec).astype(BF16), vcs[j], 0, 0) for j in range(n_g)]
            inner = [_dot((a_raw[j] * dm).astype(BF16), vcs[j], 1, 0) for j in range(n_g)]
            for j in range(n_g):
                qd_s[rws[j], :] = (qcs[j] * qdec).astype(BF16)
                kv_s[gi * n_g + j] = kv[j]

            @pl.when(d == 0)
            def _():
                for j in range(n_g):
                    o_ref[rws[j], :] = inner[j]

            @pl.when(d == 1)
            def _():
                for j in range(n_g):
                    o_ref[rws[j], :] += inner[j]

            return carry

        lax.fori_loop(0, n_chunks // n_g, local, 0)

        def scan(t, st):
            st_ref[t] = st
            return st * cd + kv_s[_ret_chunk_index(t, d, n_chunks, lat_chunks)]

        lax.fori_loop(0, n_chunks, scan, jnp.zeros((dk, dv), F32))

        def cross(gi, carry):
            ts = [gi * n_g + j for j in range(n_g)]
            rws = [rows_of(_ret_chunk_index(t, d, n_chunks, lat_chunks)) for t in ts]
            outs = [_dot(qd_s[rws[j], :], st_ref[ts[j]].astype(BF16), 1, 0) for j in range(n_g)]
            for j in range(n_g):
                o_ref[rws[j], :] += outs[j]
            return carry

        lax.fori_loop(0, n_chunks // n_g, cross, 0)

    return pl.pallas_call(
        body, name=name, grid=(heads, 2),
        in_specs=[pl.BlockSpec(memory_space=pltpu.SMEM),
                  pl.BlockSpec((t_len, dk), lambda h, d: (0, qb + h)),
                  pl.BlockSpec((t_len, dk), lambda h, d: (0, kb + h)),
                  pl.BlockSpec((t_len, dv), lambda h, d: (0, vb + h)),
                  pl.BlockSpec((t_len, dk), lambda h, d: (0, 0)),
                  pl.BlockSpec((t_len, dk), lambda h, d: (0, 0))],
        out_specs=[pl.BlockSpec((t_len, dv), lambda h, d: (0, h)),
                   pl.BlockSpec((None, None, n_chunks, dk, dv), lambda h, d: (h, d, 0, 0, 0))],
        out_shape=[jax.ShapeDtypeStruct((t_len, heads * dv), F32),
                   jax.ShapeDtypeStruct((heads, 2, n_chunks, dk, dv), F32)],
        scratch_shapes=[pltpu.VMEM((t_len, dk), BF16), pltpu.VMEM((n_chunks, dk, dv), F32)],
        compiler_params=_params(("parallel", "arbitrary")),
    )(lg, u, u, u, c2, s2)


def _ret_bwd(u, c2, s2, lg, states, do, *, s_len, heads, q_off, name):
    t_len = u.shape[0]
    cs, dk, dv = RET_CHUNK, RET_KEY_DIM, RET_VAL_DIM
    n_chunks, lat_chunks = t_len // cs, s_len // cs
    k_scale = dk ** -0.5
    qb, kb, vb = q_off // dk, q_off // dk + heads, (q_off + 2 * heads * dk) // dv

    def body(lg_ref, q_ref, k_ref, v_ref, c_ref, s_ref, st_ref, do_ref, dq_ref, dk_ref, dv_ref, dlg_ref, acc, qdo_s, dst_s):
        h, d = pl.program_id(0), pl.program_id(1)
        p_col, dm, qdec, kdec, cd = _ret_decays(lg_ref[d, h], d)
        acc[...] = jnp.zeros_like(acc)
        n_g = max(g for g in RET_GROUPS[:2] if n_chunks % g == 0)
        rows_of = lambda c: pl.ds(pl.multiple_of(c * cs, cs), cs)
        chunk_of = lambda t: _ret_chunk_index(t, d, n_chunks, lat_chunks)

        def local(gi, carry):
            rws = [rows_of(gi * n_g + j) for j in range(n_g)]
            qds = [(_rope(q_ref[r, :].astype(F32), c_ref[r, :], s_ref[r, :]) * qdec).astype(BF16) for r in rws]
            prods = [_dot(qds[j], do_ref[rws[j], :].astype(BF16), 0, 0) for j in range(n_g)]
            for j in range(n_g):
                qdo_s[gi * n_g + j] = prods[j]
            return carry

        lax.fori_loop(0, n_chunks // n_g, local, 0)

        def scan(i, dst):
            t = n_chunks - 1 - i
            dst_s[t] = dst
            return dst * cd + qdo_s[chunk_of(t)]

        lax.fori_loop(0, n_chunks, scan, jnp.zeros((dk, dv), F32))

        def grads(gi, carry):
            ts = [gi * n_g + j for j in range(n_g)]
            rws = [rows_of(chunk_of(t)) for t in ts]
            ccs, sss = [c_ref[r, :] for r in rws], [s_ref[r, :] for r in rws]
            qcs = [_rope(q_ref[r, :].astype(F32), cc, ss) for r, cc, ss in zip(rws, ccs, sss)]
            kcs = [_rope(k_ref[r, :].astype(F32), cc, ss) * k_scale for r, cc, ss in zip(rws, ccs, sss)]
            vcs = [v_ref[r, :] for r in rws]
            docs = [do_ref[r, :].astype(BF16) for r in rws]
            sts = [st_ref[t] for t in ts]
            dsts = [dst_s[t] for t in ts]
            q16 = [x.astype(BF16) for x in qcs]
            k16 = [x.astype(BF16) for x in kcs]
            dst16 = [x.astype(BF16) for x in dsts]
            rng = range(n_g)
            a_raw = [_dot(q16[j], k16[j], 1, 1) for j in rng]
            da_raw = [_dot(docs[j], vcs[j], 1, 1) for j in rng]
            dq_c = [_dot(docs[j], sts[j].astype(BF16), 1, 1) * qdec for j in rng]
            dv_s = [_dot((kcs[j] * kdec).astype(BF16), dst16[j], 1, 0) for j in rng]
            dk_s = [_dot(vcs[j], dst16[j], 1, 1) * kdec for j in rng]
            a16 = [(a_raw[j] * dm).astype(BF16) for j in rng]
            dam = [(da_raw[j] * dm).astype(BF16) for j in rng]
            dq_i = [_dot(dam[j], k16[j], 1, 0) for j in rng]
            dk_i = [_dot(dam[j], q16[j], 0, 0) for j in rng]
            dv_i = [_dot(a16[j], docs[j], 0, 0) for j in rng]
            for j in rng:
                g = (jnp.sum(qcs[j] * (p_col * dq_i[j] + (p_col + 1.0) * dq_c[j]), axis=-1, keepdims=True)
                     + jnp.sum(kcs[j] * ((cs - 1.0 - p_col) * dk_s[j] - p_col * dk_i[j]), axis=-1, keepdims=True))
                g = (jnp.sum(g, axis=0, keepdims=True)
                     + cs * cd * jnp.sum(jnp.sum(dsts[j] * sts[j], axis=-1, keepdims=True), axis=0, keepdims=True))
                acc[...] += jnp.broadcast_to(g, acc.shape)
            dqs = [_rope_t(dq_i[j] + dq_c[j], ccs[j], sss[j]) for j in rng]
            dks = [_rope_t((dk_i[j] + dk_s[j]) * k_scale, ccs[j], sss[j]) for j in rng]
            dvs = [dv_i[j] + dv_s[j] for j in rng]

            @pl.when(d == 0)
            def _():
                for j in rng:
                    dq_ref[rws[j], :] = dqs[j].astype(dq_ref.dtype)
                    dk_ref[rws[j], :] = dks[j].astype(dk_ref.dtype)
                    dv_ref[rws[j], :] = dvs[j].astype(dv_ref.dtype)

            @pl.when(d == 1)
            def _():
                for j in rng:
                    dq_ref[rws[j], :] = (dq_ref[rws[j], :].astype(F32) + dqs[j]).astype(dq_ref.dtype)
                    dk_ref[rws[j], :] = (dk_ref[rws[j], :].astype(F32) + dks[j]).astype(dk_ref.dtype)
                    dv_ref[rws[j], :] = (dv_ref[rws[j], :].astype(F32) + dvs[j]).astype(dv_ref.dtype)

            return carry

        lax.fori_loop(0, n_chunks // n_g, grads, 0)
        dlg_ref[...] = acc[...]

    return pl.pallas_call(
        body, name=name, grid=(heads, 2),
        in_specs=[pl.BlockSpec(memory_space=pltpu.SMEM),
                  pl.BlockSpec((t_len, dk), lambda h, d: (0, qb + h)),
                  pl.BlockSpec((t_len, dk), lambda h, d: (0, kb + h)),
                  pl.BlockSpec((t_len, dv), lambda h, d: (0, vb + h)),
                  pl.BlockSpec((t_len, dk), lambda h, d: (0, 0)),
                  pl.BlockSpec((t_len, dk), lambda h, d: (0, 0)),
                  pl.BlockSpec((None, None, n_chunks, dk, dv), lambda h, d: (h, d, 0, 0, 0)),
                  pl.BlockSpec((t_len, dv), lambda h, d: (0, h))],
        out_specs=[pl.BlockSpec((t_len, dk), lambda h, d: (0, h)),
                   pl.BlockSpec((t_len, dk), lambda h, d: (0, h)),
                   pl.BlockSpec((t_len, dv), lambda h, d: (0, h)),
                   pl.BlockSpec((None, None, 8, LANES), lambda h, d: (h, d, 0, 0))],
        out_shape=[jax.ShapeDtypeStruct((t_len, heads * dk), BF16),
                   jax.ShapeDtypeStruct((t_len, heads * dk), BF16),
                   jax.ShapeDtypeStruct((t_len, heads * dv), BF16),
                   jax.ShapeDtypeStruct((heads, 2, 8, LANES), F32)],
        scratch_shapes=[pltpu.VMEM((8, LANES), F32), pltpu.VMEM((n_chunks, dk, dv), F32), pltpu.VMEM((n_chunks, dk, dv), F32)],
        compiler_params=_params(("parallel", "arbitrary")),
    )(lg, u, u, u, c2, s2, states, do)


def _mesh_pos():
    return lax.axis_index("x"), lax.axis_index("y"), lax.axis_index("c")


def _all_gather_small(buf, *, name):
    r = buf.shape[0]

    def body(x_ref, o_ref, send_sems, recv_sems, local_sem):
        x, y, c = _mesh_pos()
        me = 4 * x + 2 * y + c
        mine = pltpu.make_async_copy(x_ref, o_ref.at[me], local_sem)
        mine.start()
        copies = []
        for k in range(1, N_DEV):
            px, py, pc = x ^ ((k >> 2) & 1), y ^ ((k >> 1) & 1), c ^ (k & 1)
            cp = pltpu.make_async_remote_copy(
                src_ref=x_ref, dst_ref=o_ref.at[me], send_sem=send_sems.at[k - 1], recv_sem=recv_sems.at[k - 1],
                device_id=(px, py, pc), device_id_type=MESH)
            cp.start()
            copies.append((cp, 4 * px + 2 * py + pc))
        for k, (cp, peer) in enumerate(copies):
            pltpu.make_async_remote_copy(
                src_ref=x_ref, dst_ref=o_ref.at[peer], send_sem=send_sems.at[k], recv_sem=recv_sems.at[k],
                device_id=(x, y, c), device_id_type=MESH).wait_recv()
        for cp, _ in copies:
            cp.wait_send()
        mine.wait()

    return pl.pallas_call(
        body, name=name,
        in_specs=[pl.BlockSpec(memory_space=pltpu.VMEM)],
        out_specs=pl.BlockSpec(memory_space=pltpu.VMEM),
        out_shape=jax.ShapeDtypeStruct((N_DEV, r, LANES), F32),
        scratch_shapes=[pltpu.SemaphoreType.DMA((N_DEV - 1,)), pltpu.SemaphoreType.DMA((N_DEV - 1,)),
                        pltpu.SemaphoreType.DMA],
        compiler_params=pltpu.CompilerParams(vmem_limit_bytes=VMEM_LIMIT),
    )(buf)


def _cut(ref, shard_axis, *, chip=None, half=None, lead=None):
    shape = ref.shape[1:] if lead is not None else ref.shape
    idx = [slice(None), slice(None)]
    if chip is not None:
        w = shape[shard_axis] // N_CHIPS
        idx[shard_axis] = pl.ds(pl.multiple_of(chip * w, w), w)
    if half is not None:
        hw = shape[1 - shard_axis] // 2
        idx[1 - shard_axis] = pl.ds(pl.multiple_of(half * hw, hw), hw)
    if lead is not None:
        idx = [lead] + idx
    return ref.at[tuple(idx)]


def _wait_recv(ref, send_sem, recv_sem):
    pltpu.make_async_remote_copy(src_ref=ref, dst_ref=ref, send_sem=send_sem, recv_sem=recv_sem,
                                 device_id=_mesh_pos(), device_id_type=MESH).wait_recv()


def _gather_plan(axes):
    def plan(srcs, lands, send_sems, recv_sems):
        x, y, c = _mesh_pos()
        chip = 2 * x + y
        copies = []
        for i, ax in enumerate(axes):
            for k in range(1, N_CHIPS):
                px, py = x ^ (k >> 1), y ^ (k & 1)
                mine = _cut(lands[i], ax, chip=chip, half=c)
                j = i * (N_CHIPS - 1) + k - 1
                sems = dict(send_sem=send_sems.at[j], recv_sem=recv_sems.at[j], device_id=(px, py, c), device_id_type=MESH)
                send = pltpu.make_async_remote_copy(src_ref=mine, dst_ref=mine, **sems)
                recv = pltpu.make_async_remote_copy(src_ref=mine, dst_ref=_cut(lands[i], ax, chip=2 * px + py, half=c), **sems)
                copies.append((send, recv))
        return copies
    return plan


def _gather_near_plan(axes):
    def plan(srcs, lands, send_sems, recv_sems):
        x, y, c = _mesh_pos()
        copies = []
        for i, ax in enumerate(axes):
            mine = _cut(lands[i], ax, chip=2 * x + y, half=c)
            for k, (px, py) in enumerate(((1 - x, y), (x, 1 - y))):
                sems = dict(send_sem=send_sems.at[2 * i + k], recv_sem=recv_sems.at[2 * i + k], device_id=(px, py, c), device_id_type=MESH)
                send = pltpu.make_async_remote_copy(src_ref=mine, dst_ref=mine, **sems)
                recv = pltpu.make_async_remote_copy(src_ref=mine, dst_ref=_cut(lands[i], ax, chip=2 * px + py, half=c), **sems)
                copies.append((send, recv))
        return copies
    return plan


def _gather_far_plan(axes):
    def plan(srcs, lands, send_sems, recv_sems):
        x, y, c = _mesh_pos()
        from_chip = 2 * (x ^ (1 - c)) + (y ^ c)
        to = (x ^ c, y ^ (1 - c), c)
        diag = 2 * (1 - x) + (1 - y)
        copies = []
        for i, ax in enumerate(axes):
            passed = _cut(lands[i], ax, chip=from_chip, half=c)
            sems = dict(send_sem=send_sems.at[i], recv_sem=recv_sems.at[i], device_id=to, device_id_type=MESH)
            send = pltpu.make_async_remote_copy(src_ref=passed, dst_ref=passed, **sems)
            recv = pltpu.make_async_remote_copy(src_ref=passed, dst_ref=_cut(lands[i], ax, chip=diag, half=c), **sems)
            copies.append((send, recv))
        return copies
    return plan


def _pair_plan(axes):
    def plan(srcs, lands, send_sems, recv_sems):
        x, y, c = _mesh_pos()
        copies = []
        for i, ax in enumerate(axes):
            cp = pltpu.make_async_remote_copy(
                src_ref=_cut(srcs[i], ax, half=1 - c), dst_ref=lands[i], send_sem=send_sems.at[i], recv_sem=recv_sems.at[i],
                device_id=(x, y, 1 - c), device_id_type=MESH)
            copies.append((cp, cp))
        return copies
    return plan


def _scatter_plan(axes):
    def plan(srcs, lands, send_sems, recv_sems):
        x, y, c = _mesh_pos()
        copies = []
        for i, ax in enumerate(axes):
            for k in range(1, N_CHIPS):
                px, py = x ^ (k >> 1), y ^ (k & 1)
                j = i * (N_CHIPS - 1) + k - 1
                cp = pltpu.make_async_remote_copy(
                    src_ref=_cut(srcs[i], ax, chip=2 * px + py), dst_ref=lands[i].at[k - 1],
                    send_sem=send_sems.at[j], recv_sem=recv_sems.at[j], device_id=(px, py, c), device_id_type=MESH)
                copies.append((cp, cp))
        return copies
    return plan


HBM = pl.BlockSpec(memory_space=pltpu.HBM)
SEM = pl.BlockSpec(memory_space=pltpu.SEMAPHORE)
EFFECT = pltpu.SideEffectType.DATAFLOW_SIDE_EFFECTING


def _in_hbm(arrays):
    return [pltpu.with_memory_space_constraint(a, pltpu.HBM) for a in arrays]


def _split_start(srcs, lands, plan, n_copies, *, name):
    bufs = list(srcs) + list(lands)
    ns, nb = len(srcs), len(bufs)

    def body(*refs):
        send_sems, recv_sems, token = refs[nb], refs[nb + 1], refs[-1]
        for send, _ in plan(refs[:ns], refs[ns:nb], send_sems, recv_sems):
            send.start()
        token[...] = jnp.zeros_like(token)

    sems = pltpu.SemaphoreType.DMA((n_copies,))
    res = pl.pallas_call(
        body, name=name, in_specs=[HBM] * nb,
        out_specs=[SEM, SEM] + [HBM] * nb + [pl.BlockSpec(memory_space=pltpu.VMEM)],
        out_shape=[sems, sems] + [pltpu.HBM(a.shape, a.dtype) for a in bufs] + [jax.ShapeDtypeStruct((8, LANES), F32)],
        input_output_aliases={j: 2 + j for j in range(nb)},
        compiler_params=pltpu.CompilerParams(has_side_effects=EFFECT),
    )(*_in_hbm(bufs))
    return res[0], res[1], res[2:2 + ns], res[2 + ns:2 + nb], res[-1]


def _split_wait(started, after, plan, *, name, with_srcs=False):
    send_sems, recv_sems, srcs, lands, _ = started
    bufs = list(srcs) + list(lands)
    ns, nb = len(srcs), len(bufs)

    def body(*refs):
        for send, recv in plan(refs[:ns], refs[ns:nb], refs[nb], refs[nb + 1]):
            send.wait_send()
            recv.wait_recv()

    res = pl.pallas_call(
        body, name=name, in_specs=[HBM] * nb + [SEM, SEM, ANY], out_specs=[HBM] * nb,
        out_shape=[pltpu.HBM(a.shape, a.dtype) for a in bufs],
        input_output_aliases={j: j for j in range(nb)},
        compiler_params=pltpu.CompilerParams(has_side_effects=EFFECT),
    )(*bufs, send_sems, recv_sems, after)
    return (res[:ns], res[ns:]) if with_srcs else res[ns:]


def _cast_into_full(w3, layer, ax, chip, *, after=None, name):
    _, r, wd = w3.shape
    tr = _rows_per_tile(r, wd, 4 << 20)
    nt = r // tr
    full_shape = (r, wd * N_CHIPS) if ax == 1 else (r * N_CHIPS, wd)
    out_map = (lambda i, ch: (i, ch[0])) if ax == 1 else (lambda i, ch: (ch[0] * nt + i, 0))
    zero = jnp.zeros((1, wd), F32) + (0.0 if after is None else after)

    def body(chip_ref, w_ref, z_ref, o_ref):
        o_ref[...] = (w_ref[...] + z_ref[...]).astype(o_ref.dtype)

    return pl.pallas_call(
        body, name=name,
        grid_spec=pltpu.PrefetchScalarGridSpec(
            num_scalar_prefetch=1, grid=(nt,),
            in_specs=[pl.BlockSpec((None, tr, wd), lambda i, ch: (layer, i, 0)), pl.BlockSpec((1, wd), lambda i, ch: (0, 0))],
            out_specs=pl.BlockSpec((tr, wd), out_map)),
        out_shape=jax.ShapeDtypeStruct(full_shape, BF16),
        compiler_params=_params(("parallel",)),
    )(jnp.reshape(chip, (1,)).astype(jnp.int32), w3, zero)


def _forward_halves(fulls, axes, *, name):
    n = len(fulls)

    def body(*refs):
        bufs = refs[:n]
        send_sems, recv_sems = refs[2 * n:]
        x, y, c = _mesh_pos()
        sends = []
        for i in range(n):
            for k in range(1, N_CHIPS):
                landed = _cut(bufs[i], axes[i], chip=2 * (x ^ (k >> 1)) + (y ^ (k & 1)), half=c)
                cp = pltpu.make_async_remote_copy(
                    src_ref=landed, dst_ref=landed, send_sem=send_sems.at[i, k - 1], recv_sem=recv_sems.at[i, k - 1],
                    device_id=(x, y, 1 - c), device_id_type=MESH)
                cp.start()
                sends.append(cp)
        for i in range(n):
            for k in range(1, N_CHIPS):
                other = _cut(bufs[i], axes[i], chip=2 * (x ^ (k >> 1)) + (y ^ (k & 1)), half=1 - c)
                _wait_recv(other, send_sems.at[i, k - 1], recv_sems.at[i, k - 1])
        for cp in sends:
            cp.wait_send()

    pairs = pltpu.SemaphoreType.DMA((n, N_CHIPS - 1))
    return pl.pallas_call(
        body, name=name, in_specs=[ANY] * n, out_specs=[ANY] * n,
        out_shape=[jax.ShapeDtypeStruct(a.shape, a.dtype) for a in fulls],
        input_output_aliases={j: j for j in range(n)},
        scratch_shapes=[pairs, pairs],
    )(*fulls)


def _share_halves_in_place(bufs, axes, *, name):
    n = len(bufs)

    def body(*refs):
        ins = refs[:n]
        send_sems, recv_sems = refs[2 * n:]
        x, y, c = _mesh_pos()
        sends = []
        for i in range(n):
            mine = _cut(ins[i], axes[i], half=c)
            cp = pltpu.make_async_remote_copy(
                src_ref=mine, dst_ref=mine, send_sem=send_sems.at[i], recv_sem=recv_sems.at[i],
                device_id=(x, y, 1 - c), device_id_type=MESH)
            cp.start()
            sends.append(cp)
        for i in range(n):
            _wait_recv(_cut(ins[i], axes[i], half=1 - c), send_sems.at[i], recv_sems.at[i])
        for cp in sends:
            cp.wait_send()

    sems = pltpu.SemaphoreType.DMA((n,))
    return pl.pallas_call(
        body, name=name, in_specs=[ANY] * n, out_specs=[ANY] * n,
        out_shape=[jax.ShapeDtypeStruct(b.shape, b.dtype) for b in bufs],
        input_output_aliases={j: j for j in range(n)}, scratch_shapes=[sems, sems],
    )(*bufs)


def _adamw_math(w, g, m, v):
    m = ADAM_B1 * m + (1.0 - ADAM_B1) * g
    v = ADAM_B2 * v + (1.0 - ADAM_B2) * (g * g)
    m_hat = m / (1.0 - ADAM_B1 ** ADAM_STEP)
    v_hat = v / (1.0 - ADAM_B2 ** ADAM_STEP)
    delta = -ADAM_LR * (m_hat / (jnp.sqrt(v_hat) + ADAM_EPS) + ADAM_WD * w)
    return delta, m, v


def _adamw_layer(w3, m3, v3, p, q, layer, prev, *, name):
    nl, rows, width = w3.shape
    tr = _rows_per_tile(rows, width)

    def fn(*t):
        if q is None:
            w, m, v, g = t
        else:
            w, m, v, g, g2 = t
            g = g + g2
        delta, m, v = _adamw_math(w, g, m, v)
        return g, delta, m, v

    ins = [('t', w3, 0, width, layer), ('t', m3, 0, width, layer), ('t', v3, 0, width, layer), ('t', p, 0, width)]
    if q is not None:
        ins.append(('t', q, 0, width))
    outs = [('t', width, F32, layer, nl)] * 4
    aliases = None if prev is None else [(prev[i], i) for i in range(4)]
    return _ew(fn, ins, outs, rows=rows, tr=tr, name=name, aliases=aliases)


def _pack_rows(vec):
    n = vec.shape[0]
    r = -(-n // (8 * LANES)) * 8
    return jnp.pad(vec, (0, r * LANES - n)).reshape(r, LANES)


def kernel(x, c, ctx, c_ctx, ada_w, ada_b, norm_g, w_in, na_rpb, ret_decay_logit, w_proj_na, w_proj_ret, w_out, final_g, loss_target, m_c_ctx, m_ada_w, m_ada_b, m_norm_g, m_w_in, m_na_rpb, m_ret_decay_logit, m_w_proj_na, m_w_proj_ret, m_w_out, m_final_g, v_c_ctx, v_ada_w, v_ada_b, v_norm_g, v_w_in, v_na_rpb, v_ret_decay_logit, v_w_proj_na, v_w_proj_ret, v_w_out, v_final_g):
    depth = w_in.shape[0]
    s_len, d_model = x.shape[1], x.shape[2]
    l_len = ctx.shape[1]
    t_len = s_len + l_len
    na_heads = na_rpb.shape[1]
    ret_heads = ret_decay_logit.shape[2]
    w_na = na_heads * NA_HEAD_DIM
    w_qk = ret_heads * RET_KEY_DIM
    w_v = ret_heads * RET_VAL_DIM
    in_cols = w_in.shape[2] * N_CHIPS
    assert in_cols == 4 * w_na + 2 * w_qk + 2 * w_v + 2 * d_model
    assert x.shape[0] == 1 and s_len % (NA_WIN_ROWS * GRID_W) == 0 and l_len % RET_CHUNK == 0
    off = np.cumsum([0, w_na, w_na, w_na, w_na, w_qk, w_qk, w_v, w_v, d_model, d_model])
    o_naz, o_retq, o_retz, o_gna, o_gret = int(off[3]), int(off[4]), int(off[7]), int(off[8]), int(off[9])
    rows = s_len // GRID_W
    tr = _tile(l_len, 256, 8)
    n0 = s_len // tr
    mod_cols = 3 * d_model
    mod_shard = ada_w.shape[2]

    xi, yi, ci = _mesh_pos()
    me = 4 * xi + 2 * yi + ci
    chip = 2 * xi + yi

    big_axes = [1, 1, 0, 0]
    n_big = len(big_axes) * (N_CHIPS - 1)
    gather_plan, scatter_plan = _gather_plan(big_axes), _scatter_plan(big_axes)

    c_silu = c[0] * _sigmoid(c[0])
    cc_silu = c_ctx * _sigmoid(c_ctx)
    c_all = _all_gather_small(_pack_rows(c_silu), name="gather_c")[:, :d_model // LANES].reshape(N_DEV, d_model)
    a_rows = jnp.concatenate([c_all, cc_silu[None], jnp.zeros((16 - N_DEV - 1, d_model), F32)], axis=0)
    mod_part = jnp.stack([_mm(a_rows, ada_w, b_lead=l, out_dtype=F32, name="ada_fwd_%d" % l) for l in range(depth)])
    mod_all = _all_gather_small(_pack_rows(mod_part.reshape(-1)), name="gather_mod")
    n_mod = depth * 16 * mod_shard
    mod_all = mod_all.reshape(N_DEV, -1)[:, :n_mod].reshape(N_CHIPS, 2, depth, 16, mod_shard)[:, 0]
    mod_all = jnp.transpose(mod_all, (1, 2, 0, 3)).reshape(depth, 16, mod_cols) + ada_b[:, None, :]

    big_named = list(zip((w_in, w_proj_na, w_proj_ret, w_out), big_axes, ("w_in", "w_proj_na", "w_proj_ret", "w_out")))
    w_in0 = _cast_into_full(w_in, 0, big_axes[0], chip, name="cast_w_in_0")
    mod_all, w_in0 = lax.optimization_barrier((mod_all, w_in0))
    plan_near, plan_far, plan_rest = _gather_near_plan(big_axes[:1]), _gather_far_plan(big_axes[:1]), _gather_plan(big_axes[1:])
    near_all, far_all = _gather_near_plan(big_axes), _gather_far_plan(big_axes)
    first_gather = _split_start([], [w_in0], plan_near, 2, name="gather_start_0_in")
    start_token = first_gather[4][0, 0]
    fulls = [[None if (l == 0 and tag == "w_in") else _cast_into_full(w, l, ax, chip, after=start_token, name="cast_%s_%d" % (tag, l))
              for w, ax, tag in big_named] for l in range(depth)]
    mod_lat = lax.dynamic_index_in_dim(mod_all, me, axis=1, keepdims=False)
    mod_ctx = mod_all[:, N_DEV]
    biases = [_na_bias_layout(_na_bias_table(na_rpb[l], s_len // GRID_W, name="na_bias_%d" % l)) for l in range(depth)]
    biases, fulls = lax.optimization_barrier((biases, fulls))
    landed_near = _split_wait(first_gather, biases[-1], plan_near, name="gather_wait_0_in")
    passing = _split_start([], landed_near, plan_far, 1, name="gather_pass_0_in")
    front_token = passing[4][0, 0]

    c2, s2 = _rope_tables(s_len, l_len)
    log_gamma = jax.nn.log_sigmoid(ret_decay_logit)
    x_all = jnp.concatenate([x[0], ctx[0]], axis=0)

    def grp(lat_vec, ctx_vec):
        return jnp.stack([lat_vec, ctx_vec])[:, None, :]

    saved, full_w = [], []
    for l in range(depth):
        shift, scale, gate = [grp(mod_lat[l, i * d_model:(i + 1) * d_model], mod_ctx[l, i * d_model:(i + 1) * d_model])
                              for i in range(3)]
        gs = norm_g[l][None, None, :] * (1.0 + scale) + front_token

        def modnorm(xt, gs_t, sh_t):
            r = lax.rsqrt(jnp.mean(xt * xt, axis=-1, keepdims=True) + NORM_EPS)
            return xt * r * gs_t + sh_t

        h, = _ew(modnorm, [('t', x_all, 0, d_model), ('g', gs), ('g', shift)], [('t', d_model, BF16)],
                 rows=t_len, tr=tr, n0=n0, name="modnorm_%d" % l)
        bias = biases[l]
        if l == 0:
            h, bias = lax.optimization_barrier((h, bias))
            landed_in = _split_wait(passing, h, plan_far, name="gather_wait_0_in_far")
            landed_in, rest0, later = lax.optimization_barrier((landed_in, fulls[0][1:], fulls[1:]))
            rest_gather = _split_start([], rest0, plan_rest, n_big - (N_CHIPS - 1), name="gather_start_0_rest")
            later_gathers = [_split_start([], later[j], near_all, 2 * len(big_axes), name="gather_start_%d" % (j + 1))
                             for j in range(depth - 1)]
            win_f, = _forward_halves(landed_in, big_axes[:1], name="gather_forward_0_in")
            win_f, tokens = lax.optimization_barrier((win_f, [rest_gather[4]] + [g[4] for g in later_gathers]))
            gate = gate + sum(t[0, 0] for t in tokens)
        else:
            h, bias = lax.optimization_barrier((h, bias))
            landed = _split_wait(later_passes[l - 1], h, far_all, name="gather_wait_%d" % l)
            win_f, wpn_f, wpr_f, wout_f = _forward_halves(landed, big_axes, name="gather_forward_%d" % l)
        u = _mm(h, win_f, tm=1152, tn=1024, name="in_proj_%d" % l)
        o_na = _na_fwd(u, bias, s_len=s_len, heads=na_heads, name="na_fwd_%d" % l)
        o_ret, states = _ret_fwd(u, c2, s2, log_gamma[l], s_len=s_len, heads=ret_heads, q_off=o_retq, name="ret_fwd_%d" % l)

        def act(o1, z1, o2, z2):
            a1 = o1.astype(F32) * _silu_parts(z1.astype(F32))[0]
            sz = _silu_parts(z2.astype(F32))[0]
            outs = []
            for hh in range(ret_heads):
                sl = slice(hh * RET_VAL_DIM, (hh + 1) * RET_VAL_DIM)
                oh = o2[:, sl]
                r = lax.rsqrt(jnp.mean(oh * oh, axis=-1, keepdims=True) + NORM_EPS)
                outs.append(oh * r * sz[:, sl])
            return a1, jnp.concatenate(outs, axis=-1)

        a_na, a_ret = _ew(act, [('t', o_na, 0, w_na), ('t', u, o_naz // w_na, w_na), ('t', o_ret, 0, w_v), ('t', u, o_retz // w_v, w_v)],
                          [('t', w_na, BF16), ('t', w_v, BF16)], rows=t_len, tr=tr, name="act_%d" % l)
        if l == 0:
            landed_rest = _split_wait(rest_gather, a_na, plan_rest, name="gather_wait_0_rest")
            later_passes = [_split_start([], _split_wait(later_gathers[j], a_na, near_all, name="gather_near_%d" % (j + 1)),
                                         far_all, len(big_axes), name="gather_pass_%d" % (j + 1)) for j in range(depth - 1)]
            landed_rest, tokens = lax.optimization_barrier((landed_rest, [g[4] for g in later_passes]))
            gate = gate + sum(t[0, 0] for t in tokens)
            wpn_f, wpr_f, wout_f = _forward_halves(landed_rest, big_axes[1:], name="gather_forward_0_rest")
        full_w.append((win_f, wpn_f, wpr_f, wout_f))
        y_na = _mm(a_na, wpn_f, name="proj_na_%d" % l)
        y_ret = _mm(a_ret, wpr_f, name="proj_ret_%d" % l)

        def merge(y1, y2, g1, g2):
            return _sigmoid(g1.astype(F32)) * y1.astype(F32) + _sigmoid(g2.astype(F32)) * y2.astype(F32)

        merged, = _ew(merge, [('t', y_na, 0, d_model), ('t', y_ret, 0, d_model), ('t', u, o_gna // d_model, d_model), ('t', u, o_gret // d_model, d_model)],
                      [('t', d_model, BF16)], rows=t_len, tr=tr, name="merge_%d" % l)
        out = _mm(merged, wout_f, out_dtype=F32, name="out_proj_%d" % l)
        x_new, = _ew(lambda xt, ot, gt: xt + gt * ot, [('t', x_all, 0, d_model), ('t', out, 0, d_model), ('g', gate)],
                     [('t', d_model, F32)], rows=t_len, tr=tr, n0=n0, name="resid_%d" % l)
        saved.append(dict(x=x_all, h=h, u=u, bias=bias, o_na=o_na, o_ret=o_ret, states=states, a_na=a_na, a_ret=a_ret,
                          y_na=y_na, y_ret=y_ret, merged=merged, out=out, gate=gate, gs=gs, scale=scale))
        x_all = x_new

    def final(xt, tt, gt):
        r = lax.rsqrt(jnp.mean(xt * xt, axis=-1, keepdims=True) + NORM_EPS)
        xh = xt * r
        e = xh * gt - tt
        dy = e * (1.0 / d_model)
        dyg = dy * gt
        dx = r * (dyg - xh * jnp.mean(dyg * xh, axis=-1, keepdims=True))
        return dx, _rsum(dy * xh), _rsum(e * e)

    dx_lat, d_final_g, loss_cols = _ew(final, [('t', x_all, 0, d_model), ('t', loss_target[0], 0, d_model), ('g', final_g[None, None, :])],
                                       [('t', d_model, F32), ('r', d_model, 1), ('r', d_model, 1)], rows=s_len, tr=tr, name="final")
    loss_part = (0.5 / d_model) * jnp.sum(loss_cols)
    dx_all = jnp.concatenate([dx_lat, jnp.zeros((l_len, d_model), F32)], axis=0)

    big_w = [(w_in, m_w_in, v_w_in), (w_proj_na, m_w_proj_na, v_w_proj_na), (w_proj_ret, m_w_proj_ret, v_w_proj_ret), (w_out, m_w_out, v_w_out)]
    big_res = [None] * 4
    scatters = {}
    back_token = jnp.zeros((), F32)

    pairs = {}

    def start_pair(key, grads, axes):
        plan = _pair_plan(axes)
        lands = []
        for g, ax in zip(grads, axes):
            shp = list(g.shape)
            shp[1 - ax] //= 2
            lands.append(lax.empty(tuple(shp), BF16))
        pairs[key] = (_split_start(grads, lands, plan, len(axes), name="pair_start_%s" % key), axes, plan)
        return pairs[key][0][4]

    def start_scatter(key, after):
        started, axes, pair_plan = pairs[key]
        grads, theirs = _split_wait(started, after, pair_plan, name="pair_wait_%s" % key, with_srcs=True)
        plan = _scatter_plan(axes)
        pair = [_sum_pair(g, t, ax, ci, name="sum_pair_%s_%d" % (key, i)) for i, (g, t, ax) in enumerate(zip(grads, theirs, axes))]
        own = [lax.dynamic_slice_in_dim(s, chip * (s.shape[ax] // N_CHIPS), s.shape[ax] // N_CHIPS, axis=ax) for s, ax in zip(pair, axes)]
        lands = [lax.empty((N_CHIPS - 1,) + o.shape, BF16) for o in own]
        started = _split_start(pair, lands, plan, len(axes) * (N_CHIPS - 1), name="scatter_start_%s" % key)
        scatters[key] = (started, own, axes, plan)
        return started[4]

    def finish_scatter(key, after):
        started, own, axes, plan = scatters[key]
        recv = _split_wait(started, after, plan, name="scatter_wait_%s" % key)
        bufs = [_sum_chips_into(own[i], rbuf, axes[i], ci, name="sum_chips_%s_%d" % (key, i)) for i, rbuf in enumerate(recv)]
        return _share_halves_in_place(bufs, axes, name="share_halves_%s" % key)

    def adamw_big(l, idx, grads, big_res):
        for i, g in zip(idx, grads):
            w3, m3, v3 = big_w[i]
            big_res[i] = _adamw_layer(w3, m3, v3, g, None, l, big_res[i], name="adamw_big_%d_%d" % (i, l))
        return big_res

    small = dict(dmod_lat=[None] * depth, dmod_ctx=[None] * depth, dnorm_g=[None] * depth, drpb=[None] * depth, ddecay=[None] * depth)
    for l in reversed(range(depth)):
        sv = saved[l]
        win_f, wpn_f, wpr_f, wout_f = full_w[l]

        def resid_bwd(dxt, ot, gt):
            return gt * dxt, _rsum(dxt * ot)

        dout, dgate = _ew(resid_bwd, [('t', dx_all, 0, d_model), ('t', sv['out'], 0, d_model), ('g', sv['gate'] + back_token)],
                          [('t', d_model, BF16), ('r', d_model, 2)], rows=t_len, tr=tr, n0=n0, name="resid_bwd_%d" % l)
        dmerged = _mm(dout, wout_f, tb=True, name="out_proj_dx_%d" % l)
        g_wout = _mm(sv['merged'], dout, ta=True, tm=1024, tk=t_len, name="out_proj_dw_%d" % l)

        def merge_bwd(dm, y1, y2, g1, g2):
            dm = dm.astype(F32)
            s1, s2_ = _sigmoid(g1.astype(F32)), _sigmoid(g2.astype(F32))
            return dm * s1, dm * s2_, dm * y1.astype(F32) * s1 * (1.0 - s1), dm * y2.astype(F32) * s2_ * (1.0 - s2_)

        u = sv['u']
        dy_na, dy_ret, dg_na, dg_ret = _ew(
            merge_bwd, [('t', dmerged, 0, d_model), ('t', sv['y_na'], 0, d_model), ('t', sv['y_ret'], 0, d_model),
                        ('t', u, o_gna // d_model, d_model), ('t', u, o_gret // d_model, d_model)],
            [('t', d_model, BF16)] * 4, rows=t_len, tr=tr, name="merge_bwd_%d" % l)
        da_na = _mm(dy_na, wpn_f, tb=True, name="proj_na_dx_%d" % l)
        g_wpn = _mm(sv['a_na'], dy_na, ta=True, tm=1024, tk=t_len, name="proj_na_dw_%d" % l)
        da_ret = _mm(dy_ret, wpr_f, tb=True, name="proj_ret_dx_%d" % l)
        g_wpr = _mm(sv['a_ret'], dy_ret, ta=True, tm=1024, tk=t_len, name="proj_ret_dw_%d" % l)
        lg_l = log_gamma[l]
        if l == 0:
            pair_token = start_pair("0_rest", [g_wpn, g_wpr, g_wout], big_axes[1:])

        def act_bwd(da1, o1, z1, da2, o2, z2):
            da1, da2 = da1.astype(F32), da2.astype(F32)
            si1, ds1 = _silu_parts(z1.astype(F32))
            si2, ds2 = _silu_parts(z2.astype(F32))
            do1 = da1 * si1
            dz1 = da1 * o1.astype(F32) * ds1
            dn = da2 * si2
            do2, dz2 = [], []
            for hh in range(ret_heads):
                sl = slice(hh * RET_VAL_DIM, (hh + 1) * RET_VAL_DIM)
                oh = o2[:, sl]
                r = lax.rsqrt(jnp.mean(oh * oh, axis=-1, keepdims=True) + NORM_EPS)
                nh = oh * r
                dz2.append(da2[:, sl] * nh * ds2[:, sl])
                do2.append(r * (dn[:, sl] - nh * jnp.mean(dn[:, sl] * nh, axis=-1, keepdims=True)))
            return do1, dz1, jnp.concatenate(do2, axis=-1), jnp.concatenate(dz2, axis=-1)

        do_na, dz_na, do_ret, dz_ret = _ew(
            act_bwd, [('t', da_na, 0, w_na), ('t', sv['o_na'], 0, w_na), ('t', u, o_naz // w_na, w_na),
                      ('t', da_ret, 0, w_v), ('t', sv['o_ret'], 0, w_v), ('t', u, o_retz // w_v, w_v)],
            [('t', w_na, BF16), ('t', w_na, BF16), ('t', w_v, BF16), ('t', w_v, BF16)], rows=t_len, tr=tr, name="act_bwd_%d" % l)
        dq_na, dk_na, dv_na, dbias = _na_bwd(u, sv['bias'], sv['o_na'], do_na, s_len=s_len, heads=na_heads, name="na_bwd_%d" % l)
        small['drpb'][l] = _rpb_grad(dbias, name="rpb_grad_%d" % l)
        if l == 0:
            lg_l = lg_l + start_scatter("0_rest", dq_na)[0, 0] + pair_token[0, 0]
        dq_r, dk_r, dv_r, dlg = _ret_bwd(u, c2, s2, lg_l, sv['states'], do_ret, s_len=s_len, heads=ret_heads,
                                         q_off=o_retq, name="ret_bwd_%d" % l)
        small['ddecay'][l] = jnp.transpose(dlg[:, :, 0, 0]) * _sigmoid(-ret_decay_logit[l])
        du_parts = [dq_na, dk_na, dv_na, dz_na, dq_r, dk_r, dv_r, dz_ret, dg_na, dg_ret]
        du, = _ew(lambda *t: jnp.concatenate(t, axis=-1), [('t', p, 0, p.shape[1]) for p in du_parts], [('t', in_cols, BF16)],
                  rows=t_len, tr=tr, name="du_concat_%d" % l)
        g_win = _mm(sv['h'], du, ta=True, tm=1024, tn=1024, tk=t_len, name="in_proj_dw_%d" % l)
        if l > 0:
            du, pair_token = lax.optimization_barrier((du, start_pair("%d_all" % l, [g_win, g_wpn, g_wpr, g_wout], big_axes)))
        else:
            du, in_token = lax.optimization_barrier((du, start_pair("0_in", [g_win], big_axes[:1])))
        dh = _mm(du, win_f, tb=True, out_dtype=F32, tm=1152, tn=1024, name="in_proj_dx_%d" % l)

        def modnorm_bwd(xt, dht, dxt, gs_t):
            r = lax.rsqrt(jnp.mean(xt * xt, axis=-1, keepdims=True) + NORM_EPS)
            xh = xt * r
            dhg = dht * gs_t
            dx = r * (dhg - xh * jnp.mean(dhg * xh, axis=-1, keepdims=True)) + dxt
            return dx, _rsum(dht), _rsum(dht * xh)

        dx_all, dshift, dgs = _ew(modnorm_bwd, [('t', sv['x'], 0, d_model), ('t', dh, 0, d_model), ('t', dx_all, 0, d_model), ('g', sv['gs'])],
                                  [('t', d_model, F32), ('r', d_model, 2), ('r', d_model, 2)], rows=t_len, tr=tr, n0=n0, name="modnorm_bwd_%d" % l)
        dscale = dgs * norm_g[l][None, None, :]
        small['dnorm_g'][l] = jnp.sum(dgs * (1.0 + sv['scale']), axis=(0, 1))
        dmod = jnp.concatenate([dshift, dscale, dgate], axis=-1)[:, 0]
        small['dmod_lat'][l], small['dmod_ctx'][l] = dmod[0], dmod[1]

        if l > 0:
            back_token = start_scatter("%d_all" % l, dx_all)[0, 0] + pair_token[0, 0]

    grad_x = dx_all[:s_len][None]

    drpb = jnp.stack(small['drpb']).reshape(-1)
    ddecay = jnp.stack(small['ddecay']).reshape(-1)
    pieces = [jnp.stack(small['dmod_lat']).reshape(-1), jnp.stack(small['dmod_ctx']).reshape(-1),
              jnp.stack(small['dnorm_g']).reshape(-1), d_final_g.reshape(-1), drpb, ddecay, loss_part[None]]
    sizes = [int(p.shape[0]) for p in pieces]
    pads = [-(-s // LANES) * LANES for s in sizes]
    packed = jnp.concatenate([jnp.pad(p, (0, pd - s)) for p, s, pd in zip(pieces, sizes, pads)])
    gathered = _all_gather_small(_pack_rows(packed), name="gather_small_grads")
    r_small = gathered.shape[1]

    def sum8(*t):
        acc = t[0]
        for other in t[1:]:
            acc = acc + other
        return acc

    total, = _ew(sum8, [('t', gathered, 0, LANES, k) for k in range(N_DEV)], [('t', LANES, F32)], rows=r_small, tr=r_small, name="sum_devices")
    total = total.reshape(-1)
    starts = np.cumsum([0] + pads)
    g_mod_lat_sum, g_mod_ctx, g_norm_g, g_final_g, g_rpb, g_decay, loss = [total[starts[i]:starts[i] + sizes[i]] for i in range(len(pieces))]
    loss = loss[0]
    g_ada_b = (g_mod_lat_sum + g_mod_ctx).reshape(depth, mod_cols)
    g_mod_ctx = g_mod_ctx.reshape(depth, mod_cols)
    dmod_lat_all = gathered.reshape(N_DEV, -1)[:, :depth * mod_cols].reshape(N_DEV, depth, mod_cols)

    dcc_part = jnp.zeros((16, d_model), F32)
    ctx_cols = [lax.dynamic_slice_in_dim(g_mod_ctx[l], chip * mod_shard, mod_shard, axis=0) for l in range(depth)]
    for l in reversed(range(depth)):
        c_rows = jnp.concatenate([ctx_cols[l][None], jnp.zeros((15, mod_shard), F32)], axis=0)
        dcc_part = dcc_part + _mm(c_rows, ada_w, tb=True, b_lead=l, out_dtype=F32, name="ada_dc_%d" % l)
    dcc_all = _all_gather_small(_pack_rows(dcc_part[0]), name="gather_dcc")[:, :d_model // LANES].reshape(N_CHIPS, 2, d_model)[:, 0]

    tail_token = start_scatter("0_in", dcc_all) + in_token
    dcc = ((dcc_all[0] + dcc_all[1]) + dcc_all[2]) + dcc_all[3]
    sg = _sigmoid(c_ctx)
    g_c_ctx = dcc * (sg * (1.0 + c_ctx * (1.0 - sg)))
    for l in reversed(range(1, depth)):
        big_res = adamw_big(l, range(4), finish_scatter("%d_all" % l, tail_token), big_res)

    ada_res = None
    for l in reversed(range(depth)):
        lat_cols = lax.dynamic_slice_in_dim(dmod_lat_all[:, l], chip * mod_shard, mod_shard, axis=1)
        d_rows = jnp.concatenate([lat_cols, ctx_cols[l][None], jnp.zeros((16 - N_DEV - 1, mod_shard), F32)], axis=0) + tail_token[0, 0]
        g_ada = _mm(a_rows, d_rows, ta=True, out_dtype=F32, tm=512, name="ada_dw_%d" % l)
        ada_res = _adamw_layer(ada_w, m_ada_w, v_ada_w, g_ada, None, l, ada_res, name="adamw_ada_%d" % l)

    small_w = [(c_ctx, m_c_ctx, v_c_ctx, g_c_ctx), (ada_b, m_ada_b, v_ada_b, g_ada_b),
               (norm_g, m_norm_g, v_norm_g, g_norm_g), (na_rpb, m_na_rpb, v_na_rpb, g_rpb),
               (ret_decay_logit, m_ret_decay_logit, v_ret_decay_logit, g_decay), (final_g, m_final_g, v_final_g, g_final_g)]
    sw_sizes = [int(np.prod(t[0].shape)) for t in small_w]
    sw_pads = [-(-s // LANES) * LANES for s in sw_sizes]

    def pack(j):
        return _pack_rows(jnp.concatenate([jnp.pad(t[j].reshape(-1), (0, pd - s)) for t, s, pd in zip(small_w, sw_sizes, sw_pads)]))

    pw_, pm_, pv_, pg_ = pack(0), pack(1), pack(2), pack(3)
    sw_out = _ew(lambda w, m, v, g: (g,) + _adamw_math(w, g, m, v),
                 [('t', pw_, 0, LANES), ('t', pm_, 0, LANES), ('t', pv_, 0, LANES), ('t', pg_, 0, LANES)],
                 [('t', LANES, F32)] * 4, rows=pw_.shape[0], tr=pw_.shape[0], name="adamw_small")
    sw_starts = np.cumsum([0] + sw_pads)
    sw_out, ada_res, big_res = lax.optimization_barrier((sw_out, ada_res, big_res))
    big_res = adamw_big(0, range(1, 4), finish_scatter("0_rest", sw_out[0]), big_res)
    big_res = adamw_big(0, range(1), finish_scatter("0_in", sw_out[1]), big_res)

    def unpack(arr, i):
        return arr.reshape(-1)[sw_starts[i]:sw_starts[i] + sw_sizes[i]].reshape(small_w[i][0].shape)

    sm = [[unpack(sw_out[j], i) for i in range(len(small_w))] for j in range(4)]
    def ordered(j):
        return [sm[j][0], ada_res[j], sm[j][1], sm[j][2], big_res[0][j], sm[j][3], sm[j][4],
                big_res[1][j], big_res[2][j], big_res[3][j], sm[j][5]]

    return (loss, grad_x, *ordered(0), *ordered(1), *ordered(2), *ordered(3))
```

```python
import functools
import math

import numpy as np
import jax
import jax.numpy as jnp
from jax import lax
from jax.experimental import pallas as pl
from jax.experimental.pallas import tpu as pltpu

GRID_W = 64
NA_HEAD_DIM = 128
NA_WIN_ROWS = 8
NA_WIN_COLS = 16
NA_GROUP = 8
RET_GROUPS = (1, 2, 3)
RET_KEY_DIM = 128
RET_VAL_DIM = 256
RET_CHUNK = 128
ROPE_BASE = 10000.0
NORM_EPS = 1e-6
MASK_VALUE = -1e30
ADAM_LR = 0.001
ADAM_B1 = 0.9
ADAM_B2 = 0.999
ADAM_EPS = 1e-08
ADAM_WD = 0.01
ADAM_STEP = 10

N_CHIPS = 4
N_DEV = 8
LANES = 128
VMEM_LIMIT = 56 * 1024 * 1024
BF16 = jnp.bfloat16
F32 = jnp.float32
MESH = pl.DeviceIdType.MESH
ANY = pl.BlockSpec(memory_space=pl.ANY)


def _tile(dim, pref, align=LANES):
    if dim <= pref:
        return dim
    t = (pref // align) * align
    while t >= align:
        if dim % t == 0:
            return t
        t -= align
    return dim


def _rows_per_tile(rows, width, tile_bytes=1 << 20):
    return _tile(rows, max(8, tile_bytes // (4 * width)), 8)


def _params(sem):
    return pltpu.CompilerParams(dimension_semantics=sem, vmem_limit_bytes=VMEM_LIMIT)


def _sigmoid(x):
    return 1.0 / (1.0 + jnp.exp(-x))


def _dot(a, b, ca, cb):
    return lax.dot_general(a, b, (((ca,), (cb,)), ((), ())), preferred_element_type=F32)


def _mm(a, b, *, ta=False, tb=False, a_lead=None, b_lead=None, out_dtype=BF16, tm=1152, tn=1024, tk=2048, name):
    ash = a.shape[1:] if a_lead is not None else a.shape
    bsh = b.shape[1:] if b_lead is not None else b.shape
    m, k = (ash[1], ash[0]) if ta else ash
    n, k2 = bsh if tb else (bsh[1], bsh[0])
    assert k == k2, (name, ash, bsh)
    tm, tn, tk = _tile(m, tm), _tile(n, tn), _tile(k, tk)
    nk = k // tk

    def lead(spec_shape, imap, l):
        if l is None:
            return pl.BlockSpec(spec_shape, imap)
        return pl.BlockSpec((None,) + spec_shape, lambda i, j, kk: (l,) + imap(i, j, kk))

    a_spec = lead((tk, tm), lambda i, j, kk: (kk, i), a_lead) if ta else lead((tm, tk), lambda i, j, kk: (i, kk), a_lead)
    b_spec = lead((tn, tk), lambda i, j, kk: (j, kk), b_lead) if tb else lead((tk, tn), lambda i, j, kk: (kk, j), b_lead)
    ca, cb = (0 if ta else 1), (1 if tb else 0)

    def body(a_ref, b_ref, o_ref, *scratch):
        part = _dot(a_ref[...].astype(BF16), b_ref[...].astype(BF16), ca, cb)
        if nk == 1:
            o_ref[...] = part.astype(o_ref.dtype)
            return
        acc_ref, = scratch
        kk = pl.program_id(2)

        @pl.when(kk == 0)
        def _():
            acc_ref[...] = part

        @pl.when(kk > 0)
        def _():
            acc_ref[...] += part

        @pl.when(kk == nk - 1)
        def _():
            o_ref[...] = acc_ref[...].astype(o_ref.dtype)

    return pl.pallas_call(
        body, name=name, grid=(m // tm, n // tn, nk),
        in_specs=[a_spec, b_spec],
        out_specs=pl.BlockSpec((tm, tn), lambda i, j, kk: (i, j)),
        out_shape=jax.ShapeDtypeStruct((m, n), out_dtype),
        scratch_shapes=[] if nk == 1 else [pltpu.VMEM((tm, tn), F32)],
        compiler_params=_params(("parallel", "parallel", "arbitrary")),
    )(a, b)


def _ew(fn, ins, outs, *, rows, tr, name, n0=None, aliases=None):
    assert rows % tr == 0, (name, rows, tr)
    nt = rows // tr

    def grp(i):
        return 0 if n0 is None else jnp.where(i < n0, 0, 1)

    in_specs, args = [], []
    for spec in ins:
        if spec[0] == 't':
            arr, cb, w = spec[1], spec[2], spec[3]
            l = spec[4] if len(spec) > 4 else None
            if l is None:
                in_specs.append(pl.BlockSpec((tr, w), functools.partial(lambda i, cb: (i, cb), cb=cb)))
            else:
                in_specs.append(pl.BlockSpec((None, tr, w), functools.partial(lambda i, cb, l: (l, i, cb), cb=cb, l=l)))
            args.append(arr)
        else:
            arr = spec[1]
            g = arr.shape[0]
            if g == 1:
                in_specs.append(pl.BlockSpec((None, 1, arr.shape[2]), lambda i: (0, 0, 0)))
            else:
                in_specs.append(pl.BlockSpec((None, 1, arr.shape[2]), lambda i: (grp(i), 0, 0)))
            args.append(arr)
    out_specs, out_shapes, is_red = [], [], []
    for spec in outs:
        if spec[0] == 't':
            w, dt = spec[1], spec[2]
            if len(spec) > 3:
                l, nl = spec[3], spec[4]
                out_specs.append(pl.BlockSpec((None, tr, w), functools.partial(lambda i, l: (l, i, 0), l=l)))
                out_shapes.append(jax.ShapeDtypeStruct((nl, rows, w), dt))
            else:
                out_specs.append(pl.BlockSpec((tr, w), lambda i: (i, 0)))
                out_shapes.append(jax.ShapeDtypeStruct((rows, w), dt))
            is_red.append(False)
        else:
            w, g = spec[1], spec[2]
            if g == 1:
                out_specs.append(pl.BlockSpec((None, 1, w), lambda i: (0, 0, 0)))
            else:
                out_specs.append(pl.BlockSpec((None, 1, w), lambda i: (grp(i), 0, 0)))
            out_shapes.append(jax.ShapeDtypeStruct((g, 1, w), F32))
            is_red.append(True)
    n_in = len(ins)
    n_alias = 0 if aliases is None else len(aliases)

    def body(*refs):
        in_refs = refs[:n_in]
        out_refs = refs[n_in + n_alias:]
        res = fn(*[r[...] for r in in_refs])
        if not isinstance(res, (tuple, list)):
            res = (res,)
        i = pl.program_id(0)
        first = (i == 0) if n0 is None else ((i == 0) | (i == n0))
        for o_ref, val, red in zip(out_refs, res, is_red):
            if not red:
                o_ref[...] = val.astype(o_ref.dtype)
            else:
                @pl.when(first)
                def _(o_ref=o_ref, val=val):
                    o_ref[...] = val

                @pl.when(jnp.logical_not(first))
                def _(o_ref=o_ref, val=val):
                    o_ref[...] += val

    io_alias = {}
    if aliases is not None:
        for a_idx, (arr, o_idx) in enumerate(aliases):
            in_specs.append(ANY)
            args.append(arr)
            io_alias[n_in + a_idx] = o_idx
    has_red = any(is_red)
    return pl.pallas_call(
        body, name=name, grid=(nt,), in_specs=in_specs, out_specs=out_specs, out_shape=out_shapes,
        input_output_aliases=io_alias,
        compiler_params=_params(("arbitrary",) if has_red else ("parallel",)),
    )(*args)


def _half_tiles(pr, pw):
    tr, tc = _tile(pr, 256, 16), _tile(pw, 2048)
    return tr, tc, (pr // tr, pw // tc)


def _half_spec(tr, tc, ax, grid):
    if ax == 1:
        return pl.BlockSpec((tr, tc), lambda i, j, sel: (sel[0] * grid[0] + i, j))
    return pl.BlockSpec((tr, tc), lambda i, j, sel: (i, sel[0] * grid[1] + j))


def _sum_pair(g, theirs, ax, ci, *, name):
    pr, pw = theirs.shape
    tr, tc, grid = _half_tiles(pr, pw)

    def body(sel, a_ref, b_ref, o_ref):
        o_ref[...] = (a_ref[...].astype(F32) + b_ref[...].astype(F32)).astype(o_ref.dtype)

    tile = pl.BlockSpec((tr, tc), lambda i, j, sel: (i, j))
    return pl.pallas_call(
        body, name=name,
        grid_spec=pltpu.PrefetchScalarGridSpec(
            num_scalar_prefetch=1, grid=grid, in_specs=[_half_spec(tr, tc, ax, grid), tile], out_specs=tile),
        out_shape=jax.ShapeDtypeStruct((pr, pw), BF16),
        compiler_params=_params(("parallel", "parallel")),
    )(jnp.reshape(ci, (1,)).astype(jnp.int32), g, theirs)


def _sum_chips_into(own, recv, ax, ci, *, name):
    pr, pw = own.shape
    tr, tc, grid = _half_tiles(pr, pw)
    full_shape = (2 * pr, pw) if ax == 1 else (pr, 2 * pw)

    def body(sel, a_ref, r_ref, o_ref):
        acc = a_ref[...].astype(F32)
        for k in range(N_CHIPS - 1):
            acc = acc + r_ref[k].astype(F32)
        o_ref[...] = acc

    return pl.pallas_call(
        body, name=name,
        grid_spec=pltpu.PrefetchScalarGridSpec(
            num_scalar_prefetch=1, grid=grid,
            in_specs=[pl.BlockSpec((tr, tc), lambda i, j, sel: (i, j)),
                      pl.BlockSpec((N_CHIPS - 1, tr, tc), lambda i, j, sel: (0, i, j))],
            out_specs=_half_spec(tr, tc, ax, grid)),
        out_shape=jax.ShapeDtypeStruct(full_shape, F32),
        compiler_params=_params(("parallel", "parallel")),
    )(jnp.reshape(ci, (1,)).astype(jnp.int32), own, recv)


def _rsum(v):
    return jnp.sum(v, axis=0, keepdims=True)


def _silu_parts(z):
    sg = _sigmoid(z)
    return z * sg, sg * (1.0 + z * (1.0 - sg))


def _na_bias_table(rpb, rows, *, name):
    kh, kw = NA_WIN_ROWS, NA_WIN_COLS
    assert rows >= kh
    heads = rpb.shape[0]
    e1, e2 = _na_onehots()
    rpb16 = jnp.pad(rpb, ((0, 0), (0, 16 - rpb.shape[1]), (0, LANES - rpb.shape[2])))

    def body(r_ref, e1_ref, e2_ref, o_ref):
        e1b = e1_ref[...].astype(BF16)
        y = sum(_dot(e1b, part, 0, 0) for part in _split3(r_ref[...]))
        e2b = e2_ref[...].astype(BF16)
        o_ref[...] = sum(_dot(part, e2b, 1, 1) for part in _split3(y))

    z = pl.pallas_call(
        body, name=name, grid=(heads,),
        in_specs=[pl.BlockSpec((None, 16, LANES), lambda h: (h, 0, 0)),
                  pl.BlockSpec(e1.shape, lambda h: (0, 0)), pl.BlockSpec(e2.shape, lambda h: (0, 0))],
        out_specs=pl.BlockSpec((None, kh * kh, GRID_W * GRID_W), lambda h: (h, 0, 0)),
        out_shape=jax.ShapeDtypeStruct((heads, kh * kh, GRID_W * GRID_W), F32),
        compiler_params=_params(("parallel",)),
    )(rpb16, e1, e2)
    return z


def _na_bias_layout(z):
    heads = z.shape[0]
    kh, kw = NA_WIN_ROWS, NA_WIN_COLS
    cidx = np.arange(GRID_W)
    c0 = np.clip(cidx - kw // 2, 0, GRID_W - kw)
    col_in = (cidx[None, :] >= c0[:, None]) & (cidx[None, :] < c0[:, None] + kw)
    bias = z.reshape(heads, kh, kh, GRID_W, GRID_W).transpose(0, 1, 3, 2, 4)
    bias = jnp.where(col_in[None, None, :, None, :], bias, MASK_VALUE)
    return bias.reshape(heads, kh, GRID_W, kh * GRID_W)


def _na_onehots():
    kh, kw = NA_WIN_ROWS, NA_WIN_COLS
    cidx = np.arange(GRID_W)
    dc = cidx[None, :] - cidx[:, None] + (kw - 1)
    e2 = np.zeros((GRID_W * GRID_W, LANES), np.float32)
    ok = (dc >= 0) & (dc <= 2 * kw - 2)
    cq, ck = np.nonzero(ok)
    e2[cq * GRID_W + ck, dc[cq, ck]] = 1.0
    dr = np.arange(kh)[None, :] - np.arange(kh)[:, None] + (kh - 1)
    e1 = np.zeros((16, kh * kh), np.float32)
    dl, kr = np.nonzero(np.ones_like(dr))
    e1[dr[dl, kr], dl * kh + kr] = 1.0
    return jnp.asarray(e1), jnp.asarray(e2)


def _na_fwd(u, bias, *, s_len, heads, name):
    t_len = u.shape[0]
    rows = s_len // GRID_W
    nloc = NA_WIN_ROWS * GRID_W
    scale = NA_HEAD_DIM ** -0.5
    hd = NA_HEAD_DIM

    def body(q_ref, k_ref, v_ref, b_ref, o_ref):
        kc = k_ref[s_len:t_len, :]
        vc = v_ref[s_len:t_len, :]

        def group(g, carry):
            rs = [g * NA_GROUP + i for i in range(NA_GROUP)]
            r0s = [jnp.clip(r - NA_WIN_ROWS // 2, 0, rows - NA_WIN_ROWS) for r in rs]
            gs_ = pl.multiple_of(g * (NA_GROUP * GRID_W), NA_GROUP * GRID_W)
            kss = [pl.multiple_of(r0 * GRID_W, GRID_W) for r0 in r0s]
            q_all = q_ref[pl.ds(gs_, NA_GROUP * GRID_W), :]
            s_ctx = _dot(q_all, kc, 1, 1) * scale
            s_loc = [_dot(q_all[i * GRID_W:(i + 1) * GRID_W], k_ref[pl.ds(kss[i], nloc), :], 1, 1) * scale + b_ref[rs[i] - r0s[i]]
                     for i in range(NA_GROUP)]
            p_loc, p_ctx, inv = [], [], []
            for i in range(NA_GROUP):
                sc = s_ctx[i * GRID_W:(i + 1) * GRID_W]
                m = jnp.maximum(jnp.max(s_loc[i], axis=-1, keepdims=True), jnp.max(sc, axis=-1, keepdims=True))
                pl_, pc_ = jnp.exp(s_loc[i] - m), jnp.exp(sc - m)
                inv.append(1.0 / (jnp.sum(pl_, axis=-1, keepdims=True) + jnp.sum(pc_, axis=-1, keepdims=True)))
                p_loc.append(pl_.astype(BF16))
                p_ctx.append(pc_.astype(BF16))
            o_ctx = _dot(jnp.concatenate(p_ctx, axis=0), vc, 1, 0)
            o_loc = [_dot(p_loc[i], v_ref[pl.ds(kss[i], nloc), :], 1, 0) for i in range(NA_GROUP)]
            out = jnp.concatenate([(o_loc[i] + o_ctx[i * GRID_W:(i + 1) * GRID_W]) * inv[i] for i in range(NA_GROUP)], axis=0)
            o_ref[pl.ds(gs_, NA_GROUP * GRID_W), :] = out.astype(o_ref.dtype)
            return carry

        lax.fori_loop(0, rows // NA_GROUP, group, 0)
        qc = q_ref[s_len:t_len, :]
        s = _dot(qc, kc, 1, 1) * scale
        p = jnp.exp(s - jnp.max(s, axis=-1, keepdims=True))
        o = _dot(p.astype(BF16), vc, 1, 0) / jnp.sum(p, axis=-1, keepdims=True)
        o_ref[s_len:t_len, :] = o.astype(o_ref.dtype)

    col = lambda off: pl.BlockSpec((t_len, hd), functools.partial(lambda h, off: (0, off + h), off=off))
    return pl.pallas_call(
        body, name=name, grid=(heads,),
        in_specs=[col(0), col(heads), col(2 * heads),
                  pl.BlockSpec((None, NA_WIN_ROWS, GRID_W, nloc), lambda h: (h, 0, 0, 0))],
        out_specs=pl.BlockSpec((t_len, hd), lambda h: (0, h)),
        out_shape=jax.ShapeDtypeStruct((t_len, heads * hd), BF16),
        compiler_params=_params(("parallel",)),
    )(u, u, u, bias)


def _na_bwd(u, bias, o, do, *, s_len, heads, name):
    t_len = u.shape[0]
    rows = s_len // GRID_W
    nloc = NA_WIN_ROWS * GRID_W
    scale = NA_HEAD_DIM ** -0.5
    hd = NA_HEAD_DIM

    def body(q_ref, k_ref, v_ref, b_ref, o_ref, do_ref, dq_ref, dk_ref, dv_ref, db_ref, dk_acc, dv_acc):
        kc = k_ref[s_len:t_len, :]
        vc = v_ref[s_len:t_len, :]
        dk_acc[...] = jnp.zeros_like(dk_acc)
        dv_acc[...] = jnp.zeros_like(dv_acc)
        db_ref[...] = jnp.zeros_like(db_ref)

        def group(g, carry):
            n_g, rw = NA_GROUP, GRID_W
            rs = [g * n_g + i for i in range(n_g)]
            r0s = [jnp.clip(r - NA_WIN_ROWS // 2, 0, rows - NA_WIN_ROWS) for r in rs]
            dls = [r - r0 for r, r0 in zip(rs, r0s)]
            gs_ = pl.ds(pl.multiple_of(g * (n_g * rw), n_g * rw), n_g * rw)
            kss = [pl.ds(pl.multiple_of(r0 * rw, rw), nloc) for r0 in r0s]
            row_of = lambda a, i: a[i * rw:(i + 1) * rw]
            q_all, do_all = q_ref[gs_, :], do_ref[gs_, :]
            dlt_all = jnp.sum(do_all.astype(F32) * o_ref[gs_, :].astype(F32), axis=-1, keepdims=True)
            s_ctx = _dot(q_all, kc, 1, 1) * scale
            dp_ctx = _dot(do_all, vc, 1, 1)
            s_loc = [_dot(row_of(q_all, i), k_ref[kss[i], :], 1, 1) * scale + b_ref[dls[i]] for i in range(n_g)]
            dp_loc = [_dot(row_of(do_all, i), v_ref[kss[i], :], 1, 1) for i in range(n_g)]
            p_loc_b, ds_loc_b, p_ctx_b, ds_ctx_b = [], [], [], []
            for i in range(n_g):
                sc, dlt = row_of(s_ctx, i), row_of(dlt_all, i)
                m = jnp.maximum(jnp.max(s_loc[i], axis=-1, keepdims=True), jnp.max(sc, axis=-1, keepdims=True))
                pl_, pc_ = jnp.exp(s_loc[i] - m), jnp.exp(sc - m)
                inv = 1.0 / (jnp.sum(pl_, axis=-1, keepdims=True) + jnp.sum(pc_, axis=-1, keepdims=True))
                pl_, pc_ = pl_ * inv, pc_ * inv
                ds_l = pl_ * (dp_loc[i] - dlt)
                db_ref[dls[i]] += ds_l
                p_loc_b.append(pl_.astype(BF16))
                ds_loc_b.append(ds_l.astype(BF16))
                p_ctx_b.append(pc_.astype(BF16))
                ds_ctx_b.append((pc_ * (row_of(dp_ctx, i) - dlt)).astype(BF16))
            p_ctx_all, ds_ctx_all = jnp.concatenate(p_ctx_b, axis=0), jnp.concatenate(ds_ctx_b, axis=0)
            dq_ctx = _dot(ds_ctx_all, kc, 1, 0)
            dq_loc = [_dot(ds_loc_b[i], k_ref[kss[i], :], 1, 0) for i in range(n_g)]
            dk_loc = [_dot(ds_loc_b[i], row_of(q_all, i), 0, 0) for i in range(n_g)]
            dv_loc = [_dot(p_loc_b[i], row_of(do_all, i), 0, 0) for i in range(n_g)]
            dk_ctx = _dot(ds_ctx_all, q_all, 0, 0)
            dv_ctx = _dot(p_ctx_all, do_all, 0, 0)
            dq_ref[gs_, :] = ((jnp.concatenate(dq_loc, axis=0) + dq_ctx) * scale).astype(dq_ref.dtype)
            for i in range(n_g):
                dk_acc[kss[i], :] += dk_loc[i] * scale
                dv_acc[kss[i], :] += dv_loc[i]
            dk_acc[s_len:t_len, :] += dk_ctx * scale
            dv_acc[s_len:t_len, :] += dv_ctx
            return carry

        lax.fori_loop(0, rows // NA_GROUP, group, 0)
        qc = q_ref[s_len:t_len, :]
        dout = do_ref[s_len:t_len, :]
        out = o_ref[s_len:t_len, :]
        s = _dot(qc, kc, 1, 1) * scale
        p = jnp.exp(s - jnp.max(s, axis=-1, keepdims=True))
        p = p / jnp.sum(p, axis=-1, keepdims=True)
        dlt = jnp.sum(dout.astype(F32) * out.astype(F32), axis=-1, keepdims=True)
        ds = (p * (_dot(dout, vc, 1, 1) - dlt)).astype(BF16)
        dq_ref[s_len:t_len, :] = (_dot(ds, kc, 1, 0) * scale).astype(dq_ref.dtype)
        dk_acc[s_len:t_len, :] += _dot(ds, qc, 0, 0) * scale
        dv_acc[s_len:t_len, :] += _dot(p.astype(BF16), dout, 0, 0)
        dk_ref[...] = dk_acc[...].astype(dk_ref.dtype)
        dv_ref[...] = dv_acc[...].astype(dv_ref.dtype)

    col = lambda off: pl.BlockSpec((t_len, hd), functools.partial(lambda h, off: (0, off + h), off=off))
    tbl = pl.BlockSpec((None, NA_WIN_ROWS, GRID_W, nloc), lambda h: (h, 0, 0, 0))
    tok = jax.ShapeDtypeStruct((t_len, heads * hd), BF16)
    return pl.pallas_call(
        body, name=name, grid=(heads,),
        in_specs=[col(0), col(heads), col(2 * heads), tbl, col(0), col(0)],
        out_specs=[col(0), col(0), col(0), tbl],
        out_shape=[tok, tok, tok, jax.ShapeDtypeStruct(bias.shape, F32)],
        scratch_shapes=[pltpu.VMEM((t_len, hd), F32), pltpu.VMEM((t_len, hd), F32)],
        compiler_params=_params(("parallel",)),
    )(u, u, u, bias, o, do)


def _split3(x):
    hi = x.astype(BF16)
    r1 = x - hi.astype(F32)
    mid = r1.astype(BF16)
    lo = (r1 - mid.astype(F32)).astype(BF16)
    return hi, mid, lo


def _rpb_grad(dbias, *, name):
    heads = dbias.shape[0]
    kh = NA_WIN_ROWS
    e1, e2 = _na_onehots()
    x = dbias.reshape(heads, kh, GRID_W, kh, GRID_W).transpose(0, 1, 3, 2, 4).reshape(heads, kh * kh, GRID_W * GRID_W)

    def body(x_ref, e1_ref, e2_ref, o_ref):
        e2b = e2_ref[...].astype(BF16)
        y = sum(_dot(part, e2b, 1, 0) for part in _split3(x_ref[...]))
        e1b = e1_ref[...].astype(BF16)
        o_ref[...] = sum(_dot(e1b, part, 1, 0) for part in _split3(y))

    out = pl.pallas_call(
        body, name=name, grid=(heads,),
        in_specs=[pl.BlockSpec((None, kh * kh, GRID_W * GRID_W), lambda h: (h, 0, 0)),
                  pl.BlockSpec(e1.shape, lambda h: (0, 0)), pl.BlockSpec(e2.shape, lambda h: (0, 0))],
        out_specs=pl.BlockSpec((None, 16, LANES), lambda h: (h, 0, 0)),
        out_shape=jax.ShapeDtypeStruct((heads, 16, LANES), F32),
        compiler_params=_params(("parallel",)),
    )(x, e1, e2)
    return out[:, :2 * kh - 1, :2 * NA_WIN_COLS - 1]


def _rope_tables(s_len, l_len):
    nf = RET_KEY_DIM // 4
    t = np.arange(s_len)
    row = (t // GRID_W).astype(np.float32)
    colp = (t % GRID_W).astype(np.float32)
    inv_freq = jnp.asarray(ROPE_BASE, F32) ** (-jnp.arange(nf, dtype=F32) / nf)
    ang = jnp.concatenate([jnp.asarray(row)[:, None] * inv_freq, jnp.asarray(colp)[:, None] * inv_freq], axis=-1)
    cos, sin = jnp.cos(ang), jnp.sin(ang)
    c2 = jnp.concatenate([cos, cos], axis=-1)
    s2 = jnp.concatenate([-sin, sin], axis=-1)
    c2 = jnp.concatenate([c2, jnp.ones((l_len, RET_KEY_DIM), F32)], axis=0)
    s2 = jnp.concatenate([s2, jnp.zeros((l_len, RET_KEY_DIM), F32)], axis=0)
    return c2, s2


def _rope(x, c2, s2):
    return x * c2 + pltpu.roll(x, RET_KEY_DIM // 2, 1) * s2


def _rope_t(d, c2, s2):
    return d * c2 + pltpu.roll(d * s2, RET_KEY_DIM // 2, 1)


def _ret_decays(lg, direction):
    cs = RET_CHUNK
    i_col = lax.broadcasted_iota(jnp.int32, (cs, 1), 0)
    p_col = jnp.where(direction == 0, i_col, cs - 1 - i_col).astype(F32)
    pi = lax.broadcasted_iota(jnp.int32, (cs, cs), 0)
    pj = lax.broadcasted_iota(jnp.int32, (cs, cs), 1)
    diff = jnp.where(direction == 0, pi - pj, pj - pi).astype(F32)
    dm = jnp.where(diff >= 0, jnp.exp(jnp.maximum(diff, 0.0) * lg), 0.0)
    qdec = jnp.exp((p_col + 1.0) * lg)
    kdec = jnp.exp((cs - 1.0 - p_col) * lg)
    cd = jnp.exp(jnp.full((1, 1), cs, F32) * lg)
    return p_col, dm, qdec, kdec, cd


def _ret_chunk_index(t, direction, n_chunks, lat_chunks):
    return jnp.where(direction == 0, lax.rem(t + lat_chunks, n_chunks), n_chunks - 1 - t)


def _ret_fwd(u, c2, s2, lg, *, s_len, heads, q_off, name):
    t_len = u.shape[0]
    cs, dk, dv = RET_CHUNK, RET_KEY_DIM, RET_VAL_DIM
    n_chunks, lat_chunks = t_len // cs, s_len // cs
    k_scale = dk ** -0.5
    qb, kb, vb = q_off // dk, q_off // dk + heads, (q_off + 2 * heads * dk) // dv

    def body(lg_ref, q_ref, k_ref, v_ref, c_ref, s_ref, o_ref, st_ref, qd_s, kv_s):
        h, d = pl.program_id(0), pl.program_id(1)
        _, dm, qdec, kdec, cd = _ret_decays(lg_ref[d, h], d)
        n_g = max(g for g in RET_GROUPS if n_chunks % g == 0)
        rows_of = lambda c: pl.ds(pl.multiple_of(c * cs, cs), cs)

        def local(gi, carry):
            rws = [rows_of(gi * n_g + j) for j in range(n_g)]
            qcs = [_rope(q_ref[r, :].astype(F32), c_ref[r, :], s_ref[r, :]) for r in rws]
            kcs = [_rope(k_ref[r, :].astype(F32), c_ref[r, :], s_ref[r, :]) * k_scale for r in rws]
            vcs = [v_ref[r, :] for r in rws]
            a_raw = [_dot(qcs[j].astype(BF16), kcs[j].astype(BF16), 1, 1) for j in range(n_g)]
            kv = [_dot((kcs[j] * kdec).astype(BF16), vcs[j], 0, 0) for j in range(n_g)]
            inner = [_dot((a_raw[j] * dm).astype(BF16), vcs[j], 1, 0) for j in range(n_g)]
            for j in range(n_g):
                qd_s[rws[j], :] = (qcs[j] * qdec).astype(BF16)
                kv_s[gi * n_g + j] = kv[j]

            @pl.when(d == 0)
            def _():
                for j in range(n_g):
                    o_ref[rws[j], :] = inner[j]

            @pl.when(d == 1)
            def _():
                for j in range(n_g):
                    o_ref[rws[j], :] += inner[j]

            return carry

        lax.fori_loop(0, n_chunks // n_g, local, 0)

        def scan(t, st):
            st_ref[t] = st
            return st * cd + kv_s[_ret_chunk_index(t, d, n_chunks, lat_chunks)]

        lax.fori_loop(0, n_chunks, scan, jnp.zeros((dk, dv), F32))

        def cross(gi, carry):
            ts = [gi * n_g + j for j in range(n_g)]
            rws = [rows_of(_ret_chunk_index(t, d, n_chunks, lat_chunks)) for t in ts]
            outs = [_dot(qd_s[rws[j], :], st_ref[ts[j]].astype(BF16), 1, 0) for j in range(n_g)]
            for j in range(n_g):
                o_ref[rws[j], :] += outs[j]
            return carry

        lax.fori_loop(0, n_chunks // n_g, cross, 0)

    return pl.pallas_call(
        body, name=name, grid=(heads, 2),
        in_specs=[pl.BlockSpec(memory_space=pltpu.SMEM),
                  pl.BlockSpec((t_len, dk), lambda h, d: (0, qb + h)),
                  pl.BlockSpec((t_len, dk), lambda h, d: (0, kb + h)),
                  pl.BlockSpec((t_len, dv), lambda h, d: (0, vb + h)),
                  pl.BlockSpec((t_len, dk), lambda h, d: (0, 0)),
                  pl.BlockSpec((t_len, dk), lambda h, d: (0, 0))],
        out_specs=[pl.BlockSpec((t_len, dv), lambda h, d: (0, h)),
                   pl.BlockSpec((None, None, n_chunks, dk, dv), lambda h, d: (h, d, 0, 0, 0))],
        out_shape=[jax.ShapeDtypeStruct((t_len, heads * dv), F32),
                   jax.ShapeDtypeStruct((heads, 2, n_chunks, dk, dv), F32)],
        scratch_shapes=[pltpu.VMEM((t_len, dk), BF16), pltpu.VMEM((n_chunks, dk, dv), F32)],
        compiler_params=_params(("parallel", "arbitrary")),
    )(lg, u, u, u, c2, s2)


def _ret_bwd(u, c2, s2, lg, states, do, *, s_len, heads, q_off, name):
    t_len = u.shape[0]
    cs, dk, dv = RET_CHUNK, RET_KEY_DIM, RET_VAL_DIM
    n_chunks, lat_chunks = t_len // cs, s_len // cs
    k_scale = dk ** -0.5
    qb, kb, vb = q_off // dk, q_off // dk + heads, (q_off + 2 * heads * dk) // dv

    def body(lg_ref, q_ref, k_ref, v_ref, c_ref, s_ref, st_ref, do_ref, dq_ref, dk_ref, dv_ref, dlg_ref, acc, qdo_s, dst_s):
        h, d = pl.program_id(0), pl.program_id(1)
        p_col, dm, qdec, kdec, cd = _ret_decays(lg_ref[d, h], d)
        acc[...] = jnp.zeros_like(acc)
        n_g = max(g for g in RET_GROUPS[:2] if n_chunks % g == 0)
        rows_of = lambda c: pl.ds(pl.multiple_of(c * cs, cs), cs)
        chunk_of = lambda t: _ret_chunk_index(t, d, n_chunks, lat_chunks)

        def local(gi, carry):
            rws = [rows_of(gi * n_g + j) for j in range(n_g)]
            qds = [(_rope(q_ref[r, :].astype(F32), c_ref[r, :], s_ref[r, :]) * qdec).astype(BF16) for r in rws]
            prods = [_dot(qds[j], do_ref[rws[j], :].astype(BF16), 0, 0) for j in range(n_g)]
            for j in range(n_g):
                qdo_s[gi * n_g + j] = prods[j]
            return carry

        lax.fori_loop(0, n_chunks // n_g, local, 0)

        def scan(i, dst):
            t = n_chunks - 1 - i
            dst_s[t] = dst
            return dst * cd + qdo_s[chunk_of(t)]

        lax.fori_loop(0, n_chunks, scan, jnp.zeros((dk, dv), F32))

        def grads(gi, carry):
            ts = [gi * n_g + j for j in range(n_g)]
            rws = [rows_of(chunk_of(t)) for t in ts]
            ccs, sss = [c_ref[r, :] for r in rws], [s_ref[r, :] for r in rws]
            qcs = [_rope(q_ref[r, :].astype(F32), cc, ss) for r, cc, ss in zip(rws, ccs, sss)]
            kcs = [_rope(k_ref[r, :].astype(F32), cc, ss) * k_scale for r, cc, ss in zip(rws, ccs, sss)]
            vcs = [v_ref[r, :] for r in rws]
            docs = [do_ref[r, :].astype(BF16) for r in rws]
            sts = [st_ref[t] for t in ts]
            dsts = [dst_s[t] for t in ts]
            q16 = [x.astype(BF16) for x in qcs]
            k16 = [x.astype(BF16) for x in kcs]
            dst16 = [x.astype(BF16) for x in dsts]
            rng = range(n_g)
            a_raw = [_dot(q16[j], k16[j], 1, 1) for j in rng]
            da_raw = [_dot(docs[j], vcs[j], 1, 1) for j in rng]
            dq_c = [_dot(docs[j], sts[j].astype(BF16), 1, 1) * qdec for j in rng]
            dv_s = [_dot((kcs[j] * kdec).astype(BF16), dst16[j], 1, 0) for j in rng]
            dk_s = [_dot(vcs[j], dst16[j], 1, 1) * kdec for j in rng]
            a16 = [(a_raw[j] * dm).astype(BF16) for j in rng]
            dam = [(da_raw[j] * dm).astype(BF16) for j in rng]
            dq_i = [_dot(dam[j], k16[j], 1, 0) for j in rng]
            dk_i = [_dot(dam[j], q16[j], 0, 0) for j in rng]
            dv_i = [_dot(a16[j], docs[j], 0, 0) for j in rng]
            for j in rng:
                g = (jnp.sum(qcs[j] * (p_col * dq_i[j] + (p_col + 1.0) * dq_c[j]), axis=-1, keepdims=True)
                     + jnp.sum(kcs[j] * ((cs - 1.0 - p_col) * dk_s[j] - p_col * dk_i[j]), axis=-1, keepdims=True))
                g = (jnp.sum(g, axis=0, keepdims=True)
                     + cs * cd * jnp.sum(jnp.sum(dsts[j] * sts[j], axis=-1, keepdims=True), axis=0, keepdims=True))
                acc[...] += jnp.broadcast_to(g, acc.shape)
            dqs = [_rope_t(dq_i[j] + dq_c[j], ccs[j], sss[j]) for j in rng]
            dks = [_rope_t((dk_i[j] + dk_s[j]) * k_scale, ccs[j], sss[j]) for j in rng]
            dvs = [dv_i[j] + dv_s[j] for j in rng]

            @pl.when(d == 0)
            def _():
                for j in rng:
                    dq_ref[rws[j], :] = dqs[j].astype(dq_ref.dtype)
                    dk_ref[rws[j], :] = dks[j].astype(dk_ref.dtype)
                    dv_ref[rws[j], :] = dvs[j].astype(dv_ref.dtype)

            @pl.when(d == 1)
            def _():
                for j in rng:
                    dq_ref[rws[j], :] = (dq_ref[rws[j], :].astype(F32) + dqs[j]).astype(dq_ref.dtype)
                    dk_ref[rws[j], :] = (dk_ref[rws[j], :].astype(F32) + dks[j]).astype(dk_ref.dtype)
                    dv_ref[rws[j], :] = (dv_ref[rws[j], :].astype(F32) + dvs[j]).astype(dv_ref.dtype)

            return carry

        lax.fori_loop(0, n_chunks // n_g, grads, 0)
        dlg_ref[...] = acc[...]

    return pl.pallas_call(
        body, name=name, grid=(heads, 2),
        in_specs=[pl.BlockSpec(memory_space=pltpu.SMEM),
                  pl.BlockSpec((t_len, dk), lambda h, d: (0, qb + h)),
                  pl.BlockSpec((t_len, dk), lambda h, d: (0, kb + h)),
                  pl.BlockSpec((t_len, dv), lambda h, d: (0, vb + h)),
                  pl.BlockSpec((t_len, dk), lambda h, d: (0, 0)),
                  pl.BlockSpec((t_len, dk), lambda h, d: (0, 0)),
                  pl.BlockSpec((None, None, n_chunks, dk, dv), lambda h, d: (h, d, 0, 0, 0)),
                  pl.BlockSpec((t_len, dv), lambda h, d: (0, h))],
        out_specs=[pl.BlockSpec((t_len, dk), lambda h, d: (0, h)),
                   pl.BlockSpec((t_len, dk), lambda h, d: (0, h)),
                   pl.BlockSpec((t_len, dv), lambda h, d: (0, h)),
                   pl.BlockSpec((None, None, 8, LANES), lambda h, d: (h, d, 0, 0))],
        out_shape=[jax.ShapeDtypeStruct((t_len, heads * dk), BF16),
                   jax.ShapeDtypeStruct((t_len, heads * dk), BF16),
                   jax.ShapeDtypeStruct((t_len, heads * dv), BF16),
                   jax.ShapeDtypeStruct((heads, 2, 8, LANES), F32)],
        scratch_shapes=[pltpu.VMEM((8, LANES), F32), pltpu.VMEM((n_chunks, dk, dv), F32), pltpu.VMEM((n_chunks, dk, dv), F32)],
        compiler_params=_params(("parallel", "arbitrary")),
    )(lg, u, u, u, c2, s2, states, do)


def _mesh_pos():
    return lax.axis_index("x"), lax.axis_index("y"), lax.axis_index("c")


def _all_gather_small(buf, *, name):
    r = buf.shape[0]

    def body(x_ref, o_ref, send_sems, recv_sems, local_sem):
        x, y, c = _mesh_pos()
        me = 4 * x + 2 * y + c
        mine = pltpu.make_async_copy(x_ref, o_ref.at[me], local_sem)
        mine.start()
        copies = []
        for k in range(1, N_DEV):
            px, py, pc = x ^ ((k >> 2) & 1), y ^ ((k >> 1) & 1), c ^ (k & 1)
            cp = pltpu.make_async_remote_copy(
                src_ref=x_ref, dst_ref=o_ref.at[me], send_sem=send_sems.at[k - 1], recv_sem=recv_sems.at[k - 1],
                device_id=(px, py, pc), device_id_type=MESH)
            cp.start()
            copies.append((cp, 4 * px + 2 * py + pc))
        for k, (cp, peer) in enumerate(copies):
            pltpu.make_async_remote_copy(
                src_ref=x_ref, dst_ref=o_ref.at[peer], send_sem=send_sems.at[k], recv_sem=recv_sems.at[k],
                device_id=(x, y, c), device_id_type=MESH).wait_recv()
        for cp, _ in copies:
            cp.wait_send()
        mine.wait()

    return pl.pallas_call(
        body, name=name,
        in_specs=[pl.BlockSpec(memory_space=pltpu.VMEM)],
        out_specs=pl.BlockSpec(memory_space=pltpu.VMEM),
        out_shape=jax.ShapeDtypeStruct((N_DEV, r, LANES), F32),
        scratch_shapes=[pltpu.SemaphoreType.DMA((N_DEV - 1,)), pltpu.SemaphoreType.DMA((N_DEV - 1,)),
                        pltpu.SemaphoreType.DMA],
        compiler_params=pltpu.CompilerParams(vmem_limit_bytes=VMEM_LIMIT),
    )(buf)


def _cut(ref, shard_axis, *, chip=None, half=None, lead=None):
    shape = ref.shape[1:] if lead is not None else ref.shape
    idx = [slice(None), slice(None)]
    if chip is not None:
        w = shape[shard_axis] // N_CHIPS
        idx[shard_axis] = pl.ds(pl.multiple_of(chip * w, w), w)
    if half is not None:
        hw = shape[1 - shard_axis] // 2
        idx[1 - shard_axis] = pl.ds(pl.multiple_of(half * hw, hw), hw)
    if lead is not None:
        idx = [lead] + idx
    return ref.at[tuple(idx)]


def _wait_recv(ref, send_sem, recv_sem):
    pltpu.make_async_remote_copy(src_ref=ref, dst_ref=ref, send_sem=send_sem, recv_sem=recv_sem,
                                 device_id=_mesh_pos(), device_id_type=MESH).wait_recv()


def _gather_plan(axes):
    def plan(srcs, lands, send_sems, recv_sems):
        x, y, c = _mesh_pos()
        chip = 2 * x + y
        copies = []
        for i, ax in enumerate(axes):
            for k in range(1, N_CHIPS):
                px, py = x ^ (k >> 1), y ^ (k & 1)
                mine = _cut(lands[i], ax, chip=chip, half=c)
                j = i * (N_CHIPS - 1) + k - 1
                sems = dict(send_sem=send_sems.at[j], recv_sem=recv_sems.at[j], device_id=(px, py, c), device_id_type=MESH)
                send = pltpu.make_async_remote_copy(src_ref=mine, dst_ref=mine, **sems)
                recv = pltpu.make_async_remote_copy(src_ref=mine, dst_ref=_cut(lands[i], ax, chip=2 * px + py, half=c), **sems)
                copies.append((send, recv))
        return copies
    return plan


def _gather_near_plan(axes):
    def plan(srcs, lands, send_sems, recv_sems):
        x, y, c = _mesh_pos()
        copies = []
        for i, ax in enumerate(axes):
            mine = _cut(lands[i], ax, chip=2 * x + y, half=c)
            for k, (px, py) in enumerate(((1 - x, y), (x, 1 - y))):
                sems = dict(send_sem=send_sems.at[2 * i + k], recv_sem=recv_sems.at[2 * i + k], device_id=(px, py, c), device_id_type=MESH)
                send = pltpu.make_async_remote_copy(src_ref=mine, dst_ref=mine, **sems)
                recv = pltpu.make_async_remote_copy(src_ref=mine, dst_ref=_cut(lands[i], ax, chip=2 * px + py, half=c), **sems)
                copies.append((send, recv))
        return copies
    return plan


def _gather_far_plan(axes):
    def plan(srcs, lands, send_sems, recv_sems):
        x, y, c = _mesh_pos()
        from_chip = 2 * (x ^ (1 - c)) + (y ^ c)
        to = (x ^ c, y ^ (1 - c), c)
        diag = 2 * (1 - x) + (1 - y)
        copies = []
        for i, ax in enumerate(axes):
            passed = _cut(lands[i], ax, chip=from_chip, half=c)
            sems = dict(send_sem=send_sems.at[i], recv_sem=recv_sems.at[i], device_id=to, device_id_type=MESH)
            send = pltpu.make_async_remote_copy(src_ref=passed, dst_ref=passed, **sems)
            recv = pltpu.make_async_remote_copy(src_ref=passed, dst_ref=_cut(lands[i], ax, chip=diag, half=c), **sems)
            copies.append((send, recv))
        return copies
    return plan


def _pair_plan(axes):
    def plan(srcs, lands, send_sems, recv_sems):
        x, y, c = _mesh_pos()
        copies = []
        for i, ax in enumerate(axes):
            cp = pltpu.make_async_remote_copy(
                src_ref=_cut(srcs[i], ax, half=1 - c), dst_ref=lands[i], send_sem=send_sems.at[i], recv_sem=recv_sems.at[i],
                device_id=(x, y, 1 - c), device_id_type=MESH)
            copies.append((cp, cp))
        return copies
    return plan


def _scatter_plan(axes):
    def plan(srcs, lands, send_sems, recv_sems):
        x, y, c = _mesh_pos()
        copies = []
        for i, ax in enumerate(axes):
            for k in range(1, N_CHIPS):
                px, py = x ^ (k >> 1), y ^ (k & 1)
                j = i * (N_CHIPS - 1) + k - 1
                cp = pltpu.make_async_remote_copy(
                    src_ref=_cut(srcs[i], ax, chip=2 * px + py), dst_ref=lands[i].at[k - 1],
                    send_sem=send_sems.at[j], recv_sem=recv_sems.at[j], device_id=(px, py, c), device_id_type=MESH)
                copies.append((cp, cp))
        return copies
    return plan


HBM = pl.BlockSpec(memory_space=pltpu.HBM)
SEM = pl.BlockSpec(memory_space=pltpu.SEMAPHORE)
EFFECT = pltpu.SideEffectType.DATAFLOW_SIDE_EFFECTING


def _in_hbm(arrays):
    return [pltpu.with_memory_space_constraint(a, pltpu.HBM) for a in arrays]


def _split_start(srcs, lands, plan, n_copies, *, name):
    bufs = list(srcs) + list(lands)
    ns, nb = len(srcs), len(bufs)

    def body(*refs):
        send_sems, recv_sems, token = refs[nb], refs[nb + 1], refs[-1]
        for send, _ in plan(refs[:ns], refs[ns:nb], send_sems, recv_sems):
            send.start()
        token[...] = jnp.zeros_like(token)

    sems = pltpu.SemaphoreType.DMA((n_copies,))
    res = pl.pallas_call(
        body, name=name, in_specs=[HBM] * nb,
        out_specs=[SEM, SEM] + [HBM] * nb + [pl.BlockSpec(memory_space=pltpu.VMEM)],
        out_shape=[sems, sems] + [pltpu.HBM(a.shape, a.dtype) for a in bufs] + [jax.ShapeDtypeStruct((8, LANES), F32)],
        input_output_aliases={j: 2 + j for j in range(nb)},
        compiler_params=pltpu.CompilerParams(has_side_effects=EFFECT),
    )(*_in_hbm(bufs))
    return res[0], res[1], res[2:2 + ns], res[2 + ns:2 + nb], res[-1]


def _split_wait(started, after, plan, *, name, with_srcs=False):
    send_sems, recv_sems, srcs, lands, _ = started
    bufs = list(srcs) + list(lands)
    ns, nb = len(srcs), len(bufs)

    def body(*refs):
        for send, recv in plan(refs[:ns], refs[ns:nb], refs[nb], refs[nb + 1]):
            send.wait_send()
            recv.wait_recv()

    res = pl.pallas_call(
        body, name=name, in_specs=[HBM] * nb + [SEM, SEM, ANY], out_specs=[HBM] * nb,
        out_shape=[pltpu.HBM(a.shape, a.dtype) for a in bufs],
        input_output_aliases={j: j for j in range(nb)},
        compiler_params=pltpu.CompilerParams(has_side_effects=EFFECT),
    )(*bufs, send_sems, recv_sems, after)
    return (res[:ns], res[ns:]) if with_srcs else res[ns:]


def _cast_into_full(w3, layer, ax, chip, *, after=None, name):
    _, r, wd = w3.shape
    tr = _rows_per_tile(r, wd, 4 << 20)
    nt = r // tr
    full_shape = (r, wd * N_CHIPS) if ax == 1 else (r * N_CHIPS, wd)
    out_map = (lambda i, ch: (i, ch[0])) if ax == 1 else (lambda i, ch: (ch[0] * nt + i, 0))
    zero = jnp.zeros((1, wd), F32) + (0.0 if after is None else after)

    def body(chip_ref, w_ref, z_ref, o_ref):
        o_ref[...] = (w_ref[...] + z_ref[...]).astype(o_ref.dtype)

    return pl.pallas_call(
        body, name=name,
        grid_spec=pltpu.PrefetchScalarGridSpec(
            num_scalar_prefetch=1, grid=(nt,),
            in_specs=[pl.BlockSpec((None, tr, wd), lambda i, ch: (layer, i, 0)), pl.BlockSpec((1, wd), lambda i, ch: (0, 0))],
            out_specs=pl.BlockSpec((tr, wd), out_map)),
        out_shape=jax.ShapeDtypeStruct(full_shape, BF16),
        compiler_params=_params(("parallel",)),
    )(jnp.reshape(chip, (1,)).astype(jnp.int32), w3, zero)


def _forward_halves(fulls, axes, *, name):
    n = len(fulls)

    def body(*refs):
        bufs = refs[:n]
        send_sems, recv_sems = refs[2 * n:]
        x, y, c = _mesh_pos()
        sends = []
        for i in range(n):
            for k in range(1, N_CHIPS):
                landed = _cut(bufs[i], axes[i], chip=2 * (x ^ (k >> 1)) + (y ^ (k & 1)), half=c)
                cp = pltpu.make_async_remote_copy(
                    src_ref=landed, dst_ref=landed, send_sem=send_sems.at[i, k - 1], recv_sem=recv_sems.at[i, k - 1],
                    device_id=(x, y, 1 - c), device_id_type=MESH)
                cp.start()
                sends.append(cp)
        for i in range(n):
            for k in range(1, N_CHIPS):
                other = _cut(bufs[i], axes[i], chip=2 * (x ^ (k >> 1)) + (y ^ (k & 1)), half=1 - c)
                _wait_recv(other, send_sems.at[i, k - 1], recv_sems.at[i, k - 1])
        for cp in sends:
            cp.wait_send()

    pairs = pltpu.SemaphoreType.DMA((n, N_CHIPS - 1))
    return pl.pallas_call(
        body, name=name, in_specs=[ANY] * n, out_specs=[ANY] * n,
        out_shape=[jax.ShapeDtypeStruct(a.shape, a.dtype) for a in fulls],
        input_output_aliases={j: j for j in range(n)},
        scratch_shapes=[pairs, pairs],
    )(*fulls)


def _share_halves_in_place(bufs, axes, *, name):
    n = len(bufs)

    def body(*refs):
        ins = refs[:n]
        send_sems, recv_sems = refs[2 * n:]
        x, y, c = _mesh_pos()
        sends = []
        for i in range(n):
            mine = _cut(ins[i], axes[i], half=c)
            cp = pltpu.make_async_remote_copy(
                src_ref=mine, dst_ref=mine, send_sem=send_sems.at[i], recv_sem=recv_sems.at[i],
                device_id=(x, y, 1 - c), device_id_type=MESH)
            cp.start()
            sends.append(cp)
        for i in range(n):
            _wait_recv(_cut(ins[i], axes[i], half=1 - c), send_sems.at[i], recv_sems.at[i])
        for cp in sends:
            cp.wait_send()

    sems = pltpu.SemaphoreType.DMA((n,))
    return pl.pallas_call(
        body, name=name, in_specs=[ANY] * n, out_specs=[ANY] * n,
        out_shape=[jax.ShapeDtypeStruct(b.shape, b.dtype) for b in bufs],
        input_output_aliases={j: j for j in range(n)}, scratch_shapes=[sems, sems],
    )(*bufs)


def _adamw_math(w, g, m, v):
    m = ADAM_B1 * m + (1.0 - ADAM_B1) * g
    v = ADAM_B2 * v + (1.0 - ADAM_B2) * (g * g)
    m_hat = m / (1.0 - ADAM_B1 ** ADAM_STEP)
    v_hat = v / (1.0 - ADAM_B2 ** ADAM_STEP)
    delta = -ADAM_LR * (m_hat / (jnp.sqrt(v_hat) + ADAM_EPS) + ADAM_WD * w)
    return delta, m, v


def _adamw_layer(w3, m3, v3, p, q, layer, prev, *, name):
    nl, rows, width = w3.shape
    tr = _rows_per_tile(rows, width)

    def fn(*t):
        if q is None:
            w, m, v, g = t
        else:
            w, m, v, g, g2 = t
            g = g + g2
        delta, m, v = _adamw_math(w, g, m, v)
        return g, delta, m, v

    ins = [('t', w3, 0, width, layer), ('t', m3, 0, width, layer), ('t', v3, 0, width, layer), ('t', p, 0, width)]
    if q is not None:
        ins.append(('t', q, 0, width))
    outs = [('t', width, F32, layer, nl)] * 4
    aliases = None if prev is None else [(prev[i], i) for i in range(4)]
    return _ew(fn, ins, outs, rows=rows, tr=tr, name=name, aliases=aliases)


def _pack_rows(vec):
    n = vec.shape[0]
    r = -(-n // (8 * LANES)) * 8
    return jnp.pad(vec, (0, r * LANES - n)).reshape(r, LANES)


def kernel(x, c, ctx, c_ctx, ada_w, ada_b, norm_g, w_in, na_rpb, ret_decay_logit, w_proj_na, w_proj_ret, w_out, final_g, loss_target, m_c_ctx, m_ada_w, m_ada_b, m_norm_g, m_w_in, m_na_rpb, m_ret_decay_logit, m_w_proj_na, m_w_proj_ret, m_w_out, m_final_g, v_c_ctx, v_ada_w, v_ada_b, v_norm_g, v_w_in, v_na_rpb, v_ret_decay_logit, v_w_proj_na, v_w_proj_ret, v_w_out, v_final_g):
    depth = w_in.shape[0]
    s_len, d_model = x.shape[1], x.shape[2]
    l_len = ctx.shape[1]
    t_len = s_len + l_len
    na_heads = na_rpb.shape[1]
    ret_heads = ret_decay_logit.shape[2]
    w_na = na_heads * NA_HEAD_DIM
    w_qk = ret_heads * RET_KEY_DIM
    w_v = ret_heads * RET_VAL_DIM
    in_cols = w_in.shape[2] * N_CHIPS
    assert in_cols == 4 * w_na + 2 * w_qk + 2 * w_v + 2 * d_model
    assert x.shape[0] == 1 and s_len % (NA_WIN_ROWS * GRID_W) == 0 and l_len % RET_CHUNK == 0
    off = np.cumsum([0, w_na, w_na, w_na, w_na, w_qk, w_qk, w_v, w_v, d_model, d_model])
    o_naz, o_retq, o_retz, o_gna, o_gret = int(off[3]), int(off[4]), int(off[7]), int(off[8]), int(off[9])
    rows = s_len // GRID_W
    tr = _tile(l_len, 256, 8)
    n0 = s_len // tr
    mod_cols = 3 * d_model
    mod_shard = ada_w.shape[2]

    xi, yi, ci = _mesh_pos()
    me = 4 * xi + 2 * yi + ci
    chip = 2 * xi + yi

    big_axes = [1, 1, 0, 0]
    n_big = len(big_axes) * (N_CHIPS - 1)
    gather_plan, scatter_plan = _gather_plan(big_axes), _scatter_plan(big_axes)

    c_silu = c[0] * _sigmoid(c[0])
    cc_silu = c_ctx * _sigmoid(c_ctx)
    c_all = _all_gather_small(_pack_rows(c_silu), name="gather_c")[:, :d_model // LANES].reshape(N_DEV, d_model)
    a_rows = jnp.concatenate([c_all, cc_silu[None], jnp.zeros((16 - N_DEV - 1, d_model), F32)], axis=0)
    mod_part = jnp.stack([_mm(a_rows, ada_w, b_lead=l, out_dtype=F32, name="ada_fwd_%d" % l) for l in range(depth)])
    mod_all = _all_gather_small(_pack_rows(mod_part.reshape(-1)), name="gather_mod")
    n_mod = depth * 16 * mod_shard
    mod_all = mod_all.reshape(N_DEV, -1)[:, :n_mod].reshape(N_CHIPS, 2, depth, 16, mod_shard)[:, 0]
    mod_all = jnp.transpose(mod_all, (1, 2, 0, 3)).reshape(depth, 16, mod_cols) + ada_b[:, None, :]

    big_named = list(zip((w_in, w_proj_na, w_proj_ret, w_out), big_axes, ("w_in", "w_proj_na", "w_proj_ret", "w_out")))
    w_in0 = _cast_into_full(w_in, 0, big_axes[0], chip, name="cast_w_in_0")
    mod_all, w_in0 = lax.optimization_barrier((mod_all, w_in0))
    plan_near, plan_far, plan_rest = _gather_near_plan(big_axes[:1]), _gather_far_plan(big_axes[:1]), _gather_plan(big_axes[1:])
    near_all, far_all = _gather_near_plan(big_axes), _gather_far_plan(big_axes)
    first_gather = _split_start([], [w_in0], plan_near, 2, name="gather_start_0_in")
    start_token = first_gather[4][0, 0]
    fulls = [[None if (l == 0 and tag == "w_in") else _cast_into_full(w, l, ax, chip, after=start_token, name="cast_%s_%d" % (tag, l))
              for w, ax, tag in big_named] for l in range(depth)]
    mod_lat = lax.dynamic_index_in_dim(mod_all, me, axis=1, keepdims=False)
    mod_ctx = mod_all[:, N_DEV]
    biases = [_na_bias_layout(_na_bias_table(na_rpb[l], s_len // GRID_W, name="na_bias_%d" % l)) for l in range(depth)]
    biases, fulls = lax.optimization_barrier((biases, fulls))
    landed_near = _split_wait(first_gather, biases[-1], plan_near, name="gather_wait_0_in")
    passing = _split_start([], landed_near, plan_far, 1, name="gather_pass_0_in")
    front_token = passing[4][0, 0]

    c2, s2 = _rope_tables(s_len, l_len)
    log_gamma = jax.nn.log_sigmoid(ret_decay_logit)
    x_all = jnp.concatenate([x[0], ctx[0]], axis=0)

    def grp(lat_vec, ctx_vec):
        return jnp.stack([lat_vec, ctx_vec])[:, None, :]

    saved, full_w = [], []
    for l in range(depth):
        shift, scale, gate = [grp(mod_lat[l, i * d_model:(i + 1) * d_model], mod_ctx[l, i * d_model:(i + 1) * d_model])
                              for i in range(3)]
        gs = norm_g[l][None, None, :] * (1.0 + scale) + front_token

        def modnorm(xt, gs_t, sh_t):
            r = lax.rsqrt(jnp.mean(xt * xt, axis=-1, keepdims=True) + NORM_EPS)
            return xt * r * gs_t + sh_t

        h, = _ew(modnorm, [('t', x_all, 0, d_model), ('g', gs), ('g', shift)], [('t', d_model, BF16)],
                 rows=t_len, tr=tr, n0=n0, name="modnorm_%d" % l)
        bias = biases[l]
        if l == 0:
            h, bias = lax.optimization_barrier((h, bias))
            landed_in = _split_wait(passing, h, plan_far, name="gather_wait_0_in_far")
            landed_in, rest0, later = lax.optimization_barrier((landed_in, fulls[0][1:], fulls[1:]))
            rest_gather = _split_start([], rest0, plan_rest, n_big - (N_CHIPS - 1), name="gather_start_0_rest")
            later_gathers = [_split_start([], later[j], near_all, 2 * len(big_axes), name="gather_start_%d" % (j + 1))
                             for j in range(depth - 1)]
            win_f, = _forward_halves(landed_in, big_axes[:1], name="gather_forward_0_in")
            win_f, tokens = lax.optimization_barrier((win_f, [rest_gather[4]] + [g[4] for g in later_gathers]))
            gate = gate + sum(t[0, 0] for t in tokens)
        else:
            h, bias = lax.optimization_barrier((h, bias))
            landed = _split_wait(later_passes[l - 1], h, far_all, name="gather_wait_%d" % l)
            win_f, wpn_f, wpr_f, wout_f = _forward_halves(landed, big_axes, name="gather_forward_%d" % l)
        u = _mm(h, win_f, tm=1152, tn=1024, name="in_proj_%d" % l)
        o_na = _na_fwd(u, bias, s_len=s_len, heads=na_heads, name="na_fwd_%d" % l)
        o_ret, states = _ret_fwd(u, c2, s2, log_gamma[l], s_len=s_len, heads=ret_heads, q_off=o_retq, name="ret_fwd_%d" % l)

        def act(o1, z1, o2, z2):
            a1 = o1.astype(F32) * _silu_parts(z1.astype(F32))[0]
            sz = _silu_parts(z2.astype(F32))[0]
            outs = []
            for hh in range(ret_heads):
                sl = slice(hh * RET_VAL_DIM, (hh + 1) * RET_VAL_DIM)
                oh = o2[:, sl]
                r = lax.rsqrt(jnp.mean(oh * oh, axis=-1, keepdims=True) + NORM_EPS)
                outs.append(oh * r * sz[:, sl])
            return a1, jnp.concatenate(outs, axis=-1)

        a_na, a_ret = _ew(act, [('t', o_na, 0, w_na), ('t', u, o_naz // w_na, w_na), ('t', o_ret, 0, w_v), ('t', u, o_retz // w_v, w_v)],
                          [('t', w_na, BF16), ('t', w_v, BF16)], rows=t_len, tr=tr, name="act_%d" % l)
        if l == 0:
            landed_rest = _split_wait(rest_gather, a_na, plan_rest, name="gather_wait_0_rest")
            later_passes = [_split_start([], _split_wait(later_gathers[j], a_na, near_all, name="gather_near_%d" % (j + 1)),
                                         far_all, len(big_axes), name="gather_pass_%d" % (j + 1)) for j in range(depth - 1)]
            landed_rest, tokens = lax.optimization_barrier((landed_rest, [g[4] for g in later_passes]))
            gate = gate + sum(t[0, 0] for t in tokens)
            wpn_f, wpr_f, wout_f = _forward_halves(landed_rest, big_axes[1:], name="gather_forward_0_rest")
        full_w.append((win_f, wpn_f, wpr_f, wout_f))
        y_na = _mm(a_na, wpn_f, name="proj_na_%d" % l)
        y_ret = _mm(a_ret, wpr_f, name="proj_ret_%d" % l)

        def merge(y1, y2, g1, g2):
            return _sigmoid(g1.astype(F32)) * y1.astype(F32) + _sigmoid(g2.astype(F32)) * y2.astype(F32)

        merged, = _ew(merge, [('t', y_na, 0, d_model), ('t', y_ret, 0, d_model), ('t', u, o_gna // d_model, d_model), ('t', u, o_gret // d_model, d_model)],
                      [('t', d_model, BF16)], rows=t_len, tr=tr, name="merge_%d" % l)
        out = _mm(merged, wout_f, out_dtype=F32, name="out_proj_%d" % l)
        x_new, = _ew(lambda xt, ot, gt: xt + gt * ot, [('t', x_all, 0, d_model), ('t', out, 0, d_model), ('g', gate)],
                     [('t', d_model, F32)], rows=t_len, tr=tr, n0=n0, name="resid_%d" % l)
        saved.append(dict(x=x_all, h=h, u=u, bias=bias, o_na=o_na, o_ret=o_ret, states=states, a_na=a_na, a_ret=a_ret,
                          y_na=y_na, y_ret=y_ret, merged=merged, out=out, gate=gate, gs=gs, scale=scale))
        x_all = x_new

    def final(xt, tt, gt):
        r = lax.rsqrt(jnp.mean(xt * xt, axis=-1, keepdims=True) + NORM_EPS)
        xh = xt * r
        e = xh * gt - tt
        dy = e * (1.0 / d_model)
        dyg = dy * gt
        dx = r * (dyg - xh * jnp.mean(dyg * xh, axis=-1, keepdims=True))
        return dx, _rsum(dy * xh), _rsum(e * e)

    dx_lat, d_final_g, loss_cols = _ew(final, [('t', x_all, 0, d_model), ('t', loss_target[0], 0, d_model), ('g', final_g[None, None, :])],
                                       [('t', d_model, F32), ('r', d_model, 1), ('r', d_model, 1)], rows=s_len, tr=tr, name="final")
    loss_part = (0.5 / d_model) * jnp.sum(loss_cols)
    dx_all = jnp.concatenate([dx_lat, jnp.zeros((l_len, d_model), F32)], axis=0)

    big_w = [(w_in, m_w_in, v_w_in), (w_proj_na, m_w_proj_na, v_w_proj_na), (w_proj_ret, m_w_proj_ret, v_w_proj_ret), (w_out, m_w_out, v_w_out)]
    big_res = [None] * 4
    scatters = {}
    back_token = jnp.zeros((), F32)

    pairs = {}

    def start_pair(key, grads, axes):
        plan = _pair_plan(axes)
        lands = []
        for g, ax in zip(grads, axes):
            shp = list(g.shape)
            shp[1 - ax] //= 2
            lands.append(lax.empty(tuple(shp), BF16))
        pairs[key] = (_split_start(grads, lands, plan, len(axes), name="pair_start_%s" % key), axes, plan)
        return pairs[key][0][4]

    def start_scatter(key, after):
        started, axes, pair_plan = pairs[key]
        grads, theirs = _split_wait(started, after, pair_plan, name="pair_wait_%s" % key, with_srcs=True)
        plan = _scatter_plan(axes)
        pair = [_sum_pair(g, t, ax, ci, name="sum_pair_%s_%d" % (key, i)) for i, (g, t, ax) in enumerate(zip(grads, theirs, axes))]
        own = [lax.dynamic_slice_in_dim(s, chip * (s.shape[ax] // N_CHIPS), s.shape[ax] // N_CHIPS, axis=ax) for s, ax in zip(pair, axes)]
        lands = [lax.empty((N_CHIPS - 1,) + o.shape, BF16) for o in own]
        started = _split_start(pair, lands, plan, len(axes) * (N_CHIPS - 1), name="scatter_start_%s" % key)
        scatters[key] = (started, own, axes, plan)
        return started[4]

    def finish_scatter(key, after):
        started, own, axes, plan = scatters[key]
        recv = _split_wait(started, after, plan, name="scatter_wait_%s" % key)
        bufs = [_sum_chips_into(own[i], rbuf, axes[i], ci, name="sum_chips_%s_%d" % (key, i)) for i, rbuf in enumerate(recv)]
        return _share_halves_in_place(bufs, axes, name="share_halves_%s" % key)

    def adamw_big(l, idx, grads, big_res):
        for i, g in zip(idx, grads):
            w3, m3, v3 = big_w[i]
            big_res[i] = _adamw_layer(w3, m3, v3, g, None, l, big_res[i], name="adamw_big_%d_%d" % (i, l))
        return big_res

    small = dict(dmod_lat=[None] * depth, dmod_ctx=[None] * depth, dnorm_g=[None] * depth, drpb=[None] * depth, ddecay=[None] * depth)
    for l in reversed(range(depth)):
        sv = saved[l]
        win_f, wpn_f, wpr_f, wout_f = full_w[l]

        def resid_bwd(dxt, ot, gt):
            return gt * dxt, _rsum(dxt * ot)

        dout, dgate = _ew(resid_bwd, [('t', dx_all, 0, d_model), ('t', sv['out'], 0, d_model), ('g', sv['gate'] + back_token)],
                          [('t', d_model, BF16), ('r', d_model, 2)], rows=t_len, tr=tr, n0=n0, name="resid_bwd_%d" % l)
        dmerged = _mm(dout, wout_f, tb=True, name="out_proj_dx_%d" % l)
        g_wout = _mm(sv['merged'], dout, ta=True, tm=1024, tk=t_len, name="out_proj_dw_%d" % l)

        def merge_bwd(dm, y1, y2, g1, g2):
            dm = dm.astype(F32)
            s1, s2_ = _sigmoid(g1.astype(F32)), _sigmoid(g2.astype(F32))
            return dm * s1, dm * s2_, dm * y1.astype(F32) * s1 * (1.0 - s1), dm * y2.astype(F32) * s2_ * (1.0 - s2_)

        u = sv['u']
        dy_na, dy_ret, dg_na, dg_ret = _ew(
            merge_bwd, [('t', dmerged, 0, d_model), ('t', sv['y_na'], 0, d_model), ('t', sv['y_ret'], 0, d_model),
                        ('t', u, o_gna // d_model, d_model), ('t', u, o_gret // d_model, d_model)],
            [('t', d_model, BF16)] * 4, rows=t_len, tr=tr, name="merge_bwd_%d" % l)
        da_na = _mm(dy_na, wpn_f, tb=True, name="proj_na_dx_%d" % l)
        g_wpn = _mm(sv['a_na'], dy_na, ta=True, tm=1024, tk=t_len, name="proj_na_dw_%d" % l)
        da_ret = _mm(dy_ret, wpr_f, tb=True, name="proj_ret_dx_%d" % l)
        g_wpr = _mm(sv['a_ret'], dy_ret, ta=True, tm=1024, tk=t_len, name="proj_ret_dw_%d" % l)
        lg_l = log_gamma[l]
        if l == 0:
            pair_token = start_pair("0_rest", [g_wpn, g_wpr, g_wout], big_axes[1:])

        def act_bwd(da1, o1, z1, da2, o2, z2):
            da1, da2 = da1.astype(F32), da2.astype(F32)
            si1, ds1 = _silu_parts(z1.astype(F32))
            si2, ds2 = _silu_parts(z2.astype(F32))
            do1 = da1 * si1
            dz1 = da1 * o1.astype(F32) * ds1
            dn = da2 * si2
            do2, dz2 = [], []
            for hh in range(ret_heads):
                sl = slice(hh * RET_VAL_DIM, (hh + 1) * RET_VAL_DIM)
                oh = o2[:, sl]
                r = lax.rsqrt(jnp.mean(oh * oh, axis=-1, keepdims=True) + NORM_EPS)
                nh = oh * r
                dz2.append(da2[:, sl] * nh * ds2[:, sl])
                do2.append(r * (dn[:, sl] - nh * jnp.mean(dn[:, sl] * nh, axis=-1, keepdims=True)))
            return do1, dz1, jnp.concatenate(do2, axis=-1), jnp.concatenate(dz2, axis=-1)

        do_na, dz_na, do_ret, dz_ret = _ew(
            act_bwd, [('t', da_na, 0, w_na), ('t', sv['o_na'], 0, w_na), ('t', u, o_naz // w_na, w_na),
                      ('t', da_ret, 0, w_v), ('t', sv['o_ret'], 0, w_v), ('t', u, o_retz // w_v, w_v)],
            [('t', w_na, BF16), ('t', w_na, BF16), ('t', w_v, BF16), ('t', w_v, BF16)], rows=t_len, tr=tr, name="act_bwd_%d" % l)
        dq_na, dk_na, dv_na, dbias = _na_bwd(u, sv['bias'], sv['o_na'], do_na, s_len=s_len, heads=na_heads, name="na_bwd_%d" % l)
        small['drpb'][l] = _rpb_grad(dbias, name="rpb_grad_%d" % l)
        if l == 0:
            lg_l = lg_l + start_scatter("0_rest", dq_na)[0, 0] + pair_token[0, 0]
        dq_r, dk_r, dv_r, dlg = _ret_bwd(u, c2, s2, lg_l, sv['states'], do_ret, s_len=s_len, heads=ret_heads,
                                         q_off=o_retq, name="ret_bwd_%d" % l)
        small['ddecay'][l] = jnp.transpose(dlg[:, :, 0, 0]) * _sigmoid(-ret_decay_logit[l])
        du_parts = [dq_na, dk_na, dv_na, dz_na, dq_r, dk_r, dv_r, dz_ret, dg_na, dg_ret]
        du, = _ew(lambda *t: jnp.concatenate(t, axis=-1), [('t', p, 0, p.shape[1]) for p in du_parts], [('t', in_cols, BF16)],
                  rows=t_len, tr=tr, name="du_concat_%d" % l)
        g_win = _mm(sv['h'], du, ta=True, tm=1024, tn=1024, tk=t_len, name="in_proj_dw_%d" % l)
        if l > 0:
            du, pair_token = lax.optimization_barrier((du, start_pair("%d_all" % l, [g_win, g_wpn, g_wpr, g_wout], big_axes)))
        else:
            du, in_token = lax.optimization_barrier((du, start_pair("0_in", [g_win], big_axes[:1])))
        dh = _mm(du, win_f, tb=True, out_dtype=F32, tm=1152, tn=1024, name="in_proj_dx_%d" % l)

        def modnorm_bwd(xt, dht, dxt, gs_t):
            r = lax.rsqrt(jnp.mean(xt * xt, axis=-1, keepdims=True) + NORM_EPS)
            xh = xt * r
            dhg = dht * gs_t
            dx = r * (dhg - xh * jnp.mean(dhg * xh, axis=-1, keepdims=True)) + dxt
            return dx, _rsum(dht), _rsum(dht * xh)

        dx_all, dshift, dgs = _ew(modnorm_bwd, [('t', sv['x'], 0, d_model), ('t', dh, 0, d_model), ('t', dx_all, 0, d_model), ('g', sv['gs'])],
                                  [('t', d_model, F32), ('r', d_model, 2), ('r', d_model, 2)], rows=t_len, tr=tr, n0=n0, name="modnorm_bwd_%d" % l)
        dscale = dgs * norm_g[l][None, None, :]
        small['dnorm_g'][l] = jnp.sum(dgs * (1.0 + sv['scale']), axis=(0, 1))
        dmod = jnp.concatenate([dshift, dscale, dgate], axis=-1)[:, 0]
        small['dmod_lat'][l], small['dmod_ctx'][l] = dmod[0], dmod[1]

        if l > 0:
            back_token = start_scatter("%d_all" % l, dx_all)[0, 0] + pair_token[0, 0]

    grad_x = dx_all[:s_len][None]

    drpb = jnp.stack(small['drpb']).reshape(-1)
    ddecay = jnp.stack(small['ddecay']).reshape(-1)
    pieces = [jnp.stack(small['dmod_lat']).reshape(-1), jnp.stack(small['dmod_ctx']).reshape(-1),
              jnp.stack(small['dnorm_g']).reshape(-1), d_final_g.reshape(-1), drpb, ddecay, loss_part[None]]
    sizes = [int(p.shape[0]) for p in pieces]
    pads = [-(-s // LANES) * LANES for s in sizes]
    packed = jnp.concatenate([jnp.pad(p, (0, pd - s)) for p, s, pd in zip(pieces, sizes, pads)])
    gathered = _all_gather_small(_pack_rows(packed), name="gather_small_grads")
    r_small = gathered.shape[1]

    def sum8(*t):
        acc = t[0]
        for other in t[1:]:
            acc = acc + other
        return acc

    total, = _ew(sum8, [('t', gathered, 0, LANES, k) for k in range(N_DEV)], [('t', LANES, F32)], rows=r_small, tr=r_small, name="sum_devices")
    total = total.reshape(-1)
    starts = np.cumsum([0] + pads)
    g_mod_lat_sum, g_mod_ctx, g_norm_g, g_final_g, g_rpb, g_decay, loss = [total[starts[i]:starts[i] + sizes[i]] for i in range(len(pieces))]
    loss = loss[0]
    g_ada_b = (g_mod_lat_sum + g_mod_ctx).reshape(depth, mod_cols)
    g_mod_ctx = g_mod_ctx.reshape(depth, mod_cols)
    dmod_lat_all = gathered.reshape(N_DEV, -1)[:, :depth * mod_cols].reshape(N_DEV, depth, mod_cols)

    dcc_part = jnp.zeros((16, d_model), F32)
    ctx_cols = [lax.dynamic_slice_in_dim(g_mod_ctx[l], chip * mod_shard, mod_shard, axis=0) for l in range(depth)]
    for l in reversed(range(depth)):
        c_rows = jnp.concatenate([ctx_cols[l][None], jnp.zeros((15, mod_shard), F32)], axis=0)
        dcc_part = dcc_part + _mm(c_rows, ada_w, tb=True, b_lead=l, out_dtype=F32, name="ada_dc_%d" % l)
    dcc_all = _all_gather_small(_pack_rows(dcc_part[0]), name="gather_dcc")[:, :d_model // LANES].reshape(N_CHIPS, 2, d_model)[:, 0]

    tail_token = start_scatter("0_in", dcc_all) + in_token
    dcc = ((dcc_all[0] + dcc_all[1]) + dcc_all[2]) + dcc_all[3]
    sg = _sigmoid(c_ctx)
    g_c_ctx = dcc * (sg * (1.0 + c_ctx * (1.0 - sg)))
    for l in reversed(range(1, depth)):
        big_res = adamw_big(l, range(4), finish_scatter("%d_all" % l, tail_token), big_res)

    ada_res = None
    for l in reversed(range(depth)):
        lat_cols = lax.dynamic_slice_in_dim(dmod_lat_all[:, l], chip * mod_shard, mod_shard, axis=1)
        d_rows = jnp.concatenate([lat_cols, ctx_cols[l][None], jnp.zeros((16 - N_DEV - 1, mod_shard), F32)], axis=0) + tail_token[0, 0]
        g_ada = _mm(a_rows, d_rows, ta=True, out_dtype=F32, tm=512, name="ada_dw_%d" % l)
        ada_res = _adamw_layer(ada_w, m_ada_w, v_ada_w, g_ada, None, l, ada_res, name="adamw_ada_%d" % l)

    small_w = [(c_ctx, m_c_ctx, v_c_ctx, g_c_ctx), (ada_b, m_ada_b, v_ada_b, g_ada_b),
               (norm_g, m_norm_g, v_norm_g, g_norm_g), (na_rpb, m_na_rpb, v_na_rpb, g_rpb),
               (ret_decay_logit, m_ret_decay_logit, v_ret_decay_logit, g_decay), (final_g, m_final_g, v_final_g, g_final_g)]
    sw_sizes = [int(np.prod(t[0].shape)) for t in small_w]
    sw_pads = [-(-s // LANES) * LANES for s in sw_sizes]

    def pack(j):
        return _pack_rows(jnp.concatenate([jnp.pad(t[j].reshape(-1), (0, pd - s)) for t, s, pd in zip(small_w, sw_sizes, sw_pads)]))

    pw_, pm_, pv_, pg_ = pack(0), pack(1), pack(2), pack(3)
    sw_out = _ew(lambda w, m, v, g: (g,) + _adamw_math(w, g, m, v),
                 [('t', pw_, 0, LANES), ('t', pm_, 0, LANES), ('t', pv_, 0, LANES), ('t', pg_, 0, LANES)],
                 [('t', LANES, F32)] * 4, rows=pw_.shape[0], tr=pw_.shape[0], name="adamw_small")
    sw_starts = np.cumsum([0] + sw_pads)
    sw_out, ada_res, big_res = lax.optimization_barrier((sw_out, ada_res, big_res))
    big_res = adamw_big(0, range(1, 4), finish_scatter("0_rest", sw_out[0]), big_res)
    big_res = adamw_big(0, range(1), finish_scatter("0_in", sw_out[1]), big_res)

    def unpack(arr, i):
        return arr.reshape(-1)[sw_starts[i]:sw_starts[i] + sw_sizes[i]].reshape(small_w[i][0].shape)

    sm = [[unpack(sw_out[j], i) for i in range(len(small_w))] for j in range(4)]
    def ordered(j):
        return [sm[j][0], ada_res[j], sm[j][1], sm[j][2], big_res[0][j], sm[j][3], sm[j][4],
                big_res[1][j], big_res[2][j], big_res[3][j], sm[j][5]]

    return (loss, grad_x, *ordered(0), *ordered(1), *ordered(2), *ordered(3))
```

```python
import functools
import math

import numpy as np
import jax
import jax.numpy as jnp
from jax import lax
from jax.experimental import pallas as pl
from jax.experimental.pallas import tpu as pltpu

GRID_W = 64
NA_HEAD_DIM = 128
NA_WIN_ROWS = 8
NA_WIN_COLS = 16
NA_GROUP = 8
RET_GROUPS = (1, 2, 3)
RET_KEY_DIM = 128
RET_VAL_DIM = 256
RET_CHUNK = 128
ROPE_BASE = 10000.0
NORM_EPS = 1e-6
MASK_VALUE = -1e30
ADAM_LR = 0.001
ADAM_B1 = 0.9
ADAM_B2 = 0.999
ADAM_EPS = 1e-08
ADAM_WD = 0.01
ADAM_STEP = 10

N_CHIPS = 4
N_DEV = 8
LANES = 128
VMEM_LIMIT = 56 * 1024 * 1024
BF16 = jnp.bfloat16
F32 = jnp.float32
MESH = pl.DeviceIdType.MESH
ANY = pl.BlockSpec(memory_space=pl.ANY)


def _tile(dim, pref, align=LANES):
    if dim <= pref:
        return dim
    t = (pref // align) * align
    while t >= align:
        if dim % t == 0:
            return t
        t -= align
    return dim


def _rows_per_tile(rows, width, tile_bytes=1 << 20):
    return _tile(rows, max(8, tile_bytes // (4 * width)), 8)


def _params(sem):
    return pltpu.CompilerParams(dimension_semantics=sem, vmem_limit_bytes=VMEM_LIMIT)


def _sigmoid(x):
    return 1.0 / (1.0 + jnp.exp(-x))


def _dot(a, b, ca, cb):
    return lax.dot_general(a, b, (((ca,), (cb,)), ((), ())), preferred_element_type=F32)


def _mm(a, b, *, ta=False, tb=False, a_lead=None, b_lead=None, out_dtype=BF16, tm=1152, tn=1024, tk=2048, name):
    ash = a.shape[1:] if a_lead is not None else a.shape
    bsh = b.shape[1:] if b_lead is not None else b.shape
    m, k = (ash[1], ash[0]) if ta else ash
    n, k2 = bsh if tb else (bsh[1], bsh[0])
    assert k == k2, (name, ash, bsh)
    tm, tn, tk = _tile(m, tm), _tile(n, tn), _tile(k, tk)
    nk = k // tk

    def lead(spec_shape, imap, l):
        if l is None:
            return pl.BlockSpec(spec_shape, imap)
        return pl.BlockSpec((None,) + spec_shape, lambda i, j, kk: (l,) + imap(i, j, kk))

    a_spec = lead((tk, tm), lambda i, j, kk: (kk, i), a_lead) if ta else lead((tm, tk), lambda i, j, kk: (i, kk), a_lead)
    b_spec = lead((tn, tk), lambda i, j, kk: (j, kk), b_lead) if tb else lead((tk, tn), lambda i, j, kk: (kk, j), b_lead)
    ca, cb = (0 if ta else 1), (1 if tb else 0)

    def body(a_ref, b_ref, o_ref, *scratch):
        part = _dot(a_ref[...].astype(BF16), b_ref[...].astype(BF16), ca, cb)
        if nk == 1:
            o_ref[...] = part.astype(o_ref.dtype)
            return
        acc_ref, = scratch
        kk = pl.program_id(2)

        @pl.when(kk == 0)
        def _():
            acc_ref[...] = part

        @pl.when(kk > 0)
        def _():
            acc_ref[...] += part

        @pl.when(kk == nk - 1)
        def _():
            o_ref[...] = acc_ref[...].astype(o_ref.dtype)

    return pl.pallas_call(
        body, name=name, grid=(m // tm, n // tn, nk),
        in_specs=[a_spec, b_spec],
        out_specs=pl.BlockSpec((tm, tn), lambda i, j, kk: (i, j)),
        out_shape=jax.ShapeDtypeStruct((m, n), out_dtype),
        scratch_shapes=[] if nk == 1 else [pltpu.VMEM((tm, tn), F32)],
        compiler_params=_params(("parallel", "parallel", "arbitrary")),
    )(a, b)


def _ew(fn, ins, outs, *, rows, tr, name, n0=None, aliases=None):
    assert rows % tr == 0, (name, rows, tr)
    nt = rows // tr

    def grp(i):
        return 0 if n0 is None else jnp.where(i < n0, 0, 1)

    in_specs, args = [], []
    for spec in ins:
        if spec[0] == 't':
            arr, cb, w = spec[1], spec[2], spec[3]
            l = spec[4] if len(spec) > 4 else None
            if l is None:
                in_specs.append(pl.BlockSpec((tr, w), functools.partial(lambda i, cb: (i, cb), cb=cb)))
            else:
                in_specs.append(pl.BlockSpec((None, tr, w), functools.partial(lambda i, cb, l: (l, i, cb), cb=cb, l=l)))
            args.append(arr)
        else:
            arr = spec[1]
            g = arr.shape[0]
            if g == 1:
                in_specs.append(pl.BlockSpec((None, 1, arr.shape[2]), lambda i: (0, 0, 0)))
            else:
                in_specs.append(pl.BlockSpec((None, 1, arr.shape[2]), lambda i: (grp(i), 0, 0)))
            args.append(arr)
    out_specs, out_shapes, is_red = [], [], []
    for spec in outs:
        if spec[0] == 't':
            w, dt = spec[1], spec[2]
            if len(spec) > 3:
                l, nl = spec[3], spec[4]
                out_specs.append(pl.BlockSpec((None, tr, w), functools.partial(lambda i, l: (l, i, 0), l=l)))
                out_shapes.append(jax.ShapeDtypeStruct((nl, rows, w), dt))
            else:
                out_specs.append(pl.BlockSpec((tr, w), lambda i: (i, 0)))
                out_shapes.append(jax.ShapeDtypeStruct((rows, w), dt))
            is_red.append(False)
        else:
            w, g = spec[1], spec[2]
            if g == 1:
                out_specs.append(pl.BlockSpec((None, 1, w), lambda i: (0, 0, 0)))
            else:
                out_specs.append(pl.BlockSpec((None, 1, w), lambda i: (grp(i), 0, 0)))
            out_shapes.append(jax.ShapeDtypeStruct((g, 1, w), F32))
            is_red.append(True)
    n_in = len(ins)
    n_alias = 0 if aliases is None else len(aliases)

    def body(*refs):
        in_refs = refs[:n_in]
        out_refs = refs[n_in + n_alias:]
        res = fn(*[r[...] for r in in_refs])
        if not isinstance(res, (tuple, list)):
            res = (res,)
        i = pl.program_id(0)
        first = (i == 0) if n0 is None else ((i == 0) | (i == n0))
        for o_ref, val, red in zip(out_refs, res, is_red):
            if not red:
                o_ref[...] = val.astype(o_ref.dtype)
            else:
                @pl.when(first)
                def _(o_ref=o_ref, val=val):
                    o_ref[...] = val

                @pl.when(jnp.logical_not(first))
                def _(o_ref=o_ref, val=val):
                    o_ref[...] += val

    io_alias = {}
    if aliases is not None:
        for a_idx, (arr, o_idx) in enumerate(aliases):
            in_specs.append(ANY)
            args.append(arr)
            io_alias[n_in + a_idx] = o_idx
    has_red = any(is_red)
    return pl.pallas_call(
        body, name=name, grid=(nt,), in_specs=in_specs, out_specs=out_specs, out_shape=out_shapes,
        input_output_aliases=io_alias,
        compiler_params=_params(("arbitrary",) if has_red else ("parallel",)),
    )(*args)


def _half_tiles(pr, pw):
    tr, tc = _tile(pr, 256, 16), _tile(pw, 2048)
    return tr, tc, (pr // tr, pw // tc)


def _half_spec(tr, tc, ax, grid):
    if ax == 1:
        return pl.BlockSpec((tr, tc), lambda i, j, sel: (sel[0] * grid[0] + i, j))
    return pl.BlockSpec((tr, tc), lambda i, j, sel: (i, sel[0] * grid[1] + j))


def _sum_pair(g, theirs, ax, ci, *, name):
    pr, pw = theirs.shape
    tr, tc, grid = _half_tiles(pr, pw)

    def body(sel, a_ref, b_ref, o_ref):
        o_ref[...] = (a_ref[...].astype(F32) + b_ref[...].astype(F32)).astype(o_ref.dtype)

    tile = pl.BlockSpec((tr, tc), lambda i, j, sel: (i, j))
    return pl.pallas_call(
        body, name=name,
        grid_spec=pltpu.PrefetchScalarGridSpec(
            num_scalar_prefetch=1, grid=grid, in_specs=[_half_spec(tr, tc, ax, grid), tile], out_specs=tile),
        out_shape=jax.ShapeDtypeStruct((pr, pw), BF16),
        compiler_params=_params(("parallel", "parallel")),
    )(jnp.reshape(ci, (1,)).astype(jnp.int32), g, theirs)


def _sum_chips_into(own, recv, ax, ci, *, name):
    pr, pw = own.shape
    tr, tc, grid = _half_tiles(pr, pw)
    full_shape = (2 * pr, pw) if ax == 1 else (pr, 2 * pw)

    def body(sel, a_ref, r_ref, o_ref):
        acc = a_ref[...].astype(F32)
        for k in range(N_CHIPS - 1):
            acc = acc + r_ref[k].astype(F32)
        o_ref[...] = acc

    return pl.pallas_call(
        body, name=name,
        grid_spec=pltpu.PrefetchScalarGridSpec(
            num_scalar_prefetch=1, grid=grid,
            in_specs=[pl.BlockSpec((tr, tc), lambda i, j, sel: (i, j)),
                      pl.BlockSpec((N_CHIPS - 1, tr, tc), lambda i, j, sel: (0, i, j))],
            out_specs=_half_spec(tr, tc, ax, grid)),
        out_shape=jax.ShapeDtypeStruct(full_shape, F32),
        compiler_params=_params(("parallel", "parallel")),
    )(jnp.reshape(ci, (1,)).astype(jnp.int32), own, recv)


def _rsum(v):
    return jnp.sum(v, axis=0, keepdims=True)


def _silu_parts(z):
    sg = _sigmoid(z)
    return z * sg, sg * (1.0 + z * (1.0 - sg))


def _na_bias_table(rpb, rows, *, name):
    kh, kw = NA_WIN_ROWS, NA_WIN_COLS
    assert rows >= kh
    heads = rpb.shape[0]
    e1, e2 = _na_onehots()
    rpb16 = jnp.pad(rpb, ((0, 0), (0, 16 - rpb.shape[1]), (0, LANES - rpb.shape[2])))

    def body(r_ref, e1_ref, e2_ref, o_ref):
        e1b = e1_ref[...].astype(BF16)
        y = sum(_dot(e1b, part, 0, 0) for part in _split3(r_ref[...]))
        e2b = e2_ref[...].astype(BF16)
        o_ref[...] = sum(_dot(part, e2b, 1, 1) for part in _split3(y))

    z = pl.pallas_call(
        body, name=name, grid=(heads,),
        in_specs=[pl.BlockSpec((None, 16, LANES), lambda h: (h, 0, 0)),
                  pl.BlockSpec(e1.shape, lambda h: (0, 0)), pl.BlockSpec(e2.shape, lambda h: (0, 0))],
        out_specs=pl.BlockSpec((None, kh * kh, GRID_W * GRID_W), lambda h: (h, 0, 0)),
        out_shape=jax.ShapeDtypeStruct((heads, kh * kh, GRID_W * GRID_W), F32),
        compiler_params=_params(("parallel",)),
    )(rpb16, e1, e2)
    return z


def _na_bias_layout(z):
    heads = z.shape[0]
    kh, kw = NA_WIN_ROWS, NA_WIN_COLS
    cidx = np.arange(GRID_W)
    c0 = np.clip(cidx - kw // 2, 0, GRID_W - kw)
    col_in = (cidx[None, :] >= c0[:, None]) & (cidx[None, :] < c0[:, None] + kw)
    bias = z.reshape(heads, kh, kh, GRID_W, GRID_W).transpose(0, 1, 3, 2, 4)
    bias = jnp.where(col_in[None, None, :, None, :], bias, MASK_VALUE)
    return bias.reshape(heads, kh, GRID_W, kh * GRID_W)


def _na_onehots():
    kh, kw = NA_WIN_ROWS, NA_WIN_COLS
    cidx = np.arange(GRID_W)
    dc = cidx[None, :] - cidx[:, None] + (kw - 1)
    e2 = np.zeros((GRID_W * GRID_W, LANES), np.float32)
    ok = (dc >= 0) & (dc <= 2 * kw - 2)
    cq, ck = np.nonzero(ok)
    e2[cq * GRID_W + ck, dc[cq, ck]] = 1.0
    dr = np.arange(kh)[None, :] - np.arange(kh)[:, None] + (kh - 1)
    e1 = np.zeros((16, kh * kh), np.float32)
    dl, kr = np.nonzero(np.ones_like(dr))
    e1[dr[dl, kr], dl * kh + kr] = 1.0
    return jnp.asarray(e1), jnp.asarray(e2)


def _na_fwd(u, bias, *, s_len, heads, name):
    t_len = u.shape[0]
    rows = s_len // GRID_W
    nloc = NA_WIN_ROWS * GRID_W
    scale = NA_HEAD_DIM ** -0.5
    hd = NA_HEAD_DIM

    def body(q_ref, k_ref, v_ref, b_ref, o_ref):
        kc = k_ref[s_len:t_len, :]
        vc = v_ref[s_len:t_len, :]

        def group(g, carry):
            rs = [g * NA_GROUP + i for i in range(NA_GROUP)]
            r0s = [jnp.clip(r - NA_WIN_ROWS // 2, 0, rows - NA_WIN_ROWS) for r in rs]
            gs_ = pl.multiple_of(g * (NA_GROUP * GRID_W), NA_GROUP * GRID_W)
            kss = [pl.multiple_of(r0 * GRID_W, GRID_W) for r0 in r0s]
            q_all = q_ref[pl.ds(gs_, NA_GROUP * GRID_W), :]
            s_ctx = _dot(q_all, kc, 1, 1) * scale
            s_loc = [_dot(q_all[i * GRID_W:(i + 1) * GRID_W], k_ref[pl.ds(kss[i], nloc), :], 1, 1) * scale + b_ref[rs[i] - r0s[i]]
                     for i in range(NA_GROUP)]
            p_loc, p_ctx, inv = [], [], []
            for i in range(NA_GROUP):
                sc = s_ctx[i * GRID_W:(i + 1) * GRID_W]
                m = jnp.maximum(jnp.max(s_loc[i], axis=-1, keepdims=True), jnp.max(sc, axis=-1, keepdims=True))
                pl_, pc_ = jnp.exp(s_loc[i] - m), jnp.exp(sc - m)
                inv.append(1.0 / (jnp.sum(pl_, axis=-1, keepdims=True) + jnp.sum(pc_, axis=-1, keepdims=True)))
                p_loc.append(pl_.astype(BF16))
                p_ctx.append(pc_.astype(BF16))
            o_ctx = _dot(jnp.concatenate(p_ctx, axis=0), vc, 1, 0)
            o_loc = [_dot(p_loc[i], v_ref[pl.ds(kss[i], nloc), :], 1, 0) for i in range(NA_GROUP)]
            out = jnp.concatenate([(o_loc[i] + o_ctx[i * GRID_W:(i + 1) * GRID_W]) * inv[i] for i in range(NA_GROUP)], axis=0)
            o_ref[pl.ds(gs_, NA_GROUP * GRID_W), :] = out.astype(o_ref.dtype)
            return carry

        lax.fori_loop(0, rows // NA_GROUP, group, 0)
        qc = q_ref[s_len:t_len, :]
        s = _dot(qc, kc, 1, 1) * scale
        p = jnp.exp(s - jnp.max(s, axis=-1, keepdims=True))
        o = _dot(p.astype(BF16), vc, 1, 0) / jnp.sum(p, axis=-1, keepdims=True)
        o_ref[s_len:t_len, :] = o.astype(o_ref.dtype)

    col = lambda off: pl.BlockSpec((t_len, hd), functools.partial(lambda h, off: (0, off + h), off=off))
    return pl.pallas_call(
        body, name=name, grid=(heads,),
        in_specs=[col(0), col(heads), col(2 * heads),
                  pl.BlockSpec((None, NA_WIN_ROWS, GRID_W, nloc), lambda h: (h, 0, 0, 0))],
        out_specs=pl.BlockSpec((t_len, hd), lambda h: (0, h)),
        out_shape=jax.ShapeDtypeStruct((t_len, heads * hd), BF16),
        compiler_params=_params(("parallel",)),
    )(u, u, u, bias)


def _na_bwd(u, bias, o, do, *, s_len, heads, name):
    t_len = u.shape[0]
    rows = s_len // GRID_W
    nloc = NA_WIN_ROWS * GRID_W
    scale = NA_HEAD_DIM ** -0.5
    hd = NA_HEAD_DIM

    def body(q_ref, k_ref, v_ref, b_ref, o_ref, do_ref, dq_ref, dk_ref, dv_ref, db_ref, dk_acc, dv_acc):
        kc = k_ref[s_len:t_len, :]
        vc = v_ref[s_len:t_len, :]
        dk_acc[...] = jnp.zeros_like(dk_acc)
        dv_acc[...] = jnp.zeros_like(dv_acc)
        db_ref[...] = jnp.zeros_like(db_ref)

        def group(g, carry):
            n_g, rw = NA_GROUP, GRID_W
            rs = [g * n_g + i for i in range(n_g)]
            r0s = [jnp.clip(r - NA_WIN_ROWS // 2, 0, rows - NA_WIN_ROWS) for r in rs]
            dls = [r - r0 for r, r0 in zip(rs, r0s)]
            gs_ = pl.ds(pl.multiple_of(g * (n_g * rw), n_g * rw), n_g * rw)
            kss = [pl.ds(pl.multiple_of(r0 * rw, rw), nloc) for r0 in r0s]
            row_of = lambda a, i: a[i * rw:(i + 1) * rw]
            q_all, do_all = q_ref[gs_, :], do_ref[gs_, :]
            dlt_all = jnp.sum(do_all.astype(F32) * o_ref[gs_, :].astype(F32), axis=-1, keepdims=True)
            s_ctx = _dot(q_all, kc, 1, 1) * scale
            dp_ctx = _dot(do_all, vc, 1, 1)
            s_loc = [_dot(row_of(q_all, i), k_ref[kss[i], :], 1, 1) * scale + b_ref[dls[i]] for i in range(n_g)]
            dp_loc = [_dot(row_of(do_all, i), v_ref[kss[i], :], 1, 1) for i in range(n_g)]
            p_loc_b, ds_loc_b, p_ctx_b, ds_ctx_b = [], [], [], []
            for i in range(n_g):
                sc, dlt = row_of(s_ctx, i), row_of(dlt_all, i)
                m = jnp.maximum(jnp.max(s_loc[i], axis=-1, keepdims=True), jnp.max(sc, axis=-1, keepdims=True))
                pl_, pc_ = jnp.exp(s_loc[i] - m), jnp.exp(sc - m)
                inv = 1.0 / (jnp.sum(pl_, axis=-1, keepdims=True) + jnp.sum(pc_, axis=-1, keepdims=True))
                pl_, pc_ = pl_ * inv, pc_ * inv
                ds_l = pl_ * (dp_loc[i] - dlt)
                db_ref[dls[i]] += ds_l
                p_loc_b.append(pl_.astype(BF16))
                ds_loc_b.append(ds_l.astype(BF16))
                p_ctx_b.append(pc_.astype(BF16))
                ds_ctx_b.append((pc_ * (row_of(dp_ctx, i) - dlt)).astype(BF16))
            p_ctx_all, ds_ctx_all = jnp.concatenate(p_ctx_b, axis=0), jnp.concatenate(ds_ctx_b, axis=0)
            dq_ctx = _dot(ds_ctx_all, kc, 1, 0)
            dq_loc = [_dot(ds_loc_b[i], k_ref[kss[i], :], 1, 0) for i in range(n_g)]
            dk_loc = [_dot(ds_loc_b[i], row_of(q_all, i), 0, 0) for i in range(n_g)]
            dv_loc = [_dot(p_loc_b[i], row_of(do_all, i), 0, 0) for i in range(n_g)]
            dk_ctx = _dot(ds_ctx_all, q_all, 0, 0)
            dv_ctx = _dot(p_ctx_all, do_all, 0, 0)
            dq_ref[gs_, :] = ((jnp.concatenate(dq_loc, axis=0) + dq_ctx) * scale).astype(dq_ref.dtype)
            for i in range(n_g):
                dk_acc[kss[i], :] += dk_loc[i] * scale
                dv_acc[kss[i], :] += dv_loc[i]
            dk_acc[s_len:t_len, :] += dk_ctx * scale
            dv_acc[s_len:t_len, :] += dv_ctx
            return carry

        lax.fori_loop(0, rows // NA_GROUP, group, 0)
        qc = q_ref[s_len:t_len, :]
        dout = do_ref[s_len:t_len, :]
        out = o_ref[s_len:t_len, :]
        s = _dot(qc, kc, 1, 1) * scale
        p = jnp.exp(s - jnp.max(s, axis=-1, keepdims=True))
        p = p / jnp.sum(p, axis=-1, keepdims=True)
        dlt = jnp.sum(dout.astype(F32) * out.astype(F32), axis=-1, keepdims=True)
        ds = (p * (_dot(dout, vc, 1, 1) - dlt)).astype(BF16)
        dq_ref[s_len:t_len, :] = (_dot(ds, kc, 1, 0) * scale).astype(dq_ref.dtype)
        dk_acc[s_len:t_len, :] += _dot(ds, qc, 0, 0) * scale
        dv_acc[s_len:t_len, :] += _dot(p.astype(BF16), dout, 0, 0)
        dk_ref[...] = dk_acc[...].astype(dk_ref.dtype)
        dv_ref[...] = dv_acc[...].astype(dv_ref.dtype)

    col = lambda off: pl.BlockSpec((t_len, hd), functools.partial(lambda h, off: (0, off + h), off=off))
    tbl = pl.BlockSpec((None, NA_WIN_ROWS, GRID_W, nloc), lambda h: (h, 0, 0, 0))
    tok = jax.ShapeDtypeStruct((t_len, heads * hd), BF16)
    return pl.pallas_call(
        body, name=name, grid=(heads,),
        in_specs=[col(0), col(heads), col(2 * heads), tbl, col(0), col(0)],
        out_specs=[col(0), col(0), col(0), tbl],
        out_shape=[tok, tok, tok, jax.ShapeDtypeStruct(bias.shape, F32)],
        scratch_shapes=[pltpu.VMEM((t_len, hd), F32), pltpu.VMEM((t_len, hd), F32)],
        compiler_params=_params(("parallel",)),
    )(u, u, u, bias, o, do)


def _split3(x):
    hi = x.astype(BF16)
    r1 = x - hi.astype(F32)
    mid = r1.astype(BF16)
    lo = (r1 - mid.astype(F32)).astype(BF16)
    return hi, mid, lo


def _rpb_grad(dbias, *, name):
    heads = dbias.shape[0]
    kh = NA_WIN_ROWS
    e1, e2 = _na_onehots()
    x = dbias.reshape(heads, kh, GRID_W, kh, GRID_W).transpose(0, 1, 3, 2, 4).reshape(heads, kh * kh, GRID_W * GRID_W)

    def body(x_ref, e1_ref, e2_ref, o_ref):
        e2b = e2_ref[...].astype(BF16)
        y = sum(_dot(part, e2b, 1, 0) for part in _split3(x_ref[...]))
        e1b = e1_ref[...].astype(BF16)
        o_ref[...] = sum(_dot(e1b, part, 1, 0) for part in _split3(y))

    out = pl.pallas_call(
        body, name=name, grid=(heads,),
        in_specs=[pl.BlockSpec((None, kh * kh, GRID_W * GRID_W), lambda h: (h, 0, 0)),
                  pl.BlockSpec(e1.shape, lambda h: (0, 0)), pl.BlockSpec(e2.shape, lambda h: (0, 0))],
        out_specs=pl.BlockSpec((None, 16, LANES), lambda h: (h, 0, 0)),
        out_shape=jax.ShapeDtypeStruct((heads, 16, LANES), F32),
        compiler_params=_params(("parallel",)),
    )(x, e1, e2)
    return out[:, :2 * kh - 1, :2 * NA_WIN_COLS - 1]


def _rope_tables(s_len, l_len):
    nf = RET_KEY_DIM // 4
    t = np.arange(s_len)
    row = (t // GRID_W).astype(np.float32)
    colp = (t % GRID_W).astype(np.float32)
    inv_freq = jnp.asarray(ROPE_BASE, F32) ** (-jnp.arange(nf, dtype=F32) / nf)
    ang = jnp.concatenate([jnp.asarray(row)[:, None] * inv_freq, jnp.asarray(colp)[:, None] * inv_freq], axis=-1)
    cos, sin = jnp.cos(ang), jnp.sin(ang)
    c2 = jnp.concatenate([cos, cos], axis=-1)
    s2 = jnp.concatenate([-sin, sin], axis=-1)
    c2 = jnp.concatenate([c2, jnp.ones((l_len, RET_KEY_DIM), F32)], axis=0)
    s2 = jnp.concatenate([s2, jnp.zeros((l_len, RET_KEY_DIM), F32)], axis=0)
    return c2, s2


def _rope(x, c2, s2):
    return x * c2 + pltpu.roll(x, RET_KEY_DIM // 2, 1) * s2


def _rope_t(d, c2, s2):
    return d * c2 + pltpu.roll(d * s2, RET_KEY_DIM // 2, 1)


def _ret_decays(lg, direction):
    cs = RET_CHUNK
    i_col = lax.broadcasted_iota(jnp.int32, (cs, 1), 0)
    p_col = jnp.where(direction == 0, i_col, cs - 1 - i_col).astype(F32)
    pi = lax.broadcasted_iota(jnp.int32, (cs, cs), 0)
    pj = lax.broadcasted_iota(jnp.int32, (cs, cs), 1)
    diff = jnp.where(direction == 0, pi - pj, pj - pi).astype(F32)
    dm = jnp.where(diff >= 0, jnp.exp(jnp.maximum(diff, 0.0) * lg), 0.0)
    qdec = jnp.exp((p_col + 1.0) * lg)
    kdec = jnp.exp((cs - 1.0 - p_col) * lg)
    cd = jnp.exp(jnp.full((1, 1), cs, F32) * lg)
    return p_col, dm, qdec, kdec, cd


def _ret_chunk_index(t, direction, n_chunks, lat_chunks):
    return jnp.where(direction == 0, lax.rem(t + lat_chunks, n_chunks), n_chunks - 1 - t)


def _ret_fwd(u, c2, s2, lg, *, s_len, heads, q_off, name):
    t_len = u.shape[0]
    cs, dk, dv = RET_CHUNK, RET_KEY_DIM, RET_VAL_DIM
    n_chunks, lat_chunks = t_len // cs, s_len // cs
    k_scale = dk ** -0.5
    qb, kb, vb = q_off // dk, q_off // dk + heads, (q_off + 2 * heads * dk) // dv

    def body(lg_ref, q_ref, k_ref, v_ref, c_ref, s_ref, o_ref, st_ref, qd_s, kv_s):
        h, d = pl.program_id(0), pl.program_id(1)
        _, dm, qdec, kdec, cd = _ret_decays(lg_ref[d, h], d)
        n_g = max(g for g in RET_GROUPS if n_chunks % g == 0)
        rows_of = lambda c: pl.ds(pl.multiple_of(c * cs, cs), cs)

        def local(gi, carry):
            rws = [rows_of(gi * n_g + j) for j in range(n_g)]
            qcs = [_rope(q_ref[r, :].astype(F32), c_ref[r, :], s_ref[r, :]) for r in rws]
            kcs = [_rope(k_ref[r, :].astype(F32), c_ref[r, :], s_ref[r, :]) * k_scale for r in rws]
            vcs = [v_ref[r, :] for r in rws]
            a_raw = [_dot(qcs[j].astype(BF16), kcs[j].astype(BF16), 1, 1) for j in range(n_g)]
            kv = [_dot((kcs[j] * kdec).astype(BF16), vcs[j], 0, 0) for j in range(n_g)]
            inner = [_dot((a_raw[j] * dm).astype(BF16), vcs[j], 1, 0) for j in range(n_g)]
            for j in range(n_g):
                qd_s[rws[j], :] = (qcs[j] * qdec).astype(BF16)
                kv_s[gi * n_g + j] = kv[j]

            @pl.when(d == 0)
            def _():
                for j in range(n_g):
                    o_ref[rws[j], :] = inner[j]

            @pl.when(d == 1)
            def _():
                for j in range(n_g):
                    o_ref[rws[j], :] += inner[j]

            return carry

        lax.fori_loop(0, n_chunks // n_g, local, 0)

        def scan(t, st):
            st_ref[t] = st
            return st * cd + kv_s[_ret_chunk_index(t, d, n_chunks, lat_chunks)]

        lax.fori_loop(0, n_chunks, scan, jnp.zeros((dk, dv), F32))

        def cross(gi, carry):
            ts = [gi * n_g + j for j in range(n_g)]
            rws = [rows_of(_ret_chunk_index(t, d, n_chunks, lat_chunks)) for t in ts]
            outs = [_dot(qd_s[rws[j], :], st_ref[ts[j]].astype(BF16), 1, 0) for j in range(n_g)]
            for j in range(n_g):
                o_ref[rws[j], :] += outs[j]
            return carry

        lax.fori_loop(0, n_chunks // n_g, cross, 0)

    return pl.pallas_call(
        body, name=name, grid=(heads, 2),
        in_specs=[pl.BlockSpec(memory_space=pltpu.SMEM),
                  pl.BlockSpec((t_len, dk), lambda h, d: (0, qb + h)),
                  pl.BlockSpec((t_len, dk), lambda h, d: (0, kb + h)),
                  pl.BlockSpec((t_len, dv), lambda h, d: (0, vb + h)),
                  pl.BlockSpec((t_len, dk), lambda h, d: (0, 0)),
                  pl.BlockSpec((t_len, dk), lambda h, d: (0, 0))],
        out_specs=[pl.BlockSpec((t_len, dv), lambda h, d: (0, h)),
                   pl.BlockSpec((None, None, n_chunks, dk, dv), lambda h, d: (h, d, 0, 0, 0))],
        out_shape=[jax.ShapeDtypeStruct((t_len, heads * dv), F32),
                   jax.ShapeDtypeStruct((heads, 2, n_chunks, dk, dv), F32)],
        scratch_shapes=[pltpu.VMEM((t_len, dk), BF16), pltpu.VMEM((n_chunks, dk, dv), F32)],
        compiler_params=_params(("parallel", "arbitrary")),
    )(lg, u, u, u, c2, s2)


def _ret_bwd(u, c2, s2, lg, states, do, *, s_len, heads, q_off, name):
    t_len = u.shape[0]
    cs, dk, dv = RET_CHUNK, RET_KEY_DIM, RET_VAL_DIM
    n_chunks, lat_chunks = t_len // cs, s_len // cs
    k_scale = dk ** -0.5
    qb, kb, vb = q_off // dk, q_off // dk + heads, (q_off + 2 * heads * dk) // dv

    def body(lg_ref, q_ref, k_ref, v_ref, c_ref, s_ref, st_ref, do_ref, dq_ref, dk_ref, dv_ref, dlg_ref, acc, qdo_s, dst_s):
        h, d = pl.program_id(0), pl.program_id(1)
        p_col, dm, qdec, kdec, cd = _ret_decays(lg_ref[d, h], d)
        acc[...] = jnp.zeros_like(acc)
        n_g = max(g for g in RET_GROUPS[:2] if n_chunks % g == 0)
        rows_of = lambda c: pl.ds(pl.multiple_of(c * cs, cs), cs)
        chunk_of = lambda t: _ret_chunk_index(t, d, n_chunks, lat_chunks)

        def local(gi, carry):
            rws = [rows_of(gi * n_g + j) for j in range(n_g)]
            qds = [(_rope(q_ref[r, :].astype(F32), c_ref[r, :], s_ref[r, :]) * qdec).astype(BF16) for r in rws]
            prods = [_dot(qds[j], do_ref[rws[j], :].astype(BF16), 0, 0) for j in range(n_g)]
            for j in range(n_g):
                qdo_s[gi * n_g + j] = prods[j]
            return carry

        lax.fori_loop(0, n_chunks // n_g, local, 0)

        def scan(i, dst):
            t = n_chunks - 1 - i
            dst_s[t] = dst
            return dst * cd + qdo_s[chunk_of(t)]

        lax.fori_loop(0, n_chunks, scan, jnp.zeros((dk, dv), F32))

        def grads(gi, carry):
            ts = [gi * n_g + j for j in range(n_g)]
            rws = [rows_of(chunk_of(t)) for t in ts]
            ccs, sss = [c_ref[r, :] for r in rws], [s_ref[r, :] for r in rws]
            qcs = [_rope(q_ref[r, :].astype(F32), cc, ss) for r, cc, ss in zip(rws, ccs, sss)]
            kcs = [_rope(k_ref[r, :].astype(F32), cc, ss) * k_scale for r, cc, ss in zip(rws, ccs, sss)]
            vcs = [v_ref[r, :] for r in rws]
            docs = [do_ref[r, :].astype(BF16) for r in rws]
            sts = [st_ref[t] for t in ts]
            dsts = [dst_s[t] for t in ts]
            q16 = [x.astype(BF16) for x in qcs]
            k16 = [x.astype(BF16) for x in kcs]
            dst16 = [x.astype(BF16) for x in dsts]
            rng = range(n_g)
            a_raw = [_dot(q16[j], k16[j], 1, 1) for j in rng]
            da_raw = [_dot(docs[j], vcs[j], 1, 1) for j in rng]
            dq_c = [_dot(docs[j], sts[j].astype(BF16), 1, 1) * qdec for j in rng]
            dv_s = [_dot((kcs[j] * kdec).astype(BF16), dst16[j], 1, 0) for j in rng]
            dk_s = [_dot(vcs[j], dst16[j], 1, 1) * kdec for j in rng]
            a16 = [(a_raw[j] * dm).astype(BF16) for j in rng]
            dam = [(da_raw[j] * dm).astype(BF16) for j in rng]
            dq_i = [_dot(dam[j], k16[j], 1, 0) for j in rng]
            dk_i = [_dot(dam[j], q16[j], 0, 0) for j in rng]
            dv_i = [_dot(a16[j], docs[j], 0, 0) for j in rng]
            for j in rng:
                g = (jnp.sum(qcs[j] * (p_col * dq_i[j] + (p_col + 1.0) * dq_c[j]), axis=-1, keepdims=True)
                     + jnp.sum(kcs[j] * ((cs - 1.0 - p_col) * dk_s[j] - p_col * dk_i[j]), axis=-1, keepdims=True))
                g = (jnp.sum(g, axis=0, keepdims=True)
                     + cs * cd * jnp.sum(jnp.sum(dsts[j] * sts[j], axis=-1, keepdims=True), axis=0, keepdims=True))
                acc[...] += jnp.broadcast_to(g, acc.shape)
            dqs = [_rope_t(dq_i[j] + dq_c[j], ccs[j], sss[j]) for j in rng]
            dks = [_rope_t((dk_i[j] + dk_s[j]) * k_scale, ccs[j], sss[j]) for j in rng]
            dvs = [dv_i[j] + dv_s[j] for j in rng]

            @pl.when(d == 0)
            def _():
                for j in rng:
                    dq_ref[rws[j], :] = dqs[j].astype(dq_ref.dtype)
                    dk_ref[rws[j], :] = dks[j].astype(dk_ref.dtype)
                    dv_ref[rws[j], :] = dvs[j].astype(dv_ref.dtype)

            @pl.when(d == 1)
            def _():
                for j in rng:
                    dq_ref[rws[j], :] = (dq_ref[rws[j], :].astype(F32) + dqs[j]).astype(dq_ref.dtype)
                    dk_ref[rws[j], :] = (dk_ref[rws[j], :].astype(F32) + dks[j]).astype(dk_ref.dtype)
                    dv_ref[rws[j], :] = (dv_ref[rws[j], :].astype(F32) + dvs[j]).astype(dv_ref.dtype)

            return carry

        lax.fori_loop(0, n_chunks // n_g, grads, 0)
        dlg_ref[...] = acc[...]

    return pl.pallas_call(
        body, name=name, grid=(heads, 2),
        in_specs=[pl.BlockSpec(memory_space=pltpu.SMEM),
                  pl.BlockSpec((t_len, dk), lambda h, d: (0, qb + h)),
                  pl.BlockSpec((t_len, dk), lambda h, d: (0, kb + h)),
                  pl.BlockSpec((t_len, dv), lambda h, d: (0, vb + h)),
                  pl.BlockSpec((t_len, dk), lambda h, d: (0, 0)),
                  pl.BlockSpec((t_len, dk), lambda h, d: (0, 0)),
                  pl.BlockSpec((None, None, n_chunks, dk, dv), lambda h, d: (h, d, 0, 0, 0)),
                  pl.BlockSpec((t_len, dv), lambda h, d: (0, h))],
        out_specs=[pl.BlockSpec((t_len, dk), lambda h, d: (0, h)),
                   pl.BlockSpec((t_len, dk), lambda h, d: (0, h)),
                   pl.BlockSpec((t_len, dv), lambda h, d: (0, h)),
                   pl.BlockSpec((None, None, 8, LANES), lambda h, d: (h, d, 0, 0))],
        out_shape=[jax.ShapeDtypeStruct((t_len, heads * dk), BF16),
                   jax.ShapeDtypeStruct((t_len, heads * dk), BF16),
                   jax.ShapeDtypeStruct((t_len, heads * dv), BF16),
                   jax.ShapeDtypeStruct((heads, 2, 8, LANES), F32)],
        scratch_shapes=[pltpu.VMEM((8, LANES), F32), pltpu.VMEM((n_chunks, dk, dv), F32), pltpu.VMEM((n_chunks, dk, dv), F32)],
        compiler_params=_params(("parallel", "arbitrary")),
    )(lg, u, u, u, c2, s2, states, do)


def _mesh_pos():
    return lax.axis_index("x"), lax.axis_index("y"), lax.axis_index("c")


def _all_gather_small(buf, *, name):
    r = buf.shape[0]

    def body(x_ref, o_ref, send_sems, recv_sems, local_sem):
        x, y, c = _mesh_pos()
        me = 4 * x + 2 * y + c
        mine = pltpu.make_async_copy(x_ref, o_ref.at[me], local_sem)
        mine.start()
        copies = []
        for k in range(1, N_DEV):
            px, py, pc = x ^ ((k >> 2) & 1), y ^ ((k >> 1) & 1), c ^ (k & 1)
            cp = pltpu.make_async_remote_copy(
                src_ref=x_ref, dst_ref=o_ref.at[me], send_sem=send_sems.at[k - 1], recv_sem=recv_sems.at[k - 1],
                device_id=(px, py, pc), device_id_type=MESH)
            cp.start()
            copies.append((cp, 4 * px + 2 * py + pc))
        for k, (cp, peer) in enumerate(copies):
            pltpu.make_async_remote_copy(
                src_ref=x_ref, dst_ref=o_ref.at[peer], send_sem=send_sems.at[k], recv_sem=recv_sems.at[k],
                device_id=(x, y, c), device_id_type=MESH).wait_recv()
        for cp, _ in copies:
            cp.wait_send()
        mine.wait()

    return pl.pallas_call(
        body, name=name,
        in_specs=[pl.BlockSpec(memory_space=pltpu.VMEM)],
        out_specs=pl.BlockSpec(memory_space=pltpu.VMEM),
        out_shape=jax.ShapeDtypeStruct((N_DEV, r, LANES), F32),
        scratch_shapes=[pltpu.SemaphoreType.DMA((N_DEV - 1,)), pltpu.SemaphoreType.DMA((N_DEV - 1,)),
                        pltpu.SemaphoreType.DMA],
        compiler_params=pltpu.CompilerParams(vmem_limit_bytes=VMEM_LIMIT),
    )(buf)


def _cut(ref, shard_axis, *, chip=None, half=None, lead=None):
    shape = ref.shape[1:] if lead is not None else ref.shape
    idx = [slice(None), slice(None)]
    if chip is not None:
        w = shape[shard_axis] // N_CHIPS
        idx[shard_axis] = pl.ds(pl.multiple_of(chip * w, w), w)
    if half is not None:
        hw = shape[1 - shard_axis] // 2
        idx[1 - shard_axis] = pl.ds(pl.multiple_of(half * hw, hw), hw)
    if lead is not None:
        idx = [lead] + idx
    return ref.at[tuple(idx)]


def _wait_recv(ref, send_sem, recv_sem):
    pltpu.make_async_remote_copy(src_ref=ref, dst_ref=ref, send_sem=send_sem, recv_sem=recv_sem,
                                 device_id=_mesh_pos(), device_id_type=MESH).wait_recv()


def _gather_plan(axes):
    def plan(srcs, lands, send_sems, recv_sems):
        x, y, c = _mesh_pos()
        chip = 2 * x + y
        copies = []
        for i, ax in enumerate(axes):
            for k in range(1, N_CHIPS):
                px, py = x ^ (k >> 1), y ^ (k & 1)
                mine = _cut(lands[i], ax, chip=chip, half=c)
                j = i * (N_CHIPS - 1) + k - 1
                sems = dict(send_sem=send_sems.at[j], recv_sem=recv_sems.at[j], device_id=(px, py, c), device_id_type=MESH)
                send = pltpu.make_async_remote_copy(src_ref=mine, dst_ref=mine, **sems)
                recv = pltpu.make_async_remote_copy(src_ref=mine, dst_ref=_cut(lands[i], ax, chip=2 * px + py, half=c), **sems)
                copies.append((send, recv))
        return copies
    return plan


def _gather_near_plan(axes):
    def plan(srcs, lands, send_sems, recv_sems):
        x, y, c = _mesh_pos()
        copies = []
        for i, ax in enumerate(axes):
            mine = _cut(lands[i], ax, chip=2 * x + y, half=c)
            for k, (px, py) in enumerate(((1 - x, y), (x, 1 - y))):
                sems = dict(send_sem=send_sems.at[2 * i + k], recv_sem=recv_sems.at[2 * i + k], device_id=(px, py, c), device_id_type=MESH)
                send = pltpu.make_async_remote_copy(src_ref=mine, dst_ref=mine, **sems)
                recv = pltpu.make_async_remote_copy(src_ref=mine, dst_ref=_cut(lands[i], ax, chip=2 * px + py, half=c), **sems)
                copies.append((send, recv))
        return copies
    return plan


def _gather_far_plan(axes):
    def plan(srcs, lands, send_sems, recv_sems):
        x, y, c = _mesh_pos()
        from_chip = 2 * (x ^ (1 - c)) + (y ^ c)
        to = (x ^ c, y ^ (1 - c), c)
        diag = 2 * (1 - x) + (1 - y)
        copies = []
        for i, ax in enumerate(axes):
            passed = _cut(lands[i], ax, chip=from_chip, half=c)
            sems = dict(send_sem=send_sems.at[i], recv_sem=recv_sems.at[i], device_id=to, device_id_type=MESH)
            send = pltpu.make_async_remote_copy(src_ref=passed, dst_ref=passed, **sems)
            recv = pltpu.make_async_remote_copy(src_ref=passed, dst_ref=_cut(lands[i], ax, chip=diag, half=c), **sems)
            copies.append((send, recv))
        return copies
    return plan


def _forward_plan(axes):
    def plan(srcs, lands, send_sems, recv_sems):
        x, y, c = _mesh_pos()
        copies = []
        for i, ax in enumerate(axes):
            for k in range(1, N_CHIPS):
                peer_chip = 2 * (x ^ (k >> 1)) + (y ^ (k & 1))
                j = i * (N_CHIPS - 1) + k - 1
                sems = dict(send_sem=send_sems.at[j], recv_sem=recv_sems.at[j], device_id=(x, y, 1 - c), device_id_type=MESH)
                landed = _cut(lands[i], ax, chip=peer_chip, half=c)
                send = pltpu.make_async_remote_copy(src_ref=landed, dst_ref=landed, **sems)
                recv = pltpu.make_async_remote_copy(src_ref=landed, dst_ref=_cut(lands[i], ax, chip=peer_chip, half=1 - c), **sems)
                copies.append((send, recv))
        return copies
    return plan


def _pair_plan(axes):
    def plan(srcs, lands, send_sems, recv_sems):
        x, y, c = _mesh_pos()
        copies = []
        for i, ax in enumerate(axes):
            cp = pltpu.make_async_remote_copy(
                src_ref=_cut(srcs[i], ax, half=1 - c), dst_ref=lands[i], send_sem=send_sems.at[i], recv_sem=recv_sems.at[i],
                device_id=(x, y, 1 - c), device_id_type=MESH)
            copies.append((cp, cp))
        return copies
    return plan


def _scatter_plan(axes):
    def plan(srcs, lands, send_sems, recv_sems):
        x, y, c = _mesh_pos()
        copies = []
        for i, ax in enumerate(axes):
            for k in range(1, N_CHIPS):
                px, py = x ^ (k >> 1), y ^ (k & 1)
                j = i * (N_CHIPS - 1) + k - 1
                cp = pltpu.make_async_remote_copy(
                    src_ref=_cut(srcs[i], ax, chip=2 * px + py), dst_ref=lands[i].at[k - 1],
                    send_sem=send_sems.at[j], recv_sem=recv_sems.at[j], device_id=(px, py, c), device_id_type=MESH)
                copies.append((cp, cp))
        return copies
    return plan


HBM = pl.BlockSpec(memory_space=pltpu.HBM)
SEM = pl.BlockSpec(memory_space=pltpu.SEMAPHORE)
EFFECT = pltpu.SideEffectType.DATAFLOW_SIDE_EFFECTING


def _in_hbm(arrays):
    return [pltpu.with_memory_space_constraint(a, pltpu.HBM) for a in arrays]


def _split_start(srcs, lands, plan, n_copies, *, name):
    bufs = list(srcs) + list(lands)
    ns, nb = len(srcs), len(bufs)

    def body(*refs):
        send_sems, recv_sems, token = refs[nb], refs[nb + 1], refs[-1]
        for send, _ in plan(refs[:ns], refs[ns:nb], send_sems, recv_sems):
            send.start()
        token[...] = jnp.zeros_like(token)

    sems = pltpu.SemaphoreType.DMA((n_copies,))
    res = pl.pallas_call(
        body, name=name, in_specs=[HBM] * nb,
        out_specs=[SEM, SEM] + [HBM] * nb + [pl.BlockSpec(memory_space=pltpu.VMEM)],
        out_shape=[sems, sems] + [pltpu.HBM(a.shape, a.dtype) for a in bufs] + [jax.ShapeDtypeStruct((8, LANES), F32)],
        input_output_aliases={j: 2 + j for j in range(nb)},
        compiler_params=pltpu.CompilerParams(has_side_effects=EFFECT),
    )(*_in_hbm(bufs))
    return res[0], res[1], res[2:2 + ns], res[2 + ns:2 + nb], res[-1]


def _split_wait(started, after, plan, *, name, with_srcs=False):
    send_sems, recv_sems, srcs, lands, _ = started
    bufs = list(srcs) + list(lands)
    ns, nb = len(srcs), len(bufs)

    def body(*refs):
        for send, recv in plan(refs[:ns], refs[ns:nb], refs[nb], refs[nb + 1]):
            send.wait_send()
            recv.wait_recv()

    res = pl.pallas_call(
        body, name=name, in_specs=[HBM] * nb + [SEM, SEM, ANY], out_specs=[HBM] * nb,
        out_shape=[pltpu.HBM(a.shape, a.dtype) for a in bufs],
        input_output_aliases={j: j for j in range(nb)},
        compiler_params=pltpu.CompilerParams(has_side_effects=EFFECT),
    )(*bufs, send_sems, recv_sems, after)
    return (res[:ns], res[ns:]) if with_srcs else res[ns:]


def _cast_into_full(w3, layer, ax, chip, *, after=None, name):
    _, r, wd = w3.shape
    tr = _rows_per_tile(r, wd, 4 << 20)
    nt = r // tr
    full_shape = (r, wd * N_CHIPS) if ax == 1 else (r * N_CHIPS, wd)
    out_map = (lambda i, ch: (i, ch[0])) if ax == 1 else (lambda i, ch: (ch[0] * nt + i, 0))
    zero = jnp.zeros((1, wd), F32) + (0.0 if after is None else after)

    def body(chip_ref, w_ref, z_ref, o_ref):
        o_ref[...] = (w_ref[...] + z_ref[...]).astype(o_ref.dtype)

    return pl.pallas_call(
        body, name=name,
        grid_spec=pltpu.PrefetchScalarGridSpec(
            num_scalar_prefetch=1, grid=(nt,),
            in_specs=[pl.BlockSpec((None, tr, wd), lambda i, ch: (layer, i, 0)), pl.BlockSpec((1, wd), lambda i, ch: (0, 0))],
            out_specs=pl.BlockSpec((tr, wd), out_map)),
        out_shape=jax.ShapeDtypeStruct(full_shape, BF16),
        compiler_params=_params(("parallel",)),
    )(jnp.reshape(chip, (1,)).astype(jnp.int32), w3, zero)


def _forward_halves(fulls, axes, *, name):
    n = len(fulls)

    def body(*refs):
        bufs = refs[:n]
        send_sems, recv_sems = refs[2 * n:]
        x, y, c = _mesh_pos()
        sends = []
        for i in range(n):
            for k in range(1, N_CHIPS):
                landed = _cut(bufs[i], axes[i], chip=2 * (x ^ (k >> 1)) + (y ^ (k & 1)), half=c)
                cp = pltpu.make_async_remote_copy(
                    src_ref=landed, dst_ref=landed, send_sem=send_sems.at[i, k - 1], recv_sem=recv_sems.at[i, k - 1],
                    device_id=(x, y, 1 - c), device_id_type=MESH)
                cp.start()
                sends.append(cp)
        for i in range(n):
            for k in range(1, N_CHIPS):
                other = _cut(bufs[i], axes[i], chip=2 * (x ^ (k >> 1)) + (y ^ (k & 1)), half=1 - c)
                _wait_recv(other, send_sems.at[i, k - 1], recv_sems.at[i, k - 1])
        for cp in sends:
            cp.wait_send()

    pairs = pltpu.SemaphoreType.DMA((n, N_CHIPS - 1))
    return pl.pallas_call(
        body, name=name, in_specs=[ANY] * n, out_specs=[ANY] * n,
        out_shape=[jax.ShapeDtypeStruct(a.shape, a.dtype) for a in fulls],
        input_output_aliases={j: j for j in range(n)},
        scratch_shapes=[pairs, pairs],
    )(*fulls)


def _share_halves_in_place(bufs, axes, *, name):
    n = len(bufs)

    def body(*refs):
        ins = refs[:n]
        send_sems, recv_sems = refs[2 * n:]
        x, y, c = _mesh_pos()
        sends = []
        for i in range(n):
            mine = _cut(ins[i], axes[i], half=c)
            cp = pltpu.make_async_remote_copy(
                src_ref=mine, dst_ref=mine, send_sem=send_sems.at[i], recv_sem=recv_sems.at[i],
                device_id=(x, y, 1 - c), device_id_type=MESH)
            cp.start()
            sends.append(cp)
        for i in range(n):
            _wait_recv(_cut(ins[i], axes[i], half=1 - c), send_sems.at[i], recv_sems.at[i])
        for cp in sends:
            cp.wait_send()

    sems = pltpu.SemaphoreType.DMA((n,))
    return pl.pallas_call(
        body, name=name, in_specs=[ANY] * n, out_specs=[ANY] * n,
        out_shape=[jax.ShapeDtypeStruct(b.shape, b.dtype) for b in bufs],
        input_output_aliases={j: j for j in range(n)}, scratch_shapes=[sems, sems],
    )(*bufs)


def _adamw_math(w, g, m, v):
    m = ADAM_B1 * m + (1.0 - ADAM_B1) * g
    v = ADAM_B2 * v + (1.0 - ADAM_B2) * (g * g)
    m_hat = m / (1.0 - ADAM_B1 ** ADAM_STEP)
    v_hat = v / (1.0 - ADAM_B2 ** ADAM_STEP)
    delta = -ADAM_LR * (m_hat / (jnp.sqrt(v_hat) + ADAM_EPS) + ADAM_WD * w)
    return delta, m, v


def _adamw_layer(w3, m3, v3, p, q, layer, prev, *, name):
    nl, rows, width = w3.shape
    tr = _rows_per_tile(rows, width)

    def fn(*t):
        if q is None:
            w, m, v, g = t
        else:
            w, m, v, g, g2 = t
            g = g + g2
        delta, m, v = _adamw_math(w, g, m, v)
        return g, delta, m, v

    ins = [('t', w3, 0, width, layer), ('t', m3, 0, width, layer), ('t', v3, 0, width, layer), ('t', p, 0, width)]
    if q is not None:
        ins.append(('t', q, 0, width))
    outs = [('t', width, F32, layer, nl)] * 4
    aliases = None if prev is None else [(prev[i], i) for i in range(4)]
    return _ew(fn, ins, outs, rows=rows, tr=tr, name=name, aliases=aliases)


def _pack_rows(vec):
    n = vec.shape[0]
    r = -(-n // (8 * LANES)) * 8
    return jnp.pad(vec, (0, r * LANES - n)).reshape(r, LANES)


def kernel(x, c, ctx, c_ctx, ada_w, ada_b, norm_g, w_in, na_rpb, ret_decay_logit, w_proj_na, w_proj_ret, w_out, final_g, loss_target, m_c_ctx, m_ada_w, m_ada_b, m_norm_g, m_w_in, m_na_rpb, m_ret_decay_logit, m_w_proj_na, m_w_proj_ret, m_w_out, m_final_g, v_c_ctx, v_ada_w, v_ada_b, v_norm_g, v_w_in, v_na_rpb, v_ret_decay_logit, v_w_proj_na, v_w_proj_ret, v_w_out, v_final_g):
    depth = w_in.shape[0]
    s_len, d_model = x.shape[1], x.shape[2]
    l_len = ctx.shape[1]
    t_len = s_len + l_len
    na_heads = na_rpb.shape[1]
    ret_heads = ret_decay_logit.shape[2]
    w_na = na_heads * NA_HEAD_DIM
    w_qk = ret_heads * RET_KEY_DIM
    w_v = ret_heads * RET_VAL_DIM
    in_cols = w_in.shape[2] * N_CHIPS
    assert in_cols == 4 * w_na + 2 * w_qk + 2 * w_v + 2 * d_model
    assert x.shape[0] == 1 and s_len % (NA_WIN_ROWS * GRID_W) == 0 and l_len % RET_CHUNK == 0
    off = np.cumsum([0, w_na, w_na, w_na, w_na, w_qk, w_qk, w_v, w_v, d_model, d_model])
    o_naz, o_retq, o_retz, o_gna, o_gret = int(off[3]), int(off[4]), int(off[7]), int(off[8]), int(off[9])
    rows = s_len // GRID_W
    tr = _tile(l_len, 256, 8)
    n0 = s_len // tr
    mod_cols = 3 * d_model
    mod_shard = ada_w.shape[2]

    xi, yi, ci = _mesh_pos()
    me = 4 * xi + 2 * yi + ci
    chip = 2 * xi + yi

    big_axes = [1, 1, 0, 0]
    n_big = len(big_axes) * (N_CHIPS - 1)
    gather_plan, scatter_plan = _gather_plan(big_axes), _scatter_plan(big_axes)

    c_silu = c[0] * _sigmoid(c[0])
    cc_silu = c_ctx * _sigmoid(c_ctx)
    c_all = _all_gather_small(_pack_rows(c_silu), name="gather_c")[:, :d_model // LANES].reshape(N_DEV, d_model)
    a_rows = jnp.concatenate([c_all, cc_silu[None], jnp.zeros((16 - N_DEV - 1, d_model), F32)], axis=0)
    mod_part = jnp.stack([_mm(a_rows, ada_w, b_lead=l, out_dtype=F32, name="ada_fwd_%d" % l) for l in range(depth)])
    mod_all = _all_gather_small(_pack_rows(mod_part.reshape(-1)), name="gather_mod")
    n_mod = depth * 16 * mod_shard
    mod_all = mod_all.reshape(N_DEV, -1)[:, :n_mod].reshape(N_CHIPS, 2, depth, 16, mod_shard)[:, 0]
    mod_all = jnp.transpose(mod_all, (1, 2, 0, 3)).reshape(depth, 16, mod_cols) + ada_b[:, None, :]

    big_named = list(zip((w_in, w_proj_na, w_proj_ret, w_out), big_axes, ("w_in", "w_proj_na", "w_proj_ret", "w_out")))
    w_in0 = _cast_into_full(w_in, 0, big_axes[0], chip, name="cast_w_in_0")
    mod_all, w_in0 = lax.optimization_barrier((mod_all, w_in0))
    plan_near, plan_far, plan_rest = _gather_near_plan(big_axes[:1]), _gather_far_plan(big_axes[:1]), _gather_plan(big_axes[1:])
    near_all, far_all, forward_all = _gather_near_plan(big_axes), _gather_far_plan(big_axes), _forward_plan(big_axes)
    first_gather = _split_start([], [w_in0], plan_near, 2, name="gather_start_0_in")
    start_token = first_gather[4][0, 0]
    fulls = [[None if (l == 0 and tag == "w_in") else _cast_into_full(w, l, ax, chip, after=start_token, name="cast_%s_%d" % (tag, l))
              for w, ax, tag in big_named] for l in range(depth)]
    mod_lat = lax.dynamic_index_in_dim(mod_all, me, axis=1, keepdims=False)
    mod_ctx = mod_all[:, N_DEV]
    biases = [_na_bias_layout(_na_bias_table(na_rpb[l], s_len // GRID_W, name="na_bias_%d" % l)) for l in range(depth)]
    biases, fulls = lax.optimization_barrier((biases, fulls))
    landed_near = _split_wait(first_gather, biases[-1], plan_near, name="gather_wait_0_in")
    passing = _split_start([], landed_near, plan_far, 1, name="gather_pass_0_in")
    front_token = passing[4][0, 0]

    c2, s2 = _rope_tables(s_len, l_len)
    log_gamma = jax.nn.log_sigmoid(ret_decay_logit)
    x_all = jnp.concatenate([x[0], ctx[0]], axis=0)

    def grp(lat_vec, ctx_vec):
        return jnp.stack([lat_vec, ctx_vec])[:, None, :]

    saved, full_w = [], []
    for l in range(depth):
        shift, scale, gate = [grp(mod_lat[l, i * d_model:(i + 1) * d_model], mod_ctx[l, i * d_model:(i + 1) * d_model])
                              for i in range(3)]
        gs = norm_g[l][None, None, :] * (1.0 + scale) + front_token

        def modnorm(xt, gs_t, sh_t):
            r = lax.rsqrt(jnp.mean(xt * xt, axis=-1, keepdims=True) + NORM_EPS)
            return xt * r * gs_t + sh_t

        h, = _ew(modnorm, [('t', x_all, 0, d_model), ('g', gs), ('g', shift)], [('t', d_model, BF16)],
                 rows=t_len, tr=tr, n0=n0, name="modnorm_%d" % l)
        bias = biases[l]
        if l == 0:
            h, bias = lax.optimization_barrier((h, bias))
            landed_in = _split_wait(passing, h, plan_far, name="gather_wait_0_in_far")
            landed_in, rest0, later = lax.optimization_barrier((landed_in, fulls[0][1:], fulls[1:]))
            rest_gather = _split_start([], rest0, plan_rest, n_big - (N_CHIPS - 1), name="gather_start_0_rest")
            later_gathers = [_split_start([], later[j], near_all, 2 * len(big_axes), name="gather_start_%d" % (j + 1))
                             for j in range(depth - 1)]
            win_f, = _forward_halves(landed_in, big_axes[:1], name="gather_forward_0_in")
            win_f, tokens = lax.optimization_barrier((win_f, [rest_gather[4]] + [g[4] for g in later_gathers]))
            gate = gate + sum(t[0, 0] for t in tokens)
        else:
            h, bias = lax.optimization_barrier((h, bias))
            win_f, wpn_f, wpr_f, wout_f = _split_wait(next_forward, h, forward_all, name="gather_forward_wait_%d" % l)
        u = _mm(h, win_f, tm=1152, tn=1024, name="in_proj_%d" % l)
        o_na = _na_fwd(u, bias, s_len=s_len, heads=na_heads, name="na_fwd_%d" % l)
        o_ret, states = _ret_fwd(u, c2, s2, log_gamma[l], s_len=s_len, heads=ret_heads, q_off=o_retq, name="ret_fwd_%d" % l)

        def act(o1, z1, o2, z2):
            a1 = o1.astype(F32) * _silu_parts(z1.astype(F32))[0]
            sz = _silu_parts(z2.astype(F32))[0]
            outs = []
            for hh in range(ret_heads):
                sl = slice(hh * RET_VAL_DIM, (hh + 1) * RET_VAL_DIM)
                oh = o2[:, sl]
                r = lax.rsqrt(jnp.mean(oh * oh, axis=-1, keepdims=True) + NORM_EPS)
                outs.append(oh * r * sz[:, sl])
            return a1, jnp.concatenate(outs, axis=-1)

        a_na, a_ret = _ew(act, [('t', o_na, 0, w_na), ('t', u, o_naz // w_na, w_na), ('t', o_ret, 0, w_v), ('t', u, o_retz // w_v, w_v)],
                          [('t', w_na, BF16), ('t', w_v, BF16)], rows=t_len, tr=tr, name="act_%d" % l)
        if l == 0:
            landed_rest = _split_wait(rest_gather, a_na, plan_rest, name="gather_wait_0_rest")
            later_passes = [_split_start([], _split_wait(later_gathers[j], a_na, near_all, name="gather_near_%d" % (j + 1)),
                                         far_all, len(big_axes), name="gather_pass_%d" % (j + 1)) for j in range(depth - 1)]
            landed_rest, tokens = lax.optimization_barrier((landed_rest, [g[4] for g in later_passes]))
            gate = gate + sum(t[0, 0] for t in tokens)
            wpn_f, wpr_f, wout_f = _forward_halves(landed_rest, big_axes[1:], name="gather_forward_0_rest")
        full_w.append((win_f, wpn_f, wpr_f, wout_f))
        y_na = _mm(a_na, wpn_f, name="proj_na_%d" % l)
        y_ret = _mm(a_ret, wpr_f, name="proj_ret_%d" % l)

        def merge(y1, y2, g1, g2):
            return _sigmoid(g1.astype(F32)) * y1.astype(F32) + _sigmoid(g2.astype(F32)) * y2.astype(F32)

        merged, = _ew(merge, [('t', y_na, 0, d_model), ('t', y_ret, 0, d_model), ('t', u, o_gna // d_model, d_model), ('t', u, o_gret // d_model, d_model)],
                      [('t', d_model, BF16)], rows=t_len, tr=tr, name="merge_%d" % l)
        out = _mm(merged, wout_f, out_dtype=F32, name="out_proj_%d" % l)
        if l + 1 < depth:
            landed = _split_wait(later_passes[l], out, far_all, name="gather_wait_%d" % (l + 1))
            next_forward = _split_start([], landed, forward_all, n_big, name="gather_forward_%d" % (l + 1))
            gate = gate + next_forward[4][0, 0]
        x_new, = _ew(lambda xt, ot, gt: xt + gt * ot, [('t', x_all, 0, d_model), ('t', out, 0, d_model), ('g', gate)],
                     [('t', d_model, F32)], rows=t_len, tr=tr, n0=n0, name="resid_%d" % l)
        saved.append(dict(x=x_all, h=h, u=u, bias=bias, o_na=o_na, o_ret=o_ret, states=states, a_na=a_na, a_ret=a_ret,
                          y_na=y_na, y_ret=y_ret, merged=merged, out=out, gate=gate, gs=gs, scale=scale))
        x_all = x_new

    def final(xt, tt, gt):
        r = lax.rsqrt(jnp.mean(xt * xt, axis=-1, keepdims=True) + NORM_EPS)
        xh = xt * r
        e = xh * gt - tt
        dy = e * (1.0 / d_model)
        dyg = dy * gt
        dx = r * (dyg - xh * jnp.mean(dyg * xh, axis=-1, keepdims=True))
        return dx, _rsum(dy * xh), _rsum(e * e)

    dx_lat, d_final_g, loss_cols = _ew(final, [('t', x_all, 0, d_model), ('t', loss_target[0], 0, d_model), ('g', final_g[None, None, :])],
                                       [('t', d_model, F32), ('r', d_model, 1), ('r', d_model, 1)], rows=s_len, tr=tr, name="final")
    loss_part = (0.5 / d_model) * jnp.sum(loss_cols)
    dx_all = jnp.concatenate([dx_lat, jnp.zeros((l_len, d_model), F32)], axis=0)

    big_w = [(w_in, m_w_in, v_w_in), (w_proj_na, m_w_proj_na, v_w_proj_na), (w_proj_ret, m_w_proj_ret, v_w_proj_ret), (w_out, m_w_out, v_w_out)]
    big_res = [None] * 4
    scatters = {}
    back_token = jnp.zeros((), F32)

    pairs = {}

    def start_pair(key, grads, axes):
        plan = _pair_plan(axes)
        lands = []
        for g, ax in zip(grads, axes):
            shp = list(g.shape)
            shp[1 - ax] //= 2
            lands.append(lax.empty(tuple(shp), BF16))
        pairs[key] = (_split_start(grads, lands, plan, len(axes), name="pair_start_%s" % key), axes, plan)
        return pairs[key][0][4]

    def start_scatter(key, after):
        started, axes, pair_plan = pairs[key]
        grads, theirs = _split_wait(started, after, pair_plan, name="pair_wait_%s" % key, with_srcs=True)
        plan = _scatter_plan(axes)
        pair = [_sum_pair(g, t, ax, ci, name="sum_pair_%s_%d" % (key, i)) for i, (g, t, ax) in enumerate(zip(grads, theirs, axes))]
        own = [lax.dynamic_slice_in_dim(s, chip * (s.shape[ax] // N_CHIPS), s.shape[ax] // N_CHIPS, axis=ax) for s, ax in zip(pair, axes)]
        lands = [lax.empty((N_CHIPS - 1,) + o.shape, BF16) for o in own]
        started = _split_start(pair, lands, plan, len(axes) * (N_CHIPS - 1), name="scatter_start_%s" % key)
        scatters[key] = (started, own, axes, plan)
        return started[4]

    def finish_scatter(key, after):
        started, own, axes, plan = scatters[key]
        recv = _split_wait(started, after, plan, name="scatter_wait_%s" % key)
        bufs = [_sum_chips_into(own[i], rbuf, axes[i], ci, name="sum_chips_%s_%d" % (key, i)) for i, rbuf in enumerate(recv)]
        return _share_halves_in_place(bufs, axes, name="share_halves_%s" % key)

    def adamw_big(l, idx, grads, big_res):
        for i, g in zip(idx, grads):
            w3, m3, v3 = big_w[i]
            big_res[i] = _adamw_layer(w3, m3, v3, g, None, l, big_res[i], name="adamw_big_%d_%d" % (i, l))
        return big_res

    small = dict(dmod_lat=[None] * depth, dmod_ctx=[None] * depth, dnorm_g=[None] * depth, drpb=[None] * depth, ddecay=[None] * depth)
    for l in reversed(range(depth)):
        sv = saved[l]
        win_f, wpn_f, wpr_f, wout_f = full_w[l]

        def resid_bwd(dxt, ot, gt):
            return gt * dxt, _rsum(dxt * ot)

        dout, dgate = _ew(resid_bwd, [('t', dx_all, 0, d_model), ('t', sv['out'], 0, d_model), ('g', sv['gate'] + back_token)],
                          [('t', d_model, BF16), ('r', d_model, 2)], rows=t_len, tr=tr, n0=n0, name="resid_bwd_%d" % l)
        dmerged = _mm(dout, wout_f, tb=True, name="out_proj_dx_%d" % l)
        g_wout = _mm(sv['merged'], dout, ta=True, tm=1024, tk=t_len, name="out_proj_dw_%d" % l)

        def merge_bwd(dm, y1, y2, g1, g2):
            dm = dm.astype(F32)
            s1, s2_ = _sigmoid(g1.astype(F32)), _sigmoid(g2.astype(F32))
            return dm * s1, dm * s2_, dm * y1.astype(F32) * s1 * (1.0 - s1), dm * y2.astype(F32) * s2_ * (1.0 - s2_)

        u = sv['u']
        dy_na, dy_ret, dg_na, dg_ret = _ew(
            merge_bwd, [('t', dmerged, 0, d_model), ('t', sv['y_na'], 0, d_model), ('t', sv['y_ret'], 0, d_model),
                        ('t', u, o_gna // d_model, d_model), ('t', u, o_gret // d_model, d_model)],
            [('t', d_model, BF16)] * 4, rows=t_len, tr=tr, name="merge_bwd_%d" % l)
        da_na = _mm(dy_na, wpn_f, tb=True, name="proj_na_dx_%d" % l)
        g_wpn = _mm(sv['a_na'], dy_na, ta=True, tm=1024, tk=t_len, name="proj_na_dw_%d" % l)
        da_ret = _mm(dy_ret, wpr_f, tb=True, name="proj_ret_dx_%d" % l)
        g_wpr = _mm(sv['a_ret'], dy_ret, ta=True, tm=1024, tk=t_len, name="proj_ret_dw_%d" % l)
        lg_l = log_gamma[l]
        if l == 0:
            pair_token = start_pair("0_rest", [g_wpn, g_wpr, g_wout], big_axes[1:])

        def act_bwd(da1, o1, z1, da2, o2, z2):
            da1, da2 = da1.astype(F32), da2.astype(F32)
            si1, ds1 = _silu_parts(z1.astype(F32))
            si2, ds2 = _silu_parts(z2.astype(F32))
            do1 = da1 * si1
            dz1 = da1 * o1.astype(F32) * ds1
            dn = da2 * si2
            do2, dz2 = [], []
            for hh in range(ret_heads):
                sl = slice(hh * RET_VAL_DIM, (hh + 1) * RET_VAL_DIM)
                oh = o2[:, sl]
                r = lax.rsqrt(jnp.mean(oh * oh, axis=-1, keepdims=True) + NORM_EPS)
                nh = oh * r
                dz2.append(da2[:, sl] * nh * ds2[:, sl])
                do2.append(r * (dn[:, sl] - nh * jnp.mean(dn[:, sl] * nh, axis=-1, keepdims=True)))
            return do1, dz1, jnp.concatenate(do2, axis=-1), jnp.concatenate(dz2, axis=-1)

        do_na, dz_na, do_ret, dz_ret = _ew(
            act_bwd, [('t', da_na, 0, w_na), ('t', sv['o_na'], 0, w_na), ('t', u, o_naz // w_na, w_na),
                      ('t', da_ret, 0, w_v), ('t', sv['o_ret'], 0, w_v), ('t', u, o_retz // w_v, w_v)],
            [('t', w_na, BF16), ('t', w_na, BF16), ('t', w_v, BF16), ('t', w_v, BF16)], rows=t_len, tr=tr, name="act_bwd_%d" % l)
        dq_na, dk_na, dv_na, dbias = _na_bwd(u, sv['bias'], sv['o_na'], do_na, s_len=s_len, heads=na_heads, name="na_bwd_%d" % l)
        small['drpb'][l] = _rpb_grad(dbias, name="rpb_grad_%d" % l)
        if l == 0:
            lg_l = lg_l + start_scatter("0_rest", dq_na)[0, 0] + pair_token[0, 0]
        dq_r, dk_r, dv_r, dlg = _ret_bwd(u, c2, s2, lg_l, sv['states'], do_ret, s_len=s_len, heads=ret_heads,
                                         q_off=o_retq, name="ret_bwd_%d" % l)
        small['ddecay'][l] = jnp.transpose(dlg[:, :, 0, 0]) * _sigmoid(-ret_decay_logit[l])
        du_parts = [dq_na, dk_na, dv_na, dz_na, dq_r, dk_r, dv_r, dz_ret, dg_na, dg_ret]
        du, = _ew(lambda *t: jnp.concatenate(t, axis=-1), [('t', p, 0, p.shape[1]) for p in du_parts], [('t', in_cols, BF16)],
                  rows=t_len, tr=tr, name="du_concat_%d" % l)
        g_win = _mm(sv['h'], du, ta=True, tm=1024, tn=1024, tk=t_len, name="in_proj_dw_%d" % l)
        if l > 0:
            du, pair_token = lax.optimization_barrier((du, start_pair("%d_all" % l, [g_win, g_wpn, g_wpr, g_wout], big_axes)))
        else:
            du, in_token = lax.optimization_barrier((du, start_pair("0_in", [g_win], big_axes[:1])))
        dh = _mm(du, win_f, tb=True, out_dtype=F32, tm=1152, tn=1024, name="in_proj_dx_%d" % l)

        def modnorm_bwd(xt, dht, dxt, gs_t):
            r = lax.rsqrt(jnp.mean(xt * xt, axis=-1, keepdims=True) + NORM_EPS)
            xh = xt * r
            dhg = dht * gs_t
            dx = r * (dhg - xh * jnp.mean(dhg * xh, axis=-1, keepdims=True)) + dxt
            return dx, _rsum(dht), _rsum(dht * xh)

        dx_all, dshift, dgs = _ew(modnorm_bwd, [('t', sv['x'], 0, d_model), ('t', dh, 0, d_model), ('t', dx_all, 0, d_model), ('g', sv['gs'])],
                                  [('t', d_model, F32), ('r', d_model, 2), ('r', d_model, 2)], rows=t_len, tr=tr, n0=n0, name="modnorm_bwd_%d" % l)
        dscale = dgs * norm_g[l][None, None, :]
        small['dnorm_g'][l] = jnp.sum(dgs * (1.0 + sv['scale']), axis=(0, 1))
        dmod = jnp.concatenate([dshift, dscale, dgate], axis=-1)[:, 0]
        small['dmod_lat'][l], small['dmod_ctx'][l] = dmod[0], dmod[1]

        if l > 0:
            back_token = start_scatter("%d_all" % l, dx_all)[0, 0] + pair_token[0, 0]

    grad_x = dx_all[:s_len][None]

    drpb = jnp.stack(small['drpb']).reshape(-1)
    ddecay = jnp.stack(small['ddecay']).reshape(-1)
    pieces = [jnp.stack(small['dmod_lat']).reshape(-1), jnp.stack(small['dmod_ctx']).reshape(-1),
              jnp.stack(small['dnorm_g']).reshape(-1), d_final_g.reshape(-1), drpb, ddecay, loss_part[None]]
    sizes = [int(p.shape[0]) for p in pieces]
    pads = [-(-s // LANES) * LANES for s in sizes]
    packed = jnp.concatenate([jnp.pad(p, (0, pd - s)) for p, s, pd in zip(pieces, sizes, pads)])
    gathered = _all_gather_small(_pack_rows(packed), name="gather_small_grads")
    r_small = gathered.shape[1]

    def sum8(*t):
        acc = t[0]
        for other in t[1:]:
            acc = acc + other
        return acc

    total, = _ew(sum8, [('t', gathered, 0, LANES, k) for k in range(N_DEV)], [('t', LANES, F32)], rows=r_small, tr=r_small, name="sum_devices")
    total = total.reshape(-1)
    starts = np.cumsum([0] + pads)
    g_mod_lat_sum, g_mod_ctx, g_norm_g, g_final_g, g_rpb, g_decay, loss = [total[starts[i]:starts[i] + sizes[i]] for i in range(len(pieces))]
    loss = loss[0]
    g_ada_b = (g_mod_lat_sum + g_mod_ctx).reshape(depth, mod_cols)
    g_mod_ctx = g_mod_ctx.reshape(depth, mod_cols)
    dmod_lat_all = gathered.reshape(N_DEV, -1)[:, :depth * mod_cols].reshape(N_DEV, depth, mod_cols)

    dcc_part = jnp.zeros((16, d_model), F32)
    ctx_cols = [lax.dynamic_slice_in_dim(g_mod_ctx[l], chip * mod_shard, mod_shard, axis=0) for l in range(depth)]
    for l in reversed(range(depth)):
        c_rows = jnp.concatenate([ctx_cols[l][None], jnp.zeros((15, mod_shard), F32)], axis=0)
        dcc_part = dcc_part + _mm(c_rows, ada_w, tb=True, b_lead=l, out_dtype=F32, name="ada_dc_%d" % l)
    dcc_all = _all_gather_small(_pack_rows(dcc_part[0]), name="gather_dcc")[:, :d_model // LANES].reshape(N_CHIPS, 2, d_model)[:, 0]

    tail_token = start_scatter("0_in", dcc_all) + in_token
    dcc = ((dcc_all[0] + dcc_all[1]) + dcc_all[2]) + dcc_all[3]
    sg = _sigmoid(c_ctx)
    g_c_ctx = dcc * (sg * (1.0 + c_ctx * (1.0 - sg)))
    for l in reversed(range(1, depth)):
        big_res = adamw_big(l, range(4), finish_scatter("%d_all" % l, tail_token), big_res)

    ada_res = None
    for l in reversed(range(depth)):
        lat_cols = lax.dynamic_slice_in_dim(dmod_lat_all[:, l], chip * mod_shard, mod_shard, axis=1)
        d_rows = jnp.concatenate([lat_cols, ctx_cols[l][None], jnp.zeros((16 - N_DEV - 1, mod_shard), F32)], axis=0) + tail_token[0, 0]
        g_ada = _mm(a_rows, d_rows, ta=True, out_dtype=F32, tm=512, name="ada_dw_%d" % l)
        ada_res = _adamw_layer(ada_w, m_ada_w, v_ada_w, g_ada, None, l, ada_res, name="adamw_ada_%d" % l)

    small_w = [(c_ctx, m_c_ctx, v_c_ctx, g_c_ctx), (ada_b, m_ada_b, v_ada_b, g_ada_b),
               (norm_g, m_norm_g, v_norm_g, g_norm_g), (na_rpb, m_na_rpb, v_na_rpb, g_rpb),
               (ret_decay_logit, m_ret_decay_logit, v_ret_decay_logit, g_decay), (final_g, m_final_g, v_final_g, g_final_g)]
    sw_sizes = [int(np.prod(t[0].shape)) for t in small_w]
    sw_pads = [-(-s // LANES) * LANES for s in sw_sizes]

    def pack(j):
        return _pack_rows(jnp.concatenate([jnp.pad(t[j].reshape(-1), (0, pd - s)) for t, s, pd in zip(small_w, sw_sizes, sw_pads)]))

    pw_, pm_, pv_, pg_ = pack(0), pack(1), pack(2), pack(3)
    sw_out = _ew(lambda w, m, v, g: (g,) + _adamw_math(w, g, m, v),
                 [('t', pw_, 0, LANES), ('t', pm_, 0, LANES), ('t', pv_, 0, LANES), ('t', pg_, 0, LANES)],
                 [('t', LANES, F32)] * 4, rows=pw_.shape[0], tr=pw_.shape[0], name="adamw_small")
    sw_starts = np.cumsum([0] + sw_pads)
    sw_out, ada_res, big_res = lax.optimization_barrier((sw_out, ada_res, big_res))
    big_res = adamw_big(0, range(1, 4), finish_scatter("0_rest", sw_out[0]), big_res)
    big_res = adamw_big(0, range(1), finish_scatter("0_in", sw_out[1]), big_res)

    def unpack(arr, i):
        return arr.reshape(-1)[sw_starts[i]:sw_starts[i] + sw_sizes[i]].reshape(small_w[i][0].shape)

    sm = [[unpack(sw_out[j], i) for i in range(len(small_w))] for j in range(4)]
    def ordered(j):
        return [sm[j][0], ada_res[j], sm[j][1], sm[j][2], big_res[0][j], sm[j][3], sm[j][4],
                big_res[1][j], big_res[2][j], big_res[3][j], sm[j][5]]

    return (loss, grad_x, *ordered(0), *ordered(1), *ordered(2), *ordered(3))
```

```python
import functools
import math

import numpy as np
import jax
import jax.numpy as jnp
from jax import lax
from jax.experimental import pallas as pl
from jax.experimental.pallas import tpu as pltpu

GRID_W = 64
NA_HEAD_DIM = 128
NA_WIN_ROWS = 8
NA_WIN_COLS = 16
NA_GROUP = 8
RET_GROUPS = (1, 2, 3)
RET_KEY_DIM = 128
RET_VAL_DIM = 256
RET_CHUNK = 128
ROPE_BASE = 10000.0
NORM_EPS = 1e-6
MASK_VALUE = -1e30
ADAM_LR = 0.001
ADAM_B1 = 0.9
ADAM_B2 = 0.999
ADAM_EPS = 1e-08
ADAM_WD = 0.01
ADAM_STEP = 10

N_CHIPS = 4
N_DEV = 8
LANES = 128
VMEM_LIMIT = 56 * 1024 * 1024
BF16 = jnp.bfloat16
F32 = jnp.float32
MESH = pl.DeviceIdType.MESH
ANY = pl.BlockSpec(memory_space=pl.ANY)


def _tile(dim, pref, align=LANES):
    if dim <= pref:
        return dim
    t = (pref // align) * align
    while t >= align:
        if dim % t == 0:
            return t
        t -= align
    return dim


def _rows_per_tile(rows, width, tile_bytes=1 << 20):
    return _tile(rows, max(8, tile_bytes // (4 * width)), 8)


def _params(sem):
    return pltpu.CompilerParams(dimension_semantics=sem, vmem_limit_bytes=VMEM_LIMIT)


def _sigmoid(x):
    return 1.0 / (1.0 + jnp.exp(-x))


def _dot(a, b, ca, cb):
    return lax.dot_general(a, b, (((ca,), (cb,)), ((), ())), preferred_element_type=F32)


def _mm(a, b, *, ta=False, tb=False, a_lead=None, b_lead=None, out_dtype=BF16, tm=1152, tn=1024, tk=2048, name):
    ash = a.shape[1:] if a_lead is not None else a.shape
    bsh = b.shape[1:] if b_lead is not None else b.shape
    m, k = (ash[1], ash[0]) if ta else ash
    n, k2 = bsh if tb else (bsh[1], bsh[0])
    assert k == k2, (name, ash, bsh)
    tm, tn, tk = _tile(m, tm), _tile(n, tn), _tile(k, tk)
    nk = k // tk

    def lead(spec_shape, imap, l):
        if l is None:
            return pl.BlockSpec(spec_shape, imap)
        return pl.BlockSpec((None,) + spec_shape, lambda i, j, kk: (l,) + imap(i, j, kk))

    a_spec = lead((tk, tm), lambda i, j, kk: (kk, i), a_lead) if ta else lead((tm, tk), lambda i, j, kk: (i, kk), a_lead)
    b_spec = lead((tn, tk), lambda i, j, kk: (j, kk), b_lead) if tb else lead((tk, tn), lambda i, j, kk: (kk, j), b_lead)
    ca, cb = (0 if ta else 1), (1 if tb else 0)

    def body(a_ref, b_ref, o_ref, *scratch):
        part = _dot(a_ref[...].astype(BF16), b_ref[...].astype(BF16), ca, cb)
        if nk == 1:
            o_ref[...] = part.astype(o_ref.dtype)
            return
        acc_ref, = scratch
        kk = pl.program_id(2)

        @pl.when(kk == 0)
        def _():
            acc_ref[...] = part

        @pl.when(kk > 0)
        def _():
            acc_ref[...] += part

        @pl.when(kk == nk - 1)
        def _():
            o_ref[...] = acc_ref[...].astype(o_ref.dtype)

    return pl.pallas_call(
        body, name=name, grid=(m // tm, n // tn, nk),
        in_specs=[a_spec, b_spec],
        out_specs=pl.BlockSpec((tm, tn), lambda i, j, kk: (i, j)),
        out_shape=jax.ShapeDtypeStruct((m, n), out_dtype),
        scratch_shapes=[] if nk == 1 else [pltpu.VMEM((tm, tn), F32)],
        compiler_params=_params(("parallel", "parallel", "arbitrary")),
    )(a, b)


def _ew(fn, ins, outs, *, rows, tr, name, n0=None, aliases=None):
    assert rows % tr == 0, (name, rows, tr)
    nt = rows // tr

    def grp(i):
        return 0 if n0 is None else jnp.where(i < n0, 0, 1)

    in_specs, args = [], []
    for spec in ins:
        if spec[0] == 't':
            arr, cb, w = spec[1], spec[2], spec[3]
            l = spec[4] if len(spec) > 4 else None
            if l is None:
                in_specs.append(pl.BlockSpec((tr, w), functools.partial(lambda i, cb: (i, cb), cb=cb)))
            else:
                in_specs.append(pl.BlockSpec((None, tr, w), functools.partial(lambda i, cb, l: (l, i, cb), cb=cb, l=l)))
            args.append(arr)
        else:
            arr = spec[1]
            g = arr.shape[0]
            if g == 1:
                in_specs.append(pl.BlockSpec((None, 1, arr.shape[2]), lambda i: (0, 0, 0)))
            else:
                in_specs.append(pl.BlockSpec((None, 1, arr.shape[2]), lambda i: (grp(i), 0, 0)))
            args.append(arr)
    out_specs, out_shapes, is_red = [], [], []
    for spec in outs:
        if spec[0] == 't':
            w, dt = spec[1], spec[2]
            if len(spec) > 3:
                l, nl = spec[3], spec[4]
                out_specs.append(pl.BlockSpec((None, tr, w), functools.partial(lambda i, l: (l, i, 0), l=l)))
                out_shapes.append(jax.ShapeDtypeStruct((nl, rows, w), dt))
            else:
                out_specs.append(pl.BlockSpec((tr, w), lambda i: (i, 0)))
                out_shapes.append(jax.ShapeDtypeStruct((rows, w), dt))
            is_red.append(False)
        else:
            w, g = spec[1], spec[2]
            if g == 1:
                out_specs.append(pl.BlockSpec((None, 1, w), lambda i: (0, 0, 0)))
            else:
                out_specs.append(pl.BlockSpec((None, 1, w), lambda i: (grp(i), 0, 0)))
            out_shapes.append(jax.ShapeDtypeStruct((g, 1, w), F32))
            is_red.append(True)
    n_in = len(ins)
    n_alias = 0 if aliases is None else len(aliases)

    def body(*refs):
        in_refs = refs[:n_in]
        out_refs = refs[n_in + n_alias:]
        res = fn(*[r[...] for r in in_refs])
        if not isinstance(res, (tuple, list)):
            res = (res,)
        i = pl.program_id(0)
        first = (i == 0) if n0 is None else ((i == 0) | (i == n0))
        for o_ref, val, red in zip(out_refs, res, is_red):
            if not red:
                o_ref[...] = val.astype(o_ref.dtype)
            else:
                @pl.when(first)
                def _(o_ref=o_ref, val=val):
                    o_ref[...] = val

                @pl.when(jnp.logical_not(first))
                def _(o_ref=o_ref, val=val):
                    o_ref[...] += val

    io_alias = {}
    if aliases is not None:
        for a_idx, (arr, o_idx) in enumerate(aliases):
            in_specs.append(ANY)
            args.append(arr)
            io_alias[n_in + a_idx] = o_idx
    has_red = any(is_red)
    return pl.pallas_call(
        body, name=name, grid=(nt,), in_specs=in_specs, out_specs=out_specs, out_shape=out_shapes,
        input_output_aliases=io_alias,
        compiler_params=_params(("arbitrary",) if has_red else ("parallel",)),
    )(*args)


def _half_tiles(pr, pw):
    tr, tc = _tile(pr, 256, 16), _tile(pw, 2048)
    return tr, tc, (pr // tr, pw // tc)


def _half_spec(tr, tc, ax, grid):
    if ax == 1:
        return pl.BlockSpec((tr, tc), lambda i, j, sel: (sel[0] * grid[0] + i, j))
    return pl.BlockSpec((tr, tc), lambda i, j, sel: (i, sel[0] * grid[1] + j))


def _sum_pair(g, theirs, ax, ci, *, name):
    pr, pw = theirs.shape
    tr, tc, grid = _half_tiles(pr, pw)

    def body(sel, a_ref, b_ref, o_ref):
        o_ref[...] = (a_ref[...].astype(F32) + b_ref[...].astype(F32)).astype(o_ref.dtype)

    tile = pl.BlockSpec((tr, tc), lambda i, j, sel: (i, j))
    return pl.pallas_call(
        body, name=name,
        grid_spec=pltpu.PrefetchScalarGridSpec(
            num_scalar_prefetch=1, grid=grid, in_specs=[_half_spec(tr, tc, ax, grid), tile], out_specs=tile),
        out_shape=jax.ShapeDtypeStruct((pr, pw), BF16),
        compiler_params=_params(("parallel", "parallel")),
    )(jnp.reshape(ci, (1,)).astype(jnp.int32), g, theirs)


def _sum_chips_into(own, recv, ax, ci, *, name):
    pr, pw = own.shape
    tr, tc, grid = _half_tiles(pr, pw)
    full_shape = (2 * pr, pw) if ax == 1 else (pr, 2 * pw)

    def body(sel, a_ref, r_ref, o_ref):
        acc = a_ref[...].astype(F32)
        for k in range(N_CHIPS - 1):
            acc = acc + r_ref[k].astype(F32)
        o_ref[...] = acc

    return pl.pallas_call(
        body, name=name,
        grid_spec=pltpu.PrefetchScalarGridSpec(
            num_scalar_prefetch=1, grid=grid,
            in_specs=[pl.BlockSpec((tr, tc), lambda i, j, sel: (i, j)),
                      pl.BlockSpec((N_CHIPS - 1, tr, tc), lambda i, j, sel: (0, i, j))],
            out_specs=_half_spec(tr, tc, ax, grid)),
        out_shape=jax.ShapeDtypeStruct(full_shape, F32),
        compiler_params=_params(("parallel", "parallel")),
    )(jnp.reshape(ci, (1,)).astype(jnp.int32), own, recv)


def _rsum(v):
    return jnp.sum(v, axis=0, keepdims=True)


def _silu_parts(z):
    sg = _sigmoid(z)
    return z * sg, sg * (1.0 + z * (1.0 - sg))


def _na_bias_table(rpb, rows, *, name):
    kh, kw = NA_WIN_ROWS, NA_WIN_COLS
    assert rows >= kh
    heads = rpb.shape[0]
    e1, e2 = _na_onehots()
    rpb16 = jnp.pad(rpb, ((0, 0), (0, 16 - rpb.shape[1]), (0, LANES - rpb.shape[2])))

    def body(r_ref, e1_ref, e2_ref, o_ref):
        e1b = e1_ref[...].astype(BF16)
        y = sum(_dot(e1b, part, 0, 0) for part in _split3(r_ref[...]))
        e2b = e2_ref[...].astype(BF16)
        o_ref[...] = sum(_dot(part, e2b, 1, 1) for part in _split3(y))

    z = pl.pallas_call(
        body, name=name, grid=(heads,),
        in_specs=[pl.BlockSpec((None, 16, LANES), lambda h: (h, 0, 0)),
                  pl.BlockSpec(e1.shape, lambda h: (0, 0)), pl.BlockSpec(e2.shape, lambda h: (0, 0))],
        out_specs=pl.BlockSpec((None, kh * kh, GRID_W * GRID_W), lambda h: (h, 0, 0)),
        out_shape=jax.ShapeDtypeStruct((heads, kh * kh, GRID_W * GRID_W), F32),
        compiler_params=_params(("parallel",)),
    )(rpb16, e1, e2)
    return z


def _na_bias_layout(z):
    heads = z.shape[0]
    kh, kw = NA_WIN_ROWS, NA_WIN_COLS
    cidx = np.arange(GRID_W)
    c0 = np.clip(cidx - kw // 2, 0, GRID_W - kw)
    col_in = (cidx[None, :] >= c0[:, None]) & (cidx[None, :] < c0[:, None] + kw)
    bias = z.reshape(heads, kh, kh, GRID_W, GRID_W).transpose(0, 1, 3, 2, 4)
    bias = jnp.where(col_in[None, None, :, None, :], bias, MASK_VALUE)
    return bias.reshape(heads, kh, GRID_W, kh * GRID_W)


def _na_onehots():
    kh, kw = NA_WIN_ROWS, NA_WIN_COLS
    cidx = np.arange(GRID_W)
    dc = cidx[None, :] - cidx[:, None] + (kw - 1)
    e2 = np.zeros((GRID_W * GRID_W, LANES), np.float32)
    ok = (dc >= 0) & (dc <= 2 * kw - 2)
    cq, ck = np.nonzero(ok)
    e2[cq * GRID_W + ck, dc[cq, ck]] = 1.0
    dr = np.arange(kh)[None, :] - np.arange(kh)[:, None] + (kh - 1)
    e1 = np.zeros((16, kh * kh), np.float32)
    dl, kr = np.nonzero(np.ones_like(dr))
    e1[dr[dl, kr], dl * kh + kr] = 1.0
    return jnp.asarray(e1), jnp.asarray(e2)


def _na_fwd(u, bias, *, s_len, heads, name):
    t_len = u.shape[0]
    rows = s_len // GRID_W
    nloc = NA_WIN_ROWS * GRID_W
    scale = NA_HEAD_DIM ** -0.5
    hd = NA_HEAD_DIM

    def body(q_ref, k_ref, v_ref, b_ref, o_ref):
        kc = k_ref[s_len:t_len, :]
        vc = v_ref[s_len:t_len, :]

        def group(g, carry):
            rs = [g * NA_GROUP + i for i in range(NA_GROUP)]
            r0s = [jnp.clip(r - NA_WIN_ROWS // 2, 0, rows - NA_WIN_ROWS) for r in rs]
            gs_ = pl.multiple_of(g * (NA_GROUP * GRID_W), NA_GROUP * GRID_W)
            kss = [pl.multiple_of(r0 * GRID_W, GRID_W) for r0 in r0s]
            q_all = q_ref[pl.ds(gs_, NA_GROUP * GRID_W), :]
            s_ctx = _dot(q_all, kc, 1, 1) * scale
            s_loc = [_dot(q_all[i * GRID_W:(i + 1) * GRID_W], k_ref[pl.ds(kss[i], nloc), :], 1, 1) * scale + b_ref[rs[i] - r0s[i]]
                     for i in range(NA_GROUP)]
            p_loc, p_ctx, inv = [], [], []
            for i in range(NA_GROUP):
                sc = s_ctx[i * GRID_W:(i + 1) * GRID_W]
                m = jnp.maximum(jnp.max(s_loc[i], axis=-1, keepdims=True), jnp.max(sc, axis=-1, keepdims=True))
                pl_, pc_ = jnp.exp(s_loc[i] - m), jnp.exp(sc - m)
                inv.append(1.0 / (jnp.sum(pl_, axis=-1, keepdims=True) + jnp.sum(pc_, axis=-1, keepdims=True)))
                p_loc.append(pl_.astype(BF16))
                p_ctx.append(pc_.astype(BF16))
            o_ctx = _dot(jnp.concatenate(p_ctx, axis=0), vc, 1, 0)
            o_loc = [_dot(p_loc[i], v_ref[pl.ds(kss[i], nloc), :], 1, 0) for i in range(NA_GROUP)]
            out = jnp.concatenate([(o_loc[i] + o_ctx[i * GRID_W:(i + 1) * GRID_W]) * inv[i] for i in range(NA_GROUP)], axis=0)
            o_ref[pl.ds(gs_, NA_GROUP * GRID_W), :] = out.astype(o_ref.dtype)
            return carry

        lax.fori_loop(0, rows // NA_GROUP, group, 0)
        qc = q_ref[s_len:t_len, :]
        s = _dot(qc, kc, 1, 1) * scale
        p = jnp.exp(s - jnp.max(s, axis=-1, keepdims=True))
        o = _dot(p.astype(BF16), vc, 1, 0) / jnp.sum(p, axis=-1, keepdims=True)
        o_ref[s_len:t_len, :] = o.astype(o_ref.dtype)

    col = lambda off: pl.BlockSpec((t_len, hd), functools.partial(lambda h, off: (0, off + h), off=off))
    return pl.pallas_call(
        body, name=name, grid=(heads,),
        in_specs=[col(0), col(heads), col(2 * heads),
                  pl.BlockSpec((None, NA_WIN_ROWS, GRID_W, nloc), lambda h: (h, 0, 0, 0))],
        out_specs=pl.BlockSpec((t_len, hd), lambda h: (0, h)),
        out_shape=jax.ShapeDtypeStruct((t_len, heads * hd), BF16),
        compiler_params=_params(("parallel",)),
    )(u, u, u, bias)


def _na_bwd(u, bias, o, do, *, s_len, heads, name):
    t_len = u.shape[0]
    rows = s_len // GRID_W
    nloc = NA_WIN_ROWS * GRID_W
    scale = NA_HEAD_DIM ** -0.5
    hd = NA_HEAD_DIM

    def body(q_ref, k_ref, v_ref, b_ref, o_ref, do_ref, dq_ref, dk_ref, dv_ref, db_ref, dk_acc, dv_acc):
        kc = k_ref[s_len:t_len, :]
        vc = v_ref[s_len:t_len, :]
        dk_acc[...] = jnp.zeros_like(dk_acc)
        dv_acc[...] = jnp.zeros_like(dv_acc)
        db_ref[...] = jnp.zeros_like(db_ref)

        def group(g, carry):
            n_g, rw = NA_GROUP, GRID_W
            rs = [g * n_g + i for i in range(n_g)]
            r0s = [jnp.clip(r - NA_WIN_ROWS // 2, 0, rows - NA_WIN_ROWS) for r in rs]
            dls = [r - r0 for r, r0 in zip(rs, r0s)]
            gs_ = pl.ds(pl.multiple_of(g * (n_g * rw), n_g * rw), n_g * rw)
            kss = [pl.ds(pl.multiple_of(r0 * rw, rw), nloc) for r0 in r0s]
            row_of = lambda a, i: a[i * rw:(i + 1) * rw]
            q_all, do_all = q_ref[gs_, :], do_ref[gs_, :]
            dlt_all = jnp.sum(do_all.astype(F32) * o_ref[gs_, :].astype(F32), axis=-1, keepdims=True)
            s_ctx = _dot(q_all, kc, 1, 1) * scale
            dp_ctx = _dot(do_all, vc, 1, 1)
            s_loc = [_dot(row_of(q_all, i), k_ref[kss[i], :], 1, 1) * scale + b_ref[dls[i]] for i in range(n_g)]
            dp_loc = [_dot(row_of(do_all, i), v_ref[kss[i], :], 1, 1) for i in range(n_g)]
            p_loc_b, ds_loc_b, p_ctx_b, ds_ctx_b = [], [], [], []
            for i in range(n_g):
                sc, dlt = row_of(s_ctx, i), row_of(dlt_all, i)
                m = jnp.maximum(jnp.max(s_loc[i], axis=-1, keepdims=True), jnp.max(sc, axis=-1, keepdims=True))
                pl_, pc_ = jnp.exp(s_loc[i] - m), jnp.exp(sc - m)
                inv = 1.0 / (jnp.sum(pl_, axis=-1, keepdims=True) + jnp.sum(pc_, axis=-1, keepdims=True))
                pl_, pc_ = pl_ * inv, pc_ * inv
                ds_l = pl_ * (dp_loc[i] - dlt)
                db_ref[dls[i]] += ds_l
                p_loc_b.append(pl_.astype(BF16))
                ds_loc_b.append(ds_l.astype(BF16))
                p_ctx_b.append(pc_.astype(BF16))
                ds_ctx_b.append((pc_ * (row_of(dp_ctx, i) - dlt)).astype(BF16))
            p_ctx_all, ds_ctx_all = jnp.concatenate(p_ctx_b, axis=0), jnp.concatenate(ds_ctx_b, axis=0)
            dq_ctx = _dot(ds_ctx_all, kc, 1, 0)
            dq_loc = [_dot(ds_loc_b[i], k_ref[kss[i], :], 1, 0) for i in range(n_g)]
            dk_loc = [_dot(ds_loc_b[i], row_of(q_all, i), 0, 0) for i in range(n_g)]
            dv_loc = [_dot(p_loc_b[i], row_of(do_all, i), 0, 0) for i in range(n_g)]
            dk_ctx = _dot(ds_ctx_all, q_all, 0, 0)
            dv_ctx = _dot(p_ctx_all, do_all, 0, 0)
            dq_ref[gs_, :] = ((jnp.concatenate(dq_loc, axis=0) + dq_ctx) * scale).astype(dq_ref.dtype)
            for i in range(n_g):
                dk_acc[kss[i], :] += dk_loc[i] * scale
                dv_acc[kss[i], :] += dv_loc[i]
            dk_acc[s_len:t_len, :] += dk_ctx * scale
            dv_acc[s_len:t_len, :] += dv_ctx
            return carry

        lax.fori_loop(0, rows // NA_GROUP, group, 0)
        qc = q_ref[s_len:t_len, :]
        dout = do_ref[s_len:t_len, :]
        out = o_ref[s_len:t_len, :]
        s = _dot(qc, kc, 1, 1) * scale
        p = jnp.exp(s - jnp.max(s, axis=-1, keepdims=True))
        p = p / jnp.sum(p, axis=-1, keepdims=True)
        dlt = jnp.sum(dout.astype(F32) * out.astype(F32), axis=-1, keepdims=True)
        ds = (p * (_dot(dout, vc, 1, 1) - dlt)).astype(BF16)
        dq_ref[s_len:t_len, :] = (_dot(ds, kc, 1, 0) * scale).astype(dq_ref.dtype)
        dk_acc[s_len:t_len, :] += _dot(ds, qc, 0, 0) * scale
        dv_acc[s_len:t_len, :] += _dot(p.astype(BF16), dout, 0, 0)
        dk_ref[...] = dk_acc[...].astype(dk_ref.dtype)
        dv_ref[...] = dv_acc[...].astype(dv_ref.dtype)

    col = lambda off: pl.BlockSpec((t_len, hd), functools.partial(lambda h, off: (0, off + h), off=off))
    tbl = pl.BlockSpec((None, NA_WIN_ROWS, GRID_W, nloc), lambda h: (h, 0, 0, 0))
    tok = jax.ShapeDtypeStruct((t_len, heads * hd), BF16)
    return pl.pallas_call(
        body, name=name, grid=(heads,),
        in_specs=[col(0), col(heads), col(2 * heads), tbl, col(0), col(0)],
        out_specs=[col(0), col(0), col(0), tbl],
        out_shape=[tok, tok, tok, jax.ShapeDtypeStruct(bias.shape, F32)],
        scratch_shapes=[pltpu.VMEM((t_len, hd), F32), pltpu.VMEM((t_len, hd), F32)],
        compiler_params=_params(("parallel",)),
    )(u, u, u, bias, o, do)


def _split3(x):
    hi = x.astype(BF16)
    r1 = x - hi.astype(F32)
    mid = r1.astype(BF16)
    lo = (r1 - mid.astype(F32)).astype(BF16)
    return hi, mid, lo


def _rpb_grad(dbias, *, name):
    heads = dbias.shape[0]
    kh = NA_WIN_ROWS
    e1, e2 = _na_onehots()
    x = dbias.reshape(heads, kh, GRID_W, kh, GRID_W).transpose(0, 1, 3, 2, 4).reshape(heads, kh * kh, GRID_W * GRID_W)

    def body(x_ref, e1_ref, e2_ref, o_ref):
        e2b = e2_ref[...].astype(BF16)
        y = sum(_dot(part, e2b, 1, 0) for part in _split3(x_ref[...]))
        e1b = e1_ref[...].astype(BF16)
        o_ref[...] = sum(_dot(e1b, part, 1, 0) for part in _split3(y))

    out = pl.pallas_call(
        body, name=name, grid=(heads,),
        in_specs=[pl.BlockSpec((None, kh * kh, GRID_W * GRID_W), lambda h: (h, 0, 0)),
                  pl.BlockSpec(e1.shape, lambda h: (0, 0)), pl.BlockSpec(e2.shape, lambda h: (0, 0))],
        out_specs=pl.BlockSpec((None, 16, LANES), lambda h: (h, 0, 0)),
        out_shape=jax.ShapeDtypeStruct((heads, 16, LANES), F32),
        compiler_params=_params(("parallel",)),
    )(x, e1, e2)
    return out[:, :2 * kh - 1, :2 * NA_WIN_COLS - 1]


def _rope_tables(s_len, l_len):
    nf = RET_KEY_DIM // 4
    t = np.arange(s_len)
    row = (t // GRID_W).astype(np.float32)
    colp = (t % GRID_W).astype(np.float32)
    inv_freq = jnp.asarray(ROPE_BASE, F32) ** (-jnp.arange(nf, dtype=F32) / nf)
    ang = jnp.concatenate([jnp.asarray(row)[:, None] * inv_freq, jnp.asarray(colp)[:, None] * inv_freq], axis=-1)
    cos, sin = jnp.cos(ang), jnp.sin(ang)
    c2 = jnp.concatenate([cos, cos], axis=-1)
    s2 = jnp.concatenate([-sin, sin], axis=-1)
    c2 = jnp.concatenate([c2, jnp.ones((l_len, RET_KEY_DIM), F32)], axis=0)
    s2 = jnp.concatenate([s2, jnp.zeros((l_len, RET_KEY_DIM), F32)], axis=0)
    return c2, s2


def _rope(x, c2, s2):
    return x * c2 + pltpu.roll(x, RET_KEY_DIM // 2, 1) * s2


def _rope_t(d, c2, s2):
    return d * c2 + pltpu.roll(d * s2, RET_KEY_DIM // 2, 1)


def _ret_decays(lg, direction):
    cs = RET_CHUNK
    i_col = lax.broadcasted_iota(jnp.int32, (cs, 1), 0)
    p_col = jnp.where(direction == 0, i_col, cs - 1 - i_col).astype(F32)
    pi = lax.broadcasted_iota(jnp.int32, (cs, cs), 0)
    pj = lax.broadcasted_iota(jnp.int32, (cs, cs), 1)
    diff = jnp.where(direction == 0, pi - pj, pj - pi).astype(F32)
    dm = jnp.where(diff >= 0, jnp.exp(jnp.maximum(diff, 0.0) * lg), 0.0)
    qdec = jnp.exp((p_col + 1.0) * lg)
    kdec = jnp.exp((cs - 1.0 - p_col) * lg)
    cd = jnp.exp(jnp.full((1, 1), cs, F32) * lg)
    return p_col, dm, qdec, kdec, cd


def _ret_chunk_index(t, direction, n_chunks, lat_chunks):
    return jnp.where(direction == 0, lax.rem(t + lat_chunks, n_chunks), n_chunks - 1 - t)


def _ret_fwd(u, c2, s2, lg, *, s_len, heads, q_off, name):
    t_len = u.shape[0]
    cs, dk, dv = RET_CHUNK, RET_KEY_DIM, RET_VAL_DIM
    n_chunks, lat_chunks = t_len // cs, s_len // cs
    k_scale = dk ** -0.5
    qb, kb, vb = q_off // dk, q_off // dk + heads, (q_off + 2 * heads * dk) // dv

    def body(lg_ref, q_ref, k_ref, v_ref, c_ref, s_ref, o_ref, st_ref, qd_s, kv_s):
        h, d = pl.program_id(0), pl.program_id(1)
        _, dm, qdec, kdec, cd = _ret_decays(lg_ref[d, h], d)
        n_g = max(g for g in RET_GROUPS if n_chunks % g == 0)
        rows_of = lambda c: pl.ds(pl.multiple_of(c * cs, cs), cs)

        def local(gi, carry):
            rws = [rows_of(gi * n_g + j) for j in range(n_g)]
            qcs = [_rope(q_ref[r, :].astype(F32), c_ref[r, :], s_ref[r, :]) for r in rws]
            kcs = [_rope(k_ref[r, :].astype(F32), c_ref[r, :], s_ref[r, :]) * k_scale for r in rws]
            vcs = [v_ref[r, :] for r in rws]
            a_raw = [_dot(qcs[j].astype(BF16), kcs[j].astype(BF16), 1, 1) for j in range(n_g)]
            kv = [_dot((kcs[j] * kdec).astype(BF16), vcs[j], 0, 0) for j in range(n_g)]
            inner = [_dot((a_raw[j] * dm).astype(BF16), vcs[j], 1, 0) for j in range(n_g)]
            for j in range(n_g):
                qd_s[rws[j], :] = (qcs[j] * qdec).astype(BF16)
                kv_s[gi * n_g + j] = kv[j]

            @pl.when(d == 0)
            def _():
                for j in range(n_g):
                    o_ref[rws[j], :] = inner[j]

            @pl.when(d == 1)
            def _():
                for j in range(n_g):
                    o_ref[rws[j], :] += inner[j]

            return carry

        lax.fori_loop(0, n_chunks // n_g, local, 0)

        def scan(t, st):
            st_ref[t] = st
            return st * cd + kv_s[_ret_chunk_index(t, d, n_chunks, lat_chunks)]

        lax.fori_loop(0, n_chunks, scan, jnp.zeros((dk, dv), F32))

        def cross(gi, carry):
            ts = [gi * n_g + j for j in range(n_g)]
            rws = [rows_of(_ret_chunk_index(t, d, n_chunks, lat_chunks)) for t in ts]
            outs = [_dot(qd_s[rws[j], :], st_ref[ts[j]].astype(BF16), 1, 0) for j in range(n_g)]
            for j in range(n_g):
                o_ref[rws[j], :] += outs[j]
            return carry

        lax.fori_loop(0, n_chunks // n_g, cross, 0)

    return pl.pallas_call(
        body, name=name, grid=(heads, 2),
        in_specs=[pl.BlockSpec(memory_space=pltpu.SMEM),
                  pl.BlockSpec((t_len, dk), lambda h, d: (0, qb + h)),
                  pl.BlockSpec((t_len, dk), lambda h, d: (0, kb + h)),
                  pl.BlockSpec((t_len, dv), lambda h, d: (0, vb + h)),
                  pl.BlockSpec((t_len, dk), lambda h, d: (0, 0)),
                  pl.BlockSpec((t_len, dk), lambda h, d: (0, 0))],
        out_specs=[pl.BlockSpec((t_len, dv), lambda h, d: (0, h)),
                   pl.BlockSpec((None, None, n_chunks, dk, dv), lambda h, d: (h, d, 0, 0, 0))],
        out_shape=[jax.ShapeDtypeStruct((t_len, heads * dv), F32),
                   jax.ShapeDtypeStruct((heads, 2, n_chunks, dk, dv), F32)],
        scratch_shapes=[pltpu.VMEM((t_len, dk), BF16), pltpu.VMEM((n_chunks, dk, dv), F32)],
        compiler_params=_params(("parallel", "arbitrary")),
    )(lg, u, u, u, c2, s2)


def _ret_bwd(u, c2, s2, lg, states, do, *, s_len, heads, q_off, name):
    t_len = u.shape[0]
    cs, dk, dv = RET_CHUNK, RET_KEY_DIM, RET_VAL_DIM
    n_chunks, lat_chunks = t_len // cs, s_len // cs
    k_scale = dk ** -0.5
    qb, kb, vb = q_off // dk, q_off // dk + heads, (q_off + 2 * heads * dk) // dv

    def body(lg_ref, q_ref, k_ref, v_ref, c_ref, s_ref, st_ref, do_ref, dq_ref, dk_ref, dv_ref, dlg_ref, acc, qdo_s, dst_s):
        h, d = pl.program_id(0), pl.program_id(1)
        p_col, dm, qdec, kdec, cd = _ret_decays(lg_ref[d, h], d)
        acc[...] = jnp.zeros_like(acc)
        n_g = max(g for g in RET_GROUPS[:2] if n_chunks % g == 0)
        rows_of = lambda c: pl.ds(pl.multiple_of(c * cs, cs), cs)
        chunk_of = lambda t: _ret_chunk_index(t, d, n_chunks, lat_chunks)

        def local(gi, carry):
            rws = [rows_of(gi * n_g + j) for j in range(n_g)]
            qds = [(_rope(q_ref[r, :].astype(F32), c_ref[r, :], s_ref[r, :]) * qdec).astype(BF16) for r in rws]
            prods = [_dot(qds[j], do_ref[rws[j], :].astype(BF16), 0, 0) for j in range(n_g)]
            for j in range(n_g):
                qdo_s[gi * n_g + j] = prods[j]
            return carry

        lax.fori_loop(0, n_chunks // n_g, local, 0)

        def scan(i, dst):
            t = n_chunks - 1 - i
            dst_s[t] = dst
            return dst * cd + qdo_s[chunk_of(t)]

        lax.fori_loop(0, n_chunks, scan, jnp.zeros((dk, dv), F32))

        def grads(gi, carry):
            ts = [gi * n_g + j for j in range(n_g)]
            rws = [rows_of(chunk_of(t)) for t in ts]
            ccs, sss = [c_ref[r, :] for r in rws], [s_ref[r, :] for r in rws]
            qcs = [_rope(q_ref[r, :].astype(F32), cc, ss) for r, cc, ss in zip(rws, ccs, sss)]
            kcs = [_rope(k_ref[r, :].astype(F32), cc, ss) * k_scale for r, cc, ss in zip(rws, ccs, sss)]
            vcs = [v_ref[r, :] for r in rws]
            docs = [do_ref[r, :].astype(BF16) for r in rws]
            sts = [st_ref[t] for t in ts]
            dsts = [dst_s[t] for t in ts]
            q16 = [x.astype(BF16) for x in qcs]
            k16 = [x.astype(BF16) for x in kcs]
            dst16 = [x.astype(BF16) for x in dsts]
            rng = range(n_g)
            a_raw = [_dot(q16[j], k16[j], 1, 1) for j in rng]
            da_raw = [_dot(docs[j], vcs[j], 1, 1) for j in rng]
            dq_c = [_dot(docs[j], sts[j].astype(BF16), 1, 1) * qdec for j in rng]
            dv_s = [_dot((kcs[j] * kdec).astype(BF16), dst16[j], 1, 0) for j in rng]
            dk_s = [_dot(vcs[j], dst16[j], 1, 1) * kdec for j in rng]
            a16 = [(a_raw[j] * dm).astype(BF16) for j in rng]
            dam = [(da_raw[j] * dm).astype(BF16) for j in rng]
            dq_i = [_dot(dam[j], k16[j], 1, 0) for j in rng]
            dk_i = [_dot(dam[j], q16[j], 0, 0) for j in rng]
            dv_i = [_dot(a16[j], docs[j], 0, 0) for j in rng]
            for j in rng:
                g = (jnp.sum(qcs[j] * (p_col * dq_i[j] + (p_col + 1.0) * dq_c[j]), axis=-1, keepdims=True)
                     + jnp.sum(kcs[j] * ((cs - 1.0 - p_col) * dk_s[j] - p_col * dk_i[j]), axis=-1, keepdims=True))
                g = (jnp.sum(g, axis=0, keepdims=True)
                     + cs * cd * jnp.sum(jnp.sum(dsts[j] * sts[j], axis=-1, keepdims=True), axis=0, keepdims=True))
                acc[...] += jnp.broadcast_to(g, acc.shape)
            dqs = [_rope_t(dq_i[j] + dq_c[j], ccs[j], sss[j]) for j in rng]
            dks = [_rope_t((dk_i[j] + dk_s[j]) * k_scale, ccs[j], sss[j]) for j in rng]
            dvs = [dv_i[j] + dv_s[j] for j in rng]

            @pl.when(d == 0)
            def _():
                for j in rng:
                    dq_ref[rws[j], :] = dqs[j].astype(dq_ref.dtype)
                    dk_ref[rws[j], :] = dks[j].astype(dk_ref.dtype)
                    dv_ref[rws[j], :] = dvs[j].astype(dv_ref.dtype)

            @pl.when(d == 1)
            def _():
                for j in rng:
                    dq_ref[rws[j], :] = (dq_ref[rws[j], :].astype(F32) + dqs[j]).astype(dq_ref.dtype)
                    dk_ref[rws[j], :] = (dk_ref[rws[j], :].astype(F32) + dks[j]).astype(dk_ref.dtype)
                    dv_ref[rws[j], :] = (dv_ref[rws[j], :].astype(F32) + dvs[j]).astype(dv_ref.dtype)

            return carry

        lax.fori_loop(0, n_chunks // n_g, grads, 0)
        dlg_ref[...] = acc[...]

    return pl.pallas_call(
        body, name=name, grid=(heads, 2),
        in_specs=[pl.BlockSpec(memory_space=pltpu.SMEM),
                  pl.BlockSpec((t_len, dk), lambda h, d: (0, qb + h)),
                  pl.BlockSpec((t_len, dk), lambda h, d: (0, kb + h)),
                  pl.BlockSpec((t_len, dv), lambda h, d: (0, vb + h)),
                  pl.BlockSpec((t_len, dk), lambda h, d: (0, 0)),
                  pl.BlockSpec((t_len, dk), lambda h, d: (0, 0)),
                  pl.BlockSpec((None, None, n_chunks, dk, dv), lambda h, d: (h, d, 0, 0, 0)),
                  pl.BlockSpec((t_len, dv), lambda h, d: (0, h))],
        out_specs=[pl.BlockSpec((t_len, dk), lambda h, d: (0, h)),
                   pl.BlockSpec((t_len, dk), lambda h, d: (0, h)),
                   pl.BlockSpec((t_len, dv), lambda h, d: (0, h)),
                   pl.BlockSpec((None, None, 8, LANES), lambda h, d: (h, d, 0, 0))],
        out_shape=[jax.ShapeDtypeStruct((t_len, heads * dk), BF16),
                   jax.ShapeDtypeStruct((t_len, heads * dk), BF16),
                   jax.ShapeDtypeStruct((t_len, heads * dv), BF16),
                   jax.ShapeDtypeStruct((heads, 2, 8, LANES), F32)],
        scratch_shapes=[pltpu.VMEM((8, LANES), F32), pltpu.VMEM((n_chunks, dk, dv), F32), pltpu.VMEM((n_chunks, dk, dv), F32)],
        compiler_params=_params(("parallel", "arbitrary")),
    )(lg, u, u, u, c2, s2, states, do)


def _mesh_pos():
    return lax.axis_index("x"), lax.axis_index("y"), lax.axis_index("c")


def _all_gather_small(buf, *, name):
    r = buf.shape[0]

    def body(x_ref, o_ref, send_sems, recv_sems, local_sem):
        x, y, c = _mesh_pos()
        me = 4 * x + 2 * y + c
        mine = pltpu.make_async_copy(x_ref, o_ref.at[me], local_sem)
        mine.start()
        copies = []
        for k in range(1, N_DEV):
            px, py, pc = x ^ ((k >> 2) & 1), y ^ ((k >> 1) & 1), c ^ (k & 1)
            cp = pltpu.make_async_remote_copy(
                src_ref=x_ref, dst_ref=o_ref.at[me], send_sem=send_sems.at[k - 1], recv_sem=recv_sems.at[k - 1],
                device_id=(px, py, pc), device_id_type=MESH)
            cp.start()
            copies.append((cp, 4 * px + 2 * py + pc))
        for k, (cp, peer) in enumerate(copies):
            pltpu.make_async_remote_copy(
                src_ref=x_ref, dst_ref=o_ref.at[peer], send_sem=send_sems.at[k], recv_sem=recv_sems.at[k],
                device_id=(x, y, c), device_id_type=MESH).wait_recv()
        for cp, _ in copies:
            cp.wait_send()
        mine.wait()

    return pl.pallas_call(
        body, name=name,
        in_specs=[pl.BlockSpec(memory_space=pltpu.VMEM)],
        out_specs=pl.BlockSpec(memory_space=pltpu.VMEM),
        out_shape=jax.ShapeDtypeStruct((N_DEV, r, LANES), F32),
        scratch_shapes=[pltpu.SemaphoreType.DMA((N_DEV - 1,)), pltpu.SemaphoreType.DMA((N_DEV - 1,)),
                        pltpu.SemaphoreType.DMA],
        compiler_params=pltpu.CompilerParams(vmem_limit_bytes=VMEM_LIMIT),
    )(buf)


def _cut(ref, shard_axis, *, chip=None, half=None, lead=None):
    shape = ref.shape[1:] if lead is not None else ref.shape
    idx = [slice(None), slice(None)]
    if chip is not None:
        w = shape[shard_axis] // N_CHIPS
        idx[shard_axis] = pl.ds(pl.multiple_of(chip * w, w), w)
    if half is not None:
        hw = shape[1 - shard_axis] // 2
        idx[1 - shard_axis] = pl.ds(pl.multiple_of(half * hw, hw), hw)
    if lead is not None:
        idx = [lead] + idx
    return ref.at[tuple(idx)]


def _wait_recv(ref, send_sem, recv_sem):
    pltpu.make_async_remote_copy(src_ref=ref, dst_ref=ref, send_sem=send_sem, recv_sem=recv_sem,
                                 device_id=_mesh_pos(), device_id_type=MESH).wait_recv()


def _gather_plan(axes):
    def plan(srcs, lands, send_sems, recv_sems):
        x, y, c = _mesh_pos()
        chip = 2 * x + y
        copies = []
        for i, ax in enumerate(axes):
            for k in range(1, N_CHIPS):
                px, py = x ^ (k >> 1), y ^ (k & 1)
                mine = _cut(lands[i], ax, chip=chip, half=c)
                j = i * (N_CHIPS - 1) + k - 1
                sems = dict(send_sem=send_sems.at[j], recv_sem=recv_sems.at[j], device_id=(px, py, c), device_id_type=MESH)
                send = pltpu.make_async_remote_copy(src_ref=mine, dst_ref=mine, **sems)
                recv = pltpu.make_async_remote_copy(src_ref=mine, dst_ref=_cut(lands[i], ax, chip=2 * px + py, half=c), **sems)
                copies.append((send, recv))
        return copies
    return plan


def _gather_near_plan(axes):
    def plan(srcs, lands, send_sems, recv_sems):
        x, y, c = _mesh_pos()
        copies = []
        for i, ax in enumerate(axes):
            mine = _cut(lands[i], ax, chip=2 * x + y, half=c)
            for k, (px, py) in enumerate(((1 - x, y), (x, 1 - y))):
                sems = dict(send_sem=send_sems.at[2 * i + k], recv_sem=recv_sems.at[2 * i + k], device_id=(px, py, c), device_id_type=MESH)
                send = pltpu.make_async_remote_copy(src_ref=mine, dst_ref=mine, **sems)
                recv = pltpu.make_async_remote_copy(src_ref=mine, dst_ref=_cut(lands[i], ax, chip=2 * px + py, half=c), **sems)
                copies.append((send, recv))
        return copies
    return plan


def _gather_far_plan(axes):
    def plan(srcs, lands, send_sems, recv_sems):
        x, y, c = _mesh_pos()
        from_chip = 2 * (x ^ (1 - c)) + (y ^ c)
        to = (x ^ c, y ^ (1 - c), c)
        diag = 2 * (1 - x) + (1 - y)
        copies = []
        for i, ax in enumerate(axes):
            passed = _cut(lands[i], ax, chip=from_chip, half=c)
            sems = dict(send_sem=send_sems.at[i], recv_sem=recv_sems.at[i], device_id=to, device_id_type=MESH)
            send = pltpu.make_async_remote_copy(src_ref=passed, dst_ref=passed, **sems)
            recv = pltpu.make_async_remote_copy(src_ref=passed, dst_ref=_cut(lands[i], ax, chip=diag, half=c), **sems)
            copies.append((send, recv))
        return copies
    return plan


def _forward_plan(axes, ks=(1, 2, 3)):
    def plan(srcs, lands, send_sems, recv_sems):
        x, y, c = _mesh_pos()
        copies = []
        for i, ax in enumerate(axes):
            for n, k in enumerate(ks):
                peer_chip = 2 * (x ^ (k >> 1)) + (y ^ (k & 1))
                j = i * len(ks) + n
                sems = dict(send_sem=send_sems.at[j], recv_sem=recv_sems.at[j], device_id=(x, y, 1 - c), device_id_type=MESH)
                landed = _cut(lands[i], ax, chip=peer_chip, half=c)
                send = pltpu.make_async_remote_copy(src_ref=landed, dst_ref=landed, **sems)
                recv = pltpu.make_async_remote_copy(src_ref=landed, dst_ref=_cut(lands[i], ax, chip=peer_chip, half=1 - c), **sems)
                copies.append((send, recv))
        return copies
    return plan


def _pair_plan(axes):
    def plan(srcs, lands, send_sems, recv_sems):
        x, y, c = _mesh_pos()
        copies = []
        for i, ax in enumerate(axes):
            cp = pltpu.make_async_remote_copy(
                src_ref=_cut(srcs[i], ax, half=1 - c), dst_ref=lands[i], send_sem=send_sems.at[i], recv_sem=recv_sems.at[i],
                device_id=(x, y, 1 - c), device_id_type=MESH)
            copies.append((cp, cp))
        return copies
    return plan


def _scatter_plan(axes):
    def plan(srcs, lands, send_sems, recv_sems):
        x, y, c = _mesh_pos()
        copies = []
        for i, ax in enumerate(axes):
            for k in range(1, N_CHIPS):
                px, py = x ^ (k >> 1), y ^ (k & 1)
                j = i * (N_CHIPS - 1) + k - 1
                cp = pltpu.make_async_remote_copy(
                    src_ref=_cut(srcs[i], ax, chip=2 * px + py), dst_ref=lands[i].at[k - 1],
                    send_sem=send_sems.at[j], recv_sem=recv_sems.at[j], device_id=(px, py, c), device_id_type=MESH)
                copies.append((cp, cp))
        return copies
    return plan


HBM = pl.BlockSpec(memory_space=pltpu.HBM)
SEM = pl.BlockSpec(memory_space=pltpu.SEMAPHORE)
EFFECT = pltpu.SideEffectType.DATAFLOW_SIDE_EFFECTING


def _in_hbm(arrays):
    return [pltpu.with_memory_space_constraint(a, pltpu.HBM) for a in arrays]


def _split_start(srcs, lands, plan, n_copies, *, name):
    bufs = list(srcs) + list(lands)
    ns, nb = len(srcs), len(bufs)

    def body(*refs):
        send_sems, recv_sems, token = refs[nb], refs[nb + 1], refs[-1]
        for send, _ in plan(refs[:ns], refs[ns:nb], send_sems, recv_sems):
            send.start()
        token[...] = jnp.zeros_like(token)

    sems = pltpu.SemaphoreType.DMA((n_copies,))
    res = pl.pallas_call(
        body, name=name, in_specs=[HBM] * nb,
        out_specs=[SEM, SEM] + [HBM] * nb + [pl.BlockSpec(memory_space=pltpu.VMEM)],
        out_shape=[sems, sems] + [pltpu.HBM(a.shape, a.dtype) for a in bufs] + [jax.ShapeDtypeStruct((8, LANES), F32)],
        input_output_aliases={j: 2 + j for j in range(nb)},
        compiler_params=pltpu.CompilerParams(has_side_effects=EFFECT),
    )(*_in_hbm(bufs))
    return res[0], res[1], res[2:2 + ns], res[2 + ns:2 + nb], res[-1]


def _split_wait(started, after, plan, *, name, with_srcs=False):
    send_sems, recv_sems, srcs, lands, _ = started
    bufs = list(srcs) + list(lands)
    ns, nb = len(srcs), len(bufs)

    def body(*refs):
        for send, recv in plan(refs[:ns], refs[ns:nb], refs[nb], refs[nb + 1]):
            send.wait_send()
            recv.wait_recv()

    res = pl.pallas_call(
        body, name=name, in_specs=[HBM] * nb + [SEM, SEM, ANY], out_specs=[HBM] * nb,
        out_shape=[pltpu.HBM(a.shape, a.dtype) for a in bufs],
        input_output_aliases={j: j for j in range(nb)},
        compiler_params=pltpu.CompilerParams(has_side_effects=EFFECT),
    )(*bufs, send_sems, recv_sems, after)
    return (res[:ns], res[ns:]) if with_srcs else res[ns:]


def _cast_into_full(w3, layer, ax, chip, *, after=None, name):
    _, r, wd = w3.shape
    tr = _rows_per_tile(r, wd, 4 << 20)
    nt = r // tr
    full_shape = (r, wd * N_CHIPS) if ax == 1 else (r * N_CHIPS, wd)
    out_map = (lambda i, ch: (i, ch[0])) if ax == 1 else (lambda i, ch: (ch[0] * nt + i, 0))
    zero = jnp.zeros((1, wd), F32) + (0.0 if after is None else after)

    def body(chip_ref, w_ref, z_ref, o_ref):
        o_ref[...] = (w_ref[...] + z_ref[...]).astype(o_ref.dtype)

    return pl.pallas_call(
        body, name=name,
        grid_spec=pltpu.PrefetchScalarGridSpec(
            num_scalar_prefetch=1, grid=(nt,),
            in_specs=[pl.BlockSpec((None, tr, wd), lambda i, ch: (layer, i, 0)), pl.BlockSpec((1, wd), lambda i, ch: (0, 0))],
            out_specs=pl.BlockSpec((tr, wd), out_map)),
        out_shape=jax.ShapeDtypeStruct(full_shape, BF16),
        compiler_params=_params(("parallel",)),
    )(jnp.reshape(chip, (1,)).astype(jnp.int32), w3, zero)


def _forward_halves(fulls, axes, *, name, ks=(1, 2, 3)):
    n = len(fulls)

    def body(*refs):
        bufs = refs[:n]
        send_sems, recv_sems = refs[2 * n:]
        x, y, c = _mesh_pos()
        sends = []
        for i in range(n):
            for k in ks:
                landed = _cut(bufs[i], axes[i], chip=2 * (x ^ (k >> 1)) + (y ^ (k & 1)), half=c)
                cp = pltpu.make_async_remote_copy(
                    src_ref=landed, dst_ref=landed, send_sem=send_sems.at[i, k - 1], recv_sem=recv_sems.at[i, k - 1],
                    device_id=(x, y, 1 - c), device_id_type=MESH)
                cp.start()
                sends.append(cp)
        for i in range(n):
            for k in ks:
                other = _cut(bufs[i], axes[i], chip=2 * (x ^ (k >> 1)) + (y ^ (k & 1)), half=1 - c)
                _wait_recv(other, send_sems.at[i, k - 1], recv_sems.at[i, k - 1])
        for cp in sends:
            cp.wait_send()

    pairs = pltpu.SemaphoreType.DMA((n, N_CHIPS - 1))
    return pl.pallas_call(
        body, name=name, in_specs=[ANY] * n, out_specs=[ANY] * n,
        out_shape=[jax.ShapeDtypeStruct(a.shape, a.dtype) for a in fulls],
        input_output_aliases={j: j for j in range(n)},
        scratch_shapes=[pairs, pairs],
    )(*fulls)


def _share_halves_in_place(bufs, axes, *, name):
    n = len(bufs)

    def body(*refs):
        ins = refs[:n]
        send_sems, recv_sems = refs[2 * n:]
        x, y, c = _mesh_pos()
        sends = []
        for i in range(n):
            mine = _cut(ins[i], axes[i], half=c)
            cp = pltpu.make_async_remote_copy(
                src_ref=mine, dst_ref=mine, send_sem=send_sems.at[i], recv_sem=recv_sems.at[i],
                device_id=(x, y, 1 - c), device_id_type=MESH)
            cp.start()
            sends.append(cp)
        for i in range(n):
            _wait_recv(_cut(ins[i], axes[i], half=1 - c), send_sems.at[i], recv_sems.at[i])
        for cp in sends:
            cp.wait_send()

    sems = pltpu.SemaphoreType.DMA((n,))
    return pl.pallas_call(
        body, name=name, in_specs=[ANY] * n, out_specs=[ANY] * n,
        out_shape=[jax.ShapeDtypeStruct(b.shape, b.dtype) for b in bufs],
        input_output_aliases={j: j for j in range(n)}, scratch_shapes=[sems, sems],
    )(*bufs)


def _adamw_math(w, g, m, v):
    m = ADAM_B1 * m + (1.0 - ADAM_B1) * g
    v = ADAM_B2 * v + (1.0 - ADAM_B2) * (g * g)
    m_hat = m / (1.0 - ADAM_B1 ** ADAM_STEP)
    v_hat = v / (1.0 - ADAM_B2 ** ADAM_STEP)
    delta = -ADAM_LR * (m_hat / (jnp.sqrt(v_hat) + ADAM_EPS) + ADAM_WD * w)
    return delta, m, v


def _adamw_layer(w3, m3, v3, p, q, layer, prev, *, name):
    nl, rows, width = w3.shape
    tr = _rows_per_tile(rows, width)

    def fn(*t):
        if q is None:
            w, m, v, g = t
        else:
            w, m, v, g, g2 = t
            g = g + g2
        delta, m, v = _adamw_math(w, g, m, v)
        return g, delta, m, v

    ins = [('t', w3, 0, width, layer), ('t', m3, 0, width, layer), ('t', v3, 0, width, layer), ('t', p, 0, width)]
    if q is not None:
        ins.append(('t', q, 0, width))
    outs = [('t', width, F32, layer, nl)] * 4
    aliases = None if prev is None else [(prev[i], i) for i in range(4)]
    return _ew(fn, ins, outs, rows=rows, tr=tr, name=name, aliases=aliases)


def _pack_rows(vec):
    n = vec.shape[0]
    r = -(-n // (8 * LANES)) * 8
    return jnp.pad(vec, (0, r * LANES - n)).reshape(r, LANES)


def kernel(x, c, ctx, c_ctx, ada_w, ada_b, norm_g, w_in, na_rpb, ret_decay_logit, w_proj_na, w_proj_ret, w_out, final_g, loss_target, m_c_ctx, m_ada_w, m_ada_b, m_norm_g, m_w_in, m_na_rpb, m_ret_decay_logit, m_w_proj_na, m_w_proj_ret, m_w_out, m_final_g, v_c_ctx, v_ada_w, v_ada_b, v_norm_g, v_w_in, v_na_rpb, v_ret_decay_logit, v_w_proj_na, v_w_proj_ret, v_w_out, v_final_g):
    depth = w_in.shape[0]
    s_len, d_model = x.shape[1], x.shape[2]
    l_len = ctx.shape[1]
    t_len = s_len + l_len
    na_heads = na_rpb.shape[1]
    ret_heads = ret_decay_logit.shape[2]
    w_na = na_heads * NA_HEAD_DIM
    w_qk = ret_heads * RET_KEY_DIM
    w_v = ret_heads * RET_VAL_DIM
    in_cols = w_in.shape[2] * N_CHIPS
    assert in_cols == 4 * w_na + 2 * w_qk + 2 * w_v + 2 * d_model
    assert x.shape[0] == 1 and s_len % (NA_WIN_ROWS * GRID_W) == 0 and l_len % RET_CHUNK == 0
    off = np.cumsum([0, w_na, w_na, w_na, w_na, w_qk, w_qk, w_v, w_v, d_model, d_model])
    o_naz, o_retq, o_retz, o_gna, o_gret = int(off[3]), int(off[4]), int(off[7]), int(off[8]), int(off[9])
    rows = s_len // GRID_W
    tr = _tile(l_len, 256, 8)
    n0 = s_len // tr
    mod_cols = 3 * d_model
    mod_shard = ada_w.shape[2]

    xi, yi, ci = _mesh_pos()
    me = 4 * xi + 2 * yi + ci
    chip = 2 * xi + yi

    big_axes = [1, 1, 0, 0]
    n_big = len(big_axes) * (N_CHIPS - 1)
    gather_plan, scatter_plan = _gather_plan(big_axes), _scatter_plan(big_axes)

    c_silu = c[0] * _sigmoid(c[0])
    cc_silu = c_ctx * _sigmoid(c_ctx)
    c_all = _all_gather_small(_pack_rows(c_silu), name="gather_c")[:, :d_model // LANES].reshape(N_DEV, d_model)
    a_rows = jnp.concatenate([c_all, cc_silu[None], jnp.zeros((16 - N_DEV - 1, d_model), F32)], axis=0)
    mod_part = jnp.stack([_mm(a_rows, ada_w, b_lead=l, out_dtype=F32, name="ada_fwd_%d" % l) for l in range(depth)])
    mod_all = _all_gather_small(_pack_rows(mod_part.reshape(-1)), name="gather_mod")
    n_mod = depth * 16 * mod_shard
    mod_all = mod_all.reshape(N_DEV, -1)[:, :n_mod].reshape(N_CHIPS, 2, depth, 16, mod_shard)[:, 0]
    mod_all = jnp.transpose(mod_all, (1, 2, 0, 3)).reshape(depth, 16, mod_cols) + ada_b[:, None, :]

    big_named = list(zip((w_in, w_proj_na, w_proj_ret, w_out), big_axes, ("w_in", "w_proj_na", "w_proj_ret", "w_out")))
    w_in0 = _cast_into_full(w_in, 0, big_axes[0], chip, name="cast_w_in_0")
    mod_all, w_in0 = lax.optimization_barrier((mod_all, w_in0))
    plan_near, plan_far, plan_rest = _gather_near_plan(big_axes[:1]), _gather_far_plan(big_axes[:1]), _gather_plan(big_axes[1:])
    near_all, far_all, forward_all = _gather_near_plan(big_axes), _gather_far_plan(big_axes), _forward_plan(big_axes)
    first_gather = _split_start([], [w_in0], plan_near, 2, name="gather_start_0_in")
    start_token = first_gather[4][0, 0]
    fulls = [[None if (l == 0 and tag == "w_in") else _cast_into_full(w, l, ax, chip, after=start_token, name="cast_%s_%d" % (tag, l))
              for w, ax, tag in big_named] for l in range(depth)]
    mod_lat = lax.dynamic_index_in_dim(mod_all, me, axis=1, keepdims=False)
    mod_ctx = mod_all[:, N_DEV]
    biases = [_na_bias_layout(_na_bias_table(na_rpb[l], s_len // GRID_W, name="na_bias_%d" % l)) for l in range(depth)]
    biases, fulls = lax.optimization_barrier((biases, fulls))
    landed_near = _split_wait(first_gather, biases[-1], plan_near, name="gather_wait_0_in")
    passing = _split_start([], landed_near, plan_far, 1, name="gather_pass_0_in")
    forward_near = _forward_plan(big_axes[:1], ks=(1, 2))
    near_swap = _split_start([], passing[3], forward_near, 2, name="gather_forward_near_0_in")
    front_token = passing[4][0, 0] + near_swap[4][0, 0]

    c2, s2 = _rope_tables(s_len, l_len)
    log_gamma = jax.nn.log_sigmoid(ret_decay_logit)
    x_all = jnp.concatenate([x[0], ctx[0]], axis=0)

    def grp(lat_vec, ctx_vec):
        return jnp.stack([lat_vec, ctx_vec])[:, None, :]

    saved, full_w = [], []
    for l in range(depth):
        shift, scale, gate = [grp(mod_lat[l, i * d_model:(i + 1) * d_model], mod_ctx[l, i * d_model:(i + 1) * d_model])
                              for i in range(3)]
        gs = norm_g[l][None, None, :] * (1.0 + scale) + front_token

        def modnorm(xt, gs_t, sh_t):
            r = lax.rsqrt(jnp.mean(xt * xt, axis=-1, keepdims=True) + NORM_EPS)
            return xt * r * gs_t + sh_t

        h, = _ew(modnorm, [('t', x_all, 0, d_model), ('g', gs), ('g', shift)], [('t', d_model, BF16)],
                 rows=t_len, tr=tr, n0=n0, name="modnorm_%d" % l)
        bias = biases[l]
        if l == 0:
            h, bias = lax.optimization_barrier((h, bias))
            landed_far = _split_wait((passing[0], passing[1], [], near_swap[3], None), h, plan_far, name="gather_wait_0_in_far")
            landed_in = _split_wait((near_swap[0], near_swap[1], [], landed_far, None), h, forward_near,
                                    name="gather_forward_near_wait_0_in")
            landed_in, rest0, later = lax.optimization_barrier((landed_in, fulls[0][1:], fulls[1:]))
            rest_gather = _split_start([], rest0, plan_rest, n_big - (N_CHIPS - 1), name="gather_start_0_rest")
            later_gathers = [_split_start([], later[j], near_all, 2 * len(big_axes), name="gather_start_%d" % (j + 1))
                             for j in range(depth - 1)]
            win_f, = _forward_halves(landed_in, big_axes[:1], name="gather_forward_0_in", ks=(3,))
            win_f, tokens = lax.optimization_barrier((win_f, [rest_gather[4]] + [g[4] for g in later_gathers]))
            gate = gate + sum(t[0, 0] for t in tokens)
        else:
            h, bias = lax.optimization_barrier((h, bias))
            win_f, wpn_f, wpr_f, wout_f = _split_wait(next_forward, h, forward_all, name="gather_forward_wait_%d" % l)
        u = _mm(h, win_f, tm=1152, tn=1024, name="in_proj_%d" % l)
        o_na = _na_fwd(u, bias, s_len=s_len, heads=na_heads, name="na_fwd_%d" % l)
        o_ret, states = _ret_fwd(u, c2, s2, log_gamma[l], s_len=s_len, heads=ret_heads, q_off=o_retq, name="ret_fwd_%d" % l)

        def act(o1, z1, o2, z2):
            a1 = o1.astype(F32) * _silu_parts(z1.astype(F32))[0]
            sz = _silu_parts(z2.astype(F32))[0]
            outs = []
            for hh in range(ret_heads):
                sl = slice(hh * RET_VAL_DIM, (hh + 1) * RET_VAL_DIM)
                oh = o2[:, sl]
                r = lax.rsqrt(jnp.mean(oh * oh, axis=-1, keepdims=True) + NORM_EPS)
                outs.append(oh * r * sz[:, sl])
            return a1, jnp.concatenate(outs, axis=-1)

        a_na, a_ret = _ew(act, [('t', o_na, 0, w_na), ('t', u, o_naz // w_na, w_na), ('t', o_ret, 0, w_v), ('t', u, o_retz // w_v, w_v)],
                          [('t', w_na, BF16), ('t', w_v, BF16)], rows=t_len, tr=tr, name="act_%d" % l)
        if l == 0:
            landed_rest = _split_wait(rest_gather, a_na, plan_rest, name="gather_wait_0_rest")
            later_passes = [_split_start([], _split_wait(later_gathers[j], a_na, near_all, name="gather_near_%d" % (j + 1)),
                                         far_all, len(big_axes), name="gather_pass_%d" % (j + 1)) for j in range(depth - 1)]
            landed_rest, tokens = lax.optimization_barrier((landed_rest, [g[4] for g in later_passes]))
            gate = gate + sum(t[0, 0] for t in tokens)
            wpn_f, wpr_f, wout_f = _forward_halves(landed_rest, big_axes[1:], name="gather_forward_0_rest")
        full_w.append((win_f, wpn_f, wpr_f, wout_f))
        y_na = _mm(a_na, wpn_f, name="proj_na_%d" % l)
        y_ret = _mm(a_ret, wpr_f, name="proj_ret_%d" % l)

        def merge(y1, y2, g1, g2):
            return _sigmoid(g1.astype(F32)) * y1.astype(F32) + _sigmoid(g2.astype(F32)) * y2.astype(F32)

        merged, = _ew(merge, [('t', y_na, 0, d_model), ('t', y_ret, 0, d_model), ('t', u, o_gna // d_model, d_model), ('t', u, o_gret // d_model, d_model)],
                      [('t', d_model, BF16)], rows=t_len, tr=tr, name="merge_%d" % l)
        out = _mm(merged, wout_f, out_dtype=F32, name="out_proj_%d" % l)
        if l + 1 < depth:
            landed = _split_wait(later_passes[l], out, far_all, name="gather_wait_%d" % (l + 1))
            next_forward = _split_start([], landed, forward_all, n_big, name="gather_forward_%d" % (l + 1))
            gate = gate + next_forward[4][0, 0]
        x_new, = _ew(lambda xt, ot, gt: xt + gt * ot, [('t', x_all, 0, d_model), ('t', out, 0, d_model), ('g', gate)],
                     [('t', d_model, F32)], rows=t_len, tr=tr, n0=n0, name="resid_%d" % l)
        saved.append(dict(x=x_all, h=h, u=u, bias=bias, o_na=o_na, o_ret=o_ret, states=states, a_na=a_na, a_ret=a_ret,
                          y_na=y_na, y_ret=y_ret, merged=merged, out=out, gate=gate, gs=gs, scale=scale))
        x_all = x_new

    def final(xt, tt, gt):
        r = lax.rsqrt(jnp.mean(xt * xt, axis=-1, keepdims=True) + NORM_EPS)
        xh = xt * r
        e = xh * gt - tt
        dy = e * (1.0 / d_model)
        dyg = dy * gt
        dx = r * (dyg - xh * jnp.mean(dyg * xh, axis=-1, keepdims=True))
        return dx, _rsum(dy * xh), _rsum(e * e)

    dx_lat, d_final_g, loss_cols = _ew(final, [('t', x_all, 0, d_model), ('t', loss_target[0], 0, d_model), ('g', final_g[None, None, :])],
                                       [('t', d_model, F32), ('r', d_model, 1), ('r', d_model, 1)], rows=s_len, tr=tr, name="final")
    loss_part = (0.5 / d_model) * jnp.sum(loss_cols)
    dx_all = jnp.concatenate([dx_lat, jnp.zeros((l_len, d_model), F32)], axis=0)

    big_w = [(w_in, m_w_in, v_w_in), (w_proj_na, m_w_proj_na, v_w_proj_na), (w_proj_ret, m_w_proj_ret, v_w_proj_ret), (w_out, m_w_out, v_w_out)]
    big_res = [None] * 4
    scatters = {}
    back_token = jnp.zeros((), F32)

    pairs = {}

    def start_pair(key, grads, axes):
        plan = _pair_plan(axes)
        lands = []
        for g, ax in zip(grads, axes):
            shp = list(g.shape)
            shp[1 - ax] //= 2
            lands.append(lax.empty(tuple(shp), BF16))
        pairs[key] = (_split_start(grads, lands, plan, len(axes), name="pair_start_%s" % key), axes, plan)
        return pairs[key][0][4]

    def start_scatter(key, after):
        started, axes, pair_plan = pairs[key]
        grads, theirs = _split_wait(started, after, pair_plan, name="pair_wait_%s" % key, with_srcs=True)
        plan = _scatter_plan(axes)
        pair = [_sum_pair(g, t, ax, ci, name="sum_pair_%s_%d" % (key, i)) for i, (g, t, ax) in enumerate(zip(grads, theirs, axes))]
        own = [lax.dynamic_slice_in_dim(s, chip * (s.shape[ax] // N_CHIPS), s.shape[ax] // N_CHIPS, axis=ax) for s, ax in zip(pair, axes)]
        lands = [lax.empty((N_CHIPS - 1,) + o.shape, BF16) for o in own]
        started = _split_start(pair, lands, plan, len(axes) * (N_CHIPS - 1), name="scatter_start_%s" % key)
        scatters[key] = (started, own, axes, plan)
        return started[4]

    def finish_scatter(key, after):
        started, own, axes, plan = scatters[key]
        recv = _split_wait(started, after, plan, name="scatter_wait_%s" % key)
        bufs = [_sum_chips_into(own[i], rbuf, axes[i], ci, name="sum_chips_%s_%d" % (key, i)) for i, rbuf in enumerate(recv)]
        return _share_halves_in_place(bufs, axes, name="share_halves_%s" % key)

    def adamw_big(l, idx, grads, big_res):
        for i, g in zip(idx, grads):
            w3, m3, v3 = big_w[i]
            big_res[i] = _adamw_layer(w3, m3, v3, g, None, l, big_res[i], name="adamw_big_%d_%d" % (i, l))
        return big_res

    small = dict(dmod_lat=[None] * depth, dmod_ctx=[None] * depth, dnorm_g=[None] * depth, drpb=[None] * depth, ddecay=[None] * depth)
    for l in reversed(range(depth)):
        sv = saved[l]
        win_f, wpn_f, wpr_f, wout_f = full_w[l]

        def resid_bwd(dxt, ot, gt):
            return gt * dxt, _rsum(dxt * ot)

        dout, dgate = _ew(resid_bwd, [('t', dx_all, 0, d_model), ('t', sv['out'], 0, d_model), ('g', sv['gate'] + back_token)],
                          [('t', d_model, BF16), ('r', d_model, 2)], rows=t_len, tr=tr, n0=n0, name="resid_bwd_%d" % l)
        dmerged = _mm(dout, wout_f, tb=True, name="out_proj_dx_%d" % l)
        g_wout = _mm(sv['merged'], dout, ta=True, tm=1024, tk=t_len, name="out_proj_dw_%d" % l)

        def merge_bwd(dm, y1, y2, g1, g2):
            dm = dm.astype(F32)
            s1, s2_ = _sigmoid(g1.astype(F32)), _sigmoid(g2.astype(F32))
            return dm * s1, dm * s2_, dm * y1.astype(F32) * s1 * (1.0 - s1), dm * y2.astype(F32) * s2_ * (1.0 - s2_)

        u = sv['u']
        dy_na, dy_ret, dg_na, dg_ret = _ew(
            merge_bwd, [('t', dmerged, 0, d_model), ('t', sv['y_na'], 0, d_model), ('t', sv['y_ret'], 0, d_model),
                        ('t', u, o_gna // d_model, d_model), ('t', u, o_gret // d_model, d_model)],
            [('t', d_model, BF16)] * 4, rows=t_len, tr=tr, name="merge_bwd_%d" % l)
        da_na = _mm(dy_na, wpn_f, tb=True, name="proj_na_dx_%d" % l)
        g_wpn = _mm(sv['a_na'], dy_na, ta=True, tm=1024, tk=t_len, name="proj_na_dw_%d" % l)
        da_ret = _mm(dy_ret, wpr_f, tb=True, name="proj_ret_dx_%d" % l)
        g_wpr = _mm(sv['a_ret'], dy_ret, ta=True, tm=1024, tk=t_len, name="proj_ret_dw_%d" % l)
        lg_l = log_gamma[l]
        if l == 0:
            pair_token = start_pair("0_rest", [g_wpn, g_wpr, g_wout], big_axes[1:])

        def act_bwd(da1, o1, z1, da2, o2, z2):
            da1, da2 = da1.astype(F32), da2.astype(F32)
            si1, ds1 = _silu_parts(z1.astype(F32))
            si2, ds2 = _silu_parts(z2.astype(F32))
            do1 = da1 * si1
            dz1 = da1 * o1.astype(F32) * ds1
            dn = da2 * si2
            do2, dz2 = [], []
            for hh in range(ret_heads):
                sl = slice(hh * RET_VAL_DIM, (hh + 1) * RET_VAL_DIM)
                oh = o2[:, sl]
                r = lax.rsqrt(jnp.mean(oh * oh, axis=-1, keepdims=True) + NORM_EPS)
                nh = oh * r
                dz2.append(da2[:, sl] * nh * ds2[:, sl])
                do2.append(r * (dn[:, sl] - nh * jnp.mean(dn[:, sl] * nh, axis=-1, keepdims=True)))
            return do1, dz1, jnp.concatenate(do2, axis=-1), jnp.concatenate(dz2, axis=-1)

        do_na, dz_na, do_ret, dz_ret = _ew(
            act_bwd, [('t', da_na, 0, w_na), ('t', sv['o_na'], 0, w_na), ('t', u, o_naz // w_na, w_na),
                      ('t', da_ret, 0, w_v), ('t', sv['o_ret'], 0, w_v), ('t', u, o_retz // w_v, w_v)],
            [('t', w_na, BF16), ('t', w_na, BF16), ('t', w_v, BF16), ('t', w_v, BF16)], rows=t_len, tr=tr, name="act_bwd_%d" % l)
        dq_na, dk_na, dv_na, dbias = _na_bwd(u, sv['bias'], sv['o_na'], do_na, s_len=s_len, heads=na_heads, name="na_bwd_%d" % l)
        small['drpb'][l] = _rpb_grad(dbias, name="rpb_grad_%d" % l)
        if l == 0:
            lg_l = lg_l + start_scatter("0_rest", dq_na)[0, 0] + pair_token[0, 0]
        dq_r, dk_r, dv_r, dlg = _ret_bwd(u, c2, s2, lg_l, sv['states'], do_ret, s_len=s_len, heads=ret_heads,
                                         q_off=o_retq, name="ret_bwd_%d" % l)
        small['ddecay'][l] = jnp.transpose(dlg[:, :, 0, 0]) * _sigmoid(-ret_decay_logit[l])
        du_parts = [dq_na, dk_na, dv_na, dz_na, dq_r, dk_r, dv_r, dz_ret, dg_na, dg_ret]
        du, = _ew(lambda *t: jnp.concatenate(t, axis=-1), [('t', p, 0, p.shape[1]) for p in du_parts], [('t', in_cols, BF16)],
                  rows=t_len, tr=tr, name="du_concat_%d" % l)
        g_win = _mm(sv['h'], du, ta=True, tm=1024, tn=1024, tk=t_len, name="in_proj_dw_%d" % l)
        if l > 0:
            du, pair_token = lax.optimization_barrier((du, start_pair("%d_all" % l, [g_win, g_wpn, g_wpr, g_wout], big_axes)))
        else:
            du, in_token = lax.optimization_barrier((du, start_pair("0_in", [g_win], big_axes[:1])))
        dh = _mm(du, win_f, tb=True, out_dtype=F32, tm=1152, tn=1024, name="in_proj_dx_%d" % l)

        def modnorm_bwd(xt, dht, dxt, gs_t):
            r = lax.rsqrt(jnp.mean(xt * xt, axis=-1, keepdims=True) + NORM_EPS)
            xh = xt * r
            dhg = dht * gs_t
            dx = r * (dhg - xh * jnp.mean(dhg * xh, axis=-1, keepdims=True)) + dxt
            return dx, _rsum(dht), _rsum(dht * xh)

        dx_all, dshift, dgs = _ew(modnorm_bwd, [('t', sv['x'], 0, d_model), ('t', dh, 0, d_model), ('t', dx_all, 0, d_model), ('g', sv['gs'])],
                                  [('t', d_model, F32), ('r', d_model, 2), ('r', d_model, 2)], rows=t_len, tr=tr, n0=n0, name="modnorm_bwd_%d" % l)
        dscale = dgs * norm_g[l][None, None, :]
        small['dnorm_g'][l] = jnp.sum(dgs * (1.0 + sv['scale']), axis=(0, 1))
        dmod = jnp.concatenate([dshift, dscale, dgate], axis=-1)[:, 0]
        small['dmod_lat'][l], small['dmod_ctx'][l] = dmod[0], dmod[1]

        if l > 0:
            back_token = start_scatter("%d_all" % l, dx_all)[0, 0] + pair_token[0, 0]

    grad_x = dx_all[:s_len][None]

    drpb = jnp.stack(small['drpb']).reshape(-1)
    ddecay = jnp.stack(small['ddecay']).reshape(-1)
    pieces = [jnp.stack(small['dmod_lat']).reshape(-1), jnp.stack(small['dmod_ctx']).reshape(-1),
              jnp.stack(small['dnorm_g']).reshape(-1), d_final_g.reshape(-1), drpb, ddecay, loss_part[None]]
    sizes = [int(p.shape[0]) for p in pieces]
    pads = [-(-s // LANES) * LANES for s in sizes]
    packed = jnp.concatenate([jnp.pad(p, (0, pd - s)) for p, s, pd in zip(pieces, sizes, pads)])
    gathered = _all_gather_small(_pack_rows(packed), name="gather_small_grads")
    r_small = gathered.shape[1]

    def sum8(*t):
        acc = t[0]
        for other in t[1:]:
            acc = acc + other
        return acc

    total, = _ew(sum8, [('t', gathered, 0, LANES, k) for k in range(N_DEV)], [('t', LANES, F32)], rows=r_small, tr=r_small, name="sum_devices")
    total = total.reshape(-1)
    starts = np.cumsum([0] + pads)
    g_mod_lat_sum, g_mod_ctx, g_norm_g, g_final_g, g_rpb, g_decay, loss = [total[starts[i]:starts[i] + sizes[i]] for i in range(len(pieces))]
    loss = loss[0]
    g_ada_b = (g_mod_lat_sum + g_mod_ctx).reshape(depth, mod_cols)
    g_mod_ctx = g_mod_ctx.reshape(depth, mod_cols)
    dmod_lat_all = gathered.reshape(N_DEV, -1)[:, :depth * mod_cols].reshape(N_DEV, depth, mod_cols)

    dcc_part = jnp.zeros((16, d_model), F32)
    ctx_cols = [lax.dynamic_slice_in_dim(g_mod_ctx[l], chip * mod_shard, mod_shard, axis=0) for l in range(depth)]
    for l in reversed(range(depth)):
        c_rows = jnp.concatenate([ctx_cols[l][None], jnp.zeros((15, mod_shard), F32)], axis=0)
        dcc_part = dcc_part + _mm(c_rows, ada_w, tb=True, b_lead=l, out_dtype=F32, name="ada_dc_%d" % l)
    dcc_all = _all_gather_small(_pack_rows(dcc_part[0]), name="gather_dcc")[:, :d_model // LANES].reshape(N_CHIPS, 2, d_model)[:, 0]

    tail_token = start_scatter("0_in", dcc_all) + in_token
    dcc = ((dcc_all[0] + dcc_all[1]) + dcc_all[2]) + dcc_all[3]
    sg = _sigmoid(c_ctx)
    g_c_ctx = dcc * (sg * (1.0 + c_ctx * (1.0 - sg)))
    for l in reversed(range(1, depth)):
        big_res = adamw_big(l, range(4), finish_scatter("%d_all" % l, tail_token), big_res)

    ada_res = None
    for l in reversed(range(depth)):
        lat_cols = lax.dynamic_slice_in_dim(dmod_lat_all[:, l], chip * mod_shard, mod_shard, axis=1)
        d_rows = jnp.concatenate([lat_cols, ctx_cols[l][None], jnp.zeros((16 - N_DEV - 1, mod_shard), F32)], axis=0) + tail_token[0, 0]
        g_ada = _mm(a_rows, d_rows, ta=True, out_dtype=F32, tm=512, name="ada_dw_%d" % l)
        ada_res = _adamw_layer(ada_w, m_ada_w, v_ada_w, g_ada, None, l, ada_res, name="adamw_ada_%d" % l)

    small_w = [(c_ctx, m_c_ctx, v_c_ctx, g_c_ctx), (ada_b, m_ada_b, v_ada_b, g_ada_b),
               (norm_g, m_norm_g, v_norm_g, g_norm_g), (na_rpb, m_na_rpb, v_na_rpb, g_rpb),
               (ret_decay_logit, m_ret_decay_logit, v_ret_decay_logit, g_decay), (final_g, m_final_g, v_final_g, g_final_g)]
    sw_sizes = [int(np.prod(t[0].shape)) for t in small_w]
    sw_pads = [-(-s // LANES) * LANES for s in sw_sizes]

    def pack(j):
        return _pack_rows(jnp.concatenate([jnp.pad(t[j].reshape(-1), (0, pd - s)) for t, s, pd in zip(small_w, sw_sizes, sw_pads)]))

    pw_, pm_, pv_, pg_ = pack(0), pack(1), pack(2), pack(3)
    sw_out = _ew(lambda w, m, v, g: (g,) + _adamw_math(w, g, m, v),
                 [('t', pw_, 0, LANES), ('t', pm_, 0, LANES), ('t', pv_, 0, LANES), ('t', pg_, 0, LANES)],
                 [('t', LANES, F32)] * 4, rows=pw_.shape[0], tr=pw_.shape[0], name="adamw_small")
    sw_starts = np.cumsum([0] + sw_pads)
    sw_out, ada_res, big_res = lax.optimization_barrier((sw_out, ada_res, big_res))
    big_res = adamw_big(0, range(1, 4), finish_scatter("0_rest", sw_out[0]), big_res)
    big_res = adamw_big(0, range(1), finish_scatter("0_in", sw_out[1]), big_res)

    def unpack(arr, i):
        return arr.reshape(-1)[sw_starts[i]:sw_starts[i] + sw_sizes[i]].reshape(small_w[i][0].shape)

    sm = [[unpack(sw_out[j], i) for i in range(len(small_w))] for j in range(4)]
    def ordered(j):
        return [sm[j][0], ada_res[j], sm[j][1], sm[j][2], big_res[0][j], sm[j][3], sm[j][4],
                big_res[1][j], big_res[2][j], big_res[3][j], sm[j][5]]

    return (loss, grad_x, *ordered(0), *ordered(1), *ordered(2), *ordered(3))
```

```python
import functools
import math

import numpy as np
import jax
import jax.numpy as jnp
from jax import lax
from jax.experimental import pallas as pl
from jax.experimental.pallas import tpu as pltpu

GRID_W = 64
NA_HEAD_DIM = 128
NA_WIN_ROWS = 8
NA_WIN_COLS = 16
NA_GROUP = 8
RET_GROUPS = (1, 2, 3)
RET_KEY_DIM = 128
RET_VAL_DIM = 256
RET_CHUNK = 128
ROPE_BASE = 10000.0
NORM_EPS = 1e-6
MASK_VALUE = -1e30
ADAM_LR = 0.001
ADAM_B1 = 0.9
ADAM_B2 = 0.999
ADAM_EPS = 1e-08
ADAM_WD = 0.01
ADAM_STEP = 10

N_CHIPS = 4
N_DEV = 8
LANES = 128
VMEM_LIMIT = 56 * 1024 * 1024
BF16 = jnp.bfloat16
F32 = jnp.float32
MESH = pl.DeviceIdType.MESH
ANY = pl.BlockSpec(memory_space=pl.ANY)


def _tile(dim, pref, align=LANES):
    if dim <= pref:
        return dim
    t = (pref // align) * align
    while t >= align:
        if dim % t == 0:
            return t
        t -= align
    return dim


def _rows_per_tile(rows, width, tile_bytes=1 << 20):
    return _tile(rows, max(8, tile_bytes // (4 * width)), 8)


def _params(sem):
    return pltpu.CompilerParams(dimension_semantics=sem, vmem_limit_bytes=VMEM_LIMIT)


def _sigmoid(x):
    return 1.0 / (1.0 + jnp.exp(-x))


def _dot(a, b, ca, cb):
    return lax.dot_general(a, b, (((ca,), (cb,)), ((), ())), preferred_element_type=F32)


def _mm(a, b, *, ta=False, tb=False, a_lead=None, b_lead=None, out_dtype=BF16, tm=1152, tn=1024, tk=2048, name):
    ash = a.shape[1:] if a_lead is not None else a.shape
    bsh = b.shape[1:] if b_lead is not None else b.shape
    m, k = (ash[1], ash[0]) if ta else ash
    n, k2 = bsh if tb else (bsh[1], bsh[0])
    assert k == k2, (name, ash, bsh)
    tm, tn, tk = _tile(m, tm), _tile(n, tn), _tile(k, tk)
    nk = k // tk

    def lead(spec_shape, imap, l):
        if l is None:
            return pl.BlockSpec(spec_shape, imap)
        return pl.BlockSpec((None,) + spec_shape, lambda i, j, kk: (l,) + imap(i, j, kk))

    a_spec = lead((tk, tm), lambda i, j, kk: (kk, i), a_lead) if ta else lead((tm, tk), lambda i, j, kk: (i, kk), a_lead)
    b_spec = lead((tn, tk), lambda i, j, kk: (j, kk), b_lead) if tb else lead((tk, tn), lambda i, j, kk: (kk, j), b_lead)
    ca, cb = (0 if ta else 1), (1 if tb else 0)

    def body(a_ref, b_ref, o_ref, *scratch):
        part = _dot(a_ref[...].astype(BF16), b_ref[...].astype(BF16), ca, cb)
        if nk == 1:
            o_ref[...] = part.astype(o_ref.dtype)
            return
        acc_ref, = scratch
        kk = pl.program_id(2)

        @pl.when(kk == 0)
        def _():
            acc_ref[...] = part

        @pl.when(kk > 0)
        def _():
            acc_ref[...] += part

        @pl.when(kk == nk - 1)
        def _():
            o_ref[...] = acc_ref[...].astype(o_ref.dtype)

    return pl.pallas_call(
        body, name=name, grid=(m // tm, n // tn, nk),
        in_specs=[a_spec, b_spec],
        out_specs=pl.BlockSpec((tm, tn), lambda i, j, kk: (i, j)),
        out_shape=jax.ShapeDtypeStruct((m, n), out_dtype),
        scratch_shapes=[] if nk == 1 else [pltpu.VMEM((tm, tn), F32)],
        compiler_params=_params(("parallel", "parallel", "arbitrary")),
    )(a, b)


def _ew(fn, ins, outs, *, rows, tr, name, n0=None, aliases=None):
    assert rows % tr == 0, (name, rows, tr)
    nt = rows // tr

    def grp(i):
        return 0 if n0 is None else jnp.where(i < n0, 0, 1)

    in_specs, args = [], []
    for spec in ins:
        if spec[0] == 't':
            arr, cb, w = spec[1], spec[2], spec[3]
            l = spec[4] if len(spec) > 4 else None
            if l is None:
                in_specs.append(pl.BlockSpec((tr, w), functools.partial(lambda i, cb: (i, cb), cb=cb)))
            else:
                in_specs.append(pl.BlockSpec((None, tr, w), functools.partial(lambda i, cb, l: (l, i, cb), cb=cb, l=l)))
            args.append(arr)
        else:
            arr = spec[1]
            g = arr.shape[0]
            if g == 1:
                in_specs.append(pl.BlockSpec((None, 1, arr.shape[2]), lambda i: (0, 0, 0)))
            else:
                in_specs.append(pl.BlockSpec((None, 1, arr.shape[2]), lambda i: (grp(i), 0, 0)))
            args.append(arr)
    out_specs, out_shapes, is_red = [], [], []
    for spec in outs:
        if spec[0] == 't':
            w, dt = spec[1], spec[2]
            if len(spec) > 3:
                l, nl = spec[3], spec[4]
                out_specs.append(pl.BlockSpec((None, tr, w), functools.partial(lambda i, l: (l, i, 0), l=l)))
                out_shapes.append(jax.ShapeDtypeStruct((nl, rows, w), dt))
            else:
                out_specs.append(pl.BlockSpec((tr, w), lambda i: (i, 0)))
                out_shapes.append(jax.ShapeDtypeStruct((rows, w), dt))
            is_red.append(False)
        else:
            w, g = spec[1], spec[2]
            if g == 1:
                out_specs.append(pl.BlockSpec((None, 1, w), lambda i: (0, 0, 0)))
            else:
                out_specs.append(pl.BlockSpec((None, 1, w), lambda i: (grp(i), 0, 0)))
            out_shapes.append(jax.ShapeDtypeStruct((g, 1, w), F32))
            is_red.append(True)
    n_in = len(ins)
    n_alias = 0 if aliases is None else len(aliases)

    def body(*refs):
        in_refs = refs[:n_in]
        out_refs = refs[n_in + n_alias:]
        res = fn(*[r[...] for r in in_refs])
        if not isinstance(res, (tuple, list)):
            res = (res,)
        i = pl.program_id(0)
        first = (i == 0) if n0 is None else ((i == 0) | (i == n0))
        for o_ref, val, red in zip(out_refs, res, is_red):
            if not red:
                o_ref[...] = val.astype(o_ref.dtype)
            else:
                @pl.when(first)
                def _(o_ref=o_ref, val=val):
                    o_ref[...] = val

                @pl.when(jnp.logical_not(first))
                def _(o_ref=o_ref, val=val):
                    o_ref[...] += val

    io_alias = {}
    if aliases is not None:
        for a_idx, (arr, o_idx) in enumerate(aliases):
            in_specs.append(ANY)
            args.append(arr)
            io_alias[n_in + a_idx] = o_idx
    has_red = any(is_red)
    return pl.pallas_call(
        body, name=name, grid=(nt,), in_specs=in_specs, out_specs=out_specs, out_shape=out_shapes,
        input_output_aliases=io_alias,
        compiler_params=_params(("arbitrary",) if has_red else ("parallel",)),
    )(*args)


def _half_tiles(pr, pw):
    tr, tc = _tile(pr, 256, 16), _tile(pw, 2048)
    return tr, tc, (pr // tr, pw // tc)


def _half_spec(tr, tc, ax, grid):
    if ax == 1:
        return pl.BlockSpec((tr, tc), lambda i, j, sel: (sel[0] * grid[0] + i, j))
    return pl.BlockSpec((tr, tc), lambda i, j, sel: (i, sel[0] * grid[1] + j))


def _sum_pair(g, theirs, ax, ci, *, name):
    pr, pw = theirs.shape
    tr, tc, grid = _half_tiles(pr, pw)

    def body(sel, a_ref, b_ref, o_ref):
        o_ref[...] = (a_ref[...].astype(F32) + b_ref[...].astype(F32)).astype(o_ref.dtype)

    tile = pl.BlockSpec((tr, tc), lambda i, j, sel: (i, j))
    return pl.pallas_call(
        body, name=name,
        grid_spec=pltpu.PrefetchScalarGridSpec(
            num_scalar_prefetch=1, grid=grid, in_specs=[_half_spec(tr, tc, ax, grid), tile], out_specs=tile),
        out_shape=jax.ShapeDtypeStruct((pr, pw), BF16),
        compiler_params=_params(("parallel", "parallel")),
    )(jnp.reshape(ci, (1,)).astype(jnp.int32), g, theirs)


def _sum_chips_into(own, recv, ax, ci, *, name):
    pr, pw = own.shape
    tr, tc, grid = _half_tiles(pr, pw)
    full_shape = (2 * pr, pw) if ax == 1 else (pr, 2 * pw)

    def body(sel, a_ref, r_ref, o_ref):
        acc = a_ref[...].astype(F32)
        for k in range(N_CHIPS - 1):
            acc = acc + r_ref[k].astype(F32)
        o_ref[...] = acc

    return pl.pallas_call(
        body, name=name,
        grid_spec=pltpu.PrefetchScalarGridSpec(
            num_scalar_prefetch=1, grid=grid,
            in_specs=[pl.BlockSpec((tr, tc), lambda i, j, sel: (i, j)),
                      pl.BlockSpec((N_CHIPS - 1, tr, tc), lambda i, j, sel: (0, i, j))],
            out_specs=_half_spec(tr, tc, ax, grid)),
        out_shape=jax.ShapeDtypeStruct(full_shape, F32),
        compiler_params=_params(("parallel", "parallel")),
    )(jnp.reshape(ci, (1,)).astype(jnp.int32), own, recv)


def _rsum(v):
    return jnp.sum(v, axis=0, keepdims=True)


def _silu_parts(z):
    sg = _sigmoid(z)
    return z * sg, sg * (1.0 + z * (1.0 - sg))


def _na_bias_table(rpb, rows, *, name):
    kh, kw = NA_WIN_ROWS, NA_WIN_COLS
    assert rows >= kh
    heads = rpb.shape[0]
    e1, e2 = _na_onehots()
    rpb16 = jnp.pad(rpb, ((0, 0), (0, 16 - rpb.shape[1]), (0, LANES - rpb.shape[2])))

    def body(r_ref, e1_ref, e2_ref, o_ref):
        e1b = e1_ref[...].astype(BF16)
        y = sum(_dot(e1b, part, 0, 0) for part in _split3(r_ref[...]))
        e2b = e2_ref[...].astype(BF16)
        o_ref[...] = sum(_dot(part, e2b, 1, 1) for part in _split3(y))

    z = pl.pallas_call(
        body, name=name, grid=(heads,),
        in_specs=[pl.BlockSpec((None, 16, LANES), lambda h: (h, 0, 0)),
                  pl.BlockSpec(e1.shape, lambda h: (0, 0)), pl.BlockSpec(e2.shape, lambda h: (0, 0))],
        out_specs=pl.BlockSpec((None, kh * kh, GRID_W * GRID_W), lambda h: (h, 0, 0)),
        out_shape=jax.ShapeDtypeStruct((heads, kh * kh, GRID_W * GRID_W), F32),
        compiler_params=_params(("parallel",)),
    )(rpb16, e1, e2)
    return z


def _na_bias_layout(z):
    heads = z.shape[0]
    kh, kw = NA_WIN_ROWS, NA_WIN_COLS
    cidx = np.arange(GRID_W)
    c0 = np.clip(cidx - kw // 2, 0, GRID_W - kw)
    col_in = (cidx[None, :] >= c0[:, None]) & (cidx[None, :] < c0[:, None] + kw)
    bias = z.reshape(heads, kh, kh, GRID_W, GRID_W).transpose(0, 1, 3, 2, 4)
    bias = jnp.where(col_in[None, None, :, None, :], bias, MASK_VALUE)
    return bias.reshape(heads, kh, GRID_W, kh * GRID_W)


def _na_onehots():
    kh, kw = NA_WIN_ROWS, NA_WIN_COLS
    cidx = np.arange(GRID_W)
    dc = cidx[None, :] - cidx[:, None] + (kw - 1)
    e2 = np.zeros((GRID_W * GRID_W, LANES), np.float32)
    ok = (dc >= 0) & (dc <= 2 * kw - 2)
    cq, ck = np.nonzero(ok)
    e2[cq * GRID_W + ck, dc[cq, ck]] = 1.0
    dr = np.arange(kh)[None, :] - np.arange(kh)[:, None] + (kh - 1)
    e1 = np.zeros((16, kh * kh), np.float32)
    dl, kr = np.nonzero(np.ones_like(dr))
    e1[dr[dl, kr], dl * kh + kr] = 1.0
    return jnp.asarray(e1), jnp.asarray(e2)


def _na_fwd(u, bias, *, s_len, heads, name):
    t_len = u.shape[0]
    rows = s_len // GRID_W
    nloc = NA_WIN_ROWS * GRID_W
    scale = NA_HEAD_DIM ** -0.5
    hd = NA_HEAD_DIM

    def body(q_ref, k_ref, v_ref, b_ref, o_ref):
        kc = k_ref[s_len:t_len, :]
        vc = v_ref[s_len:t_len, :]

        def group(g, carry):
            rs = [g * NA_GROUP + i for i in range(NA_GROUP)]
            r0s = [jnp.clip(r - NA_WIN_ROWS // 2, 0, rows - NA_WIN_ROWS) for r in rs]
            gs_ = pl.multiple_of(g * (NA_GROUP * GRID_W), NA_GROUP * GRID_W)
            kss = [pl.multiple_of(r0 * GRID_W, GRID_W) for r0 in r0s]
            q_all = q_ref[pl.ds(gs_, NA_GROUP * GRID_W), :]
            s_ctx = _dot(q_all, kc, 1, 1) * scale
            s_loc = [_dot(q_all[i * GRID_W:(i + 1) * GRID_W], k_ref[pl.ds(kss[i], nloc), :], 1, 1) * scale + b_ref[rs[i] - r0s[i]]
                     for i in range(NA_GROUP)]
            p_loc, p_ctx, inv = [], [], []
            for i in range(NA_GROUP):
                sc = s_ctx[i * GRID_W:(i + 1) * GRID_W]
                m = jnp.maximum(jnp.max(s_loc[i], axis=-1, keepdims=True), jnp.max(sc, axis=-1, keepdims=True))
                pl_, pc_ = jnp.exp(s_loc[i] - m), jnp.exp(sc - m)
                inv.append(1.0 / (jnp.sum(pl_, axis=-1, keepdims=True) + jnp.sum(pc_, axis=-1, keepdims=True)))
                p_loc.append(pl_.astype(BF16))
                p_ctx.append(pc_.astype(BF16))
            o_ctx = _dot(jnp.concatenate(p_ctx, axis=0), vc, 1, 0)
            o_loc = [_dot(p_loc[i], v_ref[pl.ds(kss[i], nloc), :], 1, 0) for i in range(NA_GROUP)]
            out = jnp.concatenate([(o_loc[i] + o_ctx[i * GRID_W:(i + 1) * GRID_W]) * inv[i] for i in range(NA_GROUP)], axis=0)
            o_ref[pl.ds(gs_, NA_GROUP * GRID_W), :] = out.astype(o_ref.dtype)
            return carry

        lax.fori_loop(0, rows // NA_GROUP, group, 0)
        qc = q_ref[s_len:t_len, :]
        s = _dot(qc, kc, 1, 1) * scale
        p = jnp.exp(s - jnp.max(s, axis=-1, keepdims=True))
        o = _dot(p.astype(BF16), vc, 1, 0) / jnp.sum(p, axis=-1, keepdims=True)
        o_ref[s_len:t_len, :] = o.astype(o_ref.dtype)

    col = lambda off: pl.BlockSpec((t_len, hd), functools.partial(lambda h, off: (0, off + h), off=off))
    return pl.pallas_call(
        body, name=name, grid=(heads,),
        in_specs=[col(0), col(heads), col(2 * heads),
                  pl.BlockSpec((None, NA_WIN_ROWS, GRID_W, nloc), lambda h: (h, 0, 0, 0))],
        out_specs=pl.BlockSpec((t_len, hd), lambda h: (0, h)),
        out_shape=jax.ShapeDtypeStruct((t_len, heads * hd), BF16),
        compiler_params=_params(("parallel",)),
    )(u, u, u, bias)


def _na_bwd(u, bias, o, do, *, s_len, heads, name):
    t_len = u.shape[0]
    rows = s_len // GRID_W
    nloc = NA_WIN_ROWS * GRID_W
    scale = NA_HEAD_DIM ** -0.5
    hd = NA_HEAD_DIM

    def body(q_ref, k_ref, v_ref, b_ref, o_ref, do_ref, dq_ref, dk_ref, dv_ref, db_ref, dk_acc, dv_acc):
        kc = k_ref[s_len:t_len, :]
        vc = v_ref[s_len:t_len, :]
        dk_acc[...] = jnp.zeros_like(dk_acc)
        dv_acc[...] = jnp.zeros_like(dv_acc)
        db_ref[...] = jnp.zeros_like(db_ref)

        def group(g, carry):
            n_g, rw = NA_GROUP, GRID_W
            rs = [g * n_g + i for i in range(n_g)]
            r0s = [jnp.clip(r - NA_WIN_ROWS // 2, 0, rows - NA_WIN_ROWS) for r in rs]
            dls = [r - r0 for r, r0 in zip(rs, r0s)]
            gs_ = pl.ds(pl.multiple_of(g * (n_g * rw), n_g * rw), n_g * rw)
            kss = [pl.ds(pl.multiple_of(r0 * rw, rw), nloc) for r0 in r0s]
            row_of = lambda a, i: a[i * rw:(i + 1) * rw]
            q_all, do_all = q_ref[gs_, :], do_ref[gs_, :]
            dlt_all = jnp.sum(do_all.astype(F32) * o_ref[gs_, :].astype(F32), axis=-1, keepdims=True)
            s_ctx = _dot(q_all, kc, 1, 1) * scale
            dp_ctx = _dot(do_all, vc, 1, 1)
            s_loc = [_dot(row_of(q_all, i), k_ref[kss[i], :], 1, 1) * scale + b_ref[dls[i]] for i in range(n_g)]
            dp_loc = [_dot(row_of(do_all, i), v_ref[kss[i], :], 1, 1) for i in range(n_g)]
            p_loc_b, ds_loc_b, p_ctx_b, ds_ctx_b = [], [], [], []
            for i in range(n_g):
                sc, dlt = row_of(s_ctx, i), row_of(dlt_all, i)
                m = jnp.maximum(jnp.max(s_loc[i], axis=-1, keepdims=True), jnp.max(sc, axis=-1, keepdims=True))
                pl_, pc_ = jnp.exp(s_loc[i] - m), jnp.exp(sc - m)
                inv = 1.0 / (jnp.sum(pl_, axis=-1, keepdims=True) + jnp.sum(pc_, axis=-1, keepdims=True))
                pl_, pc_ = pl_ * inv, pc_ * inv
                ds_l = pl_ * (dp_loc[i] - dlt)
                db_ref[dls[i]] += ds_l
                p_loc_b.append(pl_.astype(BF16))
                ds_loc_b.append(ds_l.astype(BF16))
                p_ctx_b.append(pc_.astype(BF16))
                ds_ctx_b.append((pc_ * (row_of(dp_ctx, i) - dlt)).astype(BF16))
            p_ctx_all, ds_ctx_all = jnp.concatenate(p_ctx_b, axis=0), jnp.concatenate(ds_ctx_b, axis=0)
            dq_ctx = _dot(ds_ctx_all, kc, 1, 0)
            dq_loc = [_dot(ds_loc_b[i], k_ref[kss[i], :], 1, 0) for i in range(n_g)]
            dk_loc = [_dot(ds_loc_b[i], row_of(q_all, i), 0, 0) for i in range(n_g)]
            dv_loc = [_dot(p_loc_b[i], row_of(do_all, i), 0, 0) for i in range(n_g)]
            dk_ctx = _dot(ds_ctx_all, q_all, 0, 0)
            dv_ctx = _dot(p_ctx_all, do_all, 0, 0)
            dq_ref[gs_, :] = ((jnp.concatenate(dq_loc, axis=0) + dq_ctx) * scale).astype(dq_ref.dtype)
            for i in range(n_g):
                dk_acc[kss[i], :] += dk_loc[i] * scale
                dv_acc[kss[i], :] += dv_loc[i]
            dk_acc[s_len:t_len, :] += dk_ctx * scale
            dv_acc[s_len:t_len, :] += dv_ctx
            return carry

        lax.fori_loop(0, rows // NA_GROUP, group, 0)
        qc = q_ref[s_len:t_len, :]
        dout = do_ref[s_len:t_len, :]
        out = o_ref[s_len:t_len, :]
        s = _dot(qc, kc, 1, 1) * scale
        p = jnp.exp(s - jnp.max(s, axis=-1, keepdims=True))
        p = p / jnp.sum(p, axis=-1, keepdims=True)
        dlt = jnp.sum(dout.astype(F32) * out.astype(F32), axis=-1, keepdims=True)
        ds = (p * (_dot(dout, vc, 1, 1) - dlt)).astype(BF16)
        dq_ref[s_len:t_len, :] = (_dot(ds, kc, 1, 0) * scale).astype(dq_ref.dtype)
        dk_acc[s_len:t_len, :] += _dot(ds, qc, 0, 0) * scale
        dv_acc[s_len:t_len, :] += _dot(p.astype(BF16), dout, 0, 0)
        dk_ref[...] = dk_acc[...].astype(dk_ref.dtype)
        dv_ref[...] = dv_acc[...].astype(dv_ref.dtype)

    col = lambda off: pl.BlockSpec((t_len, hd), functools.partial(lambda h, off: (0, off + h), off=off))
    tbl = pl.BlockSpec((None, NA_WIN_ROWS, GRID_W, nloc), lambda h: (h, 0, 0, 0))
    tok = jax.ShapeDtypeStruct((t_len, heads * hd), BF16)
    return pl.pallas_call(
        body, name=name, grid=(heads,),
        in_specs=[col(0), col(heads), col(2 * heads), tbl, col(0), col(0)],
        out_specs=[col(0), col(0), col(0), tbl],
        out_shape=[tok, tok, tok, jax.ShapeDtypeStruct(bias.shape, F32)],
        scratch_shapes=[pltpu.VMEM((t_len, hd), F32), pltpu.VMEM((t_len, hd), F32)],
        compiler_params=_params(("parallel",)),
    )(u, u, u, bias, o, do)


def _split3(x):
    hi = x.astype(BF16)
    r1 = x - hi.astype(F32)
    mid = r1.astype(BF16)
    lo = (r1 - mid.astype(F32)).astype(BF16)
    return hi, mid, lo


def _rpb_grad(dbias, *, name):
    heads = dbias.shape[0]
    kh = NA_WIN_ROWS
    e1, e2 = _na_onehots()
    x = dbias.reshape(heads, kh, GRID_W, kh, GRID_W).transpose(0, 1, 3, 2, 4).reshape(heads, kh * kh, GRID_W * GRID_W)

    def body(x_ref, e1_ref, e2_ref, o_ref):
        e2b = e2_ref[...].astype(BF16)
        y = sum(_dot(part, e2b, 1, 0) for part in _split3(x_ref[...]))
        e1b = e1_ref[...].astype(BF16)
        o_ref[...] = sum(_dot(e1b, part, 1, 0) for part in _split3(y))

    out = pl.pallas_call(
        body, name=name, grid=(heads,),
        in_specs=[pl.BlockSpec((None, kh * kh, GRID_W * GRID_W), lambda h: (h, 0, 0)),
                  pl.BlockSpec(e1.shape, lambda h: (0, 0)), pl.BlockSpec(e2.shape, lambda h: (0, 0))],
        out_specs=pl.BlockSpec((None, 16, LANES), lambda h: (h, 0, 0)),
        out_shape=jax.ShapeDtypeStruct((heads, 16, LANES), F32),
        compiler_params=_params(("parallel",)),
    )(x, e1, e2)
    return out[:, :2 * kh - 1, :2 * NA_WIN_COLS - 1]


def _rope_tables(s_len, l_len):
    nf = RET_KEY_DIM // 4
    t = np.arange(s_len)
    row = (t // GRID_W).astype(np.float32)
    colp = (t % GRID_W).astype(np.float32)
    inv_freq = jnp.asarray(ROPE_BASE, F32) ** (-jnp.arange(nf, dtype=F32) / nf)
    ang = jnp.concatenate([jnp.asarray(row)[:, None] * inv_freq, jnp.asarray(colp)[:, None] * inv_freq], axis=-1)
    cos, sin = jnp.cos(ang), jnp.sin(ang)
    c2 = jnp.concatenate([cos, cos], axis=-1)
    s2 = jnp.concatenate([-sin, sin], axis=-1)
    c2 = jnp.concatenate([c2, jnp.ones((l_len, RET_KEY_DIM), F32)], axis=0)
    s2 = jnp.concatenate([s2, jnp.zeros((l_len, RET_KEY_DIM), F32)], axis=0)
    return c2, s2


def _rope(x, c2, s2):
    return x * c2 + pltpu.roll(x, RET_KEY_DIM // 2, 1) * s2


def _rope_t(d, c2, s2):
    return d * c2 + pltpu.roll(d * s2, RET_KEY_DIM // 2, 1)


def _ret_decays(lg, direction):
    cs = RET_CHUNK
    i_col = lax.broadcasted_iota(jnp.int32, (cs, 1), 0)
    p_col = jnp.where(direction == 0, i_col, cs - 1 - i_col).astype(F32)
    pi = lax.broadcasted_iota(jnp.int32, (cs, cs), 0)
    pj = lax.broadcasted_iota(jnp.int32, (cs, cs), 1)
    diff = jnp.where(direction == 0, pi - pj, pj - pi).astype(F32)
    dm = jnp.where(diff >= 0, jnp.exp(jnp.maximum(diff, 0.0) * lg), 0.0)
    qdec = jnp.exp((p_col + 1.0) * lg)
    kdec = jnp.exp((cs - 1.0 - p_col) * lg)
    cd = jnp.exp(jnp.full((1, 1), cs, F32) * lg)
    return p_col, dm, qdec, kdec, cd


def _ret_chunk_index(t, direction, n_chunks, lat_chunks):
    return jnp.where(direction == 0, lax.rem(t + lat_chunks, n_chunks), n_chunks - 1 - t)


def _ret_fwd(u, c2, s2, lg, *, s_len, heads, q_off, name):
    t_len = u.shape[0]
    cs, dk, dv = RET_CHUNK, RET_KEY_DIM, RET_VAL_DIM
    n_chunks, lat_chunks = t_len // cs, s_len // cs
    k_scale = dk ** -0.5
    qb, kb, vb = q_off // dk, q_off // dk + heads, (q_off + 2 * heads * dk) // dv

    def body(lg_ref, q_ref, k_ref, v_ref, c_ref, s_ref, o_ref, st_ref, qd_s, kv_s):
        h, d = pl.program_id(0), pl.program_id(1)
        _, dm, qdec, kdec, cd = _ret_decays(lg_ref[d, h], d)
        n_g = max(g for g in RET_GROUPS if n_chunks % g == 0)
        rows_of = lambda c: pl.ds(pl.multiple_of(c * cs, cs), cs)

        def local(gi, carry):
            rws = [rows_of(gi * n_g + j) for j in range(n_g)]
            qcs = [_rope(q_ref[r, :].astype(F32), c_ref[r, :], s_ref[r, :]) for r in rws]
            kcs = [_rope(k_ref[r, :].astype(F32), c_ref[r, :], s_ref[r, :]) * k_scale for r in rws]
            vcs = [v_ref[r, :] for r in rws]
            a_raw = [_dot(qcs[j].astype(BF16), kcs[j].astype(BF16), 1, 1) for j in range(n_g)]
            kv = [_dot((kcs[j] * kdec).astype(BF16), vcs[j], 0, 0) for j in range(n_g)]
            inner = [_dot((a_raw[j] * dm).astype(BF16), vcs[j], 1, 0) for j in range(n_g)]
            for j in range(n_g):
                qd_s[rws[j], :] = (qcs[j] * qdec).astype(BF16)
                kv_s[gi * n_g + j] = kv[j]

            @pl.when(d == 0)
            def _():
                for j in range(n_g):
                    o_ref[rws[j], :] = inner[j]

            @pl.when(d == 1)
            def _():
                for j in range(n_g):
                    o_ref[rws[j], :] += inner[j]

            return carry

        lax.fori_loop(0, n_chunks // n_g, local, 0)

        def scan(t, st):
            st_ref[t] = st
            return st * cd + kv_s[_ret_chunk_index(t, d, n_chunks, lat_chunks)]

        lax.fori_loop(0, n_chunks, scan, jnp.zeros((dk, dv), F32))

        def cross(gi, carry):
            ts = [gi * n_g + j for j in range(n_g)]
            rws = [rows_of(_ret_chunk_index(t, d, n_chunks, lat_chunks)) for t in ts]
            outs = [_dot(qd_s[rws[j], :], st_ref[ts[j]].astype(BF16), 1, 0) for j in range(n_g)]
            for j in range(n_g):
                o_ref[rws[j], :] += outs[j]
            return carry

        lax.fori_loop(0, n_chunks // n_g, cross, 0)

    return pl.pallas_call(
        body, name=name, grid=(heads, 2),
        in_specs=[pl.BlockSpec(memory_space=pltpu.SMEM),
                  pl.BlockSpec((t_len, dk), lambda h, d: (0, qb + h)),
                  pl.BlockSpec((t_len, dk), lambda h, d: (0, kb + h)),
                  pl.BlockSpec((t_len, dv), lambda h, d: (0, vb + h)),
                  pl.BlockSpec((t_len, dk), lambda h, d: (0, 0)),
                  pl.BlockSpec((t_len, dk), lambda h, d: (0, 0))],
        out_specs=[pl.BlockSpec((t_len, dv), lambda h, d: (0, h)),
                   pl.BlockSpec((None, None, n_chunks, dk, dv), lambda h, d: (h, d, 0, 0, 0))],
        out_shape=[jax.ShapeDtypeStruct((t_len, heads * dv), F32),
                   jax.ShapeDtypeStruct((heads, 2, n_chunks, dk, dv), F32)],
        scratch_shapes=[pltpu.VMEM((t_len, dk), BF16), pltpu.VMEM((n_chunks, dk, dv), F32)],
        compiler_params=_params(("parallel", "arbitrary")),
    )(lg, u, u, u, c2, s2)


def _ret_bwd(u, c2, s2, lg, states, do, *, s_len, heads, q_off, name):
    t_len = u.shape[0]
    cs, dk, dv = RET_CHUNK, RET_KEY_DIM, RET_VAL_DIM
    n_chunks, lat_chunks = t_len // cs, s_len // cs
    k_scale = dk ** -0.5
    qb, kb, vb = q_off // dk, q_off // dk + heads, (q_off + 2 * heads * dk) // dv

    def body(lg_ref, q_ref, k_ref, v_ref, c_ref, s_ref, st_ref, do_ref, dq_ref, dk_ref, dv_ref, dlg_ref, acc, qdo_s, dst_s):
        h, d = pl.program_id(0), pl.program_id(1)
        p_col, dm, qdec, kdec, cd = _ret_decays(lg_ref[d, h], d)
        acc[...] = jnp.zeros_like(acc)
        n_g = max(g for g in RET_GROUPS[:2] if n_chunks % g == 0)
        rows_of = lambda c: pl.ds(pl.multiple_of(c * cs, cs), cs)
        chunk_of = lambda t: _ret_chunk_index(t, d, n_chunks, lat_chunks)

        def local(gi, carry):
            rws = [rows_of(gi * n_g + j) for j in range(n_g)]
            qds = [(_rope(q_ref[r, :].astype(F32), c_ref[r, :], s_ref[r, :]) * qdec).astype(BF16) for r in rws]
            prods = [_dot(qds[j], do_ref[rws[j], :].astype(BF16), 0, 0) for j in range(n_g)]
            for j in range(n_g):
                qdo_s[gi * n_g + j] = prods[j]
            return carry

        lax.fori_loop(0, n_chunks // n_g, local, 0)

        def scan(i, dst):
            t = n_chunks - 1 - i
            dst_s[t] = dst
            return dst * cd + qdo_s[chunk_of(t)]

        lax.fori_loop(0, n_chunks, scan, jnp.zeros((dk, dv), F32))

        def grads(gi, carry):
            ts = [gi * n_g + j for j in range(n_g)]
            rws = [rows_of(chunk_of(t)) for t in ts]
            ccs, sss = [c_ref[r, :] for r in rws], [s_ref[r, :] for r in rws]
            qcs = [_rope(q_ref[r, :].astype(F32), cc, ss) for r, cc, ss in zip(rws, ccs, sss)]
            kcs = [_rope(k_ref[r, :].astype(F32), cc, ss) * k_scale for r, cc, ss in zip(rws, ccs, sss)]
            vcs = [v_ref[r, :] for r in rws]
            docs = [do_ref[r, :].astype(BF16) for r in rws]
            sts = [st_ref[t] for t in ts]
            dsts = [dst_s[t] for t in ts]
            q16 = [x.astype(BF16) for x in qcs]
            k16 = [x.astype(BF16) for x in kcs]
            dst16 = [x.astype(BF16) for x in dsts]
            rng = range(n_g)
            a_raw = [_dot(q16[j], k16[j], 1, 1) for j in rng]
            da_raw = [_dot(docs[j], vcs[j], 1, 1) for j in rng]
            dq_c = [_dot(docs[j], sts[j].astype(BF16), 1, 1) * qdec for j in rng]
            dv_s = [_dot((kcs[j] * kdec).astype(BF16), dst16[j], 1, 0) for j in rng]
            dk_s = [_dot(vcs[j], dst16[j], 1, 1) * kdec for j in rng]
            a16 = [(a_raw[j] * dm).astype(BF16) for j in rng]
            dam = [(da_raw[j] * dm).astype(BF16) for j in rng]
            dq_i = [_dot(dam[j], k16[j], 1, 0) for j in rng]
            dk_i = [_dot(dam[j], q16[j], 0, 0) for j in rng]
            dv_i = [_dot(a16[j], docs[j], 0, 0) for j in rng]
            for j in rng:
                g = (jnp.sum(qcs[j] * (p_col * dq_i[j] + (p_col + 1.0) * dq_c[j]), axis=-1, keepdims=True)
                     + jnp.sum(kcs[j] * ((cs - 1.0 - p_col) * dk_s[j] - p_col * dk_i[j]), axis=-1, keepdims=True))
                g = (jnp.sum(g, axis=0, keepdims=True)
                     + cs * cd * jnp.sum(jnp.sum(dsts[j] * sts[j], axis=-1, keepdims=True), axis=0, keepdims=True))
                acc[...] += jnp.broadcast_to(g, acc.shape)
            dqs = [_rope_t(dq_i[j] + dq_c[j], ccs[j], sss[j]) for j in rng]
            dks = [_rope_t((dk_i[j] + dk_s[j]) * k_scale, ccs[j], sss[j]) for j in rng]
            dvs = [dv_i[j] + dv_s[j] for j in rng]

            @pl.when(d == 0)
            def _():
                for j in rng:
                    dq_ref[rws[j], :] = dqs[j].astype(dq_ref.dtype)
                    dk_ref[rws[j], :] = dks[j].astype(dk_ref.dtype)
                    dv_ref[rws[j], :] = dvs[j].astype(dv_ref.dtype)

            @pl.when(d == 1)
            def _():
                for j in rng:
                    dq_ref[rws[j], :] = (dq_ref[rws[j], :].astype(F32) + dqs[j]).astype(dq_ref.dtype)
                    dk_ref[rws[j], :] = (dk_ref[rws[j], :].astype(F32) + dks[j]).astype(dk_ref.dtype)
                    dv_ref[rws[j], :] = (dv_ref[rws[j], :].astype(F32) + dvs[j]).astype(dv_ref.dtype)

            return carry

        lax.fori_loop(0, n_chunks // n_g, grads, 0)
        dlg_ref[...] = acc[...]

    return pl.pallas_call(
        body, name=name, grid=(heads, 2),
        in_specs=[pl.BlockSpec(memory_space=pltpu.SMEM),
                  pl.BlockSpec((t_len, dk), lambda h, d: (0, qb + h)),
                  pl.BlockSpec((t_len, dk), lambda h, d: (0, kb + h)),
                  pl.BlockSpec((t_len, dv), lambda h, d: (0, vb + h)),
                  pl.BlockSpec((t_len, dk), lambda h, d: (0, 0)),
                  pl.BlockSpec((t_len, dk), lambda h, d: (0, 0)),
                  pl.BlockSpec((None, None, n_chunks, dk, dv), lambda h, d: (h, d, 0, 0, 0)),
                  pl.BlockSpec((t_len, dv), lambda h, d: (0, h))],
        out_specs=[pl.BlockSpec((t_len, dk), lambda h, d: (0, h)),
                   pl.BlockSpec((t_len, dk), lambda h, d: (0, h)),
                   pl.BlockSpec((t_len, dv), lambda h, d: (0, h)),
                   pl.BlockSpec((None, None, 8, LANES), lambda h, d: (h, d, 0, 0))],
        out_shape=[jax.ShapeDtypeStruct((t_len, heads * dk), BF16),
                   jax.ShapeDtypeStruct((t_len, heads * dk), BF16),
                   jax.ShapeDtypeStruct((t_len, heads * dv), BF16),
                   jax.ShapeDtypeStruct((heads, 2, 8, LANES), F32)],
        scratch_shapes=[pltpu.VMEM((8, LANES), F32), pltpu.VMEM((n_chunks, dk, dv), F32), pltpu.VMEM((n_chunks, dk, dv), F32)],
        compiler_params=_params(("parallel", "arbitrary")),
    )(lg, u, u, u, c2, s2, states, do)


def _mesh_pos():
    return lax.axis_index("x"), lax.axis_index("y"), lax.axis_index("c")


def _all_gather_small(buf, *, name):
    r = buf.shape[0]

    def body(x_ref, o_ref, send_sems, recv_sems, local_sem):
        x, y, c = _mesh_pos()
        me = 4 * x + 2 * y + c
        mine = pltpu.make_async_copy(x_ref, o_ref.at[me], local_sem)
        mine.start()
        copies = []
        for k in range(1, N_DEV):
            px, py, pc = x ^ ((k >> 2) & 1), y ^ ((k >> 1) & 1), c ^ (k & 1)
            cp = pltpu.make_async_remote_copy(
                src_ref=x_ref, dst_ref=o_ref.at[me], send_sem=send_sems.at[k - 1], recv_sem=recv_sems.at[k - 1],
                device_id=(px, py, pc), device_id_type=MESH)
            cp.start()
            copies.append((cp, 4 * px + 2 * py + pc))
        for k, (cp, peer) in enumerate(copies):
            pltpu.make_async_remote_copy(
                src_ref=x_ref, dst_ref=o_ref.at[peer], send_sem=send_sems.at[k], recv_sem=recv_sems.at[k],
                device_id=(x, y, c), device_id_type=MESH).wait_recv()
        for cp, _ in copies:
            cp.wait_send()
        mine.wait()

    return pl.pallas_call(
        body, name=name,
        in_specs=[pl.BlockSpec(memory_space=pltpu.VMEM)],
        out_specs=pl.BlockSpec(memory_space=pltpu.VMEM),
        out_shape=jax.ShapeDtypeStruct((N_DEV, r, LANES), F32),
        scratch_shapes=[pltpu.SemaphoreType.DMA((N_DEV - 1,)), pltpu.SemaphoreType.DMA((N_DEV - 1,)),
                        pltpu.SemaphoreType.DMA],
        compiler_params=pltpu.CompilerParams(vmem_limit_bytes=VMEM_LIMIT),
    )(buf)


def _cut(ref, shard_axis, *, chip=None, half=None, lead=None):
    shape = ref.shape[1:] if lead is not None else ref.shape
    idx = [slice(None), slice(None)]
    if chip is not None:
        w = shape[shard_axis] // N_CHIPS
        idx[shard_axis] = pl.ds(pl.multiple_of(chip * w, w), w)
    if half is not None:
        hw = shape[1 - shard_axis] // 2
        idx[1 - shard_axis] = pl.ds(pl.multiple_of(half * hw, hw), hw)
    if lead is not None:
        idx = [lead] + idx
    return ref.at[tuple(idx)]


def _wait_recv(ref, send_sem, recv_sem):
    pltpu.make_async_remote_copy(src_ref=ref, dst_ref=ref, send_sem=send_sem, recv_sem=recv_sem,
                                 device_id=_mesh_pos(), device_id_type=MESH).wait_recv()


def _gather_plan(axes):
    def plan(srcs, lands, send_sems, recv_sems):
        x, y, c = _mesh_pos()
        chip = 2 * x + y
        copies = []
        for i, ax in enumerate(axes):
            for k in range(1, N_CHIPS):
                px, py = x ^ (k >> 1), y ^ (k & 1)
                mine = _cut(lands[i], ax, chip=chip, half=c)
                j = i * (N_CHIPS - 1) + k - 1
                sems = dict(send_sem=send_sems.at[j], recv_sem=recv_sems.at[j], device_id=(px, py, c), device_id_type=MESH)
                send = pltpu.make_async_remote_copy(src_ref=mine, dst_ref=mine, **sems)
                recv = pltpu.make_async_remote_copy(src_ref=mine, dst_ref=_cut(lands[i], ax, chip=2 * px + py, half=c), **sems)
                copies.append((send, recv))
        return copies
    return plan


def _gather_near_plan(axes):
    def plan(srcs, lands, send_sems, recv_sems):
        x, y, c = _mesh_pos()
        copies = []
        for i, ax in enumerate(axes):
            mine = _cut(lands[i], ax, chip=2 * x + y, half=c)
            for k, (px, py) in enumerate(((1 - x, y), (x, 1 - y))):
                sems = dict(send_sem=send_sems.at[2 * i + k], recv_sem=recv_sems.at[2 * i + k], device_id=(px, py, c), device_id_type=MESH)
                send = pltpu.make_async_remote_copy(src_ref=mine, dst_ref=mine, **sems)
                recv = pltpu.make_async_remote_copy(src_ref=mine, dst_ref=_cut(lands[i], ax, chip=2 * px + py, half=c), **sems)
                copies.append((send, recv))
        return copies
    return plan


def _gather_far_plan(axes):
    def plan(srcs, lands, send_sems, recv_sems):
        x, y, c = _mesh_pos()
        from_chip = 2 * (x ^ (1 - c)) + (y ^ c)
        to = (x ^ c, y ^ (1 - c), c)
        diag = 2 * (1 - x) + (1 - y)
        copies = []
        for i, ax in enumerate(axes):
            passed = _cut(lands[i], ax, chip=from_chip, half=c)
            sems = dict(send_sem=send_sems.at[i], recv_sem=recv_sems.at[i], device_id=to, device_id_type=MESH)
            send = pltpu.make_async_remote_copy(src_ref=passed, dst_ref=passed, **sems)
            recv = pltpu.make_async_remote_copy(src_ref=passed, dst_ref=_cut(lands[i], ax, chip=diag, half=c), **sems)
            copies.append((send, recv))
        return copies
    return plan


def _forward_plan(axes, ks=(1, 2, 3)):
    def plan(srcs, lands, send_sems, recv_sems):
        x, y, c = _mesh_pos()
        copies = []
        for i, ax in enumerate(axes):
            for n, k in enumerate(ks):
                peer_chip = 2 * (x ^ (k >> 1)) + (y ^ (k & 1))
                j = i * len(ks) + n
                sems = dict(send_sem=send_sems.at[j], recv_sem=recv_sems.at[j], device_id=(x, y, 1 - c), device_id_type=MESH)
                landed = _cut(lands[i], ax, chip=peer_chip, half=c)
                send = pltpu.make_async_remote_copy(src_ref=landed, dst_ref=landed, **sems)
                recv = pltpu.make_async_remote_copy(src_ref=landed, dst_ref=_cut(lands[i], ax, chip=peer_chip, half=1 - c), **sems)
                copies.append((send, recv))
        return copies
    return plan


def _pair_plan(axes):
    def plan(srcs, lands, send_sems, recv_sems):
        x, y, c = _mesh_pos()
        copies = []
        for i, ax in enumerate(axes):
            cp = pltpu.make_async_remote_copy(
                src_ref=_cut(srcs[i], ax, half=1 - c), dst_ref=lands[i], send_sem=send_sems.at[i], recv_sem=recv_sems.at[i],
                device_id=(x, y, 1 - c), device_id_type=MESH)
            copies.append((cp, cp))
        return copies
    return plan


def _scatter_plan(axes):
    def plan(srcs, lands, send_sems, recv_sems):
        x, y, c = _mesh_pos()
        copies = []
        for i, ax in enumerate(axes):
            for k in range(1, N_CHIPS):
                px, py = x ^ (k >> 1), y ^ (k & 1)
                j = i * (N_CHIPS - 1) + k - 1
                cp = pltpu.make_async_remote_copy(
                    src_ref=_cut(srcs[i], ax, chip=2 * px + py), dst_ref=lands[i].at[k - 1],
                    send_sem=send_sems.at[j], recv_sem=recv_sems.at[j], device_id=(px, py, c), device_id_type=MESH)
                copies.append((cp, cp))
        return copies
    return plan


HBM = pl.BlockSpec(memory_space=pltpu.HBM)
SEM = pl.BlockSpec(memory_space=pltpu.SEMAPHORE)
EFFECT = pltpu.SideEffectType.DATAFLOW_SIDE_EFFECTING


def _in_hbm(arrays):
    return [pltpu.with_memory_space_constraint(a, pltpu.HBM) for a in arrays]


def _split_start(srcs, lands, plan, n_copies, *, name):
    bufs = list(srcs) + list(lands)
    ns, nb = len(srcs), len(bufs)

    def body(*refs):
        send_sems, recv_sems, token = refs[nb], refs[nb + 1], refs[-1]
        for send, _ in plan(refs[:ns], refs[ns:nb], send_sems, recv_sems):
            send.start()
        token[...] = jnp.zeros_like(token)

    sems = pltpu.SemaphoreType.DMA((n_copies,))
    res = pl.pallas_call(
        body, name=name, in_specs=[HBM] * nb,
        out_specs=[SEM, SEM] + [HBM] * nb + [pl.BlockSpec(memory_space=pltpu.VMEM)],
        out_shape=[sems, sems] + [pltpu.HBM(a.shape, a.dtype) for a in bufs] + [jax.ShapeDtypeStruct((8, LANES), F32)],
        input_output_aliases={j: 2 + j for j in range(nb)},
        compiler_params=pltpu.CompilerParams(has_side_effects=EFFECT),
    )(*_in_hbm(bufs))
    return res[0], res[1], res[2:2 + ns], res[2 + ns:2 + nb], res[-1]


def _split_wait(started, after, plan, *, name, with_srcs=False):
    send_sems, recv_sems, srcs, lands, _ = started
    bufs = list(srcs) + list(lands)
    ns, nb = len(srcs), len(bufs)

    def body(*refs):
        for send, recv in plan(refs[:ns], refs[ns:nb], refs[nb], refs[nb + 1]):
            send.wait_send()
            recv.wait_recv()

    res = pl.pallas_call(
        body, name=name, in_specs=[HBM] * nb + [SEM, SEM, ANY], out_specs=[HBM] * nb,
        out_shape=[pltpu.HBM(a.shape, a.dtype) for a in bufs],
        input_output_aliases={j: j for j in range(nb)},
        compiler_params=pltpu.CompilerParams(has_side_effects=EFFECT),
    )(*bufs, send_sems, recv_sems, after)
    return (res[:ns], res[ns:]) if with_srcs else res[ns:]


def _cast_into_full(w3, layer, ax, chip, *, after=None, name):
    _, r, wd = w3.shape
    tr = _rows_per_tile(r, wd, 4 << 20)
    nt = r // tr
    full_shape = (r, wd * N_CHIPS) if ax == 1 else (r * N_CHIPS, wd)
    out_map = (lambda i, ch: (i, ch[0])) if ax == 1 else (lambda i, ch: (ch[0] * nt + i, 0))
    zero = jnp.zeros((1, wd), F32) + (0.0 if after is None else after)

    def body(chip_ref, w_ref, z_ref, o_ref):
        o_ref[...] = (w_ref[...] + z_ref[...]).astype(o_ref.dtype)

    return pl.pallas_call(
        body, name=name,
        grid_spec=pltpu.PrefetchScalarGridSpec(
            num_scalar_prefetch=1, grid=(nt,),
            in_specs=[pl.BlockSpec((None, tr, wd), lambda i, ch: (layer, i, 0)), pl.BlockSpec((1, wd), lambda i, ch: (0, 0))],
            out_specs=pl.BlockSpec((tr, wd), out_map)),
        out_shape=jax.ShapeDtypeStruct(full_shape, BF16),
        compiler_params=_params(("parallel",)),
    )(jnp.reshape(chip, (1,)).astype(jnp.int32), w3, zero)


def _forward_halves(fulls, axes, *, name, ks=(1, 2, 3)):
    n = len(fulls)

    def body(*refs):
        bufs = refs[:n]
        send_sems, recv_sems = refs[2 * n:]
        x, y, c = _mesh_pos()
        sends = []
        for i in range(n):
            for k in ks:
                landed = _cut(bufs[i], axes[i], chip=2 * (x ^ (k >> 1)) + (y ^ (k & 1)), half=c)
                cp = pltpu.make_async_remote_copy(
                    src_ref=landed, dst_ref=landed, send_sem=send_sems.at[i, k - 1], recv_sem=recv_sems.at[i, k - 1],
                    device_id=(x, y, 1 - c), device_id_type=MESH)
                cp.start()
                sends.append(cp)
        for i in range(n):
            for k in ks:
                other = _cut(bufs[i], axes[i], chip=2 * (x ^ (k >> 1)) + (y ^ (k & 1)), half=1 - c)
                _wait_recv(other, send_sems.at[i, k - 1], recv_sems.at[i, k - 1])
        for cp in sends:
            cp.wait_send()

    pairs = pltpu.SemaphoreType.DMA((n, N_CHIPS - 1))
    return pl.pallas_call(
        body, name=name, in_specs=[ANY] * n, out_specs=[ANY] * n,
        out_shape=[jax.ShapeDtypeStruct(a.shape, a.dtype) for a in fulls],
        input_output_aliases={j: j for j in range(n)},
        scratch_shapes=[pairs, pairs],
    )(*fulls)


def _share_halves_in_place(bufs, axes, *, name):
    n = len(bufs)

    def body(*refs):
        ins = refs[:n]
        send_sems, recv_sems = refs[2 * n:]
        x, y, c = _mesh_pos()
        sends = []
        for i in range(n):
            mine = _cut(ins[i], axes[i], half=c)
            cp = pltpu.make_async_remote_copy(
                src_ref=mine, dst_ref=mine, send_sem=send_sems.at[i], recv_sem=recv_sems.at[i],
                device_id=(x, y, 1 - c), device_id_type=MESH)
            cp.start()
            sends.append(cp)
        for i in range(n):
            _wait_recv(_cut(ins[i], axes[i], half=1 - c), send_sems.at[i], recv_sems.at[i])
        for cp in sends:
            cp.wait_send()

    sems = pltpu.SemaphoreType.DMA((n,))
    return pl.pallas_call(
        body, name=name, in_specs=[ANY] * n, out_specs=[ANY] * n,
        out_shape=[jax.ShapeDtypeStruct(b.shape, b.dtype) for b in bufs],
        input_output_aliases={j: j for j in range(n)}, scratch_shapes=[sems, sems],
    )(*bufs)


def _adamw_math(w, g, m, v):
    m = ADAM_B1 * m + (1.0 - ADAM_B1) * g
    v = ADAM_B2 * v + (1.0 - ADAM_B2) * (g * g)
    m_hat = m / (1.0 - ADAM_B1 ** ADAM_STEP)
    v_hat = v / (1.0 - ADAM_B2 ** ADAM_STEP)
    delta = -ADAM_LR * (m_hat / (jnp.sqrt(v_hat) + ADAM_EPS) + ADAM_WD * w)
    return delta, m, v


def _adamw_layer(w3, m3, v3, p, q, layer, prev, *, name):
    nl, rows, width = w3.shape
    tr = _rows_per_tile(rows, width)

    def fn(*t):
        if q is None:
            w, m, v, g = t
        else:
            w, m, v, g, g2 = t
            g = g + g2
        delta, m, v = _adamw_math(w, g, m, v)
        return g, delta, m, v

    ins = [('t', w3, 0, width, layer), ('t', m3, 0, width, layer), ('t', v3, 0, width, layer), ('t', p, 0, width)]
    if q is not None:
        ins.append(('t', q, 0, width))
    outs = [('t', width, F32, layer, nl)] * 4
    aliases = None if prev is None else [(prev[i], i) for i in range(4)]
    return _ew(fn, ins, outs, rows=rows, tr=tr, name=name, aliases=aliases)


def _pack_rows(vec):
    n = vec.shape[0]
    r = -(-n // (8 * LANES)) * 8
    return jnp.pad(vec, (0, r * LANES - n)).reshape(r, LANES)


def kernel(x, c, ctx, c_ctx, ada_w, ada_b, norm_g, w_in, na_rpb, ret_decay_logit, w_proj_na, w_proj_ret, w_out, final_g, loss_target, m_c_ctx, m_ada_w, m_ada_b, m_norm_g, m_w_in, m_na_rpb, m_ret_decay_logit, m_w_proj_na, m_w_proj_ret, m_w_out, m_final_g, v_c_ctx, v_ada_w, v_ada_b, v_norm_g, v_w_in, v_na_rpb, v_ret_decay_logit, v_w_proj_na, v_w_proj_ret, v_w_out, v_final_g):
    depth = w_in.shape[0]
    s_len, d_model = x.shape[1], x.shape[2]
    l_len = ctx.shape[1]
    t_len = s_len + l_len
    na_heads = na_rpb.shape[1]
    ret_heads = ret_decay_logit.shape[2]
    w_na = na_heads * NA_HEAD_DIM
    w_qk = ret_heads * RET_KEY_DIM
    w_v = ret_heads * RET_VAL_DIM
    in_cols = w_in.shape[2] * N_CHIPS
    assert in_cols == 4 * w_na + 2 * w_qk + 2 * w_v + 2 * d_model
    assert x.shape[0] == 1 and s_len % (NA_WIN_ROWS * GRID_W) == 0 and l_len % RET_CHUNK == 0
    off = np.cumsum([0, w_na, w_na, w_na, w_na, w_qk, w_qk, w_v, w_v, d_model, d_model])
    o_naz, o_retq, o_retz, o_gna, o_gret = int(off[3]), int(off[4]), int(off[7]), int(off[8]), int(off[9])
    rows = s_len // GRID_W
    tr = _tile(l_len, 256, 8)
    n0 = s_len // tr
    mod_cols = 3 * d_model
    mod_shard = ada_w.shape[2]

    xi, yi, ci = _mesh_pos()
    me = 4 * xi + 2 * yi + ci
    chip = 2 * xi + yi

    big_axes = [1, 1, 0, 0]
    n_big = len(big_axes) * (N_CHIPS - 1)
    gather_plan, scatter_plan = _gather_plan(big_axes), _scatter_plan(big_axes)

    c_silu = c[0] * _sigmoid(c[0])
    cc_silu = c_ctx * _sigmoid(c_ctx)
    c_all = _all_gather_small(_pack_rows(c_silu), name="gather_c")[:, :d_model // LANES].reshape(N_DEV, d_model)
    a_rows = jnp.concatenate([c_all, cc_silu[None], jnp.zeros((16 - N_DEV - 1, d_model), F32)], axis=0)
    mod_part = jnp.stack([_mm(a_rows, ada_w, b_lead=l, out_dtype=F32, name="ada_fwd_%d" % l) for l in range(depth)])
    mod_all = _all_gather_small(_pack_rows(mod_part.reshape(-1)), name="gather_mod")
    n_mod = depth * 16 * mod_shard
    mod_all = mod_all.reshape(N_DEV, -1)[:, :n_mod].reshape(N_CHIPS, 2, depth, 16, mod_shard)[:, 0]
    mod_all = jnp.transpose(mod_all, (1, 2, 0, 3)).reshape(depth, 16, mod_cols) + ada_b[:, None, :]

    big_named = list(zip((w_in, w_proj_na, w_proj_ret, w_out), big_axes, ("w_in", "w_proj_na", "w_proj_ret", "w_out")))
    w_in0 = _cast_into_full(w_in, 0, big_axes[0], chip, name="cast_w_in_0")
    mod_all, w_in0 = lax.optimization_barrier((mod_all, w_in0))
    plan_near, plan_far, plan_rest = _gather_near_plan(big_axes[:1]), _gather_far_plan(big_axes[:1]), _gather_plan(big_axes[1:])
    near_all, far_all, forward_all = _gather_near_plan(big_axes), _gather_far_plan(big_axes), _forward_plan(big_axes)
    first_gather = _split_start([], [w_in0], plan_near, 2, name="gather_start_0_in")
    start_token = first_gather[4][0, 0]
    fulls = [[None if (l == 0 and tag == "w_in") else _cast_into_full(w, l, ax, chip, after=start_token, name="cast_%s_%d" % (tag, l))
              for w, ax, tag in big_named] for l in range(depth)]
    mod_lat = lax.dynamic_index_in_dim(mod_all, me, axis=1, keepdims=False)
    mod_ctx = mod_all[:, N_DEV]
    biases = [_na_bias_layout(_na_bias_table(na_rpb[l], s_len // GRID_W, name="na_bias_%d" % l)) for l in range(depth)]
    biases, fulls = lax.optimization_barrier((biases, fulls))
    landed_near = _split_wait(first_gather, biases[-1], plan_near, name="gather_wait_0_in")
    passing = _split_start([], landed_near, plan_far, 1, name="gather_pass_0_in")
    forward_near = _forward_plan(big_axes[:1], ks=(1, 2))
    near_swap = _split_start([], passing[3], forward_near, 2, name="gather_forward_near_0_in")
    front_token = passing[4][0, 0] + near_swap[4][0, 0]

    c2, s2 = _rope_tables(s_len, l_len)
    log_gamma = jax.nn.log_sigmoid(ret_decay_logit)
    x_all = jnp.concatenate([x[0], ctx[0]], axis=0)

    def grp(lat_vec, ctx_vec):
        return jnp.stack([lat_vec, ctx_vec])[:, None, :]

    saved, full_w = [], []
    for l in range(depth):
        shift, scale, gate = [grp(mod_lat[l, i * d_model:(i + 1) * d_model], mod_ctx[l, i * d_model:(i + 1) * d_model])
                              for i in range(3)]
        gs = norm_g[l][None, None, :] * (1.0 + scale) + front_token

        def modnorm(xt, gs_t, sh_t):
            r = lax.rsqrt(jnp.mean(xt * xt, axis=-1, keepdims=True) + NORM_EPS)
            return xt * r * gs_t + sh_t

        h, = _ew(modnorm, [('t', x_all, 0, d_model), ('g', gs), ('g', shift)], [('t', d_model, BF16)],
                 rows=t_len, tr=tr, n0=n0, name="modnorm_%d" % l)
        bias = biases[l]
        if l == 0:
            h, bias = lax.optimization_barrier((h, bias))
            landed_far = _split_wait((passing[0], passing[1], [], near_swap[3], None), h, plan_far, name="gather_wait_0_in_far")
            landed_in = _split_wait((near_swap[0], near_swap[1], [], landed_far, None), h, forward_near,
                                    name="gather_forward_near_wait_0_in")
            landed_in, rest0, later = lax.optimization_barrier((landed_in, fulls[0][1:], fulls[1:]))
            rest_gather = _split_start([], rest0, plan_rest, n_big - (N_CHIPS - 1), name="gather_start_0_rest")
            later_gathers = [_split_start([], later[j], near_all, 2 * len(big_axes), name="gather_start_%d" % (j + 1))
                             for j in range(depth - 1)]
            win_f, = _forward_halves(landed_in, big_axes[:1], name="gather_forward_0_in", ks=(3,))
            win_f, tokens = lax.optimization_barrier((win_f, [rest_gather[4]] + [g[4] for g in later_gathers]))
            gate = gate + sum(t[0, 0] for t in tokens)
        else:
            h, bias = lax.optimization_barrier((h, bias))
            win_f, wpn_f, wpr_f, wout_f = _split_wait(next_forward, h, forward_all, name="gather_forward_wait_%d" % l)
        u = _mm(h, win_f, tm=1152, tn=1024, name="in_proj_%d" % l)
        o_na = _na_fwd(u, bias, s_len=s_len, heads=na_heads, name="na_fwd_%d" % l)
        o_ret, states = _ret_fwd(u, c2, s2, log_gamma[l], s_len=s_len, heads=ret_heads, q_off=o_retq, name="ret_fwd_%d" % l)

        def act(o1, z1, o2, z2):
            a1 = o1.astype(F32) * _silu_parts(z1.astype(F32))[0]
            sz = _silu_parts(z2.astype(F32))[0]
            outs = []
            for hh in range(ret_heads):
                sl = slice(hh * RET_VAL_DIM, (hh + 1) * RET_VAL_DIM)
                oh = o2[:, sl]
                r = lax.rsqrt(jnp.mean(oh * oh, axis=-1, keepdims=True) + NORM_EPS)
                outs.append(oh * r * sz[:, sl])
            return a1, jnp.concatenate(outs, axis=-1)

        a_na, a_ret = _ew(act, [('t', o_na, 0, w_na), ('t', u, o_naz // w_na, w_na), ('t', o_ret, 0, w_v), ('t', u, o_retz // w_v, w_v)],
                          [('t', w_na, BF16), ('t', w_v, BF16)], rows=t_len, tr=tr, name="act_%d" % l)
        if l == 0:
            landed_rest = _split_wait(rest_gather, a_na, plan_rest, name="gather_wait_0_rest")
            later_passes = [_split_start([], _split_wait(later_gathers[j], a_na, near_all, name="gather_near_%d" % (j + 1)),
                                         far_all, len(big_axes), name="gather_pass_%d" % (j + 1)) for j in range(depth - 1)]
            landed_rest, tokens = lax.optimization_barrier((landed_rest, [g[4] for g in later_passes]))
            gate = gate + sum(t[0, 0] for t in tokens)
            wpn_f, wpr_f, wout_f = _forward_halves(landed_rest, big_axes[1:], name="gather_forward_0_rest")
        full_w.append((win_f, wpn_f, wpr_f, wout_f))
        y_na = _mm(a_na, wpn_f, name="proj_na_%d" % l)
        y_ret = _mm(a_ret, wpr_f, name="proj_ret_%d" % l)

        def merge(y1, y2, g1, g2):
            return _sigmoid(g1.astype(F32)) * y1.astype(F32) + _sigmoid(g2.astype(F32)) * y2.astype(F32)

        merged, = _ew(merge, [('t', y_na, 0, d_model), ('t', y_ret, 0, d_model), ('t', u, o_gna // d_model, d_model), ('t', u, o_gret // d_model, d_model)],
                      [('t', d_model, BF16)], rows=t_len, tr=tr, name="merge_%d" % l)
        out = _mm(merged, wout_f, out_dtype=F32, name="out_proj_%d" % l)
        if l + 1 < depth:
            landed = _split_wait(later_passes[l], out, far_all, name="gather_wait_%d" % (l + 1))
            next_forward = _split_start([], landed, forward_all, n_big, name="gather_forward_%d" % (l + 1))
            gate = gate + next_forward[4][0, 0]
        x_new, = _ew(lambda xt, ot, gt: xt + gt * ot, [('t', x_all, 0, d_model), ('t', out, 0, d_model), ('g', gate)],
                     [('t', d_model, F32)], rows=t_len, tr=tr, n0=n0, name="resid_%d" % l)
        saved.append(dict(x=x_all, h=h, u=u, bias=bias, o_na=o_na, o_ret=o_ret, states=states, a_na=a_na, a_ret=a_ret,
                          y_na=y_na, y_ret=y_ret, merged=merged, out=out, gate=gate, gs=gs, scale=scale))
        x_all = x_new

    def final(xt, tt, gt):
        r = lax.rsqrt(jnp.mean(xt * xt, axis=-1, keepdims=True) + NORM_EPS)
        xh = xt * r
        e = xh * gt - tt
        dy = e * (1.0 / d_model)
        dyg = dy * gt
        dx = r * (dyg - xh * jnp.mean(dyg * xh, axis=-1, keepdims=True))
        return dx, _rsum(dy * xh), _rsum(e * e)

    dx_lat, d_final_g, loss_cols = _ew(final, [('t', x_all, 0, d_model), ('t', loss_target[0], 0, d_model), ('g', final_g[None, None, :])],
                                       [('t', d_model, F32), ('r', d_model, 1), ('r', d_model, 1)], rows=s_len, tr=tr, name="final")
    loss_part = (0.5 / d_model) * jnp.sum(loss_cols)
    dx_all = jnp.concatenate([dx_lat, jnp.zeros((l_len, d_model), F32)], axis=0)

    big_w = [(w_in, m_w_in, v_w_in), (w_proj_na, m_w_proj_na, v_w_proj_na), (w_proj_ret, m_w_proj_ret, v_w_proj_ret), (w_out, m_w_out, v_w_out)]
    big_res = [None] * 4
    scatters = {}
    back_token = jnp.zeros((), F32)

    pairs = {}

    def start_pair(key, grads, axes):
        plan = _pair_plan(axes)
        lands = []
        for g, ax in zip(grads, axes):
            shp = list(g.shape)
            shp[1 - ax] //= 2
            lands.append(lax.empty(tuple(shp), BF16))
        pairs[key] = (_split_start(grads, lands, plan, len(axes), name="pair_start_%s" % key), axes, plan)
        return pairs[key][0][4]

    def start_scatter(key, after):
        started, axes, pair_plan = pairs[key]
        grads, theirs = _split_wait(started, after, pair_plan, name="pair_wait_%s" % key, with_srcs=True)
        plan = _scatter_plan(axes)
        pair = [_sum_pair(g, t, ax, ci, name="sum_pair_%s_%d" % (key, i)) for i, (g, t, ax) in enumerate(zip(grads, theirs, axes))]
        own = [lax.dynamic_slice_in_dim(s, chip * (s.shape[ax] // N_CHIPS), s.shape[ax] // N_CHIPS, axis=ax) for s, ax in zip(pair, axes)]
        lands = [lax.empty((N_CHIPS - 1,) + o.shape, BF16) for o in own]
        started = _split_start(pair, lands, plan, len(axes) * (N_CHIPS - 1), name="scatter_start_%s" % key)
        scatters[key] = (started, own, axes, plan)
        return started[4]

    def finish_scatter(key, after):
        started, own, axes, plan = scatters[key]
        recv = _split_wait(started, after, plan, name="scatter_wait_%s" % key)
        bufs = [_sum_chips_into(own[i], rbuf, axes[i], ci, name="sum_chips_%s_%d" % (key, i)) for i, rbuf in enumerate(recv)]
        return _share_halves_in_place(bufs, axes, name="share_halves_%s" % key)

    def adamw_big(l, idx, grads, big_res):
        for i, g in zip(idx, grads):
            w3, m3, v3 = big_w[i]
            big_res[i] = _adamw_layer(w3, m3, v3, g, None, l, big_res[i], name="adamw_big_%d_%d" % (i, l))
        return big_res

    small = dict(dmod_lat=[None] * depth, dmod_ctx=[None] * depth, dnorm_g=[None] * depth, drpb=[None] * depth, ddecay=[None] * depth)
    for l in reversed(range(depth)):
        sv = saved[l]
        win_f, wpn_f, wpr_f, wout_f = full_w[l]

        def resid_bwd(dxt, ot, gt):
            return gt * dxt, _rsum(dxt * ot)

        dout, dgate = _ew(resid_bwd, [('t', dx_all, 0, d_model), ('t', sv['out'], 0, d_model), ('g', sv['gate'] + back_token)],
                          [('t', d_model, BF16), ('r', d_model, 2)], rows=t_len, tr=tr, n0=n0, name="resid_bwd_%d" % l)
        dmerged = _mm(dout, wout_f, tb=True, name="out_proj_dx_%d" % l)
        g_wout = _mm(sv['merged'], dout, ta=True, tm=1024, tk=t_len, name="out_proj_dw_%d" % l)

        def merge_bwd(dm, y1, y2, g1, g2):
            dm = dm.astype(F32)
            s1, s2_ = _sigmoid(g1.astype(F32)), _sigmoid(g2.astype(F32))
            return dm * s1, dm * s2_, dm * y1.astype(F32) * s1 * (1.0 - s1), dm * y2.astype(F32) * s2_ * (1.0 - s2_)

        u = sv['u']
        dy_na, dy_ret, dg_na, dg_ret = _ew(
            merge_bwd, [('t', dmerged, 0, d_model), ('t', sv['y_na'], 0, d_model), ('t', sv['y_ret'], 0, d_model),
                        ('t', u, o_gna // d_model, d_model), ('t', u, o_gret // d_model, d_model)],
            [('t', d_model, BF16)] * 4, rows=t_len, tr=tr, name="merge_bwd_%d" % l)
        da_na = _mm(dy_na, wpn_f, tb=True, name="proj_na_dx_%d" % l)
        g_wpn = _mm(sv['a_na'], dy_na, ta=True, tm=1024, tk=t_len, name="proj_na_dw_%d" % l)
        da_ret = _mm(dy_ret, wpr_f, tb=True, name="proj_ret_dx_%d" % l)
        g_wpr = _mm(sv['a_ret'], dy_ret, ta=True, tm=1024, tk=t_len, name="proj_ret_dw_%d" % l)
        lg_l = log_gamma[l]
        if l == 0:
            pair_token = start_pair("0_rest", [g_wpn, g_wpr, g_wout], big_axes[1:])
            da_na, pair_token = lax.optimization_barrier((da_na, pair_token))

        def act_bwd(da1, o1, z1, da2, o2, z2):
            da1, da2 = da1.astype(F32), da2.astype(F32)
            si1, ds1 = _silu_parts(z1.astype(F32))
            si2, ds2 = _silu_parts(z2.astype(F32))
            do1 = da1 * si1
            dz1 = da1 * o1.astype(F32) * ds1
            dn = da2 * si2
            do2, dz2 = [], []
            for hh in range(ret_heads):
                sl = slice(hh * RET_VAL_DIM, (hh + 1) * RET_VAL_DIM)
                oh = o2[:, sl]
                r = lax.rsqrt(jnp.mean(oh * oh, axis=-1, keepdims=True) + NORM_EPS)
                nh = oh * r
                dz2.append(da2[:, sl] * nh * ds2[:, sl])
                do2.append(r * (dn[:, sl] - nh * jnp.mean(dn[:, sl] * nh, axis=-1, keepdims=True)))
            return do1, dz1, jnp.concatenate(do2, axis=-1), jnp.concatenate(dz2, axis=-1)

        do_na, dz_na, do_ret, dz_ret = _ew(
            act_bwd, [('t', da_na, 0, w_na), ('t', sv['o_na'], 0, w_na), ('t', u, o_naz // w_na, w_na),
                      ('t', da_ret, 0, w_v), ('t', sv['o_ret'], 0, w_v), ('t', u, o_retz // w_v, w_v)],
            [('t', w_na, BF16), ('t', w_na, BF16), ('t', w_v, BF16), ('t', w_v, BF16)], rows=t_len, tr=tr, name="act_bwd_%d" % l)
        dq_na, dk_na, dv_na, dbias = _na_bwd(u, sv['bias'], sv['o_na'], do_na, s_len=s_len, heads=na_heads, name="na_bwd_%d" % l)
        small['drpb'][l] = _rpb_grad(dbias, name="rpb_grad_%d" % l)
        if l == 0:
            lg_l = lg_l + start_scatter("0_rest", dq_na)[0, 0] + pair_token[0, 0]
        dq_r, dk_r, dv_r, dlg = _ret_bwd(u, c2, s2, lg_l, sv['states'], do_ret, s_len=s_len, heads=ret_heads,
                                         q_off=o_retq, name="ret_bwd_%d" % l)
        small['ddecay'][l] = jnp.transpose(dlg[:, :, 0, 0]) * _sigmoid(-ret_decay_logit[l])
        du_parts = [dq_na, dk_na, dv_na, dz_na, dq_r, dk_r, dv_r, dz_ret, dg_na, dg_ret]
        du, = _ew(lambda *t: jnp.concatenate(t, axis=-1), [('t', p, 0, p.shape[1]) for p in du_parts], [('t', in_cols, BF16)],
                  rows=t_len, tr=tr, name="du_concat_%d" % l)
        g_win = _mm(sv['h'], du, ta=True, tm=1024, tn=1024, tk=t_len, name="in_proj_dw_%d" % l)
        if l > 0:
            du, pair_token = lax.optimization_barrier((du, start_pair("%d_all" % l, [g_win, g_wpn, g_wpr, g_wout], big_axes)))
        else:
            du, in_token = lax.optimization_barrier((du, start_pair("0_in", [g_win], big_axes[:1])))
        dh = _mm(du, win_f, tb=True, out_dtype=F32, tm=1152, tn=1024, name="in_proj_dx_%d" % l)

        def modnorm_bwd(xt, dht, dxt, gs_t):
            r = lax.rsqrt(jnp.mean(xt * xt, axis=-1, keepdims=True) + NORM_EPS)
            xh = xt * r
            dhg = dht * gs_t
            dx = r * (dhg - xh * jnp.mean(dhg * xh, axis=-1, keepdims=True)) + dxt
            return dx, _rsum(dht), _rsum(dht * xh)

        dx_all, dshift, dgs = _ew(modnorm_bwd, [('t', sv['x'], 0, d_model), ('t', dh, 0, d_model), ('t', dx_all, 0, d_model), ('g', sv['gs'])],
                                  [('t', d_model, F32), ('r', d_model, 2), ('r', d_model, 2)], rows=t_len, tr=tr, n0=n0, name="modnorm_bwd_%d" % l)
        dscale = dgs * norm_g[l][None, None, :]
        small['dnorm_g'][l] = jnp.sum(dgs * (1.0 + sv['scale']), axis=(0, 1))
        dmod = jnp.concatenate([dshift, dscale, dgate], axis=-1)[:, 0]
        small['dmod_lat'][l], small['dmod_ctx'][l] = dmod[0], dmod[1]

        if l > 0:
            back_token = start_scatter("%d_all" % l, dx_all)[0, 0] + pair_token[0, 0]

    grad_x = dx_all[:s_len][None]

    drpb = jnp.stack(small['drpb']).reshape(-1)
    ddecay = jnp.stack(small['ddecay']).reshape(-1)
    pieces = [jnp.stack(small['dmod_lat']).reshape(-1), jnp.stack(small['dmod_ctx']).reshape(-1),
              jnp.stack(small['dnorm_g']).reshape(-1), d_final_g.reshape(-1), drpb, ddecay, loss_part[None]]
    sizes = [int(p.shape[0]) for p in pieces]
    pads = [-(-s // LANES) * LANES for s in sizes]
    packed = jnp.concatenate([jnp.pad(p, (0, pd - s)) for p, s, pd in zip(pieces, sizes, pads)])
    gathered = _all_gather_small(_pack_rows(packed), name="gather_small_grads")
    r_small = gathered.shape[1]

    def sum8(*t):
        acc = t[0]
        for other in t[1:]:
            acc = acc + other
        return acc

    total, = _ew(sum8, [('t', gathered, 0, LANES, k) for k in range(N_DEV)], [('t', LANES, F32)], rows=r_small, tr=r_small, name="sum_devices")
    total = total.reshape(-1)
    starts = np.cumsum([0] + pads)
    g_mod_lat_sum, g_mod_ctx, g_norm_g, g_final_g, g_rpb, g_decay, loss = [total[starts[i]:starts[i] + sizes[i]] for i in range(len(pieces))]
    loss = loss[0]
    g_ada_b = (g_mod_lat_sum + g_mod_ctx).reshape(depth, mod_cols)
    g_mod_ctx = g_mod_ctx.reshape(depth, mod_cols)
    dmod_lat_all = gathered.reshape(N_DEV, -1)[:, :depth * mod_cols].reshape(N_DEV, depth, mod_cols)

    dcc_part = jnp.zeros((16, d_model), F32)
    ctx_cols = [lax.dynamic_slice_in_dim(g_mod_ctx[l], chip * mod_shard, mod_shard, axis=0) for l in range(depth)]
    for l in reversed(range(depth)):
        c_rows = jnp.concatenate([ctx_cols[l][None], jnp.zeros((15, mod_shard), F32)], axis=0)
        dcc_part = dcc_part + _mm(c_rows, ada_w, tb=True, b_lead=l, out_dtype=F32, name="ada_dc_%d" % l)
    dcc_all = _all_gather_small(_pack_rows(dcc_part[0]), name="gather_dcc")[:, :d_model // LANES].reshape(N_CHIPS, 2, d_model)[:, 0]

    tail_token = start_scatter("0_in", dcc_all) + in_token
    dcc = ((dcc_all[0] + dcc_all[1]) + dcc_all[2]) + dcc_all[3]
    sg = _sigmoid(c_ctx)
    g_c_ctx = dcc * (sg * (1.0 + c_ctx * (1.0 - sg)))
    for l in reversed(range(1, depth)):
        big_res = adamw_big(l, range(4), finish_scatter("%d_all" % l, tail_token), big_res)

    ada_res = None
    for l in reversed(range(depth)):
        lat_cols = lax.dynamic_slice_in_dim(dmod_lat_all[:, l], chip * mod_shard, mod_shard, axis=1)
        d_rows = jnp.concatenate([lat_cols, ctx_cols[l][None], jnp.zeros((16 - N_DEV - 1, mod_shard), F32)], axis=0) + tail_token[0, 0]
        g_ada = _mm(a_rows, d_rows, ta=True, out_dtype=F32, tm=512, name="ada_dw_%d" % l)
        ada_res = _adamw_layer(ada_w, m_ada_w, v_ada_w, g_ada, None, l, ada_res, name="adamw_ada_%d" % l)

    small_w = [(c_ctx, m_c_ctx, v_c_ctx, g_c_ctx), (ada_b, m_ada_b, v_ada_b, g_ada_b),
               (norm_g, m_norm_g, v_norm_g, g_norm_g), (na_rpb, m_na_rpb, v_na_rpb, g_rpb),
               (ret_decay_logit, m_ret_decay_logit, v_ret_decay_logit, g_decay), (final_g, m_final_g, v_final_g, g_final_g)]
    sw_sizes = [int(np.prod(t[0].shape)) for t in small_w]
    sw_pads = [-(-s // LANES) * LANES for s in sw_sizes]

    def pack(j):
        return _pack_rows(jnp.concatenate([jnp.pad(t[j].reshape(-1), (0, pd - s)) for t, s, pd in zip(small_w, sw_sizes, sw_pads)]))

    pw_, pm_, pv_, pg_ = pack(0), pack(1), pack(2), pack(3)
    sw_out = _ew(lambda w, m, v, g: (g,) + _adamw_math(w, g, m, v),
                 [('t', pw_, 0, LANES), ('t', pm_, 0, LANES), ('t', pv_, 0, LANES), ('t', pg_, 0, LANES)],
                 [('t', LANES, F32)] * 4, rows=pw_.shape[0], tr=pw_.shape[0], name="adamw_small")
    sw_starts = np.cumsum([0] + sw_pads)
    sw_out, ada_res, big_res = lax.optimization_barrier((sw_out, ada_res, big_res))
    big_res = adamw_big(0, range(1, 4), finish_scatter("0_rest", sw_out[0]), big_res)
    big_res = adamw_big(0, range(1), finish_scatter("0_in", sw_out[1]), big_res)

    def unpack(arr, i):
        return arr.reshape(-1)[sw_starts[i]:sw_starts[i] + sw_sizes[i]].reshape(small_w[i][0].shape)

    sm = [[unpack(sw_out[j], i) for i in range(len(small_w))] for j in range(4)]
    def ordered(j):
        return [sm[j][0], ada_res[j], sm[j][1], sm[j][2], big_res[0][j], sm[j][3], sm[j][4],
                big_res[1][j], big_res[2][j], big_res[3][j], sm[j][5]]

    return (loss, grad_x, *ordered(0), *ordered(1), *ordered(2), *ordered(3))
```

```python
import functools
import math

import numpy as np
import jax
import jax.numpy as jnp
from jax import lax
from jax.experimental import pallas as pl
from jax.experimental.pallas import tpu as pltpu

GRID_W = 64
NA_HEAD_DIM = 128
NA_WIN_ROWS = 8
NA_WIN_COLS = 16
NA_GROUP = 8
RET_GROUPS = (1, 2, 3)
RET_KEY_DIM = 128
RET_VAL_DIM = 256
RET_CHUNK = 128
ROPE_BASE = 10000.0
NORM_EPS = 1e-6
MASK_VALUE = -1e30
ADAM_LR = 0.001
ADAM_B1 = 0.9
ADAM_B2 = 0.999
ADAM_EPS = 1e-08
ADAM_WD = 0.01
ADAM_STEP = 10

N_CHIPS = 4
N_DEV = 8
LANES = 128
VMEM_LIMIT = 56 * 1024 * 1024
BF16 = jnp.bfloat16
F32 = jnp.float32
MESH = pl.DeviceIdType.MESH
ANY = pl.BlockSpec(memory_space=pl.ANY)


def _tile(dim, pref, align=LANES):
    if dim <= pref:
        return dim
    t = (pref // align) * align
    while t >= align:
        if dim % t == 0:
            return t
        t -= align
    return dim


def _rows_per_tile(rows, width, tile_bytes=1 << 20):
    return _tile(rows, max(8, tile_bytes // (4 * width)), 8)


def _params(sem):
    return pltpu.CompilerParams(dimension_semantics=sem, vmem_limit_bytes=VMEM_LIMIT)


def _sigmoid(x):
    return 1.0 / (1.0 + jnp.exp(-x))


def _dot(a, b, ca, cb):
    return lax.dot_general(a, b, (((ca,), (cb,)), ((), ())), preferred_element_type=F32)


def _mm(a, b, *, ta=False, tb=False, a_lead=None, b_lead=None, out_dtype=BF16, tm=1152, tn=1024, tk=2048, name):
    ash = a.shape[1:] if a_lead is not None else a.shape
    bsh = b.shape[1:] if b_lead is not None else b.shape
    m, k = (ash[1], ash[0]) if ta else ash
    n, k2 = bsh if tb else (bsh[1], bsh[0])
    assert k == k2, (name, ash, bsh)
    tm, tn, tk = _tile(m, tm), _tile(n, tn), _tile(k, tk)
    nk = k // tk

    def lead(spec_shape, imap, l):
        if l is None:
            return pl.BlockSpec(spec_shape, imap)
        return pl.BlockSpec((None,) + spec_shape, lambda i, j, kk: (l,) + imap(i, j, kk))

    a_spec = lead((tk, tm), lambda i, j, kk: (kk, i), a_lead) if ta else lead((tm, tk), lambda i, j, kk: (i, kk), a_lead)
    b_spec = lead((tn, tk), lambda i, j, kk: (j, kk), b_lead) if tb else lead((tk, tn), lambda i, j, kk: (kk, j), b_lead)
    ca, cb = (0 if ta else 1), (1 if tb else 0)

    def body(a_ref, b_ref, o_ref, *scratch):
        part = _dot(a_ref[...].astype(BF16), b_ref[...].astype(BF16), ca, cb)
        if nk == 1:
            o_ref[...] = part.astype(o_ref.dtype)
            return
        acc_ref, = scratch
        kk = pl.program_id(2)

        @pl.when(kk == 0)
        def _():
            acc_ref[...] = part

        @pl.when(kk > 0)
        def _():
            acc_ref[...] += part

        @pl.when(kk == nk - 1)
        def _():
            o_ref[...] = acc_ref[...].astype(o_ref.dtype)

    return pl.pallas_call(
        body, name=name, grid=(m // tm, n // tn, nk),
        in_specs=[a_spec, b_spec],
        out_specs=pl.BlockSpec((tm, tn), lambda i, j, kk: (i, j)),
        out_shape=jax.ShapeDtypeStruct((m, n), out_dtype),
        scratch_shapes=[] if nk == 1 else [pltpu.VMEM((tm, tn), F32)],
        compiler_params=_params(("parallel", "parallel", "arbitrary")),
    )(a, b)


def _ew(fn, ins, outs, *, rows, tr, name, n0=None, aliases=None):
    assert rows % tr == 0, (name, rows, tr)
    nt = rows // tr

    def grp(i):
        return 0 if n0 is None else jnp.where(i < n0, 0, 1)

    in_specs, args = [], []
    for spec in ins:
        if spec[0] == 't':
            arr, cb, w = spec[1], spec[2], spec[3]
            l = spec[4] if len(spec) > 4 else None
            if l is None:
                in_specs.append(pl.BlockSpec((tr, w), functools.partial(lambda i, cb: (i, cb), cb=cb)))
            else:
                in_specs.append(pl.BlockSpec((None, tr, w), functools.partial(lambda i, cb, l: (l, i, cb), cb=cb, l=l)))
            args.append(arr)
        else:
            arr = spec[1]
            g = arr.shape[0]
            if g == 1:
                in_specs.append(pl.BlockSpec((None, 1, arr.shape[2]), lambda i: (0, 0, 0)))
            else:
                in_specs.append(pl.BlockSpec((None, 1, arr.shape[2]), lambda i: (grp(i), 0, 0)))
            args.append(arr)
    out_specs, out_shapes, is_red = [], [], []
    for spec in outs:
        if spec[0] == 't':
            w, dt = spec[1], spec[2]
            if len(spec) > 3:
                l, nl = spec[3], spec[4]
                out_specs.append(pl.BlockSpec((None, tr, w), functools.partial(lambda i, l: (l, i, 0), l=l)))
                out_shapes.append(jax.ShapeDtypeStruct((nl, rows, w), dt))
            else:
                out_specs.append(pl.BlockSpec((tr, w), lambda i: (i, 0)))
                out_shapes.append(jax.ShapeDtypeStruct((rows, w), dt))
            is_red.append(False)
        else:
            w, g = spec[1], spec[2]
            if g == 1:
                out_specs.append(pl.BlockSpec((None, 1, w), lambda i: (0, 0, 0)))
            else:
                out_specs.append(pl.BlockSpec((None, 1, w), lambda i: (grp(i), 0, 0)))
            out_shapes.append(jax.ShapeDtypeStruct((g, 1, w), F32))
            is_red.append(True)
    n_in = len(ins)
    n_alias = 0 if aliases is None else len(aliases)

    def body(*refs):
        in_refs = refs[:n_in]
        out_refs = refs[n_in + n_alias:]
        res = fn(*[r[...] for r in in_refs])
        if not isinstance(res, (tuple, list)):
            res = (res,)
        i = pl.program_id(0)
        first = (i == 0) if n0 is None else ((i == 0) | (i == n0))
        for o_ref, val, red in zip(out_refs, res, is_red):
            if not red:
                o_ref[...] = val.astype(o_ref.dtype)
            else:
                @pl.when(first)
                def _(o_ref=o_ref, val=val):
                    o_ref[...] = val

                @pl.when(jnp.logical_not(first))
                def _(o_ref=o_ref, val=val):
                    o_ref[...] += val

    io_alias = {}
    if aliases is not None:
        for a_idx, (arr, o_idx) in enumerate(aliases):
            in_specs.append(ANY)
            args.append(arr)
            io_alias[n_in + a_idx] = o_idx
    has_red = any(is_red)
    return pl.pallas_call(
        body, name=name, grid=(nt,), in_specs=in_specs, out_specs=out_specs, out_shape=out_shapes,
        input_output_aliases=io_alias,
        compiler_params=_params(("arbitrary",) if has_red else ("parallel",)),
    )(*args)


def _half_tiles(pr, pw):
    tr, tc = _tile(pr, 256, 16), _tile(pw, 2048)
    return tr, tc, (pr // tr, pw // tc)


def _half_spec(tr, tc, ax, grid, lead=None):
    pos = (lambda i, j, sel: (sel[0] * grid[0] + i, j)) if ax == 1 else (lambda i, j, sel: (i, sel[0] * grid[1] + j))
    if lead is None:
        return pl.BlockSpec((tr, tc), pos)
    return pl.BlockSpec((None, tr, tc), lambda i, j, sel: (lead,) + pos(i, j, sel))


def _sum_pair(g, theirs, ax, ci, *, name):
    pr, pw = theirs.shape
    tr, tc, grid = _half_tiles(pr, pw)

    def body(sel, a_ref, b_ref, o_ref):
        o_ref[...] = (a_ref[...].astype(F32) + b_ref[...].astype(F32)).astype(o_ref.dtype)

    tile = pl.BlockSpec((tr, tc), lambda i, j, sel: (i, j))
    return pl.pallas_call(
        body, name=name,
        grid_spec=pltpu.PrefetchScalarGridSpec(
            num_scalar_prefetch=1, grid=grid, in_specs=[_half_spec(tr, tc, ax, grid), tile], out_specs=tile),
        out_shape=jax.ShapeDtypeStruct((pr, pw), BF16),
        compiler_params=_params(("parallel", "parallel")),
    )(jnp.reshape(ci, (1,)).astype(jnp.int32), g, theirs)


def _sum_chips_into(own, recv, ax, ci, layer, n_layers, prev, *, name):
    pr, pw = own.shape
    tr, tc, grid = _half_tiles(pr, pw)
    full_shape = (n_layers, 2 * pr, pw) if ax == 1 else (n_layers, pr, 2 * pw)

    def body(sel, a_ref, r_ref, *rest):
        acc = a_ref[...].astype(F32)
        for k in range(N_CHIPS - 1):
            acc = acc + r_ref[k].astype(F32)
        rest[-1][...] = acc

    in_specs = [pl.BlockSpec((tr, tc), lambda i, j, sel: (i, j)),
                pl.BlockSpec((N_CHIPS - 1, tr, tc), lambda i, j, sel: (0, i, j))]
    args = [jnp.reshape(ci, (1,)).astype(jnp.int32), own, recv]
    if prev is not None:
        in_specs.append(ANY)
        args.append(prev)
    return pl.pallas_call(
        body, name=name,
        grid_spec=pltpu.PrefetchScalarGridSpec(
            num_scalar_prefetch=1, grid=grid, in_specs=in_specs, out_specs=_half_spec(tr, tc, ax, grid, lead=layer)),
        out_shape=jax.ShapeDtypeStruct(full_shape, F32),
        input_output_aliases={} if prev is None else {3: 0},
        compiler_params=_params(("parallel", "parallel")),
    )(*args)


def _rsum(v):
    return jnp.sum(v, axis=0, keepdims=True)


def _silu_parts(z):
    sg = _sigmoid(z)
    return z * sg, sg * (1.0 + z * (1.0 - sg))


def _na_bias_table(rpb, rows, *, name):
    kh, kw = NA_WIN_ROWS, NA_WIN_COLS
    assert rows >= kh
    heads = rpb.shape[0]
    e1, e2 = _na_onehots()
    rpb16 = jnp.pad(rpb, ((0, 0), (0, 16 - rpb.shape[1]), (0, LANES - rpb.shape[2])))

    def body(r_ref, e1_ref, e2_ref, o_ref):
        e1b = e1_ref[...].astype(BF16)
        y = sum(_dot(e1b, part, 0, 0) for part in _split3(r_ref[...]))
        e2b = e2_ref[...].astype(BF16)
        o_ref[...] = sum(_dot(part, e2b, 1, 1) for part in _split3(y))

    z = pl.pallas_call(
        body, name=name, grid=(heads,),
        in_specs=[pl.BlockSpec((None, 16, LANES), lambda h: (h, 0, 0)),
                  pl.BlockSpec(e1.shape, lambda h: (0, 0)), pl.BlockSpec(e2.shape, lambda h: (0, 0))],
        out_specs=pl.BlockSpec((None, kh * kh, GRID_W * GRID_W), lambda h: (h, 0, 0)),
        out_shape=jax.ShapeDtypeStruct((heads, kh * kh, GRID_W * GRID_W), F32),
        compiler_params=_params(("parallel",)),
    )(rpb16, e1, e2)
    return z


def _na_bias_layout(z):
    heads = z.shape[0]
    kh, kw = NA_WIN_ROWS, NA_WIN_COLS
    cidx = np.arange(GRID_W)
    c0 = np.clip(cidx - kw // 2, 0, GRID_W - kw)
    col_in = (cidx[None, :] >= c0[:, None]) & (cidx[None, :] < c0[:, None] + kw)
    bias = z.reshape(heads, kh, kh, GRID_W, GRID_W).transpose(0, 1, 3, 2, 4)
    bias = jnp.where(col_in[None, None, :, None, :], bias, MASK_VALUE)
    return bias.reshape(heads, kh, GRID_W, kh * GRID_W)


def _na_onehots():
    kh, kw = NA_WIN_ROWS, NA_WIN_COLS
    cidx = np.arange(GRID_W)
    dc = cidx[None, :] - cidx[:, None] + (kw - 1)
    e2 = np.zeros((GRID_W * GRID_W, LANES), np.float32)
    ok = (dc >= 0) & (dc <= 2 * kw - 2)
    cq, ck = np.nonzero(ok)
    e2[cq * GRID_W + ck, dc[cq, ck]] = 1.0
    dr = np.arange(kh)[None, :] - np.arange(kh)[:, None] + (kh - 1)
    e1 = np.zeros((16, kh * kh), np.float32)
    dl, kr = np.nonzero(np.ones_like(dr))
    e1[dr[dl, kr], dl * kh + kr] = 1.0
    return jnp.asarray(e1), jnp.asarray(e2)


def _na_fwd(u, bias, *, s_len, heads, name):
    t_len = u.shape[0]
    rows = s_len // GRID_W
    nloc = NA_WIN_ROWS * GRID_W
    scale = NA_HEAD_DIM ** -0.5
    hd = NA_HEAD_DIM

    def body(q_ref, k_ref, v_ref, b_ref, o_ref):
        kc = k_ref[s_len:t_len, :]
        vc = v_ref[s_len:t_len, :]

        def group(g, carry):
            rs = [g * NA_GROUP + i for i in range(NA_GROUP)]
            r0s = [jnp.clip(r - NA_WIN_ROWS // 2, 0, rows - NA_WIN_ROWS) for r in rs]
            gs_ = pl.multiple_of(g * (NA_GROUP * GRID_W), NA_GROUP * GRID_W)
            kss = [pl.multiple_of(r0 * GRID_W, GRID_W) for r0 in r0s]
            q_all = q_ref[pl.ds(gs_, NA_GROUP * GRID_W), :]
            s_ctx = _dot(q_all, kc, 1, 1) * scale
            s_loc = [_dot(q_all[i * GRID_W:(i + 1) * GRID_W], k_ref[pl.ds(kss[i], nloc), :], 1, 1) * scale + b_ref[rs[i] - r0s[i]]
                     for i in range(NA_GROUP)]
            p_loc, p_ctx, inv = [], [], []
            for i in range(NA_GROUP):
                sc = s_ctx[i * GRID_W:(i + 1) * GRID_W]
                m = jnp.maximum(jnp.max(s_loc[i], axis=-1, keepdims=True), jnp.max(sc, axis=-1, keepdims=True))
                pl_, pc_ = jnp.exp(s_loc[i] - m), jnp.exp(sc - m)
                inv.append(1.0 / (jnp.sum(pl_, axis=-1, keepdims=True) + jnp.sum(pc_, axis=-1, keepdims=True)))
                p_loc.append(pl_.astype(BF16))
                p_ctx.append(pc_.astype(BF16))
            o_ctx = _dot(jnp.concatenate(p_ctx, axis=0), vc, 1, 0)
            o_loc = [_dot(p_loc[i], v_ref[pl.ds(kss[i], nloc), :], 1, 0) for i in range(NA_GROUP)]
            out = jnp.concatenate([(o_loc[i] + o_ctx[i * GRID_W:(i + 1) * GRID_W]) * inv[i] for i in range(NA_GROUP)], axis=0)
            o_ref[pl.ds(gs_, NA_GROUP * GRID_W), :] = out.astype(o_ref.dtype)
            return carry

        lax.fori_loop(0, rows // NA_GROUP, group, 0)
        qc = q_ref[s_len:t_len, :]
        s = _dot(qc, kc, 1, 1) * scale
        p = jnp.exp(s - jnp.max(s, axis=-1, keepdims=True))
        o = _dot(p.astype(BF16), vc, 1, 0) / jnp.sum(p, axis=-1, keepdims=True)
        o_ref[s_len:t_len, :] = o.astype(o_ref.dtype)

    col = lambda off: pl.BlockSpec((t_len, hd), functools.partial(lambda h, off: (0, off + h), off=off))
    return pl.pallas_call(
        body, name=name, grid=(heads,),
        in_specs=[col(0), col(heads), col(2 * heads),
                  pl.BlockSpec((None, NA_WIN_ROWS, GRID_W, nloc), lambda h: (h, 0, 0, 0))],
        out_specs=pl.BlockSpec((t_len, hd), lambda h: (0, h)),
        out_shape=jax.ShapeDtypeStruct((t_len, heads * hd), BF16),
        compiler_params=_params(("parallel",)),
    )(u, u, u, bias)


def _na_bwd(u, bias, o, do, *, s_len, heads, name):
    t_len = u.shape[0]
    rows = s_len // GRID_W
    nloc = NA_WIN_ROWS * GRID_W
    scale = NA_HEAD_DIM ** -0.5
    hd = NA_HEAD_DIM

    def body(q_ref, k_ref, v_ref, b_ref, o_ref, do_ref, dq_ref, dk_ref, dv_ref, db_ref, dk_acc, dv_acc):
        kc = k_ref[s_len:t_len, :]
        vc = v_ref[s_len:t_len, :]
        dk_acc[...] = jnp.zeros_like(dk_acc)
        dv_acc[...] = jnp.zeros_like(dv_acc)
        db_ref[...] = jnp.zeros_like(db_ref)

        def group(g, carry):
            n_g, rw = NA_GROUP, GRID_W
            rs = [g * n_g + i for i in range(n_g)]
            r0s = [jnp.clip(r - NA_WIN_ROWS // 2, 0, rows - NA_WIN_ROWS) for r in rs]
            dls = [r - r0 for r, r0 in zip(rs, r0s)]
            gs_ = pl.ds(pl.multiple_of(g * (n_g * rw), n_g * rw), n_g * rw)
            kss = [pl.ds(pl.multiple_of(r0 * rw, rw), nloc) for r0 in r0s]
            row_of = lambda a, i: a[i * rw:(i + 1) * rw]
            q_all, do_all = q_ref[gs_, :], do_ref[gs_, :]
            dlt_all = jnp.sum(do_all.astype(F32) * o_ref[gs_, :].astype(F32), axis=-1, keepdims=True)
            s_ctx = _dot(q_all, kc, 1, 1) * scale
            dp_ctx = _dot(do_all, vc, 1, 1)
            s_loc = [_dot(row_of(q_all, i), k_ref[kss[i], :], 1, 1) * scale + b_ref[dls[i]] for i in range(n_g)]
            dp_loc = [_dot(row_of(do_all, i), v_ref[kss[i], :], 1, 1) for i in range(n_g)]
            p_loc_b, ds_loc_b, p_ctx_b, ds_ctx_b = [], [], [], []
            for i in range(n_g):
                sc, dlt = row_of(s_ctx, i), row_of(dlt_all, i)
                m = jnp.maximum(jnp.max(s_loc[i], axis=-1, keepdims=True), jnp.max(sc, axis=-1, keepdims=True))
                pl_, pc_ = jnp.exp(s_loc[i] - m), jnp.exp(sc - m)
                inv = 1.0 / (jnp.sum(pl_, axis=-1, keepdims=True) + jnp.sum(pc_, axis=-1, keepdims=True))
                pl_, pc_ = pl_ * inv, pc_ * inv
                ds_l = pl_ * (dp_loc[i] - dlt)
                db_ref[dls[i]] += ds_l
                p_loc_b.append(pl_.astype(BF16))
                ds_loc_b.append(ds_l.astype(BF16))
                p_ctx_b.append(pc_.astype(BF16))
                ds_ctx_b.append((pc_ * (row_of(dp_ctx, i) - dlt)).astype(BF16))
            p_ctx_all, ds_ctx_all = jnp.concatenate(p_ctx_b, axis=0), jnp.concatenate(ds_ctx_b, axis=0)
            dq_ctx = _dot(ds_ctx_all, kc, 1, 0)
            dq_loc = [_dot(ds_loc_b[i], k_ref[kss[i], :], 1, 0) for i in range(n_g)]
            dk_loc = [_dot(ds_loc_b[i], row_of(q_all, i), 0, 0) for i in range(n_g)]
            dv_loc = [_dot(p_loc_b[i], row_of(do_all, i), 0, 0) for i in range(n_g)]
            dk_ctx = _dot(ds_ctx_all, q_all, 0, 0)
            dv_ctx = _dot(p_ctx_all, do_all, 0, 0)
            dq_ref[gs_, :] = ((jnp.concatenate(dq_loc, axis=0) + dq_ctx) * scale).astype(dq_ref.dtype)
            for i in range(n_g):
                dk_acc[kss[i], :] += dk_loc[i] * scale
                dv_acc[kss[i], :] += dv_loc[i]
            dk_acc[s_len:t_len, :] += dk_ctx * scale
            dv_acc[s_len:t_len, :] += dv_ctx
            return carry

        lax.fori_loop(0, rows // NA_GROUP, group, 0)
        qc = q_ref[s_len:t_len, :]
        dout = do_ref[s_len:t_len, :]
        out = o_ref[s_len:t_len, :]
        s = _dot(qc, kc, 1, 1) * scale
        p = jnp.exp(s - jnp.max(s, axis=-1, keepdims=True))
        p = p / jnp.sum(p, axis=-1, keepdims=True)
        dlt = jnp.sum(dout.astype(F32) * out.astype(F32), axis=-1, keepdims=True)
        ds = (p * (_dot(dout, vc, 1, 1) - dlt)).astype(BF16)
        dq_ref[s_len:t_len, :] = (_dot(ds, kc, 1, 0) * scale).astype(dq_ref.dtype)
        dk_acc[s_len:t_len, :] += _dot(ds, qc, 0, 0) * scale
        dv_acc[s_len:t_len, :] += _dot(p.astype(BF16), dout, 0, 0)
        dk_ref[...] = dk_acc[...].astype(dk_ref.dtype)
        dv_ref[...] = dv_acc[...].astype(dv_ref.dtype)

    col = lambda off: pl.BlockSpec((t_len, hd), functools.partial(lambda h, off: (0, off + h), off=off))
    tbl = pl.BlockSpec((None, NA_WIN_ROWS, GRID_W, nloc), lambda h: (h, 0, 0, 0))
    tok = jax.ShapeDtypeStruct((t_len, heads * hd), BF16)
    return pl.pallas_call(
        body, name=name, grid=(heads,),
        in_specs=[col(0), col(heads), col(2 * heads), tbl, col(0), col(0)],
        out_specs=[col(0), col(0), col(0), tbl],
        out_shape=[tok, tok, tok, jax.ShapeDtypeStruct(bias.shape, F32)],
        scratch_shapes=[pltpu.VMEM((t_len, hd), F32), pltpu.VMEM((t_len, hd), F32)],
        compiler_params=_params(("parallel",)),
    )(u, u, u, bias, o, do)


def _split3(x):
    hi = x.astype(BF16)
    r1 = x - hi.astype(F32)
    mid = r1.astype(BF16)
    lo = (r1 - mid.astype(F32)).astype(BF16)
    return hi, mid, lo


def _rpb_grad(dbias, *, name):
    heads = dbias.shape[0]
    kh = NA_WIN_ROWS
    e1, e2 = _na_onehots()
    x = dbias.reshape(heads, kh, GRID_W, kh, GRID_W).transpose(0, 1, 3, 2, 4).reshape(heads, kh * kh, GRID_W * GRID_W)

    def body(x_ref, e1_ref, e2_ref, o_ref):
        e2b = e2_ref[...].astype(BF16)
        y = sum(_dot(part, e2b, 1, 0) for part in _split3(x_ref[...]))
        e1b = e1_ref[...].astype(BF16)
        o_ref[...] = sum(_dot(e1b, part, 1, 0) for part in _split3(y))

    out = pl.pallas_call(
        body, name=name, grid=(heads,),
        in_specs=[pl.BlockSpec((None, kh * kh, GRID_W * GRID_W), lambda h: (h, 0, 0)),
                  pl.BlockSpec(e1.shape, lambda h: (0, 0)), pl.BlockSpec(e2.shape, lambda h: (0, 0))],
        out_specs=pl.BlockSpec((None, 16, LANES), lambda h: (h, 0, 0)),
        out_shape=jax.ShapeDtypeStruct((heads, 16, LANES), F32),
        compiler_params=_params(("parallel",)),
    )(x, e1, e2)
    return out[:, :2 * kh - 1, :2 * NA_WIN_COLS - 1]


def _rope_tables(s_len, l_len):
    nf = RET_KEY_DIM // 4
    t = np.arange(s_len)
    row = (t // GRID_W).astype(np.float32)
    colp = (t % GRID_W).astype(np.float32)
    inv_freq = jnp.asarray(ROPE_BASE, F32) ** (-jnp.arange(nf, dtype=F32) / nf)
    ang = jnp.concatenate([jnp.asarray(row)[:, None] * inv_freq, jnp.asarray(colp)[:, None] * inv_freq], axis=-1)
    cos, sin = jnp.cos(ang), jnp.sin(ang)
    c2 = jnp.concatenate([cos, cos], axis=-1)
    s2 = jnp.concatenate([-sin, sin], axis=-1)
    c2 = jnp.concatenate([c2, jnp.ones((l_len, RET_KEY_DIM), F32)], axis=0)
    s2 = jnp.concatenate([s2, jnp.zeros((l_len, RET_KEY_DIM), F32)], axis=0)
    return c2, s2


def _rope(x, c2, s2):
    return x * c2 + pltpu.roll(x, RET_KEY_DIM // 2, 1) * s2


def _rope_t(d, c2, s2):
    return d * c2 + pltpu.roll(d * s2, RET_KEY_DIM // 2, 1)


def _ret_decays(lg, direction):
    cs = RET_CHUNK
    i_col = lax.broadcasted_iota(jnp.int32, (cs, 1), 0)
    p_col = jnp.where(direction == 0, i_col, cs - 1 - i_col).astype(F32)
    pi = lax.broadcasted_iota(jnp.int32, (cs, cs), 0)
    pj = lax.broadcasted_iota(jnp.int32, (cs, cs), 1)
    diff = jnp.where(direction == 0, pi - pj, pj - pi).astype(F32)
    dm = jnp.where(diff >= 0, jnp.exp(jnp.maximum(diff, 0.0) * lg), 0.0)
    qdec = jnp.exp((p_col + 1.0) * lg)
    kdec = jnp.exp((cs - 1.0 - p_col) * lg)
    cd = jnp.exp(jnp.full((1, 1), cs, F32) * lg)
    return p_col, dm, qdec, kdec, cd


def _ret_chunk_index(t, direction, n_chunks, lat_chunks):
    return jnp.where(direction == 0, lax.rem(t + lat_chunks, n_chunks), n_chunks - 1 - t)


def _ret_fwd(u, c2, s2, lg, *, s_len, heads, q_off, name):
    t_len = u.shape[0]
    cs, dk, dv = RET_CHUNK, RET_KEY_DIM, RET_VAL_DIM
    n_chunks, lat_chunks = t_len // cs, s_len // cs
    k_scale = dk ** -0.5
    qb, kb, vb = q_off // dk, q_off // dk + heads, (q_off + 2 * heads * dk) // dv

    def body(lg_ref, q_ref, k_ref, v_ref, c_ref, s_ref, o_ref, st_ref, qd_s, kv_s):
        h, d = pl.program_id(0), pl.program_id(1)
        _, dm, qdec, kdec, cd = _ret_decays(lg_ref[d, h], d)
        n_g = max(g for g in RET_GROUPS if n_chunks % g == 0)
        rows_of = lambda c: pl.ds(pl.multiple_of(c * cs, cs), cs)

        def local(gi, carry):
            rws = [rows_of(gi * n_g + j) for j in range(n_g)]
            qcs = [_rope(q_ref[r, :].astype(F32), c_ref[r, :], s_ref[r, :]) for r in rws]
            kcs = [_rope(k_ref[r, :].astype(F32), c_ref[r, :], s_ref[r, :]) * k_scale for r in rws]
            vcs = [v_ref[r, :] for r in rws]
            a_raw = [_dot(qcs[j].astype(BF16), kcs[j].astype(BF16), 1, 1) for j in range(n_g)]
            kv = [_dot((kcs[j] * kdec).astype(BF16), vcs[j], 0, 0) for j in range(n_g)]
            inner = [_dot((a_raw[j] * dm).astype(BF16), vcs[j], 1, 0) for j in range(n_g)]
            for j in range(n_g):
                qd_s[rws[j], :] = (qcs[j] * qdec).astype(BF16)
                kv_s[gi * n_g + j] = kv[j]

            @pl.when(d == 0)
            def _():
                for j in range(n_g):
                    o_ref[rws[j], :] = inner[j]

            @pl.when(d == 1)
            def _():
                for j in range(n_g):
                    o_ref[rws[j], :] += inner[j]

            return carry

        lax.fori_loop(0, n_chunks // n_g, local, 0)

        def scan(t, st):
            st_ref[t] = st
            return st * cd + kv_s[_ret_chunk_index(t, d, n_chunks, lat_chunks)]

        lax.fori_loop(0, n_chunks, scan, jnp.zeros((dk, dv), F32))

        def cross(gi, carry):
            ts = [gi * n_g + j for j in range(n_g)]
            rws = [rows_of(_ret_chunk_index(t, d, n_chunks, lat_chunks)) for t in ts]
            outs = [_dot(qd_s[rws[j], :], st_ref[ts[j]].astype(BF16), 1, 0) for j in range(n_g)]
            for j in range(n_g):
                o_ref[rws[j], :] += outs[j]
            return carry

        lax.fori_loop(0, n_chunks // n_g, cross, 0)

    return pl.pallas_call(
        body, name=name, grid=(heads, 2),
        in_specs=[pl.BlockSpec(memory_space=pltpu.SMEM),
                  pl.BlockSpec((t_len, dk), lambda h, d: (0, qb + h)),
                  pl.BlockSpec((t_len, dk), lambda h, d: (0, kb + h)),
                  pl.BlockSpec((t_len, dv), lambda h, d: (0, vb + h)),
                  pl.BlockSpec((t_len, dk), lambda h, d: (0, 0)),
                  pl.BlockSpec((t_len, dk), lambda h, d: (0, 0))],
        out_specs=[pl.BlockSpec((t_len, dv), lambda h, d: (0, h)),
                   pl.BlockSpec((None, None, n_chunks, dk, dv), lambda h, d: (h, d, 0, 0, 0))],
        out_shape=[jax.ShapeDtypeStruct((t_len, heads * dv), F32),
                   jax.ShapeDtypeStruct((heads, 2, n_chunks, dk, dv), F32)],
        scratch_shapes=[pltpu.VMEM((t_len, dk), BF16), pltpu.VMEM((n_chunks, dk, dv), F32)],
        compiler_params=_params(("parallel", "arbitrary")),
    )(lg, u, u, u, c2, s2)


def _ret_bwd(u, c2, s2, lg, states, do, *, s_len, heads, q_off, name):
    t_len = u.shape[0]
    cs, dk, dv = RET_CHUNK, RET_KEY_DIM, RET_VAL_DIM
    n_chunks, lat_chunks = t_len // cs, s_len // cs
    k_scale = dk ** -0.5
    qb, kb, vb = q_off // dk, q_off // dk + heads, (q_off + 2 * heads * dk) // dv

    def body(lg_ref, q_ref, k_ref, v_ref, c_ref, s_ref, st_ref, do_ref, dq_ref, dk_ref, dv_ref, dlg_ref, acc, qdo_s, dst_s):
        h, d = pl.program_id(0), pl.program_id(1)
        p_col, dm, qdec, kdec, cd = _ret_decays(lg_ref[d, h], d)
        acc[...] = jnp.zeros_like(acc)
        n_g = max(g for g in RET_GROUPS[:2] if n_chunks % g == 0)
        rows_of = lambda c: pl.ds(pl.multiple_of(c * cs, cs), cs)
        chunk_of = lambda t: _ret_chunk_index(t, d, n_chunks, lat_chunks)

        def local(gi, carry):
            rws = [rows_of(gi * n_g + j) for j in range(n_g)]
            qds = [(_rope(q_ref[r, :].astype(F32), c_ref[r, :], s_ref[r, :]) * qdec).astype(BF16) for r in rws]
            prods = [_dot(qds[j], do_ref[rws[j], :].astype(BF16), 0, 0) for j in range(n_g)]
            for j in range(n_g):
                qdo_s[gi * n_g + j] = prods[j]
            return carry

        lax.fori_loop(0, n_chunks // n_g, local, 0)

        def scan(i, dst):
            t = n_chunks - 1 - i
            dst_s[t] = dst
            return dst * cd + qdo_s[chunk_of(t)]

        lax.fori_loop(0, n_chunks, scan, jnp.zeros((dk, dv), F32))

        def grads(gi, carry):
            ts = [gi * n_g + j for j in range(n_g)]
            rws = [rows_of(chunk_of(t)) for t in ts]
            ccs, sss = [c_ref[r, :] for r in rws], [s_ref[r, :] for r in rws]
            qcs = [_rope(q_ref[r, :].astype(F32), cc, ss) for r, cc, ss in zip(rws, ccs, sss)]
            kcs = [_rope(k_ref[r, :].astype(F32), cc, ss) * k_scale for r, cc, ss in zip(rws, ccs, sss)]
            vcs = [v_ref[r, :] for r in rws]
            docs = [do_ref[r, :].astype(BF16) for r in rws]
            sts = [st_ref[t] for t in ts]
            dsts = [dst_s[t] for t in ts]
            q16 = [x.astype(BF16) for x in qcs]
            k16 = [x.astype(BF16) for x in kcs]
            dst16 = [x.astype(BF16) for x in dsts]
            rng = range(n_g)
            a_raw = [_dot(q16[j], k16[j], 1, 1) for j in rng]
            da_raw = [_dot(docs[j], vcs[j], 1, 1) for j in rng]
            dq_c = [_dot(docs[j], sts[j].astype(BF16), 1, 1) * qdec for j in rng]
            dv_s = [_dot((kcs[j] * kdec).astype(BF16), dst16[j], 1, 0) for j in rng]
            dk_s = [_dot(vcs[j], dst16[j], 1, 1) * kdec for j in rng]
            a16 = [(a_raw[j] * dm).astype(BF16) for j in rng]
            dam = [(da_raw[j] * dm).astype(BF16) for j in rng]
            dq_i = [_dot(dam[j], k16[j], 1, 0) for j in rng]
            dk_i = [_dot(dam[j], q16[j], 0, 0) for j in rng]
            dv_i = [_dot(a16[j], docs[j], 0, 0) for j in rng]
            for j in rng:
                g = (jnp.sum(qcs[j] * (p_col * dq_i[j] + (p_col + 1.0) * dq_c[j]), axis=-1, keepdims=True)
                     + jnp.sum(kcs[j] * ((cs - 1.0 - p_col) * dk_s[j] - p_col * dk_i[j]), axis=-1, keepdims=True))
                g = (jnp.sum(g, axis=0, keepdims=True)
                     + cs * cd * jnp.sum(jnp.sum(dsts[j] * sts[j], axis=-1, keepdims=True), axis=0, keepdims=True))
                acc[...] += jnp.broadcast_to(g, acc.shape)
            dqs = [_rope_t(dq_i[j] + dq_c[j], ccs[j], sss[j]) for j in rng]
            dks = [_rope_t((dk_i[j] + dk_s[j]) * k_scale, ccs[j], sss[j]) for j in rng]
            dvs = [dv_i[j] + dv_s[j] for j in rng]

            @pl.when(d == 0)
            def _():
                for j in rng:
                    dq_ref[rws[j], :] = dqs[j].astype(dq_ref.dtype)
                    dk_ref[rws[j], :] = dks[j].astype(dk_ref.dtype)
                    dv_ref[rws[j], :] = dvs[j].astype(dv_ref.dtype)

            @pl.when(d == 1)
            def _():
                for j in rng:
                    dq_ref[rws[j], :] = (dq_ref[rws[j], :].astype(F32) + dqs[j]).astype(dq_ref.dtype)
                    dk_ref[rws[j], :] = (dk_ref[rws[j], :].astype(F32) + dks[j]).astype(dk_ref.dtype)
                    dv_ref[rws[j], :] = (dv_ref[rws[j], :].astype(F32) + dvs[j]).astype(dv_ref.dtype)

            return carry

        lax.fori_loop(0, n_chunks // n_g, grads, 0)
        dlg_ref[...] = acc[...]

    return pl.pallas_call(
        body, name=name, grid=(heads, 2),
        in_specs=[pl.BlockSpec(memory_space=pltpu.SMEM),
                  pl.BlockSpec((t_len, dk), lambda h, d: (0, qb + h)),
                  pl.BlockSpec((t_len, dk), lambda h, d: (0, kb + h)),
                  pl.BlockSpec((t_len, dv), lambda h, d: (0, vb + h)),
                  pl.BlockSpec((t_len, dk), lambda h, d: (0, 0)),
                  pl.BlockSpec((t_len, dk), lambda h, d: (0, 0)),
                  pl.BlockSpec((None, None, n_chunks, dk, dv), lambda h, d: (h, d, 0, 0, 0)),
                  pl.BlockSpec((t_len, dv), lambda h, d: (0, h))],
        out_specs=[pl.BlockSpec((t_len, dk), lambda h, d: (0, h)),
                   pl.BlockSpec((t_len, dk), lambda h, d: (0, h)),
                   pl.BlockSpec((t_len, dv), lambda h, d: (0, h)),
                   pl.BlockSpec((None, None, 8, LANES), lambda h, d: (h, d, 0, 0))],
        out_shape=[jax.ShapeDtypeStruct((t_len, heads * dk), BF16),
                   jax.ShapeDtypeStruct((t_len, heads * dk), BF16),
                   jax.ShapeDtypeStruct((t_len, heads * dv), BF16),
                   jax.ShapeDtypeStruct((heads, 2, 8, LANES), F32)],
        scratch_shapes=[pltpu.VMEM((8, LANES), F32), pltpu.VMEM((n_chunks, dk, dv), F32), pltpu.VMEM((n_chunks, dk, dv), F32)],
        compiler_params=_params(("parallel", "arbitrary")),
    )(lg, u, u, u, c2, s2, states, do)


def _mesh_pos():
    return lax.axis_index("x"), lax.axis_index("y"), lax.axis_index("c")


def _all_gather_small(buf, *, name):
    r = buf.shape[0]

    def body(x_ref, o_ref, send_sems, recv_sems, local_sem):
        x, y, c = _mesh_pos()
        me = 4 * x + 2 * y + c
        mine = pltpu.make_async_copy(x_ref, o_ref.at[me], local_sem)
        mine.start()
        copies = []
        for k in range(1, N_DEV):
            px, py, pc = x ^ ((k >> 2) & 1), y ^ ((k >> 1) & 1), c ^ (k & 1)
            cp = pltpu.make_async_remote_copy(
                src_ref=x_ref, dst_ref=o_ref.at[me], send_sem=send_sems.at[k - 1], recv_sem=recv_sems.at[k - 1],
                device_id=(px, py, pc), device_id_type=MESH)
            cp.start()
            copies.append((cp, 4 * px + 2 * py + pc))
        for k, (cp, peer) in enumerate(copies):
            pltpu.make_async_remote_copy(
                src_ref=x_ref, dst_ref=o_ref.at[peer], send_sem=send_sems.at[k], recv_sem=recv_sems.at[k],
                device_id=(x, y, c), device_id_type=MESH).wait_recv()
        for cp, _ in copies:
            cp.wait_send()
        mine.wait()

    return pl.pallas_call(
        body, name=name,
        in_specs=[pl.BlockSpec(memory_space=pltpu.VMEM)],
        out_specs=pl.BlockSpec(memory_space=pltpu.VMEM),
        out_shape=jax.ShapeDtypeStruct((N_DEV, r, LANES), F32),
        scratch_shapes=[pltpu.SemaphoreType.DMA((N_DEV - 1,)), pltpu.SemaphoreType.DMA((N_DEV - 1,)),
                        pltpu.SemaphoreType.DMA],
        compiler_params=pltpu.CompilerParams(vmem_limit_bytes=VMEM_LIMIT),
    )(buf)


def _cut(ref, shard_axis, *, chip=None, half=None, lead=None):
    shape = ref.shape[1:] if lead is not None else ref.shape
    idx = [slice(None), slice(None)]
    if chip is not None:
        w = shape[shard_axis] // N_CHIPS
        idx[shard_axis] = pl.ds(pl.multiple_of(chip * w, w), w)
    if half is not None:
        hw = shape[1 - shard_axis] // 2
        idx[1 - shard_axis] = pl.ds(pl.multiple_of(half * hw, hw), hw)
    if lead is not None:
        idx = [lead] + idx
    return ref.at[tuple(idx)]


def _wait_recv(ref, send_sem, recv_sem):
    pltpu.make_async_remote_copy(src_ref=ref, dst_ref=ref, send_sem=send_sem, recv_sem=recv_sem,
                                 device_id=_mesh_pos(), device_id_type=MESH).wait_recv()


def _gather_plan(axes):
    def plan(srcs, lands, send_sems, recv_sems):
        x, y, c = _mesh_pos()
        chip = 2 * x + y
        copies = []
        for i, ax in enumerate(axes):
            for k in range(1, N_CHIPS):
                px, py = x ^ (k >> 1), y ^ (k & 1)
                mine = _cut(lands[i], ax, chip=chip, half=c)
                j = i * (N_CHIPS - 1) + k - 1
                sems = dict(send_sem=send_sems.at[j], recv_sem=recv_sems.at[j], device_id=(px, py, c), device_id_type=MESH)
                send = pltpu.make_async_remote_copy(src_ref=mine, dst_ref=mine, **sems)
                recv = pltpu.make_async_remote_copy(src_ref=mine, dst_ref=_cut(lands[i], ax, chip=2 * px + py, half=c), **sems)
                copies.append((send, recv))
        return copies
    return plan


def _gather_near_plan(axes):
    def plan(srcs, lands, send_sems, recv_sems):
        x, y, c = _mesh_pos()
        copies = []
        for i, ax in enumerate(axes):
            mine = _cut(lands[i], ax, chip=2 * x + y, half=c)
            for k, (px, py) in enumerate(((1 - x, y), (x, 1 - y))):
                sems = dict(send_sem=send_sems.at[2 * i + k], recv_sem=recv_sems.at[2 * i + k], device_id=(px, py, c), device_id_type=MESH)
                send = pltpu.make_async_remote_copy(src_ref=mine, dst_ref=mine, **sems)
                recv = pltpu.make_async_remote_copy(src_ref=mine, dst_ref=_cut(lands[i], ax, chip=2 * px + py, half=c), **sems)
                copies.append((send, recv))
        return copies
    return plan


def _gather_far_plan(axes):
    def plan(srcs, lands, send_sems, recv_sems):
        x, y, c = _mesh_pos()
        from_chip = 2 * (x ^ (1 - c)) + (y ^ c)
        to = (x ^ c, y ^ (1 - c), c)
        diag = 2 * (1 - x) + (1 - y)
        copies = []
        for i, ax in enumerate(axes):
            passed = _cut(lands[i], ax, chip=from_chip, half=c)
            sems = dict(send_sem=send_sems.at[i], recv_sem=recv_sems.at[i], device_id=to, device_id_type=MESH)
            send = pltpu.make_async_remote_copy(src_ref=passed, dst_ref=passed, **sems)
            recv = pltpu.make_async_remote_copy(src_ref=passed, dst_ref=_cut(lands[i], ax, chip=diag, half=c), **sems)
            copies.append((send, recv))
        return copies
    return plan


def _forward_plan(axes, ks=(1, 2, 3)):
    def plan(srcs, lands, send_sems, recv_sems):
        x, y, c = _mesh_pos()
        copies = []
        for i, ax in enumerate(axes):
            for n, k in enumerate(ks):
                peer_chip = 2 * (x ^ (k >> 1)) + (y ^ (k & 1))
                j = i * len(ks) + n
                sems = dict(send_sem=send_sems.at[j], recv_sem=recv_sems.at[j], device_id=(x, y, 1 - c), device_id_type=MESH)
                landed = _cut(lands[i], ax, chip=peer_chip, half=c)
                send = pltpu.make_async_remote_copy(src_ref=landed, dst_ref=landed, **sems)
                recv = pltpu.make_async_remote_copy(src_ref=landed, dst_ref=_cut(lands[i], ax, chip=peer_chip, half=1 - c), **sems)
                copies.append((send, recv))
        return copies
    return plan


def _pair_plan(axes):
    def plan(srcs, lands, send_sems, recv_sems):
        x, y, c = _mesh_pos()
        copies = []
        for i, ax in enumerate(axes):
            cp = pltpu.make_async_remote_copy(
                src_ref=_cut(srcs[i], ax, half=1 - c), dst_ref=lands[i], send_sem=send_sems.at[i], recv_sem=recv_sems.at[i],
                device_id=(x, y, 1 - c), device_id_type=MESH)
            copies.append((cp, cp))
        return copies
    return plan


def _scatter_plan(axes):
    def plan(srcs, lands, send_sems, recv_sems):
        x, y, c = _mesh_pos()
        copies = []
        for i, ax in enumerate(axes):
            for k in range(1, N_CHIPS):
                px, py = x ^ (k >> 1), y ^ (k & 1)
                j = i * (N_CHIPS - 1) + k - 1
                cp = pltpu.make_async_remote_copy(
                    src_ref=_cut(srcs[i], ax, chip=2 * px + py), dst_ref=lands[i].at[k - 1],
                    send_sem=send_sems.at[j], recv_sem=recv_sems.at[j], device_id=(px, py, c), device_id_type=MESH)
                copies.append((cp, cp))
        return copies
    return plan


HBM = pl.BlockSpec(memory_space=pltpu.HBM)
SEM = pl.BlockSpec(memory_space=pltpu.SEMAPHORE)
EFFECT = pltpu.SideEffectType.DATAFLOW_SIDE_EFFECTING


def _in_hbm(arrays):
    return [pltpu.with_memory_space_constraint(a, pltpu.HBM) for a in arrays]


def _split_start(srcs, lands, plan, n_copies, *, name):
    bufs = list(srcs) + list(lands)
    ns, nb = len(srcs), len(bufs)

    def body(*refs):
        send_sems, recv_sems, token = refs[nb], refs[nb + 1], refs[-1]
        for send, _ in plan(refs[:ns], refs[ns:nb], send_sems, recv_sems):
            send.start()
        token[...] = jnp.zeros_like(token)

    sems = pltpu.SemaphoreType.DMA((n_copies,))
    res = pl.pallas_call(
        body, name=name, in_specs=[HBM] * nb,
        out_specs=[SEM, SEM] + [HBM] * nb + [pl.BlockSpec(memory_space=pltpu.VMEM)],
        out_shape=[sems, sems] + [pltpu.HBM(a.shape, a.dtype) for a in bufs] + [jax.ShapeDtypeStruct((8, LANES), F32)],
        input_output_aliases={j: 2 + j for j in range(nb)},
        compiler_params=pltpu.CompilerParams(has_side_effects=EFFECT),
    )(*_in_hbm(bufs))
    return res[0], res[1], res[2:2 + ns], res[2 + ns:2 + nb], res[-1]


def _split_wait(started, after, plan, *, name, with_srcs=False):
    send_sems, recv_sems, srcs, lands, _ = started
    bufs = list(srcs) + list(lands)
    ns, nb = len(srcs), len(bufs)

    def body(*refs):
        for send, recv in plan(refs[:ns], refs[ns:nb], refs[nb], refs[nb + 1]):
            send.wait_send()
            recv.wait_recv()

    res = pl.pallas_call(
        body, name=name, in_specs=[HBM] * nb + [SEM, SEM, ANY], out_specs=[HBM] * nb,
        out_shape=[pltpu.HBM(a.shape, a.dtype) for a in bufs],
        input_output_aliases={j: j for j in range(nb)},
        compiler_params=pltpu.CompilerParams(has_side_effects=EFFECT),
    )(*bufs, send_sems, recv_sems, after)
    return (res[:ns], res[ns:]) if with_srcs else res[ns:]


def _cast_into_full(w3, layer, ax, chip, *, after=None, name):
    _, r, wd = w3.shape
    tr = _rows_per_tile(r, wd, 4 << 20)
    nt = r // tr
    full_shape = (r, wd * N_CHIPS) if ax == 1 else (r * N_CHIPS, wd)
    out_map = (lambda i, ch: (i, ch[0])) if ax == 1 else (lambda i, ch: (ch[0] * nt + i, 0))
    zero = jnp.zeros((1, wd), F32) + (0.0 if after is None else after)

    def body(chip_ref, w_ref, z_ref, o_ref):
        o_ref[...] = (w_ref[...] + z_ref[...]).astype(o_ref.dtype)

    return pl.pallas_call(
        body, name=name,
        grid_spec=pltpu.PrefetchScalarGridSpec(
            num_scalar_prefetch=1, grid=(nt,),
            in_specs=[pl.BlockSpec((None, tr, wd), lambda i, ch: (layer, i, 0)), pl.BlockSpec((1, wd), lambda i, ch: (0, 0))],
            out_specs=pl.BlockSpec((tr, wd), out_map)),
        out_shape=jax.ShapeDtypeStruct(full_shape, BF16),
        compiler_params=_params(("parallel",)),
    )(jnp.reshape(chip, (1,)).astype(jnp.int32), w3, zero)


def _forward_halves(fulls, axes, *, name, ks=(1, 2, 3)):
    n = len(fulls)

    def body(*refs):
        bufs = refs[:n]
        send_sems, recv_sems = refs[2 * n:]
        x, y, c = _mesh_pos()
        sends = []
        for i in range(n):
            for k in ks:
                landed = _cut(bufs[i], axes[i], chip=2 * (x ^ (k >> 1)) + (y ^ (k & 1)), half=c)
                cp = pltpu.make_async_remote_copy(
                    src_ref=landed, dst_ref=landed, send_sem=send_sems.at[i, k - 1], recv_sem=recv_sems.at[i, k - 1],
                    device_id=(x, y, 1 - c), device_id_type=MESH)
                cp.start()
                sends.append(cp)
        for i in range(n):
            for k in ks:
                other = _cut(bufs[i], axes[i], chip=2 * (x ^ (k >> 1)) + (y ^ (k & 1)), half=1 - c)
                _wait_recv(other, send_sems.at[i, k - 1], recv_sems.at[i, k - 1])
        for cp in sends:
            cp.wait_send()

    pairs = pltpu.SemaphoreType.DMA((n, N_CHIPS - 1))
    return pl.pallas_call(
        body, name=name, in_specs=[ANY] * n, out_specs=[ANY] * n,
        out_shape=[jax.ShapeDtypeStruct(a.shape, a.dtype) for a in fulls],
        input_output_aliases={j: j for j in range(n)},
        scratch_shapes=[pairs, pairs],
    )(*fulls)


def _share_halves_in_place(bufs, axes, *, name, lead=None):
    n = len(bufs)

    def body(*refs):
        ins = refs[:n]
        send_sems, recv_sems = refs[2 * n:]
        x, y, c = _mesh_pos()
        sends = []
        for i in range(n):
            mine = _cut(ins[i], axes[i], half=c, lead=lead)
            cp = pltpu.make_async_remote_copy(
                src_ref=mine, dst_ref=mine, send_sem=send_sems.at[i], recv_sem=recv_sems.at[i],
                device_id=(x, y, 1 - c), device_id_type=MESH)
            cp.start()
            sends.append(cp)
        for i in range(n):
            _wait_recv(_cut(ins[i], axes[i], half=1 - c, lead=lead), send_sems.at[i], recv_sems.at[i])
        for cp in sends:
            cp.wait_send()

    sems = pltpu.SemaphoreType.DMA((n,))
    return pl.pallas_call(
        body, name=name, in_specs=[ANY] * n, out_specs=[ANY] * n,
        out_shape=[jax.ShapeDtypeStruct(b.shape, b.dtype) for b in bufs],
        input_output_aliases={j: j for j in range(n)}, scratch_shapes=[sems, sems],
    )(*bufs)


def _adamw_math(w, g, m, v):
    m = ADAM_B1 * m + (1.0 - ADAM_B1) * g
    v = ADAM_B2 * v + (1.0 - ADAM_B2) * (g * g)
    m_hat = m / (1.0 - ADAM_B1 ** ADAM_STEP)
    v_hat = v / (1.0 - ADAM_B2 ** ADAM_STEP)
    delta = -ADAM_LR * (m_hat / (jnp.sqrt(v_hat) + ADAM_EPS) + ADAM_WD * w)
    return delta, m, v


def _adamw_layer(w3, m3, v3, p, q, layer, prev, *, name):
    nl, rows, width = w3.shape
    tr = _rows_per_tile(rows, width)

    def fn(*t):
        if q is None:
            w, m, v, g = t
        else:
            w, m, v, g, g2 = t
            g = g + g2
        delta, m, v = _adamw_math(w, g, m, v)
        return g, delta, m, v

    if p.ndim == 3:
        res = _ew(lambda w, m, v, g: _adamw_math(w, g, m, v),
                  [('t', w3, 0, width, layer), ('t', m3, 0, width, layer), ('t', v3, 0, width, layer), ('t', p, 0, width, layer)],
                  [('t', width, F32, layer, nl)] * 3, rows=rows, tr=tr, name=name,
                  aliases=None if prev is None else [(prev[1 + i], i) for i in range(3)])
        return [p] + list(res)
    ins = [('t', w3, 0, width, layer), ('t', m3, 0, width, layer), ('t', v3, 0, width, layer), ('t', p, 0, width)]
    if q is not None:
        ins.append(('t', q, 0, width))
    outs = [('t', width, F32, layer, nl)] * 4
    aliases = None if prev is None else [(prev[i], i) for i in range(4)]
    return _ew(fn, ins, outs, rows=rows, tr=tr, name=name, aliases=aliases)


def _pack_rows(vec):
    n = vec.shape[0]
    r = -(-n // (8 * LANES)) * 8
    return jnp.pad(vec, (0, r * LANES - n)).reshape(r, LANES)


def kernel(x, c, ctx, c_ctx, ada_w, ada_b, norm_g, w_in, na_rpb, ret_decay_logit, w_proj_na, w_proj_ret, w_out, final_g, loss_target, m_c_ctx, m_ada_w, m_ada_b, m_norm_g, m_w_in, m_na_rpb, m_ret_decay_logit, m_w_proj_na, m_w_proj_ret, m_w_out, m_final_g, v_c_ctx, v_ada_w, v_ada_b, v_norm_g, v_w_in, v_na_rpb, v_ret_decay_logit, v_w_proj_na, v_w_proj_ret, v_w_out, v_final_g):
    depth = w_in.shape[0]
    s_len, d_model = x.shape[1], x.shape[2]
    l_len = ctx.shape[1]
    t_len = s_len + l_len
    na_heads = na_rpb.shape[1]
    ret_heads = ret_decay_logit.shape[2]
    w_na = na_heads * NA_HEAD_DIM
    w_qk = ret_heads * RET_KEY_DIM
    w_v = ret_heads * RET_VAL_DIM
    in_cols = w_in.shape[2] * N_CHIPS
    assert in_cols == 4 * w_na + 2 * w_qk + 2 * w_v + 2 * d_model
    assert x.shape[0] == 1 and s_len % (NA_WIN_ROWS * GRID_W) == 0 and l_len % RET_CHUNK == 0
    off = np.cumsum([0, w_na, w_na, w_na, w_na, w_qk, w_qk, w_v, w_v, d_model, d_model])
    o_naz, o_retq, o_retz, o_gna, o_gret = int(off[3]), int(off[4]), int(off[7]), int(off[8]), int(off[9])
    rows = s_len // GRID_W
    tr = _tile(l_len, 256, 8)
    n0 = s_len // tr
    mod_cols = 3 * d_model
    mod_shard = ada_w.shape[2]

    xi, yi, ci = _mesh_pos()
    me = 4 * xi + 2 * yi + ci
    chip = 2 * xi + yi

    big_axes = [1, 1, 0, 0]
    n_big = len(big_axes) * (N_CHIPS - 1)
    gather_plan, scatter_plan = _gather_plan(big_axes), _scatter_plan(big_axes)

    c_silu = c[0] * _sigmoid(c[0])
    cc_silu = c_ctx * _sigmoid(c_ctx)
    c_all = _all_gather_small(_pack_rows(c_silu), name="gather_c")[:, :d_model // LANES].reshape(N_DEV, d_model)
    a_rows = jnp.concatenate([c_all, cc_silu[None], jnp.zeros((16 - N_DEV - 1, d_model), F32)], axis=0)
    mod_part = jnp.stack([_mm(a_rows, ada_w, b_lead=l, out_dtype=F32, name="ada_fwd_%d" % l) for l in range(depth)])
    mod_all = _all_gather_small(_pack_rows(mod_part.reshape(-1)), name="gather_mod")
    n_mod = depth * 16 * mod_shard
    mod_all = mod_all.reshape(N_DEV, -1)[:, :n_mod].reshape(N_CHIPS, 2, depth, 16, mod_shard)[:, 0]
    mod_all = jnp.transpose(mod_all, (1, 2, 0, 3)).reshape(depth, 16, mod_cols) + ada_b[:, None, :]

    big_named = list(zip((w_in, w_proj_na, w_proj_ret, w_out), big_axes, ("w_in", "w_proj_na", "w_proj_ret", "w_out")))
    w_in0 = _cast_into_full(w_in, 0, big_axes[0], chip, name="cast_w_in_0")
    mod_all, w_in0 = lax.optimization_barrier((mod_all, w_in0))
    plan_near, plan_far, plan_rest = _gather_near_plan(big_axes[:1]), _gather_far_plan(big_axes[:1]), _gather_plan(big_axes[1:])
    near_all, far_all, forward_all = _gather_near_plan(big_axes), _gather_far_plan(big_axes), _forward_plan(big_axes)
    first_gather = _split_start([], [w_in0], plan_near, 2, name="gather_start_0_in")
    start_token = first_gather[4][0, 0]
    fulls = [[None if (l == 0 and tag == "w_in") else _cast_into_full(w, l, ax, chip, after=start_token, name="cast_%s_%d" % (tag, l))
              for w, ax, tag in big_named] for l in range(depth)]
    mod_lat = lax.dynamic_index_in_dim(mod_all, me, axis=1, keepdims=False)
    mod_ctx = mod_all[:, N_DEV]
    biases = [_na_bias_layout(_na_bias_table(na_rpb[l], s_len // GRID_W, name="na_bias_%d" % l)) for l in range(depth)]
    biases, fulls = lax.optimization_barrier((biases, fulls))
    landed_near = _split_wait(first_gather, biases[-1], plan_near, name="gather_wait_0_in")
    passing = _split_start([], landed_near, plan_far, 1, name="gather_pass_0_in")
    forward_near = _forward_plan(big_axes[:1], ks=(1, 2))
    near_swap = _split_start([], passing[3], forward_near, 2, name="gather_forward_near_0_in")
    front_token = passing[4][0, 0] + near_swap[4][0, 0]

    c2, s2 = _rope_tables(s_len, l_len)
    log_gamma = jax.nn.log_sigmoid(ret_decay_logit)
    x_all = jnp.concatenate([x[0], ctx[0]], axis=0)

    def grp(lat_vec, ctx_vec):
        return jnp.stack([lat_vec, ctx_vec])[:, None, :]

    saved, full_w = [], []
    for l in range(depth):
        shift, scale, gate = [grp(mod_lat[l, i * d_model:(i + 1) * d_model], mod_ctx[l, i * d_model:(i + 1) * d_model])
                              for i in range(3)]
        gs = norm_g[l][None, None, :] * (1.0 + scale) + front_token

        def modnorm(xt, gs_t, sh_t):
            r = lax.rsqrt(jnp.mean(xt * xt, axis=-1, keepdims=True) + NORM_EPS)
            return xt * r * gs_t + sh_t

        h, = _ew(modnorm, [('t', x_all, 0, d_model), ('g', gs), ('g', shift)], [('t', d_model, BF16)],
                 rows=t_len, tr=tr, n0=n0, name="modnorm_%d" % l)
        bias = biases[l]
        if l == 0:
            h, bias = lax.optimization_barrier((h, bias))
            landed_far = _split_wait((passing[0], passing[1], [], near_swap[3], None), h, plan_far, name="gather_wait_0_in_far")
            landed_in = _split_wait((near_swap[0], near_swap[1], [], landed_far, None), h, forward_near,
                                    name="gather_forward_near_wait_0_in")
            landed_in, rest0, later = lax.optimization_barrier((landed_in, fulls[0][1:], fulls[1:]))
            rest_gather = _split_start([], rest0, plan_rest, n_big - (N_CHIPS - 1), name="gather_start_0_rest")
            later_gathers = [_split_start([], later[j], near_all, 2 * len(big_axes), name="gather_start_%d" % (j + 1))
                             for j in range(depth - 1)]
            win_f, = _forward_halves(landed_in, big_axes[:1], name="gather_forward_0_in", ks=(3,))
            win_f, tokens = lax.optimization_barrier((win_f, [rest_gather[4]] + [g[4] for g in later_gathers]))
            gate = gate + sum(t[0, 0] for t in tokens)
        else:
            h, bias = lax.optimization_barrier((h, bias))
            win_f, wpn_f, wpr_f, wout_f = _split_wait(next_forward, h, forward_all, name="gather_forward_wait_%d" % l)
        u = _mm(h, win_f, tm=1152, tn=1024, name="in_proj_%d" % l)
        o_na = _na_fwd(u, bias, s_len=s_len, heads=na_heads, name="na_fwd_%d" % l)
        o_ret, states = _ret_fwd(u, c2, s2, log_gamma[l], s_len=s_len, heads=ret_heads, q_off=o_retq, name="ret_fwd_%d" % l)

        def act(o1, z1, o2, z2):
            a1 = o1.astype(F32) * _silu_parts(z1.astype(F32))[0]
            sz = _silu_parts(z2.astype(F32))[0]
            outs = []
            for hh in range(ret_heads):
                sl = slice(hh * RET_VAL_DIM, (hh + 1) * RET_VAL_DIM)
                oh = o2[:, sl]
                r = lax.rsqrt(jnp.mean(oh * oh, axis=-1, keepdims=True) + NORM_EPS)
                outs.append(oh * r * sz[:, sl])
            return a1, jnp.concatenate(outs, axis=-1)

        a_na, a_ret = _ew(act, [('t', o_na, 0, w_na), ('t', u, o_naz // w_na, w_na), ('t', o_ret, 0, w_v), ('t', u, o_retz // w_v, w_v)],
                          [('t', w_na, BF16), ('t', w_v, BF16)], rows=t_len, tr=tr, name="act_%d" % l)
        if l == 0:
            landed_rest = _split_wait(rest_gather, a_na, plan_rest, name="gather_wait_0_rest")
            later_passes = [_split_start([], _split_wait(later_gathers[j], a_na, near_all, name="gather_near_%d" % (j + 1)),
                                         far_all, len(big_axes), name="gather_pass_%d" % (j + 1)) for j in range(depth - 1)]
            landed_rest, tokens = lax.optimization_barrier((landed_rest, [g[4] for g in later_passes]))
            gate = gate + sum(t[0, 0] for t in tokens)
            wpn_f, wpr_f, wout_f = _forward_halves(landed_rest, big_axes[1:], name="gather_forward_0_rest")
        full_w.append((win_f, wpn_f, wpr_f, wout_f))
        y_na = _mm(a_na, wpn_f, name="proj_na_%d" % l)
        y_ret = _mm(a_ret, wpr_f, name="proj_ret_%d" % l)

        def merge(y1, y2, g1, g2):
            return _sigmoid(g1.astype(F32)) * y1.astype(F32) + _sigmoid(g2.astype(F32)) * y2.astype(F32)

        merged, = _ew(merge, [('t', y_na, 0, d_model), ('t', y_ret, 0, d_model), ('t', u, o_gna // d_model, d_model), ('t', u, o_gret // d_model, d_model)],
                      [('t', d_model, BF16)], rows=t_len, tr=tr, name="merge_%d" % l)
        out = _mm(merged, wout_f, out_dtype=F32, name="out_proj_%d" % l)
        if l + 1 < depth:
            landed = _split_wait(later_passes[l], out, far_all, name="gather_wait_%d" % (l + 1))
            next_forward = _split_start([], landed, forward_all, n_big, name="gather_forward_%d" % (l + 1))
            gate = gate + next_forward[4][0, 0]
        x_new, = _ew(lambda xt, ot, gt: xt + gt * ot, [('t', x_all, 0, d_model), ('t', out, 0, d_model), ('g', gate)],
                     [('t', d_model, F32)], rows=t_len, tr=tr, n0=n0, name="resid_%d" % l)
        saved.append(dict(x=x_all, h=h, u=u, bias=bias, o_na=o_na, o_ret=o_ret, states=states, a_na=a_na, a_ret=a_ret,
                          y_na=y_na, y_ret=y_ret, merged=merged, out=out, gate=gate, gs=gs, scale=scale))
        x_all = x_new

    def final(xt, tt, gt):
        r = lax.rsqrt(jnp.mean(xt * xt, axis=-1, keepdims=True) + NORM_EPS)
        xh = xt * r
        e = xh * gt - tt
        dy = e * (1.0 / d_model)
        dyg = dy * gt
        dx = r * (dyg - xh * jnp.mean(dyg * xh, axis=-1, keepdims=True))
        return dx, _rsum(dy * xh), _rsum(e * e)

    dx_lat, d_final_g, loss_cols = _ew(final, [('t', x_all, 0, d_model), ('t', loss_target[0], 0, d_model), ('g', final_g[None, None, :])],
                                       [('t', d_model, F32), ('r', d_model, 1), ('r', d_model, 1)], rows=s_len, tr=tr, name="final")
    loss_part = (0.5 / d_model) * jnp.sum(loss_cols)
    dx_all = jnp.concatenate([dx_lat, jnp.zeros((l_len, d_model), F32)], axis=0)

    big_w = [(w_in, m_w_in, v_w_in), (w_proj_na, m_w_proj_na, v_w_proj_na), (w_proj_ret, m_w_proj_ret, v_w_proj_ret), (w_out, m_w_out, v_w_out)]
    big_res = [None] * 4
    scatters = {}
    back_token = jnp.zeros((), F32)

    pairs = {}

    def start_pair(key, grads, axes):
        plan = _pair_plan(axes)
        lands = []
        for g, ax in zip(grads, axes):
            shp = list(g.shape)
            shp[1 - ax] //= 2
            lands.append(lax.empty(tuple(shp), BF16))
        pairs[key] = (_split_start(grads, lands, plan, len(axes), name="pair_start_%s" % key), axes, plan)
        return pairs[key][0][4]

    def start_scatter(key, after):
        started, axes, pair_plan = pairs[key]
        grads, theirs = _split_wait(started, after, pair_plan, name="pair_wait_%s" % key, with_srcs=True)
        plan = _scatter_plan(axes)
        pair = [_sum_pair(g, t, ax, ci, name="sum_pair_%s_%d" % (key, i)) for i, (g, t, ax) in enumerate(zip(grads, theirs, axes))]
        own = [lax.dynamic_slice_in_dim(s, chip * (s.shape[ax] // N_CHIPS), s.shape[ax] // N_CHIPS, axis=ax) for s, ax in zip(pair, axes)]
        lands = [lax.empty((N_CHIPS - 1,) + o.shape, BF16) for o in own]
        started = _split_start(pair, lands, plan, len(axes) * (N_CHIPS - 1), name="scatter_start_%s" % key)
        scatters[key] = (started, own, axes, plan)
        return started[4]

    def finish_scatter(key, after, l, idx, big_res):
        started, own, axes, plan = scatters[key]
        recv = _split_wait(started, after, plan, name="scatter_wait_%s" % key)
        bufs = [_sum_chips_into(own[n], rbuf, axes[n], ci, l, depth, None if big_res[i] is None else big_res[i][0],
                                name="sum_chips_%s_%d" % (key, n)) for n, (i, rbuf) in enumerate(zip(idx, recv))]
        return _share_halves_in_place(bufs, axes, name="share_halves_%s" % key, lead=l)

    def adamw_big(l, idx, key, after, big_res):
        for i, g3 in zip(idx, finish_scatter(key, after, l, idx, big_res)):
            w3, m3, v3 = big_w[i]
            big_res[i] = _adamw_layer(w3, m3, v3, g3, None, l, big_res[i], name="adamw_big_%d_%d" % (i, l))
        return big_res

    small = dict(dmod_lat=[None] * depth, dmod_ctx=[None] * depth, dnorm_g=[None] * depth, drpb=[None] * depth, ddecay=[None] * depth)
    for l in reversed(range(depth)):
        sv = saved[l]
        win_f, wpn_f, wpr_f, wout_f = full_w[l]

        def resid_bwd(dxt, ot, gt):
            return gt * dxt, _rsum(dxt * ot)

        dout, dgate = _ew(resid_bwd, [('t', dx_all, 0, d_model), ('t', sv['out'], 0, d_model), ('g', sv['gate'] + back_token)],
                          [('t', d_model, BF16), ('r', d_model, 2)], rows=t_len, tr=tr, n0=n0, name="resid_bwd_%d" % l)
        dmerged = _mm(dout, wout_f, tb=True, name="out_proj_dx_%d" % l)
        g_wout = _mm(sv['merged'], dout, ta=True, tm=1024, tk=t_len, name="out_proj_dw_%d" % l)

        def merge_bwd(dm, y1, y2, g1, g2):
            dm = dm.astype(F32)
            s1, s2_ = _sigmoid(g1.astype(F32)), _sigmoid(g2.astype(F32))
            return dm * s1, dm * s2_, dm * y1.astype(F32) * s1 * (1.0 - s1), dm * y2.astype(F32) * s2_ * (1.0 - s2_)

        u = sv['u']
        dy_na, dy_ret, dg_na, dg_ret = _ew(
            merge_bwd, [('t', dmerged, 0, d_model), ('t', sv['y_na'], 0, d_model), ('t', sv['y_ret'], 0, d_model),
                        ('t', u, o_gna // d_model, d_model), ('t', u, o_gret // d_model, d_model)],
            [('t', d_model, BF16)] * 4, rows=t_len, tr=tr, name="merge_bwd_%d" % l)
        da_na = _mm(dy_na, wpn_f, tb=True, name="proj_na_dx_%d" % l)
        g_wpn = _mm(sv['a_na'], dy_na, ta=True, tm=1024, tk=t_len, name="proj_na_dw_%d" % l)
        da_ret = _mm(dy_ret, wpr_f, tb=True, name="proj_ret_dx_%d" % l)
        g_wpr = _mm(sv['a_ret'], dy_ret, ta=True, tm=1024, tk=t_len, name="proj_ret_dw_%d" % l)
        lg_l = log_gamma[l]
        if l == 0:
            pair_token = start_pair("0_rest", [g_wpn, g_wpr, g_wout], big_axes[1:])
            da_na, pair_token = lax.optimization_barrier((da_na, pair_token))

        def act_bwd(da1, o1, z1, da2, o2, z2):
            da1, da2 = da1.astype(F32), da2.astype(F32)
            si1, ds1 = _silu_parts(z1.astype(F32))
            si2, ds2 = _silu_parts(z2.astype(F32))
            do1 = da1 * si1
            dz1 = da1 * o1.astype(F32) * ds1
            dn = da2 * si2
            do2, dz2 = [], []
            for hh in range(ret_heads):
                sl = slice(hh * RET_VAL_DIM, (hh + 1) * RET_VAL_DIM)
                oh = o2[:, sl]
                r = lax.rsqrt(jnp.mean(oh * oh, axis=-1, keepdims=True) + NORM_EPS)
                nh = oh * r
                dz2.append(da2[:, sl] * nh * ds2[:, sl])
                do2.append(r * (dn[:, sl] - nh * jnp.mean(dn[:, sl] * nh, axis=-1, keepdims=True)))
            return do1, dz1, jnp.concatenate(do2, axis=-1), jnp.concatenate(dz2, axis=-1)

        do_na, dz_na, do_ret, dz_ret = _ew(
            act_bwd, [('t', da_na, 0, w_na), ('t', sv['o_na'], 0, w_na), ('t', u, o_naz // w_na, w_na),
                      ('t', da_ret, 0, w_v), ('t', sv['o_ret'], 0, w_v), ('t', u, o_retz // w_v, w_v)],
            [('t', w_na, BF16), ('t', w_na, BF16), ('t', w_v, BF16), ('t', w_v, BF16)], rows=t_len, tr=tr, name="act_bwd_%d" % l)
        dq_na, dk_na, dv_na, dbias = _na_bwd(u, sv['bias'], sv['o_na'], do_na, s_len=s_len, heads=na_heads, name="na_bwd_%d" % l)
        small['drpb'][l] = _rpb_grad(dbias, name="rpb_grad_%d" % l)
        if l == 0:
            lg_l = lg_l + start_scatter("0_rest", dq_na)[0, 0] + pair_token[0, 0]
        dq_r, dk_r, dv_r, dlg = _ret_bwd(u, c2, s2, lg_l, sv['states'], do_ret, s_len=s_len, heads=ret_heads,
                                         q_off=o_retq, name="ret_bwd_%d" % l)
        small['ddecay'][l] = jnp.transpose(dlg[:, :, 0, 0]) * _sigmoid(-ret_decay_logit[l])
        du_parts = [dq_na, dk_na, dv_na, dz_na, dq_r, dk_r, dv_r, dz_ret, dg_na, dg_ret]
        du, = _ew(lambda *t: jnp.concatenate(t, axis=-1), [('t', p, 0, p.shape[1]) for p in du_parts], [('t', in_cols, BF16)],
                  rows=t_len, tr=tr, name="du_concat_%d" % l)
        g_win = _mm(sv['h'], du, ta=True, tm=1024, tn=1024, tk=t_len, name="in_proj_dw_%d" % l)
        if l > 0:
            du, pair_token = lax.optimization_barrier((du, start_pair("%d_all" % l, [g_win, g_wpn, g_wpr, g_wout], big_axes)))
        else:
            du, in_token = lax.optimization_barrier((du, start_pair("0_in", [g_win], big_axes[:1])))
        dh = _mm(du, win_f, tb=True, out_dtype=F32, tm=1152, tn=1024, name="in_proj_dx_%d" % l)

        def modnorm_bwd(xt, dht, dxt, gs_t):
            r = lax.rsqrt(jnp.mean(xt * xt, axis=-1, keepdims=True) + NORM_EPS)
            xh = xt * r
            dhg = dht * gs_t
            dx = r * (dhg - xh * jnp.mean(dhg * xh, axis=-1, keepdims=True)) + dxt
            return dx, _rsum(dht), _rsum(dht * xh)

        dx_all, dshift, dgs = _ew(modnorm_bwd, [('t', sv['x'], 0, d_model), ('t', dh, 0, d_model), ('t', dx_all, 0, d_model), ('g', sv['gs'])],
                                  [('t', d_model, F32), ('r', d_model, 2), ('r', d_model, 2)], rows=t_len, tr=tr, n0=n0, name="modnorm_bwd_%d" % l)
        dscale = dgs * norm_g[l][None, None, :]
        small['dnorm_g'][l] = jnp.sum(dgs * (1.0 + sv['scale']), axis=(0, 1))
        dmod = jnp.concatenate([dshift, dscale, dgate], axis=-1)[:, 0]
        small['dmod_lat'][l], small['dmod_ctx'][l] = dmod[0], dmod[1]

        if l > 0:
            back_token = start_scatter("%d_all" % l, dx_all)[0, 0] + pair_token[0, 0]

    grad_x = dx_all[:s_len][None]

    drpb = jnp.stack(small['drpb']).reshape(-1)
    ddecay = jnp.stack(small['ddecay']).reshape(-1)
    pieces = [jnp.stack(small['dmod_lat']).reshape(-1), jnp.stack(small['dmod_ctx']).reshape(-1),
              jnp.stack(small['dnorm_g']).reshape(-1), d_final_g.reshape(-1), drpb, ddecay, loss_part[None]]
    sizes = [int(p.shape[0]) for p in pieces]
    pads = [-(-s // LANES) * LANES for s in sizes]
    packed = jnp.concatenate([jnp.pad(p, (0, pd - s)) for p, s, pd in zip(pieces, sizes, pads)])
    gathered = _all_gather_small(_pack_rows(packed), name="gather_small_grads")
    r_small = gathered.shape[1]

    def sum8(*t):
        acc = t[0]
        for other in t[1:]:
            acc = acc + other
        return acc

    total, = _ew(sum8, [('t', gathered, 0, LANES, k) for k in range(N_DEV)], [('t', LANES, F32)], rows=r_small, tr=r_small, name="sum_devices")
    total = total.reshape(-1)
    starts = np.cumsum([0] + pads)
    g_mod_lat_sum, g_mod_ctx, g_norm_g, g_final_g, g_rpb, g_decay, loss = [total[starts[i]:starts[i] + sizes[i]] for i in range(len(pieces))]
    loss = loss[0]
    g_ada_b = (g_mod_lat_sum + g_mod_ctx).reshape(depth, mod_cols)
    g_mod_ctx = g_mod_ctx.reshape(depth, mod_cols)
    dmod_lat_all = gathered.reshape(N_DEV, -1)[:, :depth * mod_cols].reshape(N_DEV, depth, mod_cols)

    dcc_part = jnp.zeros((16, d_model), F32)
    ctx_cols = [lax.dynamic_slice_in_dim(g_mod_ctx[l], chip * mod_shard, mod_shard, axis=0) for l in range(depth)]
    for l in reversed(range(depth)):
        c_rows = jnp.concatenate([ctx_cols[l][None], jnp.zeros((15, mod_shard), F32)], axis=0)
        dcc_part = dcc_part + _mm(c_rows, ada_w, tb=True, b_lead=l, out_dtype=F32, name="ada_dc_%d" % l)
    dcc_all = _all_gather_small(_pack_rows(dcc_part[0]), name="gather_dcc")[:, :d_model // LANES].reshape(N_CHIPS, 2, d_model)[:, 0]

    tail_token = start_scatter("0_in", dcc_all) + in_token
    dcc = ((dcc_all[0] + dcc_all[1]) + dcc_all[2]) + dcc_all[3]
    sg = _sigmoid(c_ctx)
    g_c_ctx = dcc * (sg * (1.0 + c_ctx * (1.0 - sg)))
    for l in reversed(range(1, depth)):
        big_res = adamw_big(l, range(4), "%d_all" % l, tail_token, big_res)

    ada_res = None
    for l in reversed(range(depth)):
        lat_cols = lax.dynamic_slice_in_dim(dmod_lat_all[:, l], chip * mod_shard, mod_shard, axis=1)
        d_rows = jnp.concatenate([lat_cols, ctx_cols[l][None], jnp.zeros((16 - N_DEV - 1, mod_shard), F32)], axis=0) + tail_token[0, 0]
        g_ada = _mm(a_rows, d_rows, ta=True, out_dtype=F32, tm=512, name="ada_dw_%d" % l)
        ada_res = _adamw_layer(ada_w, m_ada_w, v_ada_w, g_ada, None, l, ada_res, name="adamw_ada_%d" % l)

    small_w = [(c_ctx, m_c_ctx, v_c_ctx, g_c_ctx), (ada_b, m_ada_b, v_ada_b, g_ada_b),
               (norm_g, m_norm_g, v_norm_g, g_norm_g), (na_rpb, m_na_rpb, v_na_rpb, g_rpb),
               (ret_decay_logit, m_ret_decay_logit, v_ret_decay_logit, g_decay), (final_g, m_final_g, v_final_g, g_final_g)]
    sw_sizes = [int(np.prod(t[0].shape)) for t in small_w]
    sw_pads = [-(-s // LANES) * LANES for s in sw_sizes]

    def pack(j):
        return _pack_rows(jnp.concatenate([jnp.pad(t[j].reshape(-1), (0, pd - s)) for t, s, pd in zip(small_w, sw_sizes, sw_pads)]))

    pw_, pm_, pv_, pg_ = pack(0), pack(1), pack(2), pack(3)
    sw_out = _ew(lambda w, m, v, g: (g,) + _adamw_math(w, g, m, v),
                 [('t', pw_, 0, LANES), ('t', pm_, 0, LANES), ('t', pv_, 0, LANES), ('t', pg_, 0, LANES)],
                 [('t', LANES, F32)] * 4, rows=pw_.shape[0], tr=pw_.shape[0], name="adamw_small")
    sw_starts = np.cumsum([0] + sw_pads)
    sw_out, ada_res, big_res = lax.optimization_barrier((sw_out, ada_res, big_res))
    big_res = adamw_big(0, range(1, 4), "0_rest", sw_out[0], big_res)
    big_res = adamw_big(0, range(1), "0_in", sw_out[1], big_res)

    def unpack(arr, i):
        return arr.reshape(-1)[sw_starts[i]:sw_starts[i] + sw_sizes[i]].reshape(small_w[i][0].shape)

    sm = [[unpack(sw_out[j], i) for i in range(len(small_w))] for j in range(4)]
    def ordered(j):
        return [sm[j][0], ada_res[j], sm[j][1], sm[j][2], big_res[0][j], sm[j][3], sm[j][4],
                big_res[1][j], big_res[2][j], big_res[3][j], sm[j][5]]

    return (loss, grad_x, *ordered(0), *ordered(1), *ordered(2), *ordered(3))
```
